```python
import math
import jax, jax.numpy as jnp
from jax import lax
import numpy as np

D_MODEL = 1024
BATCH = 32
SEQ = 2048
DEPTH = 1

HEAD_DIM = 64
N_HEADS = D_MODEL // HEAD_DIM
N_HEADS_FOX = N_HEADS // 2
N_HEADS_DIL = N_HEADS - N_HEADS_FOX
W_FOX = N_HEADS_FOX * HEAD_DIM
W_DIL = N_HEADS_DIL * HEAD_DIM
DILATION_PAIRS = ((128, 1), (512, 4), (2048, 16))
ROPE_THETA = 500000.0
ROPE_DIM = HEAD_DIM // 4
Q_BLOCK = 128
D_FF = -(-8 * D_MODEL // (3 * 256)) * 256
EPS = 1e-6
NEG = -1e30
IN_SPLITS = (W_FOX, 2 * W_FOX, 3 * W_FOX, 3 * W_FOX + N_HEADS_FOX,
             3 * W_FOX + N_HEADS_FOX + W_DIL, 3 * W_FOX + N_HEADS_FOX + 2 * W_DIL)
IN_COLS = 3 * W_FOX + N_HEADS_FOX + 3 * W_DIL

kernel_name = "hymba_fox_dilated_hybrid"


def rms_norm(x, g):
    xf = x.astype(jnp.float32)
    y = xf * lax.rsqrt(jnp.mean(xf * xf, axis=-1, keepdims=True) + EPS)
    return (y * g.astype(jnp.float32)).astype(x.dtype)


def partial_rope(x, pos):
    half = ROPE_DIM // 2
    inv_freq = jnp.power(jnp.float32(ROPE_THETA),
                         -jnp.arange(half, dtype=jnp.float32) * 2.0 / ROPE_DIM)
    ang = pos.astype(jnp.float32)[:, None] * inv_freq[None, :]
    cos = jnp.cos(ang)[None, :, None, :]
    sin = jnp.sin(ang)[None, :, None, :]
    x1 = x[..., :half]
    x2 = x[..., half:ROPE_DIM]
    return jnp.concatenate([x1 * cos - x2 * sin, x2 * cos + x1 * sin, x[..., ROPE_DIM:]], axis=-1)


def fox_attention(q, k, v, log_f):
    S = q.shape[1]
    c = jnp.transpose(jnp.cumsum(log_f, axis=1), (0, 2, 1))
    scale = HEAD_DIM ** -0.5
    outs = []
    for i in range(S // Q_BLOCK):
        q0, q1 = i * Q_BLOCK, (i + 1) * Q_BLOCK
        s = jnp.einsum('bqhe,bkhe->bhqk', q[:, q0:q1], k[:, :q1]) * scale
        s = s + (c[:, :, q0:q1, None] - c[:, :, None, :q1])
        mask = np.arange(q0, q1)[:, None] >= np.arange(q1)[None, :]
        s = jnp.where(mask[None, None], s, NEG)
        p = jax.nn.softmax(s, axis=-1)
        outs.append(jnp.einsum('bhqk,bkhe->bqhe', p, v[:, :q1]))
    return jnp.concatenate(outs, axis=1)


def dilated_branch(q, k, v, window, dilation):
    B, S, H, D = q.shape
    n = S // dilation
    wk = window // dilation
    pad = min(wk, n)
    bq = math.gcd(Q_BLOCK, n)
    nb = n // bq
    L = bq + pad
    scale = HEAD_DIM ** -0.5
    qs = q.reshape(B, nb, bq, dilation, H, D)
    kp = jnp.pad(k.reshape(B, n, dilation, H, D), ((0, 0), (pad, 0), (0, 0), (0, 0), (0, 0)))
    vp = jnp.pad(v.reshape(B, n, dilation, H, D), ((0, 0), (pad, 0), (0, 0), (0, 0), (0, 0)))
    starts = np.arange(nb) * bq
    idx = starts[:, None] + np.arange(L)[None, :]
    kb = kp[:, idx]
    vb = vp[:, idx]
    s = jnp.einsum('bnirhe,bnjrhe->bnrhij', qs, kb) * scale
    ii = np.arange(bq)[:, None]
    jj = np.arange(L)[None, :]
    dist = ii + pad - jj
    key_pos = starts[:, None, None] + jj[None] - pad
    mask = (dist >= 0)[None] & (dist <= wk)[None] & (key_pos >= 0)
    s = jnp.where(mask[None, :, None, None], s, NEG)
    lse = jax.nn.logsumexp(s, axis=-1)
    p = jnp.exp(s - lse[..., None])
    o = jnp.einsum('bnrhij,bnjrhe->bnirhe', p, vb).reshape(B, S, H, D)
    lse = jnp.transpose(lse, (0, 1, 4, 2, 3)).reshape(B, S, H)
    return o, lse


def dilated_attention(q, k, v):
    outs, lses = [], []
    for window, dilation in DILATION_PAIRS:
        o, l = dilated_branch(q, k, v, window, dilation)
        outs.append(o)
        lses.append(l)
    w = jax.nn.softmax(jnp.stack(lses, axis=0), axis=0)
    return jnp.sum(w[..., None] * jnp.stack(outs, axis=0), axis=0)


def _fwd_setup_inputs(seed: int = 0) -> dict:
    key = jax.random.key(seed)
    ks = jax.random.split(key, 16)
    f32 = jnp.float32

    def gain(k, shape):
        return jnp.ones(shape, f32) + 0.02 * jax.random.normal(k, shape, f32)

    return {
        "x": jax.random.normal(ks[0], (BATCH, SEQ, D_MODEL), f32),
        "g_mix": gain(ks[1], (DEPTH, D_MODEL)),
        "w_in": jax.random.normal(ks[2], (DEPTH, D_MODEL, IN_COLS), f32) * D_MODEL ** -0.5,
        "b_forget": jax.random.uniform(ks[3], (DEPTH, N_HEADS_FOX), f32, minval=1.0, maxval=4.0),
        "g_q_fox": gain(ks[4], (DEPTH, HEAD_DIM)),
        "g_k_fox": gain(ks[5], (DEPTH, HEAD_DIM)),
        "g_q_dil": gain(ks[6], (DEPTH, HEAD_DIM)),
        "g_k_dil": gain(ks[7], (DEPTH, HEAD_DIM)),
        "g_out_fox": gain(ks[8], (DEPTH, W_FOX)),
        "g_out_dil": gain(ks[9], (DEPTH, W_DIL)),
        "w_out": jax.random.normal(ks[10], (DEPTH, D_MODEL, D_MODEL), f32) * D_MODEL ** -0.5,
        "g_ffn": gain(ks[11], (DEPTH, D_MODEL)),
        "w_gate": jax.random.normal(ks[12], (DEPTH, D_MODEL, D_FF), f32) * D_MODEL ** -0.5,
        "w_up": jax.random.normal(ks[13], (DEPTH, D_MODEL, D_FF), f32) * D_MODEL ** -0.5,
        "w_down": jax.random.normal(ks[14], (DEPTH, D_FF, D_MODEL), f32) * D_FF ** -0.5,
    }


def _fwd_reference(x, g_mix, w_in, b_forget, g_q_fox, g_k_fox, g_q_dil, g_k_dil,
              g_out_fox, g_out_dil, w_out, g_ffn, w_gate, w_up, w_down):
    B, S, _ = x.shape
    f32 = jnp.float32
    pos = jnp.arange(S)

    def heads(t, n_h):
        return t.reshape(B, S, n_h, HEAD_DIM).astype(f32)

    for l in range(DEPTH):
        h = rms_norm(x, g_mix[l])
        proj = jnp.einsum('bsd,dc->bsc', h, w_in[l])
        qa, ka, va, fa, qd, kd, vd = jnp.split(proj, IN_SPLITS, axis=-1)

        qa = rms_norm(heads(qa, N_HEADS_FOX), g_q_fox[l])
        ka = rms_norm(heads(ka, N_HEADS_FOX), g_k_fox[l])
        va = heads(va, N_HEADS_FOX)
        log_f = jax.nn.log_sigmoid(fa.astype(f32) + b_forget[l].astype(f32))
        o_fox = fox_attention(qa, ka, va, log_f).reshape(B, S, W_FOX)

        qd = partial_rope(rms_norm(heads(qd, N_HEADS_DIL), g_q_dil[l]), pos)
        kd = partial_rope(rms_norm(heads(kd, N_HEADS_DIL), g_k_dil[l]), pos)
        vd = heads(vd, N_HEADS_DIL)
        o_dil = dilated_attention(qd, kd, vd).reshape(B, S, W_DIL)

        o = jnp.concatenate([rms_norm(o_fox, g_out_fox[l]), rms_norm(o_dil, g_out_dil[l])], axis=-1)
        x = x + jnp.einsum('bsc,cd->bsd', o.astype(x.dtype), w_out[l])

        h = rms_norm(x, g_ffn[l])
        a = jnp.einsum('bsd,df->bsf', h, w_gate[l])
        u = jnp.einsum('bsd,df->bsf', h, w_up[l])
        x = x + jnp.einsum('bsf,fd->bsd', jax.nn.silu(a) * u, w_down[l])
    return x


import jax as _jax
import jax.numpy as _jnp

TWIN_FORMAT = 'train_step'
FWD_PARAMS = ['x', 'g_mix', 'w_in', 'b_forget', 'g_q_fox', 'g_k_fox', 'g_q_dil', 'g_k_dil', 'g_out_fox', 'g_out_dil', 'w_out', 'g_ffn', 'w_gate', 'w_up', 'w_down']
TWIN_WEIGHTS = ['g_mix', 'w_in', 'b_forget', 'g_q_fox', 'g_k_fox', 'g_q_dil', 'g_k_dil', 'g_out_fox', 'g_out_dil', 'w_out', 'g_ffn', 'w_gate', 'w_up', 'w_down']
TWIN_DIFF_INPUT = 'x'
TWIN_INPUTS = ['x', 'g_mix', 'w_in', 'b_forget', 'g_q_fox', 'g_k_fox', 'g_q_dil', 'g_k_dil', 'g_out_fox', 'g_out_dil', 'w_out', 'g_ffn', 'w_gate', 'w_up', 'w_down', 'loss_target', 'm_g_mix', 'm_w_in', 'm_b_forget', 'm_g_q_fox', 'm_g_k_fox', 'm_g_q_dil', 'm_g_k_dil', 'm_g_out_fox', 'm_g_out_dil', 'm_w_out', 'm_g_ffn', 'm_w_gate', 'm_w_up', 'm_w_down', 'v_g_mix', 'v_w_in', 'v_b_forget', 'v_g_q_fox', 'v_g_k_fox', 'v_g_q_dil', 'v_g_k_dil', 'v_g_out_fox', 'v_g_out_dil', 'v_w_out', 'v_g_ffn', 'v_w_gate', 'v_w_up', 'v_w_down']
TWIN_OUTPUTS = ['loss', 'grad_x', 'grad_g_mix', 'grad_w_in', 'grad_b_forget', 'grad_g_q_fox', 'grad_g_k_fox', 'grad_g_q_dil', 'grad_g_k_dil', 'grad_g_out_fox', 'grad_g_out_dil', 'grad_w_out', 'grad_g_ffn', 'grad_w_gate', 'grad_w_up', 'grad_w_down', 'delta_g_mix', 'delta_w_in', 'delta_b_forget', 'delta_g_q_fox', 'delta_g_k_fox', 'delta_g_q_dil', 'delta_g_k_dil', 'delta_g_out_fox', 'delta_g_out_dil', 'delta_w_out', 'delta_g_ffn', 'delta_w_gate', 'delta_w_up', 'delta_w_down', 'new_m_g_mix', 'new_m_w_in', 'new_m_b_forget', 'new_m_g_q_fox', 'new_m_g_k_fox', 'new_m_g_q_dil', 'new_m_g_k_dil', 'new_m_g_out_fox', 'new_m_g_out_dil', 'new_m_w_out', 'new_m_g_ffn', 'new_m_w_gate', 'new_m_w_up', 'new_m_w_down', 'new_v_g_mix', 'new_v_w_in', 'new_v_b_forget', 'new_v_g_q_fox', 'new_v_g_k_fox', 'new_v_g_q_dil', 'new_v_g_k_dil', 'new_v_g_out_fox', 'new_v_g_out_dil', 'new_v_w_out', 'new_v_g_ffn', 'new_v_w_gate', 'new_v_w_up', 'new_v_w_down']
TWIN_LEAF_KINDS = {'loss': 'loss', 'grad_x': 'grad_x', 'grad_g_mix': 'grad_w', 'grad_w_in': 'grad_w', 'grad_b_forget': 'grad_w', 'grad_g_q_fox': 'grad_w', 'grad_g_k_fox': 'grad_w', 'grad_g_q_dil': 'grad_w', 'grad_g_k_dil': 'grad_w', 'grad_g_out_fox': 'grad_w', 'grad_g_out_dil': 'grad_w', 'grad_w_out': 'grad_w', 'grad_g_ffn': 'grad_w', 'grad_w_gate': 'grad_w', 'grad_w_up': 'grad_w', 'grad_w_down': 'grad_w', 'delta_g_mix': 'delta_w', 'delta_w_in': 'delta_w', 'delta_b_forget': 'delta_w', 'delta_g_q_fox': 'delta_w', 'delta_g_k_fox': 'delta_w', 'delta_g_q_dil': 'delta_w', 'delta_g_k_dil': 'delta_w', 'delta_g_out_fox': 'delta_w', 'delta_g_out_dil': 'delta_w', 'delta_w_out': 'delta_w', 'delta_g_ffn': 'delta_w', 'delta_w_gate': 'delta_w', 'delta_w_up': 'delta_w', 'delta_w_down': 'delta_w', 'new_m_g_mix': 'new_m', 'new_m_w_in': 'new_m', 'new_m_b_forget': 'new_m', 'new_m_g_q_fox': 'new_m', 'new_m_g_k_fox': 'new_m', 'new_m_g_q_dil': 'new_m', 'new_m_g_k_dil': 'new_m', 'new_m_g_out_fox': 'new_m', 'new_m_g_out_dil': 'new_m', 'new_m_w_out': 'new_m', 'new_m_g_ffn': 'new_m', 'new_m_w_gate': 'new_m', 'new_m_w_up': 'new_m', 'new_m_w_down': 'new_m', 'new_v_g_mix': 'new_v', 'new_v_w_in': 'new_v', 'new_v_b_forget': 'new_v', 'new_v_g_q_fox': 'new_v', 'new_v_g_k_fox': 'new_v', 'new_v_g_q_dil': 'new_v', 'new_v_g_k_dil': 'new_v', 'new_v_g_out_fox': 'new_v', 'new_v_g_out_dil': 'new_v', 'new_v_w_out': 'new_v', 'new_v_g_ffn': 'new_v', 'new_v_w_gate': 'new_v', 'new_v_w_up': 'new_v', 'new_v_w_down': 'new_v'}


def _forward(args):
    return _fwd_reference(*[args[k] for k in FWD_PARAMS])


def _output_shape():
    out = _jax.eval_shape(lambda: _forward(_fwd_setup_inputs(0)))
    return out.shape, out.dtype

N_MICROBATCH = 1
ADAM_LR = 0.001
ADAM_B1 = 0.9
ADAM_B2 = 0.999
ADAM_EPS = 1e-08
ADAM_WD = 0.01
ADAM_STEP = 10
PER_EXAMPLE_BATCH_AXIS = {'x': 0, 'loss_target': 0}
SHARED_INPUTS = []
_WEIGHT_DTYPES = {'g_mix': _jnp.float32, 'w_in': _jnp.float32, 'b_forget': _jnp.float32, 'g_q_fox': _jnp.float32, 'g_k_fox': _jnp.float32, 'g_q_dil': _jnp.float32, 'g_k_dil': _jnp.float32, 'g_out_fox': _jnp.float32, 'g_out_dil': _jnp.float32, 'w_out': _jnp.float32, 'g_ffn': _jnp.float32, 'w_gate': _jnp.float32, 'w_up': _jnp.float32, 'w_down': _jnp.float32}
MOMENT_SCALE = {'g_mix': 1.296220e+00, 'w_in': 6.947170e-01, 'b_forget': 8.999906e+00, 'g_q_fox': 1.420372e+00, 'g_k_fox': 1.401268e+00, 'g_q_dil': 2.374556e+00, 'g_k_dil': 2.003291e+00, 'g_out_fox': 7.462372e+01, 'g_out_dil': 6.343442e+01, 'w_out': 1.468447e+00, 'g_ffn': 4.941184e+01, 'w_gate': 2.969632e-01, 'w_up': 3.131894e-01, 'w_down': 4.804493e-01}


def _to_microbatches(a, axis):
    t = _jnp.moveaxis(a, axis, 0)
    t = t.reshape((N_MICROBATCH, t.shape[0] // N_MICROBATCH) + t.shape[1:])
    return _jnp.moveaxis(t, 1, axis + 1)


def setup_inputs(seed: int = 0) -> dict:
    inp = _fwd_setup_inputs(seed)
    key = _jax.random.fold_in(_jax.random.key(seed), 7919)
    shape, _ = _output_shape()
    out = dict(inp)
    out["loss_target"] = _jax.random.normal(_jax.random.fold_in(key, 0), shape, _jnp.float32)
    for i, name in enumerate(TWIN_WEIGHTS):
        w = inp[name].astype(_jnp.float32)
        if MOMENT_SCALE is None:
            s = _jnp.sqrt(_jnp.mean(_jnp.square(w)) + 1e-30)
        else:
            s = MOMENT_SCALE[name]
        km, kv = _jax.random.split(_jax.random.fold_in(key, i + 1))
        out[name] = w
        out["m_" + name] = s * _jax.random.normal(km, w.shape, _jnp.float32)
        out["v_" + name] = (s * s) * _jax.random.uniform(kv, w.shape, _jnp.float32, 0.5, 1.5)
    if N_MICROBATCH > 1:
        for name, axis in PER_EXAMPLE_BATCH_AXIS.items():
            out[name] = _to_microbatches(out[name], axis)
    return {'x': out['x'], 'g_mix': out['g_mix'], 'w_in': out['w_in'], 'b_forget': out['b_forget'], 'g_q_fox': out['g_q_fox'], 'g_k_fox': out['g_k_fox'], 'g_q_dil': out['g_q_dil'], 'g_k_dil': out['g_k_dil'], 'g_out_fox': out['g_out_fox'], 'g_out_dil': out['g_out_dil'], 'w_out': out['w_out'], 'g_ffn': out['g_ffn'], 'w_gate': out['w_gate'], 'w_up': out['w_up'], 'w_down': out['w_down'], 'loss_target': out['loss_target'], 'm_g_mix': out['m_g_mix'], 'm_w_in': out['m_w_in'], 'm_b_forget': out['m_b_forget'], 'm_g_q_fox': out['m_g_q_fox'], 'm_g_k_fox': out['m_g_k_fox'], 'm_g_q_dil': out['m_g_q_dil'], 'm_g_k_dil': out['m_g_k_dil'], 'm_g_out_fox': out['m_g_out_fox'], 'm_g_out_dil': out['m_g_out_dil'], 'm_w_out': out['m_w_out'], 'm_g_ffn': out['m_g_ffn'], 'm_w_gate': out['m_w_gate'], 'm_w_up': out['m_w_up'], 'm_w_down': out['m_w_down'], 'v_g_mix': out['v_g_mix'], 'v_w_in': out['v_w_in'], 'v_b_forget': out['v_b_forget'], 'v_g_q_fox': out['v_g_q_fox'], 'v_g_k_fox': out['v_g_k_fox'], 'v_g_q_dil': out['v_g_q_dil'], 'v_g_k_dil': out['v_g_k_dil'], 'v_g_out_fox': out['v_g_out_fox'], 'v_g_out_dil': out['v_g_out_dil'], 'v_w_out': out['v_w_out'], 'v_g_ffn': out['v_g_ffn'], 'v_w_gate': out['v_w_gate'], 'v_w_up': out['v_w_up'], 'v_w_down': out['v_w_down']}


def _loss(weights, diff, rest, loss_target):
    with _jax.named_scope("forward"):
        args = {**rest, TWIN_DIFF_INPUT: diff, **{k: w.astype(_WEIGHT_DTYPES[k]) for k, w in weights.items()}}
        y = _forward(args)
    with _jax.named_scope("loss_head"):
        err = _jnp.square(y.astype(_jnp.float32) - loss_target)
        return 0.5 * _jnp.sum(_jnp.mean(err, axis=-1)) if err.ndim else 0.5 * err


def _adamw(w, g, m, v):
    m = ADAM_B1 * m + (1.0 - ADAM_B1) * g
    v = ADAM_B2 * v + (1.0 - ADAM_B2) * _jnp.square(g)
    m_hat = m / (1.0 - ADAM_B1 ** ADAM_STEP)
    v_hat = v / (1.0 - ADAM_B2 ** ADAM_STEP)
    delta = -ADAM_LR * (m_hat / (_jnp.sqrt(v_hat) + ADAM_EPS) + ADAM_WD * w)
    return delta, m, v


def reference(x, g_mix, w_in, b_forget, g_q_fox, g_k_fox, g_q_dil, g_k_dil, g_out_fox, g_out_dil, w_out, g_ffn, w_gate, w_up, w_down, loss_target, m_g_mix, m_w_in, m_b_forget, m_g_q_fox, m_g_k_fox, m_g_q_dil, m_g_k_dil, m_g_out_fox, m_g_out_dil, m_w_out, m_g_ffn, m_w_gate, m_w_up, m_w_down, v_g_mix, v_w_in, v_b_forget, v_g_q_fox, v_g_k_fox, v_g_q_dil, v_g_k_dil, v_g_out_fox, v_g_out_dil, v_w_out, v_g_ffn, v_w_gate, v_w_up, v_w_down):
    given = dict(x=x, g_mix=g_mix, w_in=w_in, b_forget=b_forget, g_q_fox=g_q_fox, g_k_fox=g_k_fox, g_q_dil=g_q_dil, g_k_dil=g_k_dil, g_out_fox=g_out_fox, g_out_dil=g_out_dil, w_out=w_out, g_ffn=g_ffn, w_gate=w_gate, w_up=w_up, w_down=w_down, loss_target=loss_target, m_g_mix=m_g_mix, m_w_in=m_w_in, m_b_forget=m_b_forget, m_g_q_fox=m_g_q_fox, m_g_k_fox=m_g_k_fox, m_g_q_dil=m_g_q_dil, m_g_k_dil=m_g_k_dil, m_g_out_fox=m_g_out_fox, m_g_out_dil=m_g_out_dil, m_w_out=m_w_out, m_g_ffn=m_g_ffn, m_w_gate=m_w_gate, m_w_up=m_w_up, m_w_down=m_w_down, v_g_mix=v_g_mix, v_w_in=v_w_in, v_b_forget=v_b_forget, v_g_q_fox=v_g_q_fox, v_g_k_fox=v_g_k_fox, v_g_q_dil=v_g_q_dil, v_g_k_dil=v_g_k_dil, v_g_out_fox=v_g_out_fox, v_g_out_dil=v_g_out_dil, v_w_out=v_w_out, v_g_ffn=v_g_ffn, v_w_gate=v_w_gate, v_w_up=v_w_up, v_w_down=v_w_down)
    weights = {n: given[n] for n in TWIN_WEIGHTS}
    shared = {n: given[n] for n in SHARED_INPUTS}
    per_example = {n: given[n] for n in ['x']}
    grad_fn = _jax.value_and_grad(_loss, argnums=(0, 1))

    def one_microbatch(ex, loss_target):
        ex = dict(ex)
        diff = ex.pop(TWIN_DIFF_INPUT)
        return grad_fn(weights, diff, {**shared, **ex}, loss_target)

    if N_MICROBATCH == 1:
        loss, (grad_w, grad_x) = one_microbatch(per_example, given["loss_target"])
    else:
        def body(carry, xs):
            loss_sum, grad_sum = carry
            l_k, (gw_k, gx_k) = one_microbatch(xs[0], xs[1])
            with _jax.named_scope("update"):
                return (loss_sum + l_k, _jax.tree.map(_jnp.add, grad_sum, gw_k)), gx_k

        init = (_jnp.zeros((), _jnp.float32), _jax.tree.map(_jnp.zeros_like, weights))
        (loss, grad_w), grad_x = _jax.lax.scan(body, init, (per_example, given["loss_target"]))
    with _jax.named_scope("update"):
        delta_w, new_m, new_v = {}, {}, {}
        for n in TWIN_WEIGHTS:
            delta_w[n], new_m[n], new_v[n] = _adamw(weights[n], grad_w[n], given["m_" + n], given["v_" + n])
    return (loss, grad_x, *[grad_w[n] for n in TWIN_WEIGHTS], *[delta_w[n] for n in TWIN_WEIGHTS],
            *[new_m[n] for n in TWIN_WEIGHTS], *[new_v[n] for n in TWIN_WEIGHTS])
```

```python
import functools
import math

import jax
import jax.numpy as jnp
import numpy as np
from jax import lax
from jax.experimental import pallas as pl
from jax.experimental.pallas import tpu as pltpu

F32 = jnp.float32
BF16 = jnp.bfloat16

D_MODEL = 1024
HEAD_DIM = 64
LANES = 128
N_PAIRS = D_MODEL // LANES
N_HEADS = 2 * N_PAIRS
N_HEADS_FOX = 8
W_GROUP = 512
D_FF = 2816
IN_COLS = 3080
DILATION_PAIRS = ((128, 1), (512, 4), (2048, 16))
ROPE_THETA = 500000.0
ROPE_DIM = 16
ROPE_HALF = ROPE_DIM // 2
EPS = 1e-6
NEG = -1e30
N_DEV = 8

ADAM_LR = 0.001
ADAM_B1 = 0.9
ADAM_B2 = 0.999
ADAM_EPS = 1e-08
ADAM_WD = 0.01
ADAM_STEP = 10

ROW_BLOCK = 512
ATT_BLOCK = 256
VMEM_LIMIT = 48 * 1024 * 1024

SMALL_ROWS = 32
SMALL_LAYOUT = (("g_mix", 0, 8), ("g_ffn", 8, 8), ("g_out_fox", 16, 4), ("g_out_dil", 20, 4),
                ("g_q_fox", 24, 1), ("g_k_fox", 25, 1), ("g_q_dil", 26, 1), ("g_k_dil", 27, 1),
                ("b_forget", 28, 1))
LOSS_ROW = 29


def _params(n_grid):
    return pltpu.CompilerParams(dimension_semantics=("arbitrary",) * n_grid, vmem_limit_bytes=VMEM_LIMIT)


def _divisor_block(n, cap):
    best = None
    for b in range(LANES, min(n, cap) + 1, LANES):
        if n % b == 0:
            best = b
    assert best is not None, n
    return best


def _split_dot(a, b_exact, terms):
    acc = None
    rest = a
    for _ in range(terms):
        hi = rest.astype(BF16)
        part = jnp.dot(hi, b_exact, preferred_element_type=F32)
        acc = part if acc is None else acc + part
        rest = rest - hi.astype(F32)
    return acc


def _split_dot_rhs(a_exact, b, terms):
    acc = None
    rest = b
    for _ in range(terms):
        hi = rest.astype(BF16)
        part = jnp.dot(a_exact, hi, preferred_element_type=F32)
        acc = part if acc is None else acc + part
        rest = rest - hi.astype(F32)
    return acc


def _dot_nt(a, b):
    return lax.dot_general(a, b, (((1,), (1,)), ((), ())), preferred_element_type=F32)


def _dot_tn(a, b):
    return lax.dot_general(a, b, (((0,), (0,)), ((), ())), preferred_element_type=F32)


def _exchange(name, items):
    n = len(items)
    out_shape = []
    for arr, scatter in items:
        shp = arr.shape[1:] if scatter else arr.shape
        out_shape.append(jax.ShapeDtypeStruct((N_DEV,) + tuple(shp), arr.dtype))

    def body(*refs):
        ins, outs = refs[:n], refs[n:2 * n]
        send_sems, recv_sems, local_sems = refs[2 * n:]
        x, y, c = lax.axis_index("x"), lax.axis_index("y"), lax.axis_index("c")
        me = 4 * x + 2 * y + c
        local, remote = [], []
        for k, (_, scatter) in enumerate(items):
            own = ins[k].at[me] if scatter else ins[k]
            cp = pltpu.make_async_copy(own, outs[k].at[me], local_sems.at[k])
            cp.start()
            local.append(cp)
        for r in range(1, N_DEV):
            px = 1 - x if r & 4 else x
            py = 1 - y if r & 2 else y
            pc = 1 - c if r & 1 else c
            peer = 4 * px + 2 * py + pc
            for k, (_, scatter) in enumerate(items):
                src = ins[k].at[peer] if scatter else ins[k]
                cp = pltpu.make_async_remote_copy(
                    src_ref=src, dst_ref=outs[k].at[me],
                    send_sem=send_sems.at[k, r - 1], recv_sem=recv_sems.at[k, r - 1],
                    device_id=(px, py, pc), device_id_type=pl.DeviceIdType.MESH)
                cp.start()
                remote.append(cp)
        for cp in remote:
            cp.wait_recv()
        for cp in remote:
            cp.wait_send()
        for cp in local:
            cp.wait()

    any_spec = pl.BlockSpec(memory_space=pl.ANY)
    return pl.pallas_call(
        body, name=name, out_shape=tuple(out_shape),
        in_specs=[any_spec] * n, out_specs=tuple([any_spec] * n),
        scratch_shapes=[pltpu.SemaphoreType.DMA((n, N_DEV - 1)), pltpu.SemaphoreType.DMA((n, N_DEV - 1)),
                        pltpu.SemaphoreType.DMA((n,))],
    )(*[a for a, _ in items])


def _matmul(a, w, *, name, out_dtype, trans_b=False, resid=None):
    t, k = a.shape
    n = w.shape[0] if trans_b else w.shape[1]
    assert (w.shape[1] if trans_b else w.shape[0]) == k
    bt = ROW_BLOCK
    bn = _divisor_block(n, 512)

    def body(*refs):
        a_ref, w_ref, o_ref = refs[0], refs[1], refs[-1]
        av = a_ref[...].astype(BF16)
        acc = _dot_nt(av, w_ref[...]) if trans_b else jnp.dot(av, w_ref[...], preferred_element_type=F32)
        if resid is not None:
            acc = refs[2][...] + acc
        o_ref[...] = acc.astype(o_ref.dtype)

    in_specs = [pl.BlockSpec((bt, k), lambda i, j: (i, 0)),
                pl.BlockSpec((bn, k), lambda i, j: (j, 0)) if trans_b else pl.BlockSpec((k, bn), lambda i, j: (0, j))]
    args = [a, w]
    if resid is not None:
        in_specs.append(pl.BlockSpec((bt, bn), lambda i, j: (i, j)))
        args.append(resid)
    return pl.pallas_call(
        body, name=name, grid=(t // bt, n // bn), in_specs=in_specs,
        out_specs=pl.BlockSpec((bt, bn), lambda i, j: (i, j)),
        out_shape=jax.ShapeDtypeStruct((t, n), out_dtype), compiler_params=_params(2),
    )(*args)


def _matmul_tn(a, b, *, name):
    t, m = a.shape
    n = b.shape[1]
    bt = ROW_BLOCK
    bm = _divisor_block(m, 1408)
    bn = _divisor_block(n, 640)

    def body(a_ref, b_ref, o_ref):
        @pl.when(pl.program_id(2) == 0)
        def _():
            o_ref[...] = jnp.zeros_like(o_ref)

        o_ref[...] += _dot_tn(a_ref[...].astype(BF16), b_ref[...].astype(BF16))

    return pl.pallas_call(
        body, name=name, grid=(m // bm, n // bn, t // bt),
        in_specs=[pl.BlockSpec((bt, bm), lambda i, j, s: (s, i)), pl.BlockSpec((bt, bn), lambda i, j, s: (s, j))],
        out_specs=pl.BlockSpec((bm, bn), lambda i, j, s: (i, j)),
        out_shape=jax.ShapeDtypeStruct((m, n), F32), compiler_params=_params(3),
    )(a, b)


def _rmsnorm_fwd(x, g, *, group, name):
    t, w = x.shape
    bt = ROW_BLOCK

    def body(x_ref, g_ref, o_ref):
        for s in range(0, w, group):
            xs = x_ref[:, s:s + group].astype(F32)
            r = lax.rsqrt(jnp.mean(xs * xs, axis=-1, keepdims=True) + EPS)
            o_ref[:, s:s + group] = (xs * r * g_ref[:, s:s + group]).astype(o_ref.dtype)

    return pl.pallas_call(
        body, name=name, grid=(t // bt,),
        in_specs=[pl.BlockSpec((bt, w), lambda i: (i, 0)), pl.BlockSpec((1, w), lambda i: (0, 0))],
        out_specs=pl.BlockSpec((bt, w), lambda i: (i, 0)),
        out_shape=jax.ShapeDtypeStruct((t, w), BF16), compiler_params=_params(1),
    )(x, g)


def _rmsnorm_bwd(dh, x, g, *, group, name, out_dtype, resid=None):
    t, w = x.shape
    bt = ROW_BLOCK

    def body(*refs):
        dh_ref, x_ref, g_ref = refs[:3]
        dx_ref, dg_ref = refs[-2:]

        @pl.when(pl.program_id(0) == 0)
        def _():
            dg_ref[...] = jnp.zeros_like(dg_ref)

        for s in range(0, w, group):
            xs = x_ref[:, s:s + group].astype(F32)
            dhs = dh_ref[:, s:s + group].astype(F32)
            r = lax.rsqrt(jnp.mean(xs * xs, axis=-1, keepdims=True) + EPS)
            xh = xs * r
            dg_ref[:, s:s + group] += jnp.sum(dhs * xh, axis=0, keepdims=True)
            dxh = dhs * g_ref[:, s:s + group]
            dx = r * (dxh - xh * jnp.mean(dxh * xh, axis=-1, keepdims=True))
            if resid is not None:
                dx = refs[3][:, s:s + group] + dx
            dx_ref[:, s:s + group] = dx.astype(dx_ref.dtype)

    row = pl.BlockSpec((bt, w), lambda i: (i, 0))
    vec = pl.BlockSpec((1, w), lambda i: (0, 0))
    args = [dh, x, g] + ([resid] if resid is not None else [])
    return pl.pallas_call(
        body, name=name, grid=(t // bt,),
        in_specs=[row, row, vec] + ([row] if resid is not None else []),
        out_specs=(row, vec),
        out_shape=(jax.ShapeDtypeStruct((t, w), out_dtype), jax.ShapeDtypeStruct((1, w), F32)),
        compiler_params=_params(1),
    )(*args)


def _tile_plan(tile):
    is_q = tile < N_PAIRS
    is_dil = (tile % N_PAIRS) >= N_PAIRS // 2
    return is_q, is_dil, (0 if is_q else 2) + (1 if is_dil else 0)


def _segment_ones():
    lane = np.arange(LANES)
    return jnp.asarray((lane[:, None] // HEAD_DIM) == (lane[None, :] // HEAD_DIM), BF16)


def _rope_tables(seq):
    inv_freq = jnp.power(jnp.float32(ROPE_THETA), -jnp.arange(ROPE_HALF, dtype=F32) * 2.0 / ROPE_DIM)
    ang = jnp.arange(seq).astype(F32)[:, None] * inv_freq[None, :]
    cos, sin = jnp.cos(ang), jnp.sin(ang)
    ones = jnp.ones((seq, HEAD_DIM - ROPE_DIM), F32)
    zeros = jnp.zeros((seq, HEAD_DIM - ROPE_DIM), F32)
    zh = jnp.zeros((seq, ROPE_HALF), F32)
    cos_t = jnp.concatenate([cos, cos, ones], axis=1)
    sin_a = jnp.concatenate([-sin, zh, zeros], axis=1)
    sin_b = jnp.concatenate([zh, sin, zeros], axis=1)
    return tuple(jnp.tile(tab, (1, 2)) for tab in (cos_t, sin_a, sin_b))


def _log_sigmoid(z):
    return jnp.minimum(z, 0.0) - jnp.log1p(jnp.exp(-jnp.abs(z)))


def _qk_prep_fwd(proj, fa, b_pad, gains, rope, seq):
    t = proj.shape[0]
    bt = ROW_BLOCK
    nsb = seq // bt
    seg = _segment_ones()
    rr = np.arange(bt)
    tri = jnp.asarray(rr[:, None] >= rr[None, :], BF16)

    def body(p_ref, fa_ref, b_ref, g_ref, cos_ref, sa_ref, sb_ref, seg_ref, tri_ref, qk_ref, c_ref, ct_ref, carry):
        @pl.when(pl.program_id(0) % nsb == 0)
        def _():
            carry[...] = jnp.zeros_like(carry)

        for tile in range(2 * N_PAIRS):
            is_q, is_dil, grow = _tile_plan(tile)
            cols = slice(tile * LANES, (tile + 1) * LANES)
            xs = p_ref[:, cols].astype(F32)
            r = lax.rsqrt(_split_dot(xs * xs, seg_ref[...], 2) * (1.0 / HEAD_DIM) + EPS)
            yv = xs * r * g_ref[grow:grow + 1, :]
            if is_dil:
                yv = (yv * cos_ref[...] + pltpu.roll(yv, LANES - ROPE_HALF, 1) * sa_ref[...]
                      + pltpu.roll(yv, ROPE_HALF, 1) * sb_ref[...])
            if is_q:
                yv = yv * (HEAD_DIM ** -0.5)
            qk_ref[:, cols] = yv.astype(BF16)

        lane = lax.broadcasted_iota(jnp.int32, (bt, LANES), 1)
        logf = jnp.where(lane < N_HEADS_FOX, _log_sigmoid(fa_ref[...] + b_ref[...]), 0.0)
        cblk = _split_dot_rhs(tri_ref[...], logf, 3) + carry[0:1, :]
        carry[0:1, :] = cblk[bt - 1:bt, :]
        c_ref[...] = cblk
        ct_ref[0] = cblk.T[0:N_HEADS, :]

    row128 = pl.BlockSpec((bt, LANES), lambda i: (i, 0))
    rope_spec = pl.BlockSpec((bt, LANES), lambda i: (i % nsb, 0))
    const = lambda shape: pl.BlockSpec(shape, lambda i: (0, 0))
    return pl.pallas_call(
        body, name="qk_prep_fwd", grid=(t // bt,),
        in_specs=[pl.BlockSpec((bt, 2 * D_MODEL), lambda i: (i, 0)), row128, const((1, LANES)), const((8, LANES)),
                  rope_spec, rope_spec, rope_spec, const((LANES, LANES)), const((bt, bt))],
        out_specs=(pl.BlockSpec((bt, 2 * D_MODEL), lambda i: (i, 0)), row128,
                   pl.BlockSpec((1, N_HEADS, bt), lambda i: (i // nsb, 0, i % nsb))),
        out_shape=(jax.ShapeDtypeStruct((t, 2 * D_MODEL), BF16), jax.ShapeDtypeStruct((t, LANES), F32),
                   jax.ShapeDtypeStruct((t // seq, N_HEADS, seq), F32)),
        scratch_shapes=[pltpu.VMEM((8, LANES), F32)], compiler_params=_params(1),
    )(proj, fa, b_pad, gains, *rope, seg, tri)


def _qk_prep_bwd(dq, dk, proj, fa, b_pad, gains, rope, dct, dcq, seq):
    t = proj.shape[0]
    bt = ROW_BLOCK
    nsb = seq // bt
    nblk = t // bt
    seg = _segment_ones()
    rr = np.arange(bt)
    triu = jnp.asarray(rr[:, None] <= rr[None, :], BF16)

    def body(dq_ref, dk_ref, p_ref, fa_ref, b_ref, g_ref, cos_ref, sa_ref, sb_ref, seg_ref, triu_ref, dct_ref, dcq_ref,
             dp_ref, dfa_ref, dg_ref, db_ref, carry):
        step = pl.program_id(0)

        @pl.when(step == 0)
        def _():
            dg_ref[...] = jnp.zeros_like(dg_ref)
            db_ref[...] = jnp.zeros_like(db_ref)

        @pl.when(step % nsb == 0)
        def _():
            carry[...] = jnp.zeros_like(carry)

        for tile in range(2 * N_PAIRS):
            is_q, is_dil, grow = _tile_plan(tile)
            cols = slice(tile * LANES, (tile + 1) * LANES)
            src = dq_ref if is_q else dk_ref
            half = slice((tile % N_PAIRS) * LANES, (tile % N_PAIRS + 1) * LANES)
            dy = src[:, half]
            if is_q:
                dy = dy * (HEAD_DIM ** -0.5)
            if is_dil:
                dy = (dy * cos_ref[...] + pltpu.roll(dy * sa_ref[...], ROPE_HALF, 1)
                      + pltpu.roll(dy * sb_ref[...], LANES - ROPE_HALF, 1))
            xs = p_ref[:, cols].astype(F32)
            r = lax.rsqrt(_split_dot(xs * xs, seg_ref[...], 2) * (1.0 / HEAD_DIM) + EPS)
            xh = xs * r
            dg_ref[tile:tile + 1, :] += jnp.sum(dy * xh, axis=0, keepdims=True)
            dxh = dy * g_ref[grow:grow + 1, :]
            seg_mean = _split_dot(dxh * xh, seg_ref[...], 2) * (1.0 / HEAD_DIM)
            dp_ref[:, cols] = (r * (dxh - xh * seg_mean)).astype(BF16)

        lane = lax.broadcasted_iota(jnp.int32, (bt, LANES), 1)
        dc_rows = jnp.concatenate([dct_ref[0], jnp.zeros((LANES - N_HEADS, bt), F32)], axis=0)
        dc = jnp.where(lane < N_HEADS_FOX, dc_rows.T, 0.0)
        for h in range(N_HEADS_FOX):
            dc = dc + jnp.where(lane == h, dcq_ref[:, h * HEAD_DIM:h * HEAD_DIM + 1], 0.0)
        dlogf = _split_dot_rhs(triu_ref[...], dc, 3) + carry[0:1, :]
        carry[0:1, :] = dlogf[0:1, :]
        z = fa_ref[...] + b_ref[...]
        dfa = dlogf * (1.0 / (1.0 + jnp.exp(z)))
        db_ref[0:1, :] += jnp.sum(dfa, axis=0, keepdims=True)
        dfa_ref[...] = dfa.astype(BF16)

    rev = lambda i: nblk - 1 - i
    row = lambda w: pl.BlockSpec((bt, w), lambda i: (rev(i), 0))
    rope_spec = pl.BlockSpec((bt, LANES), lambda i: (rev(i) % nsb, 0))
    const = lambda shape: pl.BlockSpec(shape, lambda i: (0, 0))
    return pl.pallas_call(
        body, name="qk_prep_bwd", grid=(nblk,),
        in_specs=[row(D_MODEL), row(D_MODEL), row(2 * D_MODEL), row(LANES), const((1, LANES)), const((8, LANES)),
                  rope_spec, rope_spec, rope_spec, const((LANES, LANES)), const((bt, bt)),
                  pl.BlockSpec((1, N_HEADS, bt), lambda i: (rev(i) // nsb, 0, rev(i) % nsb)), row(W_GROUP)],
        out_specs=(row(2 * D_MODEL), row(LANES), const((2 * N_PAIRS, LANES)), const((8, LANES))),
        out_shape=(jax.ShapeDtypeStruct((t, 2 * D_MODEL), BF16), jax.ShapeDtypeStruct((t, LANES), BF16),
                   jax.ShapeDtypeStruct((2 * N_PAIRS, LANES), F32), jax.ShapeDtypeStruct((8, LANES), F32)),
        scratch_shapes=[pltpu.VMEM((8, LANES), F32)], compiler_params=_params(1),
    )(dq, dk, proj, fa, b_pad, gains, *rope, seg, triu, dct, dcq)


def _bias_tables(seq):
    nb = seq // ATT_BLOCK
    idx = jnp.arange(ATT_BLOCK)
    dist = (jnp.arange(nb)[:, None, None] * ATT_BLOCK + idx[None, :, None] - idx[None, None, :])
    causal = dist >= 0
    count = jnp.zeros(dist.shape, jnp.int32)
    for window, dilation in DILATION_PAIRS:
        count = count + (causal & (dist % dilation == 0) & (dist <= window)).astype(jnp.int32)
    fox = jnp.where(causal, 0.0, NEG).astype(F32)
    dil = jnp.where(count == 3, math.log(3.0), jnp.where(count == 2, math.log(2.0), jnp.where(count == 1, 0.0, NEG)))
    return jnp.stack([fox, dil.astype(F32)], axis=0)


def _attn_specs(seq):
    col = lambda off: pl.BlockSpec((seq, LANES), lambda b, j: (b, off + j))
    nb = seq // ATT_BLOCK
    c_spec = pl.BlockSpec((seq, LANES), lambda b, j: (b, 0))
    ct_spec = pl.BlockSpec((1, N_HEADS, seq), lambda b, j: (b, 0, 0))
    table_spec = pl.BlockSpec((1, nb, ATT_BLOCK, ATT_BLOCK), lambda b, j: (j // (N_PAIRS // 2), 0, 0, 0))
    return col, c_spec, ct_spec, table_spec


def _head_bias(j, c_blk, ct_blk):
    lane = lax.broadcasted_iota(jnp.int32, c_blk.shape, 1)
    sub = lax.broadcasted_iota(jnp.int32, ct_blk.shape, 0)
    cq = [jnp.sum(jnp.where(lane == 2 * j + e, c_blk, 0.0), axis=1, keepdims=True) for e in range(2)]
    ck = [jnp.sum(jnp.where(sub == 2 * j + e, ct_blk, 0.0), axis=0, keepdims=True) for e in range(2)]
    return cq, ck


def _attn_fwd(qk, proj, c, ct, tables, seq):
    t = qk.shape[0]
    nb = seq // ATT_BLOCK
    blk = ATT_BLOCK

    def body(q_ref, k_ref, v_ref, c_ref, ct_ref, tab_ref, o_ref, lse_ref):
        j = pl.program_id(1)
        lo_q = lax.broadcasted_iota(jnp.int32, (blk, LANES), 1) < HEAD_DIM

        def q_block(i, _):
            rows = pl.ds(pl.multiple_of(i * blk, blk), blk)
            q = q_ref[rows, :]
            zero = jnp.zeros_like(q)
            qe = [jnp.where(lo_q, q, zero), jnp.where(lo_q, zero, q)]
            c_blk = c_ref[rows, :]

            def kv_block(jj, carry):
                m, l, acc = carry
                krows = pl.ds(pl.multiple_of(jj * blk, blk), blk)
                k = k_ref[krows, :]
                v = v_ref[krows, :]
                ve = [jnp.where(lo_q, v, zero), jnp.where(lo_q, zero, v)]
                cq, ck = _head_bias(j, c_blk, ct_ref[0, :, krows])
                tab = tab_ref[0, i - jj]
                new_m, new_l, alpha, pv = [], [], [], None
                for e in range(2):
                    s = _dot_nt(qe[e], k) + (cq[e] - ck[e]) + tab
                    m_e = jnp.maximum(m[e], jnp.max(s, axis=1, keepdims=True))
                    a_e = jnp.exp(m[e] - m_e)
                    p = jnp.exp(s - m_e)
                    new_m.append(m_e)
                    new_l.append(a_e * l[e] + jnp.sum(p, axis=1, keepdims=True))
                    alpha.append(a_e)
                    part = jnp.dot(p.astype(BF16), ve[e], preferred_element_type=F32)
                    pv = part if pv is None else pv + part
                acc = acc * jnp.where(lo_q, alpha[0], alpha[1]) + pv
                return tuple(new_m), tuple(new_l), acc

            col0 = jnp.full((blk, 1), NEG, F32)
            zcol = jnp.zeros((blk, 1), F32)
            m, l, acc = lax.fori_loop(0, i + 1, kv_block, ((col0, col0), (zcol, zcol), jnp.zeros((blk, LANES), F32)))
            o_ref[rows, :] = (acc / jnp.where(lo_q, l[0], l[1])).astype(o_ref.dtype)
            lse_ref[rows, :] = jnp.where(lo_q, m[0] + jnp.log(l[0]), m[1] + jnp.log(l[1]))
            return 0

        lax.fori_loop(0, nb, q_block, 0)

    col, c_spec, ct_spec, table_spec = _attn_specs(seq)
    return pl.pallas_call(
        body, name="attn_fwd", grid=(t // seq, N_PAIRS),
        in_specs=[col(0), col(N_PAIRS), col(2 * N_PAIRS), c_spec, ct_spec, table_spec],
        out_specs=(col(0), col(0)),
        out_shape=(jax.ShapeDtypeStruct((t, D_MODEL), BF16), jax.ShapeDtypeStruct((t, D_MODEL), F32)),
        compiler_params=_params(2),
    )(qk, qk, proj, c, ct, tables)


def _attn_bwd(qk, proj, c, ct, tables, o, lse, do, seq):
    t = qk.shape[0]
    nb = seq // ATT_BLOCK
    blk = ATT_BLOCK

    def body(q_ref, k_ref, v_ref, c_ref, ct_ref, tab_ref, o_ref, lse_ref, do_ref,
             dq_ref, dk_ref, dv_ref, dct_ref, dcq_ref, dv_acc, dc_acc):
        j = pl.program_id(1)
        lo_q = lax.broadcasted_iota(jnp.int32, (blk, LANES), 1) < HEAD_DIM
        dk_ref[...] = jnp.zeros_like(dk_ref)
        dv_acc[...] = jnp.zeros_like(dv_acc)
        dc_acc[...] = jnp.zeros_like(dc_acc)

        def q_block(i, _):
            rows = pl.ds(pl.multiple_of(i * blk, blk), blk)
            q = q_ref[rows, :]
            do_blk = do_ref[rows, :]
            zero = jnp.zeros_like(q)
            qe = [jnp.where(lo_q, q, zero), jnp.where(lo_q, zero, q)]
            doe = [jnp.where(lo_q, do_blk, zero), jnp.where(lo_q, zero, do_blk)]
            prod = do_blk.astype(F32) * o_ref[rows, :].astype(F32)
            delta = [jnp.sum(jnp.where(lo_q, prod, 0.0), axis=1, keepdims=True),
                     jnp.sum(jnp.where(lo_q, 0.0, prod), axis=1, keepdims=True)]
            lse_blk = lse_ref[rows, :]
            lse_e = [lse_blk[:, 0:1], lse_blk[:, HEAD_DIM:HEAD_DIM + 1]]
            c_blk = c_ref[rows, :]

            def kv_block(jj, carry):
                dq_acc, row_sum = carry
                krows = pl.ds(pl.multiple_of(jj * blk, blk), blk)
                k = k_ref[krows, :]
                v = v_ref[krows, :]
                ke = [jnp.where(lo_q, k, zero), jnp.where(lo_q, zero, k)]
                cq, ck = _head_bias(j, c_blk, ct_ref[0, :, krows])
                tab = tab_ref[0, i - jj]
                dv_part, dk_part, new_row_sum = None, None, []
                for e in range(2):
                    s = _dot_nt(qe[e], k) + (cq[e] - ck[e]) + tab
                    p = jnp.exp(s - lse_e[e])
                    dp = _dot_nt(doe[e], v)
                    ds = p * (dp - delta[e])
                    dc_acc[e:e + 1, krows] -= jnp.sum(ds, axis=0, keepdims=True)
                    new_row_sum.append(row_sum[e] + jnp.sum(ds, axis=1, keepdims=True))
                    dsb = ds.astype(BF16)
                    dv_e = _dot_tn(p.astype(BF16), doe[e])
                    dk_e = _dot_tn(dsb, qe[e])
                    dv_part = dv_e if dv_part is None else dv_part + dv_e
                    dk_part = dk_e if dk_part is None else dk_part + dk_e
                    dq_acc = dq_acc + jnp.dot(dsb, ke[e], preferred_element_type=F32)
                dv_acc[krows, :] += dv_part
                dk_ref[krows, :] += dk_part
                return dq_acc, tuple(new_row_sum)

            zcol = jnp.zeros((blk, 1), F32)
            dq_blk, row_sum = lax.fori_loop(0, i + 1, kv_block, (jnp.zeros((blk, LANES), F32), (zcol, zcol)))
            dq_ref[rows, :] = dq_blk
            dcq_ref[rows, :] = jnp.where(lo_q, row_sum[0], row_sum[1])
            return 0

        lax.fori_loop(0, nb, q_block, 0)
        dv_ref[...] = dv_acc[...].astype(dv_ref.dtype)

        @pl.when(j == 0)
        def _():
            dct_ref[...] = jnp.zeros_like(dct_ref)

        sub = lax.broadcasted_iota(jnp.int32, (N_HEADS, seq), 0)
        dct_ref[0] = jnp.where(sub == 2 * j, dc_acc[0:1, :], jnp.where(sub == 2 * j + 1, dc_acc[1:2, :], dct_ref[0]))

    col, c_spec, ct_spec, table_spec = _attn_specs(seq)
    return pl.pallas_call(
        body, name="attn_bwd", grid=(t // seq, N_PAIRS),
        in_specs=[col(0), col(N_PAIRS), col(2 * N_PAIRS), c_spec, ct_spec, table_spec, col(0), col(0), col(0)],
        out_specs=(col(0), col(0), col(0), ct_spec, col(0)),
        out_shape=(jax.ShapeDtypeStruct((t, D_MODEL), F32), jax.ShapeDtypeStruct((t, D_MODEL), F32),
                   jax.ShapeDtypeStruct((t, D_MODEL), BF16), jax.ShapeDtypeStruct((t // seq, N_HEADS, seq), F32),
                   jax.ShapeDtypeStruct((t, D_MODEL), F32)),
        scratch_shapes=[pltpu.VMEM((seq, LANES), F32), pltpu.VMEM((8, seq), F32)],
        compiler_params=_params(2),
    )(qk, qk, proj, c, ct, tables, o, lse, do)


def _swiglu_fwd(au):
    t = au.shape[0]
    bt = ROW_BLOCK

    def body(a_ref, u_ref, f_ref):
        a = a_ref[...].astype(F32)
        f_ref[...] = (a * jax.nn.sigmoid(a) * u_ref[...].astype(F32)).astype(BF16)

    return pl.pallas_call(
        body, name="swiglu_fwd", grid=(t // bt,),
        in_specs=[pl.BlockSpec((bt, D_FF), lambda i: (i, 0)), pl.BlockSpec((bt, D_FF), lambda i: (i, 1))],
        out_specs=pl.BlockSpec((bt, D_FF), lambda i: (i, 0)),
        out_shape=jax.ShapeDtypeStruct((t, D_FF), BF16), compiler_params=_params(1),
    )(au, au)


def _swiglu_bwd(df, au):
    t = au.shape[0]
    bt = ROW_BLOCK

    def body(df_ref, au_ref, o_ref):
        a = au_ref[:, 0:D_FF].astype(F32)
        u = au_ref[:, D_FF:2 * D_FF].astype(F32)
        dfv = df_ref[...].astype(F32)
        sg = jax.nn.sigmoid(a)
        o_ref[:, 0:D_FF] = (dfv * u * sg * (1.0 + a * (1.0 - sg))).astype(BF16)
        o_ref[:, D_FF:2 * D_FF] = (dfv * a * sg).astype(BF16)

    return pl.pallas_call(
        body, name="swiglu_bwd", grid=(t // bt,),
        in_specs=[pl.BlockSpec((bt, D_FF), lambda i: (i, 0)), pl.BlockSpec((bt, 2 * D_FF), lambda i: (i, 0))],
        out_specs=pl.BlockSpec((bt, 2 * D_FF), lambda i: (i, 0)),
        out_shape=jax.ShapeDtypeStruct((t, 2 * D_FF), BF16), compiler_params=_params(1),
    )(df, au)


def _loss_head(yv, target):
    t, w = yv.shape
    bt = ROW_BLOCK

    def body(y_ref, t_ref, dy_ref, loss_ref):
        @pl.when(pl.program_id(0) == 0)
        def _():
            loss_ref[...] = jnp.zeros_like(loss_ref)

        err = y_ref[...] - t_ref[...]
        dy_ref[...] = err * (1.0 / w)
        loss_ref[...] += 0.5 * jnp.sum(jnp.mean(err * err, axis=-1, keepdims=True), axis=0, keepdims=True)

    row = pl.BlockSpec((bt, w), lambda i: (i, 0))
    return pl.pallas_call(
        body, name="loss_head", grid=(t // bt,), in_specs=[row, row],
        out_specs=(row, pl.BlockSpec((8, LANES), lambda i: (0, 0))),
        out_shape=(jax.ShapeDtypeStruct((t, w), F32), jax.ShapeDtypeStruct((8, LANES), F32)),
        compiler_params=_params(1),
    )(yv, target)


def _adamw(parts, w, m, v, *, name):
    rows, cols = w.shape
    br = rows if rows <= 512 else 256
    assert rows % br == 0

    def body(p_ref, w_ref, m_ref, v_ref, g_ref, d_ref, nm_ref, nv_ref):
        g = p_ref[0]
        for r in range(1, N_DEV):
            g = g + p_ref[r]
        m2 = ADAM_B1 * m_ref[...] + (1.0 - ADAM_B1) * g
        v2 = ADAM_B2 * v_ref[...] + (1.0 - ADAM_B2) * jnp.square(g)
        m_hat = m2 / (1.0 - ADAM_B1 ** ADAM_STEP)
        v_hat = v2 / (1.0 - ADAM_B2 ** ADAM_STEP)
        g_ref[...] = g
        d_ref[...] = -ADAM_LR * (m_hat / (jnp.sqrt(v_hat) + ADAM_EPS) + ADAM_WD * w_ref[...])
        nm_ref[...] = m2
        nv_ref[...] = v2

    blk = pl.BlockSpec((br, cols), lambda i: (i, 0))
    shape = jax.ShapeDtypeStruct((rows, cols), F32)
    return pl.pallas_call(
        body, name=name, grid=(rows // br,),
        in_specs=[pl.BlockSpec((N_DEV, br, cols), lambda i: (0, i, 0)), blk, blk, blk],
        out_specs=(blk, blk, blk, blk), out_shape=(shape, shape, shape, shape), compiler_params=_params(1),
    )(parts, w, m, v)


_QA, _KA, _VA, _FA, _QD, _KD, _VD = (0, 512), (512, 1024), (1024, 1536), (1536, 1544), (1544, 2056), (2056, 2568), (2568, 3080)
_MAIN_ORDER = (_QA, _QD, _KA, _KD, _VA, _VD)
MAIN_COLS = 3 * D_MODEL
PROJ_COLS = MAIN_COLS + LANES


def _unshard_cols(parts):
    n, r, c = parts.shape
    return jnp.transpose(parts, (1, 0, 2)).reshape(r, n * c)


def _shard_cols(full):
    r, nc = full.shape
    return jnp.transpose(full.reshape(r, N_DEV, nc // N_DEV), (1, 0, 2))


def _w_in_to_kernel(w_full):
    main = jnp.concatenate([w_full[:, a:b] for a, b in _MAIN_ORDER], axis=1)
    forget = jnp.pad(w_full[:, _FA[0]:_FA[1]], ((0, 0), (0, LANES - N_HEADS_FOX)))
    return main, forget


def _w_in_from_kernel(g):
    pos = {span: i * W_GROUP for i, span in enumerate(_MAIN_ORDER)}
    parts = []
    for span in (_QA, _KA, _VA, _FA, _QD, _KD, _VD):
        if span == _FA:
            parts.append(g[:, MAIN_COLS:MAIN_COLS + N_HEADS_FOX])
        else:
            parts.append(g[:, pos[span]:pos[span] + W_GROUP])
    return jnp.concatenate(parts, axis=1)


def _pack_small(vals):
    rows = []
    for name, _, n_rows in SMALL_LAYOUT:
        flat = vals[name].reshape(-1).astype(F32)
        rows.append(jnp.pad(flat, (0, n_rows * LANES - flat.shape[0])).reshape(n_rows, LANES))
    packed = jnp.concatenate(rows, axis=0)
    return jnp.pad(packed, ((0, SMALL_ROWS - packed.shape[0]), (0, 0)))


def _unpack_small(packed, like):
    out = {}
    for name, row, n_rows in SMALL_LAYOUT:
        n = like[name].size
        out[name] = packed[row:row + n_rows].reshape(-1)[:n].reshape(like[name].shape)
    return out


def _local_step(x, target, small, w_main, w_fa, w_out, w_gu, w_down):
    bsz, seq, _ = x.shape
    t = bsz * seq
    xf = x.reshape(t, D_MODEL)
    tf = target.reshape(t, D_MODEL)
    row = lambda v: v.reshape(1, -1)
    g_out = jnp.concatenate([small["g_out_fox"], small["g_out_dil"]]).reshape(1, D_MODEL)
    gains = jnp.concatenate(
        [jnp.tile(small[n].reshape(1, HEAD_DIM), (1, 2)) for n in ("g_q_fox", "g_q_dil", "g_k_fox", "g_k_dil")]
        + [jnp.zeros((4, LANES), F32)], axis=0)
    b_pad = jnp.pad(small["b_forget"].reshape(1, N_HEADS_FOX), ((0, 0), (0, LANES - N_HEADS_FOX)))
    rope = _rope_tables(seq)
    tables = _bias_tables(seq)
    w_in_all = jnp.concatenate([w_main, w_fa], axis=1)

    h1 = _rmsnorm_fwd(xf, row(small["g_mix"]), group=D_MODEL, name="norm_mix")
    proj = _matmul(h1, w_main, name="in_proj", out_dtype=BF16)
    fa = _matmul(h1, w_fa, name="in_proj_forget", out_dtype=F32)
    qk, c, ct = _qk_prep_fwd(proj, fa, b_pad, gains, rope, seq)
    o, lse = _attn_fwd(qk, proj, c, ct, tables, seq)
    on = _rmsnorm_fwd(o, g_out, group=W_GROUP, name="norm_out")
    x2 = _matmul(on, w_out, name="out_proj", out_dtype=F32, resid=xf)
    h2 = _rmsnorm_fwd(x2, row(small["g_ffn"]), group=D_MODEL, name="norm_ffn")
    au = _matmul(h2, w_gu, name="ffn_gate_up", out_dtype=BF16)
    f = _swiglu_fwd(au)
    yv = _matmul(f, w_down, name="ffn_down", out_dtype=F32, resid=x2)
    dy, loss_tile = _loss_head(yv, tf)

    df = _matmul(dy, w_down, name="d_ffn_down", out_dtype=BF16, trans_b=True)
    dau = _swiglu_bwd(df, au)
    dh2 = _matmul(dau, w_gu, name="d_ffn_gate_up", out_dtype=F32, trans_b=True)
    dx2, dg_ffn = _rmsnorm_bwd(dh2, x2, row(small["g_ffn"]), group=D_MODEL, name="d_norm_ffn", out_dtype=F32, resid=dy)
    don = _matmul(dx2, w_out, name="d_out_proj", out_dtype=F32, trans_b=True)
    do, dg_out = _rmsnorm_bwd(don, o, g_out, group=W_GROUP, name="d_norm_out", out_dtype=BF16)
    dq, dk, dv, dct, dcq = _attn_bwd(qk, proj, c, ct, tables, o, lse, do, seq)
    dqk, dfa, dgains, db = _qk_prep_bwd(dq, dk, proj, fa, b_pad, gains, rope, dct, dcq, seq)
    dproj = jnp.concatenate([dqk, dv, dfa], axis=1)
    dh1 = _matmul(dproj, w_in_all, name="d_in_proj", out_dtype=F32, trans_b=True)
    dx, dg_mix = _rmsnorm_bwd(dh1, xf, row(small["g_mix"]), group=D_MODEL, name="d_norm_mix", out_dtype=F32, resid=dx2)

    gw_down = _matmul_tn(f, dy, name="gw_down")
    gw_gu = _matmul_tn(h2, dau, name="gw_gate_up")
    gw_out = _matmul_tn(on, dx2, name="gw_out")
    gw_in = _matmul_tn(h1, dproj, name="gw_in")

    fold = lambda rows: jnp.sum(rows[:, :HEAD_DIM] + rows[:, HEAD_DIM:], axis=0)
    half = N_PAIRS // 2
    gsmall = {
        "g_mix": dg_mix, "g_ffn": dg_ffn, "g_out_fox": dg_out[0, :W_GROUP], "g_out_dil": dg_out[0, W_GROUP:],
        "g_q_fox": fold(dgains[0:half]), "g_q_dil": fold(dgains[half:N_PAIRS]),
        "g_k_fox": fold(dgains[N_PAIRS:N_PAIRS + half]), "g_k_dil": fold(dgains[N_PAIRS + half:]),
        "b_forget": db[0, :N_HEADS_FOX],
    }
    return loss_tile, dx.reshape(x.shape), (gw_in, gw_out, gw_gu, gw_down), gsmall


def kernel(x, g_mix, w_in, b_forget, g_q_fox, g_k_fox, g_q_dil, g_k_dil, g_out_fox, g_out_dil, w_out, g_ffn, w_gate, w_up, w_down, loss_target, m_g_mix, m_w_in, m_b_forget, m_g_q_fox, m_g_k_fox, m_g_q_dil, m_g_k_dil, m_g_out_fox, m_g_out_dil, m_w_out, m_g_ffn, m_w_gate, m_w_up, m_w_down, v_g_mix, v_w_in, v_b_forget, v_g_q_fox, v_g_k_fox, v_g_q_dil, v_g_k_dil, v_g_out_fox, v_g_out_dil, v_w_out, v_g_ffn, v_w_gate, v_w_up, v_w_down):
    args = dict(locals())
    small_names = [name for name, _, _ in SMALL_LAYOUT]
    big_names = ["w_in", "w_out", "w_gate", "w_up", "w_down"]
    small = {n: args[n][0] for n in small_names}

    gathered = _exchange("weight_gather", [(args[n][0].astype(BF16), False) for n in big_names])
    w_main, w_fa = _w_in_to_kernel(_unshard_cols(gathered[0]))
    w_out_full = gathered[1].reshape(D_MODEL, D_MODEL)
    w_gu = jnp.concatenate([_unshard_cols(gathered[2]), _unshard_cols(gathered[3])], axis=1)
    w_down_full = gathered[4].reshape(D_FF, D_MODEL)

    loss_tile, grad_x, (gw_in, gw_out, gw_gu, gw_down), gsmall = _local_step(
        x, loss_target, small, w_main, w_fa, w_out_full, w_gu, w_down_full)

    rows_out = D_MODEL // N_DEV
    rows_down = D_FF // N_DEV
    packed = _pack_small(gsmall).at[LOSS_ROW].set(loss_tile[0])
    parts = _exchange("grad_exchange", [
        (_shard_cols(_w_in_from_kernel(gw_in)), True),
        (gw_out.reshape(N_DEV, rows_out, D_MODEL), True),
        (_shard_cols(gw_gu[:, :D_FF]), True),
        (_shard_cols(gw_gu[:, D_FF:]), True),
        (gw_down.reshape(N_DEV, rows_down, D_MODEL), True),
        (packed, False),
    ])

    grads, deltas, new_m, new_v = {}, {}, {}, {}
    for n, part in zip(big_names, parts[:5]):
        res = _adamw(part, args[n][0], args["m_" + n][0], args["v_" + n][0], name="adamw_" + n)
        grads[n], deltas[n], new_m[n], new_v[n] = [r[None] for r in res]
    res = _adamw(parts[5], _pack_small(small), _pack_small({n: args["m_" + n][0] for n in small_names}),
                 _pack_small({n: args["v_" + n][0] for n in small_names}), name="adamw_small")
    loss = res[0][LOSS_ROW, 0]
    for dst, packed_res in zip((grads, deltas, new_m, new_v), res):
        for n, val in _unpack_small(packed_res, small).items():
            dst[n] = val[None]

    order = ["g_mix", "w_in", "b_forget", "g_q_fox", "g_k_fox", "g_q_dil", "g_k_dil", "g_out_fox", "g_out_dil",
             "w_out", "g_ffn", "w_gate", "w_up", "w_down"]
    return (loss, grad_x, *[grads[n] for n in order], *[deltas[n] for n in order],
            *[new_m[n] for n in order], *[new_v[n] for n in order])
```

```python
import functools
import math

import jax
import jax.numpy as jnp
import numpy as np
from jax import lax
from jax.experimental import pallas as pl
from jax.experimental.pallas import tpu as pltpu

F32 = jnp.float32
BF16 = jnp.bfloat16

D_MODEL = 1024
HEAD_DIM = 64
LANES = 128
N_PAIRS = D_MODEL // LANES
N_HEADS = 2 * N_PAIRS
N_HEADS_FOX = 8
W_GROUP = 512
D_FF = 2816
IN_COLS = 3080
DILATION_PAIRS = ((128, 1), (512, 4), (2048, 16))
ROPE_THETA = 500000.0
ROPE_DIM = 16
ROPE_HALF = ROPE_DIM // 2
EPS = 1e-6
NEG = -1e30
N_DEV = 8

ADAM_LR = 0.001
ADAM_B1 = 0.9
ADAM_B2 = 0.999
ADAM_EPS = 1e-08
ADAM_WD = 0.01
ADAM_STEP = 10

ROW_BLOCK = 512
ATT_BLOCK = 256
VMEM_LIMIT = 56 * 1024 * 1024
MATMUL_VMEM_BUDGET = 40 * 1024 * 1024

SMALL_ROWS = 32
SMALL_LAYOUT = (("g_mix", 0, 8), ("g_ffn", 8, 8), ("g_out_fox", 16, 4), ("g_out_dil", 20, 4),
                ("g_q_fox", 24, 1), ("g_k_fox", 25, 1), ("g_q_dil", 26, 1), ("g_k_dil", 27, 1),
                ("b_forget", 28, 1))
LOSS_ROW = 29


def _params(n_grid):
    return pltpu.CompilerParams(dimension_semantics=("arbitrary",) * n_grid, vmem_limit_bytes=VMEM_LIMIT)


def _divisor_block(n, cap):
    best = None
    for b in range(LANES, min(n, cap) + 1, LANES):
        if n % b == 0:
            best = b
    assert best is not None, n
    return best


def _split_dot(a, b_exact, terms):
    acc = None
    rest = a
    for _ in range(terms):
        hi = rest.astype(BF16)
        part = jnp.dot(hi, b_exact, preferred_element_type=F32)
        acc = part if acc is None else acc + part
        rest = rest - hi.astype(F32)
    return acc


def _split_dot_rhs(a_exact, b, terms):
    acc = None
    rest = b
    for _ in range(terms):
        hi = rest.astype(BF16)
        part = jnp.dot(a_exact, hi, preferred_element_type=F32)
        acc = part if acc is None else acc + part
        rest = rest - hi.astype(F32)
    return acc


def _dot_nt(a, b):
    return lax.dot_general(a, b, (((1,), (1,)), ((), ())), preferred_element_type=F32)


def _dot_tn(a, b):
    return lax.dot_general(a, b, (((0,), (0,)), ((), ())), preferred_element_type=F32)


class _Exchange:
    def __init__(self, items):
        self.items = items
        self.n = len(items)
        self.arrays = [a for a, _ in items]
        self.out_shape = [jax.ShapeDtypeStruct((N_DEV,) + tuple(a.shape[1:] if sc else a.shape), a.dtype)
                          for a, sc in items]
        self.specs = [pl.BlockSpec(memory_space=pl.ANY)] * self.n
        self.scratch = [pltpu.SemaphoreType.DMA((self.n, N_DEV - 1)), pltpu.SemaphoreType.DMA((self.n, N_DEV - 1)),
                        pltpu.SemaphoreType.DMA((self.n,))]

    def run(self, ins, outs, sems, first, last, compute):
        send_sems, recv_sems, local_sems = sems
        x, y, c = lax.axis_index("x"), lax.axis_index("y"), lax.axis_index("c")
        me = 4 * x + 2 * y + c
        local, remote = [], []
        for k, (_, scatter) in enumerate(self.items):
            own = ins[k].at[me] if scatter else ins[k]
            local.append(pltpu.make_async_copy(own, outs[k].at[me], local_sems.at[k]))
        for r in range(1, N_DEV):
            px = 1 - x if r & 4 else x
            py = 1 - y if r & 2 else y
            pc = 1 - c if r & 1 else c
            peer = 4 * px + 2 * py + pc
            for k, (_, scatter) in enumerate(self.items):
                src = ins[k].at[peer] if scatter else ins[k]
                remote.append(pltpu.make_async_remote_copy(
                    src_ref=src, dst_ref=outs[k].at[me],
                    send_sem=send_sems.at[k, r - 1], recv_sem=recv_sems.at[k, r - 1],
                    device_id=(px, py, pc), device_id_type=pl.DeviceIdType.MESH))

        def start():
            for cp in local + remote:
                cp.start()

        def finish():
            for cp in remote:
                cp.wait_recv()
            for cp in remote:
                cp.wait_send()
            for cp in local:
                cp.wait()

        if first is None:
            start()
            compute()
            finish()
        else:
            pl.when(first)(start)
            compute()
            pl.when(last)(finish)


def _grid_ends(grid):
    ids = [pl.program_id(d) for d in range(len(grid))]
    first = functools.reduce(jnp.logical_and, [i == 0 for i in ids])
    last = functools.reduce(jnp.logical_and, [i == g - 1 for i, g in zip(ids, grid)])
    return first, last


def _host(core, n_in, n_out, n_scratch, hosted, grid):
    if hosted is None:
        return core
    nh = hosted.n

    def body(*refs):
        ins, rest = refs[:n_in], refs[n_in:]
        h_ins, rest = rest[:nh], rest[nh:]
        outs, rest = rest[:n_out], rest[n_out:]
        h_outs, rest = rest[:nh], rest[nh:]
        scratch, sems = rest[:n_scratch], rest[n_scratch:]
        first, last = _grid_ends(grid)
        hosted.run(h_ins, h_outs, sems, first, last, lambda: core(*ins, *outs, *scratch))

    return body


def _hosted_parts(hosted):
    if hosted is None:
        return [], [], [], []
    return list(hosted.specs), list(hosted.out_shape), list(hosted.arrays), list(hosted.scratch)


def _exchange(name, items):
    ex = _Exchange(items)
    n = ex.n

    def body(*refs):
        ex.run(refs[:n], refs[n:2 * n], refs[2 * n:], None, None, lambda: None)

    return pl.pallas_call(
        body, name=name, out_shape=tuple(ex.out_shape), in_specs=ex.specs, out_specs=tuple(ex.specs),
        scratch_shapes=ex.scratch,
    )(*ex.arrays)


def _matmul_blocks(t, k, n, a_bytes, o_bytes):
    for bt, cap in ((1024, 1408), (1024, 512), (512, 512)):
        if t % bt:
            continue
        bn = _divisor_block(n, cap)
        if 2 * (bt * k * a_bytes + bn * k * 2 + bt * bn * o_bytes) <= MATMUL_VMEM_BUDGET:
            return bt, bn
    return ROW_BLOCK, _divisor_block(n, 256)


def _matmul(a, w, *, name, out_dtype, trans_b=False, resid=None, hosted=None):
    t, k = a.shape
    n = w.shape[0] if trans_b else w.shape[1]
    assert (w.shape[1] if trans_b else w.shape[0]) == k
    o_bytes = jnp.dtype(out_dtype).itemsize + (4 if resid is not None else 0)
    bt, bn = _matmul_blocks(t, k, n, a.dtype.itemsize, o_bytes)
    grid = (t // bt, n // bn)
    n_in = 2 + (resid is not None)

    def body(*refs):
        a_ref, w_ref, o_ref = refs[0], refs[1], refs[-1]
        av = a_ref[...].astype(BF16)
        acc = _dot_nt(av, w_ref[...]) if trans_b else jnp.dot(av, w_ref[...], preferred_element_type=F32)
        if resid is not None:
            acc = refs[2][...] + acc
        o_ref[...] = acc.astype(o_ref.dtype)

    in_specs = [pl.BlockSpec((bt, k), lambda i, j: (i, 0)),
                pl.BlockSpec((bn, k), lambda i, j: (j, 0)) if trans_b else pl.BlockSpec((k, bn), lambda i, j: (0, j))]
    args = [a, w]
    if resid is not None:
        in_specs.append(pl.BlockSpec((bt, bn), lambda i, j: (i, j)))
        args.append(resid)
    h_specs, h_shapes, h_args, h_scratch = _hosted_parts(hosted)
    res = pl.pallas_call(
        _host(body, n_in, 1, 0, hosted, grid), name=name, grid=grid, in_specs=in_specs + h_specs,
        out_specs=tuple([pl.BlockSpec((bt, bn), lambda i, j: (i, j))] + h_specs),
        out_shape=tuple([jax.ShapeDtypeStruct((t, n), out_dtype)] + h_shapes),
        scratch_shapes=h_scratch, compiler_params=_params(2),
    )(*args, *h_args)
    return res if hosted else res[0]


def _matmul_tn(a, b, *, name):
    t, m = a.shape
    n = b.shape[1]
    bt = 2048 if t % 2048 == 0 else ROW_BLOCK
    bm = _divisor_block(m, 1408)
    bn = _divisor_block(n, 1024)

    def body(a_ref, b_ref, o_ref):
        @pl.when(pl.program_id(2) == 0)
        def _():
            o_ref[...] = jnp.zeros_like(o_ref)

        o_ref[...] += _dot_tn(a_ref[...], b_ref[...])

    return pl.pallas_call(
        body, name=name, grid=(m // bm, n // bn, t // bt),
        in_specs=[pl.BlockSpec((bt, bm), lambda i, j, s: (s, i)), pl.BlockSpec((bt, bn), lambda i, j, s: (s, j))],
        out_specs=pl.BlockSpec((bm, bn), lambda i, j, s: (i, j)),
        out_shape=jax.ShapeDtypeStruct((m, n), F32), compiler_params=_params(3),
    )(a, b)


def _rmsnorm_fwd(x, g, *, group, name):
    t, w = x.shape
    bt = ROW_BLOCK

    def body(x_ref, g_ref, o_ref):
        for s in range(0, w, group):
            xs = x_ref[:, s:s + group].astype(F32)
            r = lax.rsqrt(jnp.mean(xs * xs, axis=-1, keepdims=True) + EPS)
            o_ref[:, s:s + group] = (xs * r * g_ref[:, s:s + group]).astype(o_ref.dtype)

    return pl.pallas_call(
        body, name=name, grid=(t // bt,),
        in_specs=[pl.BlockSpec((bt, w), lambda i: (i, 0)), pl.BlockSpec((1, w), lambda i: (0, 0))],
        out_specs=pl.BlockSpec((bt, w), lambda i: (i, 0)),
        out_shape=jax.ShapeDtypeStruct((t, w), BF16), compiler_params=_params(1),
    )(x, g)


def _rmsnorm_bwd(dh, x, g, *, group, name, out_dtypes, resid=None):
    t, w = x.shape
    bt = ROW_BLOCK
    n_in = 3 + (resid is not None)

    def body(*refs):
        dh_ref, x_ref, g_ref = refs[:3]
        dx_refs, dg_ref = refs[n_in:-1], refs[-1]

        @pl.when(pl.program_id(0) == 0)
        def _():
            dg_ref[...] = jnp.zeros_like(dg_ref)

        for s in range(0, w, group):
            xs = x_ref[:, s:s + group].astype(F32)
            dhs = dh_ref[:, s:s + group].astype(F32)
            r = lax.rsqrt(jnp.mean(xs * xs, axis=-1, keepdims=True) + EPS)
            xh = xs * r
            dg_ref[:, s:s + group] += jnp.sum(dhs * xh, axis=0, keepdims=True)
            dxh = dhs * g_ref[:, s:s + group]
            dx = r * (dxh - xh * jnp.mean(dxh * xh, axis=-1, keepdims=True))
            if resid is not None:
                dx = refs[3][:, s:s + group] + dx
            for dx_ref in dx_refs:
                dx_ref[:, s:s + group] = dx.astype(dx_ref.dtype)

    row = pl.BlockSpec((bt, w), lambda i: (i, 0))
    vec = pl.BlockSpec((1, w), lambda i: (0, 0))
    args = [dh, x, g] + ([resid] if resid is not None else [])
    return pl.pallas_call(
        body, name=name, grid=(t // bt,),
        in_specs=[row, row, vec] + ([row] if resid is not None else []),
        out_specs=tuple([row] * len(out_dtypes) + [vec]),
        out_shape=tuple([jax.ShapeDtypeStruct((t, w), dt) for dt in out_dtypes] + [jax.ShapeDtypeStruct((1, w), F32)]),
        compiler_params=_params(1),
    )(*args)


def _tile_plan(tile):
    is_q = tile < N_PAIRS
    is_dil = (tile % N_PAIRS) >= N_PAIRS // 2
    return is_q, is_dil, (0 if is_q else 2) + (1 if is_dil else 0)


def _segment_ones():
    lane = np.arange(LANES)
    return jnp.asarray((lane[:, None] // HEAD_DIM) == (lane[None, :] // HEAD_DIM), BF16)


def _rope_tables(seq):
    inv_freq = jnp.power(jnp.float32(ROPE_THETA), -jnp.arange(ROPE_HALF, dtype=F32) * 2.0 / ROPE_DIM)
    ang = jnp.arange(seq).astype(F32)[:, None] * inv_freq[None, :]
    cos, sin = jnp.cos(ang), jnp.sin(ang)
    ones = jnp.ones((seq, HEAD_DIM - ROPE_DIM), F32)
    zeros = jnp.zeros((seq, HEAD_DIM - ROPE_DIM), F32)
    zh = jnp.zeros((seq, ROPE_HALF), F32)
    cos_t = jnp.concatenate([cos, cos, ones], axis=1)
    sin_a = jnp.concatenate([-sin, zh, zeros], axis=1)
    sin_b = jnp.concatenate([zh, sin, zeros], axis=1)
    return tuple(jnp.tile(tab, (1, 2)) for tab in (cos_t, sin_a, sin_b))


def _log_sigmoid(z):
    return jnp.minimum(z, 0.0) - jnp.log1p(jnp.exp(-jnp.abs(z)))


def _qk_prep_fwd(proj, fa, b_pad, gains, rope, seq):
    t = proj.shape[0]
    bt = ROW_BLOCK
    nsb = seq // bt
    seg = _segment_ones()
    rr = np.arange(bt)
    tri = jnp.asarray(rr[:, None] >= rr[None, :], BF16)

    def body(p_ref, fa_ref, b_ref, g_ref, cos_ref, sa_ref, sb_ref, seg_ref, tri_ref, qk_ref, c_ref, ct_ref, carry):
        @pl.when(pl.program_id(0) % nsb == 0)
        def _():
            carry[...] = jnp.zeros_like(carry)

        for tile in range(2 * N_PAIRS):
            is_q, is_dil, grow = _tile_plan(tile)
            cols = slice(tile * LANES, (tile + 1) * LANES)
            xs = p_ref[:, cols].astype(F32)
            r = lax.rsqrt(_split_dot(xs * xs, seg_ref[...], 2) * (1.0 / HEAD_DIM) + EPS)
            yv = xs * r * g_ref[grow:grow + 1, :]
            if is_dil:
                yv = (yv * cos_ref[...] + pltpu.roll(yv, LANES - ROPE_HALF, 1) * sa_ref[...]
                      + pltpu.roll(yv, ROPE_HALF, 1) * sb_ref[...])
            if is_q:
                yv = yv * (HEAD_DIM ** -0.5)
            qk_ref[:, cols] = yv.astype(BF16)

        lane = lax.broadcasted_iota(jnp.int32, (bt, LANES), 1)
        logf = jnp.where(lane < N_HEADS_FOX, _log_sigmoid(fa_ref[...] + b_ref[...]), 0.0)
        cblk = _split_dot_rhs(tri_ref[...], logf, 3) + carry[0:1, :]
        carry[0:1, :] = cblk[bt - 1:bt, :]
        c_ref[...] = cblk
        ct_ref[0] = cblk.T[0:N_HEADS, :]

    row128 = pl.BlockSpec((bt, LANES), lambda i: (i, 0))
    rope_spec = pl.BlockSpec((bt, LANES), lambda i: (i % nsb, 0))
    const = lambda shape: pl.BlockSpec(shape, lambda i: (0, 0))
    return pl.pallas_call(
        body, name="qk_prep_fwd", grid=(t // bt,),
        in_specs=[pl.BlockSpec((bt, 2 * D_MODEL), lambda i: (i, 0)), row128, const((1, LANES)), const((8, LANES)),
                  rope_spec, rope_spec, rope_spec, const((LANES, LANES)), const((bt, bt))],
        out_specs=(pl.BlockSpec((bt, 2 * D_MODEL), lambda i: (i, 0)), row128,
                   pl.BlockSpec((1, N_HEADS, bt), lambda i: (i // nsb, 0, i % nsb))),
        out_shape=(jax.ShapeDtypeStruct((t, 2 * D_MODEL), BF16), jax.ShapeDtypeStruct((t, LANES), F32),
                   jax.ShapeDtypeStruct((t // seq, N_HEADS, seq), F32)),
        scratch_shapes=[pltpu.VMEM((8, LANES), F32)], compiler_params=_params(1),
    )(proj, fa, b_pad, gains, *rope, seg, tri)


def _qk_prep_bwd(dq, dk, proj, fa, b_pad, gains, rope, dct, dcq, seq):
    t = proj.shape[0]
    bt = ROW_BLOCK
    nsb = seq // bt
    nblk = t // bt
    seg = _segment_ones()
    rr = np.arange(bt)
    triu = jnp.asarray(rr[:, None] <= rr[None, :], BF16)

    def body(dq_ref, dk_ref, p_ref, fa_ref, b_ref, g_ref, cos_ref, sa_ref, sb_ref, seg_ref, triu_ref, dct_ref, dcq_ref,
             dp_ref, dfa_ref, dg_ref, db_ref, carry):
        step = pl.program_id(0)

        @pl.when(step == 0)
        def _():
            dg_ref[...] = jnp.zeros_like(dg_ref)
            db_ref[...] = jnp.zeros_like(db_ref)

        @pl.when(step % nsb == 0)
        def _():
            carry[...] = jnp.zeros_like(carry)

        for tile in range(2 * N_PAIRS):
            is_q, is_dil, grow = _tile_plan(tile)
            cols = slice(tile * LANES, (tile + 1) * LANES)
            src = dq_ref if is_q else dk_ref
            half = slice((tile % N_PAIRS) * LANES, (tile % N_PAIRS + 1) * LANES)
            dy = src[:, half]
            if is_q:
                dy = dy * (HEAD_DIM ** -0.5)
            if is_dil:
                dy = (dy * cos_ref[...] + pltpu.roll(dy * sa_ref[...], ROPE_HALF, 1)
                      + pltpu.roll(dy * sb_ref[...], LANES - ROPE_HALF, 1))
            xs = p_ref[:, cols].astype(F32)
            r = lax.rsqrt(_split_dot(xs * xs, seg_ref[...], 2) * (1.0 / HEAD_DIM) + EPS)
            xh = xs * r
            dg_ref[tile:tile + 1, :] += jnp.sum(dy * xh, axis=0, keepdims=True)
            dxh = dy * g_ref[grow:grow + 1, :]
            seg_mean = _split_dot(dxh * xh, seg_ref[...], 2) * (1.0 / HEAD_DIM)
            dp_ref[:, cols] = (r * (dxh - xh * seg_mean)).astype(BF16)

        lane = lax.broadcasted_iota(jnp.int32, (bt, LANES), 1)
        dc_rows = jnp.concatenate([dct_ref[0], jnp.zeros((LANES - N_HEADS, bt), F32)], axis=0)
        dc = jnp.where(lane < N_HEADS_FOX, dc_rows.T, 0.0)
        for h in range(N_HEADS_FOX):
            dc = dc + jnp.where(lane == h, dcq_ref[:, h * HEAD_DIM:h * HEAD_DIM + 1], 0.0)
        dlogf = _split_dot_rhs(triu_ref[...], dc, 3) + carry[0:1, :]
        carry[0:1, :] = dlogf[0:1, :]
        z = fa_ref[...] + b_ref[...]
        dfa = dlogf * (1.0 / (1.0 + jnp.exp(z)))
        db_ref[0:1, :] += jnp.sum(dfa, axis=0, keepdims=True)
        dfa_ref[...] = dfa.astype(BF16)

    rev = lambda i: nblk - 1 - i
    row = lambda w: pl.BlockSpec((bt, w), lambda i: (rev(i), 0))
    rope_spec = pl.BlockSpec((bt, LANES), lambda i: (rev(i) % nsb, 0))
    const = lambda shape: pl.BlockSpec(shape, lambda i: (0, 0))
    return pl.pallas_call(
        body, name="qk_prep_bwd", grid=(nblk,),
        in_specs=[row(D_MODEL), row(D_MODEL), row(2 * D_MODEL), row(LANES), const((1, LANES)), const((8, LANES)),
                  rope_spec, rope_spec, rope_spec, const((LANES, LANES)), const((bt, bt)),
                  pl.BlockSpec((1, N_HEADS, bt), lambda i: (rev(i) // nsb, 0, rev(i) % nsb)), row(W_GROUP)],
        out_specs=(row(2 * D_MODEL), row(LANES), const((2 * N_PAIRS, LANES)), const((8, LANES))),
        out_shape=(jax.ShapeDtypeStruct((t, 2 * D_MODEL), BF16), jax.ShapeDtypeStruct((t, LANES), BF16),
                   jax.ShapeDtypeStruct((2 * N_PAIRS, LANES), F32), jax.ShapeDtypeStruct((8, LANES), F32)),
        scratch_shapes=[pltpu.VMEM((8, LANES), F32)], compiler_params=_params(1),
    )(dq, dk, proj, fa, b_pad, gains, *rope, seg, triu, dct, dcq)


def _bias_tables(seq):
    nb = seq // ATT_BLOCK
    idx = jnp.arange(ATT_BLOCK)
    dist = (jnp.arange(nb)[:, None, None] * ATT_BLOCK + idx[None, :, None] - idx[None, None, :])
    causal = dist >= 0
    count = jnp.zeros(dist.shape, jnp.int32)
    for window, dilation in DILATION_PAIRS:
        count = count + (causal & (dist % dilation == 0) & (dist <= window)).astype(jnp.int32)
    fox = jnp.where(causal, 0.0, NEG).astype(F32)
    dil = jnp.where(count == 3, math.log(3.0), jnp.where(count == 2, math.log(2.0), jnp.where(count == 1, 0.0, NEG)))
    return jnp.stack([fox, dil.astype(F32)], axis=0)


def _attn_specs(seq):
    col = lambda off: pl.BlockSpec((seq, LANES), lambda b, j: (b, off + j))
    nb = seq // ATT_BLOCK
    c_spec = pl.BlockSpec((seq, LANES), lambda b, j: (b, 0))
    ct_spec = pl.BlockSpec((1, N_HEADS, seq), lambda b, j: (b, 0, 0))
    table_spec = pl.BlockSpec((1, nb, ATT_BLOCK, ATT_BLOCK), lambda b, j: (j // (N_PAIRS // 2), 0, 0, 0))
    return col, c_spec, ct_spec, table_spec


def _head_bias(j, c_blk, ct_blk):
    lane = lax.broadcasted_iota(jnp.int32, c_blk.shape, 1)
    sub = lax.broadcasted_iota(jnp.int32, ct_blk.shape, 0)
    cq = [jnp.sum(jnp.where(lane == 2 * j + e, c_blk, 0.0), axis=1, keepdims=True) for e in range(2)]
    ck = [jnp.sum(jnp.where(sub == 2 * j + e, ct_blk, 0.0), axis=0, keepdims=True) for e in range(2)]
    return cq, ck


def _attn_fwd(qk, proj, c, ct, tables, seq, hosted=None):
    t = qk.shape[0]
    nb = seq // ATT_BLOCK
    blk = ATT_BLOCK

    def body(q_ref, k_ref, v_ref, c_ref, ct_ref, tab_ref, o_ref, lse_ref):
        j = pl.program_id(1)
        lo_q = lax.broadcasted_iota(jnp.int32, (blk, LANES), 1) < HEAD_DIM

        def q_block(i, _):
            rows = pl.ds(pl.multiple_of(i * blk, blk), blk)
            q = q_ref[rows, :]
            zero = jnp.zeros_like(q)
            qe = [jnp.where(lo_q, q, zero), jnp.where(lo_q, zero, q)]
            c_blk = c_ref[rows, :]

            def kv_block(jj, carry):
                m, l, acc = carry
                krows = pl.ds(pl.multiple_of(jj * blk, blk), blk)
                k = k_ref[krows, :]
                v = v_ref[krows, :]
                ve = [jnp.where(lo_q, v, zero), jnp.where(lo_q, zero, v)]
                cq, ck = _head_bias(j, c_blk, ct_ref[0, :, krows])
                tab = tab_ref[0, i - jj]
                new_m, new_l, alpha, pv = [], [], [], None
                for e in range(2):
                    s = _dot_nt(qe[e], k) + (cq[e] - ck[e]) + tab
                    m_e = jnp.maximum(m[e], jnp.max(s, axis=1, keepdims=True))
                    a_e = jnp.exp(m[e] - m_e)
                    p = jnp.exp(s - m_e)
                    new_m.append(m_e)
                    new_l.append(a_e * l[e] + jnp.sum(p, axis=1, keepdims=True))
                    alpha.append(a_e)
                    part = jnp.dot(p.astype(BF16), ve[e], preferred_element_type=F32)
                    pv = part if pv is None else pv + part
                acc = acc * jnp.where(lo_q, alpha[0], alpha[1]) + pv
                return tuple(new_m), tuple(new_l), acc

            col0 = jnp.full((blk, 1), NEG, F32)
            zcol = jnp.zeros((blk, 1), F32)
            m, l, acc = lax.fori_loop(0, i + 1, kv_block, ((col0, col0), (zcol, zcol), jnp.zeros((blk, LANES), F32)))
            o_ref[rows, :] = (acc / jnp.where(lo_q, l[0], l[1])).astype(o_ref.dtype)
            lse_ref[rows, :] = jnp.where(lo_q, m[0] + jnp.log(l[0]), m[1] + jnp.log(l[1]))
            return 0

        lax.fori_loop(0, nb, q_block, 0)

    col, c_spec, ct_spec, table_spec = _attn_specs(seq)
    grid = (t // seq, N_PAIRS)
    h_specs, h_shapes, h_args, h_scratch = _hosted_parts(hosted)
    return pl.pallas_call(
        _host(body, 6, 2, 0, hosted, grid), name="attn_fwd", grid=grid,
        in_specs=[col(0), col(N_PAIRS), col(2 * N_PAIRS), c_spec, ct_spec, table_spec] + h_specs,
        out_specs=tuple([col(0), col(0)] + h_specs),
        out_shape=tuple([jax.ShapeDtypeStruct((t, D_MODEL), BF16), jax.ShapeDtypeStruct((t, D_MODEL), F32)] + h_shapes),
        scratch_shapes=h_scratch, compiler_params=_params(2),
    )(qk, qk, proj, c, ct, tables, *h_args)


def _attn_bwd(qk, proj, c, ct, tables, o, lse, do, seq, hosted=None):
    t = qk.shape[0]
    nb = seq // ATT_BLOCK
    blk = ATT_BLOCK

    def body(q_ref, k_ref, v_ref, c_ref, ct_ref, tab_ref, o_ref, lse_ref, do_ref,
             dq_ref, dk_ref, dv_ref, dct_ref, dcq_ref, dv_acc, dc_acc):
        j = pl.program_id(1)
        lo_q = lax.broadcasted_iota(jnp.int32, (blk, LANES), 1) < HEAD_DIM
        dk_ref[...] = jnp.zeros_like(dk_ref)
        dv_acc[...] = jnp.zeros_like(dv_acc)
        dc_acc[...] = jnp.zeros_like(dc_acc)

        def q_block(i, _):
            rows = pl.ds(pl.multiple_of(i * blk, blk), blk)
            q = q_ref[rows, :]
            do_blk = do_ref[rows, :]
            zero = jnp.zeros_like(q)
            qe = [jnp.where(lo_q, q, zero), jnp.where(lo_q, zero, q)]
            doe = [jnp.where(lo_q, do_blk, zero), jnp.where(lo_q, zero, do_blk)]
            prod = do_blk.astype(F32) * o_ref[rows, :].astype(F32)
            delta = [jnp.sum(jnp.where(lo_q, prod, 0.0), axis=1, keepdims=True),
                     jnp.sum(jnp.where(lo_q, 0.0, prod), axis=1, keepdims=True)]
            lse_blk = lse_ref[rows, :]
            lse_e = [lse_blk[:, 0:1], lse_blk[:, HEAD_DIM:HEAD_DIM + 1]]
            c_blk = c_ref[rows, :]

            def kv_block(jj, carry):
                dq_acc, row_sum = carry
                krows = pl.ds(pl.multiple_of(jj * blk, blk), blk)
                k = k_ref[krows, :]
                v = v_ref[krows, :]
                ke = [jnp.where(lo_q, k, zero), jnp.where(lo_q, zero, k)]
                cq, ck = _head_bias(j, c_blk, ct_ref[0, :, krows])
                tab = tab_ref[0, i - jj]
                dv_part, dk_part, new_row_sum = None, None, []
                for e in range(2):
                    s = _dot_nt(qe[e], k) + (cq[e] - ck[e]) + tab
                    p = jnp.exp(s - lse_e[e])
                    dp = _dot_nt(doe[e], v)
                    ds = p * (dp - delta[e])
                    dc_acc[e:e + 1, krows] -= jnp.sum(ds, axis=0, keepdims=True)
                    new_row_sum.append(row_sum[e] + jnp.sum(ds, axis=1, keepdims=True))
                    dsb = ds.astype(BF16)
                    dv_e = _dot_tn(p.astype(BF16), doe[e])
                    dk_e = _dot_tn(dsb, qe[e])
                    dv_part = dv_e if dv_part is None else dv_part + dv_e
                    dk_part = dk_e if dk_part is None else dk_part + dk_e
                    dq_acc = dq_acc + jnp.dot(dsb, ke[e], preferred_element_type=F32)
                dv_acc[krows, :] += dv_part
                dk_ref[krows, :] += dk_part
                return dq_acc, tuple(new_row_sum)

            zcol = jnp.zeros((blk, 1), F32)
            dq_blk, row_sum = lax.fori_loop(0, i + 1, kv_block, (jnp.zeros((blk, LANES), F32), (zcol, zcol)))
            dq_ref[rows, :] = dq_blk
            dcq_ref[rows, :] = jnp.where(lo_q, row_sum[0], row_sum[1])
            return 0

        lax.fori_loop(0, nb, q_block, 0)
        dv_ref[...] = dv_acc[...].astype(dv_ref.dtype)

        @pl.when(j == 0)
        def _():
            dct_ref[...] = jnp.zeros_like(dct_ref)

        sub = lax.broadcasted_iota(jnp.int32, (N_HEADS, seq), 0)
        dct_ref[0] = jnp.where(sub == 2 * j, dc_acc[0:1, :], jnp.where(sub == 2 * j + 1, dc_acc[1:2, :], dct_ref[0]))

    col, c_spec, ct_spec, table_spec = _attn_specs(seq)
    grid = (t // seq, N_PAIRS)
    h_specs, h_shapes, h_args, h_scratch = _hosted_parts(hosted)
    return pl.pallas_call(
        _host(body, 9, 5, 2, hosted, grid), name="attn_bwd", grid=grid,
        in_specs=[col(0), col(N_PAIRS), col(2 * N_PAIRS), c_spec, ct_spec, table_spec, col(0), col(0), col(0)] + h_specs,
        out_specs=tuple([col(0), col(0), col(0), ct_spec, col(0)] + h_specs),
        out_shape=tuple([jax.ShapeDtypeStruct((t, D_MODEL), F32), jax.ShapeDtypeStruct((t, D_MODEL), F32),
                         jax.ShapeDtypeStruct((t, D_MODEL), BF16), jax.ShapeDtypeStruct((t // seq, N_HEADS, seq), F32),
                         jax.ShapeDtypeStruct((t, D_MODEL), F32)] + h_shapes),
        scratch_shapes=[pltpu.VMEM((seq, LANES), F32), pltpu.VMEM((8, seq), F32)] + h_scratch,
        compiler_params=_params(2),
    )(qk, qk, proj, c, ct, tables, o, lse, do, *h_args)


def _swiglu_fwd(au):
    t = au.shape[0]
    bt = ROW_BLOCK

    def body(a_ref, u_ref, f_ref):
        a = a_ref[...].astype(F32)
        f_ref[...] = (a * jax.nn.sigmoid(a) * u_ref[...].astype(F32)).astype(BF16)

    return pl.pallas_call(
        body, name="swiglu_fwd", grid=(t // bt,),
        in_specs=[pl.BlockSpec((bt, D_FF), lambda i: (i, 0)), pl.BlockSpec((bt, D_FF), lambda i: (i, 1))],
        out_specs=pl.BlockSpec((bt, D_FF), lambda i: (i, 0)),
        out_shape=jax.ShapeDtypeStruct((t, D_FF), BF16), compiler_params=_params(1),
    )(au, au)


def _swiglu_bwd(df, au):
    t = au.shape[0]
    bt = ROW_BLOCK

    def body(df_ref, au_ref, o_ref):
        a = au_ref[:, 0:D_FF].astype(F32)
        u = au_ref[:, D_FF:2 * D_FF].astype(F32)
        dfv = df_ref[...].astype(F32)
        sg = jax.nn.sigmoid(a)
        o_ref[:, 0:D_FF] = (dfv * u * sg * (1.0 + a * (1.0 - sg))).astype(BF16)
        o_ref[:, D_FF:2 * D_FF] = (dfv * a * sg).astype(BF16)

    return pl.pallas_call(
        body, name="swiglu_bwd", grid=(t // bt,),
        in_specs=[pl.BlockSpec((bt, D_FF), lambda i: (i, 0)), pl.BlockSpec((bt, 2 * D_FF), lambda i: (i, 0))],
        out_specs=pl.BlockSpec((bt, 2 * D_FF), lambda i: (i, 0)),
        out_shape=jax.ShapeDtypeStruct((t, 2 * D_FF), BF16), compiler_params=_params(1),
    )(df, au)


def _loss_head(yv, target):
    t, w = yv.shape
    bt = ROW_BLOCK

    def body(y_ref, t_ref, dy_ref, dy16_ref, loss_ref):
        @pl.when(pl.program_id(0) == 0)
        def _():
            loss_ref[...] = jnp.zeros_like(loss_ref)

        err = y_ref[...] - t_ref[...]
        dy = err * (1.0 / w)
        dy_ref[...] = dy
        dy16_ref[...] = dy.astype(BF16)
        loss_ref[...] += 0.5 * jnp.sum(jnp.mean(err * err, axis=-1, keepdims=True), axis=0, keepdims=True)

    row = pl.BlockSpec((bt, w), lambda i: (i, 0))
    return pl.pallas_call(
        body, name="loss_head", grid=(t // bt,), in_specs=[row, row],
        out_specs=(row, row, pl.BlockSpec((8, LANES), lambda i: (0, 0))),
        out_shape=(jax.ShapeDtypeStruct((t, w), F32), jax.ShapeDtypeStruct((t, w), BF16),
                   jax.ShapeDtypeStruct((8, LANES), F32)),
        compiler_params=_params(1),
    )(yv, target)


def _adamw(parts, w, m, v, *, name):
    rows, cols = w.shape
    br = rows if rows <= 512 else 256
    assert rows % br == 0

    def body(p_ref, w_ref, m_ref, v_ref, g_ref, d_ref, nm_ref, nv_ref):
        g = p_ref[0]
        for r in range(1, N_DEV):
            g = g + p_ref[r]
        m2 = ADAM_B1 * m_ref[...] + (1.0 - ADAM_B1) * g
        v2 = ADAM_B2 * v_ref[...] + (1.0 - ADAM_B2) * jnp.square(g)
        m_hat = m2 / (1.0 - ADAM_B1 ** ADAM_STEP)
        v_hat = v2 / (1.0 - ADAM_B2 ** ADAM_STEP)
        g_ref[...] = g
        d_ref[...] = -ADAM_LR * (m_hat / (jnp.sqrt(v_hat) + ADAM_EPS) + ADAM_WD * w_ref[...])
        nm_ref[...] = m2
        nv_ref[...] = v2

    blk = pl.BlockSpec((br, cols), lambda i: (i, 0))
    shape = jax.ShapeDtypeStruct((rows, cols), F32)
    return pl.pallas_call(
        body, name=name, grid=(rows // br,),
        in_specs=[pl.BlockSpec((N_DEV, br, cols), lambda i: (0, i, 0)), blk, blk, blk],
        out_specs=(blk, blk, blk, blk), out_shape=(shape, shape, shape, shape), compiler_params=_params(1),
    )(parts, w, m, v)


_QA, _KA, _VA, _FA, _QD, _KD, _VD = (0, 512), (512, 1024), (1024, 1536), (1536, 1544), (1544, 2056), (2056, 2568), (2568, 3080)
_MAIN_ORDER = (_QA, _QD, _KA, _KD, _VA, _VD)
MAIN_COLS = 3 * D_MODEL
PROJ_COLS = MAIN_COLS + LANES


def _unshard_cols(parts):
    n, r, c = parts.shape
    return jnp.transpose(parts, (1, 0, 2)).reshape(r, n * c)


def _shard_cols(full):
    r, nc = full.shape
    return jnp.transpose(full.reshape(r, N_DEV, nc // N_DEV), (1, 0, 2))


def _w_in_to_kernel(w_full):
    main = jnp.concatenate([w_full[:, a:b] for a, b in _MAIN_ORDER], axis=1)
    forget = jnp.pad(w_full[:, _FA[0]:_FA[1]], ((0, 0), (0, LANES - N_HEADS_FOX)))
    return main, forget


def _w_in_from_kernel(g):
    pos = {span: i * W_GROUP for i, span in enumerate(_MAIN_ORDER)}
    parts = []
    for span in (_QA, _KA, _VA, _FA, _QD, _KD, _VD):
        if span == _FA:
            parts.append(g[:, MAIN_COLS:MAIN_COLS + N_HEADS_FOX])
        else:
            parts.append(g[:, pos[span]:pos[span] + W_GROUP])
    return jnp.concatenate(parts, axis=1)


def _pack_small(vals):
    rows = []
    for name, _, n_rows in SMALL_LAYOUT:
        flat = vals[name].reshape(-1).astype(F32)
        rows.append(jnp.pad(flat, (0, n_rows * LANES - flat.shape[0])).reshape(n_rows, LANES))
    packed = jnp.concatenate(rows, axis=0)
    return jnp.pad(packed, ((0, SMALL_ROWS - packed.shape[0]), (0, 0)))


def _unpack_small(packed, like):
    out = {}
    for name, row, n_rows in SMALL_LAYOUT:
        n = like[name].size
        out[name] = packed[row:row + n_rows].reshape(-1)[:n].reshape(like[name].shape)
    return out


def _device_step(x, target, small, shards):
    bsz, seq, _ = x.shape
    (g_in,) = _exchange("gather_w_in", [(shards["w_in"], False)])
    w_main, w_fa = _w_in_to_kernel(_unshard_cols(g_in))
    t = bsz * seq
    xf = x.reshape(t, D_MODEL)
    tf = target.reshape(t, D_MODEL)
    row = lambda v: v.reshape(1, -1)
    g_out = jnp.concatenate([small["g_out_fox"], small["g_out_dil"]]).reshape(1, D_MODEL)
    gains = jnp.concatenate(
        [jnp.tile(small[n].reshape(1, HEAD_DIM), (1, 2)) for n in ("g_q_fox", "g_q_dil", "g_k_fox", "g_k_dil")]
        + [jnp.zeros((4, LANES), F32)], axis=0)
    b_pad = jnp.pad(small["b_forget"].reshape(1, N_HEADS_FOX), ((0, 0), (0, LANES - N_HEADS_FOX)))
    rope = _rope_tables(seq)
    tables = _bias_tables(seq)
    w_in_all = jnp.concatenate([w_main, w_fa], axis=1)

    h1 = _rmsnorm_fwd(xf, row(small["g_mix"]), group=D_MODEL, name="norm_mix")
    proj = _matmul(h1, w_main, name="in_proj", out_dtype=BF16)
    fa = _matmul(h1, w_fa, name="in_proj_forget", out_dtype=F32)
    qk, c, ct = _qk_prep_fwd(proj, fa, b_pad, gains, rope, seq)
    late = _Exchange([(shards[n], False) for n in ("w_out", "w_gate", "w_up", "w_down")])
    o, lse, g_out_w, g_gate, g_up, g_down = _attn_fwd(qk, proj, c, ct, tables, seq, hosted=late)
    w_out = g_out_w.reshape(D_MODEL, D_MODEL)
    w_gu = jnp.concatenate([_unshard_cols(g_gate), _unshard_cols(g_up)], axis=1)
    w_down = g_down.reshape(D_FF, D_MODEL)
    on = _rmsnorm_fwd(o, g_out, group=W_GROUP, name="norm_out")
    x2 = _matmul(on, w_out, name="out_proj", out_dtype=F32, resid=xf)
    h2 = _rmsnorm_fwd(x2, row(small["g_ffn"]), group=D_MODEL, name="norm_ffn")
    au = _matmul(h2, w_gu, name="ffn_gate_up", out_dtype=BF16)
    f = _swiglu_fwd(au)
    yv = _matmul(f, w_down, name="ffn_down", out_dtype=F32, resid=x2)
    dy, dy16, loss_tile = _loss_head(yv, tf)

    df = _matmul(dy16, w_down, name="d_ffn_down", out_dtype=BF16, trans_b=True)
    dau = _swiglu_bwd(df, au)
    gw_down = _matmul_tn(f, dy16, name="gw_down")
    gw_gu = _matmul_tn(h2, dau, name="gw_gate_up")
    dh2 = _matmul(dau, w_gu, name="d_ffn_gate_up", out_dtype=F32, trans_b=True)
    dx2, dx2_16, dg_ffn = _rmsnorm_bwd(dh2, x2, row(small["g_ffn"]), group=D_MODEL, name="d_norm_ffn",
                                       out_dtypes=(F32, BF16), resid=dy)
    gw_out = _matmul_tn(on, dx2_16, name="gw_out")
    don = _matmul(dx2_16, w_out, name="d_out_proj", out_dtype=F32, trans_b=True)
    do, dg_out = _rmsnorm_bwd(don, o, g_out, group=W_GROUP, name="d_norm_out", out_dtypes=(BF16,))

    ffn_grads = _Exchange([
        (gw_out.reshape(N_DEV, D_MODEL // N_DEV, D_MODEL), True),
        (_shard_cols(gw_gu[:, :D_FF]), True),
        (_shard_cols(gw_gu[:, D_FF:]), True),
        (gw_down.reshape(N_DEV, D_FF // N_DEV, D_MODEL), True),
    ])
    dq, dk, dv, dct, dcq, p_out, p_gate, p_up, p_down = _attn_bwd(qk, proj, c, ct, tables, o, lse, do, seq,
                                                                  hosted=ffn_grads)
    dqk, dfa, dgains, db = _qk_prep_bwd(dq, dk, proj, fa, b_pad, gains, rope, dct, dcq, seq)
    dproj = jnp.concatenate([dqk, dv, dfa], axis=1)
    gw_in = _matmul_tn(h1, dproj, name="gw_in")
    in_grad = _Exchange([(_shard_cols(_w_in_from_kernel(gw_in)), True)])
    dh1, p_in = _matmul(dproj, w_in_all, name="d_in_proj", out_dtype=F32, trans_b=True, hosted=in_grad)
    dx, dg_mix = _rmsnorm_bwd(dh1, xf, row(small["g_mix"]), group=D_MODEL, name="d_norm_mix", out_dtypes=(F32,),
                              resid=dx2)

    fold = lambda rows: jnp.sum(rows[:, :HEAD_DIM] + rows[:, HEAD_DIM:], axis=0)
    half = N_PAIRS // 2
    gsmall = {
        "g_mix": dg_mix, "g_ffn": dg_ffn, "g_out_fox": dg_out[0, :W_GROUP], "g_out_dil": dg_out[0, W_GROUP:],
        "g_q_fox": fold(dgains[0:half]), "g_q_dil": fold(dgains[half:N_PAIRS]),
        "g_k_fox": fold(dgains[N_PAIRS:N_PAIRS + half]), "g_k_dil": fold(dgains[N_PAIRS + half:]),
        "b_forget": db[0, :N_HEADS_FOX],
    }
    packed = _pack_small(gsmall).at[LOSS_ROW].set(loss_tile[0])
    (p_small,) = _exchange("small_exchange", [(packed, False)])
    parts = {"w_in": p_in, "w_out": p_out, "w_gate": p_gate, "w_up": p_up, "w_down": p_down}
    return dx.reshape(x.shape), parts, p_small


def kernel(x, g_mix, w_in, b_forget, g_q_fox, g_k_fox, g_q_dil, g_k_dil, g_out_fox, g_out_dil, w_out, g_ffn, w_gate, w_up, w_down, loss_target, m_g_mix, m_w_in, m_b_forget, m_g_q_fox, m_g_k_fox, m_g_q_dil, m_g_k_dil, m_g_out_fox, m_g_out_dil, m_w_out, m_g_ffn, m_w_gate, m_w_up, m_w_down, v_g_mix, v_w_in, v_b_forget, v_g_q_fox, v_g_k_fox, v_g_q_dil, v_g_k_dil, v_g_out_fox, v_g_out_dil, v_w_out, v_g_ffn, v_w_gate, v_w_up, v_w_down):
    args = dict(locals())
    small_names = [name for name, _, _ in SMALL_LAYOUT]
    big_names = ["w_in", "w_out", "w_gate", "w_up", "w_down"]
    small = {n: args[n][0] for n in small_names}

    shards = {n: args[n][0].astype(BF16) for n in big_names}
    grad_x, parts, p_small = _device_step(x, loss_target, small, shards)

    grads, deltas, new_m, new_v = {}, {}, {}, {}
    for n in big_names:
        res = _adamw(parts[n], args[n][0], args["m_" + n][0], args["v_" + n][0], name="adamw_" + n)
        grads[n], deltas[n], new_m[n], new_v[n] = [r[None] for r in res]
    res = _adamw(p_small, _pack_small(small), _pack_small({n: args["m_" + n][0] for n in small_names}),
                 _pack_small({n: args["v_" + n][0] for n in small_names}), name="adamw_small")
    loss = res[0][LOSS_ROW, 0]
    for dst, packed_res in zip((grads, deltas, new_m, new_v), res):
        for n, val in _unpack_small(packed_res, small).items():
            dst[n] = val[None]

    order = ["g_mix", "w_in", "b_forget", "g_q_fox", "g_k_fox", "g_q_dil", "g_k_dil", "g_out_fox", "g_out_dil",
             "w_out", "g_ffn", "w_gate", "w_up", "w_down"]
    return (loss, grad_x, *[grads[n] for n in order], *[deltas[n] for n in order],
            *[new_m[n] for n in order], *[new_v[n] for n in order])
```

```python
import functools
import math

import jax
import jax.numpy as jnp
import numpy as np
from jax import lax
from jax.experimental import pallas as pl
from jax.experimental.pallas import tpu as pltpu

F32 = jnp.float32
BF16 = jnp.bfloat16

D_MODEL = 1024
HEAD_DIM = 64
LANES = 128
N_PAIRS = D_MODEL // LANES
N_HEADS = 2 * N_PAIRS
N_HEADS_FOX = 8
W_GROUP = 512
D_FF = 2816
IN_COLS = 3080
DILATION_PAIRS = ((128, 1), (512, 4), (2048, 16))
ROPE_THETA = 500000.0
ROPE_DIM = 16
ROPE_HALF = ROPE_DIM // 2
EPS = 1e-6
NEG = -1e30
LOG2E = 1.4426950408889634
LN2 = 0.6931471805599453
AUG_ONE = 0
AUG_C = 3
N_DEV = 8

ADAM_LR = 0.001
ADAM_B1 = 0.9
ADAM_B2 = 0.999
ADAM_EPS = 1e-08
ADAM_WD = 0.01
ADAM_STEP = 10

ROW_BLOCK = 512
ATT_BLOCK = 256
ATT_GROUP = 4
VMEM_LIMIT = 56 * 1024 * 1024
MATMUL_VMEM_BUDGET = 40 * 1024 * 1024

SMALL_ROWS = 32
SMALL_LAYOUT = (("g_mix", 0, 8), ("g_ffn", 8, 8), ("g_out_fox", 16, 4), ("g_out_dil", 20, 4),
                ("g_q_fox", 24, 1), ("g_k_fox", 25, 1), ("g_q_dil", 26, 1), ("g_k_dil", 27, 1),
                ("b_forget", 28, 1))
LOSS_ROW = 29


def _params(n_grid):
    return pltpu.CompilerParams(dimension_semantics=("arbitrary",) * n_grid, vmem_limit_bytes=VMEM_LIMIT)


def _divisor_block(n, cap):
    best = None
    for b in range(LANES, min(n, cap) + 1, LANES):
        if n % b == 0:
            best = b
    assert best is not None, n
    return best


def _split_dot(a, b_exact, terms):
    acc = None
    rest = a
    for _ in range(terms):
        hi = rest.astype(BF16)
        part = jnp.dot(hi, b_exact, preferred_element_type=F32)
        acc = part if acc is None else acc + part
        rest = rest - hi.astype(F32)
    return acc


def _split_dot_rhs(a_exact, b, terms):
    acc = None
    rest = b
    for _ in range(terms):
        hi = rest.astype(BF16)
        part = jnp.dot(a_exact, hi, preferred_element_type=F32)
        acc = part if acc is None else acc + part
        rest = rest - hi.astype(F32)
    return acc


def _split_dot_nt(a_exact, b, terms):
    acc = None
    rest = b
    for _ in range(terms):
        hi = rest.astype(BF16)
        part = _dot_nt(a_exact, hi)
        acc = part if acc is None else acc + part
        rest = rest - hi.astype(F32)
    return acc


def _dot_nt(a, b):
    return lax.dot_general(a, b, (((1,), (1,)), ((), ())), preferred_element_type=F32)


def _dot_tn(a, b):
    return lax.dot_general(a, b, (((0,), (0,)), ((), ())), preferred_element_type=F32)


class _Exchange:
    def __init__(self, items):
        self.items = items
        self.n = len(items)
        self.arrays = [a for a, _ in items]
        self.out_shape = [jax.ShapeDtypeStruct((N_DEV,) + tuple(a.shape[1:] if sc else a.shape), a.dtype)
                          for a, sc in items]
        self.specs = [pl.BlockSpec(memory_space=pl.ANY)] * self.n
        self.scratch = [pltpu.SemaphoreType.DMA((self.n, N_DEV - 1)), pltpu.SemaphoreType.DMA((self.n, N_DEV - 1)),
                        pltpu.SemaphoreType.DMA((self.n,))]

    def run(self, ins, outs, sems, first, last, compute):
        send_sems, recv_sems, local_sems = sems
        x, y, c = lax.axis_index("x"), lax.axis_index("y"), lax.axis_index("c")
        me = 4 * x + 2 * y + c
        local, remote = [], []
        for k, (_, scatter) in enumerate(self.items):
            own = ins[k].at[me] if scatter else ins[k]
            local.append(pltpu.make_async_copy(own, outs[k].at[me], local_sems.at[k]))
        for r in range(1, N_DEV):
            px = 1 - x if r & 4 else x
            py = 1 - y if r & 2 else y
            pc = 1 - c if r & 1 else c
            peer = 4 * px + 2 * py + pc
            for k, (_, scatter) in enumerate(self.items):
                src = ins[k].at[peer] if scatter else ins[k]
                remote.append(pltpu.make_async_remote_copy(
                    src_ref=src, dst_ref=outs[k].at[me],
                    send_sem=send_sems.at[k, r - 1], recv_sem=recv_sems.at[k, r - 1],
                    device_id=(px, py, pc), device_id_type=pl.DeviceIdType.MESH))

        def start():
            for cp in local + remote:
                cp.start()

        def finish():
            for cp in remote:
                cp.wait_recv()
            for cp in remote:
                cp.wait_send()
            for cp in local:
                cp.wait()

        if first is None:
            start()
            compute()
            finish()
        else:
            pl.when(first)(start)
            compute()
            pl.when(last)(finish)


def _grid_ends(grid):
    ids = [pl.program_id(d) for d in range(len(grid))]
    first = functools.reduce(jnp.logical_and, [i == 0 for i in ids])
    last = functools.reduce(jnp.logical_and, [i == g - 1 for i, g in zip(ids, grid)])
    return first, last


def _host(core, n_in, n_out, n_scratch, hosted, grid):
    if hosted is None:
        return core
    nh = hosted.n

    def body(*refs):
        ins, rest = refs[:n_in], refs[n_in:]
        h_ins, rest = rest[:nh], rest[nh:]
        outs, rest = rest[:n_out], rest[n_out:]
        h_outs, rest = rest[:nh], rest[nh:]
        scratch, sems = rest[:n_scratch], rest[n_scratch:]
        first, last = _grid_ends(grid)
        hosted.run(h_ins, h_outs, sems, first, last, lambda: core(*ins, *outs, *scratch))

    return body


def _hosted_parts(hosted):
    if hosted is None:
        return [], [], [], []
    return list(hosted.specs), list(hosted.out_shape), list(hosted.arrays), list(hosted.scratch)


def _exchange(name, items):
    ex = _Exchange(items)
    n = ex.n

    def body(*refs):
        ex.run(refs[:n], refs[n:2 * n], refs[2 * n:], None, None, lambda: None)

    return pl.pallas_call(
        body, name=name, out_shape=tuple(ex.out_shape), in_specs=ex.specs, out_specs=tuple(ex.specs),
        scratch_shapes=ex.scratch,
    )(*ex.arrays)


def _matmul_blocks(t, k, n, a_bytes, o_bytes):
    for bt, cap in ((1024, 1408), (1024, 512), (512, 512)):
        if t % bt:
            continue
        bn = _divisor_block(n, cap)
        if 2 * (bt * k * a_bytes + bn * k * 2 + bt * bn * o_bytes) <= MATMUL_VMEM_BUDGET:
            return bt, bn
    return ROW_BLOCK, _divisor_block(n, 256)


def _matmul(a, w, *, name, out_dtype, trans_b=False, resid=None, hosted=None):
    t, k = a.shape
    n = w.shape[0] if trans_b else w.shape[1]
    assert (w.shape[1] if trans_b else w.shape[0]) == k
    o_bytes = jnp.dtype(out_dtype).itemsize + (4 if resid is not None else 0)
    bt, bn = _matmul_blocks(t, k, n, a.dtype.itemsize, o_bytes)
    grid = (t // bt, n // bn)
    n_in = 2 + (resid is not None)

    def body(*refs):
        a_ref, w_ref, o_ref = refs[0], refs[1], refs[-1]
        av = a_ref[...].astype(BF16)
        acc = _dot_nt(av, w_ref[...]) if trans_b else jnp.dot(av, w_ref[...], preferred_element_type=F32)
        if resid is not None:
            acc = refs[2][...] + acc
        o_ref[...] = acc.astype(o_ref.dtype)

    in_specs = [pl.BlockSpec((bt, k), lambda i, j: (i, 0)),
                pl.BlockSpec((bn, k), lambda i, j: (j, 0)) if trans_b else pl.BlockSpec((k, bn), lambda i, j: (0, j))]
    args = [a, w]
    if resid is not None:
        in_specs.append(pl.BlockSpec((bt, bn), lambda i, j: (i, j)))
        args.append(resid)
    h_specs, h_shapes, h_args, h_scratch = _hosted_parts(hosted)
    res = pl.pallas_call(
        _host(body, n_in, 1, 0, hosted, grid), name=name, grid=grid, in_specs=in_specs + h_specs,
        out_specs=tuple([pl.BlockSpec((bt, bn), lambda i, j: (i, j))] + h_specs),
        out_shape=tuple([jax.ShapeDtypeStruct((t, n), out_dtype)] + h_shapes),
        scratch_shapes=h_scratch, compiler_params=_params(2),
    )(*args, *h_args)
    return res if hosted else res[0]


def _matmul_tn(a, b, *, name):
    t, m = a.shape
    n = b.shape[1]
    bt = 2048 if t % 2048 == 0 else ROW_BLOCK
    bm = _divisor_block(m, 1408)
    bn = _divisor_block(n, 1024)

    def body(a_ref, b_ref, o_ref):
        @pl.when(pl.program_id(2) == 0)
        def _():
            o_ref[...] = jnp.zeros_like(o_ref)

        o_ref[...] += _dot_tn(a_ref[...], b_ref[...])

    return pl.pallas_call(
        body, name=name, grid=(m // bm, n // bn, t // bt),
        in_specs=[pl.BlockSpec((bt, bm), lambda i, j, s: (s, i)), pl.BlockSpec((bt, bn), lambda i, j, s: (s, j))],
        out_specs=pl.BlockSpec((bm, bn), lambda i, j, s: (i, j)),
        out_shape=jax.ShapeDtypeStruct((m, n), F32), compiler_params=_params(3),
    )(a, b)


def _rmsnorm_fwd(x, g, *, group, name):
    t, w = x.shape
    bt = ROW_BLOCK

    def body(x_ref, g_ref, o_ref):
        for s in range(0, w, group):
            xs = x_ref[:, s:s + group].astype(F32)
            r = lax.rsqrt(jnp.mean(xs * xs, axis=-1, keepdims=True) + EPS)
            o_ref[:, s:s + group] = (xs * r * g_ref[:, s:s + group]).astype(o_ref.dtype)

    return pl.pallas_call(
        body, name=name, grid=(t // bt,),
        in_specs=[pl.BlockSpec((bt, w), lambda i: (i, 0)), pl.BlockSpec((1, w), lambda i: (0, 0))],
        out_specs=pl.BlockSpec((bt, w), lambda i: (i, 0)),
        out_shape=jax.ShapeDtypeStruct((t, w), BF16), compiler_params=_params(1),
    )(x, g)


def _rmsnorm_bwd(dh, x, g, *, group, name, out_dtypes, resid=None):
    t, w = x.shape
    bt = ROW_BLOCK
    n_in = 3 + (resid is not None)

    def body(*refs):
        dh_ref, x_ref, g_ref = refs[:3]
        dx_refs, dg_ref = refs[n_in:-1], refs[-1]

        @pl.when(pl.program_id(0) == 0)
        def _():
            dg_ref[...] = jnp.zeros_like(dg_ref)

        for s in range(0, w, group):
            xs = x_ref[:, s:s + group].astype(F32)
            dhs = dh_ref[:, s:s + group].astype(F32)
            r = lax.rsqrt(jnp.mean(xs * xs, axis=-1, keepdims=True) + EPS)
            xh = xs * r
            dg_ref[:, s:s + group] += jnp.sum(dhs * xh, axis=0, keepdims=True)
            dxh = dhs * g_ref[:, s:s + group]
            dx = r * (dxh - xh * jnp.mean(dxh * xh, axis=-1, keepdims=True))
            if resid is not None:
                dx = refs[3][:, s:s + group] + dx
            for dx_ref in dx_refs:
                dx_ref[:, s:s + group] = dx.astype(dx_ref.dtype)

    row = pl.BlockSpec((bt, w), lambda i: (i, 0))
    vec = pl.BlockSpec((1, w), lambda i: (0, 0))
    args = [dh, x, g] + ([resid] if resid is not None else [])
    return pl.pallas_call(
        body, name=name, grid=(t // bt,),
        in_specs=[row, row, vec] + ([row] if resid is not None else []),
        out_specs=tuple([row] * len(out_dtypes) + [vec]),
        out_shape=tuple([jax.ShapeDtypeStruct((t, w), dt) for dt in out_dtypes] + [jax.ShapeDtypeStruct((1, w), F32)]),
        compiler_params=_params(1),
    )(*args)


def _tile_plan(tile):
    is_q = tile < N_PAIRS
    is_dil = (tile % N_PAIRS) >= N_PAIRS // 2
    return is_q, is_dil, (0 if is_q else 2) + (1 if is_dil else 0)


def _segment_ones():
    lane = np.arange(LANES)
    return jnp.asarray((lane[:, None] // HEAD_DIM) == (lane[None, :] // HEAD_DIM), BF16)


def _rope_tables(seq):
    inv_freq = jnp.power(jnp.float32(ROPE_THETA), -jnp.arange(ROPE_HALF, dtype=F32) * 2.0 / ROPE_DIM)
    ang = jnp.arange(seq).astype(F32)[:, None] * inv_freq[None, :]
    cos, sin = jnp.cos(ang), jnp.sin(ang)
    ones = jnp.ones((seq, HEAD_DIM - ROPE_DIM), F32)
    zeros = jnp.zeros((seq, HEAD_DIM - ROPE_DIM), F32)
    zh = jnp.zeros((seq, ROPE_HALF), F32)
    cos_t = jnp.concatenate([cos, cos, ones], axis=1)
    sin_a = jnp.concatenate([-sin, zh, zeros], axis=1)
    sin_b = jnp.concatenate([zh, sin, zeros], axis=1)
    return tuple(jnp.tile(tab, (1, 2)) for tab in (cos_t, sin_a, sin_b))


def _log_sigmoid(z):
    return jnp.minimum(z, 0.0) - jnp.log1p(jnp.exp(-jnp.abs(z)))


def _qk_prep_fwd(proj, fa, b_pad, gains, rope, seq):
    t = proj.shape[0]
    bt = ROW_BLOCK
    nsb = seq // bt
    seg = _segment_ones()
    rr = np.arange(bt)
    tri = jnp.asarray(rr[:, None] >= rr[None, :], BF16)

    def body(p_ref, fa_ref, b_ref, g_ref, cos_ref, sa_ref, sb_ref, seg_ref, tri_ref, qk_ref, carry):
        @pl.when(pl.program_id(0) % nsb == 0)
        def _():
            carry[...] = jnp.zeros_like(carry)

        lane = lax.broadcasted_iota(jnp.int32, (bt, LANES), 1)
        logf = jnp.where(lane < N_HEADS_FOX, _log_sigmoid(fa_ref[...] + b_ref[...]), 0.0)
        cblk = _split_dot_rhs(tri_ref[...], logf, 3) + carry[0:1, :]
        carry[0:1, :] = cblk[bt - 1:bt, :]
        c_terms = []
        rest = cblk * LOG2E
        for _ in range(3):
            term = rest.astype(BF16).astype(F32)
            c_terms.append(term)
            rest = rest - term

        for tile in range(2 * N_PAIRS):
            is_q, is_dil, grow = _tile_plan(tile)
            pair = tile % N_PAIRS
            xs = p_ref[:, tile * LANES:(tile + 1) * LANES].astype(F32)
            r = lax.rsqrt(_split_dot(xs * xs, seg_ref[...], 2) * (1.0 / HEAD_DIM) + EPS)
            yv = xs * r * g_ref[grow:grow + 1, :]
            if is_dil:
                yv = (yv * cos_ref[...] + pltpu.roll(yv, LANES - ROPE_HALF, 1) * sa_ref[...]
                      + pltpu.roll(yv, ROPE_HALF, 1) * sb_ref[...])
            if is_q:
                yv = yv * (HEAD_DIM ** -0.5 * LOG2E)
            for e in range(2):
                head = 2 * pair + e
                other = HEAD_DIM * (1 - e)
                aug = jnp.zeros((bt, LANES), F32)
                if not is_dil:
                    ones_at = other + (AUG_ONE if is_q else AUG_C)
                    c_at = other + (AUG_C if is_q else AUG_ONE)
                    aug = jnp.where((lane >= ones_at) & (lane < ones_at + 3), 1.0, aug)
                    for n, term in enumerate(c_terms):
                        col = term[:, head:head + 1]
                        aug = jnp.where(lane == c_at + n, col if is_q else -col, aug)
                mine = (lane < HEAD_DIM) if e == 0 else (lane >= HEAD_DIM)
                dst = ((0 if is_q else N_HEADS) + head) * LANES
                qk_ref[:, dst:dst + LANES] = jnp.where(mine, yv, aug).astype(BF16)

    row128 = pl.BlockSpec((bt, LANES), lambda i: (i, 0))
    rope_spec = pl.BlockSpec((bt, LANES), lambda i: (i % nsb, 0))
    const = lambda shape: pl.BlockSpec(shape, lambda i: (0, 0))
    return pl.pallas_call(
        body, name="qk_prep_fwd", grid=(t // bt,),
        in_specs=[pl.BlockSpec((bt, 2 * D_MODEL), lambda i: (i, 0)), row128, const((1, LANES)), const((8, LANES)),
                  rope_spec, rope_spec, rope_spec, const((LANES, LANES)), const((bt, bt))],
        out_specs=pl.BlockSpec((bt, 2 * N_HEADS * LANES), lambda i: (i, 0)),
        out_shape=jax.ShapeDtypeStruct((t, 2 * N_HEADS * LANES), BF16),
        scratch_shapes=[pltpu.VMEM((8, LANES), F32)], compiler_params=_params(1),
    )(proj, fa, b_pad, gains, *rope, seg, tri)


def _qk_prep_bwd(dq, dk, dqx, dkx, proj, fa, b_pad, gains, rope, seq):
    t = proj.shape[0]
    bt = ROW_BLOCK
    nsb = seq // bt
    nblk = t // bt
    seg = _segment_ones()
    rr = np.arange(bt)
    triu = jnp.asarray(rr[:, None] <= rr[None, :], BF16)

    def body(dq_ref, dk_ref, dqx_ref, dkx_ref, p_ref, fa_ref, b_ref, g_ref, cos_ref, sa_ref, sb_ref, seg_ref, triu_ref,
             dp_ref, dfa_ref, dg_ref, db_ref, carry):
        step = pl.program_id(0)

        @pl.when(step == 0)
        def _():
            dg_ref[...] = jnp.zeros_like(dg_ref)
            db_ref[...] = jnp.zeros_like(db_ref)

        @pl.when(step % nsb == 0)
        def _():
            carry[...] = jnp.zeros_like(carry)

        for tile in range(2 * N_PAIRS):
            is_q, is_dil, grow = _tile_plan(tile)
            cols = slice(tile * LANES, (tile + 1) * LANES)
            src = dq_ref if is_q else dk_ref
            half = slice((tile % N_PAIRS) * LANES, (tile % N_PAIRS + 1) * LANES)
            dy = src[:, half]
            dy = dy * (HEAD_DIM ** -0.5 if is_q else LN2)
            if is_dil:
                dy = (dy * cos_ref[...] + pltpu.roll(dy * sa_ref[...], ROPE_HALF, 1)
                      + pltpu.roll(dy * sb_ref[...], LANES - ROPE_HALF, 1))
            xs = p_ref[:, cols].astype(F32)
            r = lax.rsqrt(_split_dot(xs * xs, seg_ref[...], 2) * (1.0 / HEAD_DIM) + EPS)
            xh = xs * r
            dg_ref[tile:tile + 1, :] += jnp.sum(dy * xh, axis=0, keepdims=True)
            dxh = dy * g_ref[grow:grow + 1, :]
            seg_mean = _split_dot(dxh * xh, seg_ref[...], 2) * (1.0 / HEAD_DIM)
            dp_ref[:, cols] = (r * (dxh - xh * seg_mean)).astype(BF16)

        lane = lax.broadcasted_iota(jnp.int32, (bt, LANES), 1)
        dc = jnp.zeros((bt, LANES), F32)
        for h in range(N_HEADS_FOX):
            other = (h // 2) * LANES + HEAD_DIM * (1 - h % 2)
            row_sum = dqx_ref[:, other + AUG_C:other + AUG_C + 1]
            col_sum = dkx_ref[:, other + AUG_ONE:other + AUG_ONE + 1]
            dc = jnp.where(lane == h, row_sum - col_sum, dc)
        dlogf = _split_dot_rhs(triu_ref[...], dc, 3) + carry[0:1, :]
        carry[0:1, :] = dlogf[0:1, :]
        z = fa_ref[...] + b_ref[...]
        dfa = dlogf * (1.0 / (1.0 + jnp.exp(z)))
        db_ref[0:1, :] += jnp.sum(dfa, axis=0, keepdims=True)
        dfa_ref[...] = dfa.astype(BF16)

    rev = lambda i: nblk - 1 - i
    row = lambda w: pl.BlockSpec((bt, w), lambda i: (rev(i), 0))
    rope_spec = pl.BlockSpec((bt, LANES), lambda i: (rev(i) % nsb, 0))
    const = lambda shape: pl.BlockSpec(shape, lambda i: (0, 0))
    return pl.pallas_call(
        body, name="qk_prep_bwd", grid=(nblk,),
        in_specs=[row(D_MODEL), row(D_MODEL), row(W_GROUP), row(W_GROUP), row(2 * D_MODEL), row(LANES), const((1, LANES)),
                  const((8, LANES)), rope_spec, rope_spec, rope_spec, const((LANES, LANES)), const((bt, bt))],
        out_specs=(row(2 * D_MODEL), row(LANES), const((2 * N_PAIRS, LANES)), const((8, LANES))),
        out_shape=(jax.ShapeDtypeStruct((t, 2 * D_MODEL), BF16), jax.ShapeDtypeStruct((t, LANES), BF16),
                   jax.ShapeDtypeStruct((2 * N_PAIRS, LANES), F32), jax.ShapeDtypeStruct((8, LANES), F32)),
        scratch_shapes=[pltpu.VMEM((8, LANES), F32)], compiler_params=_params(1),
    )(dq, dk, dqx, dkx, proj, fa, b_pad, gains, *rope, seg, triu)


def _bias_tables(seq, keys_first):
    nb = seq // ATT_BLOCK
    idx = jnp.arange(ATT_BLOCK)
    q_idx, k_idx = (idx[None, None, :], idx[None, :, None]) if keys_first else (idx[None, :, None], idx[None, None, :])
    dist = jnp.arange(nb)[:, None, None] * ATT_BLOCK + q_idx - k_idx
    causal = dist >= 0
    count = jnp.zeros(dist.shape, jnp.int32)
    for window, dilation in DILATION_PAIRS:
        count = count + (causal & (dist % dilation == 0) & (dist <= window)).astype(jnp.int32)
    fox = jnp.where(causal, 0.0, NEG).astype(F32)
    dil = jnp.where(count == 3, math.log2(3.0), jnp.where(count == 2, 1.0, jnp.where(count == 1, 0.0, NEG)))
    return jnp.stack([fox, dil.astype(F32)], axis=0)


def _attn_specs(seq):
    nb = seq // ATT_BLOCK
    col = lambda off: pl.BlockSpec((seq, LANES), lambda b, j: (b, off + j))
    heads = lambda off: pl.BlockSpec((seq, 2 * LANES), lambda b, j: (b, off + j))
    table_spec = pl.BlockSpec((1, nb, ATT_BLOCK, ATT_BLOCK), lambda b, j: (j // (N_PAIRS // 2), 0, 0, 0))
    return col, heads, table_spec


def _head_lanes(e, shape, axis):
    pos = lax.broadcasted_iota(jnp.int32, shape, axis)
    return pos < HEAD_DIM if e == 0 else pos >= HEAD_DIM


def _attn_fwd(qk, proj, tables, seq, hosted=None):
    t = qk.shape[0]
    nb = seq // ATT_BLOCK
    blk = ATT_BLOCK

    def body(q_ref, k_ref, v_ref, tab_ref, o_ref, lse_ref):
        mine = [_head_lanes(e, (seq, LANES), 1) for e in range(2)]
        lane = lax.broadcasted_iota(jnp.int32, (seq, LANES), 1)
        v_aug = [jnp.where(mine[e], v_ref[...], (lane == HEAD_DIM * (1 - e)).astype(BF16)) for e in range(2)]
        for i in range(nb):
            rows = slice(i * blk, (i + 1) * blk)
            n_keys = (i + 1) * blk
            out, lse = [], []
            for e in range(2):
                heads_e = slice(e * LANES, (e + 1) * LANES)
                s = _dot_nt(q_ref[rows, heads_e], k_ref[0:n_keys, heads_e])
                s = jnp.concatenate([s[:, jj * blk:(jj + 1) * blk] + tab_ref[0, i - jj] for jj in range(i + 1)], axis=1)
                m = jnp.max(s, axis=1, keepdims=True)
                acc = jnp.dot(jnp.exp2(s - m).astype(BF16), v_aug[e][0:n_keys], preferred_element_type=F32)
                ones_at = HEAD_DIM * (1 - e)
                l = acc[:, ones_at:ones_at + 1]
                out.append(acc / l)
                lse.append(m + jnp.log2(l))
            o_ref[rows, :] = jnp.where(mine[0][rows], out[0], out[1]).astype(o_ref.dtype)
            lse_ref[rows, :] = jnp.where(mine[0][rows], lse[0], lse[1])

    col, heads, table_spec = _attn_specs(seq)
    grid = (t // seq, N_PAIRS)
    h_specs, h_shapes, h_args, h_scratch = _hosted_parts(hosted)
    return pl.pallas_call(
        _host(body, 4, 2, 0, hosted, grid), name="attn_fwd", grid=grid,
        in_specs=[heads(0), heads(N_PAIRS), col(2 * N_PAIRS), table_spec] + h_specs,
        out_specs=tuple([col(0), col(0)] + h_specs),
        out_shape=tuple([jax.ShapeDtypeStruct((t, D_MODEL), BF16), jax.ShapeDtypeStruct((t, D_MODEL), F32)] + h_shapes),
        scratch_shapes=h_scratch, compiler_params=_params(2),
    )(qk, qk, proj, tables, *h_args)


def _attn_bwd(qk, proj, tables, o, lse, do, seq, hosted=None):
    t = qk.shape[0]
    nb = seq // ATT_BLOCK
    blk = ATT_BLOCK
    group = math.gcd(nb, ATT_GROUP)

    def body(q_ref, k_ref, v_ref, tab_ref, o_ref, lse_ref, do_ref,
             dq_ref, dk_ref, dv_ref, dqx_ref, dkx_ref, dk_acc, dv_acc):
        mine = [_head_lanes(e, (blk, LANES), 1) for e in range(2)]
        top = _head_lanes(0, (LANES, blk), 0)
        head_rows = lax.broadcasted_iota(jnp.int32, (8, LANES), 0)
        head_of_lane = lax.broadcasted_iota(jnp.int32, (8, LANES), 1) // HEAD_DIM
        head_sel = (head_rows == head_of_lane).astype(BF16)
        dk_acc[...] = jnp.zeros_like(dk_acc)
        dv_acc[...] = jnp.zeros_like(dv_acc)

        def block_rows(i):
            return pl.ds(pl.multiple_of(i * blk, blk), blk)

        def q_group(g, _):
            base = g * group
            qs, doe, delta, lse_e = [], [], [], []
            for b in range(group):
                rows = block_rows(base + b)
                qs.append([q_ref[rows, e * LANES:(e + 1) * LANES] for e in range(2)])
                do_blk = do_ref[rows, :]
                doe.append([jnp.where(mine[e], do_blk, jnp.zeros_like(do_blk)) for e in range(2)])
                delta_t = _split_dot_nt(head_sel, do_blk.astype(F32) * o_ref[rows, :].astype(F32), 3)
                lse_t = _split_dot_nt(head_sel, lse_ref[rows, :], 3) * (1.0 / HEAD_DIM)
                delta.append([delta_t[e:e + 1, :] for e in range(2)])
                lse_e.append([lse_t[e:e + 1, :] for e in range(2)])

            def key_block(dq_t, jj, members):
                krows = block_rows(jj)
                v = v_ref[krows, :]
                dq_t = [list(d) for d in dq_t]
                dv_part = None
                for e in range(2):
                    k_e = k_ref[krows, e * LANES:(e + 1) * LANES]
                    dk_part = None
                    for b, dist in members:
                        p_t = jnp.exp2(_dot_nt(k_e, qs[b][e]) + tab_ref[0, dist] - lse_e[b][e])
                        ds_t = (p_t * (_dot_nt(v, doe[b][e]) - delta[b][e])).astype(BF16)
                        part = jnp.dot(p_t.astype(BF16), doe[b][e], preferred_element_type=F32)
                        dv_part = part if dv_part is None else dv_part + part
                        part = jnp.dot(ds_t, qs[b][e], preferred_element_type=F32)
                        dk_part = part if dk_part is None else dk_part + part
                        dq_t[b][e] = dq_t[b][e] + _dot_tn(k_e, ds_t)
                    dk_acc[e, krows, :] += dk_part
                dv_acc[krows, :] += dv_part
                return tuple(tuple(d) for d in dq_t)

            zacc = jnp.zeros((LANES, blk), F32)
            dq_t = tuple((zacc, zacc) for _ in range(group))
            dq_t = lax.fori_loop(
                0, base, lambda jj, st: key_block(st, jj, [(b, base + b - jj) for b in range(group)]), dq_t)
            for a in range(group):
                dq_t = key_block(dq_t, base + a, [(b, b - a) for b in range(a, group)])
            for b in range(group):
                rows = block_rows(base + b)
                dq_ref[rows, :] = jnp.where(top, dq_t[b][0], dq_t[b][1]).T
                dqx_ref[rows, :] = jnp.where(top, dq_t[b][1], dq_t[b][0]).T
            return 0

        lax.fori_loop(0, nb // group, q_group, 0)
        lo = _head_lanes(0, (seq, LANES), 1)
        dk_ref[...] = jnp.where(lo, dk_acc[0], dk_acc[1])
        dkx_ref[...] = jnp.where(lo, dk_acc[1], dk_acc[0])
        dv_ref[...] = dv_acc[...].astype(dv_ref.dtype)

    col, heads, table_spec = _attn_specs(seq)
    grid = (t // seq, N_PAIRS)
    h_specs, h_shapes, h_args, h_scratch = _hosted_parts(hosted)
    f32_out = jax.ShapeDtypeStruct((t, D_MODEL), F32)
    return pl.pallas_call(
        _host(body, 7, 5, 2, hosted, grid), name="attn_bwd", grid=grid,
        in_specs=[heads(0), heads(N_PAIRS), col(2 * N_PAIRS), table_spec, col(0), col(0), col(0)] + h_specs,
        out_specs=tuple([col(0)] * 5 + h_specs),
        out_shape=tuple([f32_out, f32_out, jax.ShapeDtypeStruct((t, D_MODEL), BF16), f32_out, f32_out] + h_shapes),
        scratch_shapes=[pltpu.VMEM((2, seq, LANES), F32), pltpu.VMEM((seq, LANES), F32)] + h_scratch,
        compiler_params=_params(2),
    )(qk, qk, proj, tables, o, lse, do, *h_args)


def _swiglu_fwd(au):
    t = au.shape[0]
    bt = ROW_BLOCK

    def body(a_ref, u_ref, f_ref):
        a = a_ref[...].astype(F32)
        f_ref[...] = (a * jax.nn.sigmoid(a) * u_ref[...].astype(F32)).astype(BF16)

    return pl.pallas_call(
        body, name="swiglu_fwd", grid=(t // bt,),
        in_specs=[pl.BlockSpec((bt, D_FF), lambda i: (i, 0)), pl.BlockSpec((bt, D_FF), lambda i: (i, 1))],
        out_specs=pl.BlockSpec((bt, D_FF), lambda i: (i, 0)),
        out_shape=jax.ShapeDtypeStruct((t, D_FF), BF16), compiler_params=_params(1),
    )(au, au)


def _swiglu_bwd(df, au):
    t = au.shape[0]
    bt = ROW_BLOCK

    def body(df_ref, au_ref, o_ref):
        a = au_ref[:, 0:D_FF].astype(F32)
        u = au_ref[:, D_FF:2 * D_FF].astype(F32)
        dfv = df_ref[...].astype(F32)
        sg = jax.nn.sigmoid(a)
        o_ref[:, 0:D_FF] = (dfv * u * sg * (1.0 + a * (1.0 - sg))).astype(BF16)
        o_ref[:, D_FF:2 * D_FF] = (dfv * a * sg).astype(BF16)

    return pl.pallas_call(
        body, name="swiglu_bwd", grid=(t // bt,),
        in_specs=[pl.BlockSpec((bt, D_FF), lambda i: (i, 0)), pl.BlockSpec((bt, 2 * D_FF), lambda i: (i, 0))],
        out_specs=pl.BlockSpec((bt, 2 * D_FF), lambda i: (i, 0)),
        out_shape=jax.ShapeDtypeStruct((t, 2 * D_FF), BF16), compiler_params=_params(1),
    )(df, au)


def _loss_head(yv, target):
    t, w = yv.shape
    bt = ROW_BLOCK

    def body(y_ref, t_ref, dy_ref, dy16_ref, loss_ref):
        @pl.when(pl.program_id(0) == 0)
        def _():
            loss_ref[...] = jnp.zeros_like(loss_ref)

        err = y_ref[...] - t_ref[...]
        dy = err * (1.0 / w)
        dy_ref[...] = dy
        dy16_ref[...] = dy.astype(BF16)
        loss_ref[...] += 0.5 * jnp.sum(jnp.mean(err * err, axis=-1, keepdims=True), axis=0, keepdims=True)

    row = pl.BlockSpec((bt, w), lambda i: (i, 0))
    return pl.pallas_call(
        body, name="loss_head", grid=(t // bt,), in_specs=[row, row],
        out_specs=(row, row, pl.BlockSpec((8, LANES), lambda i: (0, 0))),
        out_shape=(jax.ShapeDtypeStruct((t, w), F32), jax.ShapeDtypeStruct((t, w), BF16),
                   jax.ShapeDtypeStruct((8, LANES), F32)),
        compiler_params=_params(1),
    )(yv, target)


def _adamw(parts, w, m, v, *, name):
    rows, cols = w.shape
    br = rows if rows <= 512 else 256
    assert rows % br == 0

    def body(p_ref, w_ref, m_ref, v_ref, g_ref, d_ref, nm_ref, nv_ref):
        g = p_ref[0]
        for r in range(1, N_DEV):
            g = g + p_ref[r]
        m2 = ADAM_B1 * m_ref[...] + (1.0 - ADAM_B1) * g
        v2 = ADAM_B2 * v_ref[...] + (1.0 - ADAM_B2) * jnp.square(g)
        m_hat = m2 / (1.0 - ADAM_B1 ** ADAM_STEP)
        v_hat = v2 / (1.0 - ADAM_B2 ** ADAM_STEP)
        g_ref[...] = g
        d_ref[...] = -ADAM_LR * (m_hat / (jnp.sqrt(v_hat) + ADAM_EPS) + ADAM_WD * w_ref[...])
        nm_ref[...] = m2
        nv_ref[...] = v2

    blk = pl.BlockSpec((br, cols), lambda i: (i, 0))
    shape = jax.ShapeDtypeStruct((rows, cols), F32)
    return pl.pallas_call(
        body, name=name, grid=(rows // br,),
        in_specs=[pl.BlockSpec((N_DEV, br, cols), lambda i: (0, i, 0)), blk, blk, blk],
        out_specs=(blk, blk, blk, blk), out_shape=(shape, shape, shape, shape), compiler_params=_params(1),
    )(parts, w, m, v)


_QA, _KA, _VA, _FA, _QD, _KD, _VD = (0, 512), (512, 1024), (1024, 1536), (1536, 1544), (1544, 2056), (2056, 2568), (2568, 3080)
_MAIN_ORDER = (_QA, _QD, _KA, _KD, _VA, _VD)
MAIN_COLS = 3 * D_MODEL
PROJ_COLS = MAIN_COLS + LANES


def _unshard_cols(parts):
    n, r, c = parts.shape
    return jnp.transpose(parts, (1, 0, 2)).reshape(r, n * c)


def _shard_cols(full):
    r, nc = full.shape
    return jnp.transpose(full.reshape(r, N_DEV, nc // N_DEV), (1, 0, 2))


def _w_in_to_kernel(w_full):
    main = jnp.concatenate([w_full[:, a:b] for a, b in _MAIN_ORDER], axis=1)
    forget = jnp.pad(w_full[:, _FA[0]:_FA[1]], ((0, 0), (0, LANES - N_HEADS_FOX)))
    return main, forget


def _w_in_from_kernel(g):
    pos = {span: i * W_GROUP for i, span in enumerate(_MAIN_ORDER)}
    parts = []
    for span in (_QA, _KA, _VA, _FA, _QD, _KD, _VD):
        if span == _FA:
            parts.append(g[:, MAIN_COLS:MAIN_COLS + N_HEADS_FOX])
        else:
            parts.append(g[:, pos[span]:pos[span] + W_GROUP])
    return jnp.concatenate(parts, axis=1)


def _pack_small(vals):
    rows = []
    for name, _, n_rows in SMALL_LAYOUT:
        flat = vals[name].reshape(-1).astype(F32)
        rows.append(jnp.pad(flat, (0, n_rows * LANES - flat.shape[0])).reshape(n_rows, LANES))
    packed = jnp.concatenate(rows, axis=0)
    return jnp.pad(packed, ((0, SMALL_ROWS - packed.shape[0]), (0, 0)))


def _unpack_small(packed, like):
    out = {}
    for name, row, n_rows in SMALL_LAYOUT:
        n = like[name].size
        out[name] = packed[row:row + n_rows].reshape(-1)[:n].reshape(like[name].shape)
    return out


def _device_step(x, target, small, shards):
    bsz, seq, _ = x.shape
    (g_in,) = _exchange("gather_w_in", [(shards["w_in"], False)])
    w_main, w_fa = _w_in_to_kernel(_unshard_cols(g_in))
    t = bsz * seq
    xf = x.reshape(t, D_MODEL)
    tf = target.reshape(t, D_MODEL)
    row = lambda v: v.reshape(1, -1)
    g_out = jnp.concatenate([small["g_out_fox"], small["g_out_dil"]]).reshape(1, D_MODEL)
    gains = jnp.concatenate(
        [jnp.tile(small[n].reshape(1, HEAD_DIM), (1, 2)) for n in ("g_q_fox", "g_q_dil", "g_k_fox", "g_k_dil")]
        + [jnp.zeros((4, LANES), F32)], axis=0)
    b_pad = jnp.pad(small["b_forget"].reshape(1, N_HEADS_FOX), ((0, 0), (0, LANES - N_HEADS_FOX)))
    rope = _rope_tables(seq)
    tables_qk = _bias_tables(seq, keys_first=False)
    tables_kq = _bias_tables(seq, keys_first=True)
    w_in_all = jnp.concatenate([w_main, w_fa], axis=1)

    h1 = _rmsnorm_fwd(xf, row(small["g_mix"]), group=D_MODEL, name="norm_mix")
    proj = _matmul(h1, w_main, name="in_proj", out_dtype=BF16)
    fa = _matmul(h1, w_fa, name="in_proj_forget", out_dtype=F32)
    qk = _qk_prep_fwd(proj, fa, b_pad, gains, rope, seq)
    late = _Exchange([(shards[n], False) for n in ("w_out", "w_gate", "w_up", "w_down")])
    o, lse, g_out_w, g_gate, g_up, g_down = _attn_fwd(qk, proj, tables_qk, seq, hosted=late)
    w_out = g_out_w.reshape(D_MODEL, D_MODEL)
    w_gu = jnp.concatenate([_unshard_cols(g_gate), _unshard_cols(g_up)], axis=1)
    w_down = g_down.reshape(D_FF, D_MODEL)
    on = _rmsnorm_fwd(o, g_out, group=W_GROUP, name="norm_out")
    x2 = _matmul(on, w_out, name="out_proj", out_dtype=F32, resid=xf)
    h2 = _rmsnorm_fwd(x2, row(small["g_ffn"]), group=D_MODEL, name="norm_ffn")
    au = _matmul(h2, w_gu, name="ffn_gate_up", out_dtype=BF16)
    f = _swiglu_fwd(au)
    yv = _matmul(f, w_down, name="ffn_down", out_dtype=F32, resid=x2)
    dy, dy16, loss_tile = _loss_head(yv, tf)

    df = _matmul(dy16, w_down, name="d_ffn_down", out_dtype=BF16, trans_b=True)
    dau = _swiglu_bwd(df, au)
    gw_down = _matmul_tn(f, dy16, name="gw_down")
    gw_gu = _matmul_tn(h2, dau, name="gw_gate_up")
    dh2 = _matmul(dau, w_gu, name="d_ffn_gate_up", out_dtype=F32, trans_b=True)
    dx2, dx2_16, dg_ffn = _rmsnorm_bwd(dh2, x2, row(small["g_ffn"]), group=D_MODEL, name="d_norm_ffn",
                                       out_dtypes=(F32, BF16), resid=dy)
    gw_out = _matmul_tn(on, dx2_16, name="gw_out")
    don = _matmul(dx2_16, w_out, name="d_out_proj", out_dtype=F32, trans_b=True)
    do, dg_out = _rmsnorm_bwd(don, o, g_out, group=W_GROUP, name="d_norm_out", out_dtypes=(BF16,))

    ffn_grads = _Exchange([
        (gw_out.reshape(N_DEV, D_MODEL // N_DEV, D_MODEL), True),
        (_shard_cols(gw_gu[:, :D_FF]), True),
        (_shard_cols(gw_gu[:, D_FF:]), True),
        (gw_down.reshape(N_DEV, D_FF // N_DEV, D_MODEL), True),
    ])
    dq, dk, dv, dqx, dkx, p_out, p_gate, p_up, p_down = _attn_bwd(qk, proj, tables_kq, o, lse, do, seq, hosted=ffn_grads)
    dqk, dfa, dgains, db = _qk_prep_bwd(dq, dk, dqx, dkx, proj, fa, b_pad, gains, rope, seq)
    dproj = jnp.concatenate([dqk, dv, dfa], axis=1)
    gw_in = _matmul_tn(h1, dproj, name="gw_in")
    in_grad = _Exchange([(_shard_cols(_w_in_from_kernel(gw_in)), True)])
    dh1, p_in = _matmul(dproj, w_in_all, name="d_in_proj", out_dtype=F32, trans_b=True, hosted=in_grad)
    dx, dg_mix = _rmsnorm_bwd(dh1, xf, row(small["g_mix"]), group=D_MODEL, name="d_norm_mix", out_dtypes=(F32,),
                              resid=dx2)

    fold = lambda rows: jnp.sum(rows[:, :HEAD_DIM] + rows[:, HEAD_DIM:], axis=0)
    half = N_PAIRS // 2
    gsmall = {
        "g_mix": dg_mix, "g_ffn": dg_ffn, "g_out_fox": dg_out[0, :W_GROUP], "g_out_dil": dg_out[0, W_GROUP:],
        "g_q_fox": fold(dgains[0:half]), "g_q_dil": fold(dgains[half:N_PAIRS]),
        "g_k_fox": fold(dgains[N_PAIRS:N_PAIRS + half]), "g_k_dil": fold(dgains[N_PAIRS + half:]),
        "b_forget": db[0, :N_HEADS_FOX],
    }
    packed = _pack_small(gsmall).at[LOSS_ROW].set(loss_tile[0])
    (p_small,) = _exchange("small_exchange", [(packed, False)])
    parts = {"w_in": p_in, "w_out": p_out, "w_gate": p_gate, "w_up": p_up, "w_down": p_down}
    return dx.reshape(x.shape), parts, p_small


def kernel(x, g_mix, w_in, b_forget, g_q_fox, g_k_fox, g_q_dil, g_k_dil, g_out_fox, g_out_dil, w_out, g_ffn, w_gate, w_up, w_down, loss_target, m_g_mix, m_w_in, m_b_forget, m_g_q_fox, m_g_k_fox, m_g_q_dil, m_g_k_dil, m_g_out_fox, m_g_out_dil, m_w_out, m_g_ffn, m_w_gate, m_w_up, m_w_down, v_g_mix, v_w_in, v_b_forget, v_g_q_fox, v_g_k_fox, v_g_q_dil, v_g_k_dil, v_g_out_fox, v_g_out_dil, v_w_out, v_g_ffn, v_w_gate, v_w_up, v_w_down):
    args = dict(locals())
    small_names = [name for name, _, _ in SMALL_LAYOUT]
    big_names = ["w_in", "w_out", "w_gate", "w_up", "w_down"]
    small = {n: args[n][0] for n in small_names}

    shards = {n: args[n][0].astype(BF16) for n in big_names}
    grad_x, parts, p_small = _device_step(x, loss_target, small, shards)

    grads, deltas, new_m, new_v = {}, {}, {}, {}
    for n in big_names:
        res = _adamw(parts[n], args[n][0], args["m_" + n][0], args["v_" + n][0], name="adamw_" + n)
        grads[n], deltas[n], new_m[n], new_v[n] = [r[None] for r in res]
    res = _adamw(p_small, _pack_small(small), _pack_small({n: args["m_" + n][0] for n in small_names}),
                 _pack_small({n: args["v_" + n][0] for n in small_names}), name="adamw_small")
    loss = res[0][LOSS_ROW, 0]
    for dst, packed_res in zip((grads, deltas, new_m, new_v), res):
        for n, val in _unpack_small(packed_res, small).items():
            dst[n] = val[None]

    order = ["g_mix", "w_in", "b_forget", "g_q_fox", "g_k_fox", "g_q_dil", "g_k_dil", "g_out_fox", "g_out_dil",
             "w_out", "g_ffn", "w_gate", "w_up", "w_down"]
    return (loss, grad_x, *[grads[n] for n in order], *[deltas[n] for n in order],
            *[new_m[n] for n in order], *[new_v[n] for n in order])
```

```python
import functools
import math

import jax
import jax.numpy as jnp
import numpy as np
from jax import lax
from jax.experimental import pallas as pl
from jax.experimental.pallas import tpu as pltpu

F32 = jnp.float32
BF16 = jnp.bfloat16

D_MODEL = 1024
HEAD_DIM = 64
LANES = 128
N_PAIRS = D_MODEL // LANES
N_HEADS = 2 * N_PAIRS
N_HEADS_FOX = 8
W_GROUP = 512
D_FF = 2816
IN_COLS = 3080
DILATION_PAIRS = ((128, 1), (512, 4), (2048, 16))
ROPE_THETA = 500000.0
ROPE_DIM = 16
ROPE_HALF = ROPE_DIM // 2
EPS = 1e-6
NEG = -1e30
LOG2E = 1.4426950408889634
LN2 = 0.6931471805599453
AUG_ONE = 0
AUG_C = 3
N_DEV = 8

ADAM_LR = 0.001
ADAM_B1 = 0.9
ADAM_B2 = 0.999
ADAM_EPS = 1e-08
ADAM_WD = 0.01
ADAM_STEP = 10

ROW_BLOCK = 512
ATT_BLOCK = 256
ATT_GROUP = 4
VMEM_LIMIT = 56 * 1024 * 1024
MATMUL_VMEM_BUDGET = 40 * 1024 * 1024

SMALL_ROWS = 32
SMALL_LAYOUT = (("g_mix", 0, 8), ("g_ffn", 8, 8), ("g_out_fox", 16, 4), ("g_out_dil", 20, 4),
                ("g_q_fox", 24, 1), ("g_k_fox", 25, 1), ("g_q_dil", 26, 1), ("g_k_dil", 27, 1),
                ("b_forget", 28, 1))
LOSS_ROW = 29


def _params(n_grid):
    return pltpu.CompilerParams(dimension_semantics=("arbitrary",) * n_grid, vmem_limit_bytes=VMEM_LIMIT)


def _divisor_block(n, cap):
    best = None
    for b in range(LANES, min(n, cap) + 1, LANES):
        if n % b == 0:
            best = b
    assert best is not None, n
    return best


def _split_dot(a, b_exact, terms):
    acc = None
    rest = a
    for _ in range(terms):
        hi = rest.astype(BF16)
        part = jnp.dot(hi, b_exact, preferred_element_type=F32)
        acc = part if acc is None else acc + part
        rest = rest - hi.astype(F32)
    return acc


def _split_dot_rhs(a_exact, b, terms):
    acc = None
    rest = b
    for _ in range(terms):
        hi = rest.astype(BF16)
        part = jnp.dot(a_exact, hi, preferred_element_type=F32)
        acc = part if acc is None else acc + part
        rest = rest - hi.astype(F32)
    return acc


def _split_dot_nt(a_exact, b, terms):
    acc = None
    rest = b
    for _ in range(terms):
        hi = rest.astype(BF16)
        part = _dot_nt(a_exact, hi)
        acc = part if acc is None else acc + part
        rest = rest - hi.astype(F32)
    return acc


def _dot_nt(a, b):
    return lax.dot_general(a, b, (((1,), (1,)), ((), ())), preferred_element_type=F32)


def _dot_tn(a, b):
    return lax.dot_general(a, b, (((0,), (0,)), ((), ())), preferred_element_type=F32)


class _Exchange:
    def __init__(self, items):
        self.items = items
        self.n = len(items)
        self.arrays = [a for a, _ in items]
        self.out_shape = [jax.ShapeDtypeStruct((N_DEV,) + tuple(a.shape[1:] if sc else a.shape), a.dtype)
                          for a, sc in items]
        self.specs = [pl.BlockSpec(memory_space=pl.ANY)] * self.n
        self.scratch = [pltpu.SemaphoreType.DMA((self.n, N_DEV - 1)), pltpu.SemaphoreType.DMA((self.n, N_DEV - 1)),
                        pltpu.SemaphoreType.DMA((self.n,))]

    def run(self, ins, outs, sems, first, last, compute):
        send_sems, recv_sems, local_sems = sems
        x, y, c = lax.axis_index("x"), lax.axis_index("y"), lax.axis_index("c")
        me = 4 * x + 2 * y + c
        local, remote = [], []
        for k, (_, scatter) in enumerate(self.items):
            own = ins[k].at[me] if scatter else ins[k]
            local.append(pltpu.make_async_copy(own, outs[k].at[me], local_sems.at[k]))
        for r in range(1, N_DEV):
            px = 1 - x if r & 4 else x
            py = 1 - y if r & 2 else y
            pc = 1 - c if r & 1 else c
            peer = 4 * px + 2 * py + pc
            for k, (_, scatter) in enumerate(self.items):
                src = ins[k].at[peer] if scatter else ins[k]
                remote.append(pltpu.make_async_remote_copy(
                    src_ref=src, dst_ref=outs[k].at[me],
                    send_sem=send_sems.at[k, r - 1], recv_sem=recv_sems.at[k, r - 1],
                    device_id=(px, py, pc), device_id_type=pl.DeviceIdType.MESH))

        def start():
            for cp in local + remote:
                cp.start()

        def finish():
            for cp in remote:
                cp.wait_recv()
            for cp in remote:
                cp.wait_send()
            for cp in local:
                cp.wait()

        if first is None:
            start()
            compute()
            finish()
        else:
            pl.when(first)(start)
            compute()
            pl.when(last)(finish)


def _grid_ends(grid):
    ids = [pl.program_id(d) for d in range(len(grid))]
    first = functools.reduce(jnp.logical_and, [i == 0 for i in ids])
    last = functools.reduce(jnp.logical_and, [i == g - 1 for i, g in zip(ids, grid)])
    return first, last


def _host(core, n_in, n_out, n_scratch, hosted, grid):
    if hosted is None:
        return core
    nh = hosted.n

    def body(*refs):
        ins, rest = refs[:n_in], refs[n_in:]
        h_ins, rest = rest[:nh], rest[nh:]
        outs, rest = rest[:n_out], rest[n_out:]
        h_outs, rest = rest[:nh], rest[nh:]
        scratch, sems = rest[:n_scratch], rest[n_scratch:]
        first, last = _grid_ends(grid)
        hosted.run(h_ins, h_outs, sems, first, last, lambda: core(*ins, *outs, *scratch))

    return body


def _hosted_parts(hosted):
    if hosted is None:
        return [], [], [], []
    return list(hosted.specs), list(hosted.out_shape), list(hosted.arrays), list(hosted.scratch)


def _exchange(name, items):
    ex = _Exchange(items)
    n = ex.n

    def body(*refs):
        ex.run(refs[:n], refs[n:2 * n], refs[2 * n:], None, None, lambda: None)

    return pl.pallas_call(
        body, name=name, out_shape=tuple(ex.out_shape), in_specs=ex.specs, out_specs=tuple(ex.specs),
        scratch_shapes=ex.scratch,
    )(*ex.arrays)


def _matmul_blocks(t, k, n, a_bytes, o_bytes):
    for bt, cap in ((1024, 1408), (1024, 512), (512, 512)):
        if t % bt:
            continue
        bn = _divisor_block(n, cap)
        if 2 * (bt * k * a_bytes + bn * k * 2 + bt * bn * o_bytes) <= MATMUL_VMEM_BUDGET:
            return bt, bn
    return ROW_BLOCK, _divisor_block(n, 256)


def _matmul(a, w, *, name, out_dtype):
    t, k = a.shape
    n = w.shape[1]
    assert w.shape[0] == k
    bt, bn = _matmul_blocks(t, k, n, a.dtype.itemsize, jnp.dtype(out_dtype).itemsize)

    def body(a_ref, w_ref, o_ref):
        o_ref[...] = jnp.dot(a_ref[...], w_ref[...], preferred_element_type=F32).astype(o_ref.dtype)

    return pl.pallas_call(
        body, name=name, grid=(t // bt, n // bn),
        in_specs=[pl.BlockSpec((bt, k), lambda i, j: (i, 0)), pl.BlockSpec((k, bn), lambda i, j: (0, j))],
        out_specs=pl.BlockSpec((bt, bn), lambda i, j: (i, j)),
        out_shape=jax.ShapeDtypeStruct((t, n), out_dtype), compiler_params=_params(2),
    )(a, w)


def _matmul_tn(a, b, *, name):
    t, m = a.shape
    n = b.shape[1]
    bt = 2048 if t % 2048 == 0 else ROW_BLOCK
    bm = _divisor_block(m, 1408)
    bn = _divisor_block(n, 1024)

    def body(a_ref, b_ref, o_ref):
        @pl.when(pl.program_id(2) == 0)
        def _():
            o_ref[...] = jnp.zeros_like(o_ref)

        o_ref[...] += _dot_tn(a_ref[...], b_ref[...])

    return pl.pallas_call(
        body, name=name, grid=(m // bm, n // bn, t // bt),
        in_specs=[pl.BlockSpec((bt, bm), lambda i, j, s: (s, i)), pl.BlockSpec((bt, bn), lambda i, j, s: (s, j))],
        out_specs=pl.BlockSpec((bm, bn), lambda i, j, s: (i, j)),
        out_shape=jax.ShapeDtypeStruct((m, n), F32), compiler_params=_params(3),
    )(a, b)


def _rmsnorm_fwd(x, g, *, group, name, hosted=None):
    t, w = x.shape
    bt = ROW_BLOCK

    def body(x_ref, g_ref, o_ref):
        for s in range(0, w, group):
            xs = x_ref[:, s:s + group].astype(F32)
            r = lax.rsqrt(jnp.mean(xs * xs, axis=-1, keepdims=True) + EPS)
            o_ref[:, s:s + group] = (xs * r * g_ref[:, s:s + group]).astype(o_ref.dtype)

    grid = (t // bt,)
    h_specs, h_shapes, h_args, h_scratch = _hosted_parts(hosted)
    res = pl.pallas_call(
        _host(body, 2, 1, 0, hosted, grid), name=name, grid=grid,
        in_specs=[pl.BlockSpec((bt, w), lambda i: (i, 0)), pl.BlockSpec((1, w), lambda i: (0, 0))] + h_specs,
        out_specs=tuple([pl.BlockSpec((bt, w), lambda i: (i, 0))] + h_specs),
        out_shape=tuple([jax.ShapeDtypeStruct((t, w), BF16)] + h_shapes),
        scratch_shapes=h_scratch, compiler_params=_params(1),
    )(x, g, *h_args)
    return res if hosted else res[0]


def _norm_input_grad(terms, x, g, *, group, name, out_dtypes, resid=None, hosted=None):
    t, w = x.shape
    n_terms = len(terms)
    per_row = sum(a.shape[1] * a.dtype.itemsize for a, _, _ in terms) + w * (x.dtype.itemsize + 4 * (resid is not None))
    per_row += w * (sum(jnp.dtype(dt).itemsize for dt in out_dtypes) + 16)
    fixed = 2 * sum(wt.shape[0] * a.shape[1] * 2 for a, wt, _ in terms)
    bt = next(b for b in (1024, 512, 256, 128) if t % b == 0 and fixed + 2 * b * per_row <= MATMUL_VMEM_BUDGET)
    n_in = 2 * n_terms + 2 + (resid is not None)
    grid = (t // bt,)

    def body(*refs):
        x_ref, g_ref = refs[2 * n_terms], refs[2 * n_terms + 1]
        dx_refs, dg_ref = refs[n_in:-1], refs[-1]

        @pl.when(pl.program_id(0) == 0)
        def _():
            dg_ref[...] = jnp.zeros_like(dg_ref)

        dh = None
        for k in range(n_terms):
            part = _dot_nt(refs[2 * k][...], refs[2 * k + 1][...])
            dh = part if dh is None else dh + part
        for s in range(0, w, group):
            xs = x_ref[:, s:s + group].astype(F32)
            dhs = dh[:, s:s + group]
            r = lax.rsqrt(jnp.mean(xs * xs, axis=-1, keepdims=True) + EPS)
            xh = xs * r
            dg_ref[:, s:s + group] += jnp.sum(dhs * xh, axis=0, keepdims=True)
            dxh = dhs * g_ref[:, s:s + group]
            dx = r * (dxh - xh * jnp.mean(dxh * xh, axis=-1, keepdims=True))
            if resid is not None:
                dx = refs[n_in - 1][:, s:s + group] + dx
            for dx_ref in dx_refs:
                dx_ref[:, s:s + group] = dx.astype(dx_ref.dtype)

    row = lambda width: pl.BlockSpec((bt, width), lambda i: (i, 0))
    vec = pl.BlockSpec((1, w), lambda i: (0, 0))
    in_specs, args = [], []
    for a, wt, col_block in terms:
        in_specs += [row(a.shape[1]), pl.BlockSpec((wt.shape[0], a.shape[1]), lambda i, cb=col_block: (0, cb))]
        args += [a, wt]
    in_specs += [row(w), vec] + ([row(w)] if resid is not None else [])
    args += [x, g] + ([resid] if resid is not None else [])
    h_specs, h_shapes, h_args, h_scratch = _hosted_parts(hosted)
    return pl.pallas_call(
        _host(body, n_in, len(out_dtypes) + 1, 0, hosted, grid), name=name, grid=grid, in_specs=in_specs + h_specs,
        out_specs=tuple([row(w)] * len(out_dtypes) + [vec] + h_specs),
        out_shape=tuple([jax.ShapeDtypeStruct((t, w), dt) for dt in out_dtypes] + [jax.ShapeDtypeStruct((1, w), F32)]
                        + h_shapes),
        scratch_shapes=h_scratch, compiler_params=_params(1),
    )(*args, *h_args)


def _tile_plan(tile):
    is_q = tile < N_PAIRS
    is_dil = (tile % N_PAIRS) >= N_PAIRS // 2
    return is_q, is_dil, (0 if is_q else 2) + (1 if is_dil else 0)


def _segment_ones():
    lane = np.arange(LANES)
    return jnp.asarray((lane[:, None] // HEAD_DIM) == (lane[None, :] // HEAD_DIM), BF16)


def _rope_tables(seq):
    inv_freq = jnp.power(jnp.float32(ROPE_THETA), -jnp.arange(ROPE_HALF, dtype=F32) * 2.0 / ROPE_DIM)
    ang = jnp.arange(seq).astype(F32)[:, None] * inv_freq[None, :]
    cos, sin = jnp.cos(ang), jnp.sin(ang)
    ones = jnp.ones((seq, HEAD_DIM - ROPE_DIM), F32)
    zeros = jnp.zeros((seq, HEAD_DIM - ROPE_DIM), F32)
    zh = jnp.zeros((seq, ROPE_HALF), F32)
    cos_t = jnp.concatenate([cos, cos, ones], axis=1)
    sin_a = jnp.concatenate([-sin, zh, zeros], axis=1)
    sin_b = jnp.concatenate([zh, sin, zeros], axis=1)
    return tuple(jnp.tile(tab, (1, 2)) for tab in (cos_t, sin_a, sin_b))


def _log_sigmoid(z):
    return jnp.minimum(z, 0.0) - jnp.log1p(jnp.exp(-jnp.abs(z)))


def _qk_prep_fwd(proj, fa, b_pad, gains, rope, seq):
    t = proj.shape[0]
    bt = ROW_BLOCK
    nsb = seq // bt
    seg = _segment_ones()
    rr = np.arange(bt)
    tri = jnp.asarray(rr[:, None] >= rr[None, :], BF16)

    def body(p_ref, fa_ref, b_ref, g_ref, cos_ref, sa_ref, sb_ref, seg_ref, tri_ref, qk_ref, carry):
        @pl.when(pl.program_id(0) % nsb == 0)
        def _():
            carry[...] = jnp.zeros_like(carry)

        lane = lax.broadcasted_iota(jnp.int32, (bt, LANES), 1)
        logf = jnp.where(lane < N_HEADS_FOX, _log_sigmoid(fa_ref[...] + b_ref[...]), 0.0)
        cblk = _split_dot_rhs(tri_ref[...], logf, 3) + carry[0:1, :]
        carry[0:1, :] = cblk[bt - 1:bt, :]
        c_terms = []
        rest = cblk * LOG2E
        for _ in range(3):
            term = rest.astype(BF16).astype(F32)
            c_terms.append(term)
            rest = rest - term

        for tile in range(2 * N_PAIRS):
            is_q, is_dil, grow = _tile_plan(tile)
            pair = tile % N_PAIRS
            xs = p_ref[:, tile * LANES:(tile + 1) * LANES].astype(F32)
            r = lax.rsqrt(_split_dot(xs * xs, seg_ref[...], 2) * (1.0 / HEAD_DIM) + EPS)
            yv = xs * r * g_ref[grow:grow + 1, :]
            if is_dil:
                yv = (yv * cos_ref[...] + pltpu.roll(yv, LANES - ROPE_HALF, 1) * sa_ref[...]
                      + pltpu.roll(yv, ROPE_HALF, 1) * sb_ref[...])
            if is_q:
                yv = yv * (HEAD_DIM ** -0.5 * LOG2E)
            for e in range(2):
                head = 2 * pair + e
                other = HEAD_DIM * (1 - e)
                aug = jnp.zeros((bt, LANES), F32)
                if not is_dil:
                    ones_at = other + (AUG_ONE if is_q else AUG_C)
                    c_at = other + (AUG_C if is_q else AUG_ONE)
                    aug = jnp.where((lane >= ones_at) & (lane < ones_at + 3), 1.0, aug)
                    for n, term in enumerate(c_terms):
                        col = term[:, head:head + 1]
                        aug = jnp.where(lane == c_at + n, col if is_q else -col, aug)
                mine = (lane < HEAD_DIM) if e == 0 else (lane >= HEAD_DIM)
                dst = ((0 if is_q else N_HEADS) + head) * LANES
                qk_ref[:, dst:dst + LANES] = jnp.where(mine, yv, aug).astype(BF16)

    row128 = pl.BlockSpec((bt, LANES), lambda i: (i, 0))
    rope_spec = pl.BlockSpec((bt, LANES), lambda i: (i % nsb, 0))
    const = lambda shape: pl.BlockSpec(shape, lambda i: (0, 0))
    return pl.pallas_call(
        body, name="qk_prep_fwd", grid=(t // bt,),
        in_specs=[pl.BlockSpec((bt, 2 * D_MODEL), lambda i: (i, 0)), row128, const((1, LANES)), const((8, LANES)),
                  rope_spec, rope_spec, rope_spec, const((LANES, LANES)), const((bt, bt))],
        out_specs=pl.BlockSpec((bt, 2 * N_HEADS * LANES), lambda i: (i, 0)),
        out_shape=jax.ShapeDtypeStruct((t, 2 * N_HEADS * LANES), BF16),
        scratch_shapes=[pltpu.VMEM((8, LANES), F32)], compiler_params=_params(1),
    )(proj, fa, b_pad, gains, *rope, seg, tri)


def _qk_prep_bwd(dq, dk, dqx, dkx, dv, proj, fa, b_pad, gains, rope, seq):
    t = proj.shape[0]
    bt = ROW_BLOCK
    nsb = seq // bt
    nblk = t // bt
    seg = _segment_ones()
    rr = np.arange(bt)
    triu = jnp.asarray(rr[:, None] <= rr[None, :], BF16)

    def body(dq_ref, dk_ref, dqx_ref, dkx_ref, dv_ref, p_ref, fa_ref, b_ref, g_ref, cos_ref, sa_ref, sb_ref, seg_ref,
             triu_ref, dp_ref, dg_ref, db_ref, carry):
        step = pl.program_id(0)

        @pl.when(step == 0)
        def _():
            dg_ref[...] = jnp.zeros_like(dg_ref)
            db_ref[...] = jnp.zeros_like(db_ref)

        @pl.when(step % nsb == 0)
        def _():
            carry[...] = jnp.zeros_like(carry)

        for tile in range(2 * N_PAIRS):
            is_q, is_dil, grow = _tile_plan(tile)
            cols = slice(tile * LANES, (tile + 1) * LANES)
            src = dq_ref if is_q else dk_ref
            half = slice((tile % N_PAIRS) * LANES, (tile % N_PAIRS + 1) * LANES)
            dy = src[:, half]
            dy = dy * (HEAD_DIM ** -0.5 if is_q else LN2)
            if is_dil:
                dy = (dy * cos_ref[...] + pltpu.roll(dy * sa_ref[...], ROPE_HALF, 1)
                      + pltpu.roll(dy * sb_ref[...], LANES - ROPE_HALF, 1))
            xs = p_ref[:, cols].astype(F32)
            r = lax.rsqrt(_split_dot(xs * xs, seg_ref[...], 2) * (1.0 / HEAD_DIM) + EPS)
            xh = xs * r
            dg_ref[tile:tile + 1, :] += jnp.sum(dy * xh, axis=0, keepdims=True)
            dxh = dy * g_ref[grow:grow + 1, :]
            seg_mean = _split_dot(dxh * xh, seg_ref[...], 2) * (1.0 / HEAD_DIM)
            dp_ref[:, cols] = (r * (dxh - xh * seg_mean)).astype(BF16)

        lane = lax.broadcasted_iota(jnp.int32, (bt, LANES), 1)
        dc = jnp.zeros((bt, LANES), F32)
        for h in range(N_HEADS_FOX):
            other = (h // 2) * LANES + HEAD_DIM * (1 - h % 2)
            row_sum = dqx_ref[:, other + AUG_C:other + AUG_C + 1]
            col_sum = dkx_ref[:, other + AUG_ONE:other + AUG_ONE + 1]
            dc = jnp.where(lane == h, row_sum - col_sum, dc)
        dlogf = _split_dot_rhs(triu_ref[...], dc, 3) + carry[0:1, :]
        carry[0:1, :] = dlogf[0:1, :]
        z = fa_ref[...] + b_ref[...]
        dfa = dlogf * (1.0 / (1.0 + jnp.exp(z)))
        db_ref[0:1, :] += jnp.sum(dfa, axis=0, keepdims=True)
        dp_ref[:, 2 * D_MODEL:MAIN_COLS] = dv_ref[...]
        dp_ref[:, MAIN_COLS:PROJ_COLS] = dfa.astype(BF16)

    rev = lambda i: nblk - 1 - i
    row = lambda w: pl.BlockSpec((bt, w), lambda i: (rev(i), 0))
    rope_spec = pl.BlockSpec((bt, LANES), lambda i: (rev(i) % nsb, 0))
    const = lambda shape: pl.BlockSpec(shape, lambda i: (0, 0))
    return pl.pallas_call(
        body, name="qk_prep_bwd", grid=(nblk,),
        in_specs=[row(D_MODEL), row(D_MODEL), row(W_GROUP), row(W_GROUP), row(D_MODEL), row(2 * D_MODEL), row(LANES),
                  const((1, LANES)), const((8, LANES)), rope_spec, rope_spec, rope_spec, const((LANES, LANES)),
                  const((bt, bt))],
        out_specs=(row(PROJ_COLS), const((2 * N_PAIRS, LANES)), const((8, LANES))),
        out_shape=(jax.ShapeDtypeStruct((t, PROJ_COLS), BF16),
                   jax.ShapeDtypeStruct((2 * N_PAIRS, LANES), F32), jax.ShapeDtypeStruct((8, LANES), F32)),
        scratch_shapes=[pltpu.VMEM((8, LANES), F32)], compiler_params=_params(1),
    )(dq, dk, dqx, dkx, dv, proj, fa, b_pad, gains, *rope, seg, triu)


def _bias_tables(seq, keys_first):
    nb = seq // ATT_BLOCK
    idx = jnp.arange(ATT_BLOCK)
    q_idx, k_idx = (idx[None, None, :], idx[None, :, None]) if keys_first else (idx[None, :, None], idx[None, None, :])
    dist = jnp.arange(nb)[:, None, None] * ATT_BLOCK + q_idx - k_idx
    causal = dist >= 0
    count = jnp.zeros(dist.shape, jnp.int32)
    for window, dilation in DILATION_PAIRS:
        count = count + (causal & (dist % dilation == 0) & (dist <= window)).astype(jnp.int32)
    fox = jnp.where(causal, 0.0, NEG).astype(F32)
    dil = jnp.where(count == 3, math.log2(3.0), jnp.where(count == 2, 1.0, jnp.where(count == 1, 0.0, NEG)))
    return jnp.stack([fox, dil.astype(F32)], axis=0)


def _attn_specs(seq):
    nb = seq // ATT_BLOCK
    col = lambda off: pl.BlockSpec((seq, LANES), lambda b, j: (b, off + j))
    heads = lambda off: pl.BlockSpec((seq, 2 * LANES), lambda b, j: (b, off + j))
    table_spec = pl.BlockSpec((1, nb, ATT_BLOCK, ATT_BLOCK), lambda b, j: (j // (N_PAIRS // 2), 0, 0, 0))
    return col, heads, table_spec


def _head_lanes(e, shape, axis):
    pos = lax.broadcasted_iota(jnp.int32, shape, axis)
    return pos < HEAD_DIM if e == 0 else pos >= HEAD_DIM


def _attn_fwd(qk, proj, tables, seq, hosted=None):
    t = qk.shape[0]
    nb = seq // ATT_BLOCK
    blk = ATT_BLOCK

    def body(q_ref, k_ref, v_ref, tab_ref, o_ref, lse_ref):
        mine = [_head_lanes(e, (seq, LANES), 1) for e in range(2)]
        lane = lax.broadcasted_iota(jnp.int32, (seq, LANES), 1)
        v_aug = [jnp.where(mine[e], v_ref[...], (lane == HEAD_DIM * (1 - e)).astype(BF16)) for e in range(2)]
        for i in range(nb):
            rows = slice(i * blk, (i + 1) * blk)
            n_keys = (i + 1) * blk
            out, lse = [], []
            for e in range(2):
                heads_e = slice(e * LANES, (e + 1) * LANES)
                s = _dot_nt(q_ref[rows, heads_e], k_ref[0:n_keys, heads_e])
                s = jnp.concatenate([s[:, jj * blk:(jj + 1) * blk] + tab_ref[0, i - jj] for jj in range(i + 1)], axis=1)
                m = jnp.max(s, axis=1, keepdims=True)
                acc = jnp.dot(jnp.exp2(s - m).astype(BF16), v_aug[e][0:n_keys], preferred_element_type=F32)
                ones_at = HEAD_DIM * (1 - e)
                l = acc[:, ones_at:ones_at + 1]
                out.append(acc / l)
                lse.append(m + jnp.log2(l))
            o_ref[rows, :] = jnp.where(mine[0][rows], out[0], out[1]).astype(o_ref.dtype)
            lse_ref[rows, :] = jnp.where(mine[0][rows], lse[0], lse[1])

    col, heads, table_spec = _attn_specs(seq)
    grid = (t // seq, N_PAIRS)
    h_specs, h_shapes, h_args, h_scratch = _hosted_parts(hosted)
    return pl.pallas_call(
        _host(body, 4, 2, 0, hosted, grid), name="attn_fwd", grid=grid,
        in_specs=[heads(0), heads(N_PAIRS), col(2 * N_PAIRS), table_spec] + h_specs,
        out_specs=tuple([col(0), col(0)] + h_specs),
        out_shape=tuple([jax.ShapeDtypeStruct((t, D_MODEL), BF16), jax.ShapeDtypeStruct((t, D_MODEL), F32)] + h_shapes),
        scratch_shapes=h_scratch, compiler_params=_params(2),
    )(qk, qk, proj, tables, *h_args)


def _attn_bwd(qk, proj, tables, o, lse, do, seq, hosted=None):
    t = qk.shape[0]
    nb = seq // ATT_BLOCK
    blk = ATT_BLOCK
    group = math.gcd(nb, ATT_GROUP)

    def body(q_ref, k_ref, v_ref, tab_ref, o_ref, lse_ref, do_ref,
             dq_ref, dk_ref, dv_ref, dqx_ref, dkx_ref, dk_acc, dv_acc):
        mine = [_head_lanes(e, (blk, LANES), 1) for e in range(2)]
        top = _head_lanes(0, (LANES, blk), 0)
        head_rows = lax.broadcasted_iota(jnp.int32, (8, LANES), 0)
        head_of_lane = lax.broadcasted_iota(jnp.int32, (8, LANES), 1) // HEAD_DIM
        head_sel = (head_rows == head_of_lane).astype(BF16)
        dk_acc[...] = jnp.zeros_like(dk_acc)
        dv_acc[...] = jnp.zeros_like(dv_acc)

        def block_rows(i):
            return pl.ds(pl.multiple_of(i * blk, blk), blk)

        def q_group(g, _):
            base = g * group
            qs, doe, delta, lse_e = [], [], [], []
            for b in range(group):
                rows = block_rows(base + b)
                qs.append([q_ref[rows, e * LANES:(e + 1) * LANES] for e in range(2)])
                do_blk = do_ref[rows, :]
                doe.append([jnp.where(mine[e], do_blk, jnp.zeros_like(do_blk)) for e in range(2)])
                delta_t = _split_dot_nt(head_sel, do_blk.astype(F32) * o_ref[rows, :].astype(F32), 3)
                lse_t = _split_dot_nt(head_sel, lse_ref[rows, :], 3) * (1.0 / HEAD_DIM)
                delta.append([delta_t[e:e + 1, :] for e in range(2)])
                lse_e.append([lse_t[e:e + 1, :] for e in range(2)])

            def key_block(dq_t, jj, members):
                krows = block_rows(jj)
                v = v_ref[krows, :]
                dq_t = [list(d) for d in dq_t]
                dv_part = None
                for e in range(2):
                    k_e = k_ref[krows, e * LANES:(e + 1) * LANES]
                    dk_part = None
                    for b, dist in members:
                        p_t = jnp.exp2(_dot_nt(k_e, qs[b][e]) + tab_ref[0, dist] - lse_e[b][e])
                        ds_t = (p_t * (_dot_nt(v, doe[b][e]) - delta[b][e])).astype(BF16)
                        part = jnp.dot(p_t.astype(BF16), doe[b][e], preferred_element_type=F32)
                        dv_part = part if dv_part is None else dv_part + part
                        part = jnp.dot(ds_t, qs[b][e], preferred_element_type=F32)
                        dk_part = part if dk_part is None else dk_part + part
                        dq_t[b][e] = dq_t[b][e] + _dot_tn(k_e, ds_t)
                    dk_acc[e, krows, :] += dk_part
                dv_acc[krows, :] += dv_part
                return tuple(tuple(d) for d in dq_t)

            zacc = jnp.zeros((LANES, blk), F32)
            dq_t = tuple((zacc, zacc) for _ in range(group))
            dq_t = lax.fori_loop(
                0, base, lambda jj, st: key_block(st, jj, [(b, base + b - jj) for b in range(group)]), dq_t)
            for a in range(group):
                dq_t = key_block(dq_t, base + a, [(b, b - a) for b in range(a, group)])
            for b in range(group):
                rows = block_rows(base + b)
                dq_ref[rows, :] = jnp.where(top, dq_t[b][0], dq_t[b][1]).T
                dqx_ref[rows, :] = jnp.where(top, dq_t[b][1], dq_t[b][0]).T
            return 0

        lax.fori_loop(0, nb // group, q_group, 0)
        lo = _head_lanes(0, (seq, LANES), 1)
        dk_ref[...] = jnp.where(lo, dk_acc[0], dk_acc[1])
        dkx_ref[...] = jnp.where(lo, dk_acc[1], dk_acc[0])
        dv_ref[...] = dv_acc[...].astype(dv_ref.dtype)

    col, heads, table_spec = _attn_specs(seq)
    grid = (t // seq, N_PAIRS)
    h_specs, h_shapes, h_args, h_scratch = _hosted_parts(hosted)
    f32_out = jax.ShapeDtypeStruct((t, D_MODEL), F32)
    return pl.pallas_call(
        _host(body, 7, 5, 2, hosted, grid), name="attn_bwd", grid=grid,
        in_specs=[heads(0), heads(N_PAIRS), col(2 * N_PAIRS), table_spec, col(0), col(0), col(0)] + h_specs,
        out_specs=tuple([col(0)] * 5 + h_specs),
        out_shape=tuple([f32_out, f32_out, jax.ShapeDtypeStruct((t, D_MODEL), BF16), f32_out, f32_out] + h_shapes),
        scratch_shapes=[pltpu.VMEM((2, seq, LANES), F32), pltpu.VMEM((seq, LANES), F32)] + h_scratch,
        compiler_params=_params(2),
    )(qk, qk, proj, tables, o, lse, do, *h_args)


def _row_block(t):
    return 1024 if t % 1024 == 0 else ROW_BLOCK


def _out_proj_ffn_norm(o, g_out, w_out, x, g_ffn):
    t = o.shape[0]
    bt = _row_block(t)

    def body(o_ref, go_ref, w_ref, x_ref, gf_ref, on_ref, x2_ref, h2_ref):
        for s in range(0, D_MODEL, W_GROUP):
            os_ = o_ref[:, s:s + W_GROUP].astype(F32)
            r = lax.rsqrt(jnp.mean(os_ * os_, axis=-1, keepdims=True) + EPS)
            on_ref[:, s:s + W_GROUP] = (os_ * r * go_ref[:, s:s + W_GROUP]).astype(BF16)
        x2 = x_ref[...] + jnp.dot(on_ref[...], w_ref[...], preferred_element_type=F32)
        x2_ref[...] = x2
        r2 = lax.rsqrt(jnp.mean(x2 * x2, axis=-1, keepdims=True) + EPS)
        h2_ref[...] = (x2 * r2 * gf_ref[...]).astype(BF16)

    row = pl.BlockSpec((bt, D_MODEL), lambda i: (i, 0))
    vec = pl.BlockSpec((1, D_MODEL), lambda i: (0, 0))
    return pl.pallas_call(
        body, name="out_proj", grid=(t // bt,),
        in_specs=[row, vec, pl.BlockSpec((D_MODEL, D_MODEL), lambda i: (0, 0)), row, vec],
        out_specs=(row, row, row),
        out_shape=(jax.ShapeDtypeStruct((t, D_MODEL), BF16), jax.ShapeDtypeStruct((t, D_MODEL), F32),
                   jax.ShapeDtypeStruct((t, D_MODEL), BF16)),
        compiler_params=_params(1),
    )(o, g_out, w_out, x, g_ffn)


def _ffn_gate_up(h2, w_gu):
    t = h2.shape[0]
    bt = _row_block(t)
    bn = _divisor_block(D_FF, 1408)
    nj = D_FF // bn

    def body(h_ref, wg_ref, wu_ref, a_ref, u_ref, f_ref):
        a = jnp.dot(h_ref[...], wg_ref[...], preferred_element_type=F32)
        u = jnp.dot(h_ref[...], wu_ref[...], preferred_element_type=F32)
        a_ref[...] = a.astype(BF16)
        u_ref[...] = u.astype(BF16)
        f_ref[...] = (a * jax.nn.sigmoid(a) * u).astype(BF16)

    blk = pl.BlockSpec((bt, bn), lambda i, j: (i, j))
    shape = jax.ShapeDtypeStruct((t, D_FF), BF16)
    return pl.pallas_call(
        body, name="ffn_gate_up", grid=(t // bt, nj),
        in_specs=[pl.BlockSpec((bt, D_MODEL), lambda i, j: (i, 0)), pl.BlockSpec((D_MODEL, bn), lambda i, j: (0, j)),
                  pl.BlockSpec((D_MODEL, bn), lambda i, j: (0, nj + j))],
        out_specs=(blk, blk, blk), out_shape=(shape, shape, shape), compiler_params=_params(2),
    )(h2, w_gu, w_gu)


def _ffn_down_grad(dy16, w_down, a, u):
    t = a.shape[0]
    bt = _row_block(t)
    bn = _divisor_block(D_FF, 1408)

    def body(dy_ref, w_ref, a_ref, u_ref, da_ref, du_ref):
        df = _dot_nt(dy_ref[...], w_ref[...])
        av = a_ref[...].astype(F32)
        sg = jax.nn.sigmoid(av)
        da_ref[...] = (df * u_ref[...].astype(F32) * sg * (1.0 + av * (1.0 - sg))).astype(BF16)
        du_ref[...] = (df * av * sg).astype(BF16)

    blk = pl.BlockSpec((bt, bn), lambda i, j: (i, j))
    shape = jax.ShapeDtypeStruct((t, D_FF), BF16)
    return pl.pallas_call(
        body, name="d_ffn_down", grid=(t // bt, D_FF // bn),
        in_specs=[pl.BlockSpec((bt, D_MODEL), lambda i, j: (i, 0)), pl.BlockSpec((bn, D_MODEL), lambda i, j: (j, 0)),
                  blk, blk],
        out_specs=(blk, blk), out_shape=(shape, shape), compiler_params=_params(2),
    )(dy16, w_down, a, u)


def _ffn_down_loss(f, w_down, x2, target):
    t, w = x2.shape
    bt = _row_block(t)

    def body(f_ref, w_ref, x_ref, t_ref, dy_ref, dy16_ref, loss_ref):
        @pl.when(pl.program_id(0) == 0)
        def _():
            loss_ref[...] = jnp.zeros_like(loss_ref)

        err = (x_ref[...] + jnp.dot(f_ref[...], w_ref[...], preferred_element_type=F32)) - t_ref[...]
        dy = err * (1.0 / w)
        dy_ref[...] = dy
        dy16_ref[...] = dy.astype(BF16)
        loss_ref[...] += 0.5 * jnp.sum(jnp.mean(err * err, axis=-1, keepdims=True), axis=0, keepdims=True)

    row = pl.BlockSpec((bt, w), lambda i: (i, 0))
    return pl.pallas_call(
        body, name="ffn_down_loss", grid=(t // bt,),
        in_specs=[pl.BlockSpec((bt, D_FF), lambda i: (i, 0)), pl.BlockSpec((D_FF, w), lambda i: (0, 0)), row, row],
        out_specs=(row, row, pl.BlockSpec((8, LANES), lambda i: (0, 0))),
        out_shape=(jax.ShapeDtypeStruct((t, w), F32), jax.ShapeDtypeStruct((t, w), BF16),
                   jax.ShapeDtypeStruct((8, LANES), F32)),
        compiler_params=_params(1),
    )(f, w_down, x2, target)


def _adamw(parts, w, m, v, *, name):
    _, rows, cols = w.shape
    br = rows if rows <= 512 else 256
    assert rows % br == 0

    def body(p_ref, w_ref, m_ref, v_ref, g_ref, d_ref, nm_ref, nv_ref):
        g = p_ref[0].astype(F32)
        for r in range(1, N_DEV):
            g = g + p_ref[r].astype(F32)
        m2 = ADAM_B1 * m_ref[0] + (1.0 - ADAM_B1) * g
        v2 = ADAM_B2 * v_ref[0] + (1.0 - ADAM_B2) * jnp.square(g)
        m_hat = m2 / (1.0 - ADAM_B1 ** ADAM_STEP)
        v_hat = v2 / (1.0 - ADAM_B2 ** ADAM_STEP)
        g_ref[0] = g
        d_ref[0] = -ADAM_LR * (m_hat / (jnp.sqrt(v_hat) + ADAM_EPS) + ADAM_WD * w_ref[0])
        nm_ref[0] = m2
        nv_ref[0] = v2

    blk = pl.BlockSpec((1, br, cols), lambda i: (0, i, 0))
    shape = jax.ShapeDtypeStruct((1, rows, cols), F32)
    return pl.pallas_call(
        body, name=name, grid=(rows // br,),
        in_specs=[pl.BlockSpec((N_DEV, br, cols), lambda i: (0, i, 0)), blk, blk, blk],
        out_specs=(blk, blk, blk, blk), out_shape=(shape, shape, shape, shape), compiler_params=_params(1),
    )(parts, w, m, v)


_QA, _KA, _VA, _FA, _QD, _KD, _VD = (0, 512), (512, 1024), (1024, 1536), (1536, 1544), (1544, 2056), (2056, 2568), (2568, 3080)
_MAIN_ORDER = (_QA, _QD, _KA, _KD, _VA, _VD)
MAIN_COLS = 3 * D_MODEL
PROJ_COLS = MAIN_COLS + LANES


def _unshard_cols(parts):
    n, r, c = parts.shape
    return jnp.transpose(parts, (1, 0, 2)).reshape(r, n * c)


def _shard_cols(full):
    r, nc = full.shape
    return jnp.transpose(full.reshape(r, N_DEV, nc // N_DEV), (1, 0, 2))


def _w_in_to_kernel(w_full):
    main = jnp.concatenate([w_full[:, a:b] for a, b in _MAIN_ORDER], axis=1)
    forget = jnp.pad(w_full[:, _FA[0]:_FA[1]], ((0, 0), (0, LANES - N_HEADS_FOX)))
    return main, forget


def _w_in_from_kernel(g):
    pos = {span: i * W_GROUP for i, span in enumerate(_MAIN_ORDER)}
    parts = []
    for span in (_QA, _KA, _VA, _FA, _QD, _KD, _VD):
        if span == _FA:
            parts.append(g[:, MAIN_COLS:MAIN_COLS + N_HEADS_FOX])
        else:
            parts.append(g[:, pos[span]:pos[span] + W_GROUP])
    return jnp.concatenate(parts, axis=1)


def _pack_small(vals):
    rows = []
    for name, _, n_rows in SMALL_LAYOUT:
        flat = vals[name].reshape(-1).astype(F32)
        rows.append(jnp.pad(flat, (0, n_rows * LANES - flat.shape[0])).reshape(n_rows, LANES))
    packed = jnp.concatenate(rows, axis=0)
    return jnp.pad(packed, ((0, SMALL_ROWS - packed.shape[0]), (0, 0)))


def _unpack_small(packed, like):
    out = {}
    for name, row, n_rows in SMALL_LAYOUT:
        n = like[name].size
        out[name] = packed[row:row + n_rows].reshape(-1)[:n].reshape(like[name].shape)
    return out


def _device_step(x, target, small, shards):
    bsz, seq, _ = x.shape
    t = bsz * seq
    xf = x.reshape(t, D_MODEL)
    tf = target.reshape(t, D_MODEL)
    row = lambda v: v.reshape(1, -1)
    g_out = jnp.concatenate([small["g_out_fox"], small["g_out_dil"]]).reshape(1, D_MODEL)
    gains = jnp.concatenate(
        [jnp.tile(small[n].reshape(1, HEAD_DIM), (1, 2)) for n in ("g_q_fox", "g_q_dil", "g_k_fox", "g_k_dil")]
        + [jnp.zeros((4, LANES), F32)], axis=0)
    b_pad = jnp.pad(small["b_forget"].reshape(1, N_HEADS_FOX), ((0, 0), (0, LANES - N_HEADS_FOX)))
    rope = _rope_tables(seq)
    tables_qk = _bias_tables(seq, keys_first=False)
    tables_kq = _bias_tables(seq, keys_first=True)

    h1, g_in = _rmsnorm_fwd(xf, row(small["g_mix"]), group=D_MODEL, name="norm_mix",
                            hosted=_Exchange([(shards["w_in"], False)]))
    w_main, w_fa = _w_in_to_kernel(_unshard_cols(g_in))
    w_in_all = jnp.concatenate([w_main, w_fa], axis=1)
    proj = _matmul(h1, w_main, name="in_proj", out_dtype=BF16)
    fa = _matmul(h1, w_fa, name="in_proj_forget", out_dtype=F32)
    qk = _qk_prep_fwd(proj, fa, b_pad, gains, rope, seq)
    late = _Exchange([(shards[n], False) for n in ("w_out", "w_gate", "w_up", "w_down")])
    o, lse, g_out_w, g_gate, g_up, g_down = _attn_fwd(qk, proj, tables_qk, seq, hosted=late)
    w_out = g_out_w.reshape(D_MODEL, D_MODEL)
    w_gu = jnp.concatenate([_unshard_cols(g_gate), _unshard_cols(g_up)], axis=1)
    w_down = g_down.reshape(D_FF, D_MODEL)
    on, x2, h2 = _out_proj_ffn_norm(o, g_out, w_out, xf, row(small["g_ffn"]))
    a, u, f = _ffn_gate_up(h2, w_gu)
    dy, dy16, loss_tile = _ffn_down_loss(f, w_down, x2, tf)

    da, du = _ffn_down_grad(dy16, w_down, a, u)
    gw_down = _matmul_tn(f, dy16, name="gw_down")
    gw_gate = _matmul_tn(h2, da, name="gw_gate")
    gw_up = _matmul_tn(h2, du, name="gw_up")
    dx2, dx2_16, dg_ffn = _norm_input_grad([(da, w_gu, 0), (du, w_gu, 1)], x2, row(small["g_ffn"]), group=D_MODEL,
                                           name="d_ffn_gate_up", out_dtypes=(F32, BF16), resid=dy)
    gw_out = _matmul_tn(on, dx2_16, name="gw_out")
    do, dg_out = _norm_input_grad([(dx2_16, w_out, 0)], o, g_out, group=W_GROUP, name="d_out_proj", out_dtypes=(BF16,))

    ffn_grads = _Exchange([
        (gw_out.reshape(N_DEV, D_MODEL // N_DEV, D_MODEL), True),
        (_shard_cols(gw_gate), True),
        (_shard_cols(gw_up), True),
        (gw_down.reshape(N_DEV, D_FF // N_DEV, D_MODEL), True),
    ])
    dq, dk, dv, dqx, dkx, p_out, p_gate, p_up, p_down = _attn_bwd(qk, proj, tables_kq, o, lse, do, seq, hosted=ffn_grads)
    dproj, dgains, db = _qk_prep_bwd(dq, dk, dqx, dkx, dv, proj, fa, b_pad, gains, rope, seq)
    gw_in = _matmul_tn(h1, dproj, name="gw_in")
    in_grad = _Exchange([(_shard_cols(_w_in_from_kernel(gw_in)).astype(BF16), True)])
    dx, dg_mix, p_in = _norm_input_grad([(dproj, w_in_all, 0)], xf, row(small["g_mix"]), group=D_MODEL,
                                        name="d_in_proj", out_dtypes=(F32,), resid=dx2, hosted=in_grad)

    fold = lambda rows: jnp.sum(rows[:, :HEAD_DIM] + rows[:, HEAD_DIM:], axis=0)
    half = N_PAIRS // 2
    gsmall = {
        "g_mix": dg_mix, "g_ffn": dg_ffn, "g_out_fox": dg_out[0, :W_GROUP], "g_out_dil": dg_out[0, W_GROUP:],
        "g_q_fox": fold(dgains[0:half]), "g_q_dil": fold(dgains[half:N_PAIRS]),
        "g_k_fox": fold(dgains[N_PAIRS:N_PAIRS + half]), "g_k_dil": fold(dgains[N_PAIRS + half:]),
        "b_forget": db[0, :N_HEADS_FOX],
    }
    packed = _pack_small(gsmall).at[LOSS_ROW].set(loss_tile[0])
    (p_small,) = _exchange("small_exchange", [(packed, False)])
    parts = {"w_in": p_in, "w_out": p_out, "w_gate": p_gate, "w_up": p_up, "w_down": p_down}
    return dx.reshape(x.shape), parts, p_small


def kernel(x, g_mix, w_in, b_forget, g_q_fox, g_k_fox, g_q_dil, g_k_dil, g_out_fox, g_out_dil, w_out, g_ffn, w_gate, w_up, w_down, loss_target, m_g_mix, m_w_in, m_b_forget, m_g_q_fox, m_g_k_fox, m_g_q_dil, m_g_k_dil, m_g_out_fox, m_g_out_dil, m_w_out, m_g_ffn, m_w_gate, m_w_up, m_w_down, v_g_mix, v_w_in, v_b_forget, v_g_q_fox, v_g_k_fox, v_g_q_dil, v_g_k_dil, v_g_out_fox, v_g_out_dil, v_w_out, v_g_ffn, v_w_gate, v_w_up, v_w_down):
    args = dict(locals())
    small_names = [name for name, _, _ in SMALL_LAYOUT]
    big_names = ["w_in", "w_out", "w_gate", "w_up", "w_down"]
    small = {n: args[n][0] for n in small_names}

    shards = {n: args[n][0].astype(BF16) for n in big_names}
    grad_x, parts, p_small = _device_step(x, loss_target, small, shards)

    grads, deltas, new_m, new_v = {}, {}, {}, {}
    for n in big_names:
        grads[n], deltas[n], new_m[n], new_v[n] = _adamw(parts[n], args[n], args["m_" + n], args["v_" + n],
                                                         name="adamw_" + n)
    res = _adamw(p_small, _pack_small(small)[None], _pack_small({n: args["m_" + n][0] for n in small_names})[None],
                 _pack_small({n: args["v_" + n][0] for n in small_names})[None], name="adamw_small")
    loss = res[0][0, LOSS_ROW, 0]
    for dst, packed_res in zip((grads, deltas, new_m, new_v), res):
        for n, val in _unpack_small(packed_res[0], small).items():
            dst[n] = val[None]

    order = ["g_mix", "w_in", "b_forget", "g_q_fox", "g_k_fox", "g_q_dil", "g_k_dil", "g_out_fox", "g_out_dil",
             "w_out", "g_ffn", "w_gate", "w_up", "w_down"]
    return (loss, grad_x, *[grads[n] for n in order], *[deltas[n] for n in order],
            *[new_m[n] for n in order], *[new_v[n] for n in order])
```

```python
import functools
import math

import jax
import jax.numpy as jnp
import numpy as np
from jax import lax
from jax.experimental import pallas as pl
from jax.experimental.pallas import tpu as pltpu

F32 = jnp.float32
BF16 = jnp.bfloat16

D_MODEL = 1024
HEAD_DIM = 64
LANES = 128
N_PAIRS = D_MODEL // LANES
N_HEADS = 2 * N_PAIRS
N_HEADS_FOX = 8
W_GROUP = 512
D_FF = 2816
IN_COLS = 3080
DILATION_PAIRS = ((128, 1), (512, 4), (2048, 16))
ROPE_THETA = 500000.0
ROPE_DIM = 16
ROPE_HALF = ROPE_DIM // 2
EPS = 1e-6
NEG = -1e30
LOG2E = 1.4426950408889634
LN2 = 0.6931471805599453
AUG_ONE = 0
AUG_C = 3
N_DEV = 8

ADAM_LR = 0.001
ADAM_B1 = 0.9
ADAM_B2 = 0.999
ADAM_EPS = 1e-08
ADAM_WD = 0.01
ADAM_STEP = 10

ROW_BLOCK = 512
ATT_BLOCK = 256
ATT_GROUP = 4
VMEM_LIMIT = 56 * 1024 * 1024
MATMUL_VMEM_BUDGET = 44 * 1024 * 1024

SMALL_ROWS = 32
SMALL_LAYOUT = (("g_mix", 0, 8), ("g_ffn", 8, 8), ("g_out_fox", 16, 4), ("g_out_dil", 20, 4),
                ("g_q_fox", 24, 1), ("g_k_fox", 25, 1), ("g_q_dil", 26, 1), ("g_k_dil", 27, 1),
                ("b_forget", 28, 1))
LOSS_ROW = 29


def _params(n_grid):
    return pltpu.CompilerParams(dimension_semantics=("arbitrary",) * n_grid, vmem_limit_bytes=VMEM_LIMIT)


def _divisor_block(n, cap):
    best = None
    for b in range(LANES, min(n, cap) + 1, LANES):
        if n % b == 0:
            best = b
    assert best is not None, n
    return best


def _split_dot(a, b_exact, terms):
    acc = None
    rest = a
    for _ in range(terms):
        hi = rest.astype(BF16)
        part = jnp.dot(hi, b_exact, preferred_element_type=F32)
        acc = part if acc is None else acc + part
        rest = rest - hi.astype(F32)
    return acc


def _split_dot_rhs(a_exact, b, terms):
    acc = None
    rest = b
    for _ in range(terms):
        hi = rest.astype(BF16)
        part = jnp.dot(a_exact, hi, preferred_element_type=F32)
        acc = part if acc is None else acc + part
        rest = rest - hi.astype(F32)
    return acc


def _split_dot_nt(a_exact, b, terms):
    acc = None
    rest = b
    for _ in range(terms):
        hi = rest.astype(BF16)
        part = _dot_nt(a_exact, hi)
        acc = part if acc is None else acc + part
        rest = rest - hi.astype(F32)
    return acc


def _dot_nt(a, b):
    return lax.dot_general(a, b, (((1,), (1,)), ((), ())), preferred_element_type=F32)


def _dot_tn(a, b):
    return lax.dot_general(a, b, (((0,), (0,)), ((), ())), preferred_element_type=F32)


class _Exchange:
    def __init__(self, items):
        self.items = items
        self.n = len(items)
        self.arrays = [a for a, _ in items]
        self.out_shape = [jax.ShapeDtypeStruct((N_DEV,) + tuple(a.shape[1:] if sc else a.shape), a.dtype)
                          for a, sc in items]
        self.specs = [pl.BlockSpec(memory_space=pl.ANY)] * self.n
        self.scratch = [pltpu.SemaphoreType.DMA((self.n, N_DEV - 1)), pltpu.SemaphoreType.DMA((self.n, N_DEV - 1)),
                        pltpu.SemaphoreType.DMA((self.n,))]

    def run(self, ins, outs, sems, first, last, compute):
        send_sems, recv_sems, local_sems = sems
        x, y, c = lax.axis_index("x"), lax.axis_index("y"), lax.axis_index("c")
        me = 4 * x + 2 * y + c
        local, remote = [], []
        for k, (_, scatter) in enumerate(self.items):
            own = ins[k].at[me] if scatter else ins[k]
            local.append(pltpu.make_async_copy(own, outs[k].at[me], local_sems.at[k]))
        for r in range(1, N_DEV):
            px = 1 - x if r & 4 else x
            py = 1 - y if r & 2 else y
            pc = 1 - c if r & 1 else c
            peer = 4 * px + 2 * py + pc
            for k, (_, scatter) in enumerate(self.items):
                src = ins[k].at[peer] if scatter else ins[k]
                remote.append(pltpu.make_async_remote_copy(
                    src_ref=src, dst_ref=outs[k].at[me],
                    send_sem=send_sems.at[k, r - 1], recv_sem=recv_sems.at[k, r - 1],
                    device_id=(px, py, pc), device_id_type=pl.DeviceIdType.MESH))

        def start():
            for cp in local + remote:
                cp.start()

        def finish():
            for cp in remote:
                cp.wait_recv()
            for cp in remote:
                cp.wait_send()
            for cp in local:
                cp.wait()

        if first is None:
            start()
            compute()
            finish()
        else:
            pl.when(first)(start)
            compute()
            pl.when(last)(finish)


def _grid_ends(grid):
    ids = [pl.program_id(d) for d in range(len(grid))]
    first = functools.reduce(jnp.logical_and, [i == 0 for i in ids])
    last = functools.reduce(jnp.logical_and, [i == g - 1 for i, g in zip(ids, grid)])
    return first, last


def _host(core, n_in, n_out, n_scratch, hosted, grid):
    if hosted is None:
        return core
    nh = hosted.n

    def body(*refs):
        ins, rest = refs[:n_in], refs[n_in:]
        h_ins, rest = rest[:nh], rest[nh:]
        outs, rest = rest[:n_out], rest[n_out:]
        h_outs, rest = rest[:nh], rest[nh:]
        scratch, sems = rest[:n_scratch], rest[n_scratch:]
        first, last = _grid_ends(grid)
        hosted.run(h_ins, h_outs, sems, first, last, lambda: core(*ins, *outs, *scratch))

    return body


def _hosted_parts(hosted):
    if hosted is None:
        return [], [], [], []
    return list(hosted.specs), list(hosted.out_shape), list(hosted.arrays), list(hosted.scratch)


def _exchange(name, items):
    ex = _Exchange(items)
    n = ex.n

    def body(*refs):
        ex.run(refs[:n], refs[n:2 * n], refs[2 * n:], None, None, lambda: None)

    return pl.pallas_call(
        body, name=name, out_shape=tuple(ex.out_shape), in_specs=ex.specs, out_specs=tuple(ex.specs),
        scratch_shapes=ex.scratch,
    )(*ex.arrays)


def _matmul_blocks(t, k, n, a_bytes, o_bytes):
    for bt, cap in ((1024, 1408), (1024, 512), (512, 512)):
        if t % bt:
            continue
        bn = _divisor_block(n, cap)
        if 2 * (bt * k * a_bytes + bn * k * 2 + bt * bn * o_bytes) <= MATMUL_VMEM_BUDGET:
            return bt, bn
    return ROW_BLOCK, _divisor_block(n, 256)


def _matmul(a, w, *, name, out_dtype):
    t, k = a.shape
    n = w.shape[1]
    assert w.shape[0] == k
    bt, bn = _matmul_blocks(t, k, n, a.dtype.itemsize, jnp.dtype(out_dtype).itemsize)

    def body(a_ref, w_ref, o_ref):
        o_ref[...] = jnp.dot(a_ref[...], w_ref[...], preferred_element_type=F32).astype(o_ref.dtype)

    return pl.pallas_call(
        body, name=name, grid=(t // bt, n // bn),
        in_specs=[pl.BlockSpec((bt, k), lambda i, j: (i, 0)), pl.BlockSpec((k, bn), lambda i, j: (0, j))],
        out_specs=pl.BlockSpec((bt, bn), lambda i, j: (i, j)),
        out_shape=jax.ShapeDtypeStruct((t, n), out_dtype), compiler_params=_params(2),
    )(a, w)


def _matmul_tn(a, b, *, name):
    t, m = a.shape
    n = b.shape[1]
    bt = 2048 if t % 2048 == 0 else ROW_BLOCK
    bm = _divisor_block(m, 1408)
    bn = _divisor_block(n, 1408)

    def body(a_ref, b_ref, o_ref):
        @pl.when(pl.program_id(2) == 0)
        def _():
            o_ref[...] = jnp.zeros_like(o_ref)

        o_ref[...] += _dot_tn(a_ref[...], b_ref[...])

    return pl.pallas_call(
        body, name=name, grid=(m // bm, n // bn, t // bt),
        in_specs=[pl.BlockSpec((bt, bm), lambda i, j, s: (s, i)), pl.BlockSpec((bt, bn), lambda i, j, s: (s, j))],
        out_specs=pl.BlockSpec((bm, bn), lambda i, j, s: (i, j)),
        out_shape=jax.ShapeDtypeStruct((m, n), F32), compiler_params=_params(3),
    )(a, b)


def _rmsnorm_fwd(x, g, *, group, name, hosted=None):
    t, w = x.shape
    bt = ROW_BLOCK

    def body(x_ref, g_ref, o_ref):
        for s in range(0, w, group):
            xs = x_ref[:, s:s + group].astype(F32)
            r = lax.rsqrt(jnp.mean(xs * xs, axis=-1, keepdims=True) + EPS)
            o_ref[:, s:s + group] = (xs * r * g_ref[:, s:s + group]).astype(o_ref.dtype)

    grid = (t // bt,)
    h_specs, h_shapes, h_args, h_scratch = _hosted_parts(hosted)
    res = pl.pallas_call(
        _host(body, 2, 1, 0, hosted, grid), name=name, grid=grid,
        in_specs=[pl.BlockSpec((bt, w), lambda i: (i, 0)), pl.BlockSpec((1, w), lambda i: (0, 0))] + h_specs,
        out_specs=tuple([pl.BlockSpec((bt, w), lambda i: (i, 0))] + h_specs),
        out_shape=tuple([jax.ShapeDtypeStruct((t, w), BF16)] + h_shapes),
        scratch_shapes=h_scratch, compiler_params=_params(1),
    )(x, g, *h_args)
    return res if hosted else res[0]


def _norm_input_grad(terms, x, g, *, group, name, out_dtypes, resid=None, hosted=None, k_chunks=1):
    t, w = x.shape
    n_terms = len(terms)
    kc = [a.shape[1] // k_chunks for a, _, _ in terms]
    per_row = sum(c * a.dtype.itemsize for c, (a, _, _) in zip(kc, terms)) + w * (x.dtype.itemsize + 4 * (resid is not None))
    per_row += w * sum(jnp.dtype(dt).itemsize for dt in out_dtypes)
    fixed = 2 * sum(wt.shape[0] * c * 2 for c, (_, wt, _) in zip(kc, terms))
    bt = next(b for b in (1024, 512, 256, 128)
              if t % b == 0 and fixed + 2 * b * per_row + 5 * b * w * 4 <= MATMUL_VMEM_BUDGET)
    n_in = 2 * n_terms + 2 + (resid is not None)
    grid = (t // bt, k_chunks)

    def body(*refs):
        x_ref, g_ref = refs[2 * n_terms], refs[2 * n_terms + 1]
        dx_refs, dg_ref, dh_ref = refs[n_in:-2], refs[-2], refs[-1]
        chunk = pl.program_id(1)

        @pl.when((pl.program_id(0) == 0) & (chunk == 0))
        def _():
            dg_ref[...] = jnp.zeros_like(dg_ref)

        part = None
        for k in range(n_terms):
            term = _dot_nt(refs[2 * k][...], refs[2 * k + 1][...])
            part = term if part is None else part + term

        @pl.when(chunk == 0)
        def _():
            dh_ref[...] = part

        @pl.when(chunk > 0)
        def _():
            dh_ref[...] += part

        @pl.when(chunk == k_chunks - 1)
        def _():
            for s in range(0, w, group):
                xs = x_ref[:, s:s + group].astype(F32)
                dhs = dh_ref[:, s:s + group]
                r = lax.rsqrt(jnp.mean(xs * xs, axis=-1, keepdims=True) + EPS)
                xh = xs * r
                dg_ref[:, s:s + group] += jnp.sum(dhs * xh, axis=0, keepdims=True)
                dxh = dhs * g_ref[:, s:s + group]
                dx = r * (dxh - xh * jnp.mean(dxh * xh, axis=-1, keepdims=True))
                if resid is not None:
                    dx = refs[n_in - 1][:, s:s + group] + dx
                for dx_ref in dx_refs:
                    dx_ref[:, s:s + group] = dx.astype(dx_ref.dtype)

    row = pl.BlockSpec((bt, w), lambda i, k: (i, 0))
    vec = pl.BlockSpec((1, w), lambda i, k: (0, 0))
    in_specs, args = [], []
    for c, (a, wt, col_block) in zip(kc, terms):
        in_specs += [pl.BlockSpec((bt, c), lambda i, k: (i, k)),
                     pl.BlockSpec((wt.shape[0], c), lambda i, k, cb=col_block: (0, cb * k_chunks + k))]
        args += [a, wt]
    in_specs += [row, vec] + ([row] if resid is not None else [])
    args += [x, g] + ([resid] if resid is not None else [])
    h_specs, h_shapes, h_args, h_scratch = _hosted_parts(hosted)
    return pl.pallas_call(
        _host(body, n_in, len(out_dtypes) + 1, 1, hosted, grid), name=name, grid=grid, in_specs=in_specs + h_specs,
        out_specs=tuple([row] * len(out_dtypes) + [vec] + h_specs),
        out_shape=tuple([jax.ShapeDtypeStruct((t, w), dt) for dt in out_dtypes] + [jax.ShapeDtypeStruct((1, w), F32)]
                        + h_shapes),
        scratch_shapes=[pltpu.VMEM((bt, w), F32)] + h_scratch, compiler_params=_params(2),
    )(*args, *h_args)


def _tile_plan(tile):
    is_q = tile < N_PAIRS
    is_dil = (tile % N_PAIRS) >= N_PAIRS // 2
    return is_q, is_dil, (0 if is_q else 2) + (1 if is_dil else 0)


def _segment_ones():
    lane = np.arange(LANES)
    return jnp.asarray((lane[:, None] // HEAD_DIM) == (lane[None, :] // HEAD_DIM), BF16)


def _rope_tables(seq):
    inv_freq = jnp.power(jnp.float32(ROPE_THETA), -jnp.arange(ROPE_HALF, dtype=F32) * 2.0 / ROPE_DIM)
    ang = jnp.arange(seq).astype(F32)[:, None] * inv_freq[None, :]
    cos, sin = jnp.cos(ang), jnp.sin(ang)
    ones = jnp.ones((seq, HEAD_DIM - ROPE_DIM), F32)
    zeros = jnp.zeros((seq, HEAD_DIM - ROPE_DIM), F32)
    zh = jnp.zeros((seq, ROPE_HALF), F32)
    cos_t = jnp.concatenate([cos, cos, ones], axis=1)
    sin_a = jnp.concatenate([-sin, zh, zeros], axis=1)
    sin_b = jnp.concatenate([zh, sin, zeros], axis=1)
    return tuple(jnp.tile(tab, (1, 2)) for tab in (cos_t, sin_a, sin_b))


def _log_sigmoid(z):
    return jnp.minimum(z, 0.0) - jnp.log1p(jnp.exp(-jnp.abs(z)))


def _qk_prep_fwd(proj, fa, b_pad, gains, rope, seq):
    t = proj.shape[0]
    bt = ROW_BLOCK
    nsb = seq // bt
    seg = _segment_ones()
    rr = np.arange(bt)
    tri = jnp.asarray(rr[:, None] >= rr[None, :], BF16)

    def body(p_ref, fa_ref, b_ref, g_ref, cos_ref, sa_ref, sb_ref, seg_ref, tri_ref, qk_ref, carry):
        @pl.when(pl.program_id(0) % nsb == 0)
        def _():
            carry[...] = jnp.zeros_like(carry)

        lane = lax.broadcasted_iota(jnp.int32, (bt, LANES), 1)
        logf = jnp.where(lane < N_HEADS_FOX, _log_sigmoid(fa_ref[...] + b_ref[...]), 0.0)
        cblk = _split_dot_rhs(tri_ref[...], logf, 3) + carry[0:1, :]
        carry[0:1, :] = cblk[bt - 1:bt, :]
        c_terms = []
        rest = cblk * LOG2E
        for _ in range(3):
            term = rest.astype(BF16).astype(F32)
            c_terms.append(term)
            rest = rest - term

        for tile in range(2 * N_PAIRS):
            is_q, is_dil, grow = _tile_plan(tile)
            pair = tile % N_PAIRS
            xs = p_ref[:, tile * LANES:(tile + 1) * LANES].astype(F32)
            r = lax.rsqrt(_split_dot(xs * xs, seg_ref[...], 2) * (1.0 / HEAD_DIM) + EPS)
            yv = xs * r * g_ref[grow:grow + 1, :]
            if is_dil:
                yv = (yv * cos_ref[...] + pltpu.roll(yv, LANES - ROPE_HALF, 1) * sa_ref[...]
                      + pltpu.roll(yv, ROPE_HALF, 1) * sb_ref[...])
            if is_q:
                yv = yv * (HEAD_DIM ** -0.5 * LOG2E)
            for e in range(2):
                head = 2 * pair + e
                other = HEAD_DIM * (1 - e)
                aug = jnp.zeros((bt, LANES), F32)
                if not is_dil:
                    ones_at = other + (AUG_ONE if is_q else AUG_C)
                    c_at = other + (AUG_C if is_q else AUG_ONE)
                    aug = jnp.where((lane >= ones_at) & (lane < ones_at + 3), 1.0, aug)
                    for n, term in enumerate(c_terms):
                        col = term[:, head:head + 1]
                        aug = jnp.where(lane == c_at + n, col if is_q else -col, aug)
                mine = (lane < HEAD_DIM) if e == 0 else (lane >= HEAD_DIM)
                dst = ((0 if is_q else N_HEADS) + head) * LANES
                qk_ref[:, dst:dst + LANES] = jnp.where(mine, yv, aug).astype(BF16)

    row128 = pl.BlockSpec((bt, LANES), lambda i: (i, 0))
    rope_spec = pl.BlockSpec((bt, LANES), lambda i: (i % nsb, 0))
    const = lambda shape: pl.BlockSpec(shape, lambda i: (0, 0))
    return pl.pallas_call(
        body, name="qk_prep_fwd", grid=(t // bt,),
        in_specs=[pl.BlockSpec((bt, 2 * D_MODEL), lambda i: (i, 0)), row128, const((1, LANES)), const((8, LANES)),
                  rope_spec, rope_spec, rope_spec, const((LANES, LANES)), const((bt, bt))],
        out_specs=pl.BlockSpec((bt, 2 * N_HEADS * LANES), lambda i: (i, 0)),
        out_shape=jax.ShapeDtypeStruct((t, 2 * N_HEADS * LANES), BF16),
        scratch_shapes=[pltpu.VMEM((8, LANES), F32)], compiler_params=_params(1),
    )(proj, fa, b_pad, gains, *rope, seg, tri)


def _qk_prep_bwd(dq, dk, dqx, dkx, dv, proj, fa, b_pad, gains, rope, seq):
    t = proj.shape[0]
    bt = ROW_BLOCK
    nsb = seq // bt
    nblk = t // bt
    seg = _segment_ones()
    rr = np.arange(bt)
    triu = jnp.asarray(rr[:, None] <= rr[None, :], BF16)

    def body(dq_ref, dk_ref, dqx_ref, dkx_ref, dv_ref, p_ref, fa_ref, b_ref, g_ref, cos_ref, sa_ref, sb_ref, seg_ref,
             triu_ref, dp_ref, dg_ref, db_ref, carry):
        step = pl.program_id(0)

        @pl.when(step == 0)
        def _():
            dg_ref[...] = jnp.zeros_like(dg_ref)
            db_ref[...] = jnp.zeros_like(db_ref)

        @pl.when(step % nsb == 0)
        def _():
            carry[...] = jnp.zeros_like(carry)

        for tile in range(2 * N_PAIRS):
            is_q, is_dil, grow = _tile_plan(tile)
            cols = slice(tile * LANES, (tile + 1) * LANES)
            src = dq_ref if is_q else dk_ref
            half = slice((tile % N_PAIRS) * LANES, (tile % N_PAIRS + 1) * LANES)
            dy = src[:, half]
            dy = dy * (HEAD_DIM ** -0.5 if is_q else LN2)
            if is_dil:
                dy = (dy * cos_ref[...] + pltpu.roll(dy * sa_ref[...], ROPE_HALF, 1)
                      + pltpu.roll(dy * sb_ref[...], LANES - ROPE_HALF, 1))
            xs = p_ref[:, cols].astype(F32)
            r = lax.rsqrt(_split_dot(xs * xs, seg_ref[...], 2) * (1.0 / HEAD_DIM) + EPS)
            xh = xs * r
            dg_ref[tile:tile + 1, :] += jnp.sum(dy * xh, axis=0, keepdims=True)
            dxh = dy * g_ref[grow:grow + 1, :]
            seg_mean = _split_dot(dxh * xh, seg_ref[...], 2) * (1.0 / HEAD_DIM)
            dp_ref[:, cols] = (r * (dxh - xh * seg_mean)).astype(BF16)

        lane = lax.broadcasted_iota(jnp.int32, (bt, LANES), 1)
        dc = jnp.zeros((bt, LANES), F32)
        for h in range(N_HEADS_FOX):
            other = (h // 2) * LANES + HEAD_DIM * (1 - h % 2)
            row_sum = dqx_ref[:, other + AUG_C:other + AUG_C + 1]
            col_sum = dkx_ref[:, other + AUG_ONE:other + AUG_ONE + 1]
            dc = jnp.where(lane == h, row_sum - col_sum, dc)
        dlogf = _split_dot_rhs(triu_ref[...], dc, 3) + carry[0:1, :]
        carry[0:1, :] = dlogf[0:1, :]
        z = fa_ref[...] + b_ref[...]
        dfa = dlogf * (1.0 / (1.0 + jnp.exp(z)))
        db_ref[0:1, :] += jnp.sum(dfa, axis=0, keepdims=True)
        dp_ref[:, 2 * D_MODEL:MAIN_COLS] = dv_ref[...]
        dp_ref[:, MAIN_COLS:PROJ_COLS] = dfa.astype(BF16)

    rev = lambda i: nblk - 1 - i
    row = lambda w: pl.BlockSpec((bt, w), lambda i: (rev(i), 0))
    rope_spec = pl.BlockSpec((bt, LANES), lambda i: (rev(i) % nsb, 0))
    const = lambda shape: pl.BlockSpec(shape, lambda i: (0, 0))
    return pl.pallas_call(
        body, name="qk_prep_bwd", grid=(nblk,),
        in_specs=[row(D_MODEL), row(D_MODEL), row(W_GROUP), row(W_GROUP), row(D_MODEL), row(2 * D_MODEL), row(LANES),
                  const((1, LANES)), const((8, LANES)), rope_spec, rope_spec, rope_spec, const((LANES, LANES)),
                  const((bt, bt))],
        out_specs=(row(PROJ_COLS), const((2 * N_PAIRS, LANES)), const((8, LANES))),
        out_shape=(jax.ShapeDtypeStruct((t, PROJ_COLS), BF16),
                   jax.ShapeDtypeStruct((2 * N_PAIRS, LANES), F32), jax.ShapeDtypeStruct((8, LANES), F32)),
        scratch_shapes=[pltpu.VMEM((8, LANES), F32)], compiler_params=_params(1),
    )(dq, dk, dqx, dkx, dv, proj, fa, b_pad, gains, *rope, seg, triu)


def _bias_tables(seq, keys_first):
    nb = seq // ATT_BLOCK
    idx = jnp.arange(ATT_BLOCK)
    q_idx, k_idx = (idx[None, None, :], idx[None, :, None]) if keys_first else (idx[None, :, None], idx[None, None, :])
    dist = jnp.arange(nb)[:, None, None] * ATT_BLOCK + q_idx - k_idx
    causal = dist >= 0
    count = jnp.zeros(dist.shape, jnp.int32)
    for window, dilation in DILATION_PAIRS:
        count = count + (causal & (dist % dilation == 0) & (dist <= window)).astype(jnp.int32)
    fox = jnp.where(causal, 0.0, NEG).astype(F32)
    dil = jnp.where(count == 3, math.log2(3.0), jnp.where(count == 2, 1.0, jnp.where(count == 1, 0.0, NEG)))
    return jnp.stack([fox, dil.astype(F32)], axis=0)


def _attn_specs(seq):
    nb = seq // ATT_BLOCK
    col = lambda off: pl.BlockSpec((seq, LANES), lambda b, j: (b, off + j))
    heads = lambda off: pl.BlockSpec((seq, 2 * LANES), lambda b, j: (b, off + j))
    table_spec = pl.BlockSpec((1, nb, ATT_BLOCK, ATT_BLOCK), lambda b, j: (j // (N_PAIRS // 2), 0, 0, 0))
    return col, heads, table_spec


def _head_lanes(e, shape, axis):
    pos = lax.broadcasted_iota(jnp.int32, shape, axis)
    return pos < HEAD_DIM if e == 0 else pos >= HEAD_DIM


def _attn_fwd(qk, proj, tables, seq, hosted=None):
    t = qk.shape[0]
    nb = seq // ATT_BLOCK
    blk = ATT_BLOCK

    def body(q_ref, k_ref, v_ref, tab_ref, o_ref, lse_ref):
        mine = [_head_lanes(e, (seq, LANES), 1) for e in range(2)]
        lane = lax.broadcasted_iota(jnp.int32, (seq, LANES), 1)
        v_aug = [jnp.where(mine[e], v_ref[...], (lane == HEAD_DIM * (1 - e)).astype(BF16)) for e in range(2)]
        for i in range(nb):
            rows = slice(i * blk, (i + 1) * blk)
            n_keys = (i + 1) * blk
            out, lse = [], []
            for e in range(2):
                heads_e = slice(e * LANES, (e + 1) * LANES)
                s = _dot_nt(q_ref[rows, heads_e], k_ref[0:n_keys, heads_e])
                s = jnp.concatenate([s[:, jj * blk:(jj + 1) * blk] + tab_ref[0, i - jj] for jj in range(i + 1)], axis=1)
                m = jnp.max(s, axis=1, keepdims=True)
                acc = jnp.dot(jnp.exp2(s - m).astype(BF16), v_aug[e][0:n_keys], preferred_element_type=F32)
                ones_at = HEAD_DIM * (1 - e)
                l = acc[:, ones_at:ones_at + 1]
                out.append(acc / l)
                lse.append(m + jnp.log2(l))
            o_ref[rows, :] = jnp.where(mine[0][rows], out[0], out[1]).astype(o_ref.dtype)
            lse_ref[rows, :] = jnp.where(mine[0][rows], lse[0], lse[1])

    col, heads, table_spec = _attn_specs(seq)
    grid = (t // seq, N_PAIRS)
    h_specs, h_shapes, h_args, h_scratch = _hosted_parts(hosted)
    return pl.pallas_call(
        _host(body, 4, 2, 0, hosted, grid), name="attn_fwd", grid=grid,
        in_specs=[heads(0), heads(N_PAIRS), col(2 * N_PAIRS), table_spec] + h_specs,
        out_specs=tuple([col(0), col(0)] + h_specs),
        out_shape=tuple([jax.ShapeDtypeStruct((t, D_MODEL), BF16), jax.ShapeDtypeStruct((t, D_MODEL), F32)] + h_shapes),
        scratch_shapes=h_scratch, compiler_params=_params(2),
    )(qk, qk, proj, tables, *h_args)


def _attn_bwd(qk, proj, tables, o, lse, do, seq, hosted=None):
    t = qk.shape[0]
    nb = seq // ATT_BLOCK
    blk = ATT_BLOCK
    group = math.gcd(nb, ATT_GROUP)

    def body(q_ref, k_ref, v_ref, tab_ref, o_ref, lse_ref, do_ref,
             dq_ref, dk_ref, dv_ref, dqx_ref, dkx_ref, dk_acc, dv_acc):
        mine = [_head_lanes(e, (blk, LANES), 1) for e in range(2)]
        top = _head_lanes(0, (LANES, blk), 0)
        head_rows = lax.broadcasted_iota(jnp.int32, (8, LANES), 0)
        head_of_lane = lax.broadcasted_iota(jnp.int32, (8, LANES), 1) // HEAD_DIM
        head_sel = (head_rows == head_of_lane).astype(BF16)
        dk_acc[...] = jnp.zeros_like(dk_acc)
        dv_acc[...] = jnp.zeros_like(dv_acc)

        def block_rows(i):
            return pl.ds(pl.multiple_of(i * blk, blk), blk)

        def q_group(g, _):
            base = g * group
            qs, doe, delta, lse_e = [], [], [], []
            for b in range(group):
                rows = block_rows(base + b)
                qs.append([q_ref[rows, e * LANES:(e + 1) * LANES] for e in range(2)])
                do_blk = do_ref[rows, :]
                doe.append([jnp.where(mine[e], do_blk, jnp.zeros_like(do_blk)) for e in range(2)])
                delta_t = _split_dot_nt(head_sel, do_blk.astype(F32) * o_ref[rows, :].astype(F32), 3)
                lse_t = _split_dot_nt(head_sel, lse_ref[rows, :], 3) * (1.0 / HEAD_DIM)
                delta.append([delta_t[e:e + 1, :] for e in range(2)])
                lse_e.append([lse_t[e:e + 1, :] for e in range(2)])

            def key_block(dq_t, jj, members):
                krows = block_rows(jj)
                v = v_ref[krows, :]
                dq_t = [list(d) for d in dq_t]
                dv_part = None
                for e in range(2):
                    k_e = k_ref[krows, e * LANES:(e + 1) * LANES]
                    dk_part = None
                    for b, dist in members:
                        p_t = jnp.exp2(_dot_nt(k_e, qs[b][e]) + tab_ref[0, dist] - lse_e[b][e])
                        ds_t = (p_t * (_dot_nt(v, doe[b][e]) - delta[b][e])).astype(BF16)
                        part = jnp.dot(p_t.astype(BF16), doe[b][e], preferred_element_type=F32)
                        dv_part = part if dv_part is None else dv_part + part
                        part = jnp.dot(ds_t, qs[b][e], preferred_element_type=F32)
                        dk_part = part if dk_part is None else dk_part + part
                        dq_t[b][e] = dq_t[b][e] + _dot_tn(k_e, ds_t)
                    dk_acc[e, krows, :] += dk_part
                dv_acc[krows, :] += dv_part
                return tuple(tuple(d) for d in dq_t)

            zacc = jnp.zeros((LANES, blk), F32)
            dq_t = tuple((zacc, zacc) for _ in range(group))
            dq_t = lax.fori_loop(
                0, base, lambda jj, st: key_block(st, jj, [(b, base + b - jj) for b in range(group)]), dq_t)
            for a in range(group):
                dq_t = key_block(dq_t, base + a, [(b, b - a) for b in range(a, group)])
            for b in range(group):
                rows = block_rows(base + b)
                dq_ref[rows, :] = jnp.where(top, dq_t[b][0], dq_t[b][1]).T
                dqx_ref[rows, :] = jnp.where(top, dq_t[b][1], dq_t[b][0]).T
            return 0

        lax.fori_loop(0, nb // group, q_group, 0)
        lo = _head_lanes(0, (seq, LANES), 1)
        dk_ref[...] = jnp.where(lo, dk_acc[0], dk_acc[1])
        dkx_ref[...] = jnp.where(lo, dk_acc[1], dk_acc[0])
        dv_ref[...] = dv_acc[...].astype(dv_ref.dtype)

    col, heads, table_spec = _attn_specs(seq)
    grid = (t // seq, N_PAIRS)
    h_specs, h_shapes, h_args, h_scratch = _hosted_parts(hosted)
    f32_out = jax.ShapeDtypeStruct((t, D_MODEL), F32)
    return pl.pallas_call(
        _host(body, 7, 5, 2, hosted, grid), name="attn_bwd", grid=grid,
        in_specs=[heads(0), heads(N_PAIRS), col(2 * N_PAIRS), table_spec, col(0), col(0), col(0)] + h_specs,
        out_specs=tuple([col(0)] * 5 + h_specs),
        out_shape=tuple([f32_out, f32_out, jax.ShapeDtypeStruct((t, D_MODEL), BF16), f32_out, f32_out] + h_shapes),
        scratch_shapes=[pltpu.VMEM((2, seq, LANES), F32), pltpu.VMEM((seq, LANES), F32)] + h_scratch,
        compiler_params=_params(2),
    )(qk, qk, proj, tables, o, lse, do, *h_args)


def _row_block(t):
    return 1024 if t % 1024 == 0 else ROW_BLOCK


def _out_proj_ffn_norm(o, g_out, w_out, x, g_ffn):
    t = o.shape[0]
    bt = _row_block(t)

    def body(o_ref, go_ref, w_ref, x_ref, gf_ref, on_ref, x2_ref, h2_ref):
        for s in range(0, D_MODEL, W_GROUP):
            os_ = o_ref[:, s:s + W_GROUP].astype(F32)
            r = lax.rsqrt(jnp.mean(os_ * os_, axis=-1, keepdims=True) + EPS)
            on_ref[:, s:s + W_GROUP] = (os_ * r * go_ref[:, s:s + W_GROUP]).astype(BF16)
        x2 = x_ref[...] + jnp.dot(on_ref[...], w_ref[...], preferred_element_type=F32)
        x2_ref[...] = x2
        r2 = lax.rsqrt(jnp.mean(x2 * x2, axis=-1, keepdims=True) + EPS)
        h2_ref[...] = (x2 * r2 * gf_ref[...]).astype(BF16)

    row = pl.BlockSpec((bt, D_MODEL), lambda i: (i, 0))
    vec = pl.BlockSpec((1, D_MODEL), lambda i: (0, 0))
    return pl.pallas_call(
        body, name="out_proj", grid=(t // bt,),
        in_specs=[row, vec, pl.BlockSpec((D_MODEL, D_MODEL), lambda i: (0, 0)), row, vec],
        out_specs=(row, row, row),
        out_shape=(jax.ShapeDtypeStruct((t, D_MODEL), BF16), jax.ShapeDtypeStruct((t, D_MODEL), F32),
                   jax.ShapeDtypeStruct((t, D_MODEL), BF16)),
        compiler_params=_params(1),
    )(o, g_out, w_out, x, g_ffn)


def _ffn_gate_up(h2, w_gu):
    t = h2.shape[0]
    bt = _row_block(t)
    bn = _divisor_block(D_FF, 1408)
    nj = D_FF // bn

    def body(h_ref, wg_ref, wu_ref, a_ref, u_ref, f_ref):
        a = jnp.dot(h_ref[...], wg_ref[...], preferred_element_type=F32)
        u = jnp.dot(h_ref[...], wu_ref[...], preferred_element_type=F32)
        a_ref[...] = a.astype(BF16)
        u_ref[...] = u.astype(BF16)
        f_ref[...] = (a * jax.nn.sigmoid(a) * u).astype(BF16)

    blk = pl.BlockSpec((bt, bn), lambda i, j: (i, j))
    shape = jax.ShapeDtypeStruct((t, D_FF), BF16)
    return pl.pallas_call(
        body, name="ffn_gate_up", grid=(t // bt, nj),
        in_specs=[pl.BlockSpec((bt, D_MODEL), lambda i, j: (i, 0)), pl.BlockSpec((D_MODEL, bn), lambda i, j: (0, j)),
                  pl.BlockSpec((D_MODEL, bn), lambda i, j: (0, nj + j))],
        out_specs=(blk, blk, blk), out_shape=(shape, shape, shape), compiler_params=_params(2),
    )(h2, w_gu, w_gu)


def _ffn_down_grad(dy16, w_down, a, u):
    t = a.shape[0]
    bt = _row_block(t)
    bn = _divisor_block(D_FF, 1408)

    def body(dy_ref, w_ref, a_ref, u_ref, da_ref, du_ref):
        df = _dot_nt(dy_ref[...], w_ref[...])
        av = a_ref[...].astype(F32)
        sg = jax.nn.sigmoid(av)
        da_ref[...] = (df * u_ref[...].astype(F32) * sg * (1.0 + av * (1.0 - sg))).astype(BF16)
        du_ref[...] = (df * av * sg).astype(BF16)

    blk = pl.BlockSpec((bt, bn), lambda i, j: (i, j))
    shape = jax.ShapeDtypeStruct((t, D_FF), BF16)
    return pl.pallas_call(
        body, name="d_ffn_down", grid=(t // bt, D_FF // bn),
        in_specs=[pl.BlockSpec((bt, D_MODEL), lambda i, j: (i, 0)), pl.BlockSpec((bn, D_MODEL), lambda i, j: (j, 0)),
                  blk, blk],
        out_specs=(blk, blk), out_shape=(shape, shape), compiler_params=_params(2),
    )(dy16, w_down, a, u)


def _ffn_down_loss(f, w_down, x2, target):
    t, w = x2.shape
    bt = _row_block(t)

    def body(f_ref, w_ref, x_ref, t_ref, dy_ref, dy16_ref, loss_ref):
        @pl.when(pl.program_id(0) == 0)
        def _():
            loss_ref[...] = jnp.zeros_like(loss_ref)

        err = (x_ref[...] + jnp.dot(f_ref[...], w_ref[...], preferred_element_type=F32)) - t_ref[...]
        dy = err * (1.0 / w)
        dy_ref[...] = dy
        dy16_ref[...] = dy.astype(BF16)
        loss_ref[...] += 0.5 * jnp.sum(jnp.mean(err * err, axis=-1, keepdims=True), axis=0, keepdims=True)

    row = pl.BlockSpec((bt, w), lambda i: (i, 0))
    return pl.pallas_call(
        body, name="ffn_down_loss", grid=(t // bt,),
        in_specs=[pl.BlockSpec((bt, D_FF), lambda i: (i, 0)), pl.BlockSpec((D_FF, w), lambda i: (0, 0)), row, row],
        out_specs=(row, row, pl.BlockSpec((8, LANES), lambda i: (0, 0))),
        out_shape=(jax.ShapeDtypeStruct((t, w), F32), jax.ShapeDtypeStruct((t, w), BF16),
                   jax.ShapeDtypeStruct((8, LANES), F32)),
        compiler_params=_params(1),
    )(f, w_down, x2, target)


def _adamw(parts, w, m, v, *, name):
    _, rows, cols = w.shape
    br = rows if rows <= 512 else 256
    assert rows % br == 0

    def body(p_ref, w_ref, m_ref, v_ref, g_ref, d_ref, nm_ref, nv_ref):
        g = p_ref[0].astype(F32)
        for r in range(1, N_DEV):
            g = g + p_ref[r].astype(F32)
        m2 = ADAM_B1 * m_ref[0] + (1.0 - ADAM_B1) * g
        v2 = ADAM_B2 * v_ref[0] + (1.0 - ADAM_B2) * jnp.square(g)
        m_hat = m2 / (1.0 - ADAM_B1 ** ADAM_STEP)
        v_hat = v2 / (1.0 - ADAM_B2 ** ADAM_STEP)
        g_ref[0] = g
        d_ref[0] = -ADAM_LR * (m_hat / (jnp.sqrt(v_hat) + ADAM_EPS) + ADAM_WD * w_ref[0])
        nm_ref[0] = m2
        nv_ref[0] = v2

    blk = pl.BlockSpec((1, br, cols), lambda i: (0, i, 0))
    shape = jax.ShapeDtypeStruct((1, rows, cols), F32)
    return pl.pallas_call(
        body, name=name, grid=(rows // br,),
        in_specs=[pl.BlockSpec((N_DEV, br, cols), lambda i: (0, i, 0)), blk, blk, blk],
        out_specs=(blk, blk, blk, blk), out_shape=(shape, shape, shape, shape), compiler_params=_params(1),
    )(parts, w, m, v)


_QA, _KA, _VA, _FA, _QD, _KD, _VD = (0, 512), (512, 1024), (1024, 1536), (1536, 1544), (1544, 2056), (2056, 2568), (2568, 3080)
_MAIN_ORDER = (_QA, _QD, _KA, _KD, _VA, _VD)
MAIN_COLS = 3 * D_MODEL
PROJ_COLS = MAIN_COLS + LANES


def _unshard_cols(parts):
    n, r, c = parts.shape
    return jnp.transpose(parts, (1, 0, 2)).reshape(r, n * c)


def _shard_cols(full):
    r, nc = full.shape
    return jnp.transpose(full.reshape(r, N_DEV, nc // N_DEV), (1, 0, 2))


def _w_in_to_kernel(w_full):
    main = jnp.concatenate([w_full[:, a:b] for a, b in _MAIN_ORDER], axis=1)
    forget = jnp.pad(w_full[:, _FA[0]:_FA[1]], ((0, 0), (0, LANES - N_HEADS_FOX)))
    return main, forget


def _w_in_from_kernel(g):
    pos = {span: i * W_GROUP for i, span in enumerate(_MAIN_ORDER)}
    parts = []
    for span in (_QA, _KA, _VA, _FA, _QD, _KD, _VD):
        if span == _FA:
            parts.append(g[:, MAIN_COLS:MAIN_COLS + N_HEADS_FOX])
        else:
            parts.append(g[:, pos[span]:pos[span] + W_GROUP])
    return jnp.concatenate(parts, axis=1)


def _pack_small(vals):
    rows = []
    for name, _, n_rows in SMALL_LAYOUT:
        flat = vals[name].reshape(-1).astype(F32)
        rows.append(jnp.pad(flat, (0, n_rows * LANES - flat.shape[0])).reshape(n_rows, LANES))
    packed = jnp.concatenate(rows, axis=0)
    return jnp.pad(packed, ((0, SMALL_ROWS - packed.shape[0]), (0, 0)))


def _unpack_small(packed, like):
    out = {}
    for name, row, n_rows in SMALL_LAYOUT:
        n = like[name].size
        out[name] = packed[row:row + n_rows].reshape(-1)[:n].reshape(like[name].shape)
    return out


def _device_step(x, target, small, shards):
    bsz, seq, _ = x.shape
    t = bsz * seq
    xf = x.reshape(t, D_MODEL)
    tf = target.reshape(t, D_MODEL)
    row = lambda v: v.reshape(1, -1)
    g_out = jnp.concatenate([small["g_out_fox"], small["g_out_dil"]]).reshape(1, D_MODEL)
    gains = jnp.concatenate(
        [jnp.tile(small[n].reshape(1, HEAD_DIM), (1, 2)) for n in ("g_q_fox", "g_q_dil", "g_k_fox", "g_k_dil")]
        + [jnp.zeros((4, LANES), F32)], axis=0)
    b_pad = jnp.pad(small["b_forget"].reshape(1, N_HEADS_FOX), ((0, 0), (0, LANES - N_HEADS_FOX)))
    rope = _rope_tables(seq)
    tables_qk = _bias_tables(seq, keys_first=False)
    tables_kq = _bias_tables(seq, keys_first=True)

    h1, g_in = _rmsnorm_fwd(xf, row(small["g_mix"]), group=D_MODEL, name="norm_mix",
                            hosted=_Exchange([(shards["w_in"], False)]))
    w_main, w_fa = _w_in_to_kernel(_unshard_cols(g_in))
    w_in_all = jnp.concatenate([w_main, w_fa], axis=1)
    proj = _matmul(h1, w_main, name="in_proj", out_dtype=BF16)
    fa = _matmul(h1, w_fa, name="in_proj_forget", out_dtype=F32)
    qk = _qk_prep_fwd(proj, fa, b_pad, gains, rope, seq)
    late = _Exchange([(shards[n], False) for n in ("w_out", "w_gate", "w_up", "w_down")])
    o, lse, g_out_w, g_gate, g_up, g_down = _attn_fwd(qk, proj, tables_qk, seq, hosted=late)
    w_out = g_out_w.reshape(D_MODEL, D_MODEL)
    w_gu = jnp.concatenate([_unshard_cols(g_gate), _unshard_cols(g_up)], axis=1)
    w_down = g_down.reshape(D_FF, D_MODEL)
    on, x2, h2 = _out_proj_ffn_norm(o, g_out, w_out, xf, row(small["g_ffn"]))
    a, u, f = _ffn_gate_up(h2, w_gu)
    dy, dy16, loss_tile = _ffn_down_loss(f, w_down, x2, tf)

    da, du = _ffn_down_grad(dy16, w_down, a, u)
    gw_down = _matmul_tn(f, dy16, name="gw_down")
    gw_gate = _matmul_tn(h2, da, name="gw_gate")
    gw_up = _matmul_tn(h2, du, name="gw_up")
    dx2, dx2_16, dg_ffn = _norm_input_grad([(da, w_gu, 0), (du, w_gu, 1)], x2, row(small["g_ffn"]), group=D_MODEL,
                                           name="d_ffn_gate_up", out_dtypes=(F32, BF16), resid=dy, k_chunks=2)
    gw_out = _matmul_tn(on, dx2_16, name="gw_out")
    do, dg_out = _norm_input_grad([(dx2_16, w_out, 0)], o, g_out, group=W_GROUP, name="d_out_proj", out_dtypes=(BF16,))

    ffn_grads = _Exchange([
        (gw_out.reshape(N_DEV, D_MODEL // N_DEV, D_MODEL), True),
        (_shard_cols(gw_gate), True),
        (_shard_cols(gw_up), True),
        (gw_down.reshape(N_DEV, D_FF // N_DEV, D_MODEL), True),
    ])
    dq, dk, dv, dqx, dkx, p_out, p_gate, p_up, p_down = _attn_bwd(qk, proj, tables_kq, o, lse, do, seq, hosted=ffn_grads)
    dproj, dgains, db = _qk_prep_bwd(dq, dk, dqx, dkx, dv, proj, fa, b_pad, gains, rope, seq)
    gw_in = _matmul_tn(h1, dproj, name="gw_in")
    in_grad = _Exchange([(_shard_cols(_w_in_from_kernel(gw_in)).astype(BF16), True)])
    dx, dg_mix, p_in = _norm_input_grad([(dproj, w_in_all, 0)], xf, row(small["g_mix"]), group=D_MODEL,
                                        name="d_in_proj", out_dtypes=(F32,), resid=dx2, hosted=in_grad, k_chunks=5)

    fold = lambda rows: jnp.sum(rows[:, :HEAD_DIM] + rows[:, HEAD_DIM:], axis=0)
    half = N_PAIRS // 2
    gsmall = {
        "g_mix": dg_mix, "g_ffn": dg_ffn, "g_out_fox": dg_out[0, :W_GROUP], "g_out_dil": dg_out[0, W_GROUP:],
        "g_q_fox": fold(dgains[0:half]), "g_q_dil": fold(dgains[half:N_PAIRS]),
        "g_k_fox": fold(dgains[N_PAIRS:N_PAIRS + half]), "g_k_dil": fold(dgains[N_PAIRS + half:]),
        "b_forget": db[0, :N_HEADS_FOX],
    }
    packed = _pack_small(gsmall).at[LOSS_ROW].set(loss_tile[0])
    (p_small,) = _exchange("small_exchange", [(packed, False)])
    parts = {"w_in": p_in, "w_out": p_out, "w_gate": p_gate, "w_up": p_up, "w_down": p_down}
    return dx.reshape(x.shape), parts, p_small


def kernel(x, g_mix, w_in, b_forget, g_q_fox, g_k_fox, g_q_dil, g_k_dil, g_out_fox, g_out_dil, w_out, g_ffn, w_gate, w_up, w_down, loss_target, m_g_mix, m_w_in, m_b_forget, m_g_q_fox, m_g_k_fox, m_g_q_dil, m_g_k_dil, m_g_out_fox, m_g_out_dil, m_w_out, m_g_ffn, m_w_gate, m_w_up, m_w_down, v_g_mix, v_w_in, v_b_forget, v_g_q_fox, v_g_k_fox, v_g_q_dil, v_g_k_dil, v_g_out_fox, v_g_out_dil, v_w_out, v_g_ffn, v_w_gate, v_w_up, v_w_down):
    args = dict(locals())
    small_names = [name for name, _, _ in SMALL_LAYOUT]
    big_names = ["w_in", "w_out", "w_gate", "w_up", "w_down"]
    small = {n: args[n][0] for n in small_names}

    shards = {n: args[n][0].astype(BF16) for n in big_names}
    grad_x, parts, p_small = _device_step(x, loss_target, small, shards)

    grads, deltas, new_m, new_v = {}, {}, {}, {}
    for n in big_names:
        grads[n], deltas[n], new_m[n], new_v[n] = _adamw(parts[n], args[n], args["m_" + n], args["v_" + n],
                                                         name="adamw_" + n)
    res = _adamw(p_small, _pack_small(small)[None], _pack_small({n: args["m_" + n][0] for n in small_names})[None],
                 _pack_small({n: args["v_" + n][0] for n in small_names})[None], name="adamw_small")
    loss = res[0][0, LOSS_ROW, 0]
    for dst, packed_res in zip((grads, deltas, new_m, new_v), res):
        for n, val in _unpack_small(packed_res[0], small).items():
            dst[n] = val[None]

    order = ["g_mix", "w_in", "b_forget", "g_q_fox", "g_k_fox", "g_q_dil", "g_k_dil", "g_out_fox", "g_out_dil",
             "w_out", "g_ffn", "w_gate", "w_up", "w_down"]
    return (loss, grad_x, *[grads[n] for n in order], *[deltas[n] for n in order],
            *[new_m[n] for n in order], *[new_v[n] for n in order])
```

```python
import functools
import math

import jax
import jax.numpy as jnp
import numpy as np
from jax import lax
from jax.experimental import pallas as pl
from jax.experimental.pallas import tpu as pltpu

F32 = jnp.float32
BF16 = jnp.bfloat16

D_MODEL = 1024
HEAD_DIM = 64
LANES = 128
N_PAIRS = D_MODEL // LANES
N_HEADS = 2 * N_PAIRS
N_HEADS_FOX = 8
W_GROUP = 512
D_FF = 2816
IN_COLS = 3080
DILATION_PAIRS = ((128, 1), (512, 4), (2048, 16))
ROPE_THETA = 500000.0
ROPE_DIM = 16
ROPE_HALF = ROPE_DIM // 2
EPS = 1e-6
NEG = -1e30
LOG2E = 1.4426950408889634
LN2 = 0.6931471805599453
AUG_ONE = 0
AUG_C = 3
N_DEV = 8

ADAM_LR = 0.001
ADAM_B1 = 0.9
ADAM_B2 = 0.999
ADAM_EPS = 1e-08
ADAM_WD = 0.01
ADAM_STEP = 10

ROW_BLOCK = 512
ATT_BLOCK = 256
ATT_GROUP = 4
VMEM_LIMIT = 56 * 1024 * 1024
MATMUL_VMEM_BUDGET = 44 * 1024 * 1024

SMALL_ROWS = 32
SMALL_LAYOUT = (("g_mix", 0, 8), ("g_ffn", 8, 8), ("g_out_fox", 16, 4), ("g_out_dil", 20, 4),
                ("g_q_fox", 24, 1), ("g_k_fox", 25, 1), ("g_q_dil", 26, 1), ("g_k_dil", 27, 1),
                ("b_forget", 28, 1))
LOSS_ROW = 29


def _params(n_grid):
    return pltpu.CompilerParams(dimension_semantics=("arbitrary",) * n_grid, vmem_limit_bytes=VMEM_LIMIT)


def _divisor_block(n, cap):
    best = None
    for b in range(LANES, min(n, cap) + 1, LANES):
        if n % b == 0:
            best = b
    assert best is not None, n
    return best


def _split_dot(a, b_exact, terms):
    acc = None
    rest = a
    for _ in range(terms):
        hi = rest.astype(BF16)
        part = jnp.dot(hi, b_exact, preferred_element_type=F32)
        acc = part if acc is None else acc + part
        rest = rest - hi.astype(F32)
    return acc


def _split_dot_rhs(a_exact, b, terms):
    acc = None
    rest = b
    for _ in range(terms):
        hi = rest.astype(BF16)
        part = jnp.dot(a_exact, hi, preferred_element_type=F32)
        acc = part if acc is None else acc + part
        rest = rest - hi.astype(F32)
    return acc


def _split_dot_nt(a_exact, b, terms):
    acc = None
    rest = b
    for _ in range(terms):
        hi = rest.astype(BF16)
        part = _dot_nt(a_exact, hi)
        acc = part if acc is None else acc + part
        rest = rest - hi.astype(F32)
    return acc


def _dot_nt(a, b):
    return lax.dot_general(a, b, (((1,), (1,)), ((), ())), preferred_element_type=F32)


def _dot_tn(a, b):
    return lax.dot_general(a, b, (((0,), (0,)), ((), ())), preferred_element_type=F32)


class _Exchange:
    def __init__(self, items):
        self.items = items
        self.n = len(items)
        self.arrays = [a for a, _ in items]
        self.out_shape = [jax.ShapeDtypeStruct((N_DEV,) + tuple(a.shape[1:] if sc else a.shape), a.dtype)
                          for a, sc in items]
        self.specs = [pl.BlockSpec(memory_space=pl.ANY)] * self.n
        self.scratch = [pltpu.SemaphoreType.DMA((self.n, N_DEV - 1)), pltpu.SemaphoreType.DMA((self.n, N_DEV - 1)),
                        pltpu.SemaphoreType.DMA((self.n,))]

    def run(self, ins, outs, sems, first, last, compute):
        send_sems, recv_sems, local_sems = sems
        x, y, c = lax.axis_index("x"), lax.axis_index("y"), lax.axis_index("c")
        me = 4 * x + 2 * y + c
        local, remote = [], []
        for k, (_, scatter) in enumerate(self.items):
            own = ins[k].at[me] if scatter else ins[k]
            local.append(pltpu.make_async_copy(own, outs[k].at[me], local_sems.at[k]))
        for r in range(1, N_DEV):
            px = 1 - x if r & 4 else x
            py = 1 - y if r & 2 else y
            pc = 1 - c if r & 1 else c
            peer = 4 * px + 2 * py + pc
            for k, (_, scatter) in enumerate(self.items):
                src = ins[k].at[peer] if scatter else ins[k]
                remote.append(pltpu.make_async_remote_copy(
                    src_ref=src, dst_ref=outs[k].at[me],
                    send_sem=send_sems.at[k, r - 1], recv_sem=recv_sems.at[k, r - 1],
                    device_id=(px, py, pc), device_id_type=pl.DeviceIdType.MESH))

        def start():
            for cp in local + remote:
                cp.start()

        def finish():
            for cp in remote:
                cp.wait_recv()
            for cp in remote:
                cp.wait_send()
            for cp in local:
                cp.wait()

        _run_phases(first, last, start, compute, finish)


def _run_phases(first, last, start, compute, finish):
    if first is None:
        start()
        compute()
        finish()
    else:
        pl.when(first)(start)
        compute()
        pl.when(last)(finish)


class _ChipGather(_Exchange):
    def run(self, ins, outs, sems, first, last, compute):
        send_sems, recv_sems, local_sems = sems
        x, y, c = lax.axis_index("x"), lax.axis_index("y"), lax.axis_index("c")
        sibling = (x, y, 1 - c)
        chips = [(1 - x, y), (x, 1 - y), (1 - x, 1 - y)]
        slot = lambda px, py, pc: 4 * px + 2 * py + pc

        def copy(k, n, src, dst_slot, to):
            return pltpu.make_async_remote_copy(
                src_ref=src, dst_ref=outs[k].at[dst_slot], send_sem=send_sems.at[k, n], recv_sem=recv_sems.at[k, n],
                device_id=to, device_id_type=pl.DeviceIdType.MESH)

        local, own, passed, arrivals = [], [], [], []
        for k in range(self.n):
            me = slot(x, y, c)
            local.append(pltpu.make_async_copy(ins[k], outs[k].at[me], local_sems.at[k]))
            own.append(copy(k, 0, ins[k], me, sibling))
            arrivals.append(copy(k, 0, ins[k], slot(*sibling), sibling))
            for j, chip in enumerate(chips):
                theirs = slot(*chip, c)
                own.append(copy(k, 1 + j, ins[k], me, (*chip, c)))
                passed.append((copy(k, 1 + j, ins[k], theirs, sibling),
                               copy(k, 4 + j, outs[k].at[theirs], theirs, sibling)))
                arrivals.append(copy(k, 4 + j, ins[k], slot(*chip, 1 - c), sibling))

        def start():
            for cp in local + own:
                cp.start()

        def finish():
            for landed, onward in passed:
                landed.wait_recv()
                onward.start()
            for cp in arrivals:
                cp.wait_recv()
            for cp in own + [onward for _, onward in passed]:
                cp.wait_send()
            for cp in local:
                cp.wait()

        _run_phases(first, last, start, compute, finish)


def _grid_ends(grid):
    ids = [pl.program_id(d) for d in range(len(grid))]
    first = functools.reduce(jnp.logical_and, [i == 0 for i in ids])
    last = functools.reduce(jnp.logical_and, [i == g - 1 for i, g in zip(ids, grid)])
    return first, last


def _host(core, n_in, n_out, n_scratch, hosted, grid):
    if hosted is None:
        return core
    nh = hosted.n

    def body(*refs):
        ins, rest = refs[:n_in], refs[n_in:]
        h_ins, rest = rest[:nh], rest[nh:]
        outs, rest = rest[:n_out], rest[n_out:]
        h_outs, rest = rest[:nh], rest[nh:]
        scratch, sems = rest[:n_scratch], rest[n_scratch:]
        first, last = _grid_ends(grid)
        hosted.run(h_ins, h_outs, sems, first, last, lambda: core(*ins, *outs, *scratch))

    return body


def _hosted_parts(hosted):
    if hosted is None:
        return [], [], [], []
    return list(hosted.specs), list(hosted.out_shape), list(hosted.arrays), list(hosted.scratch)


def _exchange(name, items):
    ex = _Exchange(items)
    n = ex.n

    def body(*refs):
        ex.run(refs[:n], refs[n:2 * n], refs[2 * n:], None, None, lambda: None)

    return pl.pallas_call(
        body, name=name, out_shape=tuple(ex.out_shape), in_specs=ex.specs, out_specs=tuple(ex.specs),
        scratch_shapes=ex.scratch,
    )(*ex.arrays)


def _matmul_blocks(t, k, n, a_bytes, o_bytes):
    for bt, cap in ((1024, 1408), (1024, 512), (512, 512)):
        if t % bt:
            continue
        bn = _divisor_block(n, cap)
        if 2 * (bt * k * a_bytes + bn * k * 2 + bt * bn * o_bytes) <= MATMUL_VMEM_BUDGET:
            return bt, bn
    return ROW_BLOCK, _divisor_block(n, 256)


def _matmul(a, w, *, name, out_dtype):
    t, k = a.shape
    n = w.shape[1]
    assert w.shape[0] == k
    bt, bn = _matmul_blocks(t, k, n, a.dtype.itemsize, jnp.dtype(out_dtype).itemsize)

    def body(a_ref, w_ref, o_ref):
        o_ref[...] = jnp.dot(a_ref[...], w_ref[...], preferred_element_type=F32).astype(o_ref.dtype)

    return pl.pallas_call(
        body, name=name, grid=(t // bt, n // bn),
        in_specs=[pl.BlockSpec((bt, k), lambda i, j: (i, 0)), pl.BlockSpec((k, bn), lambda i, j: (0, j))],
        out_specs=pl.BlockSpec((bt, bn), lambda i, j: (i, j)),
        out_shape=jax.ShapeDtypeStruct((t, n), out_dtype), compiler_params=_params(2),
    )(a, w)


def _matmul_tn(a, b, *, name):
    t, m = a.shape
    n = b.shape[1]
    bt = 2048 if t % 2048 == 0 else ROW_BLOCK
    bm = _divisor_block(m, 1408)
    bn = _divisor_block(n, 1408)
    steps = t // bt

    def body(a_ref, b_ref, o_ref, acc):
        step = pl.program_id(2)

        @pl.when(step == 0)
        def _():
            acc[...] = jnp.zeros_like(acc)

        acc[...] += _dot_tn(a_ref[...], b_ref[...])

        @pl.when(step == steps - 1)
        def _():
            o_ref[...] = acc[...].astype(o_ref.dtype)

    return pl.pallas_call(
        body, name=name, grid=(m // bm, n // bn, steps),
        in_specs=[pl.BlockSpec((bt, bm), lambda i, j, s: (s, i)), pl.BlockSpec((bt, bn), lambda i, j, s: (s, j))],
        out_specs=pl.BlockSpec((bm, bn), lambda i, j, s: (i, j)),
        out_shape=jax.ShapeDtypeStruct((m, n), BF16), scratch_shapes=[pltpu.VMEM((bm, bn), F32)],
        compiler_params=_params(3),
    )(a, b)


def _rmsnorm_fwd(x, g, *, group, name, hosted=None):
    t, w = x.shape
    bt = ROW_BLOCK

    def body(x_ref, g_ref, o_ref):
        for s in range(0, w, group):
            xs = x_ref[:, s:s + group].astype(F32)
            r = lax.rsqrt(jnp.mean(xs * xs, axis=-1, keepdims=True) + EPS)
            o_ref[:, s:s + group] = (xs * r * g_ref[:, s:s + group]).astype(o_ref.dtype)

    grid = (t // bt,)
    h_specs, h_shapes, h_args, h_scratch = _hosted_parts(hosted)
    res = pl.pallas_call(
        _host(body, 2, 1, 0, hosted, grid), name=name, grid=grid,
        in_specs=[pl.BlockSpec((bt, w), lambda i: (i, 0)), pl.BlockSpec((1, w), lambda i: (0, 0))] + h_specs,
        out_specs=tuple([pl.BlockSpec((bt, w), lambda i: (i, 0))] + h_specs),
        out_shape=tuple([jax.ShapeDtypeStruct((t, w), BF16)] + h_shapes),
        scratch_shapes=h_scratch, compiler_params=_params(1),
    )(x, g, *h_args)
    return res if hosted else res[0]


def _norm_input_grad(terms, x, g, *, group, name, out_dtypes, resid=None, hosted=None, k_chunks=1):
    t, w = x.shape
    n_terms = len(terms)
    kc = [a.shape[1] // k_chunks for a, _, _ in terms]
    per_row = sum(c * a.dtype.itemsize for c, (a, _, _) in zip(kc, terms)) + w * (x.dtype.itemsize + 4 * (resid is not None))
    per_row += w * sum(jnp.dtype(dt).itemsize for dt in out_dtypes)
    fixed = 2 * sum(wt.shape[0] * c * 2 for c, (_, wt, _) in zip(kc, terms))
    bt = next(b for b in (1024, 512, 256, 128)
              if t % b == 0 and fixed + 2 * b * per_row + 5 * b * w * 4 <= MATMUL_VMEM_BUDGET)
    n_in = 2 * n_terms + 2 + (resid is not None)
    grid = (t // bt, k_chunks)

    def body(*refs):
        x_ref, g_ref = refs[2 * n_terms], refs[2 * n_terms + 1]
        dx_refs, dg_ref, dh_ref = refs[n_in:-2], refs[-2], refs[-1]
        chunk = pl.program_id(1)

        @pl.when((pl.program_id(0) == 0) & (chunk == 0))
        def _():
            dg_ref[...] = jnp.zeros_like(dg_ref)

        part = None
        for k in range(n_terms):
            term = _dot_nt(refs[2 * k][...], refs[2 * k + 1][...])
            part = term if part is None else part + term

        @pl.when(chunk == 0)
        def _():
            dh_ref[...] = part

        @pl.when(chunk > 0)
        def _():
            dh_ref[...] += part

        @pl.when(chunk == k_chunks - 1)
        def _():
            for s in range(0, w, group):
                xs = x_ref[:, s:s + group].astype(F32)
                dhs = dh_ref[:, s:s + group]
                r = lax.rsqrt(jnp.mean(xs * xs, axis=-1, keepdims=True) + EPS)
                xh = xs * r
                dg_ref[:, s:s + group] += jnp.sum(dhs * xh, axis=0, keepdims=True)
                dxh = dhs * g_ref[:, s:s + group]
                dx = r * (dxh - xh * jnp.mean(dxh * xh, axis=-1, keepdims=True))
                if resid is not None:
                    dx = refs[n_in - 1][:, s:s + group] + dx
                for dx_ref in dx_refs:
                    dx_ref[:, s:s + group] = dx.astype(dx_ref.dtype)

    row = pl.BlockSpec((bt, w), lambda i, k: (i, 0))
    vec = pl.BlockSpec((1, w), lambda i, k: (0, 0))
    in_specs, args = [], []
    for c, (a, wt, col_block) in zip(kc, terms):
        in_specs += [pl.BlockSpec((bt, c), lambda i, k: (i, k)),
                     pl.BlockSpec((wt.shape[0], c), lambda i, k, cb=col_block: (0, cb * k_chunks + k))]
        args += [a, wt]
    in_specs += [row, vec] + ([row] if resid is not None else [])
    args += [x, g] + ([resid] if resid is not None else [])
    h_specs, h_shapes, h_args, h_scratch = _hosted_parts(hosted)
    return pl.pallas_call(
        _host(body, n_in, len(out_dtypes) + 1, 1, hosted, grid), name=name, grid=grid, in_specs=in_specs + h_specs,
        out_specs=tuple([row] * len(out_dtypes) + [vec] + h_specs),
        out_shape=tuple([jax.ShapeDtypeStruct((t, w), dt) for dt in out_dtypes] + [jax.ShapeDtypeStruct((1, w), F32)]
                        + h_shapes),
        scratch_shapes=[pltpu.VMEM((bt, w), F32)] + h_scratch, compiler_params=_params(2),
    )(*args, *h_args)


def _tile_plan(tile):
    is_q = tile < N_PAIRS
    is_dil = (tile % N_PAIRS) >= N_PAIRS // 2
    return is_q, is_dil, (0 if is_q else 2) + (1 if is_dil else 0)


def _segment_ones():
    lane = np.arange(LANES)
    return jnp.asarray((lane[:, None] // HEAD_DIM) == (lane[None, :] // HEAD_DIM), BF16)


def _rope_tables(seq):
    inv_freq = jnp.power(jnp.float32(ROPE_THETA), -jnp.arange(ROPE_HALF, dtype=F32) * 2.0 / ROPE_DIM)
    ang = jnp.arange(seq).astype(F32)[:, None] * inv_freq[None, :]
    cos, sin = jnp.cos(ang), jnp.sin(ang)
    ones = jnp.ones((seq, HEAD_DIM - ROPE_DIM), F32)
    zeros = jnp.zeros((seq, HEAD_DIM - ROPE_DIM), F32)
    zh = jnp.zeros((seq, ROPE_HALF), F32)
    cos_t = jnp.concatenate([cos, cos, ones], axis=1)
    sin_a = jnp.concatenate([-sin, zh, zeros], axis=1)
    sin_b = jnp.concatenate([zh, sin, zeros], axis=1)
    return tuple(jnp.tile(tab, (1, 2)) for tab in (cos_t, sin_a, sin_b))


def _log_sigmoid(z):
    return jnp.minimum(z, 0.0) - jnp.log1p(jnp.exp(-jnp.abs(z)))


def _qk_prep_fwd(proj, fa, b_pad, gains, rope, seq):
    t = proj.shape[0]
    bt = ROW_BLOCK
    nsb = seq // bt
    seg = _segment_ones()
    rr = np.arange(bt)
    tri = jnp.asarray(rr[:, None] >= rr[None, :], BF16)

    def body(p_ref, fa_ref, b_ref, g_ref, cos_ref, sa_ref, sb_ref, seg_ref, tri_ref, qk_ref, carry):
        @pl.when(pl.program_id(0) % nsb == 0)
        def _():
            carry[...] = jnp.zeros_like(carry)

        lane = lax.broadcasted_iota(jnp.int32, (bt, LANES), 1)
        logf = jnp.where(lane < N_HEADS_FOX, _log_sigmoid(fa_ref[...] + b_ref[...]), 0.0)
        cblk = _split_dot_rhs(tri_ref[...], logf, 3) + carry[0:1, :]
        carry[0:1, :] = cblk[bt - 1:bt, :]
        c_terms = []
        rest = cblk * LOG2E
        for _ in range(3):
            term = rest.astype(BF16).astype(F32)
            c_terms.append(term)
            rest = rest - term

        for tile in range(2 * N_PAIRS):
            is_q, is_dil, grow = _tile_plan(tile)
            pair = tile % N_PAIRS
            xs = p_ref[:, tile * LANES:(tile + 1) * LANES].astype(F32)
            r = lax.rsqrt(_split_dot(xs * xs, seg_ref[...], 2) * (1.0 / HEAD_DIM) + EPS)
            yv = xs * r * g_ref[grow:grow + 1, :]
            if is_dil:
                yv = (yv * cos_ref[...] + pltpu.roll(yv, LANES - ROPE_HALF, 1) * sa_ref[...]
                      + pltpu.roll(yv, ROPE_HALF, 1) * sb_ref[...])
            if is_q:
                yv = yv * (HEAD_DIM ** -0.5 * LOG2E)
            for e in range(2):
                head = 2 * pair + e
                other = HEAD_DIM * (1 - e)
                aug = jnp.zeros((bt, LANES), F32)
                if not is_dil:
                    ones_at = other + (AUG_ONE if is_q else AUG_C)
                    c_at = other + (AUG_C if is_q else AUG_ONE)
                    aug = jnp.where((lane >= ones_at) & (lane < ones_at + 3), 1.0, aug)
                    for n, term in enumerate(c_terms):
                        col = term[:, head:head + 1]
                        aug = jnp.where(lane == c_at + n, col if is_q else -col, aug)
                mine = (lane < HEAD_DIM) if e == 0 else (lane >= HEAD_DIM)
                dst = ((0 if is_q else N_HEADS) + head) * LANES
                qk_ref[:, dst:dst + LANES] = jnp.where(mine, yv, aug).astype(BF16)

    row128 = pl.BlockSpec((bt, LANES), lambda i: (i, 0))
    rope_spec = pl.BlockSpec((bt, LANES), lambda i: (i % nsb, 0))
    const = lambda shape: pl.BlockSpec(shape, lambda i: (0, 0))
    return pl.pallas_call(
        body, name="qk_prep_fwd", grid=(t // bt,),
        in_specs=[pl.BlockSpec((bt, 2 * D_MODEL), lambda i: (i, 0)), row128, const((1, LANES)), const((8, LANES)),
                  rope_spec, rope_spec, rope_spec, const((LANES, LANES)), const((bt, bt))],
        out_specs=pl.BlockSpec((bt, 2 * N_HEADS * LANES), lambda i: (i, 0)),
        out_shape=jax.ShapeDtypeStruct((t, 2 * N_HEADS * LANES), BF16),
        scratch_shapes=[pltpu.VMEM((8, LANES), F32)], compiler_params=_params(1),
    )(proj, fa, b_pad, gains, *rope, seg, tri)


def _qk_prep_bwd(dq, dk, dqx, dkx, dv, proj, fa, b_pad, gains, rope, seq):
    t = proj.shape[0]
    bt = ROW_BLOCK
    nsb = seq // bt
    nblk = t // bt
    seg = _segment_ones()
    rr = np.arange(bt)
    triu = jnp.asarray(rr[:, None] <= rr[None, :], BF16)

    def body(dq_ref, dk_ref, dqx_ref, dkx_ref, dv_ref, p_ref, fa_ref, b_ref, g_ref, cos_ref, sa_ref, sb_ref, seg_ref,
             triu_ref, dp_ref, dg_ref, db_ref, carry):
        step = pl.program_id(0)

        @pl.when(step == 0)
        def _():
            dg_ref[...] = jnp.zeros_like(dg_ref)
            db_ref[...] = jnp.zeros_like(db_ref)

        @pl.when(step % nsb == 0)
        def _():
            carry[...] = jnp.zeros_like(carry)

        for tile in range(2 * N_PAIRS):
            is_q, is_dil, grow = _tile_plan(tile)
            cols = slice(tile * LANES, (tile + 1) * LANES)
            src = dq_ref if is_q else dk_ref
            half = slice((tile % N_PAIRS) * LANES, (tile % N_PAIRS + 1) * LANES)
            dy = src[:, half]
            dy = dy * (HEAD_DIM ** -0.5 if is_q else LN2)
            if is_dil:
                dy = (dy * cos_ref[...] + pltpu.roll(dy * sa_ref[...], ROPE_HALF, 1)
                      + pltpu.roll(dy * sb_ref[...], LANES - ROPE_HALF, 1))
            xs = p_ref[:, cols].astype(F32)
            r = lax.rsqrt(_split_dot(xs * xs, seg_ref[...], 2) * (1.0 / HEAD_DIM) + EPS)
            xh = xs * r
            dg_ref[tile:tile + 1, :] += jnp.sum(dy * xh, axis=0, keepdims=True)
            dxh = dy * g_ref[grow:grow + 1, :]
            seg_mean = _split_dot(dxh * xh, seg_ref[...], 2) * (1.0 / HEAD_DIM)
            dp_ref[:, cols] = (r * (dxh - xh * seg_mean)).astype(BF16)

        lane = lax.broadcasted_iota(jnp.int32, (bt, LANES), 1)
        dc = jnp.zeros((bt, LANES), F32)
        for h in range(N_HEADS_FOX):
            other = (h // 2) * LANES + HEAD_DIM * (1 - h % 2)
            row_sum = dqx_ref[:, other + AUG_C:other + AUG_C + 1]
            col_sum = dkx_ref[:, other + AUG_ONE:other + AUG_ONE + 1]
            dc = jnp.where(lane == h, row_sum - col_sum, dc)
        dlogf = _split_dot_rhs(triu_ref[...], dc, 3) + carry[0:1, :]
        carry[0:1, :] = dlogf[0:1, :]
        z = fa_ref[...] + b_ref[...]
        dfa = dlogf * (1.0 / (1.0 + jnp.exp(z)))
        db_ref[0:1, :] += jnp.sum(dfa, axis=0, keepdims=True)
        dp_ref[:, 2 * D_MODEL:MAIN_COLS] = dv_ref[...]
        dp_ref[:, MAIN_COLS:PROJ_COLS] = dfa.astype(BF16)

    rev = lambda i: nblk - 1 - i
    row = lambda w: pl.BlockSpec((bt, w), lambda i: (rev(i), 0))
    rope_spec = pl.BlockSpec((bt, LANES), lambda i: (rev(i) % nsb, 0))
    const = lambda shape: pl.BlockSpec(shape, lambda i: (0, 0))
    return pl.pallas_call(
        body, name="qk_prep_bwd", grid=(nblk,),
        in_specs=[row(D_MODEL), row(D_MODEL), row(W_GROUP), row(W_GROUP), row(D_MODEL), row(2 * D_MODEL), row(LANES),
                  const((1, LANES)), const((8, LANES)), rope_spec, rope_spec, rope_spec, const((LANES, LANES)),
                  const((bt, bt))],
        out_specs=(row(PROJ_COLS), const((2 * N_PAIRS, LANES)), const((8, LANES))),
        out_shape=(jax.ShapeDtypeStruct((t, PROJ_COLS), BF16),
                   jax.ShapeDtypeStruct((2 * N_PAIRS, LANES), F32), jax.ShapeDtypeStruct((8, LANES), F32)),
        scratch_shapes=[pltpu.VMEM((8, LANES), F32)], compiler_params=_params(1),
    )(dq, dk, dqx, dkx, dv, proj, fa, b_pad, gains, *rope, seg, triu)


def _bias_tables(seq, keys_first):
    nb = seq // ATT_BLOCK
    idx = jnp.arange(ATT_BLOCK)
    q_idx, k_idx = (idx[None, None, :], idx[None, :, None]) if keys_first else (idx[None, :, None], idx[None, None, :])
    dist = jnp.arange(nb)[:, None, None] * ATT_BLOCK + q_idx - k_idx
    causal = dist >= 0
    count = jnp.zeros(dist.shape, jnp.int32)
    for window, dilation in DILATION_PAIRS:
        count = count + (causal & (dist % dilation == 0) & (dist <= window)).astype(jnp.int32)
    fox = jnp.where(causal, 0.0, NEG).astype(F32)
    dil = jnp.where(count == 3, math.log2(3.0), jnp.where(count == 2, 1.0, jnp.where(count == 1, 0.0, NEG)))
    return jnp.stack([fox, dil.astype(F32)], axis=0)


def _attn_specs(seq):
    nb = seq // ATT_BLOCK
    col = lambda off: pl.BlockSpec((seq, LANES), lambda b, j: (b, off + j))
    heads = lambda off: pl.BlockSpec((seq, 2 * LANES), lambda b, j: (b, off + j))
    table_spec = pl.BlockSpec((1, nb, ATT_BLOCK, ATT_BLOCK), lambda b, j: (j // (N_PAIRS // 2), 0, 0, 0))
    return col, heads, table_spec


def _head_lanes(e, shape, axis):
    pos = lax.broadcasted_iota(jnp.int32, shape, axis)
    return pos < HEAD_DIM if e == 0 else pos >= HEAD_DIM


def _attn_fwd(qk, proj, tables, seq, hosted=None):
    t = qk.shape[0]
    nb = seq // ATT_BLOCK
    blk = ATT_BLOCK

    def body(q_ref, k_ref, v_ref, tab_ref, o_ref, lse_ref):
        mine = [_head_lanes(e, (seq, LANES), 1) for e in range(2)]
        lane = lax.broadcasted_iota(jnp.int32, (seq, LANES), 1)
        v_aug = [jnp.where(mine[e], v_ref[...], (lane == HEAD_DIM * (1 - e)).astype(BF16)) for e in range(2)]
        for i in range(nb):
            rows = slice(i * blk, (i + 1) * blk)
            n_keys = (i + 1) * blk
            out, lse = [], []
            for e in range(2):
                heads_e = slice(e * LANES, (e + 1) * LANES)
                s = _dot_nt(q_ref[rows, heads_e], k_ref[0:n_keys, heads_e])
                s = jnp.concatenate([s[:, jj * blk:(jj + 1) * blk] + tab_ref[0, i - jj] for jj in range(i + 1)], axis=1)
                m = jnp.max(s, axis=1, keepdims=True)
                acc = jnp.dot(jnp.exp2(s - m).astype(BF16), v_aug[e][0:n_keys], preferred_element_type=F32)
                ones_at = HEAD_DIM * (1 - e)
                l = acc[:, ones_at:ones_at + 1]
                out.append(acc / l)
                lse.append(m + jnp.log2(l))
            o_ref[rows, :] = jnp.where(mine[0][rows], out[0], out[1]).astype(o_ref.dtype)
            lse_ref[rows, :] = jnp.where(mine[0][rows], lse[0], lse[1])

    col, heads, table_spec = _attn_specs(seq)
    grid = (t // seq, N_PAIRS)
    h_specs, h_shapes, h_args, h_scratch = _hosted_parts(hosted)
    return pl.pallas_call(
        _host(body, 4, 2, 0, hosted, grid), name="attn_fwd", grid=grid,
        in_specs=[heads(0), heads(N_PAIRS), col(2 * N_PAIRS), table_spec] + h_specs,
        out_specs=tuple([col(0), col(0)] + h_specs),
        out_shape=tuple([jax.ShapeDtypeStruct((t, D_MODEL), BF16), jax.ShapeDtypeStruct((t, D_MODEL), F32)] + h_shapes),
        scratch_shapes=h_scratch, compiler_params=_params(2),
    )(qk, qk, proj, tables, *h_args)


def _attn_bwd(qk, proj, tables, o, lse, do, seq, hosted=None):
    t = qk.shape[0]
    nb = seq // ATT_BLOCK
    blk = ATT_BLOCK
    group = math.gcd(nb, ATT_GROUP)

    def body(q_ref, k_ref, v_ref, tab_ref, o_ref, lse_ref, do_ref,
             dq_ref, dk_ref, dv_ref, dqx_ref, dkx_ref, dk_acc, dv_acc):
        mine = [_head_lanes(e, (blk, LANES), 1) for e in range(2)]
        top = _head_lanes(0, (LANES, blk), 0)
        head_rows = lax.broadcasted_iota(jnp.int32, (8, LANES), 0)
        head_of_lane = lax.broadcasted_iota(jnp.int32, (8, LANES), 1) // HEAD_DIM
        head_sel = (head_rows == head_of_lane).astype(BF16)
        dk_acc[...] = jnp.zeros_like(dk_acc)
        dv_acc[...] = jnp.zeros_like(dv_acc)

        def block_rows(i):
            return pl.ds(pl.multiple_of(i * blk, blk), blk)

        def q_group(g, _):
            base = g * group
            qs, doe, delta, lse_e = [], [], [], []
            for b in range(group):
                rows = block_rows(base + b)
                qs.append([q_ref[rows, e * LANES:(e + 1) * LANES] for e in range(2)])
                do_blk = do_ref[rows, :]
                doe.append([jnp.where(mine[e], do_blk, jnp.zeros_like(do_blk)) for e in range(2)])
                delta_t = _split_dot_nt(head_sel, do_blk.astype(F32) * o_ref[rows, :].astype(F32), 3)
                lse_t = _split_dot_nt(head_sel, lse_ref[rows, :], 3) * (1.0 / HEAD_DIM)
                delta.append([delta_t[e:e + 1, :] for e in range(2)])
                lse_e.append([lse_t[e:e + 1, :] for e in range(2)])

            def key_block(dq_t, jj, members):
                krows = block_rows(jj)
                v = v_ref[krows, :]
                dq_t = [list(d) for d in dq_t]
                dv_part = None
                for e in range(2):
                    k_e = k_ref[krows, e * LANES:(e + 1) * LANES]
                    dk_part = None
                    for b, dist in members:
                        p_t = jnp.exp2(_dot_nt(k_e, qs[b][e]) + tab_ref[0, dist] - lse_e[b][e])
                        ds_t = (p_t * (_dot_nt(v, doe[b][e]) - delta[b][e])).astype(BF16)
                        part = jnp.dot(p_t.astype(BF16), doe[b][e], preferred_element_type=F32)
                        dv_part = part if dv_part is None else dv_part + part
                        part = jnp.dot(ds_t, qs[b][e], preferred_element_type=F32)
                        dk_part = part if dk_part is None else dk_part + part
                        dq_t[b][e] = dq_t[b][e] + _dot_tn(k_e, ds_t)
                    dk_acc[e, krows, :] += dk_part
                dv_acc[krows, :] += dv_part
                return tuple(tuple(d) for d in dq_t)

            zacc = jnp.zeros((LANES, blk), F32)
            dq_t = tuple((zacc, zacc) for _ in range(group))
            dq_t = lax.fori_loop(
                0, base, lambda jj, st: key_block(st, jj, [(b, base + b - jj) for b in range(group)]), dq_t)
            for a in range(group):
                dq_t = key_block(dq_t, base + a, [(b, b - a) for b in range(a, group)])
            for b in range(group):
                rows = block_rows(base + b)
                dq_ref[rows, :] = jnp.where(top, dq_t[b][0], dq_t[b][1]).T
                dqx_ref[rows, :] = jnp.where(top, dq_t[b][1], dq_t[b][0]).T
            return 0

        lax.fori_loop(0, nb // group, q_group, 0)
        lo = _head_lanes(0, (seq, LANES), 1)
        dk_ref[...] = jnp.where(lo, dk_acc[0], dk_acc[1])
        dkx_ref[...] = jnp.where(lo, dk_acc[1], dk_acc[0])
        dv_ref[...] = dv_acc[...].astype(dv_ref.dtype)

    col, heads, table_spec = _attn_specs(seq)
    grid = (t // seq, N_PAIRS)
    h_specs, h_shapes, h_args, h_scratch = _hosted_parts(hosted)
    f32_out = jax.ShapeDtypeStruct((t, D_MODEL), F32)
    return pl.pallas_call(
        _host(body, 7, 5, 2, hosted, grid), name="attn_bwd", grid=grid,
        in_specs=[heads(0), heads(N_PAIRS), col(2 * N_PAIRS), table_spec, col(0), col(0), col(0)] + h_specs,
        out_specs=tuple([col(0)] * 5 + h_specs),
        out_shape=tuple([f32_out, f32_out, jax.ShapeDtypeStruct((t, D_MODEL), BF16), f32_out, f32_out] + h_shapes),
        scratch_shapes=[pltpu.VMEM((2, seq, LANES), F32), pltpu.VMEM((seq, LANES), F32)] + h_scratch,
        compiler_params=_params(2),
    )(qk, qk, proj, tables, o, lse, do, *h_args)


def _row_block(t):
    return 1024 if t % 1024 == 0 else ROW_BLOCK


def _out_proj_ffn_norm(o, g_out, w_out, x, g_ffn):
    t = o.shape[0]
    bt = _row_block(t)

    def body(o_ref, go_ref, w_ref, x_ref, gf_ref, on_ref, x2_ref, h2_ref):
        for s in range(0, D_MODEL, W_GROUP):
            os_ = o_ref[:, s:s + W_GROUP].astype(F32)
            r = lax.rsqrt(jnp.mean(os_ * os_, axis=-1, keepdims=True) + EPS)
            on_ref[:, s:s + W_GROUP] = (os_ * r * go_ref[:, s:s + W_GROUP]).astype(BF16)
        x2 = x_ref[...] + jnp.dot(on_ref[...], w_ref[...], preferred_element_type=F32)
        x2_ref[...] = x2
        r2 = lax.rsqrt(jnp.mean(x2 * x2, axis=-1, keepdims=True) + EPS)
        h2_ref[...] = (x2 * r2 * gf_ref[...]).astype(BF16)

    row = pl.BlockSpec((bt, D_MODEL), lambda i: (i, 0))
    vec = pl.BlockSpec((1, D_MODEL), lambda i: (0, 0))
    return pl.pallas_call(
        body, name="out_proj", grid=(t // bt,),
        in_specs=[row, vec, pl.BlockSpec((D_MODEL, D_MODEL), lambda i: (0, 0)), row, vec],
        out_specs=(row, row, row),
        out_shape=(jax.ShapeDtypeStruct((t, D_MODEL), BF16), jax.ShapeDtypeStruct((t, D_MODEL), F32),
                   jax.ShapeDtypeStruct((t, D_MODEL), BF16)),
        compiler_params=_params(1),
    )(o, g_out, w_out, x, g_ffn)


def _ffn_gate_up(h2, w_gu):
    t = h2.shape[0]
    bt = _row_block(t)
    bn = _divisor_block(D_FF, 1408)
    nj = D_FF // bn

    def body(h_ref, wg_ref, wu_ref, a_ref, u_ref, f_ref):
        a = jnp.dot(h_ref[...], wg_ref[...], preferred_element_type=F32)
        u = jnp.dot(h_ref[...], wu_ref[...], preferred_element_type=F32)
        a_ref[...] = a.astype(BF16)
        u_ref[...] = u.astype(BF16)
        f_ref[...] = (a * jax.nn.sigmoid(a) * u).astype(BF16)

    blk = pl.BlockSpec((bt, bn), lambda i, j: (i, j))
    shape = jax.ShapeDtypeStruct((t, D_FF), BF16)
    return pl.pallas_call(
        body, name="ffn_gate_up", grid=(t // bt, nj),
        in_specs=[pl.BlockSpec((bt, D_MODEL), lambda i, j: (i, 0)), pl.BlockSpec((D_MODEL, bn), lambda i, j: (0, j)),
                  pl.BlockSpec((D_MODEL, bn), lambda i, j: (0, nj + j))],
        out_specs=(blk, blk, blk), out_shape=(shape, shape, shape), compiler_params=_params(2),
    )(h2, w_gu, w_gu)


def _ffn_down_grad(dy16, w_down, a, u):
    t = a.shape[0]
    bt = _row_block(t)
    bn = _divisor_block(D_FF, 1408)

    def body(dy_ref, w_ref, a_ref, u_ref, da_ref, du_ref):
        df = _dot_nt(dy_ref[...], w_ref[...])
        av = a_ref[...].astype(F32)
        sg = jax.nn.sigmoid(av)
        da_ref[...] = (df * u_ref[...].astype(F32) * sg * (1.0 + av * (1.0 - sg))).astype(BF16)
        du_ref[...] = (df * av * sg).astype(BF16)

    blk = pl.BlockSpec((bt, bn), lambda i, j: (i, j))
    shape = jax.ShapeDtypeStruct((t, D_FF), BF16)
    return pl.pallas_call(
        body, name="d_ffn_down", grid=(t // bt, D_FF // bn),
        in_specs=[pl.BlockSpec((bt, D_MODEL), lambda i, j: (i, 0)), pl.BlockSpec((bn, D_MODEL), lambda i, j: (j, 0)),
                  blk, blk],
        out_specs=(blk, blk), out_shape=(shape, shape), compiler_params=_params(2),
    )(dy16, w_down, a, u)


def _ffn_down_loss(f, w_down, x2, target):
    t, w = x2.shape
    bt = _row_block(t)

    def body(f_ref, w_ref, x_ref, t_ref, dy_ref, dy16_ref, loss_ref):
        @pl.when(pl.program_id(0) == 0)
        def _():
            loss_ref[...] = jnp.zeros_like(loss_ref)

        err = (x_ref[...] + jnp.dot(f_ref[...], w_ref[...], preferred_element_type=F32)) - t_ref[...]
        dy = err * (1.0 / w)
        dy_ref[...] = dy
        dy16_ref[...] = dy.astype(BF16)
        loss_ref[...] += 0.5 * jnp.sum(jnp.mean(err * err, axis=-1, keepdims=True), axis=0, keepdims=True)

    row = pl.BlockSpec((bt, w), lambda i: (i, 0))
    return pl.pallas_call(
        body, name="ffn_down_loss", grid=(t // bt,),
        in_specs=[pl.BlockSpec((bt, D_FF), lambda i: (i, 0)), pl.BlockSpec((D_FF, w), lambda i: (0, 0)), row, row],
        out_specs=(row, row, pl.BlockSpec((8, LANES), lambda i: (0, 0))),
        out_shape=(jax.ShapeDtypeStruct((t, w), F32), jax.ShapeDtypeStruct((t, w), BF16),
                   jax.ShapeDtypeStruct((8, LANES), F32)),
        compiler_params=_params(1),
    )(f, w_down, x2, target)


def _adamw(parts, w, m, v, *, name):
    _, rows, cols = w.shape
    br = rows if rows <= 512 else 256
    assert rows % br == 0

    def body(p_ref, w_ref, m_ref, v_ref, g_ref, d_ref, nm_ref, nv_ref):
        g = p_ref[0].astype(F32)
        for r in range(1, N_DEV):
            g = g + p_ref[r].astype(F32)
        m2 = ADAM_B1 * m_ref[0] + (1.0 - ADAM_B1) * g
        v2 = ADAM_B2 * v_ref[0] + (1.0 - ADAM_B2) * jnp.square(g)
        m_hat = m2 / (1.0 - ADAM_B1 ** ADAM_STEP)
        v_hat = v2 / (1.0 - ADAM_B2 ** ADAM_STEP)
        g_ref[0] = g
        d_ref[0] = -ADAM_LR * (m_hat / (jnp.sqrt(v_hat) + ADAM_EPS) + ADAM_WD * w_ref[0])
        nm_ref[0] = m2
        nv_ref[0] = v2

    blk = pl.BlockSpec((1, br, cols), lambda i: (0, i, 0))
    shape = jax.ShapeDtypeStruct((1, rows, cols), F32)
    return pl.pallas_call(
        body, name=name, grid=(rows // br,),
        in_specs=[pl.BlockSpec((N_DEV, br, cols), lambda i: (0, i, 0)), blk, blk, blk],
        out_specs=(blk, blk, blk, blk), out_shape=(shape, shape, shape, shape), compiler_params=_params(1),
    )(parts, w, m, v)


_QA, _KA, _VA, _FA, _QD, _KD, _VD = (0, 512), (512, 1024), (1024, 1536), (1536, 1544), (1544, 2056), (2056, 2568), (2568, 3080)
_MAIN_ORDER = (_QA, _QD, _KA, _KD, _VA, _VD)
MAIN_COLS = 3 * D_MODEL
PROJ_COLS = MAIN_COLS + LANES


def _unshard_cols(parts):
    n, r, c = parts.shape
    return jnp.transpose(parts, (1, 0, 2)).reshape(r, n * c)


def _shard_cols(full):
    r, nc = full.shape
    return jnp.transpose(full.reshape(r, N_DEV, nc // N_DEV), (1, 0, 2))


def _w_in_to_kernel(w_full):
    main = jnp.concatenate([w_full[:, a:b] for a, b in _MAIN_ORDER], axis=1)
    forget = jnp.pad(w_full[:, _FA[0]:_FA[1]], ((0, 0), (0, LANES - N_HEADS_FOX)))
    return main, forget


def _w_in_from_kernel(g):
    pos = {span: i * W_GROUP for i, span in enumerate(_MAIN_ORDER)}
    parts = []
    for span in (_QA, _KA, _VA, _FA, _QD, _KD, _VD):
        if span == _FA:
            parts.append(g[:, MAIN_COLS:MAIN_COLS + N_HEADS_FOX])
        else:
            parts.append(g[:, pos[span]:pos[span] + W_GROUP])
    return jnp.concatenate(parts, axis=1)


def _pack_small(vals):
    rows = []
    for name, _, n_rows in SMALL_LAYOUT:
        flat = vals[name].reshape(-1).astype(F32)
        rows.append(jnp.pad(flat, (0, n_rows * LANES - flat.shape[0])).reshape(n_rows, LANES))
    packed = jnp.concatenate(rows, axis=0)
    return jnp.pad(packed, ((0, SMALL_ROWS - packed.shape[0]), (0, 0)))


def _unpack_small(packed, like):
    out = {}
    for name, row, n_rows in SMALL_LAYOUT:
        n = like[name].size
        out[name] = packed[row:row + n_rows].reshape(-1)[:n].reshape(like[name].shape)
    return out


def _device_step(x, target, small, shards):
    bsz, seq, _ = x.shape
    t = bsz * seq
    xf = x.reshape(t, D_MODEL)
    tf = target.reshape(t, D_MODEL)
    row = lambda v: v.reshape(1, -1)
    g_out = jnp.concatenate([small["g_out_fox"], small["g_out_dil"]]).reshape(1, D_MODEL)
    gains = jnp.concatenate(
        [jnp.tile(small[n].reshape(1, HEAD_DIM), (1, 2)) for n in ("g_q_fox", "g_q_dil", "g_k_fox", "g_k_dil")]
        + [jnp.zeros((4, LANES), F32)], axis=0)
    b_pad = jnp.pad(small["b_forget"].reshape(1, N_HEADS_FOX), ((0, 0), (0, LANES - N_HEADS_FOX)))
    rope = _rope_tables(seq)
    tables_qk = _bias_tables(seq, keys_first=False)
    tables_kq = _bias_tables(seq, keys_first=True)

    h1, g_in = _rmsnorm_fwd(xf, row(small["g_mix"]), group=D_MODEL, name="norm_mix",
                            hosted=_ChipGather([(shards["w_in"], False)]))
    w_main, w_fa = _w_in_to_kernel(_unshard_cols(g_in))
    w_in_all = jnp.concatenate([w_main, w_fa], axis=1)
    proj = _matmul(h1, w_main, name="in_proj", out_dtype=BF16)
    fa = _matmul(h1, w_fa, name="in_proj_forget", out_dtype=F32)
    qk = _qk_prep_fwd(proj, fa, b_pad, gains, rope, seq)
    late = _Exchange([(shards[n], False) for n in ("w_out", "w_gate", "w_up", "w_down")])
    o, lse, g_out_w, g_gate, g_up, g_down = _attn_fwd(qk, proj, tables_qk, seq, hosted=late)
    w_out = g_out_w.reshape(D_MODEL, D_MODEL)
    w_gu = jnp.concatenate([_unshard_cols(g_gate), _unshard_cols(g_up)], axis=1)
    w_down = g_down.reshape(D_FF, D_MODEL)
    on, x2, h2 = _out_proj_ffn_norm(o, g_out, w_out, xf, row(small["g_ffn"]))
    a, u, f = _ffn_gate_up(h2, w_gu)
    dy, dy16, loss_tile = _ffn_down_loss(f, w_down, x2, tf)

    da, du = _ffn_down_grad(dy16, w_down, a, u)
    gw_down = _matmul_tn(f, dy16, name="gw_down")
    gw_gate = _matmul_tn(h2, da, name="gw_gate")
    gw_up = _matmul_tn(h2, du, name="gw_up")
    dx2, dx2_16, dg_ffn = _norm_input_grad([(da, w_gu, 0), (du, w_gu, 1)], x2, row(small["g_ffn"]), group=D_MODEL,
                                           name="d_ffn_gate_up", out_dtypes=(F32, BF16), resid=dy, k_chunks=2)
    gw_out = _matmul_tn(on, dx2_16, name="gw_out")
    do, dg_out = _norm_input_grad([(dx2_16, w_out, 0)], o, g_out, group=W_GROUP, name="d_out_proj", out_dtypes=(BF16,))

    ffn_grads = _Exchange([
        (gw_out.reshape(N_DEV, D_MODEL // N_DEV, D_MODEL), True),
        (_shard_cols(gw_gate), True),
        (_shard_cols(gw_up), True),
        (gw_down.reshape(N_DEV, D_FF // N_DEV, D_MODEL), True),
    ])
    dq, dk, dv, dqx, dkx, p_out, p_gate, p_up, p_down = _attn_bwd(qk, proj, tables_kq, o, lse, do, seq, hosted=ffn_grads)
    dproj, dgains, db = _qk_prep_bwd(dq, dk, dqx, dkx, dv, proj, fa, b_pad, gains, rope, seq)
    gw_in = _matmul_tn(h1, dproj, name="gw_in")
    in_grad = _Exchange([(_shard_cols(_w_in_from_kernel(gw_in)), True)])
    dx, dg_mix, p_in = _norm_input_grad([(dproj, w_in_all, 0)], xf, row(small["g_mix"]), group=D_MODEL,
                                        name="d_in_proj", out_dtypes=(F32,), resid=dx2, hosted=in_grad)

    fold = lambda rows: jnp.sum(rows[:, :HEAD_DIM] + rows[:, HEAD_DIM:], axis=0)
    half = N_PAIRS // 2
    gsmall = {
        "g_mix": dg_mix, "g_ffn": dg_ffn, "g_out_fox": dg_out[0, :W_GROUP], "g_out_dil": dg_out[0, W_GROUP:],
        "g_q_fox": fold(dgains[0:half]), "g_q_dil": fold(dgains[half:N_PAIRS]),
        "g_k_fox": fold(dgains[N_PAIRS:N_PAIRS + half]), "g_k_dil": fold(dgains[N_PAIRS + half:]),
        "b_forget": db[0, :N_HEADS_FOX],
    }
    packed = _pack_small(gsmall).at[LOSS_ROW].set(loss_tile[0])
    (p_small,) = _exchange("small_exchange", [(packed, False)])
    parts = {"w_in": p_in, "w_out": p_out, "w_gate": p_gate, "w_up": p_up, "w_down": p_down}
    return dx.reshape(x.shape), parts, p_small


def kernel(x, g_mix, w_in, b_forget, g_q_fox, g_k_fox, g_q_dil, g_k_dil, g_out_fox, g_out_dil, w_out, g_ffn, w_gate, w_up, w_down, loss_target, m_g_mix, m_w_in, m_b_forget, m_g_q_fox, m_g_k_fox, m_g_q_dil, m_g_k_dil, m_g_out_fox, m_g_out_dil, m_w_out, m_g_ffn, m_w_gate, m_w_up, m_w_down, v_g_mix, v_w_in, v_b_forget, v_g_q_fox, v_g_k_fox, v_g_q_dil, v_g_k_dil, v_g_out_fox, v_g_out_dil, v_w_out, v_g_ffn, v_w_gate, v_w_up, v_w_down):
    args = dict(locals())
    small_names = [name for name, _, _ in SMALL_LAYOUT]
    big_names = ["w_in", "w_out", "w_gate", "w_up", "w_down"]
    small = {n: args[n][0] for n in small_names}

    shards = {n: args[n][0].astype(BF16) for n in big_names}
    grad_x, parts, p_small = _device_step(x, loss_target, small, shards)

    grads, deltas, new_m, new_v = {}, {}, {}, {}
    for n in big_names:
        grads[n], deltas[n], new_m[n], new_v[n] = _adamw(parts[n], args[n], args["m_" + n], args["v_" + n],
                                                         name="adamw_" + n)
    res = _adamw(p_small, _pack_small(small)[None], _pack_small({n: args["m_" + n][0] for n in small_names})[None],
                 _pack_small({n: args["v_" + n][0] for n in small_names})[None], name="adamw_small")
    loss = res[0][0, LOSS_ROW, 0]
    for dst, packed_res in zip((grads, deltas, new_m, new_v), res):
        for n, val in _unpack_small(packed_res[0], small).items():
            dst[n] = val[None]

    order = ["g_mix", "w_in", "b_forget", "g_q_fox", "g_k_fox", "g_q_dil", "g_k_dil", "g_out_fox", "g_out_dil",
             "w_out", "g_ffn", "w_gate", "w_up", "w_down"]
    return (loss, grad_x, *[grads[n] for n in order], *[deltas[n] for n in order],
            *[new_m[n] for n in order], *[new_v[n] for n in order])
```

```python
import functools
import math

import jax
import jax.numpy as jnp
import numpy as np
from jax import lax
from jax.experimental import pallas as pl
from jax.experimental.pallas import tpu as pltpu

F32 = jnp.float32
BF16 = jnp.bfloat16

D_MODEL = 1024
HEAD_DIM = 64
LANES = 128
N_PAIRS = D_MODEL // LANES
N_HEADS = 2 * N_PAIRS
N_HEADS_FOX = 8
W_GROUP = 512
D_FF = 2816
IN_COLS = 3080
DILATION_PAIRS = ((128, 1), (512, 4), (2048, 16))
ROPE_THETA = 500000.0
ROPE_DIM = 16
ROPE_HALF = ROPE_DIM // 2
EPS = 1e-6
NEG = -1e30
LOG2E = 1.4426950408889634
LN2 = 0.6931471805599453
AUG_ONE = 0
AUG_C = 3
N_DEV = 8

ADAM_LR = 0.001
ADAM_B1 = 0.9
ADAM_B2 = 0.999
ADAM_EPS = 1e-08
ADAM_WD = 0.01
ADAM_STEP = 10

ROW_BLOCK = 512
ATT_BLOCK = 256
ATT_GROUP = 4
VMEM_LIMIT = 56 * 1024 * 1024
MATMUL_VMEM_BUDGET = 44 * 1024 * 1024

SMALL_ROWS = 32
SMALL_LAYOUT = (("g_mix", 0, 8), ("g_ffn", 8, 8), ("g_out_fox", 16, 4), ("g_out_dil", 20, 4),
                ("g_q_fox", 24, 1), ("g_k_fox", 25, 1), ("g_q_dil", 26, 1), ("g_k_dil", 27, 1),
                ("b_forget", 28, 1))
LOSS_ROW = 29


def _params(n_grid):
    return pltpu.CompilerParams(dimension_semantics=("arbitrary",) * n_grid, vmem_limit_bytes=VMEM_LIMIT)


def _divisor_block(n, cap):
    best = None
    for b in range(LANES, min(n, cap) + 1, LANES):
        if n % b == 0:
            best = b
    assert best is not None, n
    return best


def _split_dot(a, b_exact, terms):
    acc = None
    rest = a
    for _ in range(terms):
        hi = rest.astype(BF16)
        part = jnp.dot(hi, b_exact, preferred_element_type=F32)
        acc = part if acc is None else acc + part
        rest = rest - hi.astype(F32)
    return acc


def _split_dot_rhs(a_exact, b, terms):
    acc = None
    rest = b
    for _ in range(terms):
        hi = rest.astype(BF16)
        part = jnp.dot(a_exact, hi, preferred_element_type=F32)
        acc = part if acc is None else acc + part
        rest = rest - hi.astype(F32)
    return acc


def _split_dot_nt(a_exact, b, terms):
    acc = None
    rest = b
    for _ in range(terms):
        hi = rest.astype(BF16)
        part = _dot_nt(a_exact, hi)
        acc = part if acc is None else acc + part
        rest = rest - hi.astype(F32)
    return acc


def _dot_nt(a, b):
    return lax.dot_general(a, b, (((1,), (1,)), ((), ())), preferred_element_type=F32)


def _dot_tn(a, b):
    return lax.dot_general(a, b, (((0,), (0,)), ((), ())), preferred_element_type=F32)


class _Exchange:
    def __init__(self, items):
        self.items = items
        self.n = len(items)
        self.arrays = [a for a, _ in items]
        self.out_shape = [jax.ShapeDtypeStruct((N_DEV,) + tuple(a.shape[1:] if sc else a.shape), a.dtype)
                          for a, sc in items]
        self.specs = [pl.BlockSpec(memory_space=pl.ANY)] * self.n
        self.scratch = [pltpu.SemaphoreType.DMA((self.n, N_DEV - 1)), pltpu.SemaphoreType.DMA((self.n, N_DEV - 1)),
                        pltpu.SemaphoreType.DMA((self.n,))]

    def run(self, ins, outs, sems, first, last, compute):
        send_sems, recv_sems, local_sems = sems
        x, y, c = lax.axis_index("x"), lax.axis_index("y"), lax.axis_index("c")
        me = 4 * x + 2 * y + c
        local, remote = [], []
        for k, (_, scatter) in enumerate(self.items):
            own = ins[k].at[me] if scatter else ins[k]
            local.append(pltpu.make_async_copy(own, outs[k].at[me], local_sems.at[k]))
        for r in range(1, N_DEV):
            px = 1 - x if r & 4 else x
            py = 1 - y if r & 2 else y
            pc = 1 - c if r & 1 else c
            peer = 4 * px + 2 * py + pc
            for k, (_, scatter) in enumerate(self.items):
                src = ins[k].at[peer] if scatter else ins[k]
                remote.append(pltpu.make_async_remote_copy(
                    src_ref=src, dst_ref=outs[k].at[me],
                    send_sem=send_sems.at[k, r - 1], recv_sem=recv_sems.at[k, r - 1],
                    device_id=(px, py, pc), device_id_type=pl.DeviceIdType.MESH))

        def start():
            for cp in local + remote:
                cp.start()

        def finish():
            for cp in remote:
                cp.wait_recv()
            for cp in remote:
                cp.wait_send()
            for cp in local:
                cp.wait()

        _run_phases(first, last, start, compute, finish)


def _run_phases(first, last, start, compute, finish):
    if first is None:
        start()
        compute()
        finish()
    else:
        pl.when(first)(start)
        compute()
        pl.when(last)(finish)


class _ChipGather(_Exchange):
    def run(self, ins, outs, sems, first, last, compute):
        send_sems, recv_sems, local_sems = sems
        x, y, c = lax.axis_index("x"), lax.axis_index("y"), lax.axis_index("c")
        sibling = (x, y, 1 - c)
        chips = [(1 - x, y), (x, 1 - y), (1 - x, 1 - y)]
        slot = lambda px, py, pc: 4 * px + 2 * py + pc

        def copy(k, n, src, dst_slot, to):
            return pltpu.make_async_remote_copy(
                src_ref=src, dst_ref=outs[k].at[dst_slot], send_sem=send_sems.at[k, n], recv_sem=recv_sems.at[k, n],
                device_id=to, device_id_type=pl.DeviceIdType.MESH)

        local, own, passed, arrivals = [], [], [], []
        for k in range(self.n):
            me = slot(x, y, c)
            local.append(pltpu.make_async_copy(ins[k], outs[k].at[me], local_sems.at[k]))
            own.append(copy(k, 0, ins[k], me, sibling))
            arrivals.append(copy(k, 0, ins[k], slot(*sibling), sibling))
            for j, chip in enumerate(chips):
                theirs = slot(*chip, c)
                own.append(copy(k, 1 + j, ins[k], me, (*chip, c)))
                passed.append((copy(k, 1 + j, ins[k], theirs, sibling),
                               copy(k, 4 + j, outs[k].at[theirs], theirs, sibling)))
                arrivals.append(copy(k, 4 + j, ins[k], slot(*chip, 1 - c), sibling))

        def start():
            for cp in local + own:
                cp.start()

        def finish():
            for landed, onward in passed:
                landed.wait_recv()
                onward.start()
            for cp in arrivals:
                cp.wait_recv()
            for cp in own + [onward for _, onward in passed]:
                cp.wait_send()
            for cp in local:
                cp.wait()

        _run_phases(first, last, start, compute, finish)


def _grid_ends(grid):
    ids = [pl.program_id(d) for d in range(len(grid))]
    first = functools.reduce(jnp.logical_and, [i == 0 for i in ids])
    last = functools.reduce(jnp.logical_and, [i == g - 1 for i, g in zip(ids, grid)])
    return first, last


def _host(core, n_in, n_out, n_scratch, hosted, grid):
    if hosted is None:
        return core
    nh = hosted.n

    def body(*refs):
        ins, rest = refs[:n_in], refs[n_in:]
        h_ins, rest = rest[:nh], rest[nh:]
        outs, rest = rest[:n_out], rest[n_out:]
        h_outs, rest = rest[:nh], rest[nh:]
        scratch, sems = rest[:n_scratch], rest[n_scratch:]
        first, last = _grid_ends(grid)
        hosted.run(h_ins, h_outs, sems, first, last, lambda: core(*ins, *outs, *scratch))

    return body


def _hosted_parts(hosted):
    if hosted is None:
        return [], [], [], []
    return list(hosted.specs), list(hosted.out_shape), list(hosted.arrays), list(hosted.scratch)


def _exchange(name, items):
    ex = _Exchange(items)
    n = ex.n

    def body(*refs):
        ex.run(refs[:n], refs[n:2 * n], refs[2 * n:], None, None, lambda: None)

    return pl.pallas_call(
        body, name=name, out_shape=tuple(ex.out_shape), in_specs=ex.specs, out_specs=tuple(ex.specs),
        scratch_shapes=ex.scratch,
    )(*ex.arrays)


def _matmul_blocks(t, k, n, a_bytes, o_bytes):
    for bt, cap in ((1024, 1408), (1024, 512), (512, 512)):
        if t % bt:
            continue
        bn = _divisor_block(n, cap)
        if 2 * (bt * k * a_bytes + bn * k * 2 + bt * bn * o_bytes) <= MATMUL_VMEM_BUDGET:
            return bt, bn
    return ROW_BLOCK, _divisor_block(n, 256)


def _matmul_nt(a, w, *, name, out_dtype):
    t, k = a.shape
    n = w.shape[0]
    assert w.shape[1] == k
    bt, bn = _matmul_blocks(t, k, n, a.dtype.itemsize, jnp.dtype(out_dtype).itemsize)

    def body(a_ref, w_ref, o_ref):
        o_ref[...] = _dot_nt(a_ref[...], w_ref[...]).astype(o_ref.dtype)

    return pl.pallas_call(
        body, name=name, grid=(t // bt, n // bn),
        in_specs=[pl.BlockSpec((bt, k), lambda i, j: (i, 0)), pl.BlockSpec((bn, k), lambda i, j: (j, 0))],
        out_specs=pl.BlockSpec((bt, bn), lambda i, j: (i, j)),
        out_shape=jax.ShapeDtypeStruct((t, n), out_dtype), compiler_params=_params(2),
    )(a, w)


def _matmul_tn(a, b, *, name):
    t, m = a.shape
    n = b.shape[1]
    bt = 2048 if t % 2048 == 0 else ROW_BLOCK
    bm = _divisor_block(m, 1408)
    bn = _divisor_block(n, 1408)
    steps = t // bt

    def body(a_ref, b_ref, o_ref, acc):
        step = pl.program_id(2)

        @pl.when(step == 0)
        def _():
            acc[...] = jnp.zeros_like(acc)

        acc[...] += _dot_tn(a_ref[...], b_ref[...])

        @pl.when(step == steps - 1)
        def _():
            o_ref[...] = acc[...].astype(o_ref.dtype)

    return pl.pallas_call(
        body, name=name, grid=(m // bm, n // bn, steps),
        in_specs=[pl.BlockSpec((bt, bm), lambda i, j, s: (s, i)), pl.BlockSpec((bt, bn), lambda i, j, s: (s, j))],
        out_specs=pl.BlockSpec((bm, bn), lambda i, j, s: (i, j)),
        out_shape=jax.ShapeDtypeStruct((m, n), BF16), scratch_shapes=[pltpu.VMEM((bm, bn), F32)],
        compiler_params=_params(3),
    )(a, b)


def _rmsnorm_fwd(x, g, *, group, name, hosted=None):
    t, w = x.shape
    bt = ROW_BLOCK

    def body(x_ref, g_ref, o_ref):
        for s in range(0, w, group):
            xs = x_ref[:, s:s + group].astype(F32)
            r = lax.rsqrt(jnp.mean(xs * xs, axis=-1, keepdims=True) + EPS)
            o_ref[:, s:s + group] = (xs * r * g_ref[:, s:s + group]).astype(o_ref.dtype)

    grid = (t // bt,)
    h_specs, h_shapes, h_args, h_scratch = _hosted_parts(hosted)
    res = pl.pallas_call(
        _host(body, 2, 1, 0, hosted, grid), name=name, grid=grid,
        in_specs=[pl.BlockSpec((bt, w), lambda i: (i, 0)), pl.BlockSpec((1, w), lambda i: (0, 0))] + h_specs,
        out_specs=tuple([pl.BlockSpec((bt, w), lambda i: (i, 0))] + h_specs),
        out_shape=tuple([jax.ShapeDtypeStruct((t, w), BF16)] + h_shapes),
        scratch_shapes=h_scratch, compiler_params=_params(1),
    )(x, g, *h_args)
    return res if hosted else res[0]


def _norm_input_grad(terms, x, g, *, group, name, out_dtypes, resid=None, hosted=None, k_chunks=1):
    t, w = x.shape
    n_terms = len(terms)
    kc = [a.shape[1] // k_chunks for a, _, _ in terms]
    per_row = sum(c * a.dtype.itemsize for c, (a, _, _) in zip(kc, terms)) + w * (x.dtype.itemsize + 4 * (resid is not None))
    per_row += w * sum(jnp.dtype(dt).itemsize for dt in out_dtypes)
    fixed = 2 * sum(w * c * 2 for c in kc)
    bt = next(b for b in (1024, 512, 256, 128)
              if t % b == 0 and fixed + 2 * b * per_row + 5 * b * w * 4 <= MATMUL_VMEM_BUDGET)
    n_in = 2 * n_terms + 2 + (resid is not None)
    grid = (t // bt, k_chunks)

    def body(*refs):
        x_ref, g_ref = refs[2 * n_terms], refs[2 * n_terms + 1]
        dx_refs, dg_ref, dh_ref = refs[n_in:-2], refs[-2], refs[-1]
        chunk = pl.program_id(1)

        @pl.when((pl.program_id(0) == 0) & (chunk == 0))
        def _():
            dg_ref[...] = jnp.zeros_like(dg_ref)

        part = None
        for k in range(n_terms):
            if terms[k][2]:
                term = jnp.dot(refs[2 * k][...], refs[2 * k + 1][...], preferred_element_type=F32)
            else:
                term = _dot_nt(refs[2 * k][...], refs[2 * k + 1][...])
            part = term if part is None else part + term

        @pl.when(chunk == 0)
        def _():
            dh_ref[...] = part

        @pl.when(chunk > 0)
        def _():
            dh_ref[...] += part

        @pl.when(chunk == k_chunks - 1)
        def _():
            for s in range(0, w, group):
                xs = x_ref[:, s:s + group].astype(F32)
                dhs = dh_ref[:, s:s + group]
                r = lax.rsqrt(jnp.mean(xs * xs, axis=-1, keepdims=True) + EPS)
                xh = xs * r
                dg_ref[:, s:s + group] += jnp.sum(dhs * xh, axis=0, keepdims=True)
                dxh = dhs * g_ref[:, s:s + group]
                dx = r * (dxh - xh * jnp.mean(dxh * xh, axis=-1, keepdims=True))
                if resid is not None:
                    dx = refs[n_in - 1][:, s:s + group] + dx
                for dx_ref in dx_refs:
                    dx_ref[:, s:s + group] = dx.astype(dx_ref.dtype)

    row = pl.BlockSpec((bt, w), lambda i, k: (i, 0))
    vec = pl.BlockSpec((1, w), lambda i, k: (0, 0))
    in_specs, args = [], []
    for c, (a, wt, w_is_kn) in zip(kc, terms):
        assert wt.shape == ((a.shape[1], w) if w_is_kn else (w, a.shape[1]))
        w_spec = pl.BlockSpec((c, w), lambda i, k: (k, 0)) if w_is_kn else pl.BlockSpec((w, c), lambda i, k: (0, k))
        in_specs += [pl.BlockSpec((bt, c), lambda i, k: (i, k)), w_spec]
        args += [a, wt]
    in_specs += [row, vec] + ([row] if resid is not None else [])
    args += [x, g] + ([resid] if resid is not None else [])
    h_specs, h_shapes, h_args, h_scratch = _hosted_parts(hosted)
    return pl.pallas_call(
        _host(body, n_in, len(out_dtypes) + 1, 1, hosted, grid), name=name, grid=grid, in_specs=in_specs + h_specs,
        out_specs=tuple([row] * len(out_dtypes) + [vec] + h_specs),
        out_shape=tuple([jax.ShapeDtypeStruct((t, w), dt) for dt in out_dtypes] + [jax.ShapeDtypeStruct((1, w), F32)]
                        + h_shapes),
        scratch_shapes=[pltpu.VMEM((bt, w), F32)] + h_scratch, compiler_params=_params(2),
    )(*args, *h_args)


def _tile_plan(tile):
    is_q = tile < N_PAIRS
    is_dil = (tile % N_PAIRS) >= N_PAIRS // 2
    return is_q, is_dil, (0 if is_q else 2) + (1 if is_dil else 0)


def _segment_ones():
    lane = np.arange(LANES)
    return jnp.asarray((lane[:, None] // HEAD_DIM) == (lane[None, :] // HEAD_DIM), BF16)


def _rope_tables(seq):
    inv_freq = jnp.power(jnp.float32(ROPE_THETA), -jnp.arange(ROPE_HALF, dtype=F32) * 2.0 / ROPE_DIM)
    ang = jnp.arange(seq).astype(F32)[:, None] * inv_freq[None, :]
    cos, sin = jnp.cos(ang), jnp.sin(ang)
    ones = jnp.ones((seq, HEAD_DIM - ROPE_DIM), F32)
    zeros = jnp.zeros((seq, HEAD_DIM - ROPE_DIM), F32)
    zh = jnp.zeros((seq, ROPE_HALF), F32)
    cos_t = jnp.concatenate([cos, cos, ones], axis=1)
    sin_a = jnp.concatenate([-sin, zh, zeros], axis=1)
    sin_b = jnp.concatenate([zh, sin, zeros], axis=1)
    return tuple(jnp.tile(tab, (1, 2)) for tab in (cos_t, sin_a, sin_b))


def _log_sigmoid(z):
    return jnp.minimum(z, 0.0) - jnp.log1p(jnp.exp(-jnp.abs(z)))


def _qk_prep_fwd(proj, fa, b_pad, gains, rope, seq):
    t = proj.shape[0]
    bt = ROW_BLOCK
    nsb = seq // bt
    seg = _segment_ones()
    rr = np.arange(bt)
    tri = jnp.asarray(rr[:, None] >= rr[None, :], BF16)

    def body(p_ref, fa_ref, b_ref, g_ref, cos_ref, sa_ref, sb_ref, seg_ref, tri_ref, qk_ref, carry):
        @pl.when(pl.program_id(0) % nsb == 0)
        def _():
            carry[...] = jnp.zeros_like(carry)

        lane = lax.broadcasted_iota(jnp.int32, (bt, LANES), 1)
        logf = jnp.where(lane < N_HEADS_FOX, _log_sigmoid(fa_ref[...] + b_ref[...]), 0.0)
        cblk = _split_dot_rhs(tri_ref[...], logf, 3) + carry[0:1, :]
        carry[0:1, :] = cblk[bt - 1:bt, :]
        c_terms = []
        rest = cblk * LOG2E
        for _ in range(3):
            term = rest.astype(BF16).astype(F32)
            c_terms.append(term)
            rest = rest - term

        for tile in range(2 * N_PAIRS):
            is_q, is_dil, grow = _tile_plan(tile)
            pair = tile % N_PAIRS
            xs = p_ref[:, tile * LANES:(tile + 1) * LANES].astype(F32)
            r = lax.rsqrt(_split_dot(xs * xs, seg_ref[...], 2) * (1.0 / HEAD_DIM) + EPS)
            yv = xs * r * g_ref[grow:grow + 1, :]
            if is_dil:
                yv = (yv * cos_ref[...] + pltpu.roll(yv, LANES - ROPE_HALF, 1) * sa_ref[...]
                      + pltpu.roll(yv, ROPE_HALF, 1) * sb_ref[...])
            if is_q:
                yv = yv * (HEAD_DIM ** -0.5 * LOG2E)
            for e in range(2):
                head = 2 * pair + e
                other = HEAD_DIM * (1 - e)
                aug = jnp.zeros((bt, LANES), F32)
                if not is_dil:
                    ones_at = other + (AUG_ONE if is_q else AUG_C)
                    c_at = other + (AUG_C if is_q else AUG_ONE)
                    aug = jnp.where((lane >= ones_at) & (lane < ones_at + 3), 1.0, aug)
                    for n, term in enumerate(c_terms):
                        col = term[:, head:head + 1]
                        aug = jnp.where(lane == c_at + n, col if is_q else -col, aug)
                mine = (lane < HEAD_DIM) if e == 0 else (lane >= HEAD_DIM)
                dst = ((0 if is_q else N_HEADS) + head) * LANES
                qk_ref[:, dst:dst + LANES] = jnp.where(mine, yv, aug).astype(BF16)

    row128 = pl.BlockSpec((bt, LANES), lambda i: (i, 0))
    rope_spec = pl.BlockSpec((bt, LANES), lambda i: (i % nsb, 0))
    const = lambda shape: pl.BlockSpec(shape, lambda i: (0, 0))
    return pl.pallas_call(
        body, name="qk_prep_fwd", grid=(t // bt,),
        in_specs=[pl.BlockSpec((bt, 2 * D_MODEL), lambda i: (i, 0)), row128, const((1, LANES)), const((8, LANES)),
                  rope_spec, rope_spec, rope_spec, const((LANES, LANES)), const((bt, bt))],
        out_specs=pl.BlockSpec((bt, 2 * N_HEADS * LANES), lambda i: (i, 0)),
        out_shape=jax.ShapeDtypeStruct((t, 2 * N_HEADS * LANES), BF16),
        scratch_shapes=[pltpu.VMEM((8, LANES), F32)], compiler_params=_params(1),
    )(proj, fa, b_pad, gains, *rope, seg, tri)


def _qk_prep_bwd(dq, dk, dqx, dkx, dv, proj, fa, b_pad, gains, rope, seq):
    t = proj.shape[0]
    bt = ROW_BLOCK
    nsb = seq // bt
    nblk = t // bt
    seg = _segment_ones()
    rr = np.arange(bt)
    triu = jnp.asarray(rr[:, None] <= rr[None, :], BF16)

    def body(dq_ref, dk_ref, dqx_ref, dkx_ref, dv_ref, p_ref, fa_ref, b_ref, g_ref, cos_ref, sa_ref, sb_ref, seg_ref,
             triu_ref, dp_ref, dg_ref, db_ref, carry):
        step = pl.program_id(0)

        @pl.when(step == 0)
        def _():
            dg_ref[...] = jnp.zeros_like(dg_ref)
            db_ref[...] = jnp.zeros_like(db_ref)

        @pl.when(step % nsb == 0)
        def _():
            carry[...] = jnp.zeros_like(carry)

        for tile in range(2 * N_PAIRS):
            is_q, is_dil, grow = _tile_plan(tile)
            cols = slice(tile * LANES, (tile + 1) * LANES)
            src = dq_ref if is_q else dk_ref
            half = slice((tile % N_PAIRS) * LANES, (tile % N_PAIRS + 1) * LANES)
            dy = src[:, half]
            dy = dy * (HEAD_DIM ** -0.5 if is_q else LN2)
            if is_dil:
                dy = (dy * cos_ref[...] + pltpu.roll(dy * sa_ref[...], ROPE_HALF, 1)
                      + pltpu.roll(dy * sb_ref[...], LANES - ROPE_HALF, 1))
            xs = p_ref[:, cols].astype(F32)
            r = lax.rsqrt(_split_dot(xs * xs, seg_ref[...], 2) * (1.0 / HEAD_DIM) + EPS)
            xh = xs * r
            dg_ref[tile:tile + 1, :] += jnp.sum(dy * xh, axis=0, keepdims=True)
            dxh = dy * g_ref[grow:grow + 1, :]
            seg_mean = _split_dot(dxh * xh, seg_ref[...], 2) * (1.0 / HEAD_DIM)
            dp_ref[:, cols] = (r * (dxh - xh * seg_mean)).astype(BF16)

        lane = lax.broadcasted_iota(jnp.int32, (bt, LANES), 1)
        dc = jnp.zeros((bt, LANES), F32)
        for h in range(N_HEADS_FOX):
            other = (h // 2) * LANES + HEAD_DIM * (1 - h % 2)
            row_sum = dqx_ref[:, other + AUG_C:other + AUG_C + 1]
            col_sum = dkx_ref[:, other + AUG_ONE:other + AUG_ONE + 1]
            dc = jnp.where(lane == h, row_sum - col_sum, dc)
        dlogf = _split_dot_rhs(triu_ref[...], dc, 3) + carry[0:1, :]
        carry[0:1, :] = dlogf[0:1, :]
        z = fa_ref[...] + b_ref[...]
        dfa = dlogf * (1.0 / (1.0 + jnp.exp(z)))
        db_ref[0:1, :] += jnp.sum(dfa, axis=0, keepdims=True)
        dp_ref[:, 2 * D_MODEL:MAIN_COLS] = dv_ref[...]
        dp_ref[:, MAIN_COLS:PROJ_COLS] = dfa.astype(BF16)

    rev = lambda i: nblk - 1 - i
    row = lambda w: pl.BlockSpec((bt, w), lambda i: (rev(i), 0))
    rope_spec = pl.BlockSpec((bt, LANES), lambda i: (rev(i) % nsb, 0))
    const = lambda shape: pl.BlockSpec(shape, lambda i: (0, 0))
    return pl.pallas_call(
        body, name="qk_prep_bwd", grid=(nblk,),
        in_specs=[row(D_MODEL), row(D_MODEL), row(W_GROUP), row(W_GROUP), row(D_MODEL), row(2 * D_MODEL), row(LANES),
                  const((1, LANES)), const((8, LANES)), rope_spec, rope_spec, rope_spec, const((LANES, LANES)),
                  const((bt, bt))],
        out_specs=(row(PROJ_COLS), const((2 * N_PAIRS, LANES)), const((8, LANES))),
        out_shape=(jax.ShapeDtypeStruct((t, PROJ_COLS), BF16),
                   jax.ShapeDtypeStruct((2 * N_PAIRS, LANES), F32), jax.ShapeDtypeStruct((8, LANES), F32)),
        scratch_shapes=[pltpu.VMEM((8, LANES), F32)], compiler_params=_params(1),
    )(dq, dk, dqx, dkx, dv, proj, fa, b_pad, gains, *rope, seg, triu)


def _bias_tables(seq, keys_first):
    nb = seq // ATT_BLOCK
    idx = jnp.arange(ATT_BLOCK)
    q_idx, k_idx = (idx[None, None, :], idx[None, :, None]) if keys_first else (idx[None, :, None], idx[None, None, :])
    dist = jnp.arange(nb)[:, None, None] * ATT_BLOCK + q_idx - k_idx
    causal = dist >= 0
    count = jnp.zeros(dist.shape, jnp.int32)
    for window, dilation in DILATION_PAIRS:
        count = count + (causal & (dist % dilation == 0) & (dist <= window)).astype(jnp.int32)
    fox = jnp.where(causal, 0.0, NEG).astype(F32)
    dil = jnp.where(count == 3, math.log2(3.0), jnp.where(count == 2, 1.0, jnp.where(count == 1, 0.0, NEG)))
    return jnp.stack([fox, dil.astype(F32)], axis=0)


def _attn_specs(seq):
    nb = seq // ATT_BLOCK
    col = lambda off: pl.BlockSpec((seq, LANES), lambda b, j: (b, off + j))
    heads = lambda off: pl.BlockSpec((seq, 2 * LANES), lambda b, j: (b, off + j))
    table_spec = pl.BlockSpec((1, nb, ATT_BLOCK, ATT_BLOCK), lambda b, j: (j // (N_PAIRS // 2), 0, 0, 0))
    return col, heads, table_spec


def _head_lanes(e, shape, axis):
    pos = lax.broadcasted_iota(jnp.int32, shape, axis)
    return pos < HEAD_DIM if e == 0 else pos >= HEAD_DIM


def _attn_fwd(qk, proj, tables, seq, hosted=None):
    t = qk.shape[0]
    nb = seq // ATT_BLOCK
    blk = ATT_BLOCK

    def body(q_ref, k_ref, v_ref, tab_ref, o_ref, lse_ref):
        mine = [_head_lanes(e, (seq, LANES), 1) for e in range(2)]
        lane = lax.broadcasted_iota(jnp.int32, (seq, LANES), 1)
        v_aug = [jnp.where(mine[e], v_ref[...], (lane == HEAD_DIM * (1 - e)).astype(BF16)) for e in range(2)]
        for i in range(nb):
            rows = slice(i * blk, (i + 1) * blk)
            n_keys = (i + 1) * blk
            out, lse = [], []
            for e in range(2):
                heads_e = slice(e * LANES, (e + 1) * LANES)
                s = _dot_nt(q_ref[rows, heads_e], k_ref[0:n_keys, heads_e])
                s = jnp.concatenate([s[:, jj * blk:(jj + 1) * blk] + tab_ref[0, i - jj] for jj in range(i + 1)], axis=1)
                m = jnp.max(s, axis=1, keepdims=True)
                acc = jnp.dot(jnp.exp2(s - m).astype(BF16), v_aug[e][0:n_keys], preferred_element_type=F32)
                ones_at = HEAD_DIM * (1 - e)
                l = acc[:, ones_at:ones_at + 1]
                out.append(acc / l)
                lse.append(m + jnp.log2(l))
            o_ref[rows, :] = jnp.where(mine[0][rows], out[0], out[1]).astype(o_ref.dtype)
            lse_ref[rows, :] = jnp.where(mine[0][rows], lse[0], lse[1])

    col, heads, table_spec = _attn_specs(seq)
    grid = (t // seq, N_PAIRS)
    h_specs, h_shapes, h_args, h_scratch = _hosted_parts(hosted)
    return pl.pallas_call(
        _host(body, 4, 2, 0, hosted, grid), name="attn_fwd", grid=grid,
        in_specs=[heads(0), heads(N_PAIRS), col(2 * N_PAIRS), table_spec] + h_specs,
        out_specs=tuple([col(0), col(0)] + h_specs),
        out_shape=tuple([jax.ShapeDtypeStruct((t, D_MODEL), BF16), jax.ShapeDtypeStruct((t, D_MODEL), F32)] + h_shapes),
        scratch_shapes=h_scratch, compiler_params=_params(2),
    )(qk, qk, proj, tables, *h_args)


def _attn_bwd(qk, proj, tables, o, lse, do, seq, hosted=None):
    t = qk.shape[0]
    nb = seq // ATT_BLOCK
    blk = ATT_BLOCK
    group = math.gcd(nb, ATT_GROUP)

    def body(q_ref, k_ref, v_ref, tab_ref, o_ref, lse_ref, do_ref,
             dq_ref, dk_ref, dv_ref, dqx_ref, dkx_ref, dk_acc, dv_acc):
        mine = [_head_lanes(e, (blk, LANES), 1) for e in range(2)]
        top = _head_lanes(0, (LANES, blk), 0)
        head_rows = lax.broadcasted_iota(jnp.int32, (8, LANES), 0)
        head_of_lane = lax.broadcasted_iota(jnp.int32, (8, LANES), 1) // HEAD_DIM
        head_sel = (head_rows == head_of_lane).astype(BF16)
        dk_acc[...] = jnp.zeros_like(dk_acc)
        dv_acc[...] = jnp.zeros_like(dv_acc)

        def block_rows(i):
            return pl.ds(pl.multiple_of(i * blk, blk), blk)

        def q_group(g, _):
            base = g * group
            qs, doe, delta, lse_e = [], [], [], []
            for b in range(group):
                rows = block_rows(base + b)
                qs.append([q_ref[rows, e * LANES:(e + 1) * LANES] for e in range(2)])
                do_blk = do_ref[rows, :]
                doe.append([jnp.where(mine[e], do_blk, jnp.zeros_like(do_blk)) for e in range(2)])
                delta_t = _split_dot_nt(head_sel, do_blk.astype(F32) * o_ref[rows, :].astype(F32), 3)
                lse_t = _split_dot_nt(head_sel, lse_ref[rows, :], 3) * (1.0 / HEAD_DIM)
                delta.append([delta_t[e:e + 1, :] for e in range(2)])
                lse_e.append([lse_t[e:e + 1, :] for e in range(2)])

            def key_block(dq_t, jj, members):
                krows = block_rows(jj)
                v = v_ref[krows, :]
                dq_t = [list(d) for d in dq_t]
                dv_part = None
                for e in range(2):
                    k_e = k_ref[krows, e * LANES:(e + 1) * LANES]
                    dk_part = None
                    for b, dist in members:
                        p_t = jnp.exp2(_dot_nt(k_e, qs[b][e]) + tab_ref[0, dist] - lse_e[b][e])
                        ds_t = (p_t * (_dot_nt(v, doe[b][e]) - delta[b][e])).astype(BF16)
                        part = jnp.dot(p_t.astype(BF16), doe[b][e], preferred_element_type=F32)
                        dv_part = part if dv_part is None else dv_part + part
                        part = jnp.dot(ds_t, qs[b][e], preferred_element_type=F32)
                        dk_part = part if dk_part is None else dk_part + part
                        dq_t[b][e] = dq_t[b][e] + _dot_tn(k_e, ds_t)
                    dk_acc[e, krows, :] += dk_part
                dv_acc[krows, :] += dv_part
                return tuple(tuple(d) for d in dq_t)

            zacc = jnp.zeros((LANES, blk), F32)
            dq_t = tuple((zacc, zacc) for _ in range(group))
            dq_t = lax.fori_loop(
                0, base, lambda jj, st: key_block(st, jj, [(b, base + b - jj) for b in range(group)]), dq_t)
            for a in range(group):
                dq_t = key_block(dq_t, base + a, [(b, b - a) for b in range(a, group)])
            for b in range(group):
                rows = block_rows(base + b)
                dq_ref[rows, :] = jnp.where(top, dq_t[b][0], dq_t[b][1]).T
                dqx_ref[rows, :] = jnp.where(top, dq_t[b][1], dq_t[b][0]).T
            return 0

        lax.fori_loop(0, nb // group, q_group, 0)
        lo = _head_lanes(0, (seq, LANES), 1)
        dk_ref[...] = jnp.where(lo, dk_acc[0], dk_acc[1])
        dkx_ref[...] = jnp.where(lo, dk_acc[1], dk_acc[0])
        dv_ref[...] = dv_acc[...].astype(dv_ref.dtype)

    col, heads, table_spec = _attn_specs(seq)
    grid = (t // seq, N_PAIRS)
    h_specs, h_shapes, h_args, h_scratch = _hosted_parts(hosted)
    f32_out = jax.ShapeDtypeStruct((t, D_MODEL), F32)
    return pl.pallas_call(
        _host(body, 7, 5, 2, hosted, grid), name="attn_bwd", grid=grid,
        in_specs=[heads(0), heads(N_PAIRS), col(2 * N_PAIRS), table_spec, col(0), col(0), col(0)] + h_specs,
        out_specs=tuple([col(0)] * 5 + h_specs),
        out_shape=tuple([f32_out, f32_out, jax.ShapeDtypeStruct((t, D_MODEL), BF16), f32_out, f32_out] + h_shapes),
        scratch_shapes=[pltpu.VMEM((2, seq, LANES), F32), pltpu.VMEM((seq, LANES), F32)] + h_scratch,
        compiler_params=_params(2),
    )(qk, qk, proj, tables, o, lse, do, *h_args)


def _row_block(t):
    return 1024 if t % 1024 == 0 else ROW_BLOCK


def _out_proj_ffn_norm(o, g_out, w_out, x, g_ffn):
    t = o.shape[0]
    bt = _row_block(t)

    def body(o_ref, go_ref, w_ref, x_ref, gf_ref, on_ref, x2_ref, h2_ref):
        for s in range(0, D_MODEL, W_GROUP):
            os_ = o_ref[:, s:s + W_GROUP].astype(F32)
            r = lax.rsqrt(jnp.mean(os_ * os_, axis=-1, keepdims=True) + EPS)
            on_ref[:, s:s + W_GROUP] = (os_ * r * go_ref[:, s:s + W_GROUP]).astype(BF16)
        x2 = x_ref[...] + jnp.dot(on_ref[...], w_ref[...], preferred_element_type=F32)
        x2_ref[...] = x2
        r2 = lax.rsqrt(jnp.mean(x2 * x2, axis=-1, keepdims=True) + EPS)
        h2_ref[...] = (x2 * r2 * gf_ref[...]).astype(BF16)

    row = pl.BlockSpec((bt, D_MODEL), lambda i: (i, 0))
    vec = pl.BlockSpec((1, D_MODEL), lambda i: (0, 0))
    return pl.pallas_call(
        body, name="out_proj", grid=(t // bt,),
        in_specs=[row, vec, pl.BlockSpec((D_MODEL, D_MODEL), lambda i: (0, 0)), row, vec],
        out_specs=(row, row, row),
        out_shape=(jax.ShapeDtypeStruct((t, D_MODEL), BF16), jax.ShapeDtypeStruct((t, D_MODEL), F32),
                   jax.ShapeDtypeStruct((t, D_MODEL), BF16)),
        compiler_params=_params(1),
    )(o, g_out, w_out, x, g_ffn)


def _ffn_gate_up(h2, w_gate_t, w_up_t):
    t = h2.shape[0]
    bt = _row_block(t)
    bn = _divisor_block(D_FF, 1408)

    def body(h_ref, wg_ref, wu_ref, a_ref, u_ref, f_ref):
        a = _dot_nt(h_ref[...], wg_ref[...])
        u = _dot_nt(h_ref[...], wu_ref[...])
        a_ref[...] = a.astype(BF16)
        u_ref[...] = u.astype(BF16)
        f_ref[...] = (a * jax.nn.sigmoid(a) * u).astype(BF16)

    blk = pl.BlockSpec((bt, bn), lambda i, j: (i, j))
    w_blk = pl.BlockSpec((bn, D_MODEL), lambda i, j: (j, 0))
    shape = jax.ShapeDtypeStruct((t, D_FF), BF16)
    return pl.pallas_call(
        body, name="ffn_gate_up", grid=(t // bt, D_FF // bn),
        in_specs=[pl.BlockSpec((bt, D_MODEL), lambda i, j: (i, 0)), w_blk, w_blk],
        out_specs=(blk, blk, blk), out_shape=(shape, shape, shape), compiler_params=_params(2),
    )(h2, w_gate_t, w_up_t)


def _ffn_down_grad(dy16, w_down, a, u):
    t = a.shape[0]
    bt = _row_block(t)
    bn = _divisor_block(D_FF, 1408)

    def body(dy_ref, w_ref, a_ref, u_ref, da_ref, du_ref):
        df = _dot_nt(dy_ref[...], w_ref[...])
        av = a_ref[...].astype(F32)
        sg = jax.nn.sigmoid(av)
        da_ref[...] = (df * u_ref[...].astype(F32) * sg * (1.0 + av * (1.0 - sg))).astype(BF16)
        du_ref[...] = (df * av * sg).astype(BF16)

    blk = pl.BlockSpec((bt, bn), lambda i, j: (i, j))
    shape = jax.ShapeDtypeStruct((t, D_FF), BF16)
    return pl.pallas_call(
        body, name="d_ffn_down", grid=(t // bt, D_FF // bn),
        in_specs=[pl.BlockSpec((bt, D_MODEL), lambda i, j: (i, 0)), pl.BlockSpec((bn, D_MODEL), lambda i, j: (j, 0)),
                  blk, blk],
        out_specs=(blk, blk), out_shape=(shape, shape), compiler_params=_params(2),
    )(dy16, w_down, a, u)


def _ffn_down_loss(f, w_down, x2, target):
    t, w = x2.shape
    bt = _row_block(t)

    def body(f_ref, w_ref, x_ref, t_ref, dy_ref, dy16_ref, loss_ref):
        @pl.when(pl.program_id(0) == 0)
        def _():
            loss_ref[...] = jnp.zeros_like(loss_ref)

        err = (x_ref[...] + jnp.dot(f_ref[...], w_ref[...], preferred_element_type=F32)) - t_ref[...]
        dy = err * (1.0 / w)
        dy_ref[...] = dy
        dy16_ref[...] = dy.astype(BF16)
        loss_ref[...] += 0.5 * jnp.sum(jnp.mean(err * err, axis=-1, keepdims=True), axis=0, keepdims=True)

    row = pl.BlockSpec((bt, w), lambda i: (i, 0))
    return pl.pallas_call(
        body, name="ffn_down_loss", grid=(t // bt,),
        in_specs=[pl.BlockSpec((bt, D_FF), lambda i: (i, 0)), pl.BlockSpec((D_FF, w), lambda i: (0, 0)), row, row],
        out_specs=(row, row, pl.BlockSpec((8, LANES), lambda i: (0, 0))),
        out_shape=(jax.ShapeDtypeStruct((t, w), F32), jax.ShapeDtypeStruct((t, w), BF16),
                   jax.ShapeDtypeStruct((8, LANES), F32)),
        compiler_params=_params(1),
    )(f, w_down, x2, target)


def _adamw(parts, w, m, v, *, name):
    _, rows, cols = w.shape
    br = rows if rows <= 512 else 256
    assert rows % br == 0

    def body(p_ref, w_ref, m_ref, v_ref, g_ref, d_ref, nm_ref, nv_ref):
        g = p_ref[0].astype(F32)
        for r in range(1, N_DEV):
            g = g + p_ref[r].astype(F32)
        m2 = ADAM_B1 * m_ref[0] + (1.0 - ADAM_B1) * g
        v2 = ADAM_B2 * v_ref[0] + (1.0 - ADAM_B2) * jnp.square(g)
        m_hat = m2 / (1.0 - ADAM_B1 ** ADAM_STEP)
        v_hat = v2 / (1.0 - ADAM_B2 ** ADAM_STEP)
        g_ref[0] = g
        d_ref[0] = -ADAM_LR * (m_hat / (jnp.sqrt(v_hat) + ADAM_EPS) + ADAM_WD * w_ref[0])
        nm_ref[0] = m2
        nv_ref[0] = v2

    blk = pl.BlockSpec((1, br, cols), lambda i: (0, i, 0))
    shape = jax.ShapeDtypeStruct((1, rows, cols), F32)
    return pl.pallas_call(
        body, name=name, grid=(rows // br,),
        in_specs=[pl.BlockSpec((N_DEV, br, cols), lambda i: (0, i, 0)), blk, blk, blk],
        out_specs=(blk, blk, blk, blk), out_shape=(shape, shape, shape, shape), compiler_params=_params(1),
    )(parts, w, m, v)


_QA, _KA, _VA, _FA, _QD, _KD, _VD = (0, 512), (512, 1024), (1024, 1536), (1536, 1544), (1544, 2056), (2056, 2568), (2568, 3080)
_MAIN_ORDER = (_QA, _QD, _KA, _KD, _VA, _VD)
MAIN_COLS = 3 * D_MODEL
PROJ_COLS = MAIN_COLS + LANES
COL_SHARDED = ("w_in", "w_gate", "w_up")


def _swap(w):
    return jnp.transpose(w, (0, 2, 1))


def _w_in_to_kernel(w_t):
    main = jnp.concatenate([w_t[a:b] for a, b in _MAIN_ORDER], axis=0)
    forget = jnp.pad(w_t[_FA[0]:_FA[1]], ((0, LANES - N_HEADS_FOX), (0, 0)))
    return main, forget


def _w_in_from_kernel(g_t):
    pos = {span: i * W_GROUP for i, span in enumerate(_MAIN_ORDER)}
    parts = []
    for span in (_QA, _KA, _VA, _FA, _QD, _KD, _VD):
        if span == _FA:
            parts.append(g_t[MAIN_COLS:MAIN_COLS + N_HEADS_FOX])
        else:
            parts.append(g_t[pos[span]:pos[span] + W_GROUP])
    return jnp.concatenate(parts, axis=0)


def _pack_small(vals):
    rows = []
    for name, _, n_rows in SMALL_LAYOUT:
        flat = vals[name].reshape(-1).astype(F32)
        rows.append(jnp.pad(flat, (0, n_rows * LANES - flat.shape[0])).reshape(n_rows, LANES))
    packed = jnp.concatenate(rows, axis=0)
    return jnp.pad(packed, ((0, SMALL_ROWS - packed.shape[0]), (0, 0)))


def _unpack_small(packed, like):
    out = {}
    for name, row, n_rows in SMALL_LAYOUT:
        n = like[name].size
        out[name] = packed[row:row + n_rows].reshape(-1)[:n].reshape(like[name].shape)
    return out


def _device_step(x, target, small, shards):
    bsz, seq, _ = x.shape
    t = bsz * seq
    xf = x.reshape(t, D_MODEL)
    tf = target.reshape(t, D_MODEL)
    row = lambda v: v.reshape(1, -1)
    g_out = jnp.concatenate([small["g_out_fox"], small["g_out_dil"]]).reshape(1, D_MODEL)
    gains = jnp.concatenate(
        [jnp.tile(small[n].reshape(1, HEAD_DIM), (1, 2)) for n in ("g_q_fox", "g_q_dil", "g_k_fox", "g_k_dil")]
        + [jnp.zeros((4, LANES), F32)], axis=0)
    b_pad = jnp.pad(small["b_forget"].reshape(1, N_HEADS_FOX), ((0, 0), (0, LANES - N_HEADS_FOX)))
    rope = _rope_tables(seq)
    tables_qk = _bias_tables(seq, keys_first=False)
    tables_kq = _bias_tables(seq, keys_first=True)

    h1, g_in = _rmsnorm_fwd(xf, row(small["g_mix"]), group=D_MODEL, name="norm_mix",
                            hosted=_ChipGather([(shards["w_in"], False)]))
    w_main_t, w_fa_t = _w_in_to_kernel(g_in.reshape(IN_COLS, D_MODEL))
    w_in_all_t = jnp.concatenate([w_main_t, w_fa_t], axis=0)
    proj = _matmul_nt(h1, w_main_t, name="in_proj", out_dtype=BF16)
    fa = _matmul_nt(h1, w_fa_t, name="in_proj_forget", out_dtype=F32)
    qk = _qk_prep_fwd(proj, fa, b_pad, gains, rope, seq)
    late = _Exchange([(shards[n], False) for n in ("w_out", "w_gate", "w_up", "w_down")])
    o, lse, g_out_w, g_gate, g_up, g_down = _attn_fwd(qk, proj, tables_qk, seq, hosted=late)
    w_out = g_out_w.reshape(D_MODEL, D_MODEL)
    w_gate_t = g_gate.reshape(D_FF, D_MODEL)
    w_up_t = g_up.reshape(D_FF, D_MODEL)
    w_down = g_down.reshape(D_FF, D_MODEL)
    on, x2, h2 = _out_proj_ffn_norm(o, g_out, w_out, xf, row(small["g_ffn"]))
    a, u, f = _ffn_gate_up(h2, w_gate_t, w_up_t)
    dy, dy16, loss_tile = _ffn_down_loss(f, w_down, x2, tf)

    da, du = _ffn_down_grad(dy16, w_down, a, u)
    gw_down = _matmul_tn(f, dy16, name="gw_down")
    gw_gate_t = _matmul_tn(da, h2, name="gw_gate")
    gw_up_t = _matmul_tn(du, h2, name="gw_up")
    dx2, dx2_16, dg_ffn = _norm_input_grad([(da, w_gate_t, True), (du, w_up_t, True)], x2, row(small["g_ffn"]),
                                           group=D_MODEL, name="d_ffn_gate_up", out_dtypes=(F32, BF16), resid=dy,
                                           k_chunks=2)
    gw_out = _matmul_tn(on, dx2_16, name="gw_out")
    do, dg_out = _norm_input_grad([(dx2_16, w_out, False)], o, g_out, group=W_GROUP, name="d_out_proj",
                                  out_dtypes=(BF16,))

    shard_rows = lambda g: g.reshape(N_DEV, g.shape[0] // N_DEV, g.shape[1])
    ffn_grads = _Exchange([(shard_rows(g), True) for g in (gw_out, gw_gate_t, gw_up_t, gw_down)])
    dq, dk, dv, dqx, dkx, p_out, p_gate, p_up, p_down = _attn_bwd(qk, proj, tables_kq, o, lse, do, seq, hosted=ffn_grads)
    dproj, dgains, db = _qk_prep_bwd(dq, dk, dqx, dkx, dv, proj, fa, b_pad, gains, rope, seq)
    gw_in_t = _matmul_tn(dproj, h1, name="gw_in")
    in_grad = _Exchange([(shard_rows(_w_in_from_kernel(gw_in_t)), True)])
    dx, dg_mix, p_in = _norm_input_grad([(dproj, w_in_all_t, True)], xf, row(small["g_mix"]), group=D_MODEL,
                                        name="d_in_proj", out_dtypes=(F32,), resid=dx2, hosted=in_grad)

    fold = lambda rows: jnp.sum(rows[:, :HEAD_DIM] + rows[:, HEAD_DIM:], axis=0)
    half = N_PAIRS // 2
    gsmall = {
        "g_mix": dg_mix, "g_ffn": dg_ffn, "g_out_fox": dg_out[0, :W_GROUP], "g_out_dil": dg_out[0, W_GROUP:],
        "g_q_fox": fold(dgains[0:half]), "g_q_dil": fold(dgains[half:N_PAIRS]),
        "g_k_fox": fold(dgains[N_PAIRS:N_PAIRS + half]), "g_k_dil": fold(dgains[N_PAIRS + half:]),
        "b_forget": db[0, :N_HEADS_FOX],
    }
    packed = _pack_small(gsmall).at[LOSS_ROW].set(loss_tile[0])
    (p_small,) = _exchange("small_exchange", [(packed, False)])
    parts = {"w_in": p_in, "w_out": p_out, "w_gate": p_gate, "w_up": p_up, "w_down": p_down}
    return dx.reshape(x.shape), parts, p_small


def kernel(x, g_mix, w_in, b_forget, g_q_fox, g_k_fox, g_q_dil, g_k_dil, g_out_fox, g_out_dil, w_out, g_ffn, w_gate, w_up, w_down, loss_target, m_g_mix, m_w_in, m_b_forget, m_g_q_fox, m_g_k_fox, m_g_q_dil, m_g_k_dil, m_g_out_fox, m_g_out_dil, m_w_out, m_g_ffn, m_w_gate, m_w_up, m_w_down, v_g_mix, v_w_in, v_b_forget, v_g_q_fox, v_g_k_fox, v_g_q_dil, v_g_k_dil, v_g_out_fox, v_g_out_dil, v_w_out, v_g_ffn, v_w_gate, v_w_up, v_w_down):
    args = dict(locals())
    small_names = [name for name, _, _ in SMALL_LAYOUT]
    big_names = ["w_in", "w_out", "w_gate", "w_up", "w_down"]
    small = {n: args[n][0] for n in small_names}

    as_rows = lambda n, w: _swap(w) if n in COL_SHARDED else w
    shards = {n: as_rows(n, args[n])[0].astype(BF16) for n in big_names}
    grad_x, parts, p_small = _device_step(x, loss_target, small, shards)

    grads, deltas, new_m, new_v = {}, {}, {}, {}
    for n in big_names:
        res = _adamw(parts[n], as_rows(n, args[n]), as_rows(n, args["m_" + n]), as_rows(n, args["v_" + n]),
                     name="adamw_" + n)
        grads[n], deltas[n], new_m[n], new_v[n] = [as_rows(n, r) for r in res]
    res = _adamw(p_small, _pack_small(small)[None], _pack_small({n: args["m_" + n][0] for n in small_names})[None],
                 _pack_small({n: args["v_" + n][0] for n in small_names})[None], name="adamw_small")
    loss = res[0][0, LOSS_ROW, 0]
    for dst, packed_res in zip((grads, deltas, new_m, new_v), res):
        for n, val in _unpack_small(packed_res[0], small).items():
            dst[n] = val[None]

    order = ["g_mix", "w_in", "b_forget", "g_q_fox", "g_k_fox", "g_q_dil", "g_k_dil", "g_out_fox", "g_out_dil",
             "w_out", "g_ffn", "w_gate", "w_up", "w_down"]
    return (loss, grad_x, *[grads[n] for n in order], *[deltas[n] for n in order],
            *[new_m[n] for n in order], *[new_v[n] for n in order])
```

```python
import functools
import math

import jax
import jax.numpy as jnp
import numpy as np
from jax import lax
from jax.experimental import pallas as pl
from jax.experimental.pallas import tpu as pltpu

F32 = jnp.float32
BF16 = jnp.bfloat16

D_MODEL = 1024
HEAD_DIM = 64
LANES = 128
N_PAIRS = D_MODEL // LANES
N_HEADS = 2 * N_PAIRS
N_HEADS_FOX = 8
W_GROUP = 512
D_FF = 2816
IN_COLS = 3080
DILATION_PAIRS = ((128, 1), (512, 4), (2048, 16))
ROPE_THETA = 500000.0
ROPE_DIM = 16
ROPE_HALF = ROPE_DIM // 2
EPS = 1e-6
NEG = -1e30
LOG2E = 1.4426950408889634
LN2 = 0.6931471805599453
AUG_ONE = 0
AUG_C = 3
N_DEV = 8

ADAM_LR = 0.001
ADAM_B1 = 0.9
ADAM_B2 = 0.999
ADAM_EPS = 1e-08
ADAM_WD = 0.01
ADAM_STEP = 10

ROW_BLOCK = 512
ATT_BLOCK = 512
ATT_GROUP = 4
VMEM_LIMIT = 56 * 1024 * 1024
MATMUL_VMEM_BUDGET = 44 * 1024 * 1024

SMALL_ROWS = 32
SMALL_LAYOUT = (("g_mix", 0, 8), ("g_ffn", 8, 8), ("g_out_fox", 16, 4), ("g_out_dil", 20, 4),
                ("g_q_fox", 24, 1), ("g_k_fox", 25, 1), ("g_q_dil", 26, 1), ("g_k_dil", 27, 1),
                ("b_forget", 28, 1))
LOSS_ROW = 29


def _params(n_grid):
    return pltpu.CompilerParams(dimension_semantics=("arbitrary",) * n_grid, vmem_limit_bytes=VMEM_LIMIT)


def _divisor_block(n, cap):
    best = None
    for b in range(LANES, min(n, cap) + 1, LANES):
        if n % b == 0:
            best = b
    assert best is not None, n
    return best


def _split_dot(a, b_exact, terms):
    acc = None
    rest = a
    for _ in range(terms):
        hi = rest.astype(BF16)
        part = jnp.dot(hi, b_exact, preferred_element_type=F32)
        acc = part if acc is None else acc + part
        rest = rest - hi.astype(F32)
    return acc


def _split_dot_rhs(a_exact, b, terms):
    acc = None
    rest = b
    for _ in range(terms):
        hi = rest.astype(BF16)
        part = jnp.dot(a_exact, hi, preferred_element_type=F32)
        acc = part if acc is None else acc + part
        rest = rest - hi.astype(F32)
    return acc


def _split_dot_nt(a_exact, b, terms):
    acc = None
    rest = b
    for _ in range(terms):
        hi = rest.astype(BF16)
        part = _dot_nt(a_exact, hi)
        acc = part if acc is None else acc + part
        rest = rest - hi.astype(F32)
    return acc


def _dot_nt(a, b):
    return lax.dot_general(a, b, (((1,), (1,)), ((), ())), preferred_element_type=F32)


def _dot_tn(a, b):
    return lax.dot_general(a, b, (((0,), (0,)), ((), ())), preferred_element_type=F32)


class _Exchange:
    def __init__(self, items):
        self.items = items
        self.n = len(items)
        self.arrays = [a for a, _ in items]
        self.out_shape = [jax.ShapeDtypeStruct((N_DEV,) + tuple(a.shape[1:] if sc else a.shape), a.dtype)
                          for a, sc in items]
        self.specs = [pl.BlockSpec(memory_space=pl.ANY)] * self.n
        self.scratch = [pltpu.SemaphoreType.DMA((self.n, N_DEV - 1)), pltpu.SemaphoreType.DMA((self.n, N_DEV - 1)),
                        pltpu.SemaphoreType.DMA((self.n,))]

    def run(self, ins, outs, sems, first, last, compute):
        send_sems, recv_sems, local_sems = sems
        x, y, c = lax.axis_index("x"), lax.axis_index("y"), lax.axis_index("c")
        me = 4 * x + 2 * y + c
        local, remote = [], []
        for k, (_, scatter) in enumerate(self.items):
            own = ins[k].at[me] if scatter else ins[k]
            local.append(pltpu.make_async_copy(own, outs[k].at[me], local_sems.at[k]))
        for r in range(1, N_DEV):
            px = 1 - x if r & 4 else x
            py = 1 - y if r & 2 else y
            pc = 1 - c if r & 1 else c
            peer = 4 * px + 2 * py + pc
            for k, (_, scatter) in enumerate(self.items):
                src = ins[k].at[peer] if scatter else ins[k]
                remote.append(pltpu.make_async_remote_copy(
                    src_ref=src, dst_ref=outs[k].at[me],
                    send_sem=send_sems.at[k, r - 1], recv_sem=recv_sems.at[k, r - 1],
                    device_id=(px, py, pc), device_id_type=pl.DeviceIdType.MESH))

        def start():
            for cp in local + remote:
                cp.start()

        def finish():
            for cp in remote:
                cp.wait_recv()
            for cp in remote:
                cp.wait_send()
            for cp in local:
                cp.wait()

        _run_phases(first, last, start, compute, finish)


def _run_phases(first, last, start, compute, finish):
    if first is None:
        start()
        compute()
        finish()
    else:
        pl.when(first)(start)
        compute()
        pl.when(last)(finish)


class _ChipGather(_Exchange):
    def run(self, ins, outs, sems, first, last, compute):
        send_sems, recv_sems, local_sems = sems
        x, y, c = lax.axis_index("x"), lax.axis_index("y"), lax.axis_index("c")
        sibling = (x, y, 1 - c)
        chips = [(1 - x, y), (x, 1 - y), (1 - x, 1 - y)]
        slot = lambda px, py, pc: 4 * px + 2 * py + pc

        def copy(k, n, src, dst_slot, to):
            return pltpu.make_async_remote_copy(
                src_ref=src, dst_ref=outs[k].at[dst_slot], send_sem=send_sems.at[k, n], recv_sem=recv_sems.at[k, n],
                device_id=to, device_id_type=pl.DeviceIdType.MESH)

        local, own, passed, arrivals = [], [], [], []
        for k in range(self.n):
            me = slot(x, y, c)
            local.append(pltpu.make_async_copy(ins[k], outs[k].at[me], local_sems.at[k]))
            own.append(copy(k, 0, ins[k], me, sibling))
            arrivals.append(copy(k, 0, ins[k], slot(*sibling), sibling))
            for j, chip in enumerate(chips):
                theirs = slot(*chip, c)
                own.append(copy(k, 1 + j, ins[k], me, (*chip, c)))
                passed.append((copy(k, 1 + j, ins[k], theirs, sibling),
                               copy(k, 4 + j, outs[k].at[theirs], theirs, sibling)))
                arrivals.append(copy(k, 4 + j, ins[k], slot(*chip, 1 - c), sibling))

        def start():
            for cp in local + own:
                cp.start()

        def finish():
            for landed, onward in passed:
                landed.wait_recv()
                onward.start()
            for cp in arrivals:
                cp.wait_recv()
            for cp in own + [onward for _, onward in passed]:
                cp.wait_send()
            for cp in local:
                cp.wait()

        _run_phases(first, last, start, compute, finish)


def _grid_ends(grid):
    ids = [pl.program_id(d) for d in range(len(grid))]
    first = functools.reduce(jnp.logical_and, [i == 0 for i in ids])
    last = functools.reduce(jnp.logical_and, [i == g - 1 for i, g in zip(ids, grid)])
    return first, last


def _host(core, n_in, n_out, n_scratch, hosted, grid):
    if hosted is None:
        return core
    nh = hosted.n

    def body(*refs):
        ins, rest = refs[:n_in], refs[n_in:]
        h_ins, rest = rest[:nh], rest[nh:]
        outs, rest = rest[:n_out], rest[n_out:]
        h_outs, rest = rest[:nh], rest[nh:]
        scratch, sems = rest[:n_scratch], rest[n_scratch:]
        first, last = _grid_ends(grid)
        hosted.run(h_ins, h_outs, sems, first, last, lambda: core(*ins, *outs, *scratch))

    return body


def _hosted_parts(hosted):
    if hosted is None:
        return [], [], [], []
    return list(hosted.specs), list(hosted.out_shape), list(hosted.arrays), list(hosted.scratch)


def _exchange(name, items):
    ex = _Exchange(items)
    n = ex.n

    def body(*refs):
        ex.run(refs[:n], refs[n:2 * n], refs[2 * n:], None, None, lambda: None)

    return pl.pallas_call(
        body, name=name, out_shape=tuple(ex.out_shape), in_specs=ex.specs, out_specs=tuple(ex.specs),
        scratch_shapes=ex.scratch,
    )(*ex.arrays)


def _matmul_blocks(t, k, n, a_bytes, o_bytes):
    for bt, cap in ((1024, 1408), (1024, 512), (512, 512)):
        if t % bt:
            continue
        bn = _divisor_block(n, cap)
        if 2 * (bt * k * a_bytes + bn * k * 2 + bt * bn * o_bytes) <= MATMUL_VMEM_BUDGET:
            return bt, bn
    return ROW_BLOCK, _divisor_block(n, 256)


def _matmul_nt(a, w, *, name, out_dtype):
    t, k = a.shape
    n = w.shape[0]
    assert w.shape[1] == k
    bt, bn = _matmul_blocks(t, k, n, a.dtype.itemsize, jnp.dtype(out_dtype).itemsize)

    def body(a_ref, w_ref, o_ref):
        o_ref[...] = _dot_nt(a_ref[...], w_ref[...]).astype(o_ref.dtype)

    return pl.pallas_call(
        body, name=name, grid=(t // bt, n // bn),
        in_specs=[pl.BlockSpec((bt, k), lambda i, j: (i, 0)), pl.BlockSpec((bn, k), lambda i, j: (j, 0))],
        out_specs=pl.BlockSpec((bt, bn), lambda i, j: (i, j)),
        out_shape=jax.ShapeDtypeStruct((t, n), out_dtype), compiler_params=_params(2),
    )(a, w)


def _matmul_tn(a, b, *, name):
    t, m = a.shape
    n = b.shape[1]
    bt = 2048 if t % 2048 == 0 else ROW_BLOCK
    bm = _divisor_block(m, 1408)
    bn = _divisor_block(n, 1408)
    steps = t // bt

    def body(a_ref, b_ref, o_ref, acc):
        step = pl.program_id(2)

        @pl.when(step == 0)
        def _():
            acc[...] = jnp.zeros_like(acc)

        acc[...] += _dot_tn(a_ref[...], b_ref[...])

        @pl.when(step == steps - 1)
        def _():
            o_ref[...] = acc[...].astype(o_ref.dtype)

    return pl.pallas_call(
        body, name=name, grid=(m // bm, n // bn, steps),
        in_specs=[pl.BlockSpec((bt, bm), lambda i, j, s: (s, i)), pl.BlockSpec((bt, bn), lambda i, j, s: (s, j))],
        out_specs=pl.BlockSpec((bm, bn), lambda i, j, s: (i, j)),
        out_shape=jax.ShapeDtypeStruct((m, n), BF16), scratch_shapes=[pltpu.VMEM((bm, bn), F32)],
        compiler_params=_params(3),
    )(a, b)


def _rmsnorm_fwd(x, g, *, group, name, hosted=None):
    t, w = x.shape
    bt = ROW_BLOCK

    def body(x_ref, g_ref, o_ref):
        for s in range(0, w, group):
            xs = x_ref[:, s:s + group].astype(F32)
            r = lax.rsqrt(jnp.mean(xs * xs, axis=-1, keepdims=True) + EPS)
            o_ref[:, s:s + group] = (xs * r * g_ref[:, s:s + group]).astype(o_ref.dtype)

    grid = (t // bt,)
    h_specs, h_shapes, h_args, h_scratch = _hosted_parts(hosted)
    res = pl.pallas_call(
        _host(body, 2, 1, 0, hosted, grid), name=name, grid=grid,
        in_specs=[pl.BlockSpec((bt, w), lambda i: (i, 0)), pl.BlockSpec((1, w), lambda i: (0, 0))] + h_specs,
        out_specs=tuple([pl.BlockSpec((bt, w), lambda i: (i, 0))] + h_specs),
        out_shape=tuple([jax.ShapeDtypeStruct((t, w), BF16)] + h_shapes),
        scratch_shapes=h_scratch, compiler_params=_params(1),
    )(x, g, *h_args)
    return res if hosted else res[0]


def _norm_input_grad(terms, x, g, *, group, name, out_dtypes, resid=None, hosted=None, k_chunks=1):
    t, w = x.shape
    n_terms = len(terms)
    kc = [a.shape[1] // k_chunks for a, _, _ in terms]
    per_row = sum(c * a.dtype.itemsize for c, (a, _, _) in zip(kc, terms)) + w * (x.dtype.itemsize + 4 * (resid is not None))
    per_row += w * sum(jnp.dtype(dt).itemsize for dt in out_dtypes)
    fixed = 2 * sum(w * c * 2 for c in kc)
    bt = next(b for b in (1024, 512, 256, 128)
              if t % b == 0 and fixed + 2 * b * per_row + 5 * b * w * 4 <= MATMUL_VMEM_BUDGET)
    n_in = 2 * n_terms + 2 + (resid is not None)
    grid = (t // bt, k_chunks)

    def body(*refs):
        x_ref, g_ref = refs[2 * n_terms], refs[2 * n_terms + 1]
        dx_refs, dg_ref, dh_ref = refs[n_in:-2], refs[-2], refs[-1]
        chunk = pl.program_id(1)

        @pl.when((pl.program_id(0) == 0) & (chunk == 0))
        def _():
            dg_ref[...] = jnp.zeros_like(dg_ref)

        part = None
        for k in range(n_terms):
            if terms[k][2]:
                term = jnp.dot(refs[2 * k][...], refs[2 * k + 1][...], preferred_element_type=F32)
            else:
                term = _dot_nt(refs[2 * k][...], refs[2 * k + 1][...])
            part = term if part is None else part + term

        @pl.when(chunk == 0)
        def _():
            dh_ref[...] = part

        @pl.when(chunk > 0)
        def _():
            dh_ref[...] += part

        @pl.when(chunk == k_chunks - 1)
        def _():
            for s in range(0, w, group):
                xs = x_ref[:, s:s + group].astype(F32)
                dhs = dh_ref[:, s:s + group]
                r = lax.rsqrt(jnp.mean(xs * xs, axis=-1, keepdims=True) + EPS)
                xh = xs * r
                dg_ref[:, s:s + group] += jnp.sum(dhs * xh, axis=0, keepdims=True)
                dxh = dhs * g_ref[:, s:s + group]
                dx = r * (dxh - xh * jnp.mean(dxh * xh, axis=-1, keepdims=True))
                if resid is not None:
                    dx = refs[n_in - 1][:, s:s + group] + dx
                for dx_ref in dx_refs:
                    dx_ref[:, s:s + group] = dx.astype(dx_ref.dtype)

    row = pl.BlockSpec((bt, w), lambda i, k: (i, 0))
    vec = pl.BlockSpec((1, w), lambda i, k: (0, 0))
    in_specs, args = [], []
    for c, (a, wt, w_is_kn) in zip(kc, terms):
        assert wt.shape == ((a.shape[1], w) if w_is_kn else (w, a.shape[1]))
        w_spec = pl.BlockSpec((c, w), lambda i, k: (k, 0)) if w_is_kn else pl.BlockSpec((w, c), lambda i, k: (0, k))
        in_specs += [pl.BlockSpec((bt, c), lambda i, k: (i, k)), w_spec]
        args += [a, wt]
    in_specs += [row, vec] + ([row] if resid is not None else [])
    args += [x, g] + ([resid] if resid is not None else [])
    h_specs, h_shapes, h_args, h_scratch = _hosted_parts(hosted)
    return pl.pallas_call(
        _host(body, n_in, len(out_dtypes) + 1, 1, hosted, grid), name=name, grid=grid, in_specs=in_specs + h_specs,
        out_specs=tuple([row] * len(out_dtypes) + [vec] + h_specs),
        out_shape=tuple([jax.ShapeDtypeStruct((t, w), dt) for dt in out_dtypes] + [jax.ShapeDtypeStruct((1, w), F32)]
                        + h_shapes),
        scratch_shapes=[pltpu.VMEM((bt, w), F32)] + h_scratch, compiler_params=_params(2),
    )(*args, *h_args)


def _tile_plan(tile):
    is_q = tile < N_PAIRS
    is_dil = (tile % N_PAIRS) >= N_PAIRS // 2
    return is_q, is_dil, (0 if is_q else 2) + (1 if is_dil else 0)


def _segment_ones():
    lane = np.arange(LANES)
    return jnp.asarray((lane[:, None] // HEAD_DIM) == (lane[None, :] // HEAD_DIM), BF16)


def _rope_tables(seq):
    inv_freq = jnp.power(jnp.float32(ROPE_THETA), -jnp.arange(ROPE_HALF, dtype=F32) * 2.0 / ROPE_DIM)
    ang = jnp.arange(seq).astype(F32)[:, None] * inv_freq[None, :]
    cos, sin = jnp.cos(ang), jnp.sin(ang)
    ones = jnp.ones((seq, HEAD_DIM - ROPE_DIM), F32)
    zeros = jnp.zeros((seq, HEAD_DIM - ROPE_DIM), F32)
    zh = jnp.zeros((seq, ROPE_HALF), F32)
    cos_t = jnp.concatenate([cos, cos, ones], axis=1)
    sin_a = jnp.concatenate([-sin, zh, zeros], axis=1)
    sin_b = jnp.concatenate([zh, sin, zeros], axis=1)
    return tuple(jnp.tile(tab, (1, 2)) for tab in (cos_t, sin_a, sin_b))


def _log_sigmoid(z):
    return jnp.minimum(z, 0.0) - jnp.log1p(jnp.exp(-jnp.abs(z)))


def _qk_prep_fwd(proj, fa, b_pad, gains, rope, seq):
    t = proj.shape[0]
    bt = ROW_BLOCK
    nsb = seq // bt
    seg = _segment_ones()
    rr = np.arange(bt)
    tri = jnp.asarray(rr[:, None] >= rr[None, :], BF16)

    def body(p_ref, fa_ref, b_ref, g_ref, cos_ref, sa_ref, sb_ref, seg_ref, tri_ref, qk_ref, carry):
        @pl.when(pl.program_id(0) % nsb == 0)
        def _():
            carry[...] = jnp.zeros_like(carry)

        lane = lax.broadcasted_iota(jnp.int32, (bt, LANES), 1)
        logf = jnp.where(lane < N_HEADS_FOX, _log_sigmoid(fa_ref[...] + b_ref[...]), 0.0)
        cblk = _split_dot_rhs(tri_ref[...], logf, 3) + carry[0:1, :]
        carry[0:1, :] = cblk[bt - 1:bt, :]
        c_terms = []
        rest = cblk * LOG2E
        for _ in range(3):
            term = rest.astype(BF16).astype(F32)
            c_terms.append(term)
            rest = rest - term

        for tile in range(2 * N_PAIRS):
            is_q, is_dil, grow = _tile_plan(tile)
            pair = tile % N_PAIRS
            xs = p_ref[:, tile * LANES:(tile + 1) * LANES].astype(F32)
            r = lax.rsqrt(_split_dot(xs * xs, seg_ref[...], 2) * (1.0 / HEAD_DIM) + EPS)
            yv = xs * r * g_ref[grow:grow + 1, :]
            if is_dil:
                yv = (yv * cos_ref[...] + pltpu.roll(yv, LANES - ROPE_HALF, 1) * sa_ref[...]
                      + pltpu.roll(yv, ROPE_HALF, 1) * sb_ref[...])
            if is_q:
                yv = yv * (HEAD_DIM ** -0.5 * LOG2E)
            for e in range(2):
                head = 2 * pair + e
                other = HEAD_DIM * (1 - e)
                aug = jnp.zeros((bt, LANES), F32)
                if not is_dil:
                    ones_at = other + (AUG_ONE if is_q else AUG_C)
                    c_at = other + (AUG_C if is_q else AUG_ONE)
                    aug = jnp.where((lane >= ones_at) & (lane < ones_at + 3), 1.0, aug)
                    for n, term in enumerate(c_terms):
                        col = term[:, head:head + 1]
                        aug = jnp.where(lane == c_at + n, col if is_q else -col, aug)
                mine = (lane < HEAD_DIM) if e == 0 else (lane >= HEAD_DIM)
                dst = ((0 if is_q else N_HEADS) + head) * LANES
                qk_ref[:, dst:dst + LANES] = jnp.where(mine, yv, aug).astype(BF16)

    row128 = pl.BlockSpec((bt, LANES), lambda i: (i, 0))
    rope_spec = pl.BlockSpec((bt, LANES), lambda i: (i % nsb, 0))
    const = lambda shape: pl.BlockSpec(shape, lambda i: (0, 0))
    return pl.pallas_call(
        body, name="qk_prep_fwd", grid=(t // bt,),
        in_specs=[pl.BlockSpec((bt, 2 * D_MODEL), lambda i: (i, 0)), row128, const((1, LANES)), const((8, LANES)),
                  rope_spec, rope_spec, rope_spec, const((LANES, LANES)), const((bt, bt))],
        out_specs=pl.BlockSpec((bt, 2 * N_HEADS * LANES), lambda i: (i, 0)),
        out_shape=jax.ShapeDtypeStruct((t, 2 * N_HEADS * LANES), BF16),
        scratch_shapes=[pltpu.VMEM((8, LANES), F32)], compiler_params=_params(1),
    )(proj, fa, b_pad, gains, *rope, seg, tri)


def _qk_prep_bwd(dq, dk, dqx, dkx, dv, proj, fa, b_pad, gains, rope, seq):
    t = proj.shape[0]
    bt = ROW_BLOCK
    nsb = seq // bt
    nblk = t // bt
    seg = _segment_ones()
    rr = np.arange(bt)
    triu = jnp.asarray(rr[:, None] <= rr[None, :], BF16)

    def body(dq_ref, dk_ref, dqx_ref, dkx_ref, dv_ref, p_ref, fa_ref, b_ref, g_ref, cos_ref, sa_ref, sb_ref, seg_ref,
             triu_ref, dp_ref, dg_ref, db_ref, carry):
        step = pl.program_id(0)

        @pl.when(step == 0)
        def _():
            dg_ref[...] = jnp.zeros_like(dg_ref)
            db_ref[...] = jnp.zeros_like(db_ref)

        @pl.when(step % nsb == 0)
        def _():
            carry[...] = jnp.zeros_like(carry)

        for tile in range(2 * N_PAIRS):
            is_q, is_dil, grow = _tile_plan(tile)
            cols = slice(tile * LANES, (tile + 1) * LANES)
            src = dq_ref if is_q else dk_ref
            half = slice((tile % N_PAIRS) * LANES, (tile % N_PAIRS + 1) * LANES)
            dy = src[:, half]
            dy = dy * (HEAD_DIM ** -0.5 if is_q else LN2)
            if is_dil:
                dy = (dy * cos_ref[...] + pltpu.roll(dy * sa_ref[...], ROPE_HALF, 1)
                      + pltpu.roll(dy * sb_ref[...], LANES - ROPE_HALF, 1))
            xs = p_ref[:, cols].astype(F32)
            r = lax.rsqrt(_split_dot(xs * xs, seg_ref[...], 2) * (1.0 / HEAD_DIM) + EPS)
            xh = xs * r
            dg_ref[tile:tile + 1, :] += jnp.sum(dy * xh, axis=0, keepdims=True)
            dxh = dy * g_ref[grow:grow + 1, :]
            seg_mean = _split_dot(dxh * xh, seg_ref[...], 2) * (1.0 / HEAD_DIM)
            dp_ref[:, cols] = (r * (dxh - xh * seg_mean)).astype(BF16)

        lane = lax.broadcasted_iota(jnp.int32, (bt, LANES), 1)
        dc = jnp.zeros((bt, LANES), F32)
        for h in range(N_HEADS_FOX):
            other = (h // 2) * LANES + HEAD_DIM * (1 - h % 2)
            row_sum = dqx_ref[:, other + AUG_C:other + AUG_C + 1]
            col_sum = dkx_ref[:, other + AUG_ONE:other + AUG_ONE + 1]
            dc = jnp.where(lane == h, row_sum - col_sum, dc)
        dlogf = _split_dot_rhs(triu_ref[...], dc, 3) + carry[0:1, :]
        carry[0:1, :] = dlogf[0:1, :]
        z = fa_ref[...] + b_ref[...]
        dfa = dlogf * (1.0 / (1.0 + jnp.exp(z)))
        db_ref[0:1, :] += jnp.sum(dfa, axis=0, keepdims=True)
        dp_ref[:, 2 * D_MODEL:MAIN_COLS] = dv_ref[...]
        dp_ref[:, MAIN_COLS:PROJ_COLS] = dfa.astype(BF16)

    rev = lambda i: nblk - 1 - i
    row = lambda w: pl.BlockSpec((bt, w), lambda i: (rev(i), 0))
    rope_spec = pl.BlockSpec((bt, LANES), lambda i: (rev(i) % nsb, 0))
    const = lambda shape: pl.BlockSpec(shape, lambda i: (0, 0))
    return pl.pallas_call(
        body, name="qk_prep_bwd", grid=(nblk,),
        in_specs=[row(D_MODEL), row(D_MODEL), row(W_GROUP), row(W_GROUP), row(D_MODEL), row(2 * D_MODEL), row(LANES),
                  const((1, LANES)), const((8, LANES)), rope_spec, rope_spec, rope_spec, const((LANES, LANES)),
                  const((bt, bt))],
        out_specs=(row(PROJ_COLS), const((2 * N_PAIRS, LANES)), const((8, LANES))),
        out_shape=(jax.ShapeDtypeStruct((t, PROJ_COLS), BF16),
                   jax.ShapeDtypeStruct((2 * N_PAIRS, LANES), F32), jax.ShapeDtypeStruct((8, LANES), F32)),
        scratch_shapes=[pltpu.VMEM((8, LANES), F32)], compiler_params=_params(1),
    )(dq, dk, dqx, dkx, dv, proj, fa, b_pad, gains, *rope, seg, triu)


def _bias_tables(seq, keys_first):
    nb = seq // ATT_BLOCK
    idx = jnp.arange(ATT_BLOCK)
    q_idx, k_idx = (idx[None, None, :], idx[None, :, None]) if keys_first else (idx[None, :, None], idx[None, None, :])
    dist = jnp.arange(nb)[:, None, None] * ATT_BLOCK + q_idx - k_idx
    causal = dist >= 0
    count = jnp.zeros(dist.shape, jnp.int32)
    for window, dilation in DILATION_PAIRS:
        count = count + (causal & (dist % dilation == 0) & (dist <= window)).astype(jnp.int32)
    fox = jnp.where(causal, 0.0, NEG).astype(F32)
    dil = jnp.where(count == 3, math.log2(3.0), jnp.where(count == 2, 1.0, jnp.where(count == 1, 0.0, NEG)))
    return jnp.stack([fox, dil.astype(F32)], axis=0)


def _attn_specs(seq):
    nb = seq // ATT_BLOCK
    col = lambda off: pl.BlockSpec((seq, LANES), lambda b, j: (b, off + j))
    heads = lambda off: pl.BlockSpec((seq, 2 * LANES), lambda b, j: (b, off + j))
    table_spec = pl.BlockSpec((1, nb, ATT_BLOCK, ATT_BLOCK), lambda b, j: (j // (N_PAIRS // 2), 0, 0, 0))
    return col, heads, table_spec


def _head_lanes(e, shape, axis):
    pos = lax.broadcasted_iota(jnp.int32, shape, axis)
    return pos < HEAD_DIM if e == 0 else pos >= HEAD_DIM


def _attn_fwd(qk, proj, tables, seq, hosted=None):
    t = qk.shape[0]
    nb = seq // ATT_BLOCK
    blk = ATT_BLOCK

    def body(q_ref, k_ref, v_ref, tab_ref, o_ref, lse_ref):
        mine = [_head_lanes(e, (seq, LANES), 1) for e in range(2)]
        lane = lax.broadcasted_iota(jnp.int32, (seq, LANES), 1)
        v_aug = [jnp.where(mine[e], v_ref[...], (lane == HEAD_DIM * (1 - e)).astype(BF16)) for e in range(2)]
        for i in range(nb):
            rows = slice(i * blk, (i + 1) * blk)
            n_keys = (i + 1) * blk
            out, lse = [], []
            for e in range(2):
                heads_e = slice(e * LANES, (e + 1) * LANES)
                s = _dot_nt(q_ref[rows, heads_e], k_ref[0:n_keys, heads_e])
                s = jnp.concatenate([s[:, jj * blk:(jj + 1) * blk] + tab_ref[0, i - jj] for jj in range(i + 1)], axis=1)
                m = jnp.max(s, axis=1, keepdims=True)
                acc = jnp.dot(jnp.exp2(s - m).astype(BF16), v_aug[e][0:n_keys], preferred_element_type=F32)
                ones_at = HEAD_DIM * (1 - e)
                l = acc[:, ones_at:ones_at + 1]
                out.append(acc / l)
                lse.append(m + jnp.log2(l))
            o_ref[rows, :] = jnp.where(mine[0][rows], out[0], out[1]).astype(o_ref.dtype)
            lse_ref[rows, :] = jnp.where(mine[0][rows], lse[0], lse[1])

    col, heads, table_spec = _attn_specs(seq)
    grid = (t // seq, N_PAIRS)
    h_specs, h_shapes, h_args, h_scratch = _hosted_parts(hosted)
    return pl.pallas_call(
        _host(body, 4, 2, 0, hosted, grid), name="attn_fwd", grid=grid,
        in_specs=[heads(0), heads(N_PAIRS), col(2 * N_PAIRS), table_spec] + h_specs,
        out_specs=tuple([col(0), col(0)] + h_specs),
        out_shape=tuple([jax.ShapeDtypeStruct((t, D_MODEL), BF16), jax.ShapeDtypeStruct((t, D_MODEL), F32)] + h_shapes),
        scratch_shapes=h_scratch, compiler_params=_params(2),
    )(qk, qk, proj, tables, *h_args)


def _attn_bwd(qk, proj, tables, o, lse, do, seq, hosted=None):
    t = qk.shape[0]
    nb = seq // ATT_BLOCK
    blk = ATT_BLOCK
    group = math.gcd(nb, ATT_GROUP)

    def body(q_ref, k_ref, v_ref, tab_ref, o_ref, lse_ref, do_ref,
             dq_ref, dk_ref, dv_ref, dqx_ref, dkx_ref, dk_acc, dv_acc):
        mine = [_head_lanes(e, (blk, LANES), 1) for e in range(2)]
        top = _head_lanes(0, (LANES, blk), 0)
        head_rows = lax.broadcasted_iota(jnp.int32, (8, LANES), 0)
        head_of_lane = lax.broadcasted_iota(jnp.int32, (8, LANES), 1) // HEAD_DIM
        head_sel = (head_rows == head_of_lane).astype(BF16)
        dk_acc[...] = jnp.zeros_like(dk_acc)
        dv_acc[...] = jnp.zeros_like(dv_acc)

        def block_rows(i):
            return pl.ds(pl.multiple_of(i * blk, blk), blk)

        def q_group(g, _):
            base = g * group
            qs, doe, delta, lse_e = [], [], [], []
            for b in range(group):
                rows = block_rows(base + b)
                qs.append([q_ref[rows, e * LANES:(e + 1) * LANES] for e in range(2)])
                do_blk = do_ref[rows, :]
                doe.append([jnp.where(mine[e], do_blk, jnp.zeros_like(do_blk)) for e in range(2)])
                delta_t = _split_dot_nt(head_sel, do_blk.astype(F32) * o_ref[rows, :].astype(F32), 3)
                lse_t = _split_dot_nt(head_sel, lse_ref[rows, :], 3) * (1.0 / HEAD_DIM)
                delta.append([delta_t[e:e + 1, :] for e in range(2)])
                lse_e.append([lse_t[e:e + 1, :] for e in range(2)])

            def key_block(dq_t, jj, members):
                krows = block_rows(jj)
                v = v_ref[krows, :]
                dq_t = [list(d) for d in dq_t]
                dv_part = None
                for e in range(2):
                    k_e = k_ref[krows, e * LANES:(e + 1) * LANES]
                    dk_part = None
                    for b, dist in members:
                        p_t = jnp.exp2(_dot_nt(k_e, qs[b][e]) + tab_ref[0, dist] - lse_e[b][e])
                        ds_t = (p_t * (_dot_nt(v, doe[b][e]) - delta[b][e])).astype(BF16)
                        part = jnp.dot(p_t.astype(BF16), doe[b][e], preferred_element_type=F32)
                        dv_part = part if dv_part is None else dv_part + part
                        part = jnp.dot(ds_t, qs[b][e], preferred_element_type=F32)
                        dk_part = part if dk_part is None else dk_part + part
                        dq_t[b][e] = dq_t[b][e] + _dot_tn(k_e, ds_t)
                    dk_acc[e, krows, :] += dk_part
                dv_acc[krows, :] += dv_part
                return tuple(tuple(d) for d in dq_t)

            zacc = jnp.zeros((LANES, blk), F32)
            dq_t = tuple((zacc, zacc) for _ in range(group))
            dq_t = lax.fori_loop(
                0, base, lambda jj, st: key_block(st, jj, [(b, base + b - jj) for b in range(group)]), dq_t)
            for a in range(group):
                dq_t = key_block(dq_t, base + a, [(b, b - a) for b in range(a, group)])
            for b in range(group):
                rows = block_rows(base + b)
                dq_ref[rows, :] = jnp.where(top, dq_t[b][0], dq_t[b][1]).T
                dqx_ref[rows, :] = jnp.where(top, dq_t[b][1], dq_t[b][0]).T
            return 0

        lax.fori_loop(0, nb // group, q_group, 0)
        lo = _head_lanes(0, (seq, LANES), 1)
        dk_ref[...] = jnp.where(lo, dk_acc[0], dk_acc[1])
        dkx_ref[...] = jnp.where(lo, dk_acc[1], dk_acc[0])
        dv_ref[...] = dv_acc[...].astype(dv_ref.dtype)

    col, heads, table_spec = _attn_specs(seq)
    grid = (t // seq, N_PAIRS)
    h_specs, h_shapes, h_args, h_scratch = _hosted_parts(hosted)
    f32_out = jax.ShapeDtypeStruct((t, D_MODEL), F32)
    return pl.pallas_call(
        _host(body, 7, 5, 2, hosted, grid), name="attn_bwd", grid=grid,
        in_specs=[heads(0), heads(N_PAIRS), col(2 * N_PAIRS), table_spec, col(0), col(0), col(0)] + h_specs,
        out_specs=tuple([col(0)] * 5 + h_specs),
        out_shape=tuple([f32_out, f32_out, jax.ShapeDtypeStruct((t, D_MODEL), BF16), f32_out, f32_out] + h_shapes),
        scratch_shapes=[pltpu.VMEM((2, seq, LANES), F32), pltpu.VMEM((seq, LANES), F32)] + h_scratch,
        compiler_params=_params(2),
    )(qk, qk, proj, tables, o, lse, do, *h_args)


def _row_block(t):
    return 1024 if t % 1024 == 0 else ROW_BLOCK


def _out_proj_ffn_norm(o, g_out, w_out, x, g_ffn):
    t = o.shape[0]
    bt = _row_block(t)

    def body(o_ref, go_ref, w_ref, x_ref, gf_ref, on_ref, x2_ref, h2_ref):
        for s in range(0, D_MODEL, W_GROUP):
            os_ = o_ref[:, s:s + W_GROUP].astype(F32)
            r = lax.rsqrt(jnp.mean(os_ * os_, axis=-1, keepdims=True) + EPS)
            on_ref[:, s:s + W_GROUP] = (os_ * r * go_ref[:, s:s + W_GROUP]).astype(BF16)
        x2 = x_ref[...] + jnp.dot(on_ref[...], w_ref[...], preferred_element_type=F32)
        x2_ref[...] = x2
        r2 = lax.rsqrt(jnp.mean(x2 * x2, axis=-1, keepdims=True) + EPS)
        h2_ref[...] = (x2 * r2 * gf_ref[...]).astype(BF16)

    row = pl.BlockSpec((bt, D_MODEL), lambda i: (i, 0))
    vec = pl.BlockSpec((1, D_MODEL), lambda i: (0, 0))
    return pl.pallas_call(
        body, name="out_proj", grid=(t // bt,),
        in_specs=[row, vec, pl.BlockSpec((D_MODEL, D_MODEL), lambda i: (0, 0)), row, vec],
        out_specs=(row, row, row),
        out_shape=(jax.ShapeDtypeStruct((t, D_MODEL), BF16), jax.ShapeDtypeStruct((t, D_MODEL), F32),
                   jax.ShapeDtypeStruct((t, D_MODEL), BF16)),
        compiler_params=_params(1),
    )(o, g_out, w_out, x, g_ffn)


def _ffn_gate_up(h2, w_gate_t, w_up_t):
    t = h2.shape[0]
    bt = _row_block(t)
    bn = _divisor_block(D_FF, 1408)

    def body(h_ref, wg_ref, wu_ref, a_ref, u_ref, f_ref):
        a = _dot_nt(h_ref[...], wg_ref[...])
        u = _dot_nt(h_ref[...], wu_ref[...])
        a_ref[...] = a.astype(BF16)
        u_ref[...] = u.astype(BF16)
        f_ref[...] = (a * jax.nn.sigmoid(a) * u).astype(BF16)

    blk = pl.BlockSpec((bt, bn), lambda i, j: (i, j))
    w_blk = pl.BlockSpec((bn, D_MODEL), lambda i, j: (j, 0))
    shape = jax.ShapeDtypeStruct((t, D_FF), BF16)
    return pl.pallas_call(
        body, name="ffn_gate_up", grid=(t // bt, D_FF // bn),
        in_specs=[pl.BlockSpec((bt, D_MODEL), lambda i, j: (i, 0)), w_blk, w_blk],
        out_specs=(blk, blk, blk), out_shape=(shape, shape, shape), compiler_params=_params(2),
    )(h2, w_gate_t, w_up_t)


def _ffn_down_grad(dy16, w_down, a, u):
    t = a.shape[0]
    bt = _row_block(t)
    bn = _divisor_block(D_FF, 1408)

    def body(dy_ref, w_ref, a_ref, u_ref, da_ref, du_ref):
        df = _dot_nt(dy_ref[...], w_ref[...])
        av = a_ref[...].astype(F32)
        sg = jax.nn.sigmoid(av)
        da_ref[...] = (df * u_ref[...].astype(F32) * sg * (1.0 + av * (1.0 - sg))).astype(BF16)
        du_ref[...] = (df * av * sg).astype(BF16)

    blk = pl.BlockSpec((bt, bn), lambda i, j: (i, j))
    shape = jax.ShapeDtypeStruct((t, D_FF), BF16)
    return pl.pallas_call(
        body, name="d_ffn_down", grid=(t // bt, D_FF // bn),
        in_specs=[pl.BlockSpec((bt, D_MODEL), lambda i, j: (i, 0)), pl.BlockSpec((bn, D_MODEL), lambda i, j: (j, 0)),
                  blk, blk],
        out_specs=(blk, blk), out_shape=(shape, shape), compiler_params=_params(2),
    )(dy16, w_down, a, u)


def _ffn_down_loss(f, w_down, x2, target):
    t, w = x2.shape
    bt = _row_block(t)

    def body(f_ref, w_ref, x_ref, t_ref, dy_ref, dy16_ref, loss_ref):
        @pl.when(pl.program_id(0) == 0)
        def _():
            loss_ref[...] = jnp.zeros_like(loss_ref)

        err = (x_ref[...] + jnp.dot(f_ref[...], w_ref[...], preferred_element_type=F32)) - t_ref[...]
        dy = err * (1.0 / w)
        dy_ref[...] = dy
        dy16_ref[...] = dy.astype(BF16)
        loss_ref[...] += 0.5 * jnp.sum(jnp.mean(err * err, axis=-1, keepdims=True), axis=0, keepdims=True)

    row = pl.BlockSpec((bt, w), lambda i: (i, 0))
    return pl.pallas_call(
        body, name="ffn_down_loss", grid=(t // bt,),
        in_specs=[pl.BlockSpec((bt, D_FF), lambda i: (i, 0)), pl.BlockSpec((D_FF, w), lambda i: (0, 0)), row, row],
        out_specs=(row, row, pl.BlockSpec((8, LANES), lambda i: (0, 0))),
        out_shape=(jax.ShapeDtypeStruct((t, w), F32), jax.ShapeDtypeStruct((t, w), BF16),
                   jax.ShapeDtypeStruct((8, LANES), F32)),
        compiler_params=_params(1),
    )(f, w_down, x2, target)


def _adamw(parts, w, m, v, *, name):
    _, rows, cols = w.shape
    br = rows if rows <= 512 else 256
    assert rows % br == 0

    def body(p_ref, w_ref, m_ref, v_ref, g_ref, d_ref, nm_ref, nv_ref):
        g = p_ref[0].astype(F32)
        for r in range(1, N_DEV):
            g = g + p_ref[r].astype(F32)
        m2 = ADAM_B1 * m_ref[0] + (1.0 - ADAM_B1) * g
        v2 = ADAM_B2 * v_ref[0] + (1.0 - ADAM_B2) * jnp.square(g)
        m_hat = m2 / (1.0 - ADAM_B1 ** ADAM_STEP)
        v_hat = v2 / (1.0 - ADAM_B2 ** ADAM_STEP)
        g_ref[0] = g
        d_ref[0] = -ADAM_LR * (m_hat / (jnp.sqrt(v_hat) + ADAM_EPS) + ADAM_WD * w_ref[0])
        nm_ref[0] = m2
        nv_ref[0] = v2

    blk = pl.BlockSpec((1, br, cols), lambda i: (0, i, 0))
    shape = jax.ShapeDtypeStruct((1, rows, cols), F32)
    return pl.pallas_call(
        body, name=name, grid=(rows // br,),
        in_specs=[pl.BlockSpec((N_DEV, br, cols), lambda i: (0, i, 0)), blk, blk, blk],
        out_specs=(blk, blk, blk, blk), out_shape=(shape, shape, shape, shape), compiler_params=_params(1),
    )(parts, w, m, v)


_QA, _KA, _VA, _FA, _QD, _KD, _VD = (0, 512), (512, 1024), (1024, 1536), (1536, 1544), (1544, 2056), (2056, 2568), (2568, 3080)
_MAIN_ORDER = (_QA, _QD, _KA, _KD, _VA, _VD)
MAIN_COLS = 3 * D_MODEL
PROJ_COLS = MAIN_COLS + LANES
COL_SHARDED = ("w_in", "w_gate", "w_up")


def _swap(w):
    return jnp.transpose(w, (0, 2, 1))


def _w_in_to_kernel(w_t):
    main = jnp.concatenate([w_t[a:b] for a, b in _MAIN_ORDER], axis=0)
    forget = jnp.pad(w_t[_FA[0]:_FA[1]], ((0, LANES - N_HEADS_FOX), (0, 0)))
    return main, forget


def _w_in_from_kernel(g_t):
    pos = {span: i * W_GROUP for i, span in enumerate(_MAIN_ORDER)}
    parts = []
    for span in (_QA, _KA, _VA, _FA, _QD, _KD, _VD):
        if span == _FA:
            parts.append(g_t[MAIN_COLS:MAIN_COLS + N_HEADS_FOX])
        else:
            parts.append(g_t[pos[span]:pos[span] + W_GROUP])
    return jnp.concatenate(parts, axis=0)


def _pack_small(vals):
    rows = []
    for name, _, n_rows in SMALL_LAYOUT:
        flat = vals[name].reshape(-1).astype(F32)
        rows.append(jnp.pad(flat, (0, n_rows * LANES - flat.shape[0])).reshape(n_rows, LANES))
    packed = jnp.concatenate(rows, axis=0)
    return jnp.pad(packed, ((0, SMALL_ROWS - packed.shape[0]), (0, 0)))


def _unpack_small(packed, like):
    out = {}
    for name, row, n_rows in SMALL_LAYOUT:
        n = like[name].size
        out[name] = packed[row:row + n_rows].reshape(-1)[:n].reshape(like[name].shape)
    return out


def _device_step(x, target, small, shards):
    bsz, seq, _ = x.shape
    t = bsz * seq
    xf = x.reshape(t, D_MODEL)
    tf = target.reshape(t, D_MODEL)
    row = lambda v: v.reshape(1, -1)
    g_out = jnp.concatenate([small["g_out_fox"], small["g_out_dil"]]).reshape(1, D_MODEL)
    gains = jnp.concatenate(
        [jnp.tile(small[n].reshape(1, HEAD_DIM), (1, 2)) for n in ("g_q_fox", "g_q_dil", "g_k_fox", "g_k_dil")]
        + [jnp.zeros((4, LANES), F32)], axis=0)
    b_pad = jnp.pad(small["b_forget"].reshape(1, N_HEADS_FOX), ((0, 0), (0, LANES - N_HEADS_FOX)))
    rope = _rope_tables(seq)
    tables_qk = _bias_tables(seq, keys_first=False)
    tables_kq = _bias_tables(seq, keys_first=True)

    h1, g_in = _rmsnorm_fwd(xf, row(small["g_mix"]), group=D_MODEL, name="norm_mix",
                            hosted=_ChipGather([(shards["w_in"], False)]))
    w_main_t, w_fa_t = _w_in_to_kernel(g_in.reshape(IN_COLS, D_MODEL))
    w_in_all_t = jnp.concatenate([w_main_t, w_fa_t], axis=0)
    proj = _matmul_nt(h1, w_main_t, name="in_proj", out_dtype=BF16)
    fa = _matmul_nt(h1, w_fa_t, name="in_proj_forget", out_dtype=F32)
    qk = _qk_prep_fwd(proj, fa, b_pad, gains, rope, seq)
    late = _Exchange([(shards[n], False) for n in ("w_out", "w_gate", "w_up", "w_down")])
    o, lse, g_out_w, g_gate, g_up, g_down = _attn_fwd(qk, proj, tables_qk, seq, hosted=late)
    w_out = g_out_w.reshape(D_MODEL, D_MODEL)
    w_gate_t = g_gate.reshape(D_FF, D_MODEL)
    w_up_t = g_up.reshape(D_FF, D_MODEL)
    w_down = g_down.reshape(D_FF, D_MODEL)
    on, x2, h2 = _out_proj_ffn_norm(o, g_out, w_out, xf, row(small["g_ffn"]))
    a, u, f = _ffn_gate_up(h2, w_gate_t, w_up_t)
    dy, dy16, loss_tile = _ffn_down_loss(f, w_down, x2, tf)

    da, du = _ffn_down_grad(dy16, w_down, a, u)
    gw_down = _matmul_tn(f, dy16, name="gw_down")
    gw_gate_t = _matmul_tn(da, h2, name="gw_gate")
    gw_up_t = _matmul_tn(du, h2, name="gw_up")
    dx2, dx2_16, dg_ffn = _norm_input_grad([(da, w_gate_t, True), (du, w_up_t, True)], x2, row(small["g_ffn"]),
                                           group=D_MODEL, name="d_ffn_gate_up", out_dtypes=(F32, BF16), resid=dy,
                                           k_chunks=2)
    gw_out = _matmul_tn(on, dx2_16, name="gw_out")
    do, dg_out = _norm_input_grad([(dx2_16, w_out, False)], o, g_out, group=W_GROUP, name="d_out_proj",
                                  out_dtypes=(BF16,))

    shard_rows = lambda g: g.reshape(N_DEV, g.shape[0] // N_DEV, g.shape[1])
    ffn_grads = _Exchange([(shard_rows(g), True) for g in (gw_out, gw_gate_t, gw_up_t, gw_down)])
    dq, dk, dv, dqx, dkx, p_out, p_gate, p_up, p_down = _attn_bwd(qk, proj, tables_kq, o, lse, do, seq, hosted=ffn_grads)
    dproj, dgains, db = _qk_prep_bwd(dq, dk, dqx, dkx, dv, proj, fa, b_pad, gains, rope, seq)
    gw_in_t = _matmul_tn(dproj, h1, name="gw_in")
    in_grad = _Exchange([(shard_rows(_w_in_from_kernel(gw_in_t)), True)])
    dx, dg_mix, p_in = _norm_input_grad([(dproj, w_in_all_t, True)], xf, row(small["g_mix"]), group=D_MODEL,
                                        name="d_in_proj", out_dtypes=(F32,), resid=dx2, hosted=in_grad)

    fold = lambda rows: jnp.sum(rows[:, :HEAD_DIM] + rows[:, HEAD_DIM:], axis=0)
    half = N_PAIRS // 2
    gsmall = {
        "g_mix": dg_mix, "g_ffn": dg_ffn, "g_out_fox": dg_out[0, :W_GROUP], "g_out_dil": dg_out[0, W_GROUP:],
        "g_q_fox": fold(dgains[0:half]), "g_q_dil": fold(dgains[half:N_PAIRS]),
        "g_k_fox": fold(dgains[N_PAIRS:N_PAIRS + half]), "g_k_dil": fold(dgains[N_PAIRS + half:]),
        "b_forget": db[0, :N_HEADS_FOX],
    }
    packed = _pack_small(gsmall).at[LOSS_ROW].set(loss_tile[0])
    (p_small,) = _exchange("small_exchange", [(packed, False)])
    parts = {"w_in": p_in, "w_out": p_out, "w_gate": p_gate, "w_up": p_up, "w_down": p_down}
    return dx.reshape(x.shape), parts, p_small


def kernel(x, g_mix, w_in, b_forget, g_q_fox, g_k_fox, g_q_dil, g_k_dil, g_out_fox, g_out_dil, w_out, g_ffn, w_gate, w_up, w_down, loss_target, m_g_mix, m_w_in, m_b_forget, m_g_q_fox, m_g_k_fox, m_g_q_dil, m_g_k_dil, m_g_out_fox, m_g_out_dil, m_w_out, m_g_ffn, m_w_gate, m_w_up, m_w_down, v_g_mix, v_w_in, v_b_forget, v_g_q_fox, v_g_k_fox, v_g_q_dil, v_g_k_dil, v_g_out_fox, v_g_out_dil, v_w_out, v_g_ffn, v_w_gate, v_w_up, v_w_down):
    args = dict(locals())
    small_names = [name for name, _, _ in SMALL_LAYOUT]
    big_names = ["w_in", "w_out", "w_gate", "w_up", "w_down"]
    small = {n: args[n][0] for n in small_names}

    as_rows = lambda n, w: _swap(w) if n in COL_SHARDED else w
    shards = {n: as_rows(n, args[n])[0].astype(BF16) for n in big_names}
    grad_x, parts, p_small = _device_step(x, loss_target, small, shards)

    grads, deltas, new_m, new_v = {}, {}, {}, {}
    for n in big_names:
        res = _adamw(parts[n], as_rows(n, args[n]), as_rows(n, args["m_" + n]), as_rows(n, args["v_" + n]),
                     name="adamw_" + n)
        grads[n], deltas[n], new_m[n], new_v[n] = [as_rows(n, r) for r in res]
    res = _adamw(p_small, _pack_small(small)[None], _pack_small({n: args["m_" + n][0] for n in small_names})[None],
                 _pack_small({n: args["v_" + n][0] for n in small_names})[None], name="adamw_small")
    loss = res[0][0, LOSS_ROW, 0]
    for dst, packed_res in zip((grads, deltas, new_m, new_v), res):
        for n, val in _unpack_small(packed_res[0], small).items():
            dst[n] = val[None]

    order = ["g_mix", "w_in", "b_forget", "g_q_fox", "g_k_fox", "g_q_dil", "g_k_dil", "g_out_fox", "g_out_dil",
             "w_out", "g_ffn", "w_gate", "w_up", "w_down"]
    return (loss, grad_x, *[grads[n] for n in order], *[deltas[n] for n in order],
            *[new_m[n] for n in order], *[new_v[n] for n in order])
```

```python
import functools
import math

import jax
import jax.numpy as jnp
import numpy as np
from jax import lax
from jax.experimental import pallas as pl
from jax.experimental.pallas import tpu as pltpu

F32 = jnp.float32
BF16 = jnp.bfloat16

D_MODEL = 1024
HEAD_DIM = 64
LANES = 128
N_PAIRS = D_MODEL // LANES
N_HEADS = 2 * N_PAIRS
N_HEADS_FOX = 8
W_GROUP = 512
D_FF = 2816
IN_COLS = 3080
DILATION_PAIRS = ((128, 1), (512, 4), (2048, 16))
ROPE_THETA = 500000.0
ROPE_DIM = 16
ROPE_HALF = ROPE_DIM // 2
EPS = 1e-6
NEG = -1e30
LOG2E = 1.4426950408889634
LN2 = 0.6931471805599453
AUG_ONE = 0
AUG_C = 3
N_DEV = 8

ADAM_LR = 0.001
ADAM_B1 = 0.9
ADAM_B2 = 0.999
ADAM_EPS = 1e-08
ADAM_WD = 0.01
ADAM_STEP = 10

ROW_BLOCK = 512
ATT_BLOCK = 512
ATT_GROUP = 4
VMEM_LIMIT = 56 * 1024 * 1024
MATMUL_VMEM_BUDGET = 44 * 1024 * 1024

SMALL_ROWS = 32
SMALL_LAYOUT = (("g_mix", 0, 8), ("g_ffn", 8, 8), ("g_out_fox", 16, 4), ("g_out_dil", 20, 4),
                ("g_q_fox", 24, 1), ("g_k_fox", 25, 1), ("g_q_dil", 26, 1), ("g_k_dil", 27, 1),
                ("b_forget", 28, 1))
LOSS_ROW = 29


def _params(n_grid):
    return pltpu.CompilerParams(dimension_semantics=("arbitrary",) * n_grid, vmem_limit_bytes=VMEM_LIMIT)


def _divisor_block(n, cap):
    best = None
    for b in range(LANES, min(n, cap) + 1, LANES):
        if n % b == 0:
            best = b
    assert best is not None, n
    return best


def _split_dot(a, b_exact, terms):
    acc = None
    rest = a
    for _ in range(terms):
        hi = rest.astype(BF16)
        part = jnp.dot(hi, b_exact, preferred_element_type=F32)
        acc = part if acc is None else acc + part
        rest = rest - hi.astype(F32)
    return acc


def _split_dot_rhs(a_exact, b, terms):
    acc = None
    rest = b
    for _ in range(terms):
        hi = rest.astype(BF16)
        part = jnp.dot(a_exact, hi, preferred_element_type=F32)
        acc = part if acc is None else acc + part
        rest = rest - hi.astype(F32)
    return acc


def _split_dot_nt(a_exact, b, terms):
    acc = None
    rest = b
    for _ in range(terms):
        hi = rest.astype(BF16)
        part = _dot_nt(a_exact, hi)
        acc = part if acc is None else acc + part
        rest = rest - hi.astype(F32)
    return acc


def _dot_nt(a, b):
    return lax.dot_general(a, b, (((1,), (1,)), ((), ())), preferred_element_type=F32)


def _dot_tn(a, b):
    return lax.dot_general(a, b, (((0,), (0,)), ((), ())), preferred_element_type=F32)


class _Exchange:
    def __init__(self, items):
        self.items = items
        self.n = len(items)
        self.arrays = [a for a, _ in items]
        self.out_shape = [jax.ShapeDtypeStruct((N_DEV,) + tuple(a.shape[1:] if sc else a.shape), a.dtype)
                          for a, sc in items]
        self.specs = [pl.BlockSpec(memory_space=pl.ANY)] * self.n
        self.scratch = [pltpu.SemaphoreType.DMA((self.n, N_DEV - 1)), pltpu.SemaphoreType.DMA((self.n, N_DEV - 1)),
                        pltpu.SemaphoreType.DMA((self.n,))]

    def run(self, ins, outs, sems, first, last, compute):
        send_sems, recv_sems, local_sems = sems
        x, y, c = lax.axis_index("x"), lax.axis_index("y"), lax.axis_index("c")
        me = 4 * x + 2 * y + c
        local, remote = [], []
        for k, (_, scatter) in enumerate(self.items):
            own = ins[k].at[me] if scatter else ins[k]
            local.append(pltpu.make_async_copy(own, outs[k].at[me], local_sems.at[k]))
        for r in range(1, N_DEV):
            px = 1 - x if r & 4 else x
            py = 1 - y if r & 2 else y
            pc = 1 - c if r & 1 else c
            peer = 4 * px + 2 * py + pc
            for k, (_, scatter) in enumerate(self.items):
                src = ins[k].at[peer] if scatter else ins[k]
                remote.append(pltpu.make_async_remote_copy(
                    src_ref=src, dst_ref=outs[k].at[me],
                    send_sem=send_sems.at[k, r - 1], recv_sem=recv_sems.at[k, r - 1],
                    device_id=(px, py, pc), device_id_type=pl.DeviceIdType.MESH))

        def start():
            for cp in local + remote:
                cp.start()

        def finish():
            for cp in remote:
                cp.wait_recv()
            for cp in remote:
                cp.wait_send()
            for cp in local:
                cp.wait()

        _run_phases(first, last, start, compute, finish)


def _run_phases(first, last, start, compute, finish):
    if first is None:
        start()
        compute()
        finish()
    else:
        pl.when(first)(start)
        compute()
        pl.when(last)(finish)


class _ChipGather(_Exchange):
    def run(self, ins, outs, sems, first, last, compute):
        send_sems, recv_sems, local_sems = sems
        x, y, c = lax.axis_index("x"), lax.axis_index("y"), lax.axis_index("c")
        sibling = (x, y, 1 - c)
        chips = [(1 - x, y), (x, 1 - y), (1 - x, 1 - y)]
        slot = lambda px, py, pc: 4 * px + 2 * py + pc

        def copy(k, n, src, dst_slot, to):
            return pltpu.make_async_remote_copy(
                src_ref=src, dst_ref=outs[k].at[dst_slot], send_sem=send_sems.at[k, n], recv_sem=recv_sems.at[k, n],
                device_id=to, device_id_type=pl.DeviceIdType.MESH)

        local, own, passed, arrivals = [], [], [], []
        for k in range(self.n):
            me = slot(x, y, c)
            local.append(pltpu.make_async_copy(ins[k], outs[k].at[me], local_sems.at[k]))
            own.append(copy(k, 0, ins[k], me, sibling))
            arrivals.append(copy(k, 0, ins[k], slot(*sibling), sibling))
            for j, chip in enumerate(chips):
                theirs = slot(*chip, c)
                own.append(copy(k, 1 + j, ins[k], me, (*chip, c)))
                passed.append((copy(k, 1 + j, ins[k], theirs, sibling),
                               copy(k, 4 + j, outs[k].at[theirs], theirs, sibling)))
                arrivals.append(copy(k, 4 + j, ins[k], slot(*chip, 1 - c), sibling))

        def start():
            for cp in local + own:
                cp.start()

        def finish():
            for landed, onward in passed:
                landed.wait_recv()
                onward.start()
            for cp in arrivals:
                cp.wait_recv()
            for cp in own + [onward for _, onward in passed]:
                cp.wait_send()
            for cp in local:
                cp.wait()

        _run_phases(first, last, start, compute, finish)


def _grid_ends(grid):
    ids = [pl.program_id(d) for d in range(len(grid))]
    first = functools.reduce(jnp.logical_and, [i == 0 for i in ids])
    last = functools.reduce(jnp.logical_and, [i == g - 1 for i, g in zip(ids, grid)])
    return first, last


def _host(core, n_in, n_out, n_scratch, hosted, grid):
    if hosted is None:
        return core
    nh = hosted.n

    def body(*refs):
        ins, rest = refs[:n_in], refs[n_in:]
        h_ins, rest = rest[:nh], rest[nh:]
        outs, rest = rest[:n_out], rest[n_out:]
        h_outs, rest = rest[:nh], rest[nh:]
        scratch, sems = rest[:n_scratch], rest[n_scratch:]
        first, last = _grid_ends(grid)
        hosted.run(h_ins, h_outs, sems, first, last, lambda: core(*ins, *outs, *scratch))

    return body


def _hosted_parts(hosted):
    if hosted is None:
        return [], [], [], []
    return list(hosted.specs), list(hosted.out_shape), list(hosted.arrays), list(hosted.scratch)


def _exchange(name, items):
    ex = _Exchange(items)
    n = ex.n

    def body(*refs):
        ex.run(refs[:n], refs[n:2 * n], refs[2 * n:], None, None, lambda: None)

    return pl.pallas_call(
        body, name=name, out_shape=tuple(ex.out_shape), in_specs=ex.specs, out_specs=tuple(ex.specs),
        scratch_shapes=ex.scratch,
    )(*ex.arrays)


def _matmul_blocks(t, k, n, a_bytes, o_bytes):
    for bt, cap in ((1024, 1408), (1024, 512), (512, 512)):
        if t % bt:
            continue
        bn = _divisor_block(n, cap)
        if 2 * (bt * k * a_bytes + bn * k * 2 + bt * bn * o_bytes) <= MATMUL_VMEM_BUDGET:
            return bt, bn
    return ROW_BLOCK, _divisor_block(n, 256)


def _matmul_nt(a, w, *, name, out_dtype):
    t, k = a.shape
    n = w.shape[0]
    assert w.shape[1] == k
    bt, bn = _matmul_blocks(t, k, n, a.dtype.itemsize, jnp.dtype(out_dtype).itemsize)

    def body(a_ref, w_ref, o_ref):
        o_ref[...] = _dot_nt(a_ref[...], w_ref[...]).astype(o_ref.dtype)

    return pl.pallas_call(
        body, name=name, grid=(t // bt, n // bn),
        in_specs=[pl.BlockSpec((bt, k), lambda i, j: (i, 0)), pl.BlockSpec((bn, k), lambda i, j: (j, 0))],
        out_specs=pl.BlockSpec((bt, bn), lambda i, j: (i, j)),
        out_shape=jax.ShapeDtypeStruct((t, n), out_dtype), compiler_params=_params(2),
    )(a, w)


def _matmul_tn(a, b, *, name):
    t, m = a.shape
    n = b.shape[1]
    bt = 2048 if t % 2048 == 0 else ROW_BLOCK
    bm = _divisor_block(m, 1408)
    bn = _divisor_block(n, 1408)
    steps = t // bt

    def body(a_ref, b_ref, o_ref, acc):
        step = pl.program_id(2)

        @pl.when(step == 0)
        def _():
            acc[...] = jnp.zeros_like(acc)

        acc[...] += _dot_tn(a_ref[...], b_ref[...])

        @pl.when(step == steps - 1)
        def _():
            o_ref[...] = acc[...].astype(o_ref.dtype)

    return pl.pallas_call(
        body, name=name, grid=(m // bm, n // bn, steps),
        in_specs=[pl.BlockSpec((bt, bm), lambda i, j, s: (s, i)), pl.BlockSpec((bt, bn), lambda i, j, s: (s, j))],
        out_specs=pl.BlockSpec((bm, bn), lambda i, j, s: (i, j)),
        out_shape=jax.ShapeDtypeStruct((m, n), BF16), scratch_shapes=[pltpu.VMEM((bm, bn), F32)],
        compiler_params=_params(3),
    )(a, b)


def _rmsnorm_fwd(x, g, *, group, name, hosted=None):
    t, w = x.shape
    bt = ROW_BLOCK

    def body(x_ref, g_ref, o_ref):
        for s in range(0, w, group):
            xs = x_ref[:, s:s + group].astype(F32)
            r = lax.rsqrt(jnp.mean(xs * xs, axis=-1, keepdims=True) + EPS)
            o_ref[:, s:s + group] = (xs * r * g_ref[:, s:s + group]).astype(o_ref.dtype)

    grid = (t // bt,)
    h_specs, h_shapes, h_args, h_scratch = _hosted_parts(hosted)
    res = pl.pallas_call(
        _host(body, 2, 1, 0, hosted, grid), name=name, grid=grid,
        in_specs=[pl.BlockSpec((bt, w), lambda i: (i, 0)), pl.BlockSpec((1, w), lambda i: (0, 0))] + h_specs,
        out_specs=tuple([pl.BlockSpec((bt, w), lambda i: (i, 0))] + h_specs),
        out_shape=tuple([jax.ShapeDtypeStruct((t, w), BF16)] + h_shapes),
        scratch_shapes=h_scratch, compiler_params=_params(1),
    )(x, g, *h_args)
    return res if hosted else res[0]


def _norm_input_grad(terms, x, g, *, group, name, out_dtypes, resid=None, hosted=None, k_chunks=1):
    t, w = x.shape
    n_terms = len(terms)
    kc = [a.shape[1] // k_chunks for a, _, _ in terms]
    per_row = sum(c * a.dtype.itemsize for c, (a, _, _) in zip(kc, terms)) + w * (x.dtype.itemsize + 4 * (resid is not None))
    per_row += w * sum(jnp.dtype(dt).itemsize for dt in out_dtypes)
    fixed = 2 * sum(w * c * 2 for c in kc)
    bt = next(b for b in (1024, 512, 256, 128)
              if t % b == 0 and fixed + 2 * b * per_row + 5 * b * w * 4 <= MATMUL_VMEM_BUDGET)
    n_in = 2 * n_terms + 2 + (resid is not None)
    grid = (t // bt, k_chunks)

    def body(*refs):
        x_ref, g_ref = refs[2 * n_terms], refs[2 * n_terms + 1]
        dx_refs, dg_ref, dh_ref = refs[n_in:-2], refs[-2], refs[-1]
        chunk = pl.program_id(1)

        @pl.when((pl.program_id(0) == 0) & (chunk == 0))
        def _():
            dg_ref[...] = jnp.zeros_like(dg_ref)

        part = None
        for k in range(n_terms):
            if terms[k][2]:
                term = jnp.dot(refs[2 * k][...], refs[2 * k + 1][...], preferred_element_type=F32)
            else:
                term = _dot_nt(refs[2 * k][...], refs[2 * k + 1][...])
            part = term if part is None else part + term

        @pl.when(chunk == 0)
        def _():
            dh_ref[...] = part

        @pl.when(chunk > 0)
        def _():
            dh_ref[...] += part

        @pl.when(chunk == k_chunks - 1)
        def _():
            for s in range(0, w, group):
                xs = x_ref[:, s:s + group].astype(F32)
                dhs = dh_ref[:, s:s + group]
                r = lax.rsqrt(jnp.mean(xs * xs, axis=-1, keepdims=True) + EPS)
                xh = xs * r
                dg_ref[:, s:s + group] += jnp.sum(dhs * xh, axis=0, keepdims=True)
                dxh = dhs * g_ref[:, s:s + group]
                dx = r * (dxh - xh * jnp.mean(dxh * xh, axis=-1, keepdims=True))
                if resid is not None:
                    dx = refs[n_in - 1][:, s:s + group] + dx
                for dx_ref in dx_refs:
                    dx_ref[:, s:s + group] = dx.astype(dx_ref.dtype)

    row = pl.BlockSpec((bt, w), lambda i, k: (i, 0))
    vec = pl.BlockSpec((1, w), lambda i, k: (0, 0))
    in_specs, args = [], []
    for c, (a, wt, w_is_kn) in zip(kc, terms):
        assert wt.shape == ((a.shape[1], w) if w_is_kn else (w, a.shape[1]))
        w_spec = pl.BlockSpec((c, w), lambda i, k: (k, 0)) if w_is_kn else pl.BlockSpec((w, c), lambda i, k: (0, k))
        in_specs += [pl.BlockSpec((bt, c), lambda i, k: (i, k)), w_spec]
        args += [a, wt]
    in_specs += [row, vec] + ([row] if resid is not None else [])
    args += [x, g] + ([resid] if resid is not None else [])
    h_specs, h_shapes, h_args, h_scratch = _hosted_parts(hosted)
    return pl.pallas_call(
        _host(body, n_in, len(out_dtypes) + 1, 1, hosted, grid), name=name, grid=grid, in_specs=in_specs + h_specs,
        out_specs=tuple([row] * len(out_dtypes) + [vec] + h_specs),
        out_shape=tuple([jax.ShapeDtypeStruct((t, w), dt) for dt in out_dtypes] + [jax.ShapeDtypeStruct((1, w), F32)]
                        + h_shapes),
        scratch_shapes=[pltpu.VMEM((bt, w), F32)] + h_scratch, compiler_params=_params(2),
    )(*args, *h_args)


def _tile_plan(tile):
    is_q = tile < N_PAIRS
    is_dil = (tile % N_PAIRS) >= N_PAIRS // 2
    return is_q, is_dil, (0 if is_q else 2) + (1 if is_dil else 0)


def _segment_ones():
    lane = np.arange(LANES)
    return jnp.asarray((lane[:, None] // HEAD_DIM) == (lane[None, :] // HEAD_DIM), BF16)


def _rope_tables(seq):
    inv_freq = jnp.power(jnp.float32(ROPE_THETA), -jnp.arange(ROPE_HALF, dtype=F32) * 2.0 / ROPE_DIM)
    ang = jnp.arange(seq).astype(F32)[:, None] * inv_freq[None, :]
    cos, sin = jnp.cos(ang), jnp.sin(ang)
    ones = jnp.ones((seq, HEAD_DIM - ROPE_DIM), F32)
    zeros = jnp.zeros((seq, HEAD_DIM - ROPE_DIM), F32)
    zh = jnp.zeros((seq, ROPE_HALF), F32)
    cos_t = jnp.concatenate([cos, cos, ones], axis=1)
    sin_a = jnp.concatenate([-sin, zh, zeros], axis=1)
    sin_b = jnp.concatenate([zh, sin, zeros], axis=1)
    return tuple(jnp.tile(tab, (1, 2)) for tab in (cos_t, sin_a, sin_b))


def _log_sigmoid(z):
    return jnp.minimum(z, 0.0) - jnp.log1p(jnp.exp(-jnp.abs(z)))


def _qk_prep_fwd(proj, fa, b_pad, gains, rope, seq):
    t = proj.shape[0]
    bt = ROW_BLOCK
    nsb = seq // bt
    seg = _segment_ones()
    rr = np.arange(bt)
    tri = jnp.asarray(rr[:, None] >= rr[None, :], BF16)

    def body(p_ref, fa_ref, b_ref, g_ref, cos_ref, sa_ref, sb_ref, seg_ref, tri_ref, qk_ref, carry):
        @pl.when(pl.program_id(0) % nsb == 0)
        def _():
            carry[...] = jnp.zeros_like(carry)

        lane = lax.broadcasted_iota(jnp.int32, (bt, LANES), 1)
        logf = jnp.where(lane < N_HEADS_FOX, _log_sigmoid(fa_ref[...] + b_ref[...]), 0.0)
        cblk = _split_dot_rhs(tri_ref[...], logf, 3) + carry[0:1, :]
        carry[0:1, :] = cblk[bt - 1:bt, :]
        c_terms = []
        rest = cblk * LOG2E
        for _ in range(3):
            term = rest.astype(BF16).astype(F32)
            c_terms.append(term)
            rest = rest - term

        for tile in range(2 * N_PAIRS):
            is_q, is_dil, grow = _tile_plan(tile)
            pair = tile % N_PAIRS
            xs = p_ref[:, tile * LANES:(tile + 1) * LANES].astype(F32)
            r = lax.rsqrt(_split_dot(xs * xs, seg_ref[...], 2) * (1.0 / HEAD_DIM) + EPS)
            yv = xs * r * g_ref[grow:grow + 1, :]
            if is_dil:
                yv = (yv * cos_ref[...] + pltpu.roll(yv, LANES - ROPE_HALF, 1) * sa_ref[...]
                      + pltpu.roll(yv, ROPE_HALF, 1) * sb_ref[...])
            if is_q:
                yv = yv * (HEAD_DIM ** -0.5 * LOG2E)
            for e in range(2):
                head = 2 * pair + e
                other = HEAD_DIM * (1 - e)
                aug = jnp.zeros((bt, LANES), F32)
                if not is_dil:
                    ones_at = other + (AUG_ONE if is_q else AUG_C)
                    c_at = other + (AUG_C if is_q else AUG_ONE)
                    aug = jnp.where((lane >= ones_at) & (lane < ones_at + 3), 1.0, aug)
                    for n, term in enumerate(c_terms):
                        col = term[:, head:head + 1]
                        aug = jnp.where(lane == c_at + n, col if is_q else -col, aug)
                mine = (lane < HEAD_DIM) if e == 0 else (lane >= HEAD_DIM)
                dst = ((0 if is_q else N_HEADS) + head) * LANES
                qk_ref[:, dst:dst + LANES] = jnp.where(mine, yv, aug).astype(BF16)

    row128 = pl.BlockSpec((bt, LANES), lambda i: (i, 0))
    rope_spec = pl.BlockSpec((bt, LANES), lambda i: (i % nsb, 0))
    const = lambda shape: pl.BlockSpec(shape, lambda i: (0, 0))
    return pl.pallas_call(
        body, name="qk_prep_fwd", grid=(t // bt,),
        in_specs=[pl.BlockSpec((bt, 2 * D_MODEL), lambda i: (i, 0)), row128, const((1, LANES)), const((8, LANES)),
                  rope_spec, rope_spec, rope_spec, const((LANES, LANES)), const((bt, bt))],
        out_specs=pl.BlockSpec((bt, 2 * N_HEADS * LANES), lambda i: (i, 0)),
        out_shape=jax.ShapeDtypeStruct((t, 2 * N_HEADS * LANES), BF16),
        scratch_shapes=[pltpu.VMEM((8, LANES), F32)], compiler_params=_params(1),
    )(proj, fa, b_pad, gains, *rope, seg, tri)


def _qk_prep_bwd(dq, dk, dqx, dkx, dv, proj, fa, b_pad, gains, rope, seq):
    t = proj.shape[0]
    bt = ROW_BLOCK
    nsb = seq // bt
    nblk = t // bt
    seg = _segment_ones()
    rr = np.arange(bt)
    triu = jnp.asarray(rr[:, None] <= rr[None, :], BF16)

    def body(dq_ref, dk_ref, dqx_ref, dkx_ref, dv_ref, p_ref, fa_ref, b_ref, g_ref, cos_ref, sa_ref, sb_ref, seg_ref,
             triu_ref, dp_ref, dg_ref, db_ref, carry):
        step = pl.program_id(0)

        @pl.when(step == 0)
        def _():
            dg_ref[...] = jnp.zeros_like(dg_ref)
            db_ref[...] = jnp.zeros_like(db_ref)

        @pl.when(step % nsb == 0)
        def _():
            carry[...] = jnp.zeros_like(carry)

        for tile in range(2 * N_PAIRS):
            is_q, is_dil, grow = _tile_plan(tile)
            cols = slice(tile * LANES, (tile + 1) * LANES)
            src = dq_ref if is_q else dk_ref
            half = slice((tile % N_PAIRS) * LANES, (tile % N_PAIRS + 1) * LANES)
            dy = src[:, half]
            dy = dy * (HEAD_DIM ** -0.5 if is_q else LN2)
            if is_dil:
                dy = (dy * cos_ref[...] + pltpu.roll(dy * sa_ref[...], ROPE_HALF, 1)
                      + pltpu.roll(dy * sb_ref[...], LANES - ROPE_HALF, 1))
            xs = p_ref[:, cols].astype(F32)
            r = lax.rsqrt(_split_dot(xs * xs, seg_ref[...], 2) * (1.0 / HEAD_DIM) + EPS)
            xh = xs * r
            dg_ref[tile:tile + 1, :] += jnp.sum(dy * xh, axis=0, keepdims=True)
            dxh = dy * g_ref[grow:grow + 1, :]
            seg_mean = _split_dot(dxh * xh, seg_ref[...], 2) * (1.0 / HEAD_DIM)
            dp_ref[:, cols] = (r * (dxh - xh * seg_mean)).astype(BF16)

        lane = lax.broadcasted_iota(jnp.int32, (bt, LANES), 1)
        dc = jnp.zeros((bt, LANES), F32)
        for h in range(N_HEADS_FOX):
            other = (h // 2) * LANES + HEAD_DIM * (1 - h % 2)
            row_sum = dqx_ref[:, other + AUG_C:other + AUG_C + 1]
            col_sum = dkx_ref[:, other + AUG_ONE:other + AUG_ONE + 1]
            dc = jnp.where(lane == h, row_sum - col_sum, dc)
        dlogf = _split_dot_rhs(triu_ref[...], dc, 3) + carry[0:1, :]
        carry[0:1, :] = dlogf[0:1, :]
        z = fa_ref[...] + b_ref[...]
        dfa = dlogf * (1.0 / (1.0 + jnp.exp(z)))
        db_ref[0:1, :] += jnp.sum(dfa, axis=0, keepdims=True)
        dp_ref[:, 2 * D_MODEL:MAIN_COLS] = dv_ref[...]
        dp_ref[:, MAIN_COLS:PROJ_COLS] = dfa.astype(BF16)

    rev = lambda i: nblk - 1 - i
    row = lambda w: pl.BlockSpec((bt, w), lambda i: (rev(i), 0))
    rope_spec = pl.BlockSpec((bt, LANES), lambda i: (rev(i) % nsb, 0))
    const = lambda shape: pl.BlockSpec(shape, lambda i: (0, 0))
    return pl.pallas_call(
        body, name="qk_prep_bwd", grid=(nblk,),
        in_specs=[row(D_MODEL), row(D_MODEL), row(W_GROUP), row(W_GROUP), row(D_MODEL), row(2 * D_MODEL), row(LANES),
                  const((1, LANES)), const((8, LANES)), rope_spec, rope_spec, rope_spec, const((LANES, LANES)),
                  const((bt, bt))],
        out_specs=(row(PROJ_COLS), const((2 * N_PAIRS, LANES)), const((8, LANES))),
        out_shape=(jax.ShapeDtypeStruct((t, PROJ_COLS), BF16),
                   jax.ShapeDtypeStruct((2 * N_PAIRS, LANES), F32), jax.ShapeDtypeStruct((8, LANES), F32)),
        scratch_shapes=[pltpu.VMEM((8, LANES), F32)], compiler_params=_params(1),
    )(dq, dk, dqx, dkx, dv, proj, fa, b_pad, gains, *rope, seg, triu)


def _bias_tables(seq, keys_first):
    nb = seq // ATT_BLOCK
    idx = np.arange(ATT_BLOCK)
    q_idx, k_idx = (idx[None, None, :], idx[None, :, None]) if keys_first else (idx[None, :, None], idx[None, None, :])
    dist = np.arange(nb)[:, None, None] * ATT_BLOCK + q_idx - k_idx
    causal = dist >= 0
    count = np.zeros(dist.shape, np.int32)
    for window, dilation in DILATION_PAIRS:
        count = count + (causal & (dist % dilation == 0) & (dist <= window))
    fox = np.where(causal, 0.0, NEG)
    dil = np.where(count == 3, math.log2(3.0), np.where(count == 2, 1.0, np.where(count == 1, 0.0, NEG)))
    return jnp.asarray(np.stack([fox, dil], axis=0), F32)


def _attn_specs(seq):
    nb = seq // ATT_BLOCK
    col = lambda off: pl.BlockSpec((seq, LANES), lambda b, j: (b, off + j))
    heads = lambda off: pl.BlockSpec((seq, 2 * LANES), lambda b, j: (b, off + j))
    table_spec = pl.BlockSpec((1, nb, ATT_BLOCK, ATT_BLOCK), lambda b, j: (j // (N_PAIRS // 2), 0, 0, 0))
    return col, heads, table_spec


def _head_lanes(e, shape, axis):
    pos = lax.broadcasted_iota(jnp.int32, shape, axis)
    return pos < HEAD_DIM if e == 0 else pos >= HEAD_DIM


def _attn_fwd(qk, proj, tables, seq, hosted=None):
    t = qk.shape[0]
    nb = seq // ATT_BLOCK
    blk = ATT_BLOCK

    def body(q_ref, k_ref, v_ref, tab_ref, o_ref, lse_ref):
        mine = [_head_lanes(e, (seq, LANES), 1) for e in range(2)]
        lane = lax.broadcasted_iota(jnp.int32, (seq, LANES), 1)
        v_aug = [jnp.where(mine[e], v_ref[...], (lane == HEAD_DIM * (1 - e)).astype(BF16)) for e in range(2)]
        for i in range(nb):
            rows = slice(i * blk, (i + 1) * blk)
            n_keys = (i + 1) * blk
            out, lse = [], []
            for e in range(2):
                heads_e = slice(e * LANES, (e + 1) * LANES)
                s = _dot_nt(q_ref[rows, heads_e], k_ref[0:n_keys, heads_e])
                s = jnp.concatenate([s[:, jj * blk:(jj + 1) * blk] + tab_ref[0, i - jj] for jj in range(i + 1)], axis=1)
                m = jnp.max(s, axis=1, keepdims=True)
                acc = jnp.dot(jnp.exp2(s - m).astype(BF16), v_aug[e][0:n_keys], preferred_element_type=F32)
                ones_at = HEAD_DIM * (1 - e)
                l = acc[:, ones_at:ones_at + 1]
                out.append(acc / l)
                lse.append(m + jnp.log2(l))
            o_ref[rows, :] = jnp.where(mine[0][rows], out[0], out[1]).astype(o_ref.dtype)
            lse_ref[rows, :] = jnp.where(mine[0][rows], lse[0], lse[1])

    col, heads, table_spec = _attn_specs(seq)
    grid = (t // seq, N_PAIRS)
    h_specs, h_shapes, h_args, h_scratch = _hosted_parts(hosted)
    return pl.pallas_call(
        _host(body, 4, 2, 0, hosted, grid), name="attn_fwd", grid=grid,
        in_specs=[heads(0), heads(N_PAIRS), col(2 * N_PAIRS), table_spec] + h_specs,
        out_specs=tuple([col(0), col(0)] + h_specs),
        out_shape=tuple([jax.ShapeDtypeStruct((t, D_MODEL), BF16), jax.ShapeDtypeStruct((t, D_MODEL), F32)] + h_shapes),
        scratch_shapes=h_scratch, compiler_params=_params(2),
    )(qk, qk, proj, tables, *h_args)


def _attn_bwd(qk, proj, tables, o, lse, do, seq, hosted=None):
    t = qk.shape[0]
    nb = seq // ATT_BLOCK
    blk = ATT_BLOCK
    group = math.gcd(nb, ATT_GROUP)

    def body(q_ref, k_ref, v_ref, tab_ref, o_ref, lse_ref, do_ref,
             dq_ref, dk_ref, dv_ref, dqx_ref, dkx_ref, dk_acc, dv_acc):
        mine = [_head_lanes(e, (blk, LANES), 1) for e in range(2)]
        top = _head_lanes(0, (LANES, blk), 0)
        head_rows = lax.broadcasted_iota(jnp.int32, (8, LANES), 0)
        head_of_lane = lax.broadcasted_iota(jnp.int32, (8, LANES), 1) // HEAD_DIM
        head_sel = (head_rows == head_of_lane).astype(BF16)
        dk_acc[...] = jnp.zeros_like(dk_acc)
        dv_acc[...] = jnp.zeros_like(dv_acc)

        def block_rows(i):
            return pl.ds(pl.multiple_of(i * blk, blk), blk)

        def q_group(g, _):
            base = g * group
            qs, doe, delta, lse_e = [], [], [], []
            for b in range(group):
                rows = block_rows(base + b)
                qs.append([q_ref[rows, e * LANES:(e + 1) * LANES] for e in range(2)])
                do_blk = do_ref[rows, :]
                doe.append([jnp.where(mine[e], do_blk, jnp.zeros_like(do_blk)) for e in range(2)])
                delta_t = _split_dot_nt(head_sel, do_blk.astype(F32) * o_ref[rows, :].astype(F32), 3)
                lse_t = _split_dot_nt(head_sel, lse_ref[rows, :], 3) * (1.0 / HEAD_DIM)
                delta.append([delta_t[e:e + 1, :] for e in range(2)])
                lse_e.append([lse_t[e:e + 1, :] for e in range(2)])

            def key_block(dq_t, jj, members):
                krows = block_rows(jj)
                v = v_ref[krows, :]
                dq_t = [list(d) for d in dq_t]
                lo, hi = slice(0, blk // 2), slice(blk // 2, blk)
                dv_part = [None, None]
                add = lambda acc, part: part if acc is None else acc + part

                def chain(k_sub, v_sub, keys, queries, b, e, dist):
                    q_sub, do_sub = qs[b][e][queries], doe[b][e][queries]
                    p_t = jnp.exp2(_dot_nt(k_sub, q_sub) + tab_ref[0, dist, keys, queries] - lse_e[b][e][:, queries])
                    ds_t = (p_t * (_dot_nt(v_sub, do_sub) - delta[b][e][:, queries])).astype(BF16)
                    return (jnp.dot(p_t.astype(BF16), do_sub, preferred_element_type=F32),
                            jnp.dot(ds_t, q_sub, preferred_element_type=F32), _dot_tn(k_sub, ds_t))

                for e in range(2):
                    k_e = k_ref[krows, e * LANES:(e + 1) * LANES]
                    dk_part = [None, None]
                    for b, dist in members:
                        if isinstance(dist, int) and dist == 0:
                            dv_a, dk_a, dq_a = chain(k_e[lo], v[lo], lo, slice(0, blk), b, e, dist)
                            dv_b, dk_b, dq_b = chain(k_e[hi], v[hi], hi, hi, b, e, dist)
                            halves = ((dv_a, dk_a), (dv_b, dk_b))
                            dq = jnp.concatenate([dq_a[:, lo], dq_a[:, hi] + dq_b], axis=1)
                        else:
                            dv_f, dk_f, dq = chain(k_e, v, slice(0, blk), slice(0, blk), b, e, dist)
                            halves = ((dv_f[lo], dk_f[lo]), (dv_f[hi], dk_f[hi]))
                        for n, (dv_h, dk_h) in enumerate(halves):
                            dv_part[n] = add(dv_part[n], dv_h)
                            dk_part[n] = add(dk_part[n], dk_h)
                        dq_t[b][e] = dq_t[b][e] + dq
                    dk_acc[e, krows, :] += jnp.concatenate(dk_part, axis=0)
                dv_acc[krows, :] += jnp.concatenate(dv_part, axis=0)
                return tuple(tuple(d) for d in dq_t)

            zacc = jnp.zeros((LANES, blk), F32)
            dq_t = tuple((zacc, zacc) for _ in range(group))
            dq_t = lax.fori_loop(
                0, base, lambda jj, st: key_block(st, jj, [(b, base + b - jj) for b in range(group)]), dq_t)
            for a in range(group):
                dq_t = key_block(dq_t, base + a, [(b, b - a) for b in range(a, group)])
            for b in range(group):
                rows = block_rows(base + b)
                dq_ref[rows, :] = jnp.where(top, dq_t[b][0], dq_t[b][1]).T
                dqx_ref[rows, :] = jnp.where(top, dq_t[b][1], dq_t[b][0]).T
            return 0

        lax.fori_loop(0, nb // group, q_group, 0)
        lo = _head_lanes(0, (seq, LANES), 1)
        dk_ref[...] = jnp.where(lo, dk_acc[0], dk_acc[1])
        dkx_ref[...] = jnp.where(lo, dk_acc[1], dk_acc[0])
        dv_ref[...] = dv_acc[...].astype(dv_ref.dtype)

    col, heads, table_spec = _attn_specs(seq)
    grid = (t // seq, N_PAIRS)
    h_specs, h_shapes, h_args, h_scratch = _hosted_parts(hosted)
    f32_out = jax.ShapeDtypeStruct((t, D_MODEL), F32)
    return pl.pallas_call(
        _host(body, 7, 5, 2, hosted, grid), name="attn_bwd", grid=grid,
        in_specs=[heads(0), heads(N_PAIRS), col(2 * N_PAIRS), table_spec, col(0), col(0), col(0)] + h_specs,
        out_specs=tuple([col(0)] * 5 + h_specs),
        out_shape=tuple([f32_out, f32_out, jax.ShapeDtypeStruct((t, D_MODEL), BF16), f32_out, f32_out] + h_shapes),
        scratch_shapes=[pltpu.VMEM((2, seq, LANES), F32), pltpu.VMEM((seq, LANES), F32)] + h_scratch,
        compiler_params=_params(2),
    )(qk, qk, proj, tables, o, lse, do, *h_args)


def _row_block(t):
    return 1024 if t % 1024 == 0 else ROW_BLOCK


def _out_proj_ffn_norm(o, g_out, w_out, x, g_ffn):
    t = o.shape[0]
    bt = _row_block(t)

    def body(o_ref, go_ref, w_ref, x_ref, gf_ref, on_ref, x2_ref, h2_ref):
        for s in range(0, D_MODEL, W_GROUP):
            os_ = o_ref[:, s:s + W_GROUP].astype(F32)
            r = lax.rsqrt(jnp.mean(os_ * os_, axis=-1, keepdims=True) + EPS)
            on_ref[:, s:s + W_GROUP] = (os_ * r * go_ref[:, s:s + W_GROUP]).astype(BF16)
        x2 = x_ref[...] + jnp.dot(on_ref[...], w_ref[...], preferred_element_type=F32)
        x2_ref[...] = x2
        r2 = lax.rsqrt(jnp.mean(x2 * x2, axis=-1, keepdims=True) + EPS)
        h2_ref[...] = (x2 * r2 * gf_ref[...]).astype(BF16)

    row = pl.BlockSpec((bt, D_MODEL), lambda i: (i, 0))
    vec = pl.BlockSpec((1, D_MODEL), lambda i: (0, 0))
    return pl.pallas_call(
        body, name="out_proj", grid=(t // bt,),
        in_specs=[row, vec, pl.BlockSpec((D_MODEL, D_MODEL), lambda i: (0, 0)), row, vec],
        out_specs=(row, row, row),
        out_shape=(jax.ShapeDtypeStruct((t, D_MODEL), BF16), jax.ShapeDtypeStruct((t, D_MODEL), F32),
                   jax.ShapeDtypeStruct((t, D_MODEL), BF16)),
        compiler_params=_params(1),
    )(o, g_out, w_out, x, g_ffn)


def _ffn_gate_up(h2, w_gate_t, w_up_t):
    t = h2.shape[0]
    bt = _row_block(t)
    bn = _divisor_block(D_FF, 1408)

    def body(h_ref, wg_ref, wu_ref, a_ref, u_ref, f_ref):
        a = _dot_nt(h_ref[...], wg_ref[...])
        u = _dot_nt(h_ref[...], wu_ref[...])
        a_ref[...] = a.astype(BF16)
        u_ref[...] = u.astype(BF16)
        f_ref[...] = (a * jax.nn.sigmoid(a) * u).astype(BF16)

    blk = pl.BlockSpec((bt, bn), lambda i, j: (i, j))
    w_blk = pl.BlockSpec((bn, D_MODEL), lambda i, j: (j, 0))
    shape = jax.ShapeDtypeStruct((t, D_FF), BF16)
    return pl.pallas_call(
        body, name="ffn_gate_up", grid=(t // bt, D_FF // bn),
        in_specs=[pl.BlockSpec((bt, D_MODEL), lambda i, j: (i, 0)), w_blk, w_blk],
        out_specs=(blk, blk, blk), out_shape=(shape, shape, shape), compiler_params=_params(2),
    )(h2, w_gate_t, w_up_t)


def _ffn_down_grad(dy16, w_down, a, u):
    t = a.shape[0]
    bt = _row_block(t)
    bn = _divisor_block(D_FF, 1408)

    def body(dy_ref, w_ref, a_ref, u_ref, da_ref, du_ref):
        df = _dot_nt(dy_ref[...], w_ref[...])
        av = a_ref[...].astype(F32)
        sg = jax.nn.sigmoid(av)
        da_ref[...] = (df * u_ref[...].astype(F32) * sg * (1.0 + av * (1.0 - sg))).astype(BF16)
        du_ref[...] = (df * av * sg).astype(BF16)

    blk = pl.BlockSpec((bt, bn), lambda i, j: (i, j))
    shape = jax.ShapeDtypeStruct((t, D_FF), BF16)
    return pl.pallas_call(
        body, name="d_ffn_down", grid=(t // bt, D_FF // bn),
        in_specs=[pl.BlockSpec((bt, D_MODEL), lambda i, j: (i, 0)), pl.BlockSpec((bn, D_MODEL), lambda i, j: (j, 0)),
                  blk, blk],
        out_specs=(blk, blk), out_shape=(shape, shape), compiler_params=_params(2),
    )(dy16, w_down, a, u)


def _ffn_down_loss(f, w_down, x2, target):
    t, w = x2.shape
    bt = _row_block(t)

    def body(f_ref, w_ref, x_ref, t_ref, dy_ref, dy16_ref, loss_ref):
        @pl.when(pl.program_id(0) == 0)
        def _():
            loss_ref[...] = jnp.zeros_like(loss_ref)

        err = (x_ref[...] + jnp.dot(f_ref[...], w_ref[...], preferred_element_type=F32)) - t_ref[...]
        dy = err * (1.0 / w)
        dy_ref[...] = dy
        dy16_ref[...] = dy.astype(BF16)
        loss_ref[...] += 0.5 * jnp.sum(jnp.mean(err * err, axis=-1, keepdims=True), axis=0, keepdims=True)

    row = pl.BlockSpec((bt, w), lambda i: (i, 0))
    return pl.pallas_call(
        body, name="ffn_down_loss", grid=(t // bt,),
        in_specs=[pl.BlockSpec((bt, D_FF), lambda i: (i, 0)), pl.BlockSpec((D_FF, w), lambda i: (0, 0)), row, row],
        out_specs=(row, row, pl.BlockSpec((8, LANES), lambda i: (0, 0))),
        out_shape=(jax.ShapeDtypeStruct((t, w), F32), jax.ShapeDtypeStruct((t, w), BF16),
                   jax.ShapeDtypeStruct((8, LANES), F32)),
        compiler_params=_params(1),
    )(f, w_down, x2, target)


def _adamw(parts, w, m, v, *, name):
    _, rows, cols = w.shape
    br = rows if rows <= 512 else 256
    assert rows % br == 0

    def body(p_ref, w_ref, m_ref, v_ref, g_ref, d_ref, nm_ref, nv_ref):
        g = p_ref[0].astype(F32)
        for r in range(1, N_DEV):
            g = g + p_ref[r].astype(F32)
        m2 = ADAM_B1 * m_ref[0] + (1.0 - ADAM_B1) * g
        v2 = ADAM_B2 * v_ref[0] + (1.0 - ADAM_B2) * jnp.square(g)
        m_hat = m2 / (1.0 - ADAM_B1 ** ADAM_STEP)
        v_hat = v2 / (1.0 - ADAM_B2 ** ADAM_STEP)
        g_ref[0] = g
        d_ref[0] = -ADAM_LR * (m_hat / (jnp.sqrt(v_hat) + ADAM_EPS) + ADAM_WD * w_ref[0])
        nm_ref[0] = m2
        nv_ref[0] = v2

    blk = pl.BlockSpec((1, br, cols), lambda i: (0, i, 0))
    shape = jax.ShapeDtypeStruct((1, rows, cols), F32)
    return pl.pallas_call(
        body, name=name, grid=(rows // br,),
        in_specs=[pl.BlockSpec((N_DEV, br, cols), lambda i: (0, i, 0)), blk, blk, blk],
        out_specs=(blk, blk, blk, blk), out_shape=(shape, shape, shape, shape), compiler_params=_params(1),
    )(parts, w, m, v)


_QA, _KA, _VA, _FA, _QD, _KD, _VD = (0, 512), (512, 1024), (1024, 1536), (1536, 1544), (1544, 2056), (2056, 2568), (2568, 3080)
_MAIN_ORDER = (_QA, _QD, _KA, _KD, _VA, _VD)
MAIN_COLS = 3 * D_MODEL
PROJ_COLS = MAIN_COLS + LANES
COL_SHARDED = ("w_in", "w_gate", "w_up")


def _swap(w):
    return jnp.transpose(w, (0, 2, 1))


def _w_in_to_kernel(w_t):
    main = jnp.concatenate([w_t[a:b] for a, b in _MAIN_ORDER], axis=0)
    forget = jnp.pad(w_t[_FA[0]:_FA[1]], ((0, LANES - N_HEADS_FOX), (0, 0)))
    return main, forget


def _w_in_from_kernel(g_t):
    pos = {span: i * W_GROUP for i, span in enumerate(_MAIN_ORDER)}
    parts = []
    for span in (_QA, _KA, _VA, _FA, _QD, _KD, _VD):
        if span == _FA:
            parts.append(g_t[MAIN_COLS:MAIN_COLS + N_HEADS_FOX])
        else:
            parts.append(g_t[pos[span]:pos[span] + W_GROUP])
    return jnp.concatenate(parts, axis=0)


def _pack_small(vals):
    rows = []
    for name, _, n_rows in SMALL_LAYOUT:
        flat = vals[name].reshape(-1).astype(F32)
        rows.append(jnp.pad(flat, (0, n_rows * LANES - flat.shape[0])).reshape(n_rows, LANES))
    packed = jnp.concatenate(rows, axis=0)
    return jnp.pad(packed, ((0, SMALL_ROWS - packed.shape[0]), (0, 0)))


def _unpack_small(packed, like):
    out = {}
    for name, row, n_rows in SMALL_LAYOUT:
        n = like[name].size
        out[name] = packed[row:row + n_rows].reshape(-1)[:n].reshape(like[name].shape)
    return out


def _device_step(x, target, small, shards):
    bsz, seq, _ = x.shape
    t = bsz * seq
    xf = x.reshape(t, D_MODEL)
    tf = target.reshape(t, D_MODEL)
    row = lambda v: v.reshape(1, -1)
    g_out = jnp.concatenate([small["g_out_fox"], small["g_out_dil"]]).reshape(1, D_MODEL)
    gains = jnp.concatenate(
        [jnp.tile(small[n].reshape(1, HEAD_DIM), (1, 2)) for n in ("g_q_fox", "g_q_dil", "g_k_fox", "g_k_dil")]
        + [jnp.zeros((4, LANES), F32)], axis=0)
    b_pad = jnp.pad(small["b_forget"].reshape(1, N_HEADS_FOX), ((0, 0), (0, LANES - N_HEADS_FOX)))
    rope = _rope_tables(seq)
    tables_qk = _bias_tables(seq, keys_first=False)
    tables_kq = _bias_tables(seq, keys_first=True)

    h1, g_in = _rmsnorm_fwd(xf, row(small["g_mix"]), group=D_MODEL, name="norm_mix",
                            hosted=_ChipGather([(shards["w_in"], False)]))
    w_main_t, w_fa_t = _w_in_to_kernel(g_in.reshape(IN_COLS, D_MODEL))
    w_in_all_t = jnp.concatenate([w_main_t, w_fa_t], axis=0)
    proj = _matmul_nt(h1, w_main_t, name="in_proj", out_dtype=BF16)
    fa = _matmul_nt(h1, w_fa_t, name="in_proj_forget", out_dtype=F32)
    qk = _qk_prep_fwd(proj, fa, b_pad, gains, rope, seq)
    late = _Exchange([(shards[n], False) for n in ("w_out", "w_gate", "w_up", "w_down")])
    o, lse, g_out_w, g_gate, g_up, g_down = _attn_fwd(qk, proj, tables_qk, seq, hosted=late)
    w_out = g_out_w.reshape(D_MODEL, D_MODEL)
    w_gate_t = g_gate.reshape(D_FF, D_MODEL)
    w_up_t = g_up.reshape(D_FF, D_MODEL)
    w_down = g_down.reshape(D_FF, D_MODEL)
    on, x2, h2 = _out_proj_ffn_norm(o, g_out, w_out, xf, row(small["g_ffn"]))
    a, u, f = _ffn_gate_up(h2, w_gate_t, w_up_t)
    dy, dy16, loss_tile = _ffn_down_loss(f, w_down, x2, tf)

    da, du = _ffn_down_grad(dy16, w_down, a, u)
    gw_down = _matmul_tn(f, dy16, name="gw_down")
    gw_gate_t = _matmul_tn(da, h2, name="gw_gate")
    gw_up_t = _matmul_tn(du, h2, name="gw_up")
    dx2, dx2_16, dg_ffn = _norm_input_grad([(da, w_gate_t, True), (du, w_up_t, True)], x2, row(small["g_ffn"]),
                                           group=D_MODEL, name="d_ffn_gate_up", out_dtypes=(F32, BF16), resid=dy,
                                           k_chunks=2)
    gw_out = _matmul_tn(on, dx2_16, name="gw_out")
    do, dg_out = _norm_input_grad([(dx2_16, w_out, False)], o, g_out, group=W_GROUP, name="d_out_proj",
                                  out_dtypes=(BF16,))

    shard_rows = lambda g: g.reshape(N_DEV, g.shape[0] // N_DEV, g.shape[1])
    ffn_grads = _Exchange([(shard_rows(g), True) for g in (gw_out, gw_gate_t, gw_up_t, gw_down)])
    dq, dk, dv, dqx, dkx, p_out, p_gate, p_up, p_down = _attn_bwd(qk, proj, tables_kq, o, lse, do, seq, hosted=ffn_grads)
    dproj, dgains, db = _qk_prep_bwd(dq, dk, dqx, dkx, dv, proj, fa, b_pad, gains, rope, seq)
    gw_in_t = _matmul_tn(dproj, h1, name="gw_in")
    in_grad = _Exchange([(shard_rows(_w_in_from_kernel(gw_in_t)), True)])
    dx, dg_mix, p_in = _norm_input_grad([(dproj, w_in_all_t, True)], xf, row(small["g_mix"]), group=D_MODEL,
                                        name="d_in_proj", out_dtypes=(F32,), resid=dx2, hosted=in_grad)

    fold = lambda rows: jnp.sum(rows[:, :HEAD_DIM] + rows[:, HEAD_DIM:], axis=0)
    half = N_PAIRS // 2
    gsmall = {
        "g_mix": dg_mix, "g_ffn": dg_ffn, "g_out_fox": dg_out[0, :W_GROUP], "g_out_dil": dg_out[0, W_GROUP:],
        "g_q_fox": fold(dgains[0:half]), "g_q_dil": fold(dgains[half:N_PAIRS]),
        "g_k_fox": fold(dgains[N_PAIRS:N_PAIRS + half]), "g_k_dil": fold(dgains[N_PAIRS + half:]),
        "b_forget": db[0, :N_HEADS_FOX],
    }
    packed = _pack_small(gsmall).at[LOSS_ROW].set(loss_tile[0])
    (p_small,) = _exchange("small_exchange", [(packed, False)])
    parts = {"w_in": p_in, "w_out": p_out, "w_gate": p_gate, "w_up": p_up, "w_down": p_down}
    return dx.reshape(x.shape), parts, p_small


def kernel(x, g_mix, w_in, b_forget, g_q_fox, g_k_fox, g_q_dil, g_k_dil, g_out_fox, g_out_dil, w_out, g_ffn, w_gate, w_up, w_down, loss_target, m_g_mix, m_w_in, m_b_forget, m_g_q_fox, m_g_k_fox, m_g_q_dil, m_g_k_dil, m_g_out_fox, m_g_out_dil, m_w_out, m_g_ffn, m_w_gate, m_w_up, m_w_down, v_g_mix, v_w_in, v_b_forget, v_g_q_fox, v_g_k_fox, v_g_q_dil, v_g_k_dil, v_g_out_fox, v_g_out_dil, v_w_out, v_g_ffn, v_w_gate, v_w_up, v_w_down):
    args = dict(locals())
    small_names = [name for name, _, _ in SMALL_LAYOUT]
    big_names = ["w_in", "w_out", "w_gate", "w_up", "w_down"]
    small = {n: args[n][0] for n in small_names}

    as_rows = lambda n, w: _swap(w) if n in COL_SHARDED else w
    shards = {n: as_rows(n, args[n])[0].astype(BF16) for n in big_names}
    grad_x, parts, p_small = _device_step(x, loss_target, small, shards)

    grads, deltas, new_m, new_v = {}, {}, {}, {}
    for n in big_names:
        res = _adamw(parts[n], as_rows(n, args[n]), as_rows(n, args["m_" + n]), as_rows(n, args["v_" + n]),
                     name="adamw_" + n)
        grads[n], deltas[n], new_m[n], new_v[n] = [as_rows(n, r) for r in res]
    res = _adamw(p_small, _pack_small(small)[None], _pack_small({n: args["m_" + n][0] for n in small_names})[None],
                 _pack_small({n: args["v_" + n][0] for n in small_names})[None], name="adamw_small")
    loss = res[0][0, LOSS_ROW, 0]
    for dst, packed_res in zip((grads, deltas, new_m, new_v), res):
        for n, val in _unpack_small(packed_res[0], small).items():
            dst[n] = val[None]

    order = ["g_mix", "w_in", "b_forget", "g_q_fox", "g_k_fox", "g_q_dil", "g_k_dil", "g_out_fox", "g_out_dil",
             "w_out", "g_ffn", "w_gate", "w_up", "w_down"]
    return (loss, grad_x, *[grads[n] for n in order], *[deltas[n] for n in order],
            *[new_m[n] for n in order], *[new_v[n] for n in order])
```

```python
import functools
import math

import jax
import jax.numpy as jnp
import numpy as np
from jax import lax
from jax.experimental import pallas as pl
from jax.experimental.pallas import tpu as pltpu

F32 = jnp.float32
BF16 = jnp.bfloat16

D_MODEL = 1024
HEAD_DIM = 64
LANES = 128
N_PAIRS = D_MODEL // LANES
N_HEADS = 2 * N_PAIRS
N_HEADS_FOX = 8
W_GROUP = 512
D_FF = 2816
IN_COLS = 3080
DILATION_PAIRS = ((128, 1), (512, 4), (2048, 16))
ROPE_THETA = 500000.0
ROPE_DIM = 16
ROPE_HALF = ROPE_DIM // 2
EPS = 1e-6
NEG = -1e30
LOG2E = 1.4426950408889634
LN2 = 0.6931471805599453
AUG_ONE = 0
AUG_C = 3
N_DEV = 8

ADAM_LR = 0.001
ADAM_B1 = 0.9
ADAM_B2 = 0.999
ADAM_EPS = 1e-08
ADAM_WD = 0.01
ADAM_STEP = 10

ROW_BLOCK = 512
ATT_BLOCK = 512
ATT_GROUP = 4
VMEM_LIMIT = 56 * 1024 * 1024
MATMUL_VMEM_BUDGET = 44 * 1024 * 1024

SMALL_ROWS = 32
SMALL_LAYOUT = (("g_mix", 0, 8), ("g_ffn", 8, 8), ("g_out_fox", 16, 4), ("g_out_dil", 20, 4),
                ("g_q_fox", 24, 1), ("g_k_fox", 25, 1), ("g_q_dil", 26, 1), ("g_k_dil", 27, 1),
                ("b_forget", 28, 1))
LOSS_ROW = 29


def _params(n_grid):
    return pltpu.CompilerParams(dimension_semantics=("arbitrary",) * n_grid, vmem_limit_bytes=VMEM_LIMIT)


def _divisor_block(n, cap):
    best = None
    for b in range(LANES, min(n, cap) + 1, LANES):
        if n % b == 0:
            best = b
    assert best is not None, n
    return best


def _split_dot(a, b_exact, terms):
    acc = None
    rest = a
    for _ in range(terms):
        hi = rest.astype(BF16)
        part = jnp.dot(hi, b_exact, preferred_element_type=F32)
        acc = part if acc is None else acc + part
        rest = rest - hi.astype(F32)
    return acc


def _split_dot_rhs(a_exact, b, terms):
    acc = None
    rest = b
    for _ in range(terms):
        hi = rest.astype(BF16)
        part = jnp.dot(a_exact, hi, preferred_element_type=F32)
        acc = part if acc is None else acc + part
        rest = rest - hi.astype(F32)
    return acc


def _split_dot_nt(a_exact, b, terms):
    acc = None
    rest = b
    for _ in range(terms):
        hi = rest.astype(BF16)
        part = _dot_nt(a_exact, hi)
        acc = part if acc is None else acc + part
        rest = rest - hi.astype(F32)
    return acc


def _dot_nt(a, b):
    return lax.dot_general(a, b, (((1,), (1,)), ((), ())), preferred_element_type=F32)


def _dot_tn(a, b):
    return lax.dot_general(a, b, (((0,), (0,)), ((), ())), preferred_element_type=F32)


class _Exchange:
    def __init__(self, items):
        self.items = items
        self.n = len(items)
        self.arrays = [a for a, _ in items]
        self.out_shape = [jax.ShapeDtypeStruct((N_DEV,) + tuple(a.shape[1:] if sc else a.shape), a.dtype)
                          for a, sc in items]
        self.specs = [pl.BlockSpec(memory_space=pl.ANY)] * self.n
        self.scratch = [pltpu.SemaphoreType.DMA((self.n, N_DEV - 1)), pltpu.SemaphoreType.DMA((self.n, N_DEV - 1)),
                        pltpu.SemaphoreType.DMA((self.n,))]

    def run(self, ins, outs, sems, first, last, compute):
        send_sems, recv_sems, local_sems = sems
        x, y, c = lax.axis_index("x"), lax.axis_index("y"), lax.axis_index("c")
        me = 4 * x + 2 * y + c
        local, remote = [], []
        for k, (_, scatter) in enumerate(self.items):
            own = ins[k].at[me] if scatter else ins[k]
            local.append(pltpu.make_async_copy(own, outs[k].at[me], local_sems.at[k]))
        for r in range(1, N_DEV):
            px = 1 - x if r & 4 else x
            py = 1 - y if r & 2 else y
            pc = 1 - c if r & 1 else c
            peer = 4 * px + 2 * py + pc
            for k, (_, scatter) in enumerate(self.items):
                src = ins[k].at[peer] if scatter else ins[k]
                remote.append(pltpu.make_async_remote_copy(
                    src_ref=src, dst_ref=outs[k].at[me],
                    send_sem=send_sems.at[k, r - 1], recv_sem=recv_sems.at[k, r - 1],
                    device_id=(px, py, pc), device_id_type=pl.DeviceIdType.MESH))

        def start():
            for cp in local + remote:
                cp.start()

        def finish():
            for cp in remote:
                cp.wait_recv()
            for cp in remote:
                cp.wait_send()
            for cp in local:
                cp.wait()

        _run_phases(first, last, start, compute, finish)


def _run_phases(first, last, start, compute, finish):
    if first is None:
        start()
        compute()
        finish()
    else:
        pl.when(first)(start)
        compute()
        pl.when(last)(finish)


class _ChipGather(_Exchange):
    def run(self, ins, outs, sems, first, last, compute):
        send_sems, recv_sems, local_sems = sems
        x, y, c = lax.axis_index("x"), lax.axis_index("y"), lax.axis_index("c")
        sibling = (x, y, 1 - c)
        chips = [(1 - x, y), (x, 1 - y), (1 - x, 1 - y)]
        slot = lambda px, py, pc: 4 * px + 2 * py + pc

        def copy(k, n, src, dst_slot, to):
            return pltpu.make_async_remote_copy(
                src_ref=src, dst_ref=outs[k].at[dst_slot], send_sem=send_sems.at[k, n], recv_sem=recv_sems.at[k, n],
                device_id=to, device_id_type=pl.DeviceIdType.MESH)

        local, own, passed, arrivals = [], [], [], []
        for k in range(self.n):
            me = slot(x, y, c)
            local.append(pltpu.make_async_copy(ins[k], outs[k].at[me], local_sems.at[k]))
            own.append(copy(k, 0, ins[k], me, sibling))
            arrivals.append(copy(k, 0, ins[k], slot(*sibling), sibling))
            for j, chip in enumerate(chips):
                theirs = slot(*chip, c)
                own.append(copy(k, 1 + j, ins[k], me, (*chip, c)))
                passed.append((copy(k, 1 + j, ins[k], theirs, sibling),
                               copy(k, 4 + j, outs[k].at[theirs], theirs, sibling)))
                arrivals.append(copy(k, 4 + j, ins[k], slot(*chip, 1 - c), sibling))

        def start():
            for cp in local + own:
                cp.start()

        def finish():
            for landed, onward in passed:
                landed.wait_recv()
                onward.start()
            for cp in arrivals:
                cp.wait_recv()
            for cp in own + [onward for _, onward in passed]:
                cp.wait_send()
            for cp in local:
                cp.wait()

        _run_phases(first, last, start, compute, finish)


def _grid_ends(grid):
    ids = [pl.program_id(d) for d in range(len(grid))]
    first = functools.reduce(jnp.logical_and, [i == 0 for i in ids])
    last = functools.reduce(jnp.logical_and, [i == g - 1 for i, g in zip(ids, grid)])
    return first, last


def _host(core, n_in, n_out, n_scratch, hosted, grid):
    if hosted is None:
        return core
    nh = hosted.n

    def body(*refs):
        ins, rest = refs[:n_in], refs[n_in:]
        h_ins, rest = rest[:nh], rest[nh:]
        outs, rest = rest[:n_out], rest[n_out:]
        h_outs, rest = rest[:nh], rest[nh:]
        scratch, sems = rest[:n_scratch], rest[n_scratch:]
        first, last = _grid_ends(grid)
        hosted.run(h_ins, h_outs, sems, first, last, lambda: core(*ins, *outs, *scratch))

    return body


def _hosted_parts(hosted):
    if hosted is None:
        return [], [], [], []
    return list(hosted.specs), list(hosted.out_shape), list(hosted.arrays), list(hosted.scratch)


def _exchange(name, items):
    ex = _Exchange(items)
    n = ex.n

    def body(*refs):
        ex.run(refs[:n], refs[n:2 * n], refs[2 * n:], None, None, lambda: None)

    return pl.pallas_call(
        body, name=name, out_shape=tuple(ex.out_shape), in_specs=ex.specs, out_specs=tuple(ex.specs),
        scratch_shapes=ex.scratch,
    )(*ex.arrays)


def _matmul_blocks(t, k, n, a_bytes, o_bytes):
    for bt, cap in ((1024, 1408), (1024, 512), (512, 512)):
        if t % bt:
            continue
        bn = _divisor_block(n, cap)
        if 2 * (bt * k * a_bytes + bn * k * 2 + bt * bn * o_bytes) <= MATMUL_VMEM_BUDGET:
            return bt, bn
    return ROW_BLOCK, _divisor_block(n, 256)


def _matmul_nt(a, w, *, name, out_dtype):
    t, k = a.shape
    n = w.shape[0]
    assert w.shape[1] == k
    bt, bn = _matmul_blocks(t, k, n, a.dtype.itemsize, jnp.dtype(out_dtype).itemsize)

    def body(a_ref, w_ref, o_ref):
        o_ref[...] = _dot_nt(a_ref[...], w_ref[...]).astype(o_ref.dtype)

    return pl.pallas_call(
        body, name=name, grid=(t // bt, n // bn),
        in_specs=[pl.BlockSpec((bt, k), lambda i, j: (i, 0)), pl.BlockSpec((bn, k), lambda i, j: (j, 0))],
        out_specs=pl.BlockSpec((bt, bn), lambda i, j: (i, j)),
        out_shape=jax.ShapeDtypeStruct((t, n), out_dtype), compiler_params=_params(2),
    )(a, w)


def _matmul_tn(a, b, *, name):
    t, m = a.shape
    n = b.shape[1]
    bt = 2048 if t % 2048 == 0 else ROW_BLOCK
    bm = _divisor_block(m, 1408)
    bn = _divisor_block(n, 1408)
    steps = t // bt

    def body(a_ref, b_ref, o_ref, acc):
        step = pl.program_id(2)

        @pl.when(step == 0)
        def _():
            acc[...] = jnp.zeros_like(acc)

        acc[...] += _dot_tn(a_ref[...], b_ref[...])

        @pl.when(step == steps - 1)
        def _():
            o_ref[...] = acc[...].astype(o_ref.dtype)

    return pl.pallas_call(
        body, name=name, grid=(m // bm, n // bn, steps),
        in_specs=[pl.BlockSpec((bt, bm), lambda i, j, s: (s, i)), pl.BlockSpec((bt, bn), lambda i, j, s: (s, j))],
        out_specs=pl.BlockSpec((bm, bn), lambda i, j, s: (i, j)),
        out_shape=jax.ShapeDtypeStruct((m, n), BF16), scratch_shapes=[pltpu.VMEM((bm, bn), F32)],
        compiler_params=_params(3),
    )(a, b)


def _rmsnorm_fwd(x, g, *, group, name, hosted=None):
    t, w = x.shape
    bt = ROW_BLOCK

    def body(x_ref, g_ref, o_ref):
        for s in range(0, w, group):
            xs = x_ref[:, s:s + group].astype(F32)
            r = lax.rsqrt(jnp.mean(xs * xs, axis=-1, keepdims=True) + EPS)
            o_ref[:, s:s + group] = (xs * r * g_ref[:, s:s + group]).astype(o_ref.dtype)

    grid = (t // bt,)
    h_specs, h_shapes, h_args, h_scratch = _hosted_parts(hosted)
    res = pl.pallas_call(
        _host(body, 2, 1, 0, hosted, grid), name=name, grid=grid,
        in_specs=[pl.BlockSpec((bt, w), lambda i: (i, 0)), pl.BlockSpec((1, w), lambda i: (0, 0))] + h_specs,
        out_specs=tuple([pl.BlockSpec((bt, w), lambda i: (i, 0))] + h_specs),
        out_shape=tuple([jax.ShapeDtypeStruct((t, w), BF16)] + h_shapes),
        scratch_shapes=h_scratch, compiler_params=_params(1),
    )(x, g, *h_args)
    return res if hosted else res[0]


def _norm_input_grad(terms, x, g, *, group, name, out_dtypes, resid=None, hosted=None, k_chunks=1):
    t, w = x.shape
    n_terms = len(terms)
    kc = [a.shape[1] // k_chunks for a, _, _ in terms]
    per_row = sum(c * a.dtype.itemsize for c, (a, _, _) in zip(kc, terms)) + w * (x.dtype.itemsize + 4 * (resid is not None))
    per_row += w * sum(jnp.dtype(dt).itemsize for dt in out_dtypes)
    fixed = 2 * sum(w * c * 2 for c in kc)
    bt = next(b for b in (1024, 512, 256, 128)
              if t % b == 0 and fixed + 2 * b * per_row + 5 * b * w * 4 <= MATMUL_VMEM_BUDGET)
    n_in = 2 * n_terms + 2 + (resid is not None)
    grid = (t // bt, k_chunks)

    def body(*refs):
        x_ref, g_ref = refs[2 * n_terms], refs[2 * n_terms + 1]
        dx_refs, dg_ref, dh_ref = refs[n_in:-2], refs[-2], refs[-1]
        chunk = pl.program_id(1)

        @pl.when((pl.program_id(0) == 0) & (chunk == 0))
        def _():
            dg_ref[...] = jnp.zeros_like(dg_ref)

        part = None
        for k in range(n_terms):
            if terms[k][2]:
                term = jnp.dot(refs[2 * k][...], refs[2 * k + 1][...], preferred_element_type=F32)
            else:
                term = _dot_nt(refs[2 * k][...], refs[2 * k + 1][...])
            part = term if part is None else part + term

        @pl.when(chunk == 0)
        def _():
            dh_ref[...] = part

        @pl.when(chunk > 0)
        def _():
            dh_ref[...] += part

        @pl.when(chunk == k_chunks - 1)
        def _():
            for s in range(0, w, group):
                xs = x_ref[:, s:s + group].astype(F32)
                dhs = dh_ref[:, s:s + group]
                r = lax.rsqrt(jnp.mean(xs * xs, axis=-1, keepdims=True) + EPS)
                xh = xs * r
                dg_ref[:, s:s + group] += jnp.sum(dhs * xh, axis=0, keepdims=True)
                dxh = dhs * g_ref[:, s:s + group]
                dx = r * (dxh - xh * jnp.mean(dxh * xh, axis=-1, keepdims=True))
                if resid is not None:
                    dx = refs[n_in - 1][:, s:s + group] + dx
                for dx_ref in dx_refs:
                    dx_ref[:, s:s + group] = dx.astype(dx_ref.dtype)

    row = pl.BlockSpec((bt, w), lambda i, k: (i, 0))
    vec = pl.BlockSpec((1, w), lambda i, k: (0, 0))
    in_specs, args = [], []
    for c, (a, wt, w_is_kn) in zip(kc, terms):
        assert wt.shape == ((a.shape[1], w) if w_is_kn else (w, a.shape[1]))
        w_spec = pl.BlockSpec((c, w), lambda i, k: (k, 0)) if w_is_kn else pl.BlockSpec((w, c), lambda i, k: (0, k))
        in_specs += [pl.BlockSpec((bt, c), lambda i, k: (i, k)), w_spec]
        args += [a, wt]
    in_specs += [row, vec] + ([row] if resid is not None else [])
    args += [x, g] + ([resid] if resid is not None else [])
    h_specs, h_shapes, h_args, h_scratch = _hosted_parts(hosted)
    return pl.pallas_call(
        _host(body, n_in, len(out_dtypes) + 1, 1, hosted, grid), name=name, grid=grid, in_specs=in_specs + h_specs,
        out_specs=tuple([row] * len(out_dtypes) + [vec] + h_specs),
        out_shape=tuple([jax.ShapeDtypeStruct((t, w), dt) for dt in out_dtypes] + [jax.ShapeDtypeStruct((1, w), F32)]
                        + h_shapes),
        scratch_shapes=[pltpu.VMEM((bt, w), F32)] + h_scratch, compiler_params=_params(2),
    )(*args, *h_args)


def _tile_plan(tile):
    is_q = tile < N_PAIRS
    is_dil = (tile % N_PAIRS) >= N_PAIRS // 2
    return is_q, is_dil, (0 if is_q else 2) + (1 if is_dil else 0)


def _segment_ones():
    lane = np.arange(LANES)
    return jnp.asarray((lane[:, None] // HEAD_DIM) == (lane[None, :] // HEAD_DIM), BF16)


def _rope_tables(seq):
    inv_freq = jnp.power(jnp.float32(ROPE_THETA), -jnp.arange(ROPE_HALF, dtype=F32) * 2.0 / ROPE_DIM)
    ang = jnp.arange(seq).astype(F32)[:, None] * inv_freq[None, :]
    cos, sin = jnp.cos(ang), jnp.sin(ang)
    ones = jnp.ones((seq, HEAD_DIM - ROPE_DIM), F32)
    zeros = jnp.zeros((seq, HEAD_DIM - ROPE_DIM), F32)
    zh = jnp.zeros((seq, ROPE_HALF), F32)
    cos_t = jnp.concatenate([cos, cos, ones], axis=1)
    sin_a = jnp.concatenate([-sin, zh, zeros], axis=1)
    sin_b = jnp.concatenate([zh, sin, zeros], axis=1)
    return tuple(jnp.tile(tab, (1, 2)) for tab in (cos_t, sin_a, sin_b))


def _log_sigmoid(z):
    return jnp.minimum(z, 0.0) - jnp.log1p(jnp.exp(-jnp.abs(z)))


def _qk_prep_fwd(proj, fa, b_pad, gains, rope, seq):
    t = proj.shape[0]
    bt = ROW_BLOCK
    nsb = seq // bt
    seg = _segment_ones()
    rr = np.arange(bt)
    tri = jnp.asarray(rr[:, None] >= rr[None, :], BF16)

    def body(p_ref, fa_ref, b_ref, g_ref, cos_ref, sa_ref, sb_ref, seg_ref, tri_ref, qk_ref, carry):
        @pl.when(pl.program_id(0) % nsb == 0)
        def _():
            carry[...] = jnp.zeros_like(carry)

        lane = lax.broadcasted_iota(jnp.int32, (bt, LANES), 1)
        logf = jnp.where(lane < N_HEADS_FOX, _log_sigmoid(fa_ref[...] + b_ref[...]), 0.0)
        cblk = _split_dot_rhs(tri_ref[...], logf, 3) + carry[0:1, :]
        carry[0:1, :] = cblk[bt - 1:bt, :]
        c_terms = []
        rest = cblk * LOG2E
        for _ in range(3):
            term = rest.astype(BF16).astype(F32)
            c_terms.append(term)
            rest = rest - term

        for tile in range(2 * N_PAIRS):
            is_q, is_dil, grow = _tile_plan(tile)
            pair = tile % N_PAIRS
            xs = p_ref[:, tile * LANES:(tile + 1) * LANES].astype(F32)
            r = lax.rsqrt(_split_dot(xs * xs, seg_ref[...], 2) * (1.0 / HEAD_DIM) + EPS)
            yv = xs * r * g_ref[grow:grow + 1, :]
            if is_dil:
                yv = (yv * cos_ref[...] + pltpu.roll(yv, LANES - ROPE_HALF, 1) * sa_ref[...]
                      + pltpu.roll(yv, ROPE_HALF, 1) * sb_ref[...])
            if is_q:
                yv = yv * (HEAD_DIM ** -0.5 * LOG2E)
            for e in range(2):
                head = 2 * pair + e
                other = HEAD_DIM * (1 - e)
                aug = jnp.zeros((bt, LANES), F32)
                if not is_dil:
                    ones_at = other + (AUG_ONE if is_q else AUG_C)
                    c_at = other + (AUG_C if is_q else AUG_ONE)
                    aug = jnp.where((lane >= ones_at) & (lane < ones_at + 3), 1.0, aug)
                    for n, term in enumerate(c_terms):
                        col = term[:, head:head + 1]
                        aug = jnp.where(lane == c_at + n, col if is_q else -col, aug)
                mine = (lane < HEAD_DIM) if e == 0 else (lane >= HEAD_DIM)
                dst = ((0 if is_q else N_HEADS) + head) * LANES
                qk_ref[:, dst:dst + LANES] = jnp.where(mine, yv, aug).astype(BF16)

    row128 = pl.BlockSpec((bt, LANES), lambda i: (i, 0))
    rope_spec = pl.BlockSpec((bt, LANES), lambda i: (i % nsb, 0))
    const = lambda shape: pl.BlockSpec(shape, lambda i: (0, 0))
    return pl.pallas_call(
        body, name="qk_prep_fwd", grid=(t // bt,),
        in_specs=[pl.BlockSpec((bt, 2 * D_MODEL), lambda i: (i, 0)), row128, const((1, LANES)), const((8, LANES)),
                  rope_spec, rope_spec, rope_spec, const((LANES, LANES)), const((bt, bt))],
        out_specs=pl.BlockSpec((bt, 2 * N_HEADS * LANES), lambda i: (i, 0)),
        out_shape=jax.ShapeDtypeStruct((t, 2 * N_HEADS * LANES), BF16),
        scratch_shapes=[pltpu.VMEM((8, LANES), F32)], compiler_params=_params(1),
    )(proj, fa, b_pad, gains, *rope, seg, tri)


def _qk_prep_bwd(dq, dk, dqx, dkx, dv, proj, fa, b_pad, gains, rope, seq):
    t = proj.shape[0]
    bt = ROW_BLOCK
    nsb = seq // bt
    nblk = t // bt
    seg = _segment_ones()
    rr = np.arange(bt)
    triu = jnp.asarray(rr[:, None] <= rr[None, :], BF16)

    def body(dq_ref, dk_ref, dqx_ref, dkx_ref, dv_ref, p_ref, fa_ref, b_ref, g_ref, cos_ref, sa_ref, sb_ref, seg_ref,
             triu_ref, dp_ref, dg_ref, db_ref, carry):
        step = pl.program_id(0)

        @pl.when(step == 0)
        def _():
            dg_ref[...] = jnp.zeros_like(dg_ref)
            db_ref[...] = jnp.zeros_like(db_ref)

        @pl.when(step % nsb == 0)
        def _():
            carry[...] = jnp.zeros_like(carry)

        for tile in range(2 * N_PAIRS):
            is_q, is_dil, grow = _tile_plan(tile)
            cols = slice(tile * LANES, (tile + 1) * LANES)
            src = dq_ref if is_q else dk_ref
            half = slice((tile % N_PAIRS) * LANES, (tile % N_PAIRS + 1) * LANES)
            dy = src[:, half]
            dy = dy * (HEAD_DIM ** -0.5 if is_q else LN2)
            if is_dil:
                dy = (dy * cos_ref[...] + pltpu.roll(dy * sa_ref[...], ROPE_HALF, 1)
                      + pltpu.roll(dy * sb_ref[...], LANES - ROPE_HALF, 1))
            xs = p_ref[:, cols].astype(F32)
            r = lax.rsqrt(_split_dot(xs * xs, seg_ref[...], 2) * (1.0 / HEAD_DIM) + EPS)
            xh = xs * r
            dg_ref[tile:tile + 1, :] += jnp.sum(dy * xh, axis=0, keepdims=True)
            dxh = dy * g_ref[grow:grow + 1, :]
            seg_mean = _split_dot(dxh * xh, seg_ref[...], 2) * (1.0 / HEAD_DIM)
            dp_ref[:, cols] = (r * (dxh - xh * seg_mean)).astype(BF16)

        lane = lax.broadcasted_iota(jnp.int32, (bt, LANES), 1)
        dc = jnp.zeros((bt, LANES), F32)
        for h in range(N_HEADS_FOX):
            other = (h // 2) * LANES + HEAD_DIM * (1 - h % 2)
            row_sum = dqx_ref[:, other + AUG_C:other + AUG_C + 1]
            col_sum = dkx_ref[:, other + AUG_ONE:other + AUG_ONE + 1]
            dc = jnp.where(lane == h, row_sum - col_sum, dc)
        dlogf = _split_dot_rhs(triu_ref[...], dc, 3) + carry[0:1, :]
        carry[0:1, :] = dlogf[0:1, :]
        z = fa_ref[...] + b_ref[...]
        dfa = dlogf * (1.0 / (1.0 + jnp.exp(z)))
        db_ref[0:1, :] += jnp.sum(dfa, axis=0, keepdims=True)
        dp_ref[:, 2 * D_MODEL:MAIN_COLS] = dv_ref[...]
        dp_ref[:, MAIN_COLS:PROJ_COLS] = dfa.astype(BF16)

    rev = lambda i: nblk - 1 - i
    row = lambda w: pl.BlockSpec((bt, w), lambda i: (rev(i), 0))
    rope_spec = pl.BlockSpec((bt, LANES), lambda i: (rev(i) % nsb, 0))
    const = lambda shape: pl.BlockSpec(shape, lambda i: (0, 0))
    return pl.pallas_call(
        body, name="qk_prep_bwd", grid=(nblk,),
        in_specs=[row(D_MODEL), row(D_MODEL), row(W_GROUP), row(W_GROUP), row(D_MODEL), row(2 * D_MODEL), row(LANES),
                  const((1, LANES)), const((8, LANES)), rope_spec, rope_spec, rope_spec, const((LANES, LANES)),
                  const((bt, bt))],
        out_specs=(row(PROJ_COLS), const((2 * N_PAIRS, LANES)), const((8, LANES))),
        out_shape=(jax.ShapeDtypeStruct((t, PROJ_COLS), BF16),
                   jax.ShapeDtypeStruct((2 * N_PAIRS, LANES), F32), jax.ShapeDtypeStruct((8, LANES), F32)),
        scratch_shapes=[pltpu.VMEM((8, LANES), F32)], compiler_params=_params(1),
    )(dq, dk, dqx, dkx, dv, proj, fa, b_pad, gains, *rope, seg, triu)


def _bias_tables(seq, keys_first):
    nb = seq // ATT_BLOCK
    idx = np.arange(ATT_BLOCK)
    q_idx, k_idx = (idx[None, None, :], idx[None, :, None]) if keys_first else (idx[None, :, None], idx[None, None, :])
    dist = np.arange(nb)[:, None, None] * ATT_BLOCK + q_idx - k_idx
    causal = dist >= 0
    count = np.zeros(dist.shape, np.int32)
    for window, dilation in DILATION_PAIRS:
        count = count + (causal & (dist % dilation == 0) & (dist <= window))
    fox = np.where(causal, 0.0, NEG)
    dil = np.where(count == 3, math.log2(3.0), np.where(count == 2, 1.0, np.where(count == 1, 0.0, NEG)))
    return jnp.asarray(np.stack([fox, dil], axis=0), F32)


def _attn_specs(seq):
    nb = seq // ATT_BLOCK
    col = lambda off: pl.BlockSpec((seq, LANES), lambda b, j: (b, off + j))
    heads = lambda off: pl.BlockSpec((seq, 2 * LANES), lambda b, j: (b, off + j))
    table_spec = pl.BlockSpec((1, nb, ATT_BLOCK, ATT_BLOCK), lambda b, j: (j // (N_PAIRS // 2), 0, 0, 0))
    return col, heads, table_spec


def _head_lanes(e, shape, axis):
    pos = lax.broadcasted_iota(jnp.int32, shape, axis)
    return pos < HEAD_DIM if e == 0 else pos >= HEAD_DIM


def _attn_fwd(qk, proj, tables, seq, hosted=None):
    t = qk.shape[0]
    nb = seq // ATT_BLOCK
    blk = ATT_BLOCK

    def body(q_ref, k_ref, v_ref, tab_ref, o_ref, lse_ref):
        mine = [_head_lanes(e, (seq, LANES), 1) for e in range(2)]
        lane = lax.broadcasted_iota(jnp.int32, (seq, LANES), 1)
        v_aug = [jnp.where(mine[e], v_ref[...], (lane == HEAD_DIM * (1 - e)).astype(BF16)) for e in range(2)]
        def scores(i, e):
            heads_e = slice(e * LANES, (e + 1) * LANES)
            s = _dot_nt(q_ref[i * blk:(i + 1) * blk, heads_e], k_ref[0:(i + 1) * blk, heads_e])
            s = jnp.concatenate([s[:, jj * blk:(jj + 1) * blk] + tab_ref[0, i - jj] for jj in range(i + 1)], axis=1)
            return s, jnp.max(s, axis=1, keepdims=True)

        chains = [(i, e) for i in range(nb) for e in range(2)]
        pending = scores(*chains[0])
        done = {}
        for n, (i, e) in enumerate(chains):
            s, m = pending
            if n + 1 < len(chains):
                pending = scores(*chains[n + 1])
            acc = jnp.dot(jnp.exp2(s - m).astype(BF16), v_aug[e][0:(i + 1) * blk], preferred_element_type=F32)
            ones_at = HEAD_DIM * (1 - e)
            l = acc[:, ones_at:ones_at + 1]
            done[e] = (acc / l, m + jnp.log2(l))
            if e == 1:
                rows = slice(i * blk, (i + 1) * blk)
                o_ref[rows, :] = jnp.where(mine[0][rows], done[0][0], done[1][0]).astype(o_ref.dtype)
                lse_ref[rows, :] = jnp.where(mine[0][rows], done[0][1], done[1][1])

    col, heads, table_spec = _attn_specs(seq)
    grid = (t // seq, N_PAIRS)
    h_specs, h_shapes, h_args, h_scratch = _hosted_parts(hosted)
    return pl.pallas_call(
        _host(body, 4, 2, 0, hosted, grid), name="attn_fwd", grid=grid,
        in_specs=[heads(0), heads(N_PAIRS), col(2 * N_PAIRS), table_spec] + h_specs,
        out_specs=tuple([col(0), col(0)] + h_specs),
        out_shape=tuple([jax.ShapeDtypeStruct((t, D_MODEL), BF16), jax.ShapeDtypeStruct((t, D_MODEL), F32)] + h_shapes),
        scratch_shapes=h_scratch, compiler_params=_params(2),
    )(qk, qk, proj, tables, *h_args)


def _attn_bwd(qk, proj, tables, o, lse, do, seq, hosted=None):
    t = qk.shape[0]
    nb = seq // ATT_BLOCK
    blk = ATT_BLOCK
    group = math.gcd(nb, ATT_GROUP)

    def body(q_ref, k_ref, v_ref, tab_ref, o_ref, lse_ref, do_ref,
             dq_ref, dk_ref, dv_ref, dqx_ref, dkx_ref, dk_acc, dv_acc):
        mine = [_head_lanes(e, (blk, LANES), 1) for e in range(2)]
        top = _head_lanes(0, (LANES, blk), 0)
        head_rows = lax.broadcasted_iota(jnp.int32, (8, LANES), 0)
        head_of_lane = lax.broadcasted_iota(jnp.int32, (8, LANES), 1) // HEAD_DIM
        head_sel = (head_rows == head_of_lane).astype(BF16)
        dk_acc[...] = jnp.zeros_like(dk_acc)
        dv_acc[...] = jnp.zeros_like(dv_acc)

        def block_rows(i):
            return pl.ds(pl.multiple_of(i * blk, blk), blk)

        def q_group(g, _):
            base = g * group
            qs, doe, delta, lse_e = [], [], [], []
            for b in range(group):
                rows = block_rows(base + b)
                qs.append([q_ref[rows, e * LANES:(e + 1) * LANES] for e in range(2)])
                do_blk = do_ref[rows, :]
                doe.append([jnp.where(mine[e], do_blk, jnp.zeros_like(do_blk)) for e in range(2)])
                delta_t = _split_dot_nt(head_sel, do_blk.astype(F32) * o_ref[rows, :].astype(F32), 3)
                lse_t = _split_dot_nt(head_sel, lse_ref[rows, :], 3) * (1.0 / HEAD_DIM)
                delta.append([delta_t[e:e + 1, :] for e in range(2)])
                lse_e.append([lse_t[e:e + 1, :] for e in range(2)])

            def key_block(dq_t, jj, members):
                krows = block_rows(jj)
                v = v_ref[krows, :]
                dq_t = [list(d) for d in dq_t]
                lo, hi = slice(0, blk // 2), slice(blk // 2, blk)
                dv_part = [None, None]
                add = lambda acc, part: part if acc is None else acc + part

                def probs(k_sub, v_sub, keys, queries, b, e, dist):
                    q_sub, do_sub = qs[b][e][queries], doe[b][e][queries]
                    p_t = jnp.exp2(_dot_nt(k_sub, q_sub) + tab_ref[0, dist, keys, queries] - lse_e[b][e][:, queries])
                    ds_t = (p_t * (_dot_nt(v_sub, do_sub) - delta[b][e][:, queries])).astype(BF16)
                    return p_t.astype(BF16), ds_t, q_sub, do_sub, k_sub

                def outputs(tile):
                    p_t, ds_t, q_sub, do_sub, k_sub = tile
                    return (jnp.dot(p_t, do_sub, preferred_element_type=F32),
                            jnp.dot(ds_t, q_sub, preferred_element_type=F32), _dot_tn(k_sub, ds_t))

                for e in range(2):
                    k_e = k_ref[krows, e * LANES:(e + 1) * LANES]
                    dk_part = [None, None]
                    tiles = []
                    for b, dist in members:
                        if isinstance(dist, int) and dist == 0:
                            tiles.append((b, probs(k_e[lo], v[lo], lo, slice(0, blk), b, e, dist),
                                          probs(k_e[hi], v[hi], hi, hi, b, e, dist)))
                        else:
                            tiles.append((b, probs(k_e, v, slice(0, blk), slice(0, blk), b, e, dist), None))
                    for b, first, second in tiles:
                        if second is not None:
                            dv_a, dk_a, dq_a = outputs(first)
                            dv_b, dk_b, dq_b = outputs(second)
                            halves = ((dv_a, dk_a), (dv_b, dk_b))
                            dq = jnp.concatenate([dq_a[:, lo], dq_a[:, hi] + dq_b], axis=1)
                        else:
                            dv_f, dk_f, dq = outputs(first)
                            halves = ((dv_f[lo], dk_f[lo]), (dv_f[hi], dk_f[hi]))
                        for n, (dv_h, dk_h) in enumerate(halves):
                            dv_part[n] = add(dv_part[n], dv_h)
                            dk_part[n] = add(dk_part[n], dk_h)
                        dq_t[b][e] = dq_t[b][e] + dq
                    dk_acc[e, krows, :] += jnp.concatenate(dk_part, axis=0)
                dv_acc[krows, :] += jnp.concatenate(dv_part, axis=0)
                return tuple(tuple(d) for d in dq_t)

            zacc = jnp.zeros((LANES, blk), F32)
            dq_t = tuple((zacc, zacc) for _ in range(group))
            dq_t = lax.fori_loop(
                0, base, lambda jj, st: key_block(st, jj, [(b, base + b - jj) for b in range(group)]), dq_t)
            for a in range(group):
                dq_t = key_block(dq_t, base + a, [(b, b - a) for b in range(a, group)])
            for b in range(group):
                rows = block_rows(base + b)
                dq_ref[rows, :] = jnp.where(top, dq_t[b][0], dq_t[b][1]).T
                dqx_ref[rows, :] = jnp.where(top, dq_t[b][1], dq_t[b][0]).T
            return 0

        lax.fori_loop(0, nb // group, q_group, 0)
        lo = _head_lanes(0, (seq, LANES), 1)
        dk_ref[...] = jnp.where(lo, dk_acc[0], dk_acc[1])
        dkx_ref[...] = jnp.where(lo, dk_acc[1], dk_acc[0])
        dv_ref[...] = dv_acc[...].astype(dv_ref.dtype)

    col, heads, table_spec = _attn_specs(seq)
    grid = (t // seq, N_PAIRS)
    h_specs, h_shapes, h_args, h_scratch = _hosted_parts(hosted)
    f32_out = jax.ShapeDtypeStruct((t, D_MODEL), F32)
    return pl.pallas_call(
        _host(body, 7, 5, 2, hosted, grid), name="attn_bwd", grid=grid,
        in_specs=[heads(0), heads(N_PAIRS), col(2 * N_PAIRS), table_spec, col(0), col(0), col(0)] + h_specs,
        out_specs=tuple([col(0)] * 5 + h_specs),
        out_shape=tuple([f32_out, f32_out, jax.ShapeDtypeStruct((t, D_MODEL), BF16), f32_out, f32_out] + h_shapes),
        scratch_shapes=[pltpu.VMEM((2, seq, LANES), F32), pltpu.VMEM((seq, LANES), F32)] + h_scratch,
        compiler_params=_params(2),
    )(qk, qk, proj, tables, o, lse, do, *h_args)


def _row_block(t):
    return 1024 if t % 1024 == 0 else ROW_BLOCK


def _out_proj_ffn_norm(o, g_out, w_out, x, g_ffn):
    t = o.shape[0]
    bt = _row_block(t)

    def body(o_ref, go_ref, w_ref, x_ref, gf_ref, on_ref, x2_ref, h2_ref):
        for s in range(0, D_MODEL, W_GROUP):
            os_ = o_ref[:, s:s + W_GROUP].astype(F32)
            r = lax.rsqrt(jnp.mean(os_ * os_, axis=-1, keepdims=True) + EPS)
            on_ref[:, s:s + W_GROUP] = (os_ * r * go_ref[:, s:s + W_GROUP]).astype(BF16)
        x2 = x_ref[...] + jnp.dot(on_ref[...], w_ref[...], preferred_element_type=F32)
        x2_ref[...] = x2
        r2 = lax.rsqrt(jnp.mean(x2 * x2, axis=-1, keepdims=True) + EPS)
        h2_ref[...] = (x2 * r2 * gf_ref[...]).astype(BF16)

    row = pl.BlockSpec((bt, D_MODEL), lambda i: (i, 0))
    vec = pl.BlockSpec((1, D_MODEL), lambda i: (0, 0))
    return pl.pallas_call(
        body, name="out_proj", grid=(t // bt,),
        in_specs=[row, vec, pl.BlockSpec((D_MODEL, D_MODEL), lambda i: (0, 0)), row, vec],
        out_specs=(row, row, row),
        out_shape=(jax.ShapeDtypeStruct((t, D_MODEL), BF16), jax.ShapeDtypeStruct((t, D_MODEL), F32),
                   jax.ShapeDtypeStruct((t, D_MODEL), BF16)),
        compiler_params=_params(1),
    )(o, g_out, w_out, x, g_ffn)


def _ffn_gate_up(h2, w_gate_t, w_up_t):
    t = h2.shape[0]
    bt = _row_block(t)
    bn = _divisor_block(D_FF, 1408)

    def body(h_ref, wg_ref, wu_ref, a_ref, u_ref, f_ref):
        a = _dot_nt(h_ref[...], wg_ref[...])
        u = _dot_nt(h_ref[...], wu_ref[...])
        a_ref[...] = a.astype(BF16)
        u_ref[...] = u.astype(BF16)
        f_ref[...] = (a * jax.nn.sigmoid(a) * u).astype(BF16)

    blk = pl.BlockSpec((bt, bn), lambda i, j: (i, j))
    w_blk = pl.BlockSpec((bn, D_MODEL), lambda i, j: (j, 0))
    shape = jax.ShapeDtypeStruct((t, D_FF), BF16)
    return pl.pallas_call(
        body, name="ffn_gate_up", grid=(t // bt, D_FF // bn),
        in_specs=[pl.BlockSpec((bt, D_MODEL), lambda i, j: (i, 0)), w_blk, w_blk],
        out_specs=(blk, blk, blk), out_shape=(shape, shape, shape), compiler_params=_params(2),
    )(h2, w_gate_t, w_up_t)


def _ffn_down_grad(dy16, w_down, a, u):
    t = a.shape[0]
    bt = _row_block(t)
    bn = _divisor_block(D_FF, 1408)

    def body(dy_ref, w_ref, a_ref, u_ref, da_ref, du_ref):
        df = _dot_nt(dy_ref[...], w_ref[...])
        av = a_ref[...].astype(F32)
        sg = jax.nn.sigmoid(av)
        da_ref[...] = (df * u_ref[...].astype(F32) * sg * (1.0 + av * (1.0 - sg))).astype(BF16)
        du_ref[...] = (df * av * sg).astype(BF16)

    blk = pl.BlockSpec((bt, bn), lambda i, j: (i, j))
    shape = jax.ShapeDtypeStruct((t, D_FF), BF16)
    return pl.pallas_call(
        body, name="d_ffn_down", grid=(t // bt, D_FF // bn),
        in_specs=[pl.BlockSpec((bt, D_MODEL), lambda i, j: (i, 0)), pl.BlockSpec((bn, D_MODEL), lambda i, j: (j, 0)),
                  blk, blk],
        out_specs=(blk, blk), out_shape=(shape, shape), compiler_params=_params(2),
    )(dy16, w_down, a, u)


def _ffn_down_loss(f, w_down, x2, target):
    t, w = x2.shape
    bt = _row_block(t)

    def body(f_ref, w_ref, x_ref, t_ref, dy_ref, dy16_ref, loss_ref):
        @pl.when(pl.program_id(0) == 0)
        def _():
            loss_ref[...] = jnp.zeros_like(loss_ref)

        err = (x_ref[...] + jnp.dot(f_ref[...], w_ref[...], preferred_element_type=F32)) - t_ref[...]
        dy = err * (1.0 / w)
        dy_ref[...] = dy
        dy16_ref[...] = dy.astype(BF16)
        loss_ref[...] += 0.5 * jnp.sum(jnp.mean(err * err, axis=-1, keepdims=True), axis=0, keepdims=True)

    row = pl.BlockSpec((bt, w), lambda i: (i, 0))
    return pl.pallas_call(
        body, name="ffn_down_loss", grid=(t // bt,),
        in_specs=[pl.BlockSpec((bt, D_FF), lambda i: (i, 0)), pl.BlockSpec((D_FF, w), lambda i: (0, 0)), row, row],
        out_specs=(row, row, pl.BlockSpec((8, LANES), lambda i: (0, 0))),
        out_shape=(jax.ShapeDtypeStruct((t, w), F32), jax.ShapeDtypeStruct((t, w), BF16),
                   jax.ShapeDtypeStruct((8, LANES), F32)),
        compiler_params=_params(1),
    )(f, w_down, x2, target)


def _adamw(parts, w, m, v, *, name):
    _, rows, cols = w.shape
    br = rows if rows <= 512 else 256
    assert rows % br == 0

    def body(p_ref, w_ref, m_ref, v_ref, g_ref, d_ref, nm_ref, nv_ref):
        g = p_ref[0].astype(F32)
        for r in range(1, N_DEV):
            g = g + p_ref[r].astype(F32)
        m2 = ADAM_B1 * m_ref[0] + (1.0 - ADAM_B1) * g
        v2 = ADAM_B2 * v_ref[0] + (1.0 - ADAM_B2) * jnp.square(g)
        m_hat = m2 / (1.0 - ADAM_B1 ** ADAM_STEP)
        v_hat = v2 / (1.0 - ADAM_B2 ** ADAM_STEP)
        g_ref[0] = g
        d_ref[0] = -ADAM_LR * (m_hat / (jnp.sqrt(v_hat) + ADAM_EPS) + ADAM_WD * w_ref[0])
        nm_ref[0] = m2
        nv_ref[0] = v2

    blk = pl.BlockSpec((1, br, cols), lambda i: (0, i, 0))
    shape = jax.ShapeDtypeStruct((1, rows, cols), F32)
    return pl.pallas_call(
        body, name=name, grid=(rows // br,),
        in_specs=[pl.BlockSpec((N_DEV, br, cols), lambda i: (0, i, 0)), blk, blk, blk],
        out_specs=(blk, blk, blk, blk), out_shape=(shape, shape, shape, shape), compiler_params=_params(1),
    )(parts, w, m, v)


_QA, _KA, _VA, _FA, _QD, _KD, _VD = (0, 512), (512, 1024), (1024, 1536), (1536, 1544), (1544, 2056), (2056, 2568), (2568, 3080)
_MAIN_ORDER = (_QA, _QD, _KA, _KD, _VA, _VD)
MAIN_COLS = 3 * D_MODEL
PROJ_COLS = MAIN_COLS + LANES
COL_SHARDED = ("w_in", "w_gate", "w_up")


def _swap(w):
    return jnp.transpose(w, (0, 2, 1))


def _w_in_to_kernel(w_t):
    main = jnp.concatenate([w_t[a:b] for a, b in _MAIN_ORDER], axis=0)
    forget = jnp.pad(w_t[_FA[0]:_FA[1]], ((0, LANES - N_HEADS_FOX), (0, 0)))
    return main, forget


def _w_in_from_kernel(g_t):
    pos = {span: i * W_GROUP for i, span in enumerate(_MAIN_ORDER)}
    parts = []
    for span in (_QA, _KA, _VA, _FA, _QD, _KD, _VD):
        if span == _FA:
            parts.append(g_t[MAIN_COLS:MAIN_COLS + N_HEADS_FOX])
        else:
            parts.append(g_t[pos[span]:pos[span] + W_GROUP])
    return jnp.concatenate(parts, axis=0)


def _pack_small(vals):
    rows = []
    for name, _, n_rows in SMALL_LAYOUT:
        flat = vals[name].reshape(-1).astype(F32)
        rows.append(jnp.pad(flat, (0, n_rows * LANES - flat.shape[0])).reshape(n_rows, LANES))
    packed = jnp.concatenate(rows, axis=0)
    return jnp.pad(packed, ((0, SMALL_ROWS - packed.shape[0]), (0, 0)))


def _unpack_small(packed, like):
    out = {}
    for name, row, n_rows in SMALL_LAYOUT:
        n = like[name].size
        out[name] = packed[row:row + n_rows].reshape(-1)[:n].reshape(like[name].shape)
    return out


def _device_step(x, target, small, shards):
    bsz, seq, _ = x.shape
    t = bsz * seq
    xf = x.reshape(t, D_MODEL)
    tf = target.reshape(t, D_MODEL)
    row = lambda v: v.reshape(1, -1)
    g_out = jnp.concatenate([small["g_out_fox"], small["g_out_dil"]]).reshape(1, D_MODEL)
    gains = jnp.concatenate(
        [jnp.tile(small[n].reshape(1, HEAD_DIM), (1, 2)) for n in ("g_q_fox", "g_q_dil", "g_k_fox", "g_k_dil")]
        + [jnp.zeros((4, LANES), F32)], axis=0)
    b_pad = jnp.pad(small["b_forget"].reshape(1, N_HEADS_FOX), ((0, 0), (0, LANES - N_HEADS_FOX)))
    rope = _rope_tables(seq)
    tables_qk = _bias_tables(seq, keys_first=False)
    tables_kq = _bias_tables(seq, keys_first=True)

    h1, g_in = _rmsnorm_fwd(xf, row(small["g_mix"]), group=D_MODEL, name="norm_mix",
                            hosted=_ChipGather([(shards["w_in"], False)]))
    w_main_t, w_fa_t = _w_in_to_kernel(g_in.reshape(IN_COLS, D_MODEL))
    w_in_all_t = jnp.concatenate([w_main_t, w_fa_t], axis=0)
    proj = _matmul_nt(h1, w_main_t, name="in_proj", out_dtype=BF16)
    fa = _matmul_nt(h1, w_fa_t, name="in_proj_forget", out_dtype=F32)
    qk = _qk_prep_fwd(proj, fa, b_pad, gains, rope, seq)
    late = _Exchange([(shards[n], False) for n in ("w_out", "w_gate", "w_up", "w_down")])
    o, lse, g_out_w, g_gate, g_up, g_down = _attn_fwd(qk, proj, tables_qk, seq, hosted=late)
    w_out = g_out_w.reshape(D_MODEL, D_MODEL)
    w_gate_t = g_gate.reshape(D_FF, D_MODEL)
    w_up_t = g_up.reshape(D_FF, D_MODEL)
    w_down = g_down.reshape(D_FF, D_MODEL)
    on, x2, h2 = _out_proj_ffn_norm(o, g_out, w_out, xf, row(small["g_ffn"]))
    a, u, f = _ffn_gate_up(h2, w_gate_t, w_up_t)
    dy, dy16, loss_tile = _ffn_down_loss(f, w_down, x2, tf)

    da, du = _ffn_down_grad(dy16, w_down, a, u)
    gw_down = _matmul_tn(f, dy16, name="gw_down")
    gw_gate_t = _matmul_tn(da, h2, name="gw_gate")
    gw_up_t = _matmul_tn(du, h2, name="gw_up")
    dx2, dx2_16, dg_ffn = _norm_input_grad([(da, w_gate_t, True), (du, w_up_t, True)], x2, row(small["g_ffn"]),
                                           group=D_MODEL, name="d_ffn_gate_up", out_dtypes=(F32, BF16), resid=dy,
                                           k_chunks=2)
    gw_out = _matmul_tn(on, dx2_16, name="gw_out")
    do, dg_out = _norm_input_grad([(dx2_16, w_out, False)], o, g_out, group=W_GROUP, name="d_out_proj",
                                  out_dtypes=(BF16,))

    shard_rows = lambda g: g.reshape(N_DEV, g.shape[0] // N_DEV, g.shape[1])
    ffn_grads = _Exchange([(shard_rows(g), True) for g in (gw_out, gw_gate_t, gw_up_t, gw_down)])
    dq, dk, dv, dqx, dkx, p_out, p_gate, p_up, p_down = _attn_bwd(qk, proj, tables_kq, o, lse, do, seq, hosted=ffn_grads)
    dproj, dgains, db = _qk_prep_bwd(dq, dk, dqx, dkx, dv, proj, fa, b_pad, gains, rope, seq)
    gw_in_t = _matmul_tn(dproj, h1, name="gw_in")
    in_grad = _Exchange([(shard_rows(_w_in_from_kernel(gw_in_t)), True)])
    dx, dg_mix, p_in = _norm_input_grad([(dproj, w_in_all_t, True)], xf, row(small["g_mix"]), group=D_MODEL,
                                        name="d_in_proj", out_dtypes=(F32,), resid=dx2, hosted=in_grad)

    fold = lambda rows: jnp.sum(rows[:, :HEAD_DIM] + rows[:, HEAD_DIM:], axis=0)
    half = N_PAIRS // 2
    gsmall = {
        "g_mix": dg_mix, "g_ffn": dg_ffn, "g_out_fox": dg_out[0, :W_GROUP], "g_out_dil": dg_out[0, W_GROUP:],
        "g_q_fox": fold(dgains[0:half]), "g_q_dil": fold(dgains[half:N_PAIRS]),
        "g_k_fox": fold(dgains[N_PAIRS:N_PAIRS + half]), "g_k_dil": fold(dgains[N_PAIRS + half:]),
        "b_forget": db[0, :N_HEADS_FOX],
    }
    packed = _pack_small(gsmall).at[LOSS_ROW].set(loss_tile[0])
    (p_small,) = _exchange("small_exchange", [(packed, False)])
    parts = {"w_in": p_in, "w_out": p_out, "w_gate": p_gate, "w_up": p_up, "w_down": p_down}
    return dx.reshape(x.shape), parts, p_small


def kernel(x, g_mix, w_in, b_forget, g_q_fox, g_k_fox, g_q_dil, g_k_dil, g_out_fox, g_out_dil, w_out, g_ffn, w_gate, w_up, w_down, loss_target, m_g_mix, m_w_in, m_b_forget, m_g_q_fox, m_g_k_fox, m_g_q_dil, m_g_k_dil, m_g_out_fox, m_g_out_dil, m_w_out, m_g_ffn, m_w_gate, m_w_up, m_w_down, v_g_mix, v_w_in, v_b_forget, v_g_q_fox, v_g_k_fox, v_g_q_dil, v_g_k_dil, v_g_out_fox, v_g_out_dil, v_w_out, v_g_ffn, v_w_gate, v_w_up, v_w_down):
    args = dict(locals())
    small_names = [name for name, _, _ in SMALL_LAYOUT]
    big_names = ["w_in", "w_out", "w_gate", "w_up", "w_down"]
    small = {n: args[n][0] for n in small_names}

    as_rows = lambda n, w: _swap(w) if n in COL_SHARDED else w
    shards = {n: as_rows(n, args[n])[0].astype(BF16) for n in big_names}
    grad_x, parts, p_small = _device_step(x, loss_target, small, shards)

    grads, deltas, new_m, new_v = {}, {}, {}, {}
    for n in big_names:
        res = _adamw(parts[n], as_rows(n, args[n]), as_rows(n, args["m_" + n]), as_rows(n, args["v_" + n]),
                     name="adamw_" + n)
        grads[n], deltas[n], new_m[n], new_v[n] = [as_rows(n, r) for r in res]
    res = _adamw(p_small, _pack_small(small)[None], _pack_small({n: args["m_" + n][0] for n in small_names})[None],
                 _pack_small({n: args["v_" + n][0] for n in small_names})[None], name="adamw_small")
    loss = res[0][0, LOSS_ROW, 0]
    for dst, packed_res in zip((grads, deltas, new_m, new_v), res):
        for n, val in _unpack_small(packed_res[0], small).items():
            dst[n] = val[None]

    order = ["g_mix", "w_in", "b_forget", "g_q_fox", "g_k_fox", "g_q_dil", "g_k_dil", "g_out_fox", "g_out_dil",
             "w_out", "g_ffn", "w_gate", "w_up", "w_down"]
    return (loss, grad_x, *[grads[n] for n in order], *[deltas[n] for n in order],
            *[new_m[n] for n in order], *[new_v[n] for n in order])
```

```python
import functools
import math

import jax
import jax.numpy as jnp
import numpy as np
from jax import lax
from jax.experimental import pallas as pl
from jax.experimental.pallas import tpu as pltpu

F32 = jnp.float32
BF16 = jnp.bfloat16

D_MODEL = 1024
HEAD_DIM = 64
LANES = 128
N_PAIRS = D_MODEL // LANES
N_HEADS = 2 * N_PAIRS
N_HEADS_FOX = 8
W_GROUP = 512
D_FF = 2816
IN_COLS = 3080
DILATION_PAIRS = ((128, 1), (512, 4), (2048, 16))
ROPE_THETA = 500000.0
ROPE_DIM = 16
ROPE_HALF = ROPE_DIM // 2
EPS = 1e-6
NEG = -1e30
LOG2E = 1.4426950408889634
LN2 = 0.6931471805599453
AUG_ONE = 0
AUG_C = 3
N_DEV = 8

ADAM_LR = 0.001
ADAM_B1 = 0.9
ADAM_B2 = 0.999
ADAM_EPS = 1e-08
ADAM_WD = 0.01
ADAM_STEP = 10

ROW_BLOCK = 512
ATT_BLOCK = 512
ATT_GROUP = 4
VMEM_LIMIT = 56 * 1024 * 1024
MATMUL_VMEM_BUDGET = 44 * 1024 * 1024

SMALL_ROWS = 32
SMALL_LAYOUT = (("g_mix", 0, 8), ("g_ffn", 8, 8), ("g_out_fox", 16, 4), ("g_out_dil", 20, 4),
                ("g_q_fox", 24, 1), ("g_k_fox", 25, 1), ("g_q_dil", 26, 1), ("g_k_dil", 27, 1),
                ("b_forget", 28, 1))
LOSS_ROW = 29


def _params(n_grid):
    return pltpu.CompilerParams(dimension_semantics=("arbitrary",) * n_grid, vmem_limit_bytes=VMEM_LIMIT)


def _divisor_block(n, cap):
    best = None
    for b in range(LANES, min(n, cap) + 1, LANES):
        if n % b == 0:
            best = b
    assert best is not None, n
    return best


def _split_dot(a, b_exact, terms):
    acc = None
    rest = a
    for _ in range(terms):
        hi = rest.astype(BF16)
        part = jnp.dot(hi, b_exact, preferred_element_type=F32)
        acc = part if acc is None else acc + part
        rest = rest - hi.astype(F32)
    return acc


def _split_dot_nt(a_exact, b, terms):
    acc = None
    rest = b
    for _ in range(terms):
        hi = rest.astype(BF16)
        part = _dot_nt(a_exact, hi)
        acc = part if acc is None else acc + part
        rest = rest - hi.astype(F32)
    return acc


def _dot_nt(a, b):
    return lax.dot_general(a, b, (((1,), (1,)), ((), ())), preferred_element_type=F32)


def _dot_tn(a, b):
    return lax.dot_general(a, b, (((0,), (0,)), ((), ())), preferred_element_type=F32)


class _Exchange:
    def __init__(self, items):
        self.items = items
        self.n = len(items)
        self.arrays = [a for a, _ in items]
        self.out_shape = [jax.ShapeDtypeStruct((N_DEV,) + tuple(a.shape[1:] if sc else a.shape), a.dtype)
                          for a, sc in items]
        self.specs = [pl.BlockSpec(memory_space=pl.ANY)] * self.n
        self.scratch = [pltpu.SemaphoreType.DMA((self.n, N_DEV - 1)), pltpu.SemaphoreType.DMA((self.n, N_DEV - 1)),
                        pltpu.SemaphoreType.DMA((self.n,))]

    def run(self, ins, outs, sems, first, last, compute):
        send_sems, recv_sems, local_sems = sems
        x, y, c = lax.axis_index("x"), lax.axis_index("y"), lax.axis_index("c")
        me = 4 * x + 2 * y + c
        local, remote = [], []
        for k, (_, scatter) in enumerate(self.items):
            own = ins[k].at[me] if scatter else ins[k]
            local.append(pltpu.make_async_copy(own, outs[k].at[me], local_sems.at[k]))
        for r in range(1, N_DEV):
            px = 1 - x if r & 4 else x
            py = 1 - y if r & 2 else y
            pc = 1 - c if r & 1 else c
            peer = 4 * px + 2 * py + pc
            for k, (_, scatter) in enumerate(self.items):
                src = ins[k].at[peer] if scatter else ins[k]
                remote.append(pltpu.make_async_remote_copy(
                    src_ref=src, dst_ref=outs[k].at[me],
                    send_sem=send_sems.at[k, r - 1], recv_sem=recv_sems.at[k, r - 1],
                    device_id=(px, py, pc), device_id_type=pl.DeviceIdType.MESH))

        def start():
            for cp in local + remote:
                cp.start()

        def finish():
            for cp in remote:
                cp.wait_recv()
            for cp in remote:
                cp.wait_send()
            for cp in local:
                cp.wait()

        _run_phases(first, last, start, compute, finish)


def _run_phases(first, last, start, compute, finish):
    if first is None:
        start()
        compute()
        finish()
    else:
        pl.when(first)(start)
        compute()
        pl.when(last)(finish)


class _ChipGather(_Exchange):
    def run(self, ins, outs, sems, first, last, compute):
        send_sems, recv_sems, local_sems = sems
        x, y, c = lax.axis_index("x"), lax.axis_index("y"), lax.axis_index("c")
        sibling = (x, y, 1 - c)
        chips = [(1 - x, y), (x, 1 - y), (1 - x, 1 - y)]
        slot = lambda px, py, pc: 4 * px + 2 * py + pc

        def copy(k, n, src, dst_slot, to):
            return pltpu.make_async_remote_copy(
                src_ref=src, dst_ref=outs[k].at[dst_slot], send_sem=send_sems.at[k, n], recv_sem=recv_sems.at[k, n],
                device_id=to, device_id_type=pl.DeviceIdType.MESH)

        local, own, passed, arrivals = [], [], [], []
        for k in range(self.n):
            me = slot(x, y, c)
            local.append(pltpu.make_async_copy(ins[k], outs[k].at[me], local_sems.at[k]))
            own.append(copy(k, 0, ins[k], me, sibling))
            arrivals.append(copy(k, 0, ins[k], slot(*sibling), sibling))
            for j, chip in enumerate(chips):
                theirs = slot(*chip, c)
                own.append(copy(k, 1 + j, ins[k], me, (*chip, c)))
                passed.append((copy(k, 1 + j, ins[k], theirs, sibling),
                               copy(k, 4 + j, outs[k].at[theirs], theirs, sibling)))
                arrivals.append(copy(k, 4 + j, ins[k], slot(*chip, 1 - c), sibling))

        def start():
            for cp in local + own:
                cp.start()

        def finish():
            for landed, onward in passed:
                landed.wait_recv()
                onward.start()
            for cp in arrivals:
                cp.wait_recv()
            for cp in own + [onward for _, onward in passed]:
                cp.wait_send()
            for cp in local:
                cp.wait()

        _run_phases(first, last, start, compute, finish)


def _grid_ends(grid):
    ids = [pl.program_id(d) for d in range(len(grid))]
    first = functools.reduce(jnp.logical_and, [i == 0 for i in ids])
    last = functools.reduce(jnp.logical_and, [i == g - 1 for i, g in zip(ids, grid)])
    return first, last


def _host(core, n_in, n_out, n_scratch, hosted, grid):
    if hosted is None:
        return core
    nh = hosted.n

    def body(*refs):
        ins, rest = refs[:n_in], refs[n_in:]
        h_ins, rest = rest[:nh], rest[nh:]
        outs, rest = rest[:n_out], rest[n_out:]
        h_outs, rest = rest[:nh], rest[nh:]
        scratch, sems = rest[:n_scratch], rest[n_scratch:]
        first, last = _grid_ends(grid)
        hosted.run(h_ins, h_outs, sems, first, last, lambda: core(*ins, *outs, *scratch))

    return body


def _hosted_parts(hosted):
    if hosted is None:
        return [], [], [], []
    return list(hosted.specs), list(hosted.out_shape), list(hosted.arrays), list(hosted.scratch)


def _exchange(name, items):
    ex = _Exchange(items)
    n = ex.n

    def body(*refs):
        ex.run(refs[:n], refs[n:2 * n], refs[2 * n:], None, None, lambda: None)

    return pl.pallas_call(
        body, name=name, out_shape=tuple(ex.out_shape), in_specs=ex.specs, out_specs=tuple(ex.specs),
        scratch_shapes=ex.scratch,
    )(*ex.arrays)


def _matmul_blocks(t, k, n, a_bytes, o_bytes):
    for bt, cap in ((1024, 1408), (1024, 512), (512, 512)):
        if t % bt:
            continue
        bn = _divisor_block(n, cap)
        if 2 * (bt * k * a_bytes + bn * k * 2 + bt * bn * o_bytes) <= MATMUL_VMEM_BUDGET:
            return bt, bn
    return ROW_BLOCK, _divisor_block(n, 256)


def _matmul_nt(a, w, *, name, out_dtype):
    t, k = a.shape
    n = w.shape[0]
    assert w.shape[1] == k
    bt, bn = _matmul_blocks(t, k, n, a.dtype.itemsize, jnp.dtype(out_dtype).itemsize)

    def body(a_ref, w_ref, o_ref):
        o_ref[...] = _dot_nt(a_ref[...], w_ref[...]).astype(o_ref.dtype)

    return pl.pallas_call(
        body, name=name, grid=(t // bt, n // bn),
        in_specs=[pl.BlockSpec((bt, k), lambda i, j: (i, 0)), pl.BlockSpec((bn, k), lambda i, j: (j, 0))],
        out_specs=pl.BlockSpec((bt, bn), lambda i, j: (i, j)),
        out_shape=jax.ShapeDtypeStruct((t, n), out_dtype), compiler_params=_params(2),
    )(a, w)


def _matmul_tn(a, b, *, name):
    t, m = a.shape
    n = b.shape[1]
    bt = 2048 if t % 2048 == 0 else ROW_BLOCK
    bm = _divisor_block(m, 1408)
    bn = _divisor_block(n, 1408)
    steps = t // bt

    def body(a_ref, b_ref, o_ref, acc):
        step = pl.program_id(2)

        @pl.when(step == 0)
        def _():
            acc[...] = jnp.zeros_like(acc)

        acc[...] += _dot_tn(a_ref[...], b_ref[...])

        @pl.when(step == steps - 1)
        def _():
            o_ref[...] = acc[...].astype(o_ref.dtype)

    return pl.pallas_call(
        body, name=name, grid=(m // bm, n // bn, steps),
        in_specs=[pl.BlockSpec((bt, bm), lambda i, j, s: (s, i)), pl.BlockSpec((bt, bn), lambda i, j, s: (s, j))],
        out_specs=pl.BlockSpec((bm, bn), lambda i, j, s: (i, j)),
        out_shape=jax.ShapeDtypeStruct((m, n), BF16), scratch_shapes=[pltpu.VMEM((bm, bn), F32)],
        compiler_params=_params(3),
    )(a, b)


def _rmsnorm_fwd(x, g, *, group, name, hosted=None):
    t, w = x.shape
    bt = ROW_BLOCK

    def body(x_ref, g_ref, o_ref):
        for s in range(0, w, group):
            xs = x_ref[:, s:s + group].astype(F32)
            r = lax.rsqrt(jnp.mean(xs * xs, axis=-1, keepdims=True) + EPS)
            o_ref[:, s:s + group] = (xs * r * g_ref[:, s:s + group]).astype(o_ref.dtype)

    grid = (t // bt,)
    h_specs, h_shapes, h_args, h_scratch = _hosted_parts(hosted)
    res = pl.pallas_call(
        _host(body, 2, 1, 0, hosted, grid), name=name, grid=grid,
        in_specs=[pl.BlockSpec((bt, w), lambda i: (i, 0)), pl.BlockSpec((1, w), lambda i: (0, 0))] + h_specs,
        out_specs=tuple([pl.BlockSpec((bt, w), lambda i: (i, 0))] + h_specs),
        out_shape=tuple([jax.ShapeDtypeStruct((t, w), BF16)] + h_shapes),
        scratch_shapes=h_scratch, compiler_params=_params(1),
    )(x, g, *h_args)
    return res if hosted else res[0]


def _norm_input_grad(terms, x, g, *, group, name, out_dtypes, resid=None, hosted=None, k_chunks=1):
    t, w = x.shape
    n_terms = len(terms)
    kc = [a.shape[1] // k_chunks for a, _, _ in terms]
    per_row = sum(c * a.dtype.itemsize for c, (a, _, _) in zip(kc, terms)) + w * (x.dtype.itemsize + 4 * (resid is not None))
    per_row += w * sum(jnp.dtype(dt).itemsize for dt in out_dtypes)
    fixed = 2 * sum(w * c * 2 for c in kc)
    bt = next(b for b in (1024, 512, 256, 128)
              if t % b == 0 and fixed + 2 * b * per_row + 5 * b * w * 4 <= MATMUL_VMEM_BUDGET)
    n_in = 2 * n_terms + 2 + (resid is not None)
    grid = (t // bt, k_chunks)

    def body(*refs):
        x_ref, g_ref = refs[2 * n_terms], refs[2 * n_terms + 1]
        dx_refs, dg_ref, dh_ref = refs[n_in:-2], refs[-2], refs[-1]
        chunk = pl.program_id(1)

        @pl.when((pl.program_id(0) == 0) & (chunk == 0))
        def _():
            dg_ref[...] = jnp.zeros_like(dg_ref)

        part = None
        for k in range(n_terms):
            if terms[k][2]:
                term = jnp.dot(refs[2 * k][...], refs[2 * k + 1][...], preferred_element_type=F32)
            else:
                term = _dot_nt(refs[2 * k][...], refs[2 * k + 1][...])
            part = term if part is None else part + term

        @pl.when(chunk == 0)
        def _():
            dh_ref[...] = part

        @pl.when(chunk > 0)
        def _():
            dh_ref[...] += part

        @pl.when(chunk == k_chunks - 1)
        def _():
            for s in range(0, w, group):
                xs = x_ref[:, s:s + group].astype(F32)
                dhs = dh_ref[:, s:s + group]
                r = lax.rsqrt(jnp.mean(xs * xs, axis=-1, keepdims=True) + EPS)
                xh = xs * r
                dg_ref[:, s:s + group] += jnp.sum(dhs * xh, axis=0, keepdims=True)
                dxh = dhs * g_ref[:, s:s + group]
                dx = r * (dxh - xh * jnp.mean(dxh * xh, axis=-1, keepdims=True))
                if resid is not None:
                    dx = refs[n_in - 1][:, s:s + group] + dx
                for dx_ref in dx_refs:
                    dx_ref[:, s:s + group] = dx.astype(dx_ref.dtype)

    row = pl.BlockSpec((bt, w), lambda i, k: (i, 0))
    vec = pl.BlockSpec((1, w), lambda i, k: (0, 0))
    in_specs, args = [], []
    for c, (a, wt, w_is_kn) in zip(kc, terms):
        assert wt.shape == ((a.shape[1], w) if w_is_kn else (w, a.shape[1]))
        w_spec = pl.BlockSpec((c, w), lambda i, k: (k, 0)) if w_is_kn else pl.BlockSpec((w, c), lambda i, k: (0, k))
        in_specs += [pl.BlockSpec((bt, c), lambda i, k: (i, k)), w_spec]
        args += [a, wt]
    in_specs += [row, vec] + ([row] if resid is not None else [])
    args += [x, g] + ([resid] if resid is not None else [])
    h_specs, h_shapes, h_args, h_scratch = _hosted_parts(hosted)
    return pl.pallas_call(
        _host(body, n_in, len(out_dtypes) + 1, 1, hosted, grid), name=name, grid=grid, in_specs=in_specs + h_specs,
        out_specs=tuple([row] * len(out_dtypes) + [vec] + h_specs),
        out_shape=tuple([jax.ShapeDtypeStruct((t, w), dt) for dt in out_dtypes] + [jax.ShapeDtypeStruct((1, w), F32)]
                        + h_shapes),
        scratch_shapes=[pltpu.VMEM((bt, w), F32)] + h_scratch, compiler_params=_params(2),
    )(*args, *h_args)


def _tile_plan(tile):
    is_q = tile < N_PAIRS
    is_dil = (tile % N_PAIRS) >= N_PAIRS // 2
    return is_q, is_dil, (0 if is_q else 2) + (1 if is_dil else 0)


def _segment_ones():
    lane = np.arange(LANES)
    return jnp.asarray((lane[:, None] // HEAD_DIM) == (lane[None, :] // HEAD_DIM), BF16)


def _rope_tables(seq):
    inv_freq = jnp.power(jnp.float32(ROPE_THETA), -jnp.arange(ROPE_HALF, dtype=F32) * 2.0 / ROPE_DIM)
    ang = jnp.arange(seq).astype(F32)[:, None] * inv_freq[None, :]
    cos, sin = jnp.cos(ang), jnp.sin(ang)
    ones = jnp.ones((seq, HEAD_DIM - ROPE_DIM), F32)
    zeros = jnp.zeros((seq, HEAD_DIM - ROPE_DIM), F32)
    zh = jnp.zeros((seq, ROPE_HALF), F32)
    cos_t = jnp.concatenate([cos, cos, ones], axis=1)
    sin_a = jnp.concatenate([-sin, zh, zeros], axis=1)
    sin_b = jnp.concatenate([zh, sin, zeros], axis=1)
    return tuple(jnp.tile(tab, (1, 2)) for tab in (cos_t, sin_a, sin_b))


def _log_sigmoid(z):
    return jnp.minimum(z, 0.0) - jnp.log1p(jnp.exp(-jnp.abs(z)))


def _aug_placement():
    place = np.zeros((N_PAIRS, LANES, LANES), np.float32)
    for is_k in range(2):
        for pair in range(N_PAIRS // 2):
            for e in range(2):
                other = HEAD_DIM * (1 - e)
                ones_at = other + (AUG_C if is_k else AUG_ONE)
                c_at = other + (AUG_ONE if is_k else AUG_C)
                for n in range(3):
                    place[4 * is_k + pair, N_HEADS_FOX * n + 2 * pair + e, c_at + n] = -1.0 if is_k else 1.0
                    place[4 * is_k + pair, 3 * N_HEADS_FOX, ones_at + n] = 1.0
    return jnp.asarray(place, BF16)


def _qk_prep_fwd(proj, h1, w_fa_t, b_pad, gains, rope, seq):
    t = proj.shape[0]
    bt = ROW_BLOCK
    nsb = seq // bt
    seg = _segment_ones()
    rr = np.arange(bt)
    tri = jnp.asarray(rr[:, None] <= rr[None, :], BF16)

    def body(p_ref, h_ref, wfa_ref, b_ref, g_ref, cos_ref, sa_ref, sb_ref, seg_ref, tri_ref, place_ref,
             qk_ref, fa_ref, carry):
        @pl.when(pl.program_id(0) % nsb == 0)
        def _():
            carry[...] = jnp.zeros_like(carry)

        lane = lax.broadcasted_iota(jnp.int32, (bt, LANES), 1)
        fa = _dot_nt(h_ref[...], wfa_ref[...])
        fa_ref[...] = fa
        logf = jnp.where(lane < N_HEADS_FOX, _log_sigmoid(fa + b_ref[...]), 0.0)
        c_rows = _split_dot(logf.T[0:N_HEADS_FOX, :], tri_ref[...], 3) + carry[:, 0:1]
        carry[...] = jnp.broadcast_to(c_rows[:, bt - 1:bt], carry.shape)
        cblk = jnp.concatenate([c_rows, jnp.zeros((LANES - N_HEADS_FOX, bt), F32)], axis=0).T
        packed = jnp.where(lane == 3 * N_HEADS_FOX, 1.0, 0.0)
        rest = cblk * LOG2E
        for n in range(3):
            term = rest.astype(BF16).astype(F32)
            packed = packed + (pltpu.roll(term, N_HEADS_FOX * n, 1) if n else term)
            rest = rest - term
        packed = packed.astype(BF16)
        low = lane < HEAD_DIM

        for tile in range(2 * N_PAIRS):
            is_q, is_dil, grow = _tile_plan(tile)
            pair = tile % N_PAIRS
            xs = p_ref[:, tile * LANES:(tile + 1) * LANES].astype(F32)
            r = lax.rsqrt(_split_dot(xs * xs, seg_ref[...], 2) * (1.0 / HEAD_DIM) + EPS)
            yv = xs * r * g_ref[grow:grow + 1, :]
            if is_dil:
                yv = (yv * cos_ref[...] + pltpu.roll(yv, LANES - ROPE_HALF, 1) * sa_ref[...]
                      + pltpu.roll(yv, ROPE_HALF, 1) * sb_ref[...])
                aug = jnp.zeros((bt, LANES), F32)
            else:
                aug = jnp.dot(packed, place_ref[(0 if is_q else N_PAIRS // 2) + pair], preferred_element_type=F32)
            if is_q:
                yv = yv * (HEAD_DIM ** -0.5 * LOG2E)
            dst = ((0 if is_q else N_HEADS) + 2 * pair) * LANES
            qk_ref[:, dst:dst + LANES] = jnp.where(low, yv, aug).astype(BF16)
            qk_ref[:, dst + LANES:dst + 2 * LANES] = jnp.where(low, aug, yv).astype(BF16)

    row128 = pl.BlockSpec((bt, LANES), lambda i: (i, 0))
    rope_spec = pl.BlockSpec((bt, LANES), lambda i: (i % nsb, 0))
    const = lambda shape: pl.BlockSpec(shape, lambda i: (0,) * len(shape))
    return pl.pallas_call(
        body, name="qk_prep_fwd", grid=(t // bt,),
        in_specs=[pl.BlockSpec((bt, 2 * D_MODEL), lambda i: (i, 0)), pl.BlockSpec((bt, D_MODEL), lambda i: (i, 0)),
                  const((LANES, D_MODEL)), const((1, LANES)), const((8, LANES)), rope_spec, rope_spec, rope_spec,
                  const((LANES, LANES)), const((bt, bt)), const((N_PAIRS, LANES, LANES))],
        out_specs=(pl.BlockSpec((bt, 2 * N_HEADS * LANES), lambda i: (i, 0)), row128),
        out_shape=(jax.ShapeDtypeStruct((t, 2 * N_HEADS * LANES), BF16), jax.ShapeDtypeStruct((t, LANES), F32)),
        scratch_shapes=[pltpu.VMEM((8, LANES), F32)], compiler_params=_params(1),
    )(proj, h1, w_fa_t, b_pad, gains, *rope, seg, tri, _aug_placement())


def _qk_prep_bwd(dq, dk, dqx, dkx, dv, proj, fa, b_pad, gains, rope, seq):
    t = proj.shape[0]
    bt = ROW_BLOCK
    nsb = seq // bt
    nblk = t // bt
    seg = _segment_ones()
    rr = np.arange(bt)
    triu = jnp.asarray(rr[:, None] >= rr[None, :], BF16)

    def body(dq_ref, dk_ref, dqx_ref, dkx_ref, dv_ref, p_ref, fa_ref, b_ref, g_ref, cos_ref, sa_ref, sb_ref, seg_ref,
             triu_ref, dp_ref, dg_ref, db_ref, carry):
        step = pl.program_id(0)

        @pl.when(step == 0)
        def _():
            dg_ref[...] = jnp.zeros_like(dg_ref)
            db_ref[...] = jnp.zeros_like(db_ref)

        @pl.when(step % nsb == 0)
        def _():
            carry[...] = jnp.zeros_like(carry)

        for tile in range(2 * N_PAIRS):
            is_q, is_dil, grow = _tile_plan(tile)
            cols = slice(tile * LANES, (tile + 1) * LANES)
            src = dq_ref if is_q else dk_ref
            half = slice((tile % N_PAIRS) * LANES, (tile % N_PAIRS + 1) * LANES)
            dy = src[:, half]
            dy = dy * (HEAD_DIM ** -0.5 if is_q else LN2)
            if is_dil:
                dy = (dy * cos_ref[...] + pltpu.roll(dy * sa_ref[...], ROPE_HALF, 1)
                      + pltpu.roll(dy * sb_ref[...], LANES - ROPE_HALF, 1))
            xs = p_ref[:, cols].astype(F32)
            r = lax.rsqrt(_split_dot(xs * xs, seg_ref[...], 2) * (1.0 / HEAD_DIM) + EPS)
            xh = xs * r
            dg_ref[tile:tile + 1, :] += jnp.sum(dy * xh, axis=0, keepdims=True)
            dxh = dy * g_ref[grow:grow + 1, :]
            seg_mean = _split_dot(dxh * xh, seg_ref[...], 2) * (1.0 / HEAD_DIM)
            dp_ref[:, cols] = (r * (dxh - xh * seg_mean)).astype(BF16)

        lane = lax.broadcasted_iota(jnp.int32, (bt, LANES), 1)
        dc = jnp.zeros((bt, LANES), F32)
        for h in range(N_HEADS_FOX):
            other = (h // 2) * LANES + HEAD_DIM * (1 - h % 2)
            row_sum = dqx_ref[:, other + AUG_C:other + AUG_C + 1]
            col_sum = dkx_ref[:, other + AUG_ONE:other + AUG_ONE + 1]
            dc = jnp.where(lane == h, row_sum - col_sum, dc)
        d_rows = _split_dot(dc.T[0:N_HEADS_FOX, :], triu_ref[...], 3) + carry[:, 0:1]
        carry[...] = jnp.broadcast_to(d_rows[:, 0:1], carry.shape)
        dlogf = jnp.concatenate([d_rows, jnp.zeros((LANES - N_HEADS_FOX, bt), F32)], axis=0).T
        z = fa_ref[...] + b_ref[...]
        dfa = dlogf * (1.0 / (1.0 + jnp.exp(z)))
        db_ref[0:1, :] += jnp.sum(dfa, axis=0, keepdims=True)
        dp_ref[:, 2 * D_MODEL:MAIN_COLS] = dv_ref[...]
        dp_ref[:, MAIN_COLS:PROJ_COLS] = dfa.astype(BF16)

    rev = lambda i: nblk - 1 - i
    row = lambda w: pl.BlockSpec((bt, w), lambda i: (rev(i), 0))
    rope_spec = pl.BlockSpec((bt, LANES), lambda i: (rev(i) % nsb, 0))
    const = lambda shape: pl.BlockSpec(shape, lambda i: (0, 0))
    return pl.pallas_call(
        body, name="qk_prep_bwd", grid=(nblk,),
        in_specs=[row(D_MODEL), row(D_MODEL), row(W_GROUP), row(W_GROUP), row(D_MODEL), row(2 * D_MODEL), row(LANES),
                  const((1, LANES)), const((8, LANES)), rope_spec, rope_spec, rope_spec, const((LANES, LANES)),
                  const((bt, bt))],
        out_specs=(row(PROJ_COLS), const((2 * N_PAIRS, LANES)), const((8, LANES))),
        out_shape=(jax.ShapeDtypeStruct((t, PROJ_COLS), BF16),
                   jax.ShapeDtypeStruct((2 * N_PAIRS, LANES), F32), jax.ShapeDtypeStruct((8, LANES), F32)),
        scratch_shapes=[pltpu.VMEM((8, LANES), F32)], compiler_params=_params(1),
    )(dq, dk, dqx, dkx, dv, proj, fa, b_pad, gains, *rope, seg, triu)


def _bias_tables(seq, keys_first):
    nb = seq // ATT_BLOCK
    idx = np.arange(ATT_BLOCK)
    q_idx, k_idx = (idx[None, None, :], idx[None, :, None]) if keys_first else (idx[None, :, None], idx[None, None, :])
    dist = np.arange(nb)[:, None, None] * ATT_BLOCK + q_idx - k_idx
    causal = dist >= 0
    count = np.zeros(dist.shape, np.int32)
    for window, dilation in DILATION_PAIRS:
        count = count + (causal & (dist % dilation == 0) & (dist <= window))
    fox = np.where(causal, 0.0, NEG)
    dil = np.where(count == 3, math.log2(3.0), np.where(count == 2, 1.0, np.where(count == 1, 0.0, NEG)))
    return jnp.asarray(np.stack([fox, dil], axis=0), F32)


def _attn_specs(seq):
    nb = seq // ATT_BLOCK
    col = lambda off: pl.BlockSpec((seq, LANES), lambda b, j: (b, off + j))
    heads = lambda off: pl.BlockSpec((seq, 2 * LANES), lambda b, j: (b, off + j))
    table_spec = pl.BlockSpec((1, nb, ATT_BLOCK, ATT_BLOCK), lambda b, j: (j // (N_PAIRS // 2), 0, 0, 0))
    return col, heads, table_spec


def _head_lanes(e, shape, axis):
    pos = lax.broadcasted_iota(jnp.int32, shape, axis)
    return pos < HEAD_DIM if e == 0 else pos >= HEAD_DIM


def _attn_fwd(qk, proj, tables, seq, hosted=None):
    t = qk.shape[0]
    nb = seq // ATT_BLOCK
    blk = ATT_BLOCK

    def body(q_ref, k_ref, v_ref, tab_ref, o_ref, lse_ref):
        mine = [_head_lanes(e, (seq, LANES), 1) for e in range(2)]
        lane = lax.broadcasted_iota(jnp.int32, (seq, LANES), 1)
        v_aug = [jnp.where(mine[e], v_ref[...], (lane == HEAD_DIM * (1 - e)).astype(BF16)) for e in range(2)]
        def scores(i, e):
            heads_e = slice(e * LANES, (e + 1) * LANES)
            s = _dot_nt(q_ref[i * blk:(i + 1) * blk, heads_e], k_ref[0:(i + 1) * blk, heads_e])
            s = jnp.concatenate([s[:, jj * blk:(jj + 1) * blk] + tab_ref[0, i - jj] for jj in range(i + 1)], axis=1)
            return s, jnp.max(s, axis=1, keepdims=True)

        chains = [(i, e) for i in range(nb) for e in range(2)]
        pending = scores(*chains[0])
        done = {}
        for n, (i, e) in enumerate(chains):
            s, m = pending
            if n + 1 < len(chains):
                pending = scores(*chains[n + 1])
            acc = jnp.dot(jnp.exp2(s - m).astype(BF16), v_aug[e][0:(i + 1) * blk], preferred_element_type=F32)
            ones_at = HEAD_DIM * (1 - e)
            l = acc[:, ones_at:ones_at + 1]
            done[e] = (acc / l, m + jnp.log2(l))
            if e == 1:
                rows = slice(i * blk, (i + 1) * blk)
                o_ref[rows, :] = jnp.where(mine[0][rows], done[0][0], done[1][0]).astype(o_ref.dtype)
                lse_ref[rows, :] = jnp.where(mine[0][rows], done[0][1], done[1][1])

    col, heads, table_spec = _attn_specs(seq)
    grid = (t // seq, N_PAIRS)
    h_specs, h_shapes, h_args, h_scratch = _hosted_parts(hosted)
    return pl.pallas_call(
        _host(body, 4, 2, 0, hosted, grid), name="attn_fwd", grid=grid,
        in_specs=[heads(0), heads(N_PAIRS), col(2 * N_PAIRS), table_spec] + h_specs,
        out_specs=tuple([col(0), col(0)] + h_specs),
        out_shape=tuple([jax.ShapeDtypeStruct((t, D_MODEL), BF16), jax.ShapeDtypeStruct((t, D_MODEL), F32)] + h_shapes),
        scratch_shapes=h_scratch, compiler_params=_params(2),
    )(qk, qk, proj, tables, *h_args)


def _attn_bwd(qk, proj, tables, o, lse, do, seq, hosted=None):
    t = qk.shape[0]
    nb = seq // ATT_BLOCK
    blk = ATT_BLOCK
    group = math.gcd(nb, ATT_GROUP)

    def body(q_ref, k_ref, v_ref, tab_ref, o_ref, lse_ref, do_ref,
             dq_ref, dk_ref, dv_ref, dqx_ref, dkx_ref, dk_acc, dv_acc):
        mine = [_head_lanes(e, (blk, LANES), 1) for e in range(2)]
        top = _head_lanes(0, (LANES, blk), 0)
        head_rows = lax.broadcasted_iota(jnp.int32, (8, LANES), 0)
        head_of_lane = lax.broadcasted_iota(jnp.int32, (8, LANES), 1) // HEAD_DIM
        head_sel = (head_rows == head_of_lane).astype(BF16)
        dk_acc[...] = jnp.zeros_like(dk_acc)
        dv_acc[...] = jnp.zeros_like(dv_acc)

        def block_rows(i):
            return pl.ds(pl.multiple_of(i * blk, blk), blk)

        def q_group(g, _):
            base = g * group
            qs, doe, delta, lse_e = [], [], [], []
            for b in range(group):
                rows = block_rows(base + b)
                qs.append([q_ref[rows, e * LANES:(e + 1) * LANES] for e in range(2)])
                do_blk = do_ref[rows, :]
                doe.append([jnp.where(mine[e], do_blk, jnp.zeros_like(do_blk)) for e in range(2)])
                delta_t = _split_dot_nt(head_sel, do_blk.astype(F32) * o_ref[rows, :].astype(F32), 3)
                lse_t = _split_dot_nt(head_sel, lse_ref[rows, :], 3) * (1.0 / HEAD_DIM)
                delta.append([delta_t[e:e + 1, :] for e in range(2)])
                lse_e.append([lse_t[e:e + 1, :] for e in range(2)])

            def key_block(dq_t, jj, members):
                krows = block_rows(jj)
                v = v_ref[krows, :]
                dq_t = [list(d) for d in dq_t]
                lo, hi = slice(0, blk // 2), slice(blk // 2, blk)
                dv_part = [None, None]
                add = lambda acc, part: part if acc is None else acc + part

                def probs(k_sub, v_sub, keys, queries, b, e, dist):
                    q_sub, do_sub = qs[b][e][queries], doe[b][e][queries]
                    p_t = jnp.exp2(_dot_nt(k_sub, q_sub) + tab_ref[0, dist, keys, queries] - lse_e[b][e][:, queries])
                    ds_t = (p_t * (_dot_nt(v_sub, do_sub) - delta[b][e][:, queries])).astype(BF16)
                    return p_t.astype(BF16), ds_t, q_sub, do_sub, k_sub

                def outputs(tile):
                    p_t, ds_t, q_sub, do_sub, k_sub = tile
                    return (jnp.dot(p_t, do_sub, preferred_element_type=F32),
                            jnp.dot(ds_t, q_sub, preferred_element_type=F32), _dot_tn(k_sub, ds_t))

                for e in range(2):
                    k_e = k_ref[krows, e * LANES:(e + 1) * LANES]
                    dk_part = [None, None]
                    tiles = []
                    for b, dist in members:
                        if isinstance(dist, int) and dist == 0:
                            tiles.append((b, probs(k_e[lo], v[lo], lo, slice(0, blk), b, e, dist),
                                          probs(k_e[hi], v[hi], hi, hi, b, e, dist)))
                        else:
                            tiles.append((b, probs(k_e, v, slice(0, blk), slice(0, blk), b, e, dist), None))
                    for b, first, second in tiles:
                        if second is not None:
                            dv_a, dk_a, dq_a = outputs(first)
                            dv_b, dk_b, dq_b = outputs(second)
                            halves = ((dv_a, dk_a), (dv_b, dk_b))
                            dq = jnp.concatenate([dq_a[:, lo], dq_a[:, hi] + dq_b], axis=1)
                        else:
                            dv_f, dk_f, dq = outputs(first)
                            halves = ((dv_f[lo], dk_f[lo]), (dv_f[hi], dk_f[hi]))
                        for n, (dv_h, dk_h) in enumerate(halves):
                            dv_part[n] = add(dv_part[n], dv_h)
                            dk_part[n] = add(dk_part[n], dk_h)
                        dq_t[b][e] = dq_t[b][e] + dq
                    dk_acc[e, krows, :] += jnp.concatenate(dk_part, axis=0)
                dv_acc[krows, :] += jnp.concatenate(dv_part, axis=0)
                return tuple(tuple(d) for d in dq_t)

            zacc = jnp.zeros((LANES, blk), F32)
            dq_t = tuple((zacc, zacc) for _ in range(group))
            dq_t = lax.fori_loop(
                0, base, lambda jj, st: key_block(st, jj, [(b, base + b - jj) for b in range(group)]), dq_t)
            for a in range(group):
                dq_t = key_block(dq_t, base + a, [(b, b - a) for b in range(a, group)])
            for b in range(group):
                rows = block_rows(base + b)
                dq_ref[rows, :] = jnp.where(top, dq_t[b][0], dq_t[b][1]).T
                dqx_ref[rows, :] = jnp.where(top, dq_t[b][1], dq_t[b][0]).T
            return 0

        lax.fori_loop(0, nb // group, q_group, 0)
        lo = _head_lanes(0, (seq, LANES), 1)
        dk_ref[...] = jnp.where(lo, dk_acc[0], dk_acc[1])
        dkx_ref[...] = jnp.where(lo, dk_acc[1], dk_acc[0])
        dv_ref[...] = dv_acc[...].astype(dv_ref.dtype)

    col, heads, table_spec = _attn_specs(seq)
    grid = (t // seq, N_PAIRS)
    h_specs, h_shapes, h_args, h_scratch = _hosted_parts(hosted)
    f32_out = jax.ShapeDtypeStruct((t, D_MODEL), F32)
    return pl.pallas_call(
        _host(body, 7, 5, 2, hosted, grid), name="attn_bwd", grid=grid,
        in_specs=[heads(0), heads(N_PAIRS), col(2 * N_PAIRS), table_spec, col(0), col(0), col(0)] + h_specs,
        out_specs=tuple([col(0)] * 5 + h_specs),
        out_shape=tuple([f32_out, f32_out, jax.ShapeDtypeStruct((t, D_MODEL), BF16), f32_out, f32_out] + h_shapes),
        scratch_shapes=[pltpu.VMEM((2, seq, LANES), F32), pltpu.VMEM((seq, LANES), F32)] + h_scratch,
        compiler_params=_params(2),
    )(qk, qk, proj, tables, o, lse, do, *h_args)


def _row_block(t):
    return 1024 if t % 1024 == 0 else ROW_BLOCK


def _out_proj_ffn_norm(o, g_out, w_out, x, g_ffn):
    t = o.shape[0]
    bt = _row_block(t)

    def body(o_ref, go_ref, w_ref, x_ref, gf_ref, on_ref, x2_ref, h2_ref):
        for s in range(0, D_MODEL, W_GROUP):
            os_ = o_ref[:, s:s + W_GROUP].astype(F32)
            r = lax.rsqrt(jnp.mean(os_ * os_, axis=-1, keepdims=True) + EPS)
            on_ref[:, s:s + W_GROUP] = (os_ * r * go_ref[:, s:s + W_GROUP]).astype(BF16)
        x2 = x_ref[...] + jnp.dot(on_ref[...], w_ref[...], preferred_element_type=F32)
        x2_ref[...] = x2
        r2 = lax.rsqrt(jnp.mean(x2 * x2, axis=-1, keepdims=True) + EPS)
        h2_ref[...] = (x2 * r2 * gf_ref[...]).astype(BF16)

    row = pl.BlockSpec((bt, D_MODEL), lambda i: (i, 0))
    vec = pl.BlockSpec((1, D_MODEL), lambda i: (0, 0))
    return pl.pallas_call(
        body, name="out_proj", grid=(t // bt,),
        in_specs=[row, vec, pl.BlockSpec((D_MODEL, D_MODEL), lambda i: (0, 0)), row, vec],
        out_specs=(row, row, row),
        out_shape=(jax.ShapeDtypeStruct((t, D_MODEL), BF16), jax.ShapeDtypeStruct((t, D_MODEL), F32),
                   jax.ShapeDtypeStruct((t, D_MODEL), BF16)),
        compiler_params=_params(1),
    )(o, g_out, w_out, x, g_ffn)


def _ffn_gate_up(h2, w_gate_t, w_up_t):
    t = h2.shape[0]
    bt = _row_block(t)
    bn = _divisor_block(D_FF, 1408)

    def body(h_ref, wg_ref, wu_ref, a_ref, u_ref, f_ref):
        a = _dot_nt(h_ref[...], wg_ref[...])
        u = _dot_nt(h_ref[...], wu_ref[...])
        a_ref[...] = a.astype(BF16)
        u_ref[...] = u.astype(BF16)
        f_ref[...] = (a * jax.nn.sigmoid(a) * u).astype(BF16)

    blk = pl.BlockSpec((bt, bn), lambda i, j: (i, j))
    w_blk = pl.BlockSpec((bn, D_MODEL), lambda i, j: (j, 0))
    shape = jax.ShapeDtypeStruct((t, D_FF), BF16)
    return pl.pallas_call(
        body, name="ffn_gate_up", grid=(t // bt, D_FF // bn),
        in_specs=[pl.BlockSpec((bt, D_MODEL), lambda i, j: (i, 0)), w_blk, w_blk],
        out_specs=(blk, blk, blk), out_shape=(shape, shape, shape), compiler_params=_params(2),
    )(h2, w_gate_t, w_up_t)


def _ffn_down_grad(dy16, w_down, a, u):
    t = a.shape[0]
    bt = _row_block(t)
    bn = _divisor_block(D_FF, 1408)

    def body(dy_ref, w_ref, a_ref, u_ref, da_ref, du_ref):
        df = _dot_nt(dy_ref[...], w_ref[...])
        av = a_ref[...].astype(F32)
        sg = jax.nn.sigmoid(av)
        da_ref[...] = (df * u_ref[...].astype(F32) * sg * (1.0 + av * (1.0 - sg))).astype(BF16)
        du_ref[...] = (df * av * sg).astype(BF16)

    blk = pl.BlockSpec((bt, bn), lambda i, j: (i, j))
    shape = jax.ShapeDtypeStruct((t, D_FF), BF16)
    return pl.pallas_call(
        body, name="d_ffn_down", grid=(t // bt, D_FF // bn),
        in_specs=[pl.BlockSpec((bt, D_MODEL), lambda i, j: (i, 0)), pl.BlockSpec((bn, D_MODEL), lambda i, j: (j, 0)),
                  blk, blk],
        out_specs=(blk, blk), out_shape=(shape, shape), compiler_params=_params(2),
    )(dy16, w_down, a, u)


def _ffn_down_loss(f, w_down, x2, target):
    t, w = x2.shape
    bt = _row_block(t)

    def body(f_ref, w_ref, x_ref, t_ref, dy_ref, dy16_ref, loss_ref):
        @pl.when(pl.program_id(0) == 0)
        def _():
            loss_ref[...] = jnp.zeros_like(loss_ref)

        err = (x_ref[...] + jnp.dot(f_ref[...], w_ref[...], preferred_element_type=F32)) - t_ref[...]
        dy = err * (1.0 / w)
        dy_ref[...] = dy
        dy16_ref[...] = dy.astype(BF16)
        loss_ref[...] += 0.5 * jnp.sum(jnp.mean(err * err, axis=-1, keepdims=True), axis=0, keepdims=True)

    row = pl.BlockSpec((bt, w), lambda i: (i, 0))
    return pl.pallas_call(
        body, name="ffn_down_loss", grid=(t // bt,),
        in_specs=[pl.BlockSpec((bt, D_FF), lambda i: (i, 0)), pl.BlockSpec((D_FF, w), lambda i: (0, 0)), row, row],
        out_specs=(row, row, pl.BlockSpec((8, LANES), lambda i: (0, 0))),
        out_shape=(jax.ShapeDtypeStruct((t, w), F32), jax.ShapeDtypeStruct((t, w), BF16),
                   jax.ShapeDtypeStruct((8, LANES), F32)),
        compiler_params=_params(1),
    )(f, w_down, x2, target)


def _adamw(parts, w, m, v, *, name):
    _, rows, cols = w.shape
    br = rows if rows <= 512 else 256
    assert rows % br == 0

    def body(p_ref, w_ref, m_ref, v_ref, g_ref, d_ref, nm_ref, nv_ref):
        g = p_ref[0].astype(F32)
        for r in range(1, N_DEV):
            g = g + p_ref[r].astype(F32)
        m2 = ADAM_B1 * m_ref[0] + (1.0 - ADAM_B1) * g
        v2 = ADAM_B2 * v_ref[0] + (1.0 - ADAM_B2) * jnp.square(g)
        m_hat = m2 / (1.0 - ADAM_B1 ** ADAM_STEP)
        v_hat = v2 / (1.0 - ADAM_B2 ** ADAM_STEP)
        g_ref[0] = g
        d_ref[0] = -ADAM_LR * (m_hat / (jnp.sqrt(v_hat) + ADAM_EPS) + ADAM_WD * w_ref[0])
        nm_ref[0] = m2
        nv_ref[0] = v2

    blk = pl.BlockSpec((1, br, cols), lambda i: (0, i, 0))
    shape = jax.ShapeDtypeStruct((1, rows, cols), F32)
    return pl.pallas_call(
        body, name=name, grid=(rows // br,),
        in_specs=[pl.BlockSpec((N_DEV, br, cols), lambda i: (0, i, 0)), blk, blk, blk],
        out_specs=(blk, blk, blk, blk), out_shape=(shape, shape, shape, shape), compiler_params=_params(1),
    )(parts, w, m, v)


_QA, _KA, _VA, _FA, _QD, _KD, _VD = (0, 512), (512, 1024), (1024, 1536), (1536, 1544), (1544, 2056), (2056, 2568), (2568, 3080)
_MAIN_ORDER = (_QA, _QD, _KA, _KD, _VA, _VD)
MAIN_COLS = 3 * D_MODEL
PROJ_COLS = MAIN_COLS + LANES
COL_SHARDED = ("w_in", "w_gate", "w_up")


def _swap(w):
    return jnp.transpose(w, (0, 2, 1))


def _w_in_to_kernel(w_t):
    main = jnp.concatenate([w_t[a:b] for a, b in _MAIN_ORDER], axis=0)
    forget = jnp.pad(w_t[_FA[0]:_FA[1]], ((0, LANES - N_HEADS_FOX), (0, 0)))
    return main, forget


def _w_in_from_kernel(g_t):
    pos = {span: i * W_GROUP for i, span in enumerate(_MAIN_ORDER)}
    parts = []
    for span in (_QA, _KA, _VA, _FA, _QD, _KD, _VD):
        if span == _FA:
            parts.append(g_t[MAIN_COLS:MAIN_COLS + N_HEADS_FOX])
        else:
            parts.append(g_t[pos[span]:pos[span] + W_GROUP])
    return jnp.concatenate(parts, axis=0)


def _pack_small(vals):
    rows = []
    for name, _, n_rows in SMALL_LAYOUT:
        flat = vals[name].reshape(-1).astype(F32)
        rows.append(jnp.pad(flat, (0, n_rows * LANES - flat.shape[0])).reshape(n_rows, LANES))
    packed = jnp.concatenate(rows, axis=0)
    return jnp.pad(packed, ((0, SMALL_ROWS - packed.shape[0]), (0, 0)))


def _unpack_small(packed, like):
    out = {}
    for name, row, n_rows in SMALL_LAYOUT:
        n = like[name].size
        out[name] = packed[row:row + n_rows].reshape(-1)[:n].reshape(like[name].shape)
    return out


def _device_step(x, target, small, shards):
    bsz, seq, _ = x.shape
    t = bsz * seq
    xf = x.reshape(t, D_MODEL)
    tf = target.reshape(t, D_MODEL)
    row = lambda v: v.reshape(1, -1)
    g_out = jnp.concatenate([small["g_out_fox"], small["g_out_dil"]]).reshape(1, D_MODEL)
    gains = jnp.concatenate(
        [jnp.tile(small[n].reshape(1, HEAD_DIM), (1, 2)) for n in ("g_q_fox", "g_q_dil", "g_k_fox", "g_k_dil")]
        + [jnp.zeros((4, LANES), F32)], axis=0)
    b_pad = jnp.pad(small["b_forget"].reshape(1, N_HEADS_FOX), ((0, 0), (0, LANES - N_HEADS_FOX)))
    rope = _rope_tables(seq)
    tables_qk = _bias_tables(seq, keys_first=False)
    tables_kq = _bias_tables(seq, keys_first=True)

    h1, g_in = _rmsnorm_fwd(xf, row(small["g_mix"]), group=D_MODEL, name="norm_mix",
                            hosted=_ChipGather([(shards["w_in"], False)]))
    w_main_t, w_fa_t = _w_in_to_kernel(g_in.reshape(IN_COLS, D_MODEL))
    w_in_all_t = jnp.concatenate([w_main_t, w_fa_t], axis=0)
    proj = _matmul_nt(h1, w_main_t, name="in_proj", out_dtype=BF16)
    qk, fa = _qk_prep_fwd(proj, h1, w_fa_t, b_pad, gains, rope, seq)
    late = _Exchange([(shards[n], False) for n in ("w_out", "w_gate", "w_up", "w_down")])
    o, lse, g_out_w, g_gate, g_up, g_down = _attn_fwd(qk, proj, tables_qk, seq, hosted=late)
    w_out = g_out_w.reshape(D_MODEL, D_MODEL)
    w_gate_t = g_gate.reshape(D_FF, D_MODEL)
    w_up_t = g_up.reshape(D_FF, D_MODEL)
    w_down = g_down.reshape(D_FF, D_MODEL)
    on, x2, h2 = _out_proj_ffn_norm(o, g_out, w_out, xf, row(small["g_ffn"]))
    a, u, f = _ffn_gate_up(h2, w_gate_t, w_up_t)
    dy, dy16, loss_tile = _ffn_down_loss(f, w_down, x2, tf)

    da, du = _ffn_down_grad(dy16, w_down, a, u)
    gw_down = _matmul_tn(f, dy16, name="gw_down")
    gw_gate_t = _matmul_tn(da, h2, name="gw_gate")
    gw_up_t = _matmul_tn(du, h2, name="gw_up")
    dx2, dx2_16, dg_ffn = _norm_input_grad([(da, w_gate_t, True), (du, w_up_t, True)], x2, row(small["g_ffn"]),
                                           group=D_MODEL, name="d_ffn_gate_up", out_dtypes=(F32, BF16), resid=dy,
                                           k_chunks=2)
    gw_out = _matmul_tn(on, dx2_16, name="gw_out")
    do, dg_out = _norm_input_grad([(dx2_16, w_out, False)], o, g_out, group=W_GROUP, name="d_out_proj",
                                  out_dtypes=(BF16,))

    shard_rows = lambda g: g.reshape(N_DEV, g.shape[0] // N_DEV, g.shape[1])
    ffn_grads = _Exchange([(shard_rows(g), True) for g in (gw_out, gw_gate_t, gw_up_t, gw_down)])
    dq, dk, dv, dqx, dkx, p_out, p_gate, p_up, p_down = _attn_bwd(qk, proj, tables_kq, o, lse, do, seq, hosted=ffn_grads)
    dproj, dgains, db = _qk_prep_bwd(dq, dk, dqx, dkx, dv, proj, fa, b_pad, gains, rope, seq)
    gw_in_t = _matmul_tn(dproj, h1, name="gw_in")
    in_grad = _Exchange([(shard_rows(_w_in_from_kernel(gw_in_t)), True)])
    dx, dg_mix, p_in = _norm_input_grad([(dproj, w_in_all_t, True)], xf, row(small["g_mix"]), group=D_MODEL,
                                        name="d_in_proj", out_dtypes=(F32,), resid=dx2, hosted=in_grad)

    fold = lambda rows: jnp.sum(rows[:, :HEAD_DIM] + rows[:, HEAD_DIM:], axis=0)
    half = N_PAIRS // 2
    gsmall = {
        "g_mix": dg_mix, "g_ffn": dg_ffn, "g_out_fox": dg_out[0, :W_GROUP], "g_out_dil": dg_out[0, W_GROUP:],
        "g_q_fox": fold(dgains[0:half]), "g_q_dil": fold(dgains[half:N_PAIRS]),
        "g_k_fox": fold(dgains[N_PAIRS:N_PAIRS + half]), "g_k_dil": fold(dgains[N_PAIRS + half:]),
        "b_forget": db[0, :N_HEADS_FOX],
    }
    packed = _pack_small(gsmall).at[LOSS_ROW].set(loss_tile[0])
    (p_small,) = _exchange("small_exchange", [(packed, False)])
    parts = {"w_in": p_in, "w_out": p_out, "w_gate": p_gate, "w_up": p_up, "w_down": p_down}
    return dx.reshape(x.shape), parts, p_small


def kernel(x, g_mix, w_in, b_forget, g_q_fox, g_k_fox, g_q_dil, g_k_dil, g_out_fox, g_out_dil, w_out, g_ffn, w_gate, w_up, w_down, loss_target, m_g_mix, m_w_in, m_b_forget, m_g_q_fox, m_g_k_fox, m_g_q_dil, m_g_k_dil, m_g_out_fox, m_g_out_dil, m_w_out, m_g_ffn, m_w_gate, m_w_up, m_w_down, v_g_mix, v_w_in, v_b_forget, v_g_q_fox, v_g_k_fox, v_g_q_dil, v_g_k_dil, v_g_out_fox, v_g_out_dil, v_w_out, v_g_ffn, v_w_gate, v_w_up, v_w_down):
    args = dict(locals())
    small_names = [name for name, _, _ in SMALL_LAYOUT]
    big_names = ["w_in", "w_out", "w_gate", "w_up", "w_down"]
    small = {n: args[n][0] for n in small_names}

    as_rows = lambda n, w: _swap(w) if n in COL_SHARDED else w
    shards = {n: as_rows(n, args[n])[0].astype(BF16) for n in big_names}
    grad_x, parts, p_small = _device_step(x, loss_target, small, shards)

    grads, deltas, new_m, new_v = {}, {}, {}, {}
    for n in big_names:
        res = _adamw(parts[n], as_rows(n, args[n]), as_rows(n, args["m_" + n]), as_rows(n, args["v_" + n]),
                     name="adamw_" + n)
        grads[n], deltas[n], new_m[n], new_v[n] = [as_rows(n, r) for r in res]
    res = _adamw(p_small, _pack_small(small)[None], _pack_small({n: args["m_" + n][0] for n in small_names})[None],
                 _pack_small({n: args["v_" + n][0] for n in small_names})[None], name="adamw_small")
    loss = res[0][0, LOSS_ROW, 0]
    for dst, packed_res in zip((grads, deltas, new_m, new_v), res):
        for n, val in _unpack_small(packed_res[0], small).items():
            dst[n] = val[None]

    order = ["g_mix", "w_in", "b_forget", "g_q_fox", "g_k_fox", "g_q_dil", "g_k_dil", "g_out_fox", "g_out_dil",
             "w_out", "g_ffn", "w_gate", "w_up", "w_down"]
    return (loss, grad_x, *[grads[n] for n in order], *[deltas[n] for n in order],
            *[new_m[n] for n in order], *[new_v[n] for n in order])
```

```python
import functools
import math

import jax
import jax.numpy as jnp
import numpy as np
from jax import lax
from jax.experimental import pallas as pl
from jax.experimental.pallas import tpu as pltpu

F32 = jnp.float32
BF16 = jnp.bfloat16

D_MODEL = 1024
HEAD_DIM = 64
LANES = 128
N_PAIRS = D_MODEL // LANES
N_HEADS = 2 * N_PAIRS
N_HEADS_FOX = 8
W_GROUP = 512
D_FF = 2816
IN_COLS = 3080
DILATION_PAIRS = ((128, 1), (512, 4), (2048, 16))
ROPE_THETA = 500000.0
ROPE_DIM = 16
ROPE_HALF = ROPE_DIM // 2
EPS = 1e-6
NEG = -1e30
LOG2E = 1.4426950408889634
LN2 = 0.6931471805599453
AUG_ONE = 0
AUG_C = 3
N_DEV = 8

ADAM_LR = 0.001
ADAM_B1 = 0.9
ADAM_B2 = 0.999
ADAM_EPS = 1e-08
ADAM_WD = 0.01
ADAM_STEP = 10

ROW_BLOCK = 512
ATT_BLOCK = 512
ATT_GROUP = 4
VMEM_LIMIT = 56 * 1024 * 1024
MATMUL_VMEM_BUDGET = 44 * 1024 * 1024

SMALL_ROWS = 32
SMALL_LAYOUT = (("g_mix", 0, 8), ("g_ffn", 8, 8), ("g_out_fox", 16, 4), ("g_out_dil", 20, 4),
                ("g_q_fox", 24, 1), ("g_k_fox", 25, 1), ("g_q_dil", 26, 1), ("g_k_dil", 27, 1),
                ("b_forget", 28, 1))
LOSS_ROW = 29


def _params(n_grid):
    return pltpu.CompilerParams(dimension_semantics=("arbitrary",) * n_grid, vmem_limit_bytes=VMEM_LIMIT)


def _divisor_block(n, cap):
    best = None
    for b in range(LANES, min(n, cap) + 1, LANES):
        if n % b == 0:
            best = b
    assert best is not None, n
    return best


def _split_dot(a, b_exact, terms):
    acc = None
    rest = a
    for _ in range(terms):
        hi = rest.astype(BF16)
        part = jnp.dot(hi, b_exact, preferred_element_type=F32)
        acc = part if acc is None else acc + part
        rest = rest - hi.astype(F32)
    return acc


def _split_dot_nt(a_exact, b, terms):
    acc = None
    rest = b
    for _ in range(terms):
        hi = rest.astype(BF16)
        part = _dot_nt(a_exact, hi)
        acc = part if acc is None else acc + part
        rest = rest - hi.astype(F32)
    return acc


def _dot_nt(a, b):
    return lax.dot_general(a, b, (((1,), (1,)), ((), ())), preferred_element_type=F32)


def _dot_tn(a, b):
    return lax.dot_general(a, b, (((0,), (0,)), ((), ())), preferred_element_type=F32)


class _Exchange:
    def __init__(self, items):
        self.items = items
        self.n = len(items)
        self.arrays = [a for a, _ in items]
        self.out_shape = [jax.ShapeDtypeStruct((N_DEV,) + tuple(a.shape[1:] if sc else a.shape), a.dtype)
                          for a, sc in items]
        self.specs = [pl.BlockSpec(memory_space=pl.ANY)] * self.n
        self.scratch = [pltpu.SemaphoreType.DMA((self.n, N_DEV - 1)), pltpu.SemaphoreType.DMA((self.n, N_DEV - 1)),
                        pltpu.SemaphoreType.DMA((self.n,))]

    def run(self, ins, outs, sems, first, last, compute):
        send_sems, recv_sems, local_sems = sems
        x, y, c = lax.axis_index("x"), lax.axis_index("y"), lax.axis_index("c")
        me = 4 * x + 2 * y + c
        local, remote = [], []
        for k, (_, scatter) in enumerate(self.items):
            own = ins[k].at[me] if scatter else ins[k]
            local.append(pltpu.make_async_copy(own, outs[k].at[me], local_sems.at[k]))
        for r in range(1, N_DEV):
            px = 1 - x if r & 4 else x
            py = 1 - y if r & 2 else y
            pc = 1 - c if r & 1 else c
            peer = 4 * px + 2 * py + pc
            for k, (_, scatter) in enumerate(self.items):
                src = ins[k].at[peer] if scatter else ins[k]
                remote.append(pltpu.make_async_remote_copy(
                    src_ref=src, dst_ref=outs[k].at[me],
                    send_sem=send_sems.at[k, r - 1], recv_sem=recv_sems.at[k, r - 1],
                    device_id=(px, py, pc), device_id_type=pl.DeviceIdType.MESH))

        def start():
            for cp in local + remote:
                cp.start()

        def finish():
            for cp in remote:
                cp.wait_recv()
            for cp in remote:
                cp.wait_send()
            for cp in local:
                cp.wait()

        _run_phases(first, last, start, compute, finish)


def _run_phases(first, last, start, compute, finish):
    if first is None:
        start()
        compute()
        finish()
    else:
        pl.when(first)(start)
        compute()
        pl.when(last)(finish)


class _ChipGather(_Exchange):
    def run(self, ins, outs, sems, first, last, compute):
        send_sems, recv_sems, local_sems = sems
        x, y, c = lax.axis_index("x"), lax.axis_index("y"), lax.axis_index("c")
        sibling = (x, y, 1 - c)
        chips = [(1 - x, y), (x, 1 - y), (1 - x, 1 - y)]
        slot = lambda px, py, pc: 4 * px + 2 * py + pc

        def copy(k, n, src, dst_slot, to):
            return pltpu.make_async_remote_copy(
                src_ref=src, dst_ref=outs[k].at[dst_slot], send_sem=send_sems.at[k, n], recv_sem=recv_sems.at[k, n],
                device_id=to, device_id_type=pl.DeviceIdType.MESH)

        local, own, passed, arrivals = [], [], [], []
        for k in range(self.n):
            me = slot(x, y, c)
            local.append(pltpu.make_async_copy(ins[k], outs[k].at[me], local_sems.at[k]))
            own.append(copy(k, 0, ins[k], me, sibling))
            arrivals.append(copy(k, 0, ins[k], slot(*sibling), sibling))
            for j, chip in enumerate(chips):
                theirs = slot(*chip, c)
                own.append(copy(k, 1 + j, ins[k], me, (*chip, c)))
                passed.append((copy(k, 1 + j, ins[k], theirs, sibling),
                               copy(k, 4 + j, outs[k].at[theirs], theirs, sibling)))
                arrivals.append(copy(k, 4 + j, ins[k], slot(*chip, 1 - c), sibling))

        def start():
            for cp in local + own:
                cp.start()

        def finish():
            for landed, onward in passed:
                landed.wait_recv()
                onward.start()
            for cp in arrivals:
                cp.wait_recv()
            for cp in own + [onward for _, onward in passed]:
                cp.wait_send()
            for cp in local:
                cp.wait()

        _run_phases(first, last, start, compute, finish)


def _grid_ends(grid):
    ids = [pl.program_id(d) for d in range(len(grid))]
    first = functools.reduce(jnp.logical_and, [i == 0 for i in ids])
    last = functools.reduce(jnp.logical_and, [i == g - 1 for i, g in zip(ids, grid)])
    return first, last


def _host(core, n_in, n_out, n_scratch, hosted, grid):
    if hosted is None:
        return core
    nh = hosted.n

    def body(*refs):
        ins, rest = refs[:n_in], refs[n_in:]
        h_ins, rest = rest[:nh], rest[nh:]
        outs, rest = rest[:n_out], rest[n_out:]
        h_outs, rest = rest[:nh], rest[nh:]
        scratch, sems = rest[:n_scratch], rest[n_scratch:]
        first, last = _grid_ends(grid)
        hosted.run(h_ins, h_outs, sems, first, last, lambda: core(*ins, *outs, *scratch))

    return body


def _hosted_parts(hosted):
    if hosted is None:
        return [], [], [], []
    return list(hosted.specs), list(hosted.out_shape), list(hosted.arrays), list(hosted.scratch)


def _exchange(name, items):
    ex = _Exchange(items)
    n = ex.n

    def body(*refs):
        ex.run(refs[:n], refs[n:2 * n], refs[2 * n:], None, None, lambda: None)

    return pl.pallas_call(
        body, name=name, out_shape=tuple(ex.out_shape), in_specs=ex.specs, out_specs=tuple(ex.specs),
        scratch_shapes=ex.scratch,
    )(*ex.arrays)


def _matmul_blocks(t, k, n, a_bytes, o_bytes):
    for bt, cap in ((1024, 1408), (1024, 512), (512, 512)):
        if t % bt:
            continue
        bn = _divisor_block(n, cap)
        if 2 * (bt * k * a_bytes + bn * k * 2 + bt * bn * o_bytes) <= MATMUL_VMEM_BUDGET:
            return bt, bn
    return ROW_BLOCK, _divisor_block(n, 256)


def _matmul_nt(a, w, *, name, out_dtype):
    t, k = a.shape
    n = w.shape[0]
    assert w.shape[1] == k
    bt, bn = _matmul_blocks(t, k, n, a.dtype.itemsize, jnp.dtype(out_dtype).itemsize)

    def body(a_ref, w_ref, o_ref):
        o_ref[...] = _dot_nt(a_ref[...], w_ref[...]).astype(o_ref.dtype)

    return pl.pallas_call(
        body, name=name, grid=(t // bt, n // bn),
        in_specs=[pl.BlockSpec((bt, k), lambda i, j: (i, 0)), pl.BlockSpec((bn, k), lambda i, j: (j, 0))],
        out_specs=pl.BlockSpec((bt, bn), lambda i, j: (i, j)),
        out_shape=jax.ShapeDtypeStruct((t, n), out_dtype), compiler_params=_params(2),
    )(a, w)


def _matmul_tn(a, b, *, name):
    t, m = a.shape
    n = b.shape[1]
    bt = 2048 if t % 2048 == 0 else ROW_BLOCK
    bm = _divisor_block(m, 1408)
    bn = _divisor_block(n, 1408)
    steps = t // bt

    def body(a_ref, b_ref, o_ref, acc):
        step = pl.program_id(2)

        @pl.when(step == 0)
        def _():
            acc[...] = jnp.zeros_like(acc)

        acc[...] += _dot_tn(a_ref[...], b_ref[...])

        @pl.when(step == steps - 1)
        def _():
            o_ref[...] = acc[...].astype(o_ref.dtype)

    return pl.pallas_call(
        body, name=name, grid=(m // bm, n // bn, steps),
        in_specs=[pl.BlockSpec((bt, bm), lambda i, j, s: (s, i)), pl.BlockSpec((bt, bn), lambda i, j, s: (s, j))],
        out_specs=pl.BlockSpec((bm, bn), lambda i, j, s: (i, j)),
        out_shape=jax.ShapeDtypeStruct((m, n), BF16), scratch_shapes=[pltpu.VMEM((bm, bn), F32)],
        compiler_params=_params(3),
    )(a, b)


def _rmsnorm_fwd(x, g, *, group, name, hosted=None):
    t, w = x.shape
    bt = ROW_BLOCK

    def body(x_ref, g_ref, o_ref):
        for s in range(0, w, group):
            xs = x_ref[:, s:s + group].astype(F32)
            r = lax.rsqrt(jnp.mean(xs * xs, axis=-1, keepdims=True) + EPS)
            o_ref[:, s:s + group] = (xs * r * g_ref[:, s:s + group]).astype(o_ref.dtype)

    grid = (t // bt,)
    h_specs, h_shapes, h_args, h_scratch = _hosted_parts(hosted)
    res = pl.pallas_call(
        _host(body, 2, 1, 0, hosted, grid), name=name, grid=grid,
        in_specs=[pl.BlockSpec((bt, w), lambda i: (i, 0)), pl.BlockSpec((1, w), lambda i: (0, 0))] + h_specs,
        out_specs=tuple([pl.BlockSpec((bt, w), lambda i: (i, 0))] + h_specs),
        out_shape=tuple([jax.ShapeDtypeStruct((t, w), BF16)] + h_shapes),
        scratch_shapes=h_scratch, compiler_params=_params(1),
    )(x, g, *h_args)
    return res if hosted else res[0]


def _norm_input_grad(terms, x, g, *, group, name, out_dtypes, resid=None, hosted=None, k_chunks=1):
    t, w = x.shape
    n_terms = len(terms)
    kc = [a.shape[1] // k_chunks for a, _, _ in terms]
    per_row = sum(c * a.dtype.itemsize for c, (a, _, _) in zip(kc, terms)) + w * (x.dtype.itemsize + 4 * (resid is not None))
    per_row += w * sum(jnp.dtype(dt).itemsize for dt in out_dtypes)
    fixed = 2 * sum(w * c * 2 for c in kc)
    bt = next(b for b in (1024, 512, 256, 128)
              if t % b == 0 and fixed + 2 * b * per_row + 5 * b * w * 4 <= MATMUL_VMEM_BUDGET)
    n_in = 2 * n_terms + 2 + (resid is not None)
    grid = (t // bt, k_chunks)

    def body(*refs):
        x_ref, g_ref = refs[2 * n_terms], refs[2 * n_terms + 1]
        dx_refs, dg_ref, dh_ref = refs[n_in:-2], refs[-2], refs[-1]
        chunk = pl.program_id(1)

        @pl.when((pl.program_id(0) == 0) & (chunk == 0))
        def _():
            dg_ref[...] = jnp.zeros_like(dg_ref)

        part = None
        for k in range(n_terms):
            if terms[k][2]:
                term = jnp.dot(refs[2 * k][...], refs[2 * k + 1][...], preferred_element_type=F32)
            else:
                term = _dot_nt(refs[2 * k][...], refs[2 * k + 1][...])
            part = term if part is None else part + term

        @pl.when(chunk == 0)
        def _():
            dh_ref[...] = part

        @pl.when(chunk > 0)
        def _():
            dh_ref[...] += part

        @pl.when(chunk == k_chunks - 1)
        def _():
            for s in range(0, w, group):
                xs = x_ref[:, s:s + group].astype(F32)
                dhs = dh_ref[:, s:s + group]
                r = lax.rsqrt(jnp.mean(xs * xs, axis=-1, keepdims=True) + EPS)
                xh = xs * r
                dg_ref[:, s:s + group] += jnp.sum(dhs * xh, axis=0, keepdims=True)
                dxh = dhs * g_ref[:, s:s + group]
                dx = r * (dxh - xh * jnp.mean(dxh * xh, axis=-1, keepdims=True))
                if resid is not None:
                    dx = refs[n_in - 1][:, s:s + group] + dx
                for dx_ref in dx_refs:
                    dx_ref[:, s:s + group] = dx.astype(dx_ref.dtype)

    row = pl.BlockSpec((bt, w), lambda i, k: (i, 0))
    vec = pl.BlockSpec((1, w), lambda i, k: (0, 0))
    in_specs, args = [], []
    for c, (a, wt, w_is_kn) in zip(kc, terms):
        assert wt.shape == ((a.shape[1], w) if w_is_kn else (w, a.shape[1]))
        w_spec = pl.BlockSpec((c, w), lambda i, k: (k, 0)) if w_is_kn else pl.BlockSpec((w, c), lambda i, k: (0, k))
        in_specs += [pl.BlockSpec((bt, c), lambda i, k: (i, k)), w_spec]
        args += [a, wt]
    in_specs += [row, vec] + ([row] if resid is not None else [])
    args += [x, g] + ([resid] if resid is not None else [])
    h_specs, h_shapes, h_args, h_scratch = _hosted_parts(hosted)
    return pl.pallas_call(
        _host(body, n_in, len(out_dtypes) + 1, 1, hosted, grid), name=name, grid=grid, in_specs=in_specs + h_specs,
        out_specs=tuple([row] * len(out_dtypes) + [vec] + h_specs),
        out_shape=tuple([jax.ShapeDtypeStruct((t, w), dt) for dt in out_dtypes] + [jax.ShapeDtypeStruct((1, w), F32)]
                        + h_shapes),
        scratch_shapes=[pltpu.VMEM((bt, w), F32)] + h_scratch, compiler_params=_params(2),
    )(*args, *h_args)


def _tile_plan(tile):
    is_q = tile < N_PAIRS
    is_dil = (tile % N_PAIRS) >= N_PAIRS // 2
    return is_q, is_dil, (0 if is_q else 2) + (1 if is_dil else 0)


def _proj_tile(kind, pair):
    half = N_PAIRS // 2
    return (pair // half) * 3 * half + kind * half + pair % half


def _segment_ones():
    lane = np.arange(LANES)
    return jnp.asarray((lane[:, None] // HEAD_DIM) == (lane[None, :] // HEAD_DIM), BF16)


def _rope_tables(seq):
    inv_freq = jnp.power(jnp.float32(ROPE_THETA), -jnp.arange(ROPE_HALF, dtype=F32) * 2.0 / ROPE_DIM)
    ang = jnp.arange(seq).astype(F32)[:, None] * inv_freq[None, :]
    cos, sin = jnp.cos(ang), jnp.sin(ang)
    ones = jnp.ones((seq, HEAD_DIM - ROPE_DIM), F32)
    zeros = jnp.zeros((seq, HEAD_DIM - ROPE_DIM), F32)
    zh = jnp.zeros((seq, ROPE_HALF), F32)
    cos_t = jnp.concatenate([cos, cos, ones], axis=1)
    sin_a = jnp.concatenate([-sin, zh, zeros], axis=1)
    sin_b = jnp.concatenate([zh, sin, zeros], axis=1)
    return tuple(jnp.tile(tab, (1, 2)) for tab in (cos_t, sin_a, sin_b))


def _log_sigmoid(z):
    return jnp.minimum(z, 0.0) - jnp.log1p(jnp.exp(-jnp.abs(z)))


def _aug_placement():
    place = np.zeros((N_PAIRS, LANES, LANES), np.float32)
    for is_k in range(2):
        for pair in range(N_PAIRS // 2):
            for e in range(2):
                other = HEAD_DIM * (1 - e)
                ones_at = other + (AUG_C if is_k else AUG_ONE)
                c_at = other + (AUG_ONE if is_k else AUG_C)
                for n in range(3):
                    place[4 * is_k + pair, N_HEADS_FOX * n + 2 * pair + e, c_at + n] = -1.0 if is_k else 1.0
                    place[4 * is_k + pair, 3 * N_HEADS_FOX, ones_at + n] = 1.0
    return jnp.asarray(place, BF16)


def _qk_prep_fwd(proj, h1, w_fa_t, b_pad, gains, rope, seq):
    t = proj.shape[0]
    bt = ROW_BLOCK
    nsb = seq // bt
    seg = _segment_ones()
    rr = np.arange(bt)
    tri = jnp.asarray(rr[:, None] <= rr[None, :], BF16)

    def body(p_ref, h_ref, wfa_ref, b_ref, g_ref, cos_ref, sa_ref, sb_ref, seg_ref, tri_ref, place_ref,
             qk_ref, fa_ref, carry):
        @pl.when(pl.program_id(0) % nsb == 0)
        def _():
            carry[...] = jnp.zeros_like(carry)

        lane = lax.broadcasted_iota(jnp.int32, (bt, LANES), 1)
        fa = _dot_nt(h_ref[...], wfa_ref[...])
        fa_ref[...] = fa
        logf = jnp.where(lane < N_HEADS_FOX, _log_sigmoid(fa + b_ref[...]), 0.0)
        c_rows = _split_dot(logf.T[0:N_HEADS_FOX, :], tri_ref[...], 3) + carry[:, 0:1]
        carry[...] = jnp.broadcast_to(c_rows[:, bt - 1:bt], carry.shape)
        cblk = jnp.concatenate([c_rows, jnp.zeros((LANES - N_HEADS_FOX, bt), F32)], axis=0).T
        packed = jnp.where(lane == 3 * N_HEADS_FOX, 1.0, 0.0)
        rest = cblk * LOG2E
        for n in range(3):
            term = rest.astype(BF16).astype(F32)
            packed = packed + (pltpu.roll(term, N_HEADS_FOX * n, 1) if n else term)
            rest = rest - term
        packed = packed.astype(BF16)
        low = lane < HEAD_DIM

        for tile in range(2 * N_PAIRS):
            is_q, is_dil, grow = _tile_plan(tile)
            pair = tile % N_PAIRS
            src = _proj_tile(0 if is_q else 1, pair) * LANES
            xs = p_ref[:, src:src + LANES].astype(F32)
            r = lax.rsqrt(_split_dot(xs * xs, seg_ref[...], 2) * (1.0 / HEAD_DIM) + EPS)
            yv = xs * r * g_ref[grow:grow + 1, :]
            if is_dil:
                yv = (yv * cos_ref[...] + pltpu.roll(yv, LANES - ROPE_HALF, 1) * sa_ref[...]
                      + pltpu.roll(yv, ROPE_HALF, 1) * sb_ref[...])
                aug = jnp.zeros((bt, LANES), F32)
            else:
                aug = jnp.dot(packed, place_ref[(0 if is_q else N_PAIRS // 2) + pair], preferred_element_type=F32)
            if is_q:
                yv = yv * (HEAD_DIM ** -0.5 * LOG2E)
            dst = ((0 if is_q else N_HEADS) + 2 * pair) * LANES
            qk_ref[:, dst:dst + LANES] = jnp.where(low, yv, aug).astype(BF16)
            qk_ref[:, dst + LANES:dst + 2 * LANES] = jnp.where(low, aug, yv).astype(BF16)

    row128 = pl.BlockSpec((bt, LANES), lambda i: (i, 0))
    rope_spec = pl.BlockSpec((bt, LANES), lambda i: (i % nsb, 0))
    const = lambda shape: pl.BlockSpec(shape, lambda i: (0,) * len(shape))
    return pl.pallas_call(
        body, name="qk_prep_fwd", grid=(t // bt,),
        in_specs=[pl.BlockSpec((bt, MAIN_COLS), lambda i: (i, 0)), pl.BlockSpec((bt, D_MODEL), lambda i: (i, 0)),
                  const((LANES, D_MODEL)), const((1, LANES)), const((8, LANES)), rope_spec, rope_spec, rope_spec,
                  const((LANES, LANES)), const((bt, bt)), const((N_PAIRS, LANES, LANES))],
        out_specs=(pl.BlockSpec((bt, 2 * N_HEADS * LANES), lambda i: (i, 0)), row128),
        out_shape=(jax.ShapeDtypeStruct((t, 2 * N_HEADS * LANES), BF16), jax.ShapeDtypeStruct((t, LANES), F32)),
        scratch_shapes=[pltpu.VMEM((8, LANES), F32)], compiler_params=_params(1),
    )(proj, h1, w_fa_t, b_pad, gains, *rope, seg, tri, _aug_placement())


def _qk_prep_bwd(dq, dk, dqx, dkx, dv, proj, fa, b_pad, gains, rope, seq):
    t = proj.shape[0]
    bt = ROW_BLOCK
    nsb = seq // bt
    nblk = t // bt
    seg = _segment_ones()
    rr = np.arange(bt)
    triu = jnp.asarray(rr[:, None] >= rr[None, :], BF16)

    def body(dq_ref, dk_ref, dqx_ref, dkx_ref, dv_ref, p_ref, fa_ref, b_ref, g_ref, cos_ref, sa_ref, sb_ref, seg_ref,
             triu_ref, dp_ref, dg_ref, db_ref, carry):
        step = pl.program_id(0)

        @pl.when(step == 0)
        def _():
            dg_ref[...] = jnp.zeros_like(dg_ref)
            db_ref[...] = jnp.zeros_like(db_ref)

        @pl.when(step % nsb == 0)
        def _():
            carry[...] = jnp.zeros_like(carry)

        for tile in range(2 * N_PAIRS):
            is_q, is_dil, grow = _tile_plan(tile)
            first = _proj_tile(0 if is_q else 1, tile % N_PAIRS) * LANES
            cols = slice(first, first + LANES)
            src = dq_ref if is_q else dk_ref
            half = slice((tile % N_PAIRS) * LANES, (tile % N_PAIRS + 1) * LANES)
            dy = src[:, half]
            dy = dy * (HEAD_DIM ** -0.5 if is_q else LN2)
            if is_dil:
                dy = (dy * cos_ref[...] + pltpu.roll(dy * sa_ref[...], ROPE_HALF, 1)
                      + pltpu.roll(dy * sb_ref[...], LANES - ROPE_HALF, 1))
            xs = p_ref[:, cols].astype(F32)
            r = lax.rsqrt(_split_dot(xs * xs, seg_ref[...], 2) * (1.0 / HEAD_DIM) + EPS)
            xh = xs * r
            dg_ref[tile:tile + 1, :] += jnp.sum(dy * xh, axis=0, keepdims=True)
            dxh = dy * g_ref[grow:grow + 1, :]
            seg_mean = _split_dot(dxh * xh, seg_ref[...], 2) * (1.0 / HEAD_DIM)
            dp_ref[:, cols] = (r * (dxh - xh * seg_mean)).astype(BF16)

        lane = lax.broadcasted_iota(jnp.int32, (bt, LANES), 1)
        dc = jnp.zeros((bt, LANES), F32)
        for h in range(N_HEADS_FOX):
            other = (h // 2) * LANES + HEAD_DIM * (1 - h % 2)
            row_sum = dqx_ref[:, other + AUG_C:other + AUG_C + 1]
            col_sum = dkx_ref[:, other + AUG_ONE:other + AUG_ONE + 1]
            dc = jnp.where(lane == h, row_sum - col_sum, dc)
        d_rows = _split_dot(dc.T[0:N_HEADS_FOX, :], triu_ref[...], 3) + carry[:, 0:1]
        carry[...] = jnp.broadcast_to(d_rows[:, 0:1], carry.shape)
        dlogf = jnp.concatenate([d_rows, jnp.zeros((LANES - N_HEADS_FOX, bt), F32)], axis=0).T
        z = fa_ref[...] + b_ref[...]
        dfa = dlogf * (1.0 / (1.0 + jnp.exp(z)))
        db_ref[0:1, :] += jnp.sum(dfa, axis=0, keepdims=True)
        for group in range(2):
            first = _proj_tile(2, group * (N_PAIRS // 2)) * LANES
            dp_ref[:, first:first + W_GROUP] = dv_ref[:, group * W_GROUP:(group + 1) * W_GROUP]
        dp_ref[:, MAIN_COLS:PROJ_COLS] = dfa.astype(BF16)

    rev = lambda i: nblk - 1 - i
    row = lambda w: pl.BlockSpec((bt, w), lambda i: (rev(i), 0))
    rope_spec = pl.BlockSpec((bt, LANES), lambda i: (rev(i) % nsb, 0))
    const = lambda shape: pl.BlockSpec(shape, lambda i: (0, 0))
    return pl.pallas_call(
        body, name="qk_prep_bwd", grid=(nblk,),
        in_specs=[row(D_MODEL), row(D_MODEL), row(W_GROUP), row(W_GROUP), row(D_MODEL), row(MAIN_COLS), row(LANES),
                  const((1, LANES)), const((8, LANES)), rope_spec, rope_spec, rope_spec, const((LANES, LANES)),
                  const((bt, bt))],
        out_specs=(row(PROJ_COLS), const((2 * N_PAIRS, LANES)), const((8, LANES))),
        out_shape=(jax.ShapeDtypeStruct((t, PROJ_COLS), BF16),
                   jax.ShapeDtypeStruct((2 * N_PAIRS, LANES), F32), jax.ShapeDtypeStruct((8, LANES), F32)),
        scratch_shapes=[pltpu.VMEM((8, LANES), F32)], compiler_params=_params(1),
    )(dq, dk, dqx, dkx, dv, proj, fa, b_pad, gains, *rope, seg, triu)


def _bias_tables(seq, keys_first):
    nb = seq // ATT_BLOCK
    idx = np.arange(ATT_BLOCK)
    q_idx, k_idx = (idx[None, None, :], idx[None, :, None]) if keys_first else (idx[None, :, None], idx[None, None, :])
    dist = np.arange(nb)[:, None, None] * ATT_BLOCK + q_idx - k_idx
    causal = dist >= 0
    count = np.zeros(dist.shape, np.int32)
    for window, dilation in DILATION_PAIRS:
        count = count + (causal & (dist % dilation == 0) & (dist <= window))
    fox = np.where(causal, 0.0, NEG)
    dil = np.where(count == 3, math.log2(3.0), np.where(count == 2, 1.0, np.where(count == 1, 0.0, NEG)))
    return jnp.asarray(np.stack([fox, dil], axis=0), F32)


def _attn_specs(seq):
    nb = seq // ATT_BLOCK
    col = pl.BlockSpec((seq, LANES), lambda b, j: (b, j))
    heads = lambda off: pl.BlockSpec((seq, 2 * LANES), lambda b, j: (b, off + j))
    v_spec = pl.BlockSpec((seq, LANES), lambda b, j: (b, _proj_tile(2, j)))
    table_spec = pl.BlockSpec((1, nb, ATT_BLOCK, ATT_BLOCK), lambda b, j: (j // (N_PAIRS // 2), 0, 0, 0))
    return col, heads, v_spec, table_spec


def _head_lanes(e, shape, axis):
    pos = lax.broadcasted_iota(jnp.int32, shape, axis)
    return pos < HEAD_DIM if e == 0 else pos >= HEAD_DIM


def _attn_fwd(qk, proj, tables, seq, hosted=None):
    t = qk.shape[0]
    nb = seq // ATT_BLOCK
    blk = ATT_BLOCK

    def body(q_ref, k_ref, v_ref, tab_ref, o_ref, lse_ref):
        mine = [_head_lanes(e, (seq, LANES), 1) for e in range(2)]
        lane = lax.broadcasted_iota(jnp.int32, (seq, LANES), 1)
        v_aug = [jnp.where(mine[e], v_ref[...], (lane == HEAD_DIM * (1 - e)).astype(BF16)) for e in range(2)]
        def scores(i, e):
            heads_e = slice(e * LANES, (e + 1) * LANES)
            s = _dot_nt(q_ref[i * blk:(i + 1) * blk, heads_e], k_ref[0:(i + 1) * blk, heads_e])
            s = jnp.concatenate([s[:, jj * blk:(jj + 1) * blk] + tab_ref[0, i - jj] for jj in range(i + 1)], axis=1)
            return s, jnp.max(s, axis=1, keepdims=True)

        chains = [(i, e) for i in range(nb) for e in range(2)]
        pending = scores(*chains[0])
        done = {}
        for n, (i, e) in enumerate(chains):
            s, m = pending
            if n + 1 < len(chains):
                pending = scores(*chains[n + 1])
            acc = jnp.dot(jnp.exp2(s - m).astype(BF16), v_aug[e][0:(i + 1) * blk], preferred_element_type=F32)
            ones_at = HEAD_DIM * (1 - e)
            l = acc[:, ones_at:ones_at + 1]
            done[e] = (acc / l, m + jnp.log2(l))
            if e == 1:
                rows = slice(i * blk, (i + 1) * blk)
                o_ref[rows, :] = jnp.where(mine[0][rows], done[0][0], done[1][0]).astype(o_ref.dtype)
                lse_ref[rows, :] = jnp.where(mine[0][rows], done[0][1], done[1][1])

    col, heads, v_spec, table_spec = _attn_specs(seq)
    grid = (t // seq, N_PAIRS)
    h_specs, h_shapes, h_args, h_scratch = _hosted_parts(hosted)
    return pl.pallas_call(
        _host(body, 4, 2, 0, hosted, grid), name="attn_fwd", grid=grid,
        in_specs=[heads(0), heads(N_PAIRS), v_spec, table_spec] + h_specs,
        out_specs=tuple([col, col] + h_specs),
        out_shape=tuple([jax.ShapeDtypeStruct((t, D_MODEL), BF16), jax.ShapeDtypeStruct((t, D_MODEL), F32)] + h_shapes),
        scratch_shapes=h_scratch, compiler_params=_params(2),
    )(qk, qk, proj, tables, *h_args)


def _attn_bwd(qk, proj, tables, o, lse, do, seq, hosted=None):
    t = qk.shape[0]
    nb = seq // ATT_BLOCK
    blk = ATT_BLOCK
    group = math.gcd(nb, ATT_GROUP)

    def body(q_ref, k_ref, v_ref, tab_ref, o_ref, lse_ref, do_ref,
             dq_ref, dk_ref, dv_ref, dqx_ref, dkx_ref, dk_acc, dv_acc):
        mine = [_head_lanes(e, (blk, LANES), 1) for e in range(2)]
        top = _head_lanes(0, (LANES, blk), 0)
        head_rows = lax.broadcasted_iota(jnp.int32, (8, LANES), 0)
        head_of_lane = lax.broadcasted_iota(jnp.int32, (8, LANES), 1) // HEAD_DIM
        head_sel = (head_rows == head_of_lane).astype(BF16)
        dk_acc[...] = jnp.zeros_like(dk_acc)
        dv_acc[...] = jnp.zeros_like(dv_acc)

        def block_rows(i):
            return pl.ds(pl.multiple_of(i * blk, blk), blk)

        def q_group(g, _):
            base = g * group
            qs, doe, delta, lse_e = [], [], [], []
            for b in range(group):
                rows = block_rows(base + b)
                qs.append([q_ref[rows, e * LANES:(e + 1) * LANES] for e in range(2)])
                do_blk = do_ref[rows, :]
                doe.append([jnp.where(mine[e], do_blk, jnp.zeros_like(do_blk)) for e in range(2)])
                delta_t = _split_dot_nt(head_sel, do_blk.astype(F32) * o_ref[rows, :].astype(F32), 3)
                lse_t = _split_dot_nt(head_sel, lse_ref[rows, :], 3) * (1.0 / HEAD_DIM)
                delta.append([delta_t[e:e + 1, :] for e in range(2)])
                lse_e.append([lse_t[e:e + 1, :] for e in range(2)])

            def key_block(dq_t, jj, members):
                krows = block_rows(jj)
                v = v_ref[krows, :]
                dq_t = [list(d) for d in dq_t]
                lo, hi = slice(0, blk // 2), slice(blk // 2, blk)
                dv_part = [None, None]
                add = lambda acc, part: part if acc is None else acc + part

                def probs(k_sub, v_sub, keys, queries, b, e, dist):
                    q_sub, do_sub = qs[b][e][queries], doe[b][e][queries]
                    p_t = jnp.exp2(_dot_nt(k_sub, q_sub) + tab_ref[0, dist, keys, queries] - lse_e[b][e][:, queries])
                    ds_t = (p_t * (_dot_nt(v_sub, do_sub) - delta[b][e][:, queries])).astype(BF16)
                    return p_t.astype(BF16), ds_t, q_sub, do_sub, k_sub

                def outputs(tile):
                    p_t, ds_t, q_sub, do_sub, k_sub = tile
                    return (jnp.dot(p_t, do_sub, preferred_element_type=F32),
                            jnp.dot(ds_t, q_sub, preferred_element_type=F32), _dot_tn(k_sub, ds_t))

                for e in range(2):
                    k_e = k_ref[krows, e * LANES:(e + 1) * LANES]
                    dk_part = [None, None]
                    tiles = []
                    for b, dist in members:
                        if isinstance(dist, int) and dist == 0:
                            tiles.append((b, probs(k_e[lo], v[lo], lo, slice(0, blk), b, e, dist),
                                          probs(k_e[hi], v[hi], hi, hi, b, e, dist)))
                        else:
                            tiles.append((b, probs(k_e, v, slice(0, blk), slice(0, blk), b, e, dist), None))
                    for b, first, second in tiles:
                        if second is not None:
                            dv_a, dk_a, dq_a = outputs(first)
                            dv_b, dk_b, dq_b = outputs(second)
                            halves = ((dv_a, dk_a), (dv_b, dk_b))
                            dq = jnp.concatenate([dq_a[:, lo], dq_a[:, hi] + dq_b], axis=1)
                        else:
                            dv_f, dk_f, dq = outputs(first)
                            halves = ((dv_f[lo], dk_f[lo]), (dv_f[hi], dk_f[hi]))
                        for n, (dv_h, dk_h) in enumerate(halves):
                            dv_part[n] = add(dv_part[n], dv_h)
                            dk_part[n] = add(dk_part[n], dk_h)
                        dq_t[b][e] = dq_t[b][e] + dq
                    dk_acc[e, krows, :] += jnp.concatenate(dk_part, axis=0)
                dv_acc[krows, :] += jnp.concatenate(dv_part, axis=0)
                return tuple(tuple(d) for d in dq_t)

            zacc = jnp.zeros((LANES, blk), F32)
            dq_t = tuple((zacc, zacc) for _ in range(group))
            dq_t = lax.fori_loop(
                0, base, lambda jj, st: key_block(st, jj, [(b, base + b - jj) for b in range(group)]), dq_t)
            for a in range(group):
                dq_t = key_block(dq_t, base + a, [(b, b - a) for b in range(a, group)])
            for b in range(group):
                rows = block_rows(base + b)
                dq_ref[rows, :] = jnp.where(top, dq_t[b][0], dq_t[b][1]).T
                dqx_ref[rows, :] = jnp.where(top, dq_t[b][1], dq_t[b][0]).T
            return 0

        lax.fori_loop(0, nb // group, q_group, 0)
        lo = _head_lanes(0, (seq, LANES), 1)
        dk_ref[...] = jnp.where(lo, dk_acc[0], dk_acc[1])
        dkx_ref[...] = jnp.where(lo, dk_acc[1], dk_acc[0])
        dv_ref[...] = dv_acc[...].astype(dv_ref.dtype)

    col, heads, v_spec, table_spec = _attn_specs(seq)
    grid = (t // seq, N_PAIRS)
    h_specs, h_shapes, h_args, h_scratch = _hosted_parts(hosted)
    f32_out = jax.ShapeDtypeStruct((t, D_MODEL), F32)
    return pl.pallas_call(
        _host(body, 7, 5, 2, hosted, grid), name="attn_bwd", grid=grid,
        in_specs=[heads(0), heads(N_PAIRS), v_spec, table_spec, col, col, col] + h_specs,
        out_specs=tuple([col] * 5 + h_specs),
        out_shape=tuple([f32_out, f32_out, jax.ShapeDtypeStruct((t, D_MODEL), BF16), f32_out, f32_out] + h_shapes),
        scratch_shapes=[pltpu.VMEM((2, seq, LANES), F32), pltpu.VMEM((seq, LANES), F32)] + h_scratch,
        compiler_params=_params(2),
    )(qk, qk, proj, tables, o, lse, do, *h_args)


def _row_block(t):
    return 1024 if t % 1024 == 0 else ROW_BLOCK


def _out_proj_ffn_norm(o, g_out, w_out, x, g_ffn):
    t = o.shape[0]
    bt = _row_block(t)

    def body(o_ref, go_ref, w_ref, x_ref, gf_ref, on_ref, x2_ref, h2_ref):
        for s in range(0, D_MODEL, W_GROUP):
            os_ = o_ref[:, s:s + W_GROUP].astype(F32)
            r = lax.rsqrt(jnp.mean(os_ * os_, axis=-1, keepdims=True) + EPS)
            on_ref[:, s:s + W_GROUP] = (os_ * r * go_ref[:, s:s + W_GROUP]).astype(BF16)
        x2 = x_ref[...] + jnp.dot(on_ref[...], w_ref[...], preferred_element_type=F32)
        x2_ref[...] = x2
        r2 = lax.rsqrt(jnp.mean(x2 * x2, axis=-1, keepdims=True) + EPS)
        h2_ref[...] = (x2 * r2 * gf_ref[...]).astype(BF16)

    row = pl.BlockSpec((bt, D_MODEL), lambda i: (i, 0))
    vec = pl.BlockSpec((1, D_MODEL), lambda i: (0, 0))
    return pl.pallas_call(
        body, name="out_proj", grid=(t // bt,),
        in_specs=[row, vec, pl.BlockSpec((D_MODEL, D_MODEL), lambda i: (0, 0)), row, vec],
        out_specs=(row, row, row),
        out_shape=(jax.ShapeDtypeStruct((t, D_MODEL), BF16), jax.ShapeDtypeStruct((t, D_MODEL), F32),
                   jax.ShapeDtypeStruct((t, D_MODEL), BF16)),
        compiler_params=_params(1),
    )(o, g_out, w_out, x, g_ffn)


def _ffn_gate_up(h2, w_gate_t, w_up_t):
    t = h2.shape[0]
    bt = _row_block(t)
    bn = _divisor_block(D_FF, 1408)

    def body(h_ref, wg_ref, wu_ref, a_ref, u_ref, f_ref):
        a = _dot_nt(h_ref[...], wg_ref[...])
        u = _dot_nt(h_ref[...], wu_ref[...])
        a_ref[...] = a.astype(BF16)
        u_ref[...] = u.astype(BF16)
        f_ref[...] = (a * jax.nn.sigmoid(a) * u).astype(BF16)

    blk = pl.BlockSpec((bt, bn), lambda j, i: (i, j))
    w_blk = pl.BlockSpec((bn, D_MODEL), lambda j, i: (j, 0))
    shape = jax.ShapeDtypeStruct((t, D_FF), BF16)
    return pl.pallas_call(
        body, name="ffn_gate_up", grid=(D_FF // bn, t // bt),
        in_specs=[pl.BlockSpec((bt, D_MODEL), lambda j, i: (i, 0)), w_blk, w_blk],
        out_specs=(blk, blk, blk), out_shape=(shape, shape, shape), compiler_params=_params(2),
    )(h2, w_gate_t, w_up_t)


def _ffn_down_grad(dy16, w_down, a, u):
    t = a.shape[0]
    bt = _row_block(t)
    bn = _divisor_block(D_FF, 1408)

    def body(dy_ref, w_ref, a_ref, u_ref, da_ref, du_ref):
        df = _dot_nt(dy_ref[...], w_ref[...])
        av = a_ref[...].astype(F32)
        sg = jax.nn.sigmoid(av)
        da_ref[...] = (df * u_ref[...].astype(F32) * sg * (1.0 + av * (1.0 - sg))).astype(BF16)
        du_ref[...] = (df * av * sg).astype(BF16)

    blk = pl.BlockSpec((bt, bn), lambda j, i: (i, j))
    shape = jax.ShapeDtypeStruct((t, D_FF), BF16)
    return pl.pallas_call(
        body, name="d_ffn_down", grid=(D_FF // bn, t // bt),
        in_specs=[pl.BlockSpec((bt, D_MODEL), lambda j, i: (i, 0)), pl.BlockSpec((bn, D_MODEL), lambda j, i: (j, 0)),
                  blk, blk],
        out_specs=(blk, blk), out_shape=(shape, shape), compiler_params=_params(2),
    )(dy16, w_down, a, u)


def _ffn_down_loss(f, w_down, x2, target):
    t, w = x2.shape
    bt = _row_block(t)

    def body(f_ref, w_ref, x_ref, t_ref, dy_ref, dy16_ref, loss_ref):
        @pl.when(pl.program_id(0) == 0)
        def _():
            loss_ref[...] = jnp.zeros_like(loss_ref)

        err = (x_ref[...] + jnp.dot(f_ref[...], w_ref[...], preferred_element_type=F32)) - t_ref[...]
        dy = err * (1.0 / w)
        dy_ref[...] = dy
        dy16_ref[...] = dy.astype(BF16)
        loss_ref[...] += 0.5 * jnp.sum(jnp.mean(err * err, axis=-1, keepdims=True), axis=0, keepdims=True)

    row = pl.BlockSpec((bt, w), lambda i: (i, 0))
    return pl.pallas_call(
        body, name="ffn_down_loss", grid=(t // bt,),
        in_specs=[pl.BlockSpec((bt, D_FF), lambda i: (i, 0)), pl.BlockSpec((D_FF, w), lambda i: (0, 0)), row, row],
        out_specs=(row, row, pl.BlockSpec((8, LANES), lambda i: (0, 0))),
        out_shape=(jax.ShapeDtypeStruct((t, w), F32), jax.ShapeDtypeStruct((t, w), BF16),
                   jax.ShapeDtypeStruct((8, LANES), F32)),
        compiler_params=_params(1),
    )(f, w_down, x2, target)


def _adamw(parts, w, m, v, *, name):
    _, rows, cols = w.shape
    br = rows if rows <= 512 else 256
    assert rows % br == 0

    def body(p_ref, w_ref, m_ref, v_ref, g_ref, d_ref, nm_ref, nv_ref):
        g = p_ref[0].astype(F32)
        for r in range(1, N_DEV):
            g = g + p_ref[r].astype(F32)
        m2 = ADAM_B1 * m_ref[0] + (1.0 - ADAM_B1) * g
        v2 = ADAM_B2 * v_ref[0] + (1.0 - ADAM_B2) * jnp.square(g)
        m_hat = m2 / (1.0 - ADAM_B1 ** ADAM_STEP)
        v_hat = v2 / (1.0 - ADAM_B2 ** ADAM_STEP)
        g_ref[0] = g
        d_ref[0] = -ADAM_LR * (m_hat / (jnp.sqrt(v_hat) + ADAM_EPS) + ADAM_WD * w_ref[0])
        nm_ref[0] = m2
        nv_ref[0] = v2

    blk = pl.BlockSpec((1, br, cols), lambda i: (0, i, 0))
    shape = jax.ShapeDtypeStruct((1, rows, cols), F32)
    return pl.pallas_call(
        body, name=name, grid=(rows // br,),
        in_specs=[pl.BlockSpec((N_DEV, br, cols), lambda i: (0, i, 0)), blk, blk, blk],
        out_specs=(blk, blk, blk, blk), out_shape=(shape, shape, shape, shape), compiler_params=_params(1),
    )(parts, w, m, v)


_QA, _KA, _VA, _FA, _QD, _KD, _VD = (0, 512), (512, 1024), (1024, 1536), (1536, 1544), (1544, 2056), (2056, 2568), (2568, 3080)
_MAIN_ORDER = (_QA, _KA, _VA, _QD, _KD, _VD)
MAIN_COLS = 3 * D_MODEL
PROJ_COLS = MAIN_COLS + LANES
COL_SHARDED = ("w_in", "w_gate", "w_up")


def _swap(w):
    return jnp.transpose(w, (0, 2, 1))


def _w_in_to_kernel(w_t):
    main = jnp.concatenate([w_t[a:b] for a, b in _MAIN_ORDER], axis=0)
    forget = jnp.pad(w_t[_FA[0]:_FA[1]], ((0, LANES - N_HEADS_FOX), (0, 0)))
    return main, forget


def _w_in_from_kernel(g_t):
    pos = {span: i * W_GROUP for i, span in enumerate(_MAIN_ORDER)}
    parts = []
    for span in (_QA, _KA, _VA, _FA, _QD, _KD, _VD):
        if span == _FA:
            parts.append(g_t[MAIN_COLS:MAIN_COLS + N_HEADS_FOX])
        else:
            parts.append(g_t[pos[span]:pos[span] + W_GROUP])
    return jnp.concatenate(parts, axis=0)


def _pack_small(vals):
    rows = []
    for name, _, n_rows in SMALL_LAYOUT:
        flat = vals[name].reshape(-1).astype(F32)
        rows.append(jnp.pad(flat, (0, n_rows * LANES - flat.shape[0])).reshape(n_rows, LANES))
    packed = jnp.concatenate(rows, axis=0)
    return jnp.pad(packed, ((0, SMALL_ROWS - packed.shape[0]), (0, 0)))


def _unpack_small(packed, like):
    out = {}
    for name, row, n_rows in SMALL_LAYOUT:
        n = like[name].size
        out[name] = packed[row:row + n_rows].reshape(-1)[:n].reshape(like[name].shape)
    return out


def _device_step(x, target, small, shards):
    bsz, seq, _ = x.shape
    t = bsz * seq
    xf = x.reshape(t, D_MODEL)
    tf = target.reshape(t, D_MODEL)
    row = lambda v: v.reshape(1, -1)
    g_out = jnp.concatenate([small["g_out_fox"], small["g_out_dil"]]).reshape(1, D_MODEL)
    gains = jnp.concatenate(
        [jnp.tile(small[n].reshape(1, HEAD_DIM), (1, 2)) for n in ("g_q_fox", "g_q_dil", "g_k_fox", "g_k_dil")]
        + [jnp.zeros((4, LANES), F32)], axis=0)
    b_pad = jnp.pad(small["b_forget"].reshape(1, N_HEADS_FOX), ((0, 0), (0, LANES - N_HEADS_FOX)))
    rope = _rope_tables(seq)
    tables_qk = _bias_tables(seq, keys_first=False)
    tables_kq = _bias_tables(seq, keys_first=True)

    h1, g_in = _rmsnorm_fwd(xf, row(small["g_mix"]), group=D_MODEL, name="norm_mix",
                            hosted=_ChipGather([(shards["w_in"], False)]))
    w_main_t, w_fa_t = _w_in_to_kernel(g_in.reshape(IN_COLS, D_MODEL))
    w_in_all_t = jnp.concatenate([w_main_t, w_fa_t], axis=0)
    proj = _matmul_nt(h1, w_main_t, name="in_proj", out_dtype=BF16)
    qk, fa = _qk_prep_fwd(proj, h1, w_fa_t, b_pad, gains, rope, seq)
    late = _Exchange([(shards[n], False) for n in ("w_out", "w_gate", "w_up", "w_down")])
    o, lse, g_out_w, g_gate, g_up, g_down = _attn_fwd(qk, proj, tables_qk, seq, hosted=late)
    w_out = g_out_w.reshape(D_MODEL, D_MODEL)
    w_gate_t = g_gate.reshape(D_FF, D_MODEL)
    w_up_t = g_up.reshape(D_FF, D_MODEL)
    w_down = g_down.reshape(D_FF, D_MODEL)
    on, x2, h2 = _out_proj_ffn_norm(o, g_out, w_out, xf, row(small["g_ffn"]))
    a, u, f = _ffn_gate_up(h2, w_gate_t, w_up_t)
    dy, dy16, loss_tile = _ffn_down_loss(f, w_down, x2, tf)

    da, du = _ffn_down_grad(dy16, w_down, a, u)
    gw_down = _matmul_tn(f, dy16, name="gw_down")
    gw_gate_t = _matmul_tn(da, h2, name="gw_gate")
    gw_up_t = _matmul_tn(du, h2, name="gw_up")
    dx2, dx2_16, dg_ffn = _norm_input_grad([(da, w_gate_t, True), (du, w_up_t, True)], x2, row(small["g_ffn"]),
                                           group=D_MODEL, name="d_ffn_gate_up", out_dtypes=(F32, BF16), resid=dy,
                                           k_chunks=2)
    gw_out = _matmul_tn(on, dx2_16, name="gw_out")
    do, dg_out = _norm_input_grad([(dx2_16, w_out, False)], o, g_out, group=W_GROUP, name="d_out_proj",
                                  out_dtypes=(BF16,))

    shard_rows = lambda g: g.reshape(N_DEV, g.shape[0] // N_DEV, g.shape[1])
    ffn_grads = _Exchange([(shard_rows(g), True) for g in (gw_out, gw_gate_t, gw_up_t, gw_down)])
    dq, dk, dv, dqx, dkx, p_out, p_gate, p_up, p_down = _attn_bwd(qk, proj, tables_kq, o, lse, do, seq, hosted=ffn_grads)
    dproj, dgains, db = _qk_prep_bwd(dq, dk, dqx, dkx, dv, proj, fa, b_pad, gains, rope, seq)
    gw_in_t = _matmul_tn(dproj, h1, name="gw_in")
    in_grad = _Exchange([(shard_rows(_w_in_from_kernel(gw_in_t)), True)])
    dx, dg_mix, p_in = _norm_input_grad([(dproj, w_in_all_t, True)], xf, row(small["g_mix"]), group=D_MODEL,
                                        name="d_in_proj", out_dtypes=(F32,), resid=dx2, hosted=in_grad)

    fold = lambda rows: jnp.sum(rows[:, :HEAD_DIM] + rows[:, HEAD_DIM:], axis=0)
    half = N_PAIRS // 2
    gsmall = {
        "g_mix": dg_mix, "g_ffn": dg_ffn, "g_out_fox": dg_out[0, :W_GROUP], "g_out_dil": dg_out[0, W_GROUP:],
        "g_q_fox": fold(dgains[0:half]), "g_q_dil": fold(dgains[half:N_PAIRS]),
        "g_k_fox": fold(dgains[N_PAIRS:N_PAIRS + half]), "g_k_dil": fold(dgains[N_PAIRS + half:]),
        "b_forget": db[0, :N_HEADS_FOX],
    }
    packed = _pack_small(gsmall).at[LOSS_ROW].set(loss_tile[0])
    (p_small,) = _exchange("small_exchange", [(packed, False)])
    parts = {"w_in": p_in, "w_out": p_out, "w_gate": p_gate, "w_up": p_up, "w_down": p_down}
    return dx.reshape(x.shape), parts, p_small


def kernel(x, g_mix, w_in, b_forget, g_q_fox, g_k_fox, g_q_dil, g_k_dil, g_out_fox, g_out_dil, w_out, g_ffn, w_gate, w_up, w_down, loss_target, m_g_mix, m_w_in, m_b_forget, m_g_q_fox, m_g_k_fox, m_g_q_dil, m_g_k_dil, m_g_out_fox, m_g_out_dil, m_w_out, m_g_ffn, m_w_gate, m_w_up, m_w_down, v_g_mix, v_w_in, v_b_forget, v_g_q_fox, v_g_k_fox, v_g_q_dil, v_g_k_dil, v_g_out_fox, v_g_out_dil, v_w_out, v_g_ffn, v_w_gate, v_w_up, v_w_down):
    args = dict(locals())
    small_names = [name for name, _, _ in SMALL_LAYOUT]
    big_names = ["w_in", "w_out", "w_gate", "w_up", "w_down"]
    small = {n: args[n][0] for n in small_names}

    as_rows = lambda n, w: _swap(w) if n in COL_SHARDED else w
    shards = {n: as_rows(n, args[n])[0].astype(BF16) for n in big_names}
    grad_x, parts, p_small = _device_step(x, loss_target, small, shards)

    grads, deltas, new_m, new_v = {}, {}, {}, {}
    for n in big_names:
        res = _adamw(parts[n], as_rows(n, args[n]), as_rows(n, args["m_" + n]), as_rows(n, args["v_" + n]),
                     name="adamw_" + n)
        grads[n], deltas[n], new_m[n], new_v[n] = [as_rows(n, r) for r in res]
    res = _adamw(p_small, _pack_small(small)[None], _pack_small({n: args["m_" + n][0] for n in small_names})[None],
                 _pack_small({n: args["v_" + n][0] for n in small_names})[None], name="adamw_small")
    loss = res[0][0, LOSS_ROW, 0]
    for dst, packed_res in zip((grads, deltas, new_m, new_v), res):
        for n, val in _unpack_small(packed_res[0], small).items():
            dst[n] = val[None]

    order = ["g_mix", "w_in", "b_forget", "g_q_fox", "g_k_fox", "g_q_dil", "g_k_dil", "g_out_fox", "g_out_dil",
             "w_out", "g_ffn", "w_gate", "w_up", "w_down"]
    return (loss, grad_x, *[grads[n] for n in order], *[deltas[n] for n in order],
            *[new_m[n] for n in order], *[new_v[n] for n in order])
```

```python
import functools
import math

import jax
import jax.numpy as jnp
import numpy as np
from jax import lax
from jax.experimental import pallas as pl
from jax.experimental.pallas import tpu as pltpu

F32 = jnp.float32
BF16 = jnp.bfloat16

D_MODEL = 1024
HEAD_DIM = 64
LANES = 128
N_PAIRS = D_MODEL // LANES
N_HEADS = 2 * N_PAIRS
N_HEADS_FOX = 8
W_GROUP = 512
D_FF = 2816
IN_COLS = 3080
DILATION_PAIRS = ((128, 1), (512, 4), (2048, 16))
ROPE_THETA = 500000.0
ROPE_DIM = 16
ROPE_HALF = ROPE_DIM // 2
EPS = 1e-6
NEG = -1e30
LOG2E = 1.4426950408889634
LN2 = 0.6931471805599453
AUG_ONE = 0
AUG_C = 3
N_DEV = 8

ADAM_LR = 0.001
ADAM_B1 = 0.9
ADAM_B2 = 0.999
ADAM_EPS = 1e-08
ADAM_WD = 0.01
ADAM_STEP = 10

ROW_BLOCK = 512
ATT_BLOCK = 512
ATT_GROUP = 4
VMEM_LIMIT = 56 * 1024 * 1024
MATMUL_VMEM_BUDGET = 44 * 1024 * 1024

SMALL_ROWS = 32
SMALL_LAYOUT = (("g_mix", 0, 8), ("g_ffn", 8, 8), ("g_out_fox", 16, 4), ("g_out_dil", 20, 4),
                ("g_q_fox", 24, 1), ("g_k_fox", 25, 1), ("g_q_dil", 26, 1), ("g_k_dil", 27, 1),
                ("b_forget", 28, 1))
LOSS_ROW = 29


def _params(n_grid):
    return pltpu.CompilerParams(dimension_semantics=("arbitrary",) * n_grid, vmem_limit_bytes=VMEM_LIMIT)


def _divisor_block(n, cap):
    best = None
    for b in range(LANES, min(n, cap) + 1, LANES):
        if n % b == 0:
            best = b
    assert best is not None, n
    return best


def _split_dot(a, b_exact, terms):
    acc = None
    rest = a
    for _ in range(terms):
        hi = rest.astype(BF16)
        part = jnp.dot(hi, b_exact, preferred_element_type=F32)
        acc = part if acc is None else acc + part
        rest = rest - hi.astype(F32)
    return acc


def _split_dot_nt(a_exact, b, terms):
    acc = None
    rest = b
    for _ in range(terms):
        hi = rest.astype(BF16)
        part = _dot_nt(a_exact, hi)
        acc = part if acc is None else acc + part
        rest = rest - hi.astype(F32)
    return acc


def _dot_nt(a, b):
    return lax.dot_general(a, b, (((1,), (1,)), ((), ())), preferred_element_type=F32)


def _dot_tn(a, b):
    return lax.dot_general(a, b, (((0,), (0,)), ((), ())), preferred_element_type=F32)


class _Exchange:
    def __init__(self, items):
        self.items = items
        self.n = len(items)
        self.arrays = [a for a, _ in items]
        self.out_shape = [jax.ShapeDtypeStruct((N_DEV,) + tuple(a.shape[1:] if sc else a.shape), a.dtype)
                          for a, sc in items]
        self.specs = [pl.BlockSpec(memory_space=pl.ANY)] * self.n
        self.scratch = [pltpu.SemaphoreType.DMA((self.n, N_DEV - 1)), pltpu.SemaphoreType.DMA((self.n, N_DEV - 1)),
                        pltpu.SemaphoreType.DMA((self.n,))]

    def run(self, ins, outs, sems, first, last, compute):
        send_sems, recv_sems, local_sems = sems
        x, y, c = lax.axis_index("x"), lax.axis_index("y"), lax.axis_index("c")
        me = 4 * x + 2 * y + c
        local, remote = [], []
        for k, (_, scatter) in enumerate(self.items):
            own = ins[k].at[me] if scatter else ins[k]
            local.append(pltpu.make_async_copy(own, outs[k].at[me], local_sems.at[k]))
        for r in range(1, N_DEV):
            px = 1 - x if r & 4 else x
            py = 1 - y if r & 2 else y
            pc = 1 - c if r & 1 else c
            peer = 4 * px + 2 * py + pc
            for k, (_, scatter) in enumerate(self.items):
                src = ins[k].at[peer] if scatter else ins[k]
                remote.append(pltpu.make_async_remote_copy(
                    src_ref=src, dst_ref=outs[k].at[me],
                    send_sem=send_sems.at[k, r - 1], recv_sem=recv_sems.at[k, r - 1],
                    device_id=(px, py, pc), device_id_type=pl.DeviceIdType.MESH))

        def start():
            for cp in local + remote:
                cp.start()

        def finish():
            for cp in remote:
                cp.wait_recv()
            for cp in remote:
                cp.wait_send()
            for cp in local:
                cp.wait()

        _run_phases(first, last, start, compute, finish)


def _run_phases(first, last, start, compute, finish):
    if first is None:
        start()
        compute()
        finish()
    else:
        pl.when(first)(start)
        compute()
        pl.when(last)(finish)


class _ChipGather(_Exchange):
    def run(self, ins, outs, sems, first, last, compute):
        send_sems, recv_sems, local_sems = sems
        x, y, c = lax.axis_index("x"), lax.axis_index("y"), lax.axis_index("c")
        sibling = (x, y, 1 - c)
        chips = [(1 - x, y), (x, 1 - y), (1 - x, 1 - y)]
        slot = lambda px, py, pc: 4 * px + 2 * py + pc

        def copy(k, n, src, dst_slot, to):
            return pltpu.make_async_remote_copy(
                src_ref=src, dst_ref=outs[k].at[dst_slot], send_sem=send_sems.at[k, n], recv_sem=recv_sems.at[k, n],
                device_id=to, device_id_type=pl.DeviceIdType.MESH)

        local, own, passed, arrivals = [], [], [], []
        for k in range(self.n):
            me = slot(x, y, c)
            local.append(pltpu.make_async_copy(ins[k], outs[k].at[me], local_sems.at[k]))
            own.append(copy(k, 0, ins[k], me, sibling))
            arrivals.append(copy(k, 0, ins[k], slot(*sibling), sibling))
            for j, chip in enumerate(chips):
                theirs = slot(*chip, c)
                own.append(copy(k, 1 + j, ins[k], me, (*chip, c)))
                passed.append((copy(k, 1 + j, ins[k], theirs, sibling),
                               copy(k, 4 + j, outs[k].at[theirs], theirs, sibling)))
                arrivals.append(copy(k, 4 + j, ins[k], slot(*chip, 1 - c), sibling))

        def start():
            for cp in local + own:
                cp.start()

        def finish():
            for landed, onward in passed:
                landed.wait_recv()
                onward.start()
            for cp in arrivals:
                cp.wait_recv()
            for cp in own + [onward for _, onward in passed]:
                cp.wait_send()
            for cp in local:
                cp.wait()

        _run_phases(first, last, start, compute, finish)


def _grid_ends(grid):
    ids = [pl.program_id(d) for d in range(len(grid))]
    first = functools.reduce(jnp.logical_and, [i == 0 for i in ids])
    last = functools.reduce(jnp.logical_and, [i == g - 1 for i, g in zip(ids, grid)])
    return first, last


def _host(core, n_in, n_out, n_scratch, hosted, grid):
    if hosted is None:
        return core
    nh = hosted.n

    def body(*refs):
        ins, rest = refs[:n_in], refs[n_in:]
        h_ins, rest = rest[:nh], rest[nh:]
        outs, rest = rest[:n_out], rest[n_out:]
        h_outs, rest = rest[:nh], rest[nh:]
        scratch, sems = rest[:n_scratch], rest[n_scratch:]
        first, last = _grid_ends(grid)
        hosted.run(h_ins, h_outs, sems, first, last, lambda: core(*ins, *outs, *scratch))

    return body


def _hosted_parts(hosted):
    if hosted is None:
        return [], [], [], []
    return list(hosted.specs), list(hosted.out_shape), list(hosted.arrays), list(hosted.scratch)


def _exchange(name, items):
    ex = _Exchange(items)
    n = ex.n

    def body(*refs):
        ex.run(refs[:n], refs[n:2 * n], refs[2 * n:], None, None, lambda: None)

    return pl.pallas_call(
        body, name=name, out_shape=tuple(ex.out_shape), in_specs=ex.specs, out_specs=tuple(ex.specs),
        scratch_shapes=ex.scratch,
    )(*ex.arrays)


def _matmul_blocks(t, k, n, a_bytes, o_bytes):
    for bt, cap in ((1024, 1408), (1024, 512), (512, 512)):
        if t % bt:
            continue
        bn = _divisor_block(n, cap)
        if 2 * (bt * k * a_bytes + bn * k * 2 + bt * bn * o_bytes) <= MATMUL_VMEM_BUDGET:
            return bt, bn
    return ROW_BLOCK, _divisor_block(n, 256)


def _matmul_nt(a, w, *, name, out_dtype):
    t, k = a.shape
    n = w.shape[0]
    assert w.shape[1] == k
    bt, bn = _matmul_blocks(t, k, n, a.dtype.itemsize, jnp.dtype(out_dtype).itemsize)

    def body(a_ref, w_ref, o_ref):
        o_ref[...] = _dot_nt(a_ref[...], w_ref[...]).astype(o_ref.dtype)

    return pl.pallas_call(
        body, name=name, grid=(t // bt, n // bn),
        in_specs=[pl.BlockSpec((bt, k), lambda i, j: (i, 0)), pl.BlockSpec((bn, k), lambda i, j: (j, 0))],
        out_specs=pl.BlockSpec((bt, bn), lambda i, j: (i, j)),
        out_shape=jax.ShapeDtypeStruct((t, n), out_dtype), compiler_params=_params(2),
    )(a, w)


def _matmul_rows(a, w, *, name):
    t, k = a.shape
    n = w.shape[1]
    assert w.shape[0] == k
    bt = _row_block(t)

    def body(a_ref, w_ref, o_ref):
        o_ref[...] = jnp.dot(a_ref[...], w_ref[...], preferred_element_type=F32)

    return pl.pallas_call(
        body, name=name, grid=(t // bt,),
        in_specs=[pl.BlockSpec((bt, k), lambda i: (i, 0)), pl.BlockSpec((k, n), lambda i: (0, 0))],
        out_specs=pl.BlockSpec((bt, n), lambda i: (i, 0)),
        out_shape=jax.ShapeDtypeStruct((t, n), F32), compiler_params=_params(1),
    )(a, w)


def _matmul_tn(a, b, *, name):
    t, m = a.shape
    n = b.shape[1]
    bt = 2048 if t % 2048 == 0 else ROW_BLOCK
    bm = _divisor_block(m, 1408)
    bn = _divisor_block(n, 1408)
    steps = t // bt

    def body(a_ref, b_ref, o_ref, acc):
        step = pl.program_id(2)

        @pl.when(step == 0)
        def _():
            acc[...] = jnp.zeros_like(acc)

        acc[...] += _dot_tn(a_ref[...], b_ref[...])

        @pl.when(step == steps - 1)
        def _():
            o_ref[...] = acc[...].astype(o_ref.dtype)

    return pl.pallas_call(
        body, name=name, grid=(m // bm, n // bn, steps),
        in_specs=[pl.BlockSpec((bt, bm), lambda i, j, s: (s, i)), pl.BlockSpec((bt, bn), lambda i, j, s: (s, j))],
        out_specs=pl.BlockSpec((bm, bn), lambda i, j, s: (i, j)),
        out_shape=jax.ShapeDtypeStruct((m, n), BF16), scratch_shapes=[pltpu.VMEM((bm, bn), F32)],
        compiler_params=_params(3),
    )(a, b)


def _rmsnorm_fwd(x, g, *, group, name, hosted=None):
    t, w = x.shape
    bt = ROW_BLOCK

    def body(x_ref, g_ref, o_ref):
        for s in range(0, w, group):
            xs = x_ref[:, s:s + group].astype(F32)
            r = lax.rsqrt(jnp.mean(xs * xs, axis=-1, keepdims=True) + EPS)
            o_ref[:, s:s + group] = (xs * r * g_ref[:, s:s + group]).astype(o_ref.dtype)

    grid = (t // bt,)
    h_specs, h_shapes, h_args, h_scratch = _hosted_parts(hosted)
    res = pl.pallas_call(
        _host(body, 2, 1, 0, hosted, grid), name=name, grid=grid,
        in_specs=[pl.BlockSpec((bt, w), lambda i: (i, 0)), pl.BlockSpec((1, w), lambda i: (0, 0))] + h_specs,
        out_specs=tuple([pl.BlockSpec((bt, w), lambda i: (i, 0))] + h_specs),
        out_shape=tuple([jax.ShapeDtypeStruct((t, w), BF16)] + h_shapes),
        scratch_shapes=h_scratch, compiler_params=_params(1),
    )(x, g, *h_args)
    return res if hosted else res[0]


def _norm_input_grad(terms, x, g, *, group, name, out_dtypes, resid=None, init=None, hosted=None, k_chunks=1):
    t, w = x.shape
    n_terms = len(terms)
    kc = [a.shape[1] // k_chunks for a, _, _ in terms]
    per_row = sum(c * a.dtype.itemsize for c, (a, _, _) in zip(kc, terms)) + w * (x.dtype.itemsize + 4 * (resid is not None))
    per_row += w * (sum(jnp.dtype(dt).itemsize for dt in out_dtypes) + 4 * (init is not None))
    fixed = 2 * sum(w * c * 2 for c in kc)
    bt = next(b for b in (1024, 512, 256, 128)
              if t % b == 0 and fixed + 2 * b * per_row + 5 * b * w * 4 <= MATMUL_VMEM_BUDGET)
    resid_at = 2 * n_terms + 2
    init_at = resid_at + (resid is not None)
    n_in = init_at + (init is not None)
    grid = (t // bt, k_chunks)

    def body(*refs):
        x_ref, g_ref = refs[2 * n_terms], refs[2 * n_terms + 1]
        dx_refs, dg_ref, dh_ref = refs[n_in:-2], refs[-2], refs[-1]
        chunk = pl.program_id(1)

        @pl.when((pl.program_id(0) == 0) & (chunk == 0))
        def _():
            dg_ref[...] = jnp.zeros_like(dg_ref)

        part = None
        for k in range(n_terms):
            if terms[k][2]:
                term = jnp.dot(refs[2 * k][...], refs[2 * k + 1][...], preferred_element_type=F32)
            else:
                term = _dot_nt(refs[2 * k][...], refs[2 * k + 1][...])
            part = term if part is None else part + term

        @pl.when(chunk == 0)
        def _():
            dh_ref[...] = part if init is None else refs[init_at][...] + part

        @pl.when(chunk > 0)
        def _():
            dh_ref[...] += part

        @pl.when(chunk == k_chunks - 1)
        def _():
            for s in range(0, w, group):
                xs = x_ref[:, s:s + group].astype(F32)
                dhs = dh_ref[:, s:s + group]
                r = lax.rsqrt(jnp.mean(xs * xs, axis=-1, keepdims=True) + EPS)
                xh = xs * r
                dg_ref[:, s:s + group] += jnp.sum(dhs * xh, axis=0, keepdims=True)
                dxh = dhs * g_ref[:, s:s + group]
                dx = r * (dxh - xh * jnp.mean(dxh * xh, axis=-1, keepdims=True))
                if resid is not None:
                    dx = refs[resid_at][:, s:s + group] + dx
                for dx_ref in dx_refs:
                    dx_ref[:, s:s + group] = dx.astype(dx_ref.dtype)

    row = pl.BlockSpec((bt, w), lambda i, k: (i, 0))
    vec = pl.BlockSpec((1, w), lambda i, k: (0, 0))
    in_specs, args = [], []
    for c, (a, wt, w_is_kn) in zip(kc, terms):
        assert wt.shape == ((a.shape[1], w) if w_is_kn else (w, a.shape[1]))
        w_spec = pl.BlockSpec((c, w), lambda i, k: (k, 0)) if w_is_kn else pl.BlockSpec((w, c), lambda i, k: (0, k))
        in_specs += [pl.BlockSpec((bt, c), lambda i, k: (i, k)), w_spec]
        args += [a, wt]
    extra = [r for r in (resid, init) if r is not None]
    in_specs += [row, vec] + [row] * len(extra)
    args += [x, g] + extra
    h_specs, h_shapes, h_args, h_scratch = _hosted_parts(hosted)
    return pl.pallas_call(
        _host(body, n_in, len(out_dtypes) + 1, 1, hosted, grid), name=name, grid=grid, in_specs=in_specs + h_specs,
        out_specs=tuple([row] * len(out_dtypes) + [vec] + h_specs),
        out_shape=tuple([jax.ShapeDtypeStruct((t, w), dt) for dt in out_dtypes] + [jax.ShapeDtypeStruct((1, w), F32)]
                        + h_shapes),
        scratch_shapes=[pltpu.VMEM((bt, w), F32)] + h_scratch, compiler_params=_params(2),
    )(*args, *h_args)


def _tile_plan(tile):
    is_q = tile < N_PAIRS
    is_dil = (tile % N_PAIRS) >= N_PAIRS // 2
    return is_q, is_dil, (0 if is_q else 2) + (1 if is_dil else 0)


def _proj_tile(kind, pair):
    return kind * N_PAIRS + pair


def _segment_ones():
    lane = np.arange(LANES)
    return jnp.asarray((lane[:, None] // HEAD_DIM) == (lane[None, :] // HEAD_DIM), BF16)


def _rope_tables(seq):
    inv_freq = jnp.power(jnp.float32(ROPE_THETA), -jnp.arange(ROPE_HALF, dtype=F32) * 2.0 / ROPE_DIM)
    ang = jnp.arange(seq).astype(F32)[:, None] * inv_freq[None, :]
    cos, sin = jnp.cos(ang), jnp.sin(ang)
    ones = jnp.ones((seq, HEAD_DIM - ROPE_DIM), F32)
    zeros = jnp.zeros((seq, HEAD_DIM - ROPE_DIM), F32)
    zh = jnp.zeros((seq, ROPE_HALF), F32)
    cos_t = jnp.concatenate([cos, cos, ones], axis=1)
    sin_a = jnp.concatenate([-sin, zh, zeros], axis=1)
    sin_b = jnp.concatenate([zh, sin, zeros], axis=1)
    return tuple(jnp.tile(tab, (1, 2)) for tab in (cos_t, sin_a, sin_b))


def _log_sigmoid(z):
    return jnp.minimum(z, 0.0) - jnp.log1p(jnp.exp(-jnp.abs(z)))


def _aug_placement():
    place = np.zeros((N_PAIRS, LANES, LANES), np.float32)
    for is_k in range(2):
        for pair in range(N_PAIRS // 2):
            for e in range(2):
                other = HEAD_DIM * (1 - e)
                ones_at = other + (AUG_C if is_k else AUG_ONE)
                c_at = other + (AUG_ONE if is_k else AUG_C)
                for n in range(3):
                    place[4 * is_k + pair, N_HEADS_FOX * n + 2 * pair + e, c_at + n] = -1.0 if is_k else 1.0
                    place[4 * is_k + pair, 3 * N_HEADS_FOX, ones_at + n] = 1.0
    return jnp.asarray(place, BF16)


def _qk_prep_fwd(proj, h1, w_fa_t, b_pad, gains, rope, seq):
    t = proj.shape[0]
    bt = ROW_BLOCK
    nsb = seq // bt
    seg = _segment_ones()
    rr = np.arange(bt)
    tri = jnp.asarray(rr[:, None] <= rr[None, :], BF16)

    def body(p_ref, h_ref, wfa_ref, b_ref, g_ref, cos_ref, sa_ref, sb_ref, seg_ref, tri_ref, place_ref,
             qk_ref, fa_ref, carry):
        @pl.when(pl.program_id(0) % nsb == 0)
        def _():
            carry[...] = jnp.zeros_like(carry)

        lane = lax.broadcasted_iota(jnp.int32, (bt, LANES), 1)
        fa = _dot_nt(h_ref[...], wfa_ref[...])
        fa_ref[...] = fa
        logf = jnp.where(lane < N_HEADS_FOX, _log_sigmoid(fa + b_ref[...]), 0.0)
        c_rows = _split_dot(logf.T[0:N_HEADS_FOX, :], tri_ref[...], 3) + carry[:, 0:1]
        carry[...] = jnp.broadcast_to(c_rows[:, bt - 1:bt], carry.shape)
        cblk = jnp.concatenate([c_rows, jnp.zeros((LANES - N_HEADS_FOX, bt), F32)], axis=0).T
        packed = jnp.where(lane == 3 * N_HEADS_FOX, 1.0, 0.0)
        rest = cblk * LOG2E
        for n in range(3):
            term = rest.astype(BF16).astype(F32)
            packed = packed + (pltpu.roll(term, N_HEADS_FOX * n, 1) if n else term)
            rest = rest - term
        packed = packed.astype(BF16)
        low = lane < HEAD_DIM

        for tile in range(2 * N_PAIRS):
            is_q, is_dil, grow = _tile_plan(tile)
            pair = tile % N_PAIRS
            src = _proj_tile(0 if is_q else 1, pair) * LANES
            xs = p_ref[:, src:src + LANES].astype(F32)
            r = lax.rsqrt(_split_dot(xs * xs, seg_ref[...], 2) * (1.0 / HEAD_DIM) + EPS)
            yv = xs * r * g_ref[grow:grow + 1, :]
            if is_dil:
                yv = (yv * cos_ref[...] + pltpu.roll(yv, LANES - ROPE_HALF, 1) * sa_ref[...]
                      + pltpu.roll(yv, ROPE_HALF, 1) * sb_ref[...])
                aug = jnp.zeros((bt, LANES), F32)
            else:
                aug = jnp.dot(packed, place_ref[(0 if is_q else N_PAIRS // 2) + pair], preferred_element_type=F32)
            if is_q:
                yv = yv * (HEAD_DIM ** -0.5 * LOG2E)
            dst = ((0 if is_q else N_HEADS) + 2 * pair) * LANES
            qk_ref[:, dst:dst + LANES] = jnp.where(low, yv, aug).astype(BF16)
            qk_ref[:, dst + LANES:dst + 2 * LANES] = jnp.where(low, aug, yv).astype(BF16)

    row128 = pl.BlockSpec((bt, LANES), lambda i: (i, 0))
    rope_spec = pl.BlockSpec((bt, LANES), lambda i: (i % nsb, 0))
    const = lambda shape: pl.BlockSpec(shape, lambda i: (0,) * len(shape))
    return pl.pallas_call(
        body, name="qk_prep_fwd", grid=(t // bt,),
        in_specs=[pl.BlockSpec((bt, 2 * D_MODEL), lambda i: (i, 0)), pl.BlockSpec((bt, D_MODEL), lambda i: (i, 0)),
                  const((LANES, D_MODEL)), const((1, LANES)), const((8, LANES)), rope_spec, rope_spec, rope_spec,
                  const((LANES, LANES)), const((bt, bt)), const((N_PAIRS, LANES, LANES))],
        out_specs=(pl.BlockSpec((bt, 2 * N_HEADS * LANES), lambda i: (i, 0)), row128),
        out_shape=(jax.ShapeDtypeStruct((t, 2 * N_HEADS * LANES), BF16), jax.ShapeDtypeStruct((t, LANES), F32)),
        scratch_shapes=[pltpu.VMEM((8, LANES), F32)], compiler_params=_params(1),
    )(proj, h1, w_fa_t, b_pad, gains, *rope, seg, tri, _aug_placement())


def _qk_prep_bwd(dq, dk, dqx, dkx, dv, proj, fa, b_pad, gains, rope, seq):
    t = proj.shape[0]
    bt = ROW_BLOCK
    nsb = seq // bt
    nblk = t // bt
    seg = _segment_ones()
    rr = np.arange(bt)
    triu = jnp.asarray(rr[:, None] >= rr[None, :], BF16)

    def body(dq_ref, dk_ref, dqx_ref, dkx_ref, dv_ref, p_ref, fa_ref, b_ref, g_ref, cos_ref, sa_ref, sb_ref, seg_ref,
             triu_ref, dp_ref, dg_ref, db_ref, carry):
        step = pl.program_id(0)

        @pl.when(step == 0)
        def _():
            dg_ref[...] = jnp.zeros_like(dg_ref)
            db_ref[...] = jnp.zeros_like(db_ref)

        @pl.when(step % nsb == 0)
        def _():
            carry[...] = jnp.zeros_like(carry)

        for tile in range(2 * N_PAIRS):
            is_q, is_dil, grow = _tile_plan(tile)
            first = _proj_tile(0 if is_q else 1, tile % N_PAIRS) * LANES
            cols = slice(first, first + LANES)
            src = dq_ref if is_q else dk_ref
            half = slice((tile % N_PAIRS) * LANES, (tile % N_PAIRS + 1) * LANES)
            dy = src[:, half]
            dy = dy * (HEAD_DIM ** -0.5 if is_q else LN2)
            if is_dil:
                dy = (dy * cos_ref[...] + pltpu.roll(dy * sa_ref[...], ROPE_HALF, 1)
                      + pltpu.roll(dy * sb_ref[...], LANES - ROPE_HALF, 1))
            xs = p_ref[:, cols].astype(F32)
            r = lax.rsqrt(_split_dot(xs * xs, seg_ref[...], 2) * (1.0 / HEAD_DIM) + EPS)
            xh = xs * r
            dg_ref[tile:tile + 1, :] += jnp.sum(dy * xh, axis=0, keepdims=True)
            dxh = dy * g_ref[grow:grow + 1, :]
            seg_mean = _split_dot(dxh * xh, seg_ref[...], 2) * (1.0 / HEAD_DIM)
            dp_ref[:, cols] = (r * (dxh - xh * seg_mean)).astype(BF16)

        lane = lax.broadcasted_iota(jnp.int32, (bt, LANES), 1)
        dc = jnp.zeros((bt, LANES), F32)
        for h in range(N_HEADS_FOX):
            other = (h // 2) * LANES + HEAD_DIM * (1 - h % 2)
            row_sum = dqx_ref[:, other + AUG_C:other + AUG_C + 1]
            col_sum = dkx_ref[:, other + AUG_ONE:other + AUG_ONE + 1]
            dc = jnp.where(lane == h, row_sum - col_sum, dc)
        d_rows = _split_dot(dc.T[0:N_HEADS_FOX, :], triu_ref[...], 3) + carry[:, 0:1]
        carry[...] = jnp.broadcast_to(d_rows[:, 0:1], carry.shape)
        dlogf = jnp.concatenate([d_rows, jnp.zeros((LANES - N_HEADS_FOX, bt), F32)], axis=0).T
        z = fa_ref[...] + b_ref[...]
        dfa = dlogf * (1.0 / (1.0 + jnp.exp(z)))
        db_ref[0:1, :] += jnp.sum(dfa, axis=0, keepdims=True)
        for group in range(2):
            first = _proj_tile(2, group * (N_PAIRS // 2)) * LANES
            dp_ref[:, first:first + W_GROUP] = dv_ref[:, group * W_GROUP:(group + 1) * W_GROUP]
        dp_ref[:, MAIN_COLS:PROJ_COLS] = dfa.astype(BF16)

    rev = lambda i: nblk - 1 - i
    row = lambda w: pl.BlockSpec((bt, w), lambda i: (rev(i), 0))
    rope_spec = pl.BlockSpec((bt, LANES), lambda i: (rev(i) % nsb, 0))
    const = lambda shape: pl.BlockSpec(shape, lambda i: (0, 0))
    return pl.pallas_call(
        body, name="qk_prep_bwd", grid=(nblk,),
        in_specs=[row(D_MODEL), row(D_MODEL), row(W_GROUP), row(W_GROUP), row(D_MODEL), row(2 * D_MODEL), row(LANES),
                  const((1, LANES)), const((8, LANES)), rope_spec, rope_spec, rope_spec, const((LANES, LANES)),
                  const((bt, bt))],
        out_specs=(row(PROJ_COLS), const((2 * N_PAIRS, LANES)), const((8, LANES))),
        out_shape=(jax.ShapeDtypeStruct((t, PROJ_COLS), BF16),
                   jax.ShapeDtypeStruct((2 * N_PAIRS, LANES), F32), jax.ShapeDtypeStruct((8, LANES), F32)),
        scratch_shapes=[pltpu.VMEM((8, LANES), F32)], compiler_params=_params(1),
    )(dq, dk, dqx, dkx, dv, proj, fa, b_pad, gains, *rope, seg, triu)


def _bias_tables(seq, keys_first):
    nb = seq // ATT_BLOCK
    idx = np.arange(ATT_BLOCK)
    q_idx, k_idx = (idx[None, None, :], idx[None, :, None]) if keys_first else (idx[None, :, None], idx[None, None, :])
    dist = np.arange(nb)[:, None, None] * ATT_BLOCK + q_idx - k_idx
    causal = dist >= 0
    count = np.zeros(dist.shape, np.int32)
    for window, dilation in DILATION_PAIRS:
        count = count + (causal & (dist % dilation == 0) & (dist <= window))
    fox = np.where(causal, 0.0, NEG)
    dil = np.where(count == 3, math.log2(3.0), np.where(count == 2, 1.0, np.where(count == 1, 0.0, NEG)))
    return jnp.asarray(np.stack([fox, dil], axis=0), F32)


def _attn_specs(seq):
    nb = seq // ATT_BLOCK
    col = pl.BlockSpec((seq, LANES), lambda b, j: (b, j))
    heads = lambda off: pl.BlockSpec((seq, 2 * LANES), lambda b, j: (b, off + j))
    v_spec = pl.BlockSpec((seq, LANES), lambda b, j: (b, _proj_tile(2, j)))
    table_spec = pl.BlockSpec((1, nb, ATT_BLOCK, ATT_BLOCK), lambda b, j: (j // (N_PAIRS // 2), 0, 0, 0))
    return col, heads, v_spec, table_spec


def _head_lanes(e, shape, axis):
    pos = lax.broadcasted_iota(jnp.int32, shape, axis)
    return pos < HEAD_DIM if e == 0 else pos >= HEAD_DIM


def _attn_fwd(qk, proj, tables, seq, hosted=None):
    t = qk.shape[0]
    nb = seq // ATT_BLOCK
    blk = ATT_BLOCK

    def body(q_ref, k_ref, v_ref, tab_ref, o_ref, lse_ref):
        mine = [_head_lanes(e, (seq, LANES), 1) for e in range(2)]
        lane = lax.broadcasted_iota(jnp.int32, (seq, LANES), 1)
        v_aug = [jnp.where(mine[e], v_ref[...], (lane == HEAD_DIM * (1 - e)).astype(BF16)) for e in range(2)]
        def scores(i, e):
            heads_e = slice(e * LANES, (e + 1) * LANES)
            s = _dot_nt(q_ref[i * blk:(i + 1) * blk, heads_e], k_ref[0:(i + 1) * blk, heads_e])
            s = jnp.concatenate([s[:, jj * blk:(jj + 1) * blk] + tab_ref[0, i - jj] for jj in range(i + 1)], axis=1)
            return s, jnp.max(s, axis=1, keepdims=True)

        chains = [(i, e) for i in range(nb) for e in range(2)]
        pending = scores(*chains[0])
        done = {}
        for n, (i, e) in enumerate(chains):
            s, m = pending
            if n + 1 < len(chains):
                pending = scores(*chains[n + 1])
            acc = jnp.dot(jnp.exp2(s - m).astype(BF16), v_aug[e][0:(i + 1) * blk], preferred_element_type=F32)
            ones_at = HEAD_DIM * (1 - e)
            l = acc[:, ones_at:ones_at + 1]
            done[e] = (acc / l, m + jnp.log2(l))
            if e == 1:
                rows = slice(i * blk, (i + 1) * blk)
                o_ref[rows, :] = jnp.where(mine[0][rows], done[0][0], done[1][0]).astype(o_ref.dtype)
                lse_ref[rows, :] = jnp.where(mine[0][rows], done[0][1], done[1][1])

    col, heads, v_spec, table_spec = _attn_specs(seq)
    grid = (t // seq, N_PAIRS)
    h_specs, h_shapes, h_args, h_scratch = _hosted_parts(hosted)
    return pl.pallas_call(
        _host(body, 4, 2, 0, hosted, grid), name="attn_fwd", grid=grid,
        in_specs=[heads(0), heads(N_PAIRS), v_spec, table_spec] + h_specs,
        out_specs=tuple([col, col] + h_specs),
        out_shape=tuple([jax.ShapeDtypeStruct((t, D_MODEL), BF16), jax.ShapeDtypeStruct((t, D_MODEL), F32)] + h_shapes),
        scratch_shapes=h_scratch, compiler_params=_params(2),
    )(qk, qk, proj, tables, *h_args)


def _attn_bwd(qk, proj, tables, o, lse, do, seq, hosted=None):
    t = qk.shape[0]
    nb = seq // ATT_BLOCK
    blk = ATT_BLOCK
    group = math.gcd(nb, ATT_GROUP)

    def body(q_ref, k_ref, v_ref, tab_ref, o_ref, lse_ref, do_ref,
             dq_ref, dk_ref, dv_ref, dqx_ref, dkx_ref, dk_acc, dv_acc):
        mine = [_head_lanes(e, (blk, LANES), 1) for e in range(2)]
        top = _head_lanes(0, (LANES, blk), 0)
        head_rows = lax.broadcasted_iota(jnp.int32, (8, LANES), 0)
        head_of_lane = lax.broadcasted_iota(jnp.int32, (8, LANES), 1) // HEAD_DIM
        head_sel = (head_rows == head_of_lane).astype(BF16)
        dk_acc[...] = jnp.zeros_like(dk_acc)
        dv_acc[...] = jnp.zeros_like(dv_acc)

        def block_rows(i):
            return pl.ds(pl.multiple_of(i * blk, blk), blk)

        def q_group(g, _):
            base = g * group
            qs, doe, delta, lse_e = [], [], [], []
            for b in range(group):
                rows = block_rows(base + b)
                qs.append([q_ref[rows, e * LANES:(e + 1) * LANES] for e in range(2)])
                do_blk = do_ref[rows, :]
                doe.append([jnp.where(mine[e], do_blk, jnp.zeros_like(do_blk)) for e in range(2)])
                delta_t = _split_dot_nt(head_sel, do_blk.astype(F32) * o_ref[rows, :].astype(F32), 3)
                lse_t = _split_dot_nt(head_sel, lse_ref[rows, :], 3) * (1.0 / HEAD_DIM)
                delta.append([delta_t[e:e + 1, :] for e in range(2)])
                lse_e.append([lse_t[e:e + 1, :] for e in range(2)])

            def key_block(dq_t, jj, members):
                krows = block_rows(jj)
                v = v_ref[krows, :]
                dq_t = [list(d) for d in dq_t]
                lo, hi = slice(0, blk // 2), slice(blk // 2, blk)
                dv_part = [None, None]
                add = lambda acc, part: part if acc is None else acc + part

                def probs(k_sub, v_sub, keys, queries, b, e, dist):
                    q_sub, do_sub = qs[b][e][queries], doe[b][e][queries]
                    p_t = jnp.exp2(_dot_nt(k_sub, q_sub) + tab_ref[0, dist, keys, queries] - lse_e[b][e][:, queries])
                    ds_t = (p_t * (_dot_nt(v_sub, do_sub) - delta[b][e][:, queries])).astype(BF16)
                    return p_t.astype(BF16), ds_t, q_sub, do_sub, k_sub

                def outputs(tile):
                    p_t, ds_t, q_sub, do_sub, k_sub = tile
                    return (jnp.dot(p_t, do_sub, preferred_element_type=F32),
                            jnp.dot(ds_t, q_sub, preferred_element_type=F32), _dot_tn(k_sub, ds_t))

                for e in range(2):
                    k_e = k_ref[krows, e * LANES:(e + 1) * LANES]
                    dk_part = [None, None]
                    tiles = []
                    for b, dist in members:
                        if isinstance(dist, int) and dist == 0:
                            tiles.append((b, probs(k_e[lo], v[lo], lo, slice(0, blk), b, e, dist),
                                          probs(k_e[hi], v[hi], hi, hi, b, e, dist)))
                        else:
                            tiles.append((b, probs(k_e, v, slice(0, blk), slice(0, blk), b, e, dist), None))
                    for b, first, second in tiles:
                        if second is not None:
                            dv_a, dk_a, dq_a = outputs(first)
                            dv_b, dk_b, dq_b = outputs(second)
                            halves = ((dv_a, dk_a), (dv_b, dk_b))
                            dq = jnp.concatenate([dq_a[:, lo], dq_a[:, hi] + dq_b], axis=1)
                        else:
                            dv_f, dk_f, dq = outputs(first)
                            halves = ((dv_f[lo], dk_f[lo]), (dv_f[hi], dk_f[hi]))
                        for n, (dv_h, dk_h) in enumerate(halves):
                            dv_part[n] = add(dv_part[n], dv_h)
                            dk_part[n] = add(dk_part[n], dk_h)
                        dq_t[b][e] = dq_t[b][e] + dq
                    dk_acc[e, krows, :] += jnp.concatenate(dk_part, axis=0)
                dv_acc[krows, :] += jnp.concatenate(dv_part, axis=0)
                return tuple(tuple(d) for d in dq_t)

            zacc = jnp.zeros((LANES, blk), F32)
            dq_t = tuple((zacc, zacc) for _ in range(group))
            dq_t = lax.fori_loop(
                0, base, lambda jj, st: key_block(st, jj, [(b, base + b - jj) for b in range(group)]), dq_t)
            for a in range(group):
                dq_t = key_block(dq_t, base + a, [(b, b - a) for b in range(a, group)])
            for b in range(group):
                rows = block_rows(base + b)
                dq_ref[rows, :] = jnp.where(top, dq_t[b][0], dq_t[b][1]).T
                dqx_ref[rows, :] = jnp.where(top, dq_t[b][1], dq_t[b][0]).T
            return 0

        lax.fori_loop(0, nb // group, q_group, 0)
        lo = _head_lanes(0, (seq, LANES), 1)
        dk_ref[...] = jnp.where(lo, dk_acc[0], dk_acc[1])
        dkx_ref[...] = jnp.where(lo, dk_acc[1], dk_acc[0])
        dv_ref[...] = dv_acc[...].astype(dv_ref.dtype)

    col, heads, v_spec, table_spec = _attn_specs(seq)
    grid = (t // seq, N_PAIRS)
    h_specs, h_shapes, h_args, h_scratch = _hosted_parts(hosted)
    f32_out = jax.ShapeDtypeStruct((t, D_MODEL), F32)
    return pl.pallas_call(
        _host(body, 7, 5, 2, hosted, grid), name="attn_bwd", grid=grid,
        in_specs=[heads(0), heads(N_PAIRS), v_spec, table_spec, col, col, col] + h_specs,
        out_specs=tuple([col] * 5 + h_specs),
        out_shape=tuple([f32_out, f32_out, jax.ShapeDtypeStruct((t, D_MODEL), BF16), f32_out, f32_out] + h_shapes),
        scratch_shapes=[pltpu.VMEM((2, seq, LANES), F32), pltpu.VMEM((seq, LANES), F32)] + h_scratch,
        compiler_params=_params(2),
    )(qk, qk, proj, tables, o, lse, do, *h_args)


def _row_block(t):
    return 1024 if t % 1024 == 0 else ROW_BLOCK


def _out_proj_ffn_norm(o, g_out, w_out, x, g_ffn):
    t = o.shape[0]
    bt = _row_block(t)

    def body(o_ref, go_ref, w_ref, x_ref, gf_ref, on_ref, x2_ref, h2_ref):
        for s in range(0, D_MODEL, W_GROUP):
            os_ = o_ref[:, s:s + W_GROUP].astype(F32)
            r = lax.rsqrt(jnp.mean(os_ * os_, axis=-1, keepdims=True) + EPS)
            on_ref[:, s:s + W_GROUP] = (os_ * r * go_ref[:, s:s + W_GROUP]).astype(BF16)
        x2 = x_ref[...] + jnp.dot(on_ref[...], w_ref[...], preferred_element_type=F32)
        x2_ref[...] = x2
        r2 = lax.rsqrt(jnp.mean(x2 * x2, axis=-1, keepdims=True) + EPS)
        h2_ref[...] = (x2 * r2 * gf_ref[...]).astype(BF16)

    row = pl.BlockSpec((bt, D_MODEL), lambda i: (i, 0))
    vec = pl.BlockSpec((1, D_MODEL), lambda i: (0, 0))
    return pl.pallas_call(
        body, name="out_proj", grid=(t // bt,),
        in_specs=[row, vec, pl.BlockSpec((D_MODEL, D_MODEL), lambda i: (0, 0)), row, vec],
        out_specs=(row, row, row),
        out_shape=(jax.ShapeDtypeStruct((t, D_MODEL), BF16), jax.ShapeDtypeStruct((t, D_MODEL), F32),
                   jax.ShapeDtypeStruct((t, D_MODEL), BF16)),
        compiler_params=_params(1),
    )(o, g_out, w_out, x, g_ffn)


def _ffn_gate_up(h2, w_gate_t, w_up_t):
    t = h2.shape[0]
    bt = _row_block(t)
    bn = _divisor_block(D_FF, 1408)

    def body(h_ref, wg_ref, wu_ref, a_ref, u_ref, f_ref):
        a = _dot_nt(h_ref[...], wg_ref[...])
        u = _dot_nt(h_ref[...], wu_ref[...])
        a_ref[...] = a.astype(BF16)
        u_ref[...] = u.astype(BF16)
        f_ref[...] = (a * jax.nn.sigmoid(a) * u).astype(BF16)

    blk = pl.BlockSpec((bt, bn), lambda j, i: (i, j))
    w_blk = pl.BlockSpec((bn, D_MODEL), lambda j, i: (j, 0))
    shape = jax.ShapeDtypeStruct((t, D_FF), BF16)
    return pl.pallas_call(
        body, name="ffn_gate_up", grid=(D_FF // bn, t // bt),
        in_specs=[pl.BlockSpec((bt, D_MODEL), lambda j, i: (i, 0)), w_blk, w_blk],
        out_specs=(blk, blk, blk), out_shape=(shape, shape, shape), compiler_params=_params(2),
    )(h2, w_gate_t, w_up_t)


def _ffn_down_grad(dy16, w_down, a, u):
    t = a.shape[0]
    bt = _row_block(t)
    bn = _divisor_block(D_FF, 1408)

    def body(dy_ref, w_ref, a_ref, u_ref, da_ref, du_ref):
        df = _dot_nt(dy_ref[...], w_ref[...])
        av = a_ref[...].astype(F32)
        sg = jax.nn.sigmoid(av)
        da_ref[...] = (df * u_ref[...].astype(F32) * sg * (1.0 + av * (1.0 - sg))).astype(BF16)
        du_ref[...] = (df * av * sg).astype(BF16)

    blk = pl.BlockSpec((bt, bn), lambda j, i: (i, j))
    shape = jax.ShapeDtypeStruct((t, D_FF), BF16)
    return pl.pallas_call(
        body, name="d_ffn_down", grid=(D_FF // bn, t // bt),
        in_specs=[pl.BlockSpec((bt, D_MODEL), lambda j, i: (i, 0)), pl.BlockSpec((bn, D_MODEL), lambda j, i: (j, 0)),
                  blk, blk],
        out_specs=(blk, blk), out_shape=(shape, shape), compiler_params=_params(2),
    )(dy16, w_down, a, u)


def _ffn_down_loss(f, w_down, x2, target):
    t, w = x2.shape
    bt = _row_block(t)

    def body(f_ref, w_ref, x_ref, t_ref, dy_ref, dy16_ref, loss_ref):
        @pl.when(pl.program_id(0) == 0)
        def _():
            loss_ref[...] = jnp.zeros_like(loss_ref)

        err = (x_ref[...] + jnp.dot(f_ref[...], w_ref[...], preferred_element_type=F32)) - t_ref[...]
        dy = err * (1.0 / w)
        dy_ref[...] = dy
        dy16_ref[...] = dy.astype(BF16)
        loss_ref[...] += 0.5 * jnp.sum(jnp.mean(err * err, axis=-1, keepdims=True), axis=0, keepdims=True)

    row = pl.BlockSpec((bt, w), lambda i: (i, 0))
    return pl.pallas_call(
        body, name="ffn_down_loss", grid=(t // bt,),
        in_specs=[pl.BlockSpec((bt, D_FF), lambda i: (i, 0)), pl.BlockSpec((D_FF, w), lambda i: (0, 0)), row, row],
        out_specs=(row, row, pl.BlockSpec((8, LANES), lambda i: (0, 0))),
        out_shape=(jax.ShapeDtypeStruct((t, w), F32), jax.ShapeDtypeStruct((t, w), BF16),
                   jax.ShapeDtypeStruct((8, LANES), F32)),
        compiler_params=_params(1),
    )(f, w_down, x2, target)


def _adamw(parts, w, m, v, *, name):
    _, rows, cols = w.shape
    br = rows if rows <= 512 else 256
    assert rows % br == 0

    def body(p_ref, w_ref, m_ref, v_ref, g_ref, d_ref, nm_ref, nv_ref):
        g = p_ref[0].astype(F32)
        for r in range(1, N_DEV):
            g = g + p_ref[r].astype(F32)
        m2 = ADAM_B1 * m_ref[0] + (1.0 - ADAM_B1) * g
        v2 = ADAM_B2 * v_ref[0] + (1.0 - ADAM_B2) * jnp.square(g)
        m_hat = m2 / (1.0 - ADAM_B1 ** ADAM_STEP)
        v_hat = v2 / (1.0 - ADAM_B2 ** ADAM_STEP)
        g_ref[0] = g
        d_ref[0] = -ADAM_LR * (m_hat / (jnp.sqrt(v_hat) + ADAM_EPS) + ADAM_WD * w_ref[0])
        nm_ref[0] = m2
        nv_ref[0] = v2

    blk = pl.BlockSpec((1, br, cols), lambda i: (0, i, 0))
    shape = jax.ShapeDtypeStruct((1, rows, cols), F32)
    return pl.pallas_call(
        body, name=name, grid=(rows // br,),
        in_specs=[pl.BlockSpec((N_DEV, br, cols), lambda i: (0, i, 0)), blk, blk, blk],
        out_specs=(blk, blk, blk, blk), out_shape=(shape, shape, shape, shape), compiler_params=_params(1),
    )(parts, w, m, v)


_QA, _KA, _VA, _FA, _QD, _KD, _VD = (0, 512), (512, 1024), (1024, 1536), (1536, 1544), (1544, 2056), (2056, 2568), (2568, 3080)
_MAIN_ORDER = (_QA, _QD, _KA, _KD, _VA, _VD)
MAIN_COLS = 3 * D_MODEL
PROJ_COLS = MAIN_COLS + LANES
COL_SHARDED = ("w_in", "w_gate", "w_up")


def _swap(w):
    return jnp.transpose(w, (0, 2, 1))


def _w_in_to_kernel(w_t):
    main = jnp.concatenate([w_t[a:b] for a, b in _MAIN_ORDER], axis=0)
    forget = jnp.pad(w_t[_FA[0]:_FA[1]], ((0, LANES - N_HEADS_FOX), (0, 0)))
    return main, forget


def _w_in_from_kernel(g_t):
    pos = {span: i * W_GROUP for i, span in enumerate(_MAIN_ORDER)}
    parts = []
    for span in (_QA, _KA, _VA, _FA, _QD, _KD, _VD):
        if span == _FA:
            parts.append(g_t[MAIN_COLS:MAIN_COLS + N_HEADS_FOX])
        else:
            parts.append(g_t[pos[span]:pos[span] + W_GROUP])
    return jnp.concatenate(parts, axis=0)


def _pack_small(vals):
    rows = []
    for name, _, n_rows in SMALL_LAYOUT:
        flat = vals[name].reshape(-1).astype(F32)
        rows.append(jnp.pad(flat, (0, n_rows * LANES - flat.shape[0])).reshape(n_rows, LANES))
    packed = jnp.concatenate(rows, axis=0)
    return jnp.pad(packed, ((0, SMALL_ROWS - packed.shape[0]), (0, 0)))


def _unpack_small(packed, like):
    out = {}
    for name, row, n_rows in SMALL_LAYOUT:
        n = like[name].size
        out[name] = packed[row:row + n_rows].reshape(-1)[:n].reshape(like[name].shape)
    return out


def _device_step(x, target, small, shards):
    bsz, seq, _ = x.shape
    t = bsz * seq
    xf = x.reshape(t, D_MODEL)
    tf = target.reshape(t, D_MODEL)
    row = lambda v: v.reshape(1, -1)
    g_out = jnp.concatenate([small["g_out_fox"], small["g_out_dil"]]).reshape(1, D_MODEL)
    gains = jnp.concatenate(
        [jnp.tile(small[n].reshape(1, HEAD_DIM), (1, 2)) for n in ("g_q_fox", "g_q_dil", "g_k_fox", "g_k_dil")]
        + [jnp.zeros((4, LANES), F32)], axis=0)
    b_pad = jnp.pad(small["b_forget"].reshape(1, N_HEADS_FOX), ((0, 0), (0, LANES - N_HEADS_FOX)))
    rope = _rope_tables(seq)
    tables_qk = _bias_tables(seq, keys_first=False)
    tables_kq = _bias_tables(seq, keys_first=True)

    h1, g_in = _rmsnorm_fwd(xf, row(small["g_mix"]), group=D_MODEL, name="norm_mix",
                            hosted=_ChipGather([(shards["w_in"], False)]))
    w_main_t, w_fa_t = _w_in_to_kernel(g_in.reshape(IN_COLS, D_MODEL))
    w_in_all_t = jnp.concatenate([w_main_t, w_fa_t], axis=0)
    proj = _matmul_nt(h1, w_main_t, name="in_proj", out_dtype=BF16)
    qk, fa = _qk_prep_fwd(proj, h1, w_fa_t, b_pad, gains, rope, seq)
    late = _Exchange([(shards[n], False) for n in ("w_out", "w_gate", "w_up", "w_down")])
    o, lse, g_out_w, g_gate, g_up, g_down = _attn_fwd(qk, proj, tables_qk, seq, hosted=late)
    w_out = g_out_w.reshape(D_MODEL, D_MODEL)
    w_gate_t = g_gate.reshape(D_FF, D_MODEL)
    w_up_t = g_up.reshape(D_FF, D_MODEL)
    w_down = g_down.reshape(D_FF, D_MODEL)
    on, x2, h2 = _out_proj_ffn_norm(o, g_out, w_out, xf, row(small["g_ffn"]))
    a, u, f = _ffn_gate_up(h2, w_gate_t, w_up_t)
    dy, dy16, loss_tile = _ffn_down_loss(f, w_down, x2, tf)

    da, du = _ffn_down_grad(dy16, w_down, a, u)
    gw_down = _matmul_tn(f, dy16, name="gw_down")
    gw_gate_t = _matmul_tn(da, h2, name="gw_gate")
    gw_up_t = _matmul_tn(du, h2, name="gw_up")
    dh2_gate = _matmul_rows(da, w_gate_t, name="d_ffn_gate")
    dx2, dx2_16, dg_ffn = _norm_input_grad([(du, w_up_t, True)], x2, row(small["g_ffn"]), group=D_MODEL,
                                           name="d_ffn_up", out_dtypes=(F32, BF16), resid=dy, init=dh2_gate)
    gw_out = _matmul_tn(on, dx2_16, name="gw_out")
    do, dg_out = _norm_input_grad([(dx2_16, w_out, False)], o, g_out, group=W_GROUP, name="d_out_proj",
                                  out_dtypes=(BF16,))

    shard_rows = lambda g: g.reshape(N_DEV, g.shape[0] // N_DEV, g.shape[1])
    ffn_grads = _Exchange([(shard_rows(g), True) for g in (gw_out, gw_gate_t, gw_up_t, gw_down)])
    dq, dk, dv, dqx, dkx, p_out, p_gate, p_up, p_down = _attn_bwd(qk, proj, tables_kq, o, lse, do, seq, hosted=ffn_grads)
    dproj, dgains, db = _qk_prep_bwd(dq, dk, dqx, dkx, dv, proj, fa, b_pad, gains, rope, seq)
    gw_in_t = _matmul_tn(dproj, h1, name="gw_in")
    in_grad = _Exchange([(shard_rows(_w_in_from_kernel(gw_in_t)), True)])
    dx, dg_mix, p_in = _norm_input_grad([(dproj, w_in_all_t, True)], xf, row(small["g_mix"]), group=D_MODEL,
                                        name="d_in_proj", out_dtypes=(F32,), resid=dx2, hosted=in_grad)

    fold = lambda rows: jnp.sum(rows[:, :HEAD_DIM] + rows[:, HEAD_DIM:], axis=0)
    half = N_PAIRS // 2
    gsmall = {
        "g_mix": dg_mix, "g_ffn": dg_ffn, "g_out_fox": dg_out[0, :W_GROUP], "g_out_dil": dg_out[0, W_GROUP:],
        "g_q_fox": fold(dgains[0:half]), "g_q_dil": fold(dgains[half:N_PAIRS]),
        "g_k_fox": fold(dgains[N_PAIRS:N_PAIRS + half]), "g_k_dil": fold(dgains[N_PAIRS + half:]),
        "b_forget": db[0, :N_HEADS_FOX],
    }
    packed = _pack_small(gsmall).at[LOSS_ROW].set(loss_tile[0])
    (p_small,) = _exchange("small_exchange", [(packed, False)])
    parts = {"w_in": p_in, "w_out": p_out, "w_gate": p_gate, "w_up": p_up, "w_down": p_down}
    return dx.reshape(x.shape), parts, p_small


def kernel(x, g_mix, w_in, b_forget, g_q_fox, g_k_fox, g_q_dil, g_k_dil, g_out_fox, g_out_dil, w_out, g_ffn, w_gate, w_up, w_down, loss_target, m_g_mix, m_w_in, m_b_forget, m_g_q_fox, m_g_k_fox, m_g_q_dil, m_g_k_dil, m_g_out_fox, m_g_out_dil, m_w_out, m_g_ffn, m_w_gate, m_w_up, m_w_down, v_g_mix, v_w_in, v_b_forget, v_g_q_fox, v_g_k_fox, v_g_q_dil, v_g_k_dil, v_g_out_fox, v_g_out_dil, v_w_out, v_g_ffn, v_w_gate, v_w_up, v_w_down):
    args = dict(locals())
    small_names = [name for name, _, _ in SMALL_LAYOUT]
    big_names = ["w_in", "w_out", "w_gate", "w_up", "w_down"]
    small = {n: args[n][0] for n in small_names}

    as_rows = lambda n, w: _swap(w) if n in COL_SHARDED else w
    shards = {n: as_rows(n, args[n])[0].astype(BF16) for n in big_names}
    grad_x, parts, p_small = _device_step(x, loss_target, small, shards)

    grads, deltas, new_m, new_v = {}, {}, {}, {}
    for n in big_names:
        res = _adamw(parts[n], as_rows(n, args[n]), as_rows(n, args["m_" + n]), as_rows(n, args["v_" + n]),
                     name="adamw_" + n)
        grads[n], deltas[n], new_m[n], new_v[n] = [as_rows(n, r) for r in res]
    res = _adamw(p_small, _pack_small(small)[None], _pack_small({n: args["m_" + n][0] for n in small_names})[None],
                 _pack_small({n: args["v_" + n][0] for n in small_names})[None], name="adamw_small")
    loss = res[0][0, LOSS_ROW, 0]
    for dst, packed_res in zip((grads, deltas, new_m, new_v), res):
        for n, val in _unpack_small(packed_res[0], small).items():
            dst[n] = val[None]

    order = ["g_mix", "w_in", "b_forget", "g_q_fox", "g_k_fox", "g_q_dil", "g_k_dil", "g_out_fox", "g_out_dil",
             "w_out", "g_ffn", "w_gate", "w_up", "w_down"]
    return (loss, grad_x, *[grads[n] for n in order], *[deltas[n] for n in order],
            *[new_m[n] for n in order], *[new_v[n] for n in order])
```

```python
import functools
import math

import jax
import jax.numpy as jnp
import numpy as np
from jax import lax
from jax.experimental import pallas as pl
from jax.experimental.pallas import tpu as pltpu

F32 = jnp.float32
BF16 = jnp.bfloat16

D_MODEL = 1024
HEAD_DIM = 64
LANES = 128
N_PAIRS = D_MODEL // LANES
N_HEADS = 2 * N_PAIRS
N_HEADS_FOX = 8
W_GROUP = 512
D_FF = 2816
IN_COLS = 3080
DILATION_PAIRS = ((128, 1), (512, 4), (2048, 16))
ROPE_THETA = 500000.0
ROPE_DIM = 16
ROPE_HALF = ROPE_DIM // 2
EPS = 1e-6
NEG = -1e30
LOG2E = 1.4426950408889634
LN2 = 0.6931471805599453
AUG_ONE = 0
AUG_C = 3
N_DEV = 8

ADAM_LR = 0.001
ADAM_B1 = 0.9
ADAM_B2 = 0.999
ADAM_EPS = 1e-08
ADAM_WD = 0.01
ADAM_STEP = 10

ROW_BLOCK = 512
TOKEN_STEP = 2048
WIDE_BLOCK = D_FF // 2
ATT_BLOCK = 512
ATT_GROUP = 4
VMEM_LIMIT = 56 * 1024 * 1024
MATMUL_VMEM_BUDGET = 44 * 1024 * 1024

SMALL_ROWS = 32
SMALL_LAYOUT = (("g_mix", 0, 8), ("g_ffn", 8, 8), ("g_out_fox", 16, 4), ("g_out_dil", 20, 4),
                ("g_q_fox", 24, 1), ("g_k_fox", 25, 1), ("g_q_dil", 26, 1), ("g_k_dil", 27, 1),
                ("b_forget", 28, 1))
LOSS_ROW = 29


def _params(n_grid):
    return pltpu.CompilerParams(dimension_semantics=("arbitrary",) * n_grid, vmem_limit_bytes=VMEM_LIMIT)


def _divisor_block(n, cap):
    best = None
    for b in range(LANES, min(n, cap) + 1, LANES):
        if n % b == 0:
            best = b
    assert best is not None, n
    return best


def _split_dot(a, b_exact, terms):
    acc = None
    rest = a
    for _ in range(terms):
        hi = rest.astype(BF16)
        part = jnp.dot(hi, b_exact, preferred_element_type=F32)
        acc = part if acc is None else acc + part
        rest = rest - hi.astype(F32)
    return acc


def _split_dot_nt(a_exact, b, terms):
    acc = None
    rest = b
    for _ in range(terms):
        hi = rest.astype(BF16)
        part = _dot_nt(a_exact, hi)
        acc = part if acc is None else acc + part
        rest = rest - hi.astype(F32)
    return acc


def _dot_nt(a, b):
    return lax.dot_general(a, b, (((1,), (1,)), ((), ())), preferred_element_type=F32)


def _dot_tn(a, b):
    return lax.dot_general(a, b, (((0,), (0,)), ((), ())), preferred_element_type=F32)


class _Exchange:
    def __init__(self, items):
        self.items = items
        self.n = len(items)
        self.arrays = [a for a, _ in items]
        self.out_shape = [jax.ShapeDtypeStruct((N_DEV,) + tuple(a.shape[1:] if sc else a.shape), a.dtype)
                          for a, sc in items]
        self.specs = [pl.BlockSpec(memory_space=pl.ANY)] * self.n
        self.scratch = [pltpu.SemaphoreType.DMA((self.n, N_DEV - 1)), pltpu.SemaphoreType.DMA((self.n, N_DEV - 1)),
                        pltpu.SemaphoreType.DMA((self.n,))]

    def run(self, ins, outs, sems, first, last, compute):
        send_sems, recv_sems, local_sems = sems
        x, y, c = lax.axis_index("x"), lax.axis_index("y"), lax.axis_index("c")
        me = 4 * x + 2 * y + c
        local, remote = [], []
        for k, (_, scatter) in enumerate(self.items):
            own = ins[k].at[me] if scatter else ins[k]
            local.append(pltpu.make_async_copy(own, outs[k].at[me], local_sems.at[k]))
        for r in range(1, N_DEV):
            px = 1 - x if r & 4 else x
            py = 1 - y if r & 2 else y
            pc = 1 - c if r & 1 else c
            peer = 4 * px + 2 * py + pc
            for k, (_, scatter) in enumerate(self.items):
                src = ins[k].at[peer] if scatter else ins[k]
                remote.append(pltpu.make_async_remote_copy(
                    src_ref=src, dst_ref=outs[k].at[me],
                    send_sem=send_sems.at[k, r - 1], recv_sem=recv_sems.at[k, r - 1],
                    device_id=(px, py, pc), device_id_type=pl.DeviceIdType.MESH))

        def start():
            for cp in local + remote:
                cp.start()

        def finish():
            for cp in remote:
                cp.wait_recv()
            for cp in remote:
                cp.wait_send()
            for cp in local:
                cp.wait()

        _run_phases(first, last, start, compute, finish)


def _run_phases(first, last, start, compute, finish):
    if first is None:
        start()
        compute()
        finish()
    else:
        pl.when(first)(start)
        compute()
        pl.when(last)(finish)


class _ChipGather(_Exchange):
    def run(self, ins, outs, sems, first, last, compute):
        send_sems, recv_sems, local_sems = sems
        x, y, c = lax.axis_index("x"), lax.axis_index("y"), lax.axis_index("c")
        sibling = (x, y, 1 - c)
        chips = [(1 - x, y), (x, 1 - y), (1 - x, 1 - y)]
        slot = lambda px, py, pc: 4 * px + 2 * py + pc

        def copy(k, n, src, dst_slot, to):
            return pltpu.make_async_remote_copy(
                src_ref=src, dst_ref=outs[k].at[dst_slot], send_sem=send_sems.at[k, n], recv_sem=recv_sems.at[k, n],
                device_id=to, device_id_type=pl.DeviceIdType.MESH)

        local, own, passed, arrivals = [], [], [], []
        for k in range(self.n):
            me = slot(x, y, c)
            local.append(pltpu.make_async_copy(ins[k], outs[k].at[me], local_sems.at[k]))
            own.append(copy(k, 0, ins[k], me, sibling))
            arrivals.append(copy(k, 0, ins[k], slot(*sibling), sibling))
            for j, chip in enumerate(chips):
                theirs = slot(*chip, c)
                own.append(copy(k, 1 + j, ins[k], me, (*chip, c)))
                passed.append((copy(k, 1 + j, ins[k], theirs, sibling),
                               copy(k, 4 + j, outs[k].at[theirs], theirs, sibling)))
                arrivals.append(copy(k, 4 + j, ins[k], slot(*chip, 1 - c), sibling))

        def start():
            for cp in local + own:
                cp.start()

        def finish():
            for landed, onward in passed:
                landed.wait_recv()
                onward.start()
            for cp in arrivals:
                cp.wait_recv()
            for cp in own + [onward for _, onward in passed]:
                cp.wait_send()
            for cp in local:
                cp.wait()

        _run_phases(first, last, start, compute, finish)


def _grid_ends(grid):
    ids = [pl.program_id(d) for d in range(len(grid))]
    first = functools.reduce(jnp.logical_and, [i == 0 for i in ids])
    last = functools.reduce(jnp.logical_and, [i == g - 1 for i, g in zip(ids, grid)])
    return first, last


def _host(core, n_in, n_out, n_scratch, hosted, grid):
    if hosted is None:
        return core
    nh = hosted.n

    def body(*refs):
        ins, rest = refs[:n_in], refs[n_in:]
        h_ins, rest = rest[:nh], rest[nh:]
        outs, rest = rest[:n_out], rest[n_out:]
        h_outs, rest = rest[:nh], rest[nh:]
        scratch, sems = rest[:n_scratch], rest[n_scratch:]
        first, last = _grid_ends(grid)
        hosted.run(h_ins, h_outs, sems, first, last, lambda: core(*ins, *outs, *scratch))

    return body


def _hosted_parts(hosted):
    if hosted is None:
        return [], [], [], []
    return list(hosted.specs), list(hosted.out_shape), list(hosted.arrays), list(hosted.scratch)


def _exchange(name, items):
    ex = _Exchange(items)
    n = ex.n

    def body(*refs):
        ex.run(refs[:n], refs[n:2 * n], refs[2 * n:], None, None, lambda: None)

    return pl.pallas_call(
        body, name=name, out_shape=tuple(ex.out_shape), in_specs=ex.specs, out_specs=tuple(ex.specs),
        scratch_shapes=ex.scratch,
    )(*ex.arrays)


def _matmul_blocks(t, k, n, a_bytes, o_bytes):
    for bt, cap in ((_row_block(t), WIDE_BLOCK), (_row_block(t), ROW_BLOCK), (ROW_BLOCK, ROW_BLOCK)):
        bn = _divisor_block(n, cap)
        if 2 * (bt * k * a_bytes + bn * k * 2 + bt * bn * o_bytes) <= MATMUL_VMEM_BUDGET:
            return bt, bn
    return ROW_BLOCK, _divisor_block(n, 2 * LANES)


def _matmul_nt(a, w, *, name, out_dtype):
    t, k = a.shape
    n = w.shape[0]
    assert w.shape[1] == k
    bt, bn = _matmul_blocks(t, k, n, a.dtype.itemsize, jnp.dtype(out_dtype).itemsize)

    def body(a_ref, w_ref, o_ref):
        o_ref[...] = _dot_nt(a_ref[...], w_ref[...]).astype(o_ref.dtype)

    return pl.pallas_call(
        body, name=name, grid=(t // bt, n // bn),
        in_specs=[pl.BlockSpec((bt, k), lambda i, j: (i, 0)), pl.BlockSpec((bn, k), lambda i, j: (j, 0))],
        out_specs=pl.BlockSpec((bt, bn), lambda i, j: (i, j)),
        out_shape=jax.ShapeDtypeStruct((t, n), out_dtype), compiler_params=_params(2),
    )(a, w)


def _matmul_rows(a, w, *, name):
    t, k = a.shape
    n = w.shape[1]
    assert w.shape[0] == k
    bt = _row_block(t)

    def body(a_ref, w_ref, o_ref):
        o_ref[...] = jnp.dot(a_ref[...], w_ref[...], preferred_element_type=F32)

    return pl.pallas_call(
        body, name=name, grid=(t // bt,),
        in_specs=[pl.BlockSpec((bt, k), lambda i: (i, 0)), pl.BlockSpec((k, n), lambda i: (0, 0))],
        out_specs=pl.BlockSpec((bt, n), lambda i: (i, 0)),
        out_shape=jax.ShapeDtypeStruct((t, n), F32), compiler_params=_params(1),
    )(a, w)


def _matmul_tn(a, b, *, name):
    t, m = a.shape
    n = b.shape[1]
    bt = TOKEN_STEP if t % TOKEN_STEP == 0 else ROW_BLOCK
    bm = _divisor_block(m, WIDE_BLOCK)
    bn = _divisor_block(n, WIDE_BLOCK)
    steps = t // bt

    def body(a_ref, b_ref, o_ref, acc):
        step = pl.program_id(2)

        @pl.when(step == 0)
        def _():
            acc[...] = jnp.zeros_like(acc)

        acc[...] += _dot_tn(a_ref[...], b_ref[...])

        @pl.when(step == steps - 1)
        def _():
            o_ref[...] = acc[...].astype(o_ref.dtype)

    return pl.pallas_call(
        body, name=name, grid=(m // bm, n // bn, steps),
        in_specs=[pl.BlockSpec((bt, bm), lambda i, j, s: (s, i)), pl.BlockSpec((bt, bn), lambda i, j, s: (s, j))],
        out_specs=pl.BlockSpec((bm, bn), lambda i, j, s: (i, j)),
        out_shape=jax.ShapeDtypeStruct((m, n), BF16), scratch_shapes=[pltpu.VMEM((bm, bn), F32)],
        compiler_params=_params(3),
    )(a, b)


def _rmsnorm_fwd(x, g, *, group, name, hosted=None):
    t, w = x.shape
    bt = ROW_BLOCK

    def body(x_ref, g_ref, o_ref):
        for s in range(0, w, group):
            xs = x_ref[:, s:s + group].astype(F32)
            r = lax.rsqrt(jnp.mean(xs * xs, axis=-1, keepdims=True) + EPS)
            o_ref[:, s:s + group] = (xs * r * g_ref[:, s:s + group]).astype(o_ref.dtype)

    grid = (t // bt,)
    h_specs, h_shapes, h_args, h_scratch = _hosted_parts(hosted)
    res = pl.pallas_call(
        _host(body, 2, 1, 0, hosted, grid), name=name, grid=grid,
        in_specs=[pl.BlockSpec((bt, w), lambda i: (i, 0)), pl.BlockSpec((1, w), lambda i: (0, 0))] + h_specs,
        out_specs=tuple([pl.BlockSpec((bt, w), lambda i: (i, 0))] + h_specs),
        out_shape=tuple([jax.ShapeDtypeStruct((t, w), BF16)] + h_shapes),
        scratch_shapes=h_scratch, compiler_params=_params(1),
    )(x, g, *h_args)
    return res if hosted else res[0]


def _norm_input_grad(terms, x, g, *, group, name, out_dtypes, resid=None, init=None, hosted=None, k_chunks=1):
    t, w = x.shape
    n_terms = len(terms)
    kc = [a.shape[1] // k_chunks for a, _, _ in terms]
    per_row = sum(c * a.dtype.itemsize for c, (a, _, _) in zip(kc, terms)) + w * (x.dtype.itemsize + 4 * (resid is not None))
    per_row += w * (sum(jnp.dtype(dt).itemsize for dt in out_dtypes) + 4 * (init is not None))
    fixed = 2 * sum(w * c * 2 for c in kc)
    bt = next(b for b in (2 * ROW_BLOCK, ROW_BLOCK, ROW_BLOCK // 2, ROW_BLOCK // 4)
              if t % b == 0 and fixed + 2 * b * per_row + 5 * b * w * 4 <= MATMUL_VMEM_BUDGET)
    resid_at = 2 * n_terms + 2
    init_at = resid_at + (resid is not None)
    n_in = init_at + (init is not None)
    grid = (t // bt, k_chunks)

    def body(*refs):
        x_ref, g_ref = refs[2 * n_terms], refs[2 * n_terms + 1]
        dx_refs, dg_ref, dh_ref = refs[n_in:-2], refs[-2], refs[-1]
        chunk = pl.program_id(1)

        @pl.when((pl.program_id(0) == 0) & (chunk == 0))
        def _():
            dg_ref[...] = jnp.zeros_like(dg_ref)

        part = None
        for k in range(n_terms):
            if terms[k][2]:
                term = jnp.dot(refs[2 * k][...], refs[2 * k + 1][...], preferred_element_type=F32)
            else:
                term = _dot_nt(refs[2 * k][...], refs[2 * k + 1][...])
            part = term if part is None else part + term

        @pl.when(chunk == 0)
        def _():
            dh_ref[...] = part if init is None else refs[init_at][...] + part

        @pl.when(chunk > 0)
        def _():
            dh_ref[...] += part

        @pl.when(chunk == k_chunks - 1)
        def _():
            for s in range(0, w, group):
                xs = x_ref[:, s:s + group].astype(F32)
                dhs = dh_ref[:, s:s + group]
                r = lax.rsqrt(jnp.mean(xs * xs, axis=-1, keepdims=True) + EPS)
                xh = xs * r
                dg_ref[:, s:s + group] += jnp.sum(dhs * xh, axis=0, keepdims=True)
                dxh = dhs * g_ref[:, s:s + group]
                dx = r * (dxh - xh * jnp.mean(dxh * xh, axis=-1, keepdims=True))
                if resid is not None:
                    dx = refs[resid_at][:, s:s + group] + dx
                for dx_ref in dx_refs:
                    dx_ref[:, s:s + group] = dx.astype(dx_ref.dtype)

    row = pl.BlockSpec((bt, w), lambda i, k: (i, 0))
    vec = pl.BlockSpec((1, w), lambda i, k: (0, 0))
    in_specs, args = [], []
    for c, (a, wt, w_is_kn) in zip(kc, terms):
        assert wt.shape == ((a.shape[1], w) if w_is_kn else (w, a.shape[1]))
        w_spec = pl.BlockSpec((c, w), lambda i, k: (k, 0)) if w_is_kn else pl.BlockSpec((w, c), lambda i, k: (0, k))
        in_specs += [pl.BlockSpec((bt, c), lambda i, k: (i, k)), w_spec]
        args += [a, wt]
    extra = [r for r in (resid, init) if r is not None]
    in_specs += [row, vec] + [row] * len(extra)
    args += [x, g] + extra
    h_specs, h_shapes, h_args, h_scratch = _hosted_parts(hosted)
    return pl.pallas_call(
        _host(body, n_in, len(out_dtypes) + 1, 1, hosted, grid), name=name, grid=grid, in_specs=in_specs + h_specs,
        out_specs=tuple([row] * len(out_dtypes) + [vec] + h_specs),
        out_shape=tuple([jax.ShapeDtypeStruct((t, w), dt) for dt in out_dtypes] + [jax.ShapeDtypeStruct((1, w), F32)]
                        + h_shapes),
        scratch_shapes=[pltpu.VMEM((bt, w), F32)] + h_scratch, compiler_params=_params(2),
    )(*args, *h_args)


def _tile_plan(tile):
    is_q = tile < N_PAIRS
    is_dil = (tile % N_PAIRS) >= N_PAIRS // 2
    return is_q, is_dil, (0 if is_q else 2) + (1 if is_dil else 0)


def _proj_tile(kind, pair):
    return kind * N_PAIRS + pair


def _segment_ones():
    lane = np.arange(LANES)
    return jnp.asarray((lane[:, None] // HEAD_DIM) == (lane[None, :] // HEAD_DIM), BF16)


def _rope_tables(seq):
    inv_freq = jnp.power(jnp.float32(ROPE_THETA), -jnp.arange(ROPE_HALF, dtype=F32) * 2.0 / ROPE_DIM)
    ang = jnp.arange(seq).astype(F32)[:, None] * inv_freq[None, :]
    cos, sin = jnp.cos(ang), jnp.sin(ang)
    ones = jnp.ones((seq, HEAD_DIM - ROPE_DIM), F32)
    zeros = jnp.zeros((seq, HEAD_DIM - ROPE_DIM), F32)
    zh = jnp.zeros((seq, ROPE_HALF), F32)
    cos_t = jnp.concatenate([cos, cos, ones], axis=1)
    sin_a = jnp.concatenate([-sin, zh, zeros], axis=1)
    sin_b = jnp.concatenate([zh, sin, zeros], axis=1)
    return tuple(jnp.tile(tab, (1, 2)) for tab in (cos_t, sin_a, sin_b))


def _log_sigmoid(z):
    return jnp.minimum(z, 0.0) - jnp.log1p(jnp.exp(-jnp.abs(z)))


def _aug_placement():
    place = np.zeros((N_PAIRS, LANES, LANES), np.float32)
    for is_k in range(2):
        for pair in range(N_PAIRS // 2):
            for e in range(2):
                other = HEAD_DIM * (1 - e)
                ones_at = other + (AUG_C if is_k else AUG_ONE)
                c_at = other + (AUG_ONE if is_k else AUG_C)
                for n in range(3):
                    place[4 * is_k + pair, N_HEADS_FOX * n + 2 * pair + e, c_at + n] = -1.0 if is_k else 1.0
                    place[4 * is_k + pair, 3 * N_HEADS_FOX, ones_at + n] = 1.0
    return jnp.asarray(place, BF16)


def _qk_prep_fwd(proj, h1, w_fa_t, b_pad, gains, rope, seq):
    t = proj.shape[0]
    bt = ROW_BLOCK
    nsb = seq // bt
    seg = _segment_ones()
    rr = np.arange(bt)
    tri = jnp.asarray(rr[:, None] <= rr[None, :], BF16)

    def body(p_ref, h_ref, wfa_ref, b_ref, g_ref, cos_ref, sa_ref, sb_ref, seg_ref, tri_ref, place_ref,
             qk_ref, fa_ref, carry):
        @pl.when(pl.program_id(0) % nsb == 0)
        def _():
            carry[...] = jnp.zeros_like(carry)

        lane = lax.broadcasted_iota(jnp.int32, (bt, LANES), 1)
        fa = _dot_nt(h_ref[...], wfa_ref[...])
        fa_ref[...] = fa
        logf = jnp.where(lane < N_HEADS_FOX, _log_sigmoid(fa + b_ref[...]), 0.0)
        c_rows = _split_dot(logf.T[0:N_HEADS_FOX, :], tri_ref[...], 3) + carry[:, 0:1]
        carry[...] = jnp.broadcast_to(c_rows[:, bt - 1:bt], carry.shape)
        cblk = jnp.concatenate([c_rows, jnp.zeros((LANES - N_HEADS_FOX, bt), F32)], axis=0).T
        packed = jnp.where(lane == 3 * N_HEADS_FOX, 1.0, 0.0)
        rest = cblk * LOG2E
        for n in range(3):
            term = rest.astype(BF16).astype(F32)
            packed = packed + (pltpu.roll(term, N_HEADS_FOX * n, 1) if n else term)
            rest = rest - term
        packed = packed.astype(BF16)
        low = lane < HEAD_DIM

        for tile in range(2 * N_PAIRS):
            is_q, is_dil, grow = _tile_plan(tile)
            pair = tile % N_PAIRS
            src = _proj_tile(0 if is_q else 1, pair) * LANES
            xs = p_ref[:, src:src + LANES].astype(F32)
            r = lax.rsqrt(_split_dot(xs * xs, seg_ref[...], 2) * (1.0 / HEAD_DIM) + EPS)
            yv = xs * r * g_ref[grow:grow + 1, :]
            if is_dil:
                yv = (yv * cos_ref[...] + pltpu.roll(yv, LANES - ROPE_HALF, 1) * sa_ref[...]
                      + pltpu.roll(yv, ROPE_HALF, 1) * sb_ref[...])
                aug = jnp.zeros((bt, LANES), F32)
            else:
                aug = jnp.dot(packed, place_ref[(0 if is_q else N_PAIRS // 2) + pair], preferred_element_type=F32)
            if is_q:
                yv = yv * (HEAD_DIM ** -0.5 * LOG2E)
            dst = ((0 if is_q else N_HEADS) + 2 * pair) * LANES
            qk_ref[:, dst:dst + LANES] = jnp.where(low, yv, aug).astype(BF16)
            qk_ref[:, dst + LANES:dst + 2 * LANES] = jnp.where(low, aug, yv).astype(BF16)

    row128 = pl.BlockSpec((bt, LANES), lambda i: (i, 0))
    rope_spec = pl.BlockSpec((bt, LANES), lambda i: (i % nsb, 0))
    const = lambda shape: pl.BlockSpec(shape, lambda i: (0,) * len(shape))
    return pl.pallas_call(
        body, name="qk_prep_fwd", grid=(t // bt,),
        in_specs=[pl.BlockSpec((bt, 2 * D_MODEL), lambda i: (i, 0)), pl.BlockSpec((bt, D_MODEL), lambda i: (i, 0)),
                  const((LANES, D_MODEL)), const((1, LANES)), const((8, LANES)), rope_spec, rope_spec, rope_spec,
                  const((LANES, LANES)), const((bt, bt)), const((N_PAIRS, LANES, LANES))],
        out_specs=(pl.BlockSpec((bt, 2 * N_HEADS * LANES), lambda i: (i, 0)), row128),
        out_shape=(jax.ShapeDtypeStruct((t, 2 * N_HEADS * LANES), BF16), jax.ShapeDtypeStruct((t, LANES), F32)),
        scratch_shapes=[pltpu.VMEM((8, LANES), F32)], compiler_params=_params(1),
    )(proj, h1, w_fa_t, b_pad, gains, *rope, seg, tri, _aug_placement())


def _qk_prep_bwd(dq, dk, dqx, dkx, dv, proj, fa, b_pad, gains, rope, seq):
    t = proj.shape[0]
    bt = ROW_BLOCK
    nsb = seq // bt
    nblk = t // bt
    seg = _segment_ones()
    rr = np.arange(bt)
    triu = jnp.asarray(rr[:, None] >= rr[None, :], BF16)

    def body(dq_ref, dk_ref, dqx_ref, dkx_ref, dv_ref, p_ref, fa_ref, b_ref, g_ref, cos_ref, sa_ref, sb_ref, seg_ref,
             triu_ref, dp_ref, dg_ref, db_ref, carry):
        step = pl.program_id(0)

        @pl.when(step == 0)
        def _():
            dg_ref[...] = jnp.zeros_like(dg_ref)
            db_ref[...] = jnp.zeros_like(db_ref)

        @pl.when(step % nsb == 0)
        def _():
            carry[...] = jnp.zeros_like(carry)

        for tile in range(2 * N_PAIRS):
            is_q, is_dil, grow = _tile_plan(tile)
            first = _proj_tile(0 if is_q else 1, tile % N_PAIRS) * LANES
            cols = slice(first, first + LANES)
            src = dq_ref if is_q else dk_ref
            half = slice((tile % N_PAIRS) * LANES, (tile % N_PAIRS + 1) * LANES)
            dy = src[:, half]
            dy = dy * (HEAD_DIM ** -0.5 if is_q else LN2)
            if is_dil:
                dy = (dy * cos_ref[...] + pltpu.roll(dy * sa_ref[...], ROPE_HALF, 1)
                      + pltpu.roll(dy * sb_ref[...], LANES - ROPE_HALF, 1))
            xs = p_ref[:, cols].astype(F32)
            r = lax.rsqrt(_split_dot(xs * xs, seg_ref[...], 2) * (1.0 / HEAD_DIM) + EPS)
            xh = xs * r
            dg_ref[tile:tile + 1, :] += jnp.sum(dy * xh, axis=0, keepdims=True)
            dxh = dy * g_ref[grow:grow + 1, :]
            seg_mean = _split_dot(dxh * xh, seg_ref[...], 2) * (1.0 / HEAD_DIM)
            dp_ref[:, cols] = (r * (dxh - xh * seg_mean)).astype(BF16)

        lane = lax.broadcasted_iota(jnp.int32, (bt, LANES), 1)
        dc = jnp.zeros((bt, LANES), F32)
        for h in range(N_HEADS_FOX):
            other = (h // 2) * LANES + HEAD_DIM * (1 - h % 2)
            row_sum = dqx_ref[:, other + AUG_C:other + AUG_C + 1]
            col_sum = dkx_ref[:, other + AUG_ONE:other + AUG_ONE + 1]
            dc = jnp.where(lane == h, row_sum - col_sum, dc)
        d_rows = _split_dot(dc.T[0:N_HEADS_FOX, :], triu_ref[...], 3) + carry[:, 0:1]
        carry[...] = jnp.broadcast_to(d_rows[:, 0:1], carry.shape)
        dlogf = jnp.concatenate([d_rows, jnp.zeros((LANES - N_HEADS_FOX, bt), F32)], axis=0).T
        z = fa_ref[...] + b_ref[...]
        dfa = dlogf * (1.0 / (1.0 + jnp.exp(z)))
        db_ref[0:1, :] += jnp.sum(dfa, axis=0, keepdims=True)
        for group in range(2):
            first = _proj_tile(2, group * (N_PAIRS // 2)) * LANES
            dp_ref[:, first:first + W_GROUP] = dv_ref[:, group * W_GROUP:(group + 1) * W_GROUP]
        dp_ref[:, MAIN_COLS:PROJ_COLS] = dfa.astype(BF16)

    rev = lambda i: nblk - 1 - i
    row = lambda w: pl.BlockSpec((bt, w), lambda i: (rev(i), 0))
    rope_spec = pl.BlockSpec((bt, LANES), lambda i: (rev(i) % nsb, 0))
    const = lambda shape: pl.BlockSpec(shape, lambda i: (0, 0))
    return pl.pallas_call(
        body, name="qk_prep_bwd", grid=(nblk,),
        in_specs=[row(D_MODEL), row(D_MODEL), row(W_GROUP), row(W_GROUP), row(D_MODEL), row(2 * D_MODEL), row(LANES),
                  const((1, LANES)), const((8, LANES)), rope_spec, rope_spec, rope_spec, const((LANES, LANES)),
                  const((bt, bt))],
        out_specs=(row(PROJ_COLS), const((2 * N_PAIRS, LANES)), const((8, LANES))),
        out_shape=(jax.ShapeDtypeStruct((t, PROJ_COLS), BF16),
                   jax.ShapeDtypeStruct((2 * N_PAIRS, LANES), F32), jax.ShapeDtypeStruct((8, LANES), F32)),
        scratch_shapes=[pltpu.VMEM((8, LANES), F32)], compiler_params=_params(1),
    )(dq, dk, dqx, dkx, dv, proj, fa, b_pad, gains, *rope, seg, triu)


def _bias_tables(seq, keys_first):
    nb = seq // ATT_BLOCK
    idx = np.arange(ATT_BLOCK)
    q_idx, k_idx = (idx[None, None, :], idx[None, :, None]) if keys_first else (idx[None, :, None], idx[None, None, :])
    dist = np.arange(nb)[:, None, None] * ATT_BLOCK + q_idx - k_idx
    causal = dist >= 0
    count = np.zeros(dist.shape, np.int32)
    for window, dilation in DILATION_PAIRS:
        count = count + (causal & (dist % dilation == 0) & (dist <= window))
    fox = np.where(causal, 0.0, NEG)
    dil = np.where(count == 3, math.log2(3.0), np.where(count == 2, 1.0, np.where(count == 1, 0.0, NEG)))
    return jnp.asarray(np.stack([fox, dil], axis=0), F32)


def _attn_specs(seq):
    nb = seq // ATT_BLOCK
    col = pl.BlockSpec((seq, LANES), lambda b, j: (b, j))
    heads = lambda off: pl.BlockSpec((seq, 2 * LANES), lambda b, j: (b, off + j))
    v_spec = pl.BlockSpec((seq, LANES), lambda b, j: (b, _proj_tile(2, j)))
    table_spec = pl.BlockSpec((1, nb, ATT_BLOCK, ATT_BLOCK), lambda b, j: (j // (N_PAIRS // 2), 0, 0, 0))
    return col, heads, v_spec, table_spec


def _head_lanes(e, shape, axis):
    pos = lax.broadcasted_iota(jnp.int32, shape, axis)
    return pos < HEAD_DIM if e == 0 else pos >= HEAD_DIM


def _attn_fwd(qk, proj, tables, seq, hosted=None):
    t = qk.shape[0]
    nb = seq // ATT_BLOCK
    blk = ATT_BLOCK

    def body(q_ref, k_ref, v_ref, tab_ref, o_ref, lse_ref):
        mine = [_head_lanes(e, (seq, LANES), 1) for e in range(2)]
        lane = lax.broadcasted_iota(jnp.int32, (seq, LANES), 1)
        v_aug = [jnp.where(mine[e], v_ref[...], (lane == HEAD_DIM * (1 - e)).astype(BF16)) for e in range(2)]
        def scores(i, e):
            heads_e = slice(e * LANES, (e + 1) * LANES)
            s = _dot_nt(q_ref[i * blk:(i + 1) * blk, heads_e], k_ref[0:(i + 1) * blk, heads_e])
            s = jnp.concatenate([s[:, jj * blk:(jj + 1) * blk] + tab_ref[0, i - jj] for jj in range(i + 1)], axis=1)
            return s, jnp.max(s, axis=1, keepdims=True)

        chains = [(i, e) for i in reversed(range(nb)) for e in range(2)]
        ahead = 2
        pending = [scores(*chain) for chain in chains[:ahead]]
        done = {}
        for n, (i, e) in enumerate(chains):
            s, m = pending.pop(0)
            if n + ahead < len(chains):
                pending.append(scores(*chains[n + ahead]))
            acc = jnp.dot(jnp.exp2(s - m).astype(BF16), v_aug[e][0:(i + 1) * blk], preferred_element_type=F32)
            ones_at = HEAD_DIM * (1 - e)
            l = acc[:, ones_at:ones_at + 1]
            done[e] = (acc / l, m + jnp.log2(l))
            if e == 1:
                rows = slice(i * blk, (i + 1) * blk)
                o_ref[rows, :] = jnp.where(mine[0][rows], done[0][0], done[1][0]).astype(o_ref.dtype)
                lse_ref[rows, :] = jnp.where(mine[0][rows], done[0][1], done[1][1])

    col, heads, v_spec, table_spec = _attn_specs(seq)
    grid = (t // seq, N_PAIRS)
    h_specs, h_shapes, h_args, h_scratch = _hosted_parts(hosted)
    return pl.pallas_call(
        _host(body, 4, 2, 0, hosted, grid), name="attn_fwd", grid=grid,
        in_specs=[heads(0), heads(N_PAIRS), v_spec, table_spec] + h_specs,
        out_specs=tuple([col, col] + h_specs),
        out_shape=tuple([jax.ShapeDtypeStruct((t, D_MODEL), BF16), jax.ShapeDtypeStruct((t, D_MODEL), F32)] + h_shapes),
        scratch_shapes=h_scratch, compiler_params=_params(2),
    )(qk, qk, proj, tables, *h_args)


def _attn_bwd(qk, proj, tables, o, lse, do, seq, hosted=None):
    t = qk.shape[0]
    nb = seq // ATT_BLOCK
    blk = ATT_BLOCK
    group = math.gcd(nb, ATT_GROUP)

    def body(q_ref, k_ref, v_ref, tab_ref, o_ref, lse_ref, do_ref,
             dq_ref, dk_ref, dv_ref, dqx_ref, dkx_ref, dk_acc, dv_acc):
        mine = [_head_lanes(e, (blk, LANES), 1) for e in range(2)]
        top = _head_lanes(0, (LANES, blk), 0)
        head_rows = lax.broadcasted_iota(jnp.int32, (8, LANES), 0)
        head_of_lane = lax.broadcasted_iota(jnp.int32, (8, LANES), 1) // HEAD_DIM
        head_sel = (head_rows == head_of_lane).astype(BF16)
        dk_acc[...] = jnp.zeros_like(dk_acc)
        dv_acc[...] = jnp.zeros_like(dv_acc)

        def block_rows(i):
            return pl.ds(pl.multiple_of(i * blk, blk), blk)

        def q_group(g, _):
            base = g * group
            qs, doe, delta, lse_e = [], [], [], []
            for b in range(group):
                rows = block_rows(base + b)
                qs.append([q_ref[rows, e * LANES:(e + 1) * LANES] for e in range(2)])
                do_blk = do_ref[rows, :]
                doe.append([jnp.where(mine[e], do_blk, jnp.zeros_like(do_blk)) for e in range(2)])
                delta_t = _split_dot_nt(head_sel, do_blk.astype(F32) * o_ref[rows, :].astype(F32), 3)
                lse_t = _split_dot_nt(head_sel, lse_ref[rows, :], 3) * (1.0 / HEAD_DIM)
                delta.append([delta_t[e:e + 1, :] for e in range(2)])
                lse_e.append([lse_t[e:e + 1, :] for e in range(2)])

            def key_block(dq_t, jj, members):
                krows = block_rows(jj)
                v = v_ref[krows, :]
                dq_t = [list(d) for d in dq_t]
                lo, hi = slice(0, blk // 2), slice(blk // 2, blk)
                dv_part = [None, None]
                add = lambda acc, part: part if acc is None else acc + part

                def probs(k_sub, v_sub, keys, queries, b, e, dist):
                    q_sub, do_sub = qs[b][e][queries], doe[b][e][queries]
                    p_t = jnp.exp2(_dot_nt(k_sub, q_sub) + tab_ref[0, dist, keys, queries] - lse_e[b][e][:, queries])
                    ds_t = (p_t * (_dot_nt(v_sub, do_sub) - delta[b][e][:, queries])).astype(BF16)
                    return p_t.astype(BF16), ds_t, q_sub, do_sub, k_sub

                def outputs(tile):
                    p_t, ds_t, q_sub, do_sub, k_sub = tile
                    return (jnp.dot(p_t, do_sub, preferred_element_type=F32),
                            jnp.dot(ds_t, q_sub, preferred_element_type=F32), _dot_tn(k_sub, ds_t))

                for e in range(2):
                    k_e = k_ref[krows, e * LANES:(e + 1) * LANES]
                    dk_part = [None, None]
                    tiles = []
                    for b, dist in members:
                        if isinstance(dist, int) and dist == 0:
                            tiles.append((b, probs(k_e[lo], v[lo], lo, slice(0, blk), b, e, dist),
                                          probs(k_e[hi], v[hi], hi, hi, b, e, dist)))
                        else:
                            tiles.append((b, probs(k_e, v, slice(0, blk), slice(0, blk), b, e, dist), None))
                    for b, first, second in tiles:
                        if second is not None:
                            dv_a, dk_a, dq_a = outputs(first)
                            dv_b, dk_b, dq_b = outputs(second)
                            halves = ((dv_a, dk_a), (dv_b, dk_b))
                            dq = jnp.concatenate([dq_a[:, lo], dq_a[:, hi] + dq_b], axis=1)
                        else:
                            dv_f, dk_f, dq = outputs(first)
                            halves = ((dv_f[lo], dk_f[lo]), (dv_f[hi], dk_f[hi]))
                        for n, (dv_h, dk_h) in enumerate(halves):
                            dv_part[n] = add(dv_part[n], dv_h)
                            dk_part[n] = add(dk_part[n], dk_h)
                        dq_t[b][e] = dq_t[b][e] + dq
                    dk_acc[e, krows, :] += jnp.concatenate(dk_part, axis=0)
                dv_acc[krows, :] += jnp.concatenate(dv_part, axis=0)
                return tuple(tuple(d) for d in dq_t)

            zacc = jnp.zeros((LANES, blk), F32)
            dq_t = tuple((zacc, zacc) for _ in range(group))
            dq_t = lax.fori_loop(
                0, base, lambda jj, st: key_block(st, jj, [(b, base + b - jj) for b in range(group)]), dq_t)
            for a in range(group):
                dq_t = key_block(dq_t, base + a, [(b, b - a) for b in range(a, group)])
            for b in range(group):
                rows = block_rows(base + b)
                dq_ref[rows, :] = jnp.where(top, dq_t[b][0], dq_t[b][1]).T
                dqx_ref[rows, :] = jnp.where(top, dq_t[b][1], dq_t[b][0]).T
            return 0

        lax.fori_loop(0, nb // group, q_group, 0)
        lo = _head_lanes(0, (seq, LANES), 1)
        dk_ref[...] = jnp.where(lo, dk_acc[0], dk_acc[1])
        dkx_ref[...] = jnp.where(lo, dk_acc[1], dk_acc[0])
        dv_ref[...] = dv_acc[...].astype(dv_ref.dtype)

    col, heads, v_spec, table_spec = _attn_specs(seq)
    grid = (t // seq, N_PAIRS)
    h_specs, h_shapes, h_args, h_scratch = _hosted_parts(hosted)
    f32_out = jax.ShapeDtypeStruct((t, D_MODEL), F32)
    return pl.pallas_call(
        _host(body, 7, 5, 2, hosted, grid), name="attn_bwd", grid=grid,
        in_specs=[heads(0), heads(N_PAIRS), v_spec, table_spec, col, col, col] + h_specs,
        out_specs=tuple([col] * 5 + h_specs),
        out_shape=tuple([f32_out, f32_out, jax.ShapeDtypeStruct((t, D_MODEL), BF16), f32_out, f32_out] + h_shapes),
        scratch_shapes=[pltpu.VMEM((2, seq, LANES), F32), pltpu.VMEM((seq, LANES), F32)] + h_scratch,
        compiler_params=_params(2),
    )(qk, qk, proj, tables, o, lse, do, *h_args)


def _row_block(t):
    return 2 * ROW_BLOCK if t % (2 * ROW_BLOCK) == 0 else ROW_BLOCK


def _out_proj_ffn_norm(o, g_out, w_out, x, g_ffn):
    t = o.shape[0]
    bt = _row_block(t)

    def body(o_ref, go_ref, w_ref, x_ref, gf_ref, on_ref, x2_ref, h2_ref):
        for s in range(0, D_MODEL, W_GROUP):
            os_ = o_ref[:, s:s + W_GROUP].astype(F32)
            r = lax.rsqrt(jnp.mean(os_ * os_, axis=-1, keepdims=True) + EPS)
            on_ref[:, s:s + W_GROUP] = (os_ * r * go_ref[:, s:s + W_GROUP]).astype(BF16)
        x2 = x_ref[...] + jnp.dot(on_ref[...], w_ref[...], preferred_element_type=F32)
        x2_ref[...] = x2
        r2 = lax.rsqrt(jnp.mean(x2 * x2, axis=-1, keepdims=True) + EPS)
        h2_ref[...] = (x2 * r2 * gf_ref[...]).astype(BF16)

    row = pl.BlockSpec((bt, D_MODEL), lambda i: (i, 0))
    vec = pl.BlockSpec((1, D_MODEL), lambda i: (0, 0))
    return pl.pallas_call(
        body, name="out_proj", grid=(t // bt,),
        in_specs=[row, vec, pl.BlockSpec((D_MODEL, D_MODEL), lambda i: (0, 0)), row, vec],
        out_specs=(row, row, row),
        out_shape=(jax.ShapeDtypeStruct((t, D_MODEL), BF16), jax.ShapeDtypeStruct((t, D_MODEL), F32),
                   jax.ShapeDtypeStruct((t, D_MODEL), BF16)),
        compiler_params=_params(1),
    )(o, g_out, w_out, x, g_ffn)


def _ffn_gate_up(h2, w_gate_t, w_up_t):
    t = h2.shape[0]
    bt = _row_block(t)
    bn = _divisor_block(D_FF, WIDE_BLOCK)

    def body(h_ref, wg_ref, wu_ref, a_ref, u_ref, f_ref):
        a = _dot_nt(h_ref[...], wg_ref[...])
        u = _dot_nt(h_ref[...], wu_ref[...])
        a_ref[...] = a.astype(BF16)
        u_ref[...] = u.astype(BF16)
        f_ref[...] = (a * jax.nn.sigmoid(a) * u).astype(BF16)

    blk = pl.BlockSpec((bt, bn), lambda j, i: (i, j))
    w_blk = pl.BlockSpec((bn, D_MODEL), lambda j, i: (j, 0))
    shape = jax.ShapeDtypeStruct((t, D_FF), BF16)
    return pl.pallas_call(
        body, name="ffn_gate_up", grid=(D_FF // bn, t // bt),
        in_specs=[pl.BlockSpec((bt, D_MODEL), lambda j, i: (i, 0)), w_blk, w_blk],
        out_specs=(blk, blk, blk), out_shape=(shape, shape, shape), compiler_params=_params(2),
    )(h2, w_gate_t, w_up_t)


def _ffn_down_grad(dy16, w_down, a, u):
    t = a.shape[0]
    bt = _row_block(t)
    bn = _divisor_block(D_FF, WIDE_BLOCK)

    def body(dy_ref, w_ref, a_ref, u_ref, da_ref, du_ref):
        df = _dot_nt(dy_ref[...], w_ref[...])
        av = a_ref[...].astype(F32)
        sg = jax.nn.sigmoid(av)
        da_ref[...] = (df * u_ref[...].astype(F32) * sg * (1.0 + av * (1.0 - sg))).astype(BF16)
        du_ref[...] = (df * av * sg).astype(BF16)

    blk = pl.BlockSpec((bt, bn), lambda j, i: (i, j))
    shape = jax.ShapeDtypeStruct((t, D_FF), BF16)
    return pl.pallas_call(
        body, name="d_ffn_down", grid=(D_FF // bn, t // bt),
        in_specs=[pl.BlockSpec((bt, D_MODEL), lambda j, i: (i, 0)), pl.BlockSpec((bn, D_MODEL), lambda j, i: (j, 0)),
                  blk, blk],
        out_specs=(blk, blk), out_shape=(shape, shape), compiler_params=_params(2),
    )(dy16, w_down, a, u)


def _ffn_down_loss(f, w_down, x2, target):
    t, w = x2.shape
    bt = _row_block(t)

    def body(f_ref, w_ref, x_ref, t_ref, dy_ref, dy16_ref, loss_ref):
        @pl.when(pl.program_id(0) == 0)
        def _():
            loss_ref[...] = jnp.zeros_like(loss_ref)

        err = (x_ref[...] + jnp.dot(f_ref[...], w_ref[...], preferred_element_type=F32)) - t_ref[...]
        dy = err * (1.0 / w)
        dy_ref[...] = dy
        dy16_ref[...] = dy.astype(BF16)
        loss_ref[...] += 0.5 * jnp.sum(jnp.mean(err * err, axis=-1, keepdims=True), axis=0, keepdims=True)

    row = pl.BlockSpec((bt, w), lambda i: (i, 0))
    return pl.pallas_call(
        body, name="ffn_down_loss", grid=(t // bt,),
        in_specs=[pl.BlockSpec((bt, D_FF), lambda i: (i, 0)), pl.BlockSpec((D_FF, w), lambda i: (0, 0)), row, row],
        out_specs=(row, row, pl.BlockSpec((8, LANES), lambda i: (0, 0))),
        out_shape=(jax.ShapeDtypeStruct((t, w), F32), jax.ShapeDtypeStruct((t, w), BF16),
                   jax.ShapeDtypeStruct((8, LANES), F32)),
        compiler_params=_params(1),
    )(f, w_down, x2, target)


def _adamw(parts, w, m, v, *, name):
    _, rows, cols = w.shape
    br = rows if rows <= 512 else 256
    assert rows % br == 0

    def body(p_ref, w_ref, m_ref, v_ref, g_ref, d_ref, nm_ref, nv_ref):
        g = p_ref[0].astype(F32)
        for r in range(1, N_DEV):
            g = g + p_ref[r].astype(F32)
        m2 = ADAM_B1 * m_ref[0] + (1.0 - ADAM_B1) * g
        v2 = ADAM_B2 * v_ref[0] + (1.0 - ADAM_B2) * jnp.square(g)
        m_hat = m2 / (1.0 - ADAM_B1 ** ADAM_STEP)
        v_hat = v2 / (1.0 - ADAM_B2 ** ADAM_STEP)
        g_ref[0] = g
        d_ref[0] = -ADAM_LR * (m_hat / (jnp.sqrt(v_hat) + ADAM_EPS) + ADAM_WD * w_ref[0])
        nm_ref[0] = m2
        nv_ref[0] = v2

    blk = pl.BlockSpec((1, br, cols), lambda i: (0, i, 0))
    shape = jax.ShapeDtypeStruct((1, rows, cols), F32)
    return pl.pallas_call(
        body, name=name, grid=(rows // br,),
        in_specs=[pl.BlockSpec((N_DEV, br, cols), lambda i: (0, i, 0)), blk, blk, blk],
        out_specs=(blk, blk, blk, blk), out_shape=(shape, shape, shape, shape), compiler_params=_params(1),
    )(parts, w, m, v)


_QA, _KA, _VA, _FA, _QD, _KD, _VD = (0, 512), (512, 1024), (1024, 1536), (1536, 1544), (1544, 2056), (2056, 2568), (2568, 3080)
_MAIN_ORDER = (_QA, _QD, _KA, _KD, _VA, _VD)
MAIN_COLS = 3 * D_MODEL
PROJ_COLS = MAIN_COLS + LANES
COL_SHARDED = ("w_in", "w_gate", "w_up")


def _swap(w):
    return jnp.transpose(w, (0, 2, 1))


def _w_in_to_kernel(w_t):
    main = jnp.concatenate([w_t[a:b] for a, b in _MAIN_ORDER], axis=0)
    forget = jnp.pad(w_t[_FA[0]:_FA[1]], ((0, LANES - N_HEADS_FOX), (0, 0)))
    return main, forget


def _w_in_from_kernel(g_t):
    pos = {span: i * W_GROUP for i, span in enumerate(_MAIN_ORDER)}
    parts = []
    for span in (_QA, _KA, _VA, _FA, _QD, _KD, _VD):
        if span == _FA:
            parts.append(g_t[MAIN_COLS:MAIN_COLS + N_HEADS_FOX])
        else:
            parts.append(g_t[pos[span]:pos[span] + W_GROUP])
    return jnp.concatenate(parts, axis=0)


def _pack_small(vals):
    rows = []
    for name, _, n_rows in SMALL_LAYOUT:
        flat = vals[name].reshape(-1).astype(F32)
        rows.append(jnp.pad(flat, (0, n_rows * LANES - flat.shape[0])).reshape(n_rows, LANES))
    packed = jnp.concatenate(rows, axis=0)
    return jnp.pad(packed, ((0, SMALL_ROWS - packed.shape[0]), (0, 0)))


def _unpack_small(packed, like):
    out = {}
    for name, row, n_rows in SMALL_LAYOUT:
        n = like[name].size
        out[name] = packed[row:row + n_rows].reshape(-1)[:n].reshape(like[name].shape)
    return out


def _device_step(x, target, small, shards):
    bsz, seq, _ = x.shape
    t = bsz * seq
    xf = x.reshape(t, D_MODEL)
    tf = target.reshape(t, D_MODEL)
    row = lambda v: v.reshape(1, -1)
    g_out = jnp.concatenate([small["g_out_fox"], small["g_out_dil"]]).reshape(1, D_MODEL)
    gains = jnp.concatenate(
        [jnp.tile(small[n].reshape(1, HEAD_DIM), (1, 2)) for n in ("g_q_fox", "g_q_dil", "g_k_fox", "g_k_dil")]
        + [jnp.zeros((4, LANES), F32)], axis=0)
    b_pad = jnp.pad(small["b_forget"].reshape(1, N_HEADS_FOX), ((0, 0), (0, LANES - N_HEADS_FOX)))
    rope = _rope_tables(seq)
    tables_qk = _bias_tables(seq, keys_first=False)
    tables_kq = _bias_tables(seq, keys_first=True)

    h1, g_in = _rmsnorm_fwd(xf, row(small["g_mix"]), group=D_MODEL, name="norm_mix",
                            hosted=_ChipGather([(shards["w_in"], False)]))
    w_main_t, w_fa_t = _w_in_to_kernel(g_in.reshape(IN_COLS, D_MODEL))
    w_in_all_t = jnp.concatenate([w_main_t, w_fa_t], axis=0)
    proj = _matmul_nt(h1, w_main_t, name="in_proj", out_dtype=BF16)
    qk, fa = _qk_prep_fwd(proj, h1, w_fa_t, b_pad, gains, rope, seq)
    late = _Exchange([(shards[n], False) for n in ("w_out", "w_gate", "w_up", "w_down")])
    o, lse, g_out_w, g_gate, g_up, g_down = _attn_fwd(qk, proj, tables_qk, seq, hosted=late)
    w_out = g_out_w.reshape(D_MODEL, D_MODEL)
    w_gate_t = g_gate.reshape(D_FF, D_MODEL)
    w_up_t = g_up.reshape(D_FF, D_MODEL)
    w_down = g_down.reshape(D_FF, D_MODEL)
    on, x2, h2 = _out_proj_ffn_norm(o, g_out, w_out, xf, row(small["g_ffn"]))
    a, u, f = _ffn_gate_up(h2, w_gate_t, w_up_t)
    dy, dy16, loss_tile = _ffn_down_loss(f, w_down, x2, tf)

    da, du = _ffn_down_grad(dy16, w_down, a, u)
    gw_down = _matmul_tn(f, dy16, name="gw_down")
    gw_gate_t = _matmul_tn(da, h2, name="gw_gate")
    gw_up_t = _matmul_tn(du, h2, name="gw_up")
    dh2_gate = _matmul_rows(da, w_gate_t, name="d_ffn_gate")
    dx2, dx2_16, dg_ffn = _norm_input_grad([(du, w_up_t, True)], x2, row(small["g_ffn"]), group=D_MODEL,
                                           name="d_ffn_up", out_dtypes=(F32, BF16), resid=dy, init=dh2_gate)
    gw_out = _matmul_tn(on, dx2_16, name="gw_out")
    do, dg_out = _norm_input_grad([(dx2_16, w_out, False)], o, g_out, group=W_GROUP, name="d_out_proj",
                                  out_dtypes=(BF16,))

    shard_rows = lambda g: g.reshape(N_DEV, g.shape[0] // N_DEV, g.shape[1])
    ffn_grads = _Exchange([(shard_rows(g), True) for g in (gw_out, gw_gate_t, gw_up_t, gw_down)])
    dq, dk, dv, dqx, dkx, p_out, p_gate, p_up, p_down = _attn_bwd(qk, proj, tables_kq, o, lse, do, seq, hosted=ffn_grads)
    dproj, dgains, db = _qk_prep_bwd(dq, dk, dqx, dkx, dv, proj, fa, b_pad, gains, rope, seq)
    gw_in_t = _matmul_tn(dproj, h1, name="gw_in")
    in_grad = _Exchange([(shard_rows(_w_in_from_kernel(gw_in_t)), True)])
    dx, dg_mix, p_in = _norm_input_grad([(dproj, w_in_all_t, True)], xf, row(small["g_mix"]), group=D_MODEL,
                                        name="d_in_proj", out_dtypes=(F32,), resid=dx2, hosted=in_grad)

    fold = lambda rows: jnp.sum(rows[:, :HEAD_DIM] + rows[:, HEAD_DIM:], axis=0)
    half = N_PAIRS // 2
    gsmall = {
        "g_mix": dg_mix, "g_ffn": dg_ffn, "g_out_fox": dg_out[0, :W_GROUP], "g_out_dil": dg_out[0, W_GROUP:],
        "g_q_fox": fold(dgains[0:half]), "g_q_dil": fold(dgains[half:N_PAIRS]),
        "g_k_fox": fold(dgains[N_PAIRS:N_PAIRS + half]), "g_k_dil": fold(dgains[N_PAIRS + half:]),
        "b_forget": db[0, :N_HEADS_FOX],
    }
    packed = _pack_small(gsmall).at[LOSS_ROW].set(loss_tile[0])
    (p_small,) = _exchange("small_exchange", [(packed, False)])
    parts = {"w_in": p_in, "w_out": p_out, "w_gate": p_gate, "w_up": p_up, "w_down": p_down}
    return dx.reshape(x.shape), parts, p_small


def kernel(x, g_mix, w_in, b_forget, g_q_fox, g_k_fox, g_q_dil, g_k_dil, g_out_fox, g_out_dil, w_out, g_ffn, w_gate, w_up, w_down, loss_target, m_g_mix, m_w_in, m_b_forget, m_g_q_fox, m_g_k_fox, m_g_q_dil, m_g_k_dil, m_g_out_fox, m_g_out_dil, m_w_out, m_g_ffn, m_w_gate, m_w_up, m_w_down, v_g_mix, v_w_in, v_b_forget, v_g_q_fox, v_g_k_fox, v_g_q_dil, v_g_k_dil, v_g_out_fox, v_g_out_dil, v_w_out, v_g_ffn, v_w_gate, v_w_up, v_w_down):
    args = dict(locals())
    small_names = [name for name, _, _ in SMALL_LAYOUT]
    big_names = ["w_in", "w_out", "w_gate", "w_up", "w_down"]
    small = {n: args[n][0] for n in small_names}

    as_rows = lambda n, w: _swap(w) if n in COL_SHARDED else w
    shards = {n: as_rows(n, args[n])[0].astype(BF16) for n in big_names}
    grad_x, parts, p_small = _device_step(x, loss_target, small, shards)

    grads, deltas, new_m, new_v = {}, {}, {}, {}
    for n in big_names:
        res = _adamw(parts[n], as_rows(n, args[n]), as_rows(n, args["m_" + n]), as_rows(n, args["v_" + n]),
                     name="adamw_" + n)
        grads[n], deltas[n], new_m[n], new_v[n] = [as_rows(n, r) for r in res]
    res = _adamw(p_small, _pack_small(small)[None], _pack_small({n: args["m_" + n][0] for n in small_names})[None],
                 _pack_small({n: args["v_" + n][0] for n in small_names})[None], name="adamw_small")
    loss = res[0][0, LOSS_ROW, 0]
    for dst, packed_res in zip((grads, deltas, new_m, new_v), res):
        for n, val in _unpack_small(packed_res[0], small).items():
            dst[n] = val[None]

    order = ["g_mix", "w_in", "b_forget", "g_q_fox", "g_k_fox", "g_q_dil", "g_k_dil", "g_out_fox", "g_out_dil",
             "w_out", "g_ffn", "w_gate", "w_up", "w_down"]
    return (loss, grad_x, *[grads[n] for n in order], *[deltas[n] for n in order],
            *[new_m[n] for n in order], *[new_v[n] for n in order])
```

```python
import functools
import math

import jax
import jax.numpy as jnp
import numpy as np
from jax import lax
from jax.experimental import pallas as pl
from jax.experimental.pallas import tpu as pltpu

F32 = jnp.float32
BF16 = jnp.bfloat16

D_MODEL = 1024
HEAD_DIM = 64
LANES = 128
N_PAIRS = D_MODEL // LANES
N_HEADS = 2 * N_PAIRS
N_HEADS_FOX = 8
W_GROUP = 512
D_FF = 2816
IN_COLS = 3080
DILATION_PAIRS = ((128, 1), (512, 4), (2048, 16))
ROPE_THETA = 500000.0
ROPE_DIM = 16
ROPE_HALF = ROPE_DIM // 2
EPS = 1e-6
NEG = -1e30
LOG2E = 1.4426950408889634
LN2 = 0.6931471805599453
AUG_ONE = 0
AUG_C = 3
N_DEV = 8

ADAM_LR = 0.001
ADAM_B1 = 0.9
ADAM_B2 = 0.999
ADAM_EPS = 1e-08
ADAM_WD = 0.01
ADAM_STEP = 10

ROW_BLOCK = 512
TOKEN_STEP = 2048
WIDE_BLOCK = D_FF // 2
ATT_BLOCK = 512
ATT_GROUP = 4
VMEM_LIMIT = 56 * 1024 * 1024
MATMUL_VMEM_BUDGET = 44 * 1024 * 1024

SMALL_ROWS = 32
SMALL_LAYOUT = (("g_mix", 0, 8), ("g_ffn", 8, 8), ("g_out_fox", 16, 4), ("g_out_dil", 20, 4),
                ("g_q_fox", 24, 1), ("g_k_fox", 25, 1), ("g_q_dil", 26, 1), ("g_k_dil", 27, 1),
                ("b_forget", 28, 1))
LOSS_ROW = 29


def _params(n_grid):
    return pltpu.CompilerParams(dimension_semantics=("arbitrary",) * n_grid, vmem_limit_bytes=VMEM_LIMIT)


def _divisor_block(n, cap):
    best = None
    for b in range(LANES, min(n, cap) + 1, LANES):
        if n % b == 0:
            best = b
    assert best is not None, n
    return best


def _split_dot(a, b_exact, terms):
    acc = None
    rest = a
    for _ in range(terms):
        hi = rest.astype(BF16)
        part = jnp.dot(hi, b_exact, preferred_element_type=F32)
        acc = part if acc is None else acc + part
        rest = rest - hi.astype(F32)
    return acc


def _split_dot_nt(a_exact, b, terms):
    acc = None
    rest = b
    for _ in range(terms):
        hi = rest.astype(BF16)
        part = _dot_nt(a_exact, hi)
        acc = part if acc is None else acc + part
        rest = rest - hi.astype(F32)
    return acc


def _dot_nt(a, b):
    return lax.dot_general(a, b, (((1,), (1,)), ((), ())), preferred_element_type=F32)


def _dot_tn(a, b):
    return lax.dot_general(a, b, (((0,), (0,)), ((), ())), preferred_element_type=F32)


class _Exchange:
    def __init__(self, items):
        self.items = items
        self.n = len(items)
        self.arrays = [a for a, _ in items]
        self.out_shape = [jax.ShapeDtypeStruct((N_DEV,) + tuple(a.shape[1:] if sc else a.shape), a.dtype)
                          for a, sc in items]
        self.specs = [pl.BlockSpec(memory_space=pl.ANY)] * self.n
        self.scratch = [pltpu.SemaphoreType.DMA((self.n, N_DEV - 1)), pltpu.SemaphoreType.DMA((self.n, N_DEV - 1)),
                        pltpu.SemaphoreType.DMA((self.n,))]

    def run(self, ins, outs, sems, first, last, compute):
        send_sems, recv_sems, local_sems = sems
        x, y, c = lax.axis_index("x"), lax.axis_index("y"), lax.axis_index("c")
        me = 4 * x + 2 * y + c
        local, remote = [], []
        for k, (_, scatter) in enumerate(self.items):
            own = ins[k].at[me] if scatter else ins[k]
            local.append(pltpu.make_async_copy(own, outs[k].at[me], local_sems.at[k]))
        for r in range(1, N_DEV):
            px = 1 - x if r & 4 else x
            py = 1 - y if r & 2 else y
            pc = 1 - c if r & 1 else c
            peer = 4 * px + 2 * py + pc
            for k, (_, scatter) in enumerate(self.items):
                src = ins[k].at[peer] if scatter else ins[k]
                remote.append(pltpu.make_async_remote_copy(
                    src_ref=src, dst_ref=outs[k].at[me],
                    send_sem=send_sems.at[k, r - 1], recv_sem=recv_sems.at[k, r - 1],
                    device_id=(px, py, pc), device_id_type=pl.DeviceIdType.MESH))

        def start():
            for cp in local + remote:
                cp.start()

        def finish():
            for cp in remote:
                cp.wait_recv()
            for cp in remote:
                cp.wait_send()
            for cp in local:
                cp.wait()

        _run_phases(first, last, start, compute, finish)


def _run_phases(first, last, start, compute, finish):
    if first is None:
        start()
        compute()
        finish()
    else:
        pl.when(first)(start)
        compute()
        pl.when(last)(finish)


class _ChipGather(_Exchange):
    def run(self, ins, outs, sems, first, last, compute):
        send_sems, recv_sems, local_sems = sems
        x, y, c = lax.axis_index("x"), lax.axis_index("y"), lax.axis_index("c")
        sibling = (x, y, 1 - c)
        chips = [(1 - x, y), (x, 1 - y), (1 - x, 1 - y)]
        slot = lambda px, py, pc: 4 * px + 2 * py + pc

        def copy(k, n, src, dst_slot, to):
            return pltpu.make_async_remote_copy(
                src_ref=src, dst_ref=outs[k].at[dst_slot], send_sem=send_sems.at[k, n], recv_sem=recv_sems.at[k, n],
                device_id=to, device_id_type=pl.DeviceIdType.MESH)

        local, own, passed, arrivals = [], [], [], []
        for k in range(self.n):
            me = slot(x, y, c)
            local.append(pltpu.make_async_copy(ins[k], outs[k].at[me], local_sems.at[k]))
            own.append(copy(k, 0, ins[k], me, sibling))
            arrivals.append(copy(k, 0, ins[k], slot(*sibling), sibling))
            for j, chip in enumerate(chips):
                theirs = slot(*chip, c)
                own.append(copy(k, 1 + j, ins[k], me, (*chip, c)))
                passed.append((copy(k, 1 + j, ins[k], theirs, sibling),
                               copy(k, 4 + j, outs[k].at[theirs], theirs, sibling)))
                arrivals.append(copy(k, 4 + j, ins[k], slot(*chip, 1 - c), sibling))

        def start():
            for cp in local + own:
                cp.start()

        def finish():
            for landed, onward in passed:
                landed.wait_recv()
                onward.start()
            for cp in arrivals:
                cp.wait_recv()
            for cp in own + [onward for _, onward in passed]:
                cp.wait_send()
            for cp in local:
                cp.wait()

        _run_phases(first, last, start, compute, finish)


def _grid_ends(grid):
    ids = [pl.program_id(d) for d in range(len(grid))]
    first = functools.reduce(jnp.logical_and, [i == 0 for i in ids])
    last = functools.reduce(jnp.logical_and, [i == g - 1 for i, g in zip(ids, grid)])
    return first, last


def _host(core, n_in, n_out, n_scratch, hosted, grid):
    if hosted is None:
        return core
    nh = hosted.n

    def body(*refs):
        ins, rest = refs[:n_in], refs[n_in:]
        h_ins, rest = rest[:nh], rest[nh:]
        outs, rest = rest[:n_out], rest[n_out:]
        h_outs, rest = rest[:nh], rest[nh:]
        scratch, sems = rest[:n_scratch], rest[n_scratch:]
        first, last = _grid_ends(grid)
        hosted.run(h_ins, h_outs, sems, first, last, lambda: core(*ins, *outs, *scratch))

    return body


def _hosted_parts(hosted):
    if hosted is None:
        return [], [], [], []
    return list(hosted.specs), list(hosted.out_shape), list(hosted.arrays), list(hosted.scratch)


def _exchange(name, items):
    ex = _Exchange(items)
    n = ex.n

    def body(*refs):
        ex.run(refs[:n], refs[n:2 * n], refs[2 * n:], None, None, lambda: None)

    return pl.pallas_call(
        body, name=name, out_shape=tuple(ex.out_shape), in_specs=ex.specs, out_specs=tuple(ex.specs),
        scratch_shapes=ex.scratch,
    )(*ex.arrays)


def _matmul_blocks(t, k, n, a_bytes, o_bytes):
    for bt, cap in ((_row_block(t), WIDE_BLOCK), (_row_block(t), ROW_BLOCK), (ROW_BLOCK, ROW_BLOCK)):
        bn = _divisor_block(n, cap)
        if 2 * (bt * k * a_bytes + bn * k * 2 + bt * bn * o_bytes) <= MATMUL_VMEM_BUDGET:
            return bt, bn
    return ROW_BLOCK, _divisor_block(n, 2 * LANES)


def _matmul_nt(a, w, *, name, out_dtype):
    t, k = a.shape
    n = w.shape[0]
    assert w.shape[1] == k
    bt, bn = _matmul_blocks(t, k, n, a.dtype.itemsize, jnp.dtype(out_dtype).itemsize)

    def body(a_ref, w_ref, o_ref):
        o_ref[...] = _dot_nt(a_ref[...], w_ref[...]).astype(o_ref.dtype)

    return pl.pallas_call(
        body, name=name, grid=(t // bt, n // bn),
        in_specs=[pl.BlockSpec((bt, k), lambda i, j: (i, 0)), pl.BlockSpec((bn, k), lambda i, j: (j, 0))],
        out_specs=pl.BlockSpec((bt, bn), lambda i, j: (i, j)),
        out_shape=jax.ShapeDtypeStruct((t, n), out_dtype), compiler_params=_params(2),
    )(a, w)


def _matmul_rows(a, w, *, name):
    t, k = a.shape
    n = w.shape[1]
    assert w.shape[0] == k
    bt = _row_block(t)

    def body(a_ref, w_ref, o_ref):
        o_ref[...] = jnp.dot(a_ref[...], w_ref[...], preferred_element_type=F32)

    return pl.pallas_call(
        body, name=name, grid=(t // bt,),
        in_specs=[pl.BlockSpec((bt, k), lambda i: (i, 0)), pl.BlockSpec((k, n), lambda i: (0, 0))],
        out_specs=pl.BlockSpec((bt, n), lambda i: (i, 0)),
        out_shape=jax.ShapeDtypeStruct((t, n), F32), compiler_params=_params(1),
    )(a, w)


def _matmul_tn(a_parts, b, *, name, make_a=None, vectors=()):
    t, m = a_parts[0].shape
    n = b.shape[1]
    step_rows = TOKEN_STEP if make_a is None else TOKEN_STEP // 2
    bt = step_rows if t % step_rows == 0 else ROW_BLOCK
    bm = _divisor_block(m, WIDE_BLOCK)
    bn = _divisor_block(n, WIDE_BLOCK)
    steps = t // bt
    n_a, n_v = len(a_parts), len(vectors)

    def body(*refs):
        b_ref, o_ref, acc = refs[n_a + n_v:]
        step = pl.program_id(2)

        @pl.when(step == 0)
        def _():
            acc[...] = jnp.zeros_like(acc)

        a_blk = refs[0][...] if make_a is None else make_a(*[r[...] for r in refs[:n_a + n_v]])
        acc[...] += _dot_tn(a_blk, b_ref[...].astype(BF16))

        @pl.when(step == steps - 1)
        def _():
            o_ref[...] = acc[...].astype(o_ref.dtype)

    return pl.pallas_call(
        body, name=name, grid=(m // bm, n // bn, steps),
        in_specs=[pl.BlockSpec((bt, bm), lambda i, j, s: (s, i))] * n_a
                 + [pl.BlockSpec((1, bm), lambda i, j, s: (0, i))] * n_v
                 + [pl.BlockSpec((bt, bn), lambda i, j, s: (s, j))],
        out_specs=pl.BlockSpec((bm, bn), lambda i, j, s: (i, j)),
        out_shape=jax.ShapeDtypeStruct((m, n), BF16), scratch_shapes=[pltpu.VMEM((bm, bn), F32)],
        compiler_params=_params(3),
    )(*a_parts, *vectors, b)


def _group_norm(o_blk, g_blk):
    parts = []
    for s in range(0, o_blk.shape[1], W_GROUP):
        os_ = o_blk[:, s:s + W_GROUP].astype(F32)
        r = lax.rsqrt(jnp.mean(os_ * os_, axis=-1, keepdims=True) + EPS)
        parts.append((os_ * r * g_blk[:, s:s + W_GROUP]).astype(BF16))
    return jnp.concatenate(parts, axis=1)


def _silu_mul(a_blk, u_blk):
    a = a_blk.astype(F32)
    return (a * jax.nn.sigmoid(a) * u_blk.astype(F32)).astype(BF16)


def _rmsnorm_fwd(x, g, *, group, name, hosted=None):
    t, w = x.shape
    bt = ROW_BLOCK

    def body(x_ref, g_ref, o_ref):
        for s in range(0, w, group):
            xs = x_ref[:, s:s + group].astype(F32)
            r = lax.rsqrt(jnp.mean(xs * xs, axis=-1, keepdims=True) + EPS)
            o_ref[:, s:s + group] = (xs * r * g_ref[:, s:s + group]).astype(o_ref.dtype)

    grid = (t // bt,)
    h_specs, h_shapes, h_args, h_scratch = _hosted_parts(hosted)
    res = pl.pallas_call(
        _host(body, 2, 1, 0, hosted, grid), name=name, grid=grid,
        in_specs=[pl.BlockSpec((bt, w), lambda i: (i, 0)), pl.BlockSpec((1, w), lambda i: (0, 0))] + h_specs,
        out_specs=tuple([pl.BlockSpec((bt, w), lambda i: (i, 0))] + h_specs),
        out_shape=tuple([jax.ShapeDtypeStruct((t, w), BF16)] + h_shapes),
        scratch_shapes=h_scratch, compiler_params=_params(1),
    )(x, g, *h_args)
    return res if hosted else res[0]


def _norm_input_grad(terms, x, g, *, group, name, out_dtypes, resid=None, init=None, hosted=None, k_chunks=1):
    t, w = x.shape
    n_terms = len(terms)
    kc = [a.shape[1] // k_chunks for a, _, _ in terms]
    per_row = sum(c * a.dtype.itemsize for c, (a, _, _) in zip(kc, terms)) + w * (x.dtype.itemsize + 4 * (resid is not None))
    per_row += w * (sum(jnp.dtype(dt).itemsize for dt in out_dtypes) + 4 * (init is not None))
    fixed = 2 * sum(w * c * 2 for c in kc)
    bt = next(b for b in (2 * ROW_BLOCK, ROW_BLOCK, ROW_BLOCK // 2, ROW_BLOCK // 4)
              if t % b == 0 and fixed + 2 * b * per_row + 5 * b * w * 4 <= MATMUL_VMEM_BUDGET)
    resid_at = 2 * n_terms + 2
    init_at = resid_at + (resid is not None)
    n_in = init_at + (init is not None)
    grid = (t // bt, k_chunks)

    def body(*refs):
        x_ref, g_ref = refs[2 * n_terms], refs[2 * n_terms + 1]
        dx_refs, dg_ref, dh_ref = refs[n_in:-2], refs[-2], refs[-1]
        chunk = pl.program_id(1)

        @pl.when((pl.program_id(0) == 0) & (chunk == 0))
        def _():
            dg_ref[...] = jnp.zeros_like(dg_ref)

        part = None
        for k in range(n_terms):
            if terms[k][2]:
                term = jnp.dot(refs[2 * k][...].astype(BF16), refs[2 * k + 1][...], preferred_element_type=F32)
            else:
                term = _dot_nt(refs[2 * k][...].astype(BF16), refs[2 * k + 1][...])
            part = term if part is None else part + term

        @pl.when(chunk == 0)
        def _():
            dh_ref[...] = part if init is None else refs[init_at][...] + part

        @pl.when(chunk > 0)
        def _():
            dh_ref[...] += part

        @pl.when(chunk == k_chunks - 1)
        def _():
            for s in range(0, w, group):
                xs = x_ref[:, s:s + group].astype(F32)
                dhs = dh_ref[:, s:s + group]
                r = lax.rsqrt(jnp.mean(xs * xs, axis=-1, keepdims=True) + EPS)
                xh = xs * r
                dg_ref[:, s:s + group] += jnp.sum(dhs * xh, axis=0, keepdims=True)
                dxh = dhs * g_ref[:, s:s + group]
                dx = r * (dxh - xh * jnp.mean(dxh * xh, axis=-1, keepdims=True))
                if resid is not None:
                    dx = refs[resid_at][:, s:s + group] + dx
                for dx_ref in dx_refs:
                    dx_ref[:, s:s + group] = dx.astype(dx_ref.dtype)

    row = pl.BlockSpec((bt, w), lambda i, k: (i, 0))
    vec = pl.BlockSpec((1, w), lambda i, k: (0, 0))
    in_specs, args = [], []
    for c, (a, wt, w_is_kn) in zip(kc, terms):
        assert wt.shape == ((a.shape[1], w) if w_is_kn else (w, a.shape[1]))
        w_spec = pl.BlockSpec((c, w), lambda i, k: (k, 0)) if w_is_kn else pl.BlockSpec((w, c), lambda i, k: (0, k))
        in_specs += [pl.BlockSpec((bt, c), lambda i, k: (i, k)), w_spec]
        args += [a, wt]
    extra = [r for r in (resid, init) if r is not None]
    in_specs += [row, vec] + [row] * len(extra)
    args += [x, g] + extra
    h_specs, h_shapes, h_args, h_scratch = _hosted_parts(hosted)
    return pl.pallas_call(
        _host(body, n_in, len(out_dtypes) + 1, 1, hosted, grid), name=name, grid=grid, in_specs=in_specs + h_specs,
        out_specs=tuple([row] * len(out_dtypes) + [vec] + h_specs),
        out_shape=tuple([jax.ShapeDtypeStruct((t, w), dt) for dt in out_dtypes] + [jax.ShapeDtypeStruct((1, w), F32)]
                        + h_shapes),
        scratch_shapes=[pltpu.VMEM((bt, w), F32)] + h_scratch, compiler_params=_params(2),
    )(*args, *h_args)


def _tile_plan(tile):
    is_q = tile < N_PAIRS
    is_dil = (tile % N_PAIRS) >= N_PAIRS // 2
    return is_q, is_dil, (0 if is_q else 2) + (1 if is_dil else 0)


def _proj_tile(kind, pair):
    return kind * N_PAIRS + pair


def _segment_ones():
    lane = np.arange(LANES)
    return jnp.asarray((lane[:, None] // HEAD_DIM) == (lane[None, :] // HEAD_DIM), BF16)


def _rope_tables(seq):
    inv_freq = jnp.power(jnp.float32(ROPE_THETA), -jnp.arange(ROPE_HALF, dtype=F32) * 2.0 / ROPE_DIM)
    ang = jnp.arange(seq).astype(F32)[:, None] * inv_freq[None, :]
    cos, sin = jnp.cos(ang), jnp.sin(ang)
    ones = jnp.ones((seq, HEAD_DIM - ROPE_DIM), F32)
    zeros = jnp.zeros((seq, HEAD_DIM - ROPE_DIM), F32)
    zh = jnp.zeros((seq, ROPE_HALF), F32)
    cos_t = jnp.concatenate([cos, cos, ones], axis=1)
    sin_a = jnp.concatenate([-sin, zh, zeros], axis=1)
    sin_b = jnp.concatenate([zh, sin, zeros], axis=1)
    return tuple(jnp.tile(tab, (1, 2)) for tab in (cos_t, sin_a, sin_b))


def _log_sigmoid(z):
    return jnp.minimum(z, 0.0) - jnp.log1p(jnp.exp(-jnp.abs(z)))


def _aug_placement():
    place = np.zeros((N_PAIRS, LANES, LANES), np.float32)
    for is_k in range(2):
        for pair in range(N_PAIRS // 2):
            for e in range(2):
                other = HEAD_DIM * (1 - e)
                ones_at = other + (AUG_C if is_k else AUG_ONE)
                c_at = other + (AUG_ONE if is_k else AUG_C)
                for n in range(3):
                    place[4 * is_k + pair, N_HEADS_FOX * n + 2 * pair + e, c_at + n] = -1.0 if is_k else 1.0
                    place[4 * is_k + pair, 3 * N_HEADS_FOX, ones_at + n] = 1.0
    return jnp.asarray(place, BF16)


def _qk_prep_fwd(proj, h1, w_fa_t, b_pad, gains, rope, seq):
    t = proj.shape[0]
    bt = ROW_BLOCK
    nsb = seq // bt
    seg = _segment_ones()
    rr = np.arange(bt)
    tri = jnp.asarray(rr[:, None] <= rr[None, :], BF16)

    def body(p_ref, h_ref, wfa_ref, b_ref, g_ref, cos_ref, sa_ref, sb_ref, seg_ref, tri_ref, place_ref,
             qk_ref, fa_ref, carry):
        @pl.when(pl.program_id(0) % nsb == 0)
        def _():
            carry[...] = jnp.zeros_like(carry)

        lane = lax.broadcasted_iota(jnp.int32, (bt, LANES), 1)
        fa = _dot_nt(h_ref[...], wfa_ref[...])
        fa_ref[...] = fa
        logf = jnp.where(lane < N_HEADS_FOX, _log_sigmoid(fa + b_ref[...]), 0.0)
        c_rows = _split_dot(logf.T[0:N_HEADS_FOX, :], tri_ref[...], 3) + carry[:, 0:1]
        carry[...] = jnp.broadcast_to(c_rows[:, bt - 1:bt], carry.shape)
        cblk = jnp.concatenate([c_rows, jnp.zeros((LANES - N_HEADS_FOX, bt), F32)], axis=0).T
        packed = jnp.where(lane == 3 * N_HEADS_FOX, 1.0, 0.0)
        rest = cblk * LOG2E
        for n in range(3):
            term = rest.astype(BF16).astype(F32)
            packed = packed + (pltpu.roll(term, N_HEADS_FOX * n, 1) if n else term)
            rest = rest - term
        packed = packed.astype(BF16)
        low = lane < HEAD_DIM

        for tile in range(2 * N_PAIRS):
            is_q, is_dil, grow = _tile_plan(tile)
            pair = tile % N_PAIRS
            src = _proj_tile(0 if is_q else 1, pair) * LANES
            xs = p_ref[:, src:src + LANES].astype(F32)
            r = lax.rsqrt(_split_dot(xs * xs, seg_ref[...], 2) * (1.0 / HEAD_DIM) + EPS)
            yv = xs * r * g_ref[grow:grow + 1, :]
            if is_dil:
                yv = (yv * cos_ref[...] + pltpu.roll(yv, LANES - ROPE_HALF, 1) * sa_ref[...]
                      + pltpu.roll(yv, ROPE_HALF, 1) * sb_ref[...])
                aug = jnp.zeros((bt, LANES), F32)
            else:
                aug = jnp.dot(packed, place_ref[(0 if is_q else N_PAIRS // 2) + pair], preferred_element_type=F32)
            if is_q:
                yv = yv * (HEAD_DIM ** -0.5 * LOG2E)
            dst = ((0 if is_q else N_HEADS) + 2 * pair) * LANES
            qk_ref[:, dst:dst + LANES] = jnp.where(low, yv, aug).astype(BF16)
            qk_ref[:, dst + LANES:dst + 2 * LANES] = jnp.where(low, aug, yv).astype(BF16)

    row128 = pl.BlockSpec((bt, LANES), lambda i: (i, 0))
    rope_spec = pl.BlockSpec((bt, LANES), lambda i: (i % nsb, 0))
    const = lambda shape: pl.BlockSpec(shape, lambda i: (0,) * len(shape))
    return pl.pallas_call(
        body, name="qk_prep_fwd", grid=(t // bt,),
        in_specs=[pl.BlockSpec((bt, 2 * D_MODEL), lambda i: (i, 0)), pl.BlockSpec((bt, D_MODEL), lambda i: (i, 0)),
                  const((LANES, D_MODEL)), const((1, LANES)), const((8, LANES)), rope_spec, rope_spec, rope_spec,
                  const((LANES, LANES)), const((bt, bt)), const((N_PAIRS, LANES, LANES))],
        out_specs=(pl.BlockSpec((bt, 2 * N_HEADS * LANES), lambda i: (i, 0)), row128),
        out_shape=(jax.ShapeDtypeStruct((t, 2 * N_HEADS * LANES), BF16), jax.ShapeDtypeStruct((t, LANES), F32)),
        scratch_shapes=[pltpu.VMEM((8, LANES), F32)], compiler_params=_params(1),
    )(proj, h1, w_fa_t, b_pad, gains, *rope, seg, tri, _aug_placement())


def _qk_prep_bwd(dq, dk, dqx, dkx, dv, proj, fa, b_pad, gains, rope, seq):
    t = proj.shape[0]
    bt = ROW_BLOCK
    nsb = seq // bt
    nblk = t // bt
    seg = _segment_ones()
    rr = np.arange(bt)
    triu = jnp.asarray(rr[:, None] >= rr[None, :], BF16)

    def body(dq_ref, dk_ref, dqx_ref, dkx_ref, dv_ref, p_ref, fa_ref, b_ref, g_ref, cos_ref, sa_ref, sb_ref, seg_ref,
             triu_ref, dp_ref, dg_ref, db_ref, carry):
        step = pl.program_id(0)

        @pl.when(step == 0)
        def _():
            dg_ref[...] = jnp.zeros_like(dg_ref)
            db_ref[...] = jnp.zeros_like(db_ref)

        @pl.when(step % nsb == 0)
        def _():
            carry[...] = jnp.zeros_like(carry)

        for tile in range(2 * N_PAIRS):
            is_q, is_dil, grow = _tile_plan(tile)
            first = _proj_tile(0 if is_q else 1, tile % N_PAIRS) * LANES
            cols = slice(first, first + LANES)
            src = dq_ref if is_q else dk_ref
            half = slice((tile % N_PAIRS) * LANES, (tile % N_PAIRS + 1) * LANES)
            dy = src[:, half]
            dy = dy * (HEAD_DIM ** -0.5 if is_q else LN2)
            if is_dil:
                dy = (dy * cos_ref[...] + pltpu.roll(dy * sa_ref[...], ROPE_HALF, 1)
                      + pltpu.roll(dy * sb_ref[...], LANES - ROPE_HALF, 1))
            xs = p_ref[:, cols].astype(F32)
            r = lax.rsqrt(_split_dot(xs * xs, seg_ref[...], 2) * (1.0 / HEAD_DIM) + EPS)
            xh = xs * r
            dg_ref[tile:tile + 1, :] += jnp.sum(dy * xh, axis=0, keepdims=True)
            dxh = dy * g_ref[grow:grow + 1, :]
            seg_mean = _split_dot(dxh * xh, seg_ref[...], 2) * (1.0 / HEAD_DIM)
            dp_ref[:, cols] = (r * (dxh - xh * seg_mean)).astype(BF16)

        lane = lax.broadcasted_iota(jnp.int32, (bt, LANES), 1)
        dc = jnp.zeros((bt, LANES), F32)
        for h in range(N_HEADS_FOX):
            other = (h // 2) * LANES + HEAD_DIM * (1 - h % 2)
            row_sum = dqx_ref[:, other + AUG_C:other + AUG_C + 1]
            col_sum = dkx_ref[:, other + AUG_ONE:other + AUG_ONE + 1]
            dc = jnp.where(lane == h, row_sum - col_sum, dc)
        d_rows = _split_dot(dc.T[0:N_HEADS_FOX, :], triu_ref[...], 3) + carry[:, 0:1]
        carry[...] = jnp.broadcast_to(d_rows[:, 0:1], carry.shape)
        dlogf = jnp.concatenate([d_rows, jnp.zeros((LANES - N_HEADS_FOX, bt), F32)], axis=0).T
        z = fa_ref[...] + b_ref[...]
        dfa = dlogf * (1.0 / (1.0 + jnp.exp(z)))
        db_ref[0:1, :] += jnp.sum(dfa, axis=0, keepdims=True)
        for group in range(2):
            first = _proj_tile(2, group * (N_PAIRS // 2)) * LANES
            dp_ref[:, first:first + W_GROUP] = dv_ref[:, group * W_GROUP:(group + 1) * W_GROUP]
        dp_ref[:, MAIN_COLS:PROJ_COLS] = dfa.astype(BF16)

    rev = lambda i: nblk - 1 - i
    row = lambda w: pl.BlockSpec((bt, w), lambda i: (rev(i), 0))
    rope_spec = pl.BlockSpec((bt, LANES), lambda i: (rev(i) % nsb, 0))
    const = lambda shape: pl.BlockSpec(shape, lambda i: (0, 0))
    return pl.pallas_call(
        body, name="qk_prep_bwd", grid=(nblk,),
        in_specs=[row(D_MODEL), row(D_MODEL), row(W_GROUP), row(W_GROUP), row(D_MODEL), row(2 * D_MODEL), row(LANES),
                  const((1, LANES)), const((8, LANES)), rope_spec, rope_spec, rope_spec, const((LANES, LANES)),
                  const((bt, bt))],
        out_specs=(row(PROJ_COLS), const((2 * N_PAIRS, LANES)), const((8, LANES))),
        out_shape=(jax.ShapeDtypeStruct((t, PROJ_COLS), BF16),
                   jax.ShapeDtypeStruct((2 * N_PAIRS, LANES), F32), jax.ShapeDtypeStruct((8, LANES), F32)),
        scratch_shapes=[pltpu.VMEM((8, LANES), F32)], compiler_params=_params(1),
    )(dq, dk, dqx, dkx, dv, proj, fa, b_pad, gains, *rope, seg, triu)


def _bias_tables(seq, keys_first):
    nb = seq // ATT_BLOCK
    idx = np.arange(ATT_BLOCK)
    q_idx, k_idx = (idx[None, None, :], idx[None, :, None]) if keys_first else (idx[None, :, None], idx[None, None, :])
    dist = np.arange(nb)[:, None, None] * ATT_BLOCK + q_idx - k_idx
    causal = dist >= 0
    count = np.zeros(dist.shape, np.int32)
    for window, dilation in DILATION_PAIRS:
        count = count + (causal & (dist % dilation == 0) & (dist <= window))
    fox = np.where(causal, 0.0, NEG)
    dil = np.where(count == 3, math.log2(3.0), np.where(count == 2, 1.0, np.where(count == 1, 0.0, NEG)))
    return jnp.asarray(np.stack([fox, dil], axis=0), F32)


def _attn_specs(seq):
    nb = seq // ATT_BLOCK
    col = pl.BlockSpec((seq, LANES), lambda b, j: (b, j))
    heads = lambda off: pl.BlockSpec((seq, 2 * LANES), lambda b, j: (b, off + j))
    v_spec = pl.BlockSpec((seq, LANES), lambda b, j: (b, _proj_tile(2, j)))
    table_spec = pl.BlockSpec((1, nb, ATT_BLOCK, ATT_BLOCK), lambda b, j: (j // (N_PAIRS // 2), 0, 0, 0))
    return col, heads, v_spec, table_spec


def _head_lanes(e, shape, axis):
    pos = lax.broadcasted_iota(jnp.int32, shape, axis)
    return pos < HEAD_DIM if e == 0 else pos >= HEAD_DIM


def _attn_fwd(qk, proj, tables, seq, hosted=None):
    t = qk.shape[0]
    nb = seq // ATT_BLOCK
    blk = ATT_BLOCK

    def body(q_ref, k_ref, v_ref, tab_ref, o_ref, lse_ref):
        mine = [_head_lanes(e, (seq, LANES), 1) for e in range(2)]
        lane = lax.broadcasted_iota(jnp.int32, (seq, LANES), 1)
        v_aug = [jnp.where(mine[e], v_ref[...], (lane == HEAD_DIM * (1 - e)).astype(BF16)) for e in range(2)]
        def scores(i, e):
            heads_e = slice(e * LANES, (e + 1) * LANES)
            s = _dot_nt(q_ref[i * blk:(i + 1) * blk, heads_e], k_ref[0:(i + 1) * blk, heads_e])
            s = jnp.concatenate([s[:, jj * blk:(jj + 1) * blk] + tab_ref[0, i - jj] for jj in range(i + 1)], axis=1)
            return s, jnp.max(s, axis=1, keepdims=True)

        chains = [(i, e) for i in reversed(range(nb)) for e in range(2)]
        ahead = 2
        pending = [scores(*chain) for chain in chains[:ahead]]
        done = {}
        for n, (i, e) in enumerate(chains):
            s, m = pending.pop(0)
            if n + ahead < len(chains):
                pending.append(scores(*chains[n + ahead]))
            acc = jnp.dot(jnp.exp2(s - m).astype(BF16), v_aug[e][0:(i + 1) * blk], preferred_element_type=F32)
            ones_at = HEAD_DIM * (1 - e)
            l = acc[:, ones_at:ones_at + 1]
            done[e] = (acc / l, m + jnp.log2(l))
            if e == 1:
                rows = slice(i * blk, (i + 1) * blk)
                o_ref[rows, :] = jnp.where(mine[0][rows], done[0][0], done[1][0]).astype(o_ref.dtype)
                lse_ref[rows, :] = jnp.where(mine[0][rows], done[0][1], done[1][1])

    col, heads, v_spec, table_spec = _attn_specs(seq)
    grid = (t // seq, N_PAIRS)
    h_specs, h_shapes, h_args, h_scratch = _hosted_parts(hosted)
    return pl.pallas_call(
        _host(body, 4, 2, 0, hosted, grid), name="attn_fwd", grid=grid,
        in_specs=[heads(0), heads(N_PAIRS), v_spec, table_spec] + h_specs,
        out_specs=tuple([col, col] + h_specs),
        out_shape=tuple([jax.ShapeDtypeStruct((t, D_MODEL), BF16), jax.ShapeDtypeStruct((t, D_MODEL), F32)] + h_shapes),
        scratch_shapes=h_scratch, compiler_params=_params(2),
    )(qk, qk, proj, tables, *h_args)


def _attn_bwd(qk, proj, tables, o, lse, do, seq, hosted=None):
    t = qk.shape[0]
    nb = seq // ATT_BLOCK
    blk = ATT_BLOCK
    group = math.gcd(nb, ATT_GROUP)

    def body(q_ref, k_ref, v_ref, tab_ref, o_ref, lse_ref, do_ref,
             dq_ref, dk_ref, dv_ref, dqx_ref, dkx_ref, dk_acc, dv_acc):
        mine = [_head_lanes(e, (blk, LANES), 1) for e in range(2)]
        top = _head_lanes(0, (LANES, blk), 0)
        head_rows = lax.broadcasted_iota(jnp.int32, (8, LANES), 0)
        head_of_lane = lax.broadcasted_iota(jnp.int32, (8, LANES), 1) // HEAD_DIM
        head_sel = (head_rows == head_of_lane).astype(BF16)
        dk_acc[...] = jnp.zeros_like(dk_acc)
        dv_acc[...] = jnp.zeros_like(dv_acc)

        def block_rows(i):
            return pl.ds(pl.multiple_of(i * blk, blk), blk)

        def q_group(g, _):
            base = g * group
            qs, doe, delta, lse_e = [], [], [], []
            for b in range(group):
                rows = block_rows(base + b)
                qs.append([q_ref[rows, e * LANES:(e + 1) * LANES] for e in range(2)])
                do_blk = do_ref[rows, :]
                doe.append([jnp.where(mine[e], do_blk, jnp.zeros_like(do_blk)) for e in range(2)])
                delta_t = _split_dot_nt(head_sel, do_blk.astype(F32) * o_ref[rows, :].astype(F32), 3)
                lse_t = _split_dot_nt(head_sel, lse_ref[rows, :], 3) * (1.0 / HEAD_DIM)
                delta.append([delta_t[e:e + 1, :] for e in range(2)])
                lse_e.append([lse_t[e:e + 1, :] for e in range(2)])

            def key_block(dq_t, jj, members):
                krows = block_rows(jj)
                v = v_ref[krows, :]
                dq_t = [list(d) for d in dq_t]
                lo, hi = slice(0, blk // 2), slice(blk // 2, blk)
                dv_part = [None, None]
                add = lambda acc, part: part if acc is None else acc + part

                def probs(k_sub, v_sub, keys, queries, b, e, dist):
                    q_sub, do_sub = qs[b][e][queries], doe[b][e][queries]
                    p_t = jnp.exp2(_dot_nt(k_sub, q_sub) + tab_ref[0, dist, keys, queries] - lse_e[b][e][:, queries])
                    ds_t = (p_t * (_dot_nt(v_sub, do_sub) - delta[b][e][:, queries])).astype(BF16)
                    return p_t.astype(BF16), ds_t, q_sub, do_sub, k_sub

                def outputs(tile):
                    p_t, ds_t, q_sub, do_sub, k_sub = tile
                    return (jnp.dot(p_t, do_sub, preferred_element_type=F32),
                            jnp.dot(ds_t, q_sub, preferred_element_type=F32), _dot_tn(k_sub, ds_t))

                for e in range(2):
                    k_e = k_ref[krows, e * LANES:(e + 1) * LANES]
                    dk_part = [None, None]
                    tiles = []
                    for b, dist in members:
                        if isinstance(dist, int) and dist == 0:
                            tiles.append((b, probs(k_e[lo], v[lo], lo, slice(0, blk), b, e, dist),
                                          probs(k_e[hi], v[hi], hi, hi, b, e, dist)))
                        else:
                            tiles.append((b, probs(k_e, v, slice(0, blk), slice(0, blk), b, e, dist), None))
                    for b, first, second in tiles:
                        if second is not None:
                            dv_a, dk_a, dq_a = outputs(first)
                            dv_b, dk_b, dq_b = outputs(second)
                            halves = ((dv_a, dk_a), (dv_b, dk_b))
                            dq = jnp.concatenate([dq_a[:, lo], dq_a[:, hi] + dq_b], axis=1)
                        else:
                            dv_f, dk_f, dq = outputs(first)
                            halves = ((dv_f[lo], dk_f[lo]), (dv_f[hi], dk_f[hi]))
                        for n, (dv_h, dk_h) in enumerate(halves):
                            dv_part[n] = add(dv_part[n], dv_h)
                            dk_part[n] = add(dk_part[n], dk_h)
                        dq_t[b][e] = dq_t[b][e] + dq
                    dk_acc[e, krows, :] += jnp.concatenate(dk_part, axis=0)
                dv_acc[krows, :] += jnp.concatenate(dv_part, axis=0)
                return tuple(tuple(d) for d in dq_t)

            zacc = jnp.zeros((LANES, blk), F32)
            dq_t = tuple((zacc, zacc) for _ in range(group))
            dq_t = lax.fori_loop(
                0, base, lambda jj, st: key_block(st, jj, [(b, base + b - jj) for b in range(group)]), dq_t)
            for a in range(group):
                dq_t = key_block(dq_t, base + a, [(b, b - a) for b in range(a, group)])
            for b in range(group):
                rows = block_rows(base + b)
                dq_ref[rows, :] = jnp.where(top, dq_t[b][0], dq_t[b][1]).T
                dqx_ref[rows, :] = jnp.where(top, dq_t[b][1], dq_t[b][0]).T
            return 0

        lax.fori_loop(0, nb // group, q_group, 0)
        lo = _head_lanes(0, (seq, LANES), 1)
        dk_ref[...] = jnp.where(lo, dk_acc[0], dk_acc[1])
        dkx_ref[...] = jnp.where(lo, dk_acc[1], dk_acc[0])
        dv_ref[...] = dv_acc[...].astype(dv_ref.dtype)

    col, heads, v_spec, table_spec = _attn_specs(seq)
    grid = (t // seq, N_PAIRS)
    h_specs, h_shapes, h_args, h_scratch = _hosted_parts(hosted)
    f32_out = jax.ShapeDtypeStruct((t, D_MODEL), F32)
    return pl.pallas_call(
        _host(body, 7, 5, 2, hosted, grid), name="attn_bwd", grid=grid,
        in_specs=[heads(0), heads(N_PAIRS), v_spec, table_spec, col, col, col] + h_specs,
        out_specs=tuple([col] * 5 + h_specs),
        out_shape=tuple([f32_out, f32_out, jax.ShapeDtypeStruct((t, D_MODEL), BF16), f32_out, f32_out] + h_shapes),
        scratch_shapes=[pltpu.VMEM((2, seq, LANES), F32), pltpu.VMEM((seq, LANES), F32)] + h_scratch,
        compiler_params=_params(2),
    )(qk, qk, proj, tables, o, lse, do, *h_args)


def _row_block(t):
    return 2 * ROW_BLOCK if t % (2 * ROW_BLOCK) == 0 else ROW_BLOCK


def _out_proj_ffn_norm(o, g_out, w_out, x, g_ffn):
    t = o.shape[0]
    bt = _row_block(t)

    def body(o_ref, go_ref, w_ref, x_ref, gf_ref, x2_ref, h2_ref):
        on = _group_norm(o_ref[...], go_ref[...])
        x2 = x_ref[...] + jnp.dot(on, w_ref[...], preferred_element_type=F32)
        x2_ref[...] = x2
        r2 = lax.rsqrt(jnp.mean(x2 * x2, axis=-1, keepdims=True) + EPS)
        h2_ref[...] = (x2 * r2 * gf_ref[...]).astype(BF16)

    row = pl.BlockSpec((bt, D_MODEL), lambda i: (i, 0))
    vec = pl.BlockSpec((1, D_MODEL), lambda i: (0, 0))
    return pl.pallas_call(
        body, name="out_proj", grid=(t // bt,),
        in_specs=[row, vec, pl.BlockSpec((D_MODEL, D_MODEL), lambda i: (0, 0)), row, vec],
        out_specs=(row, row),
        out_shape=(jax.ShapeDtypeStruct((t, D_MODEL), F32), jax.ShapeDtypeStruct((t, D_MODEL), BF16)),
        compiler_params=_params(1),
    )(o, g_out, w_out, x, g_ffn)


def _ffn_gate_up(h2, w_gate_t, w_up_t):
    t = h2.shape[0]
    bt = _row_block(t)
    bn = _divisor_block(D_FF, WIDE_BLOCK)

    def body(h_ref, wg_ref, wu_ref, a_ref, u_ref):
        a_ref[...] = _dot_nt(h_ref[...], wg_ref[...]).astype(BF16)
        u_ref[...] = _dot_nt(h_ref[...], wu_ref[...]).astype(BF16)

    blk = pl.BlockSpec((bt, bn), lambda j, i: (i, j))
    w_blk = pl.BlockSpec((bn, D_MODEL), lambda j, i: (j, 0))
    shape = jax.ShapeDtypeStruct((t, D_FF), BF16)
    return pl.pallas_call(
        body, name="ffn_gate_up", grid=(D_FF // bn, t // bt),
        in_specs=[pl.BlockSpec((bt, D_MODEL), lambda j, i: (i, 0)), w_blk, w_blk],
        out_specs=(blk, blk), out_shape=(shape, shape), compiler_params=_params(2),
    )(h2, w_gate_t, w_up_t)


def _ffn_down_grad(dy16, w_down, a, u):
    t = a.shape[0]
    bt = _row_block(t)
    bn = _divisor_block(D_FF, WIDE_BLOCK)

    def body(dy_ref, w_ref, a_ref, u_ref, da_ref, du_ref):
        df = _dot_nt(dy_ref[...], w_ref[...])
        av = a_ref[...].astype(F32)
        sg = jax.nn.sigmoid(av)
        da_ref[...] = (df * u_ref[...].astype(F32) * sg * (1.0 + av * (1.0 - sg))).astype(BF16)
        du_ref[...] = (df * av * sg).astype(BF16)

    blk = pl.BlockSpec((bt, bn), lambda j, i: (i, j))
    shape = jax.ShapeDtypeStruct((t, D_FF), BF16)
    return pl.pallas_call(
        body, name="d_ffn_down", grid=(D_FF // bn, t // bt),
        in_specs=[pl.BlockSpec((bt, D_MODEL), lambda j, i: (i, 0)), pl.BlockSpec((bn, D_MODEL), lambda j, i: (j, 0)),
                  blk, blk],
        out_specs=(blk, blk), out_shape=(shape, shape), compiler_params=_params(2),
    )(dy16, w_down, a, u)


def _ffn_down_loss(a, u, w_down, x2, target):
    t, w = x2.shape
    bt = ROW_BLOCK

    def body(a_ref, u_ref, w_ref, x_ref, t_ref, dy_ref, dy16_ref, loss_ref):
        @pl.when(pl.program_id(0) == 0)
        def _():
            loss_ref[...] = jnp.zeros_like(loss_ref)

        yv = x_ref[...]
        for c in range(0, D_FF, WIDE_BLOCK):
            f = _silu_mul(a_ref[:, c:c + WIDE_BLOCK], u_ref[:, c:c + WIDE_BLOCK])
            yv = yv + jnp.dot(f, w_ref[c:c + WIDE_BLOCK, :], preferred_element_type=F32)
        err = yv - t_ref[...]
        dy = err * (1.0 / w)
        dy_ref[...] = dy
        dy16_ref[...] = dy.astype(BF16)
        loss_ref[...] += 0.5 * jnp.sum(jnp.mean(err * err, axis=-1, keepdims=True), axis=0, keepdims=True)

    row = pl.BlockSpec((bt, w), lambda i: (i, 0))
    return pl.pallas_call(
        body, name="ffn_down_loss", grid=(t // bt,),
        in_specs=[pl.BlockSpec((bt, D_FF), lambda i: (i, 0)), pl.BlockSpec((bt, D_FF), lambda i: (i, 0)),
                  pl.BlockSpec((D_FF, w), lambda i: (0, 0)), row, row],
        out_specs=(row, row, pl.BlockSpec((8, LANES), lambda i: (0, 0))),
        out_shape=(jax.ShapeDtypeStruct((t, w), F32), jax.ShapeDtypeStruct((t, w), BF16),
                   jax.ShapeDtypeStruct((8, LANES), F32)),
        compiler_params=_params(1),
    )(a, u, w_down, x2, target)


def _adamw(parts, w, m, v, *, name):
    _, rows, cols = w.shape
    br = rows if rows <= 512 else 256
    assert rows % br == 0

    def body(p_ref, w_ref, m_ref, v_ref, g_ref, d_ref, nm_ref, nv_ref):
        g = p_ref[0].astype(F32)
        for r in range(1, N_DEV):
            g = g + p_ref[r].astype(F32)
        m2 = ADAM_B1 * m_ref[0] + (1.0 - ADAM_B1) * g
        v2 = ADAM_B2 * v_ref[0] + (1.0 - ADAM_B2) * jnp.square(g)
        m_hat = m2 / (1.0 - ADAM_B1 ** ADAM_STEP)
        v_hat = v2 / (1.0 - ADAM_B2 ** ADAM_STEP)
        g_ref[0] = g
        d_ref[0] = -ADAM_LR * (m_hat / (jnp.sqrt(v_hat) + ADAM_EPS) + ADAM_WD * w_ref[0])
        nm_ref[0] = m2
        nv_ref[0] = v2

    blk = pl.BlockSpec((1, br, cols), lambda i: (0, i, 0))
    shape = jax.ShapeDtypeStruct((1, rows, cols), F32)
    return pl.pallas_call(
        body, name=name, grid=(rows // br,),
        in_specs=[pl.BlockSpec((N_DEV, br, cols), lambda i: (0, i, 0)), blk, blk, blk],
        out_specs=(blk, blk, blk, blk), out_shape=(shape, shape, shape, shape), compiler_params=_params(1),
    )(parts, w, m, v)


_QA, _KA, _VA, _FA, _QD, _KD, _VD = (0, 512), (512, 1024), (1024, 1536), (1536, 1544), (1544, 2056), (2056, 2568), (2568, 3080)
_MAIN_ORDER = (_QA, _QD, _KA, _KD, _VA, _VD)
MAIN_COLS = 3 * D_MODEL
PROJ_COLS = MAIN_COLS + LANES
COL_SHARDED = ("w_in", "w_gate", "w_up")


def _swap(w):
    return jnp.transpose(w, (0, 2, 1))


def _w_in_to_kernel(w_t):
    main = jnp.concatenate([w_t[a:b] for a, b in _MAIN_ORDER], axis=0)
    forget = jnp.pad(w_t[_FA[0]:_FA[1]], ((0, LANES - N_HEADS_FOX), (0, 0)))
    return main, forget


def _w_in_from_kernel(g_t):
    pos = {span: i * W_GROUP for i, span in enumerate(_MAIN_ORDER)}
    parts = []
    for span in (_QA, _KA, _VA, _FA, _QD, _KD, _VD):
        if span == _FA:
            parts.append(g_t[MAIN_COLS:MAIN_COLS + N_HEADS_FOX])
        else:
            parts.append(g_t[pos[span]:pos[span] + W_GROUP])
    return jnp.concatenate(parts, axis=0)


def _pack_small(vals):
    rows = []
    for name, _, n_rows in SMALL_LAYOUT:
        flat = vals[name].reshape(-1).astype(F32)
        rows.append(jnp.pad(flat, (0, n_rows * LANES - flat.shape[0])).reshape(n_rows, LANES))
    packed = jnp.concatenate(rows, axis=0)
    return jnp.pad(packed, ((0, SMALL_ROWS - packed.shape[0]), (0, 0)))


def _unpack_small(packed, like):
    out = {}
    for name, row, n_rows in SMALL_LAYOUT:
        n = like[name].size
        out[name] = packed[row:row + n_rows].reshape(-1)[:n].reshape(like[name].shape)
    return out


def _device_step(x, target, small, shards):
    bsz, seq, _ = x.shape
    t = bsz * seq
    xf = x.reshape(t, D_MODEL)
    tf = target.reshape(t, D_MODEL)
    row = lambda v: v.reshape(1, -1)
    g_out = jnp.concatenate([small["g_out_fox"], small["g_out_dil"]]).reshape(1, D_MODEL)
    gains = jnp.concatenate(
        [jnp.tile(small[n].reshape(1, HEAD_DIM), (1, 2)) for n in ("g_q_fox", "g_q_dil", "g_k_fox", "g_k_dil")]
        + [jnp.zeros((4, LANES), F32)], axis=0)
    b_pad = jnp.pad(small["b_forget"].reshape(1, N_HEADS_FOX), ((0, 0), (0, LANES - N_HEADS_FOX)))
    rope = _rope_tables(seq)
    tables_qk = _bias_tables(seq, keys_first=False)
    tables_kq = _bias_tables(seq, keys_first=True)

    h1, g_in = _rmsnorm_fwd(xf, row(small["g_mix"]), group=D_MODEL, name="norm_mix",
                            hosted=_ChipGather([(shards["w_in"], False)]))
    w_main_t, w_fa_t = _w_in_to_kernel(g_in.reshape(IN_COLS, D_MODEL))
    w_in_all_t = jnp.concatenate([w_main_t, w_fa_t], axis=0)
    proj = _matmul_nt(h1, w_main_t, name="in_proj", out_dtype=BF16)
    qk, fa = _qk_prep_fwd(proj, h1, w_fa_t, b_pad, gains, rope, seq)
    late = _Exchange([(shards[n], False) for n in ("w_out", "w_gate", "w_up", "w_down")])
    o, lse, g_out_w, g_gate, g_up, g_down = _attn_fwd(qk, proj, tables_qk, seq, hosted=late)
    w_out = g_out_w.reshape(D_MODEL, D_MODEL)
    w_gate_t = g_gate.reshape(D_FF, D_MODEL)
    w_up_t = g_up.reshape(D_FF, D_MODEL)
    w_down = g_down.reshape(D_FF, D_MODEL)
    x2, h2 = _out_proj_ffn_norm(o, g_out, w_out, xf, row(small["g_ffn"]))
    a, u = _ffn_gate_up(h2, w_gate_t, w_up_t)
    dy, dy16, loss_tile = _ffn_down_loss(a, u, w_down, x2, tf)

    da, du = _ffn_down_grad(dy16, w_down, a, u)
    gw_down = _matmul_tn((a, u), dy16, name="gw_down", make_a=_silu_mul)
    gw_gate_t = _matmul_tn((da,), h2, name="gw_gate")
    gw_up_t = _matmul_tn((du,), h2, name="gw_up")
    dh2_gate = _matmul_rows(da, w_gate_t, name="d_ffn_gate")
    dx2, dg_ffn = _norm_input_grad([(du, w_up_t, True)], x2, row(small["g_ffn"]), group=D_MODEL,
                                   name="d_ffn_up", out_dtypes=(F32,), resid=dy, init=dh2_gate)
    gw_out = _matmul_tn((o,), dx2, name="gw_out", make_a=_group_norm, vectors=(g_out,))
    do, dg_out = _norm_input_grad([(dx2, w_out, False)], o, g_out, group=W_GROUP, name="d_out_proj",
                                  out_dtypes=(BF16,))

    shard_rows = lambda g: g.reshape(N_DEV, g.shape[0] // N_DEV, g.shape[1])
    ffn_grads = _Exchange([(shard_rows(g), True) for g in (gw_out, gw_gate_t, gw_up_t, gw_down)])
    dq, dk, dv, dqx, dkx, p_out, p_gate, p_up, p_down = _attn_bwd(qk, proj, tables_kq, o, lse, do, seq, hosted=ffn_grads)
    dproj, dgains, db = _qk_prep_bwd(dq, dk, dqx, dkx, dv, proj, fa, b_pad, gains, rope, seq)
    gw_in_t = _matmul_tn((dproj,), h1, name="gw_in")
    in_grad = _Exchange([(shard_rows(_w_in_from_kernel(gw_in_t)), True)])
    dx, dg_mix, p_in = _norm_input_grad([(dproj, w_in_all_t, True)], xf, row(small["g_mix"]), group=D_MODEL,
                                        name="d_in_proj", out_dtypes=(F32,), resid=dx2, hosted=in_grad)

    fold = lambda rows: jnp.sum(rows[:, :HEAD_DIM] + rows[:, HEAD_DIM:], axis=0)
    half = N_PAIRS // 2
    gsmall = {
        "g_mix": dg_mix, "g_ffn": dg_ffn, "g_out_fox": dg_out[0, :W_GROUP], "g_out_dil": dg_out[0, W_GROUP:],
        "g_q_fox": fold(dgains[0:half]), "g_q_dil": fold(dgains[half:N_PAIRS]),
        "g_k_fox": fold(dgains[N_PAIRS:N_PAIRS + half]), "g_k_dil": fold(dgains[N_PAIRS + half:]),
        "b_forget": db[0, :N_HEADS_FOX],
    }
    packed = _pack_small(gsmall).at[LOSS_ROW].set(loss_tile[0])
    (p_small,) = _exchange("small_exchange", [(packed, False)])
    parts = {"w_in": p_in, "w_out": p_out, "w_gate": p_gate, "w_up": p_up, "w_down": p_down}
    return dx.reshape(x.shape), parts, p_small


def kernel(x, g_mix, w_in, b_forget, g_q_fox, g_k_fox, g_q_dil, g_k_dil, g_out_fox, g_out_dil, w_out, g_ffn, w_gate, w_up, w_down, loss_target, m_g_mix, m_w_in, m_b_forget, m_g_q_fox, m_g_k_fox, m_g_q_dil, m_g_k_dil, m_g_out_fox, m_g_out_dil, m_w_out, m_g_ffn, m_w_gate, m_w_up, m_w_down, v_g_mix, v_w_in, v_b_forget, v_g_q_fox, v_g_k_fox, v_g_q_dil, v_g_k_dil, v_g_out_fox, v_g_out_dil, v_w_out, v_g_ffn, v_w_gate, v_w_up, v_w_down):
    args = dict(locals())
    small_names = [name for name, _, _ in SMALL_LAYOUT]
    big_names = ["w_in", "w_out", "w_gate", "w_up", "w_down"]
    small = {n: args[n][0] for n in small_names}

    as_rows = lambda n, w: _swap(w) if n in COL_SHARDED else w
    shards = {n: as_rows(n, args[n])[0].astype(BF16) for n in big_names}
    grad_x, parts, p_small = _device_step(x, loss_target, small, shards)

    grads, deltas, new_m, new_v = {}, {}, {}, {}
    for n in big_names:
        res = _adamw(parts[n], as_rows(n, args[n]), as_rows(n, args["m_" + n]), as_rows(n, args["v_" + n]),
                     name="adamw_" + n)
        grads[n], deltas[n], new_m[n], new_v[n] = [as_rows(n, r) for r in res]
    res = _adamw(p_small, _pack_small(small)[None], _pack_small({n: args["m_" + n][0] for n in small_names})[None],
                 _pack_small({n: args["v_" + n][0] for n in small_names})[None], name="adamw_small")
    loss = res[0][0, LOSS_ROW, 0]
    for dst, packed_res in zip((grads, deltas, new_m, new_v), res):
        for n, val in _unpack_small(packed_res[0], small).items():
            dst[n] = val[None]

    order = ["g_mix", "w_in", "b_forget", "g_q_fox", "g_k_fox", "g_q_dil", "g_k_dil", "g_out_fox", "g_out_dil",
             "w_out", "g_ffn", "w_gate", "w_up", "w_down"]
    return (loss, grad_x, *[grads[n] for n in order], *[deltas[n] for n in order],
            *[new_m[n] for n in order], *[new_v[n] for n in order])
```

```python
import functools
import math

import jax
import jax.numpy as jnp
import numpy as np
from jax import lax
from jax.experimental import pallas as pl
from jax.experimental.pallas import tpu as pltpu

F32 = jnp.float32
BF16 = jnp.bfloat16

D_MODEL = 1024
HEAD_DIM = 64
LANES = 128
N_PAIRS = D_MODEL // LANES
N_HEADS = 2 * N_PAIRS
N_HEADS_FOX = 8
W_GROUP = 512
D_FF = 2816
IN_COLS = 3080
DILATION_PAIRS = ((128, 1), (512, 4), (2048, 16))
ROPE_THETA = 500000.0
ROPE_DIM = 16
ROPE_HALF = ROPE_DIM // 2
EPS = 1e-6
NEG = -1e30
LOG2E = 1.4426950408889634
LN2 = 0.6931471805599453
AUG_ONE = 0
AUG_C = 3
N_DEV = 8

ADAM_LR = 0.001
ADAM_B1 = 0.9
ADAM_B2 = 0.999
ADAM_EPS = 1e-08
ADAM_WD = 0.01
ADAM_STEP = 10

ROW_BLOCK = 512
TOKEN_STEP = 2048
WIDE_BLOCK = D_FF // 2
ATT_BLOCK = 512
ATT_GROUP = 4
VMEM_LIMIT = 56 * 1024 * 1024
MATMUL_VMEM_BUDGET = 44 * 1024 * 1024

SMALL_ROWS = 32
SMALL_LAYOUT = (("g_mix", 0, 8), ("g_ffn", 8, 8), ("g_out_fox", 16, 4), ("g_out_dil", 20, 4),
                ("g_q_fox", 24, 1), ("g_k_fox", 25, 1), ("g_q_dil", 26, 1), ("g_k_dil", 27, 1),
                ("b_forget", 28, 1))
LOSS_ROW = 29


def _params(n_grid):
    return pltpu.CompilerParams(dimension_semantics=("arbitrary",) * n_grid, vmem_limit_bytes=VMEM_LIMIT)


def _divisor_block(n, cap):
    best = None
    for b in range(LANES, min(n, cap) + 1, LANES):
        if n % b == 0:
            best = b
    assert best is not None, n
    return best


def _split_dot(a, b_exact, terms):
    acc = None
    rest = a
    for _ in range(terms):
        hi = rest.astype(BF16)
        part = jnp.dot(hi, b_exact, preferred_element_type=F32)
        acc = part if acc is None else acc + part
        rest = rest - hi.astype(F32)
    return acc


def _split_dot_nt(a_exact, b, terms):
    acc = None
    rest = b
    for _ in range(terms):
        hi = rest.astype(BF16)
        part = _dot_nt(a_exact, hi)
        acc = part if acc is None else acc + part
        rest = rest - hi.astype(F32)
    return acc


def _dot_nt(a, b):
    return lax.dot_general(a, b, (((1,), (1,)), ((), ())), preferred_element_type=F32)


def _dot_tn(a, b):
    return lax.dot_general(a, b, (((0,), (0,)), ((), ())), preferred_element_type=F32)


class _Exchange:
    def __init__(self, items):
        self.items = items
        self.n = len(items)
        self.arrays = [a for a, _ in items]
        self.out_shape = [jax.ShapeDtypeStruct((N_DEV,) + tuple(a.shape[1:] if sc else a.shape), a.dtype)
                          for a, sc in items]
        self.specs = [pl.BlockSpec(memory_space=pl.ANY)] * self.n
        self.scratch = [pltpu.SemaphoreType.DMA((self.n, N_DEV - 1)), pltpu.SemaphoreType.DMA((self.n, N_DEV - 1)),
                        pltpu.SemaphoreType.DMA((self.n,))]

    def run(self, ins, outs, sems, first, last, compute):
        send_sems, recv_sems, local_sems = sems
        x, y, c = lax.axis_index("x"), lax.axis_index("y"), lax.axis_index("c")
        me = 4 * x + 2 * y + c
        local, remote = [], []
        for k, (_, scatter) in enumerate(self.items):
            own = ins[k].at[me] if scatter else ins[k]
            local.append(pltpu.make_async_copy(own, outs[k].at[me], local_sems.at[k]))
        for r in range(1, N_DEV):
            px = 1 - x if r & 4 else x
            py = 1 - y if r & 2 else y
            pc = 1 - c if r & 1 else c
            peer = 4 * px + 2 * py + pc
            for k, (_, scatter) in enumerate(self.items):
                src = ins[k].at[peer] if scatter else ins[k]
                remote.append(pltpu.make_async_remote_copy(
                    src_ref=src, dst_ref=outs[k].at[me],
                    send_sem=send_sems.at[k, r - 1], recv_sem=recv_sems.at[k, r - 1],
                    device_id=(px, py, pc), device_id_type=pl.DeviceIdType.MESH))

        def start():
            for cp in local + remote:
                cp.start()

        def finish():
            for cp in remote:
                cp.wait_recv()
            for cp in remote:
                cp.wait_send()
            for cp in local:
                cp.wait()

        _run_phases(first, last, start, compute, finish)


def _run_phases(first, last, start, compute, finish):
    if first is None:
        start()
        compute()
        finish()
    else:
        pl.when(first)(start)
        compute()
        pl.when(last)(finish)


class _ChipGather(_Exchange):
    def run(self, ins, outs, sems, first, last, compute):
        send_sems, recv_sems, local_sems = sems
        x, y, c = lax.axis_index("x"), lax.axis_index("y"), lax.axis_index("c")
        sibling = (x, y, 1 - c)
        chips = [(1 - x, y), (x, 1 - y), (1 - x, 1 - y)]
        slot = lambda px, py, pc: 4 * px + 2 * py + pc

        def copy(k, n, src, dst_slot, to):
            return pltpu.make_async_remote_copy(
                src_ref=src, dst_ref=outs[k].at[dst_slot], send_sem=send_sems.at[k, n], recv_sem=recv_sems.at[k, n],
                device_id=to, device_id_type=pl.DeviceIdType.MESH)

        local, own, passed, arrivals = [], [], [], []
        for k in range(self.n):
            me = slot(x, y, c)
            local.append(pltpu.make_async_copy(ins[k], outs[k].at[me], local_sems.at[k]))
            own.append(copy(k, 0, ins[k], me, sibling))
            arrivals.append(copy(k, 0, ins[k], slot(*sibling), sibling))
            for j, chip in enumerate(chips):
                theirs = slot(*chip, c)
                own.append(copy(k, 1 + j, ins[k], me, (*chip, c)))
                passed.append((copy(k, 1 + j, ins[k], theirs, sibling),
                               copy(k, 4 + j, outs[k].at[theirs], theirs, sibling)))
                arrivals.append(copy(k, 4 + j, ins[k], slot(*chip, 1 - c), sibling))

        def start():
            for cp in local + own:
                cp.start()

        def finish():
            for landed, onward in passed:
                landed.wait_recv()
                onward.start()
            for cp in arrivals:
                cp.wait_recv()
            for cp in own + [onward for _, onward in passed]:
                cp.wait_send()
            for cp in local:
                cp.wait()

        _run_phases(first, last, start, compute, finish)


def _grid_ends(grid):
    ids = [pl.program_id(d) for d in range(len(grid))]
    first = functools.reduce(jnp.logical_and, [i == 0 for i in ids])
    last = functools.reduce(jnp.logical_and, [i == g - 1 for i, g in zip(ids, grid)])
    return first, last


def _host(core, n_in, n_out, n_scratch, hosted, grid):
    if hosted is None:
        return core
    nh = hosted.n

    def body(*refs):
        ins, rest = refs[:n_in], refs[n_in:]
        h_ins, rest = rest[:nh], rest[nh:]
        outs, rest = rest[:n_out], rest[n_out:]
        h_outs, rest = rest[:nh], rest[nh:]
        scratch, sems = rest[:n_scratch], rest[n_scratch:]
        first, last = _grid_ends(grid)
        hosted.run(h_ins, h_outs, sems, first, last, lambda: core(*ins, *outs, *scratch))

    return body


def _hosted_parts(hosted):
    if hosted is None:
        return [], [], [], []
    return list(hosted.specs), list(hosted.out_shape), list(hosted.arrays), list(hosted.scratch)


def _exchange(name, items):
    ex = _Exchange(items)
    n = ex.n

    def body(*refs):
        ex.run(refs[:n], refs[n:2 * n], refs[2 * n:], None, None, lambda: None)

    return pl.pallas_call(
        body, name=name, out_shape=tuple(ex.out_shape), in_specs=ex.specs, out_specs=tuple(ex.specs),
        scratch_shapes=ex.scratch,
    )(*ex.arrays)


def _matmul_blocks(t, k, n, a_bytes, o_bytes):
    for bt, cap in ((_row_block(t), WIDE_BLOCK), (_row_block(t), ROW_BLOCK), (ROW_BLOCK, ROW_BLOCK)):
        bn = _divisor_block(n, cap)
        if 2 * (bt * k * a_bytes + bn * k * 2 + bt * bn * o_bytes) <= MATMUL_VMEM_BUDGET:
            return bt, bn
    return ROW_BLOCK, _divisor_block(n, 2 * LANES)


def _matmul_nt(a, w, *, name, out_dtype):
    t, k = a.shape
    n = w.shape[0]
    assert w.shape[1] == k
    bt, bn = _matmul_blocks(t, k, n, a.dtype.itemsize, jnp.dtype(out_dtype).itemsize)

    def body(a_ref, w_ref, o_ref):
        o_ref[...] = _dot_nt(a_ref[...], w_ref[...]).astype(o_ref.dtype)

    return pl.pallas_call(
        body, name=name, grid=(t // bt, n // bn),
        in_specs=[pl.BlockSpec((bt, k), lambda i, j: (i, 0)), pl.BlockSpec((bn, k), lambda i, j: (j, 0))],
        out_specs=pl.BlockSpec((bt, bn), lambda i, j: (i, j)),
        out_shape=jax.ShapeDtypeStruct((t, n), out_dtype), compiler_params=_params(2),
    )(a, w)


def _matmul_rows(a, w, *, name, out_dtype=F32):
    t, k = a.shape
    n = w.shape[1]
    assert w.shape[0] == k
    bt = _row_block(t)

    def body(a_ref, w_ref, o_ref):
        o_ref[...] = jnp.dot(a_ref[...], w_ref[...], preferred_element_type=F32).astype(o_ref.dtype)

    return pl.pallas_call(
        body, name=name, grid=(t // bt,),
        in_specs=[pl.BlockSpec((bt, k), lambda i: (i, 0)), pl.BlockSpec((k, n), lambda i: (0, 0))],
        out_specs=pl.BlockSpec((bt, n), lambda i: (i, 0)),
        out_shape=jax.ShapeDtypeStruct((t, n), out_dtype), compiler_params=_params(1),
    )(a, w)


def _matmul_tn(a, b, *, name):
    t, m = a.shape
    n = b.shape[1]
    bt = TOKEN_STEP if t % TOKEN_STEP == 0 else ROW_BLOCK
    bm = _divisor_block(m, WIDE_BLOCK)
    bn = _divisor_block(n, WIDE_BLOCK)
    steps = t // bt

    def body(a_ref, b_ref, o_ref, acc):
        step = pl.program_id(2)

        @pl.when(step == 0)
        def _():
            acc[...] = jnp.zeros_like(acc)

        acc[...] += _dot_tn(a_ref[...], b_ref[...])

        @pl.when(step == steps - 1)
        def _():
            o_ref[...] = acc[...].astype(o_ref.dtype)

    return pl.pallas_call(
        body, name=name, grid=(m // bm, n // bn, steps),
        in_specs=[pl.BlockSpec((bt, bm), lambda i, j, s: (s, i)), pl.BlockSpec((bt, bn), lambda i, j, s: (s, j))],
        out_specs=pl.BlockSpec((bm, bn), lambda i, j, s: (i, j)),
        out_shape=jax.ShapeDtypeStruct((m, n), BF16), scratch_shapes=[pltpu.VMEM((bm, bn), F32)],
        compiler_params=_params(3),
    )(a, b)


def _rmsnorm_fwd(x, g, *, group, name, hosted=None):
    t, w = x.shape
    bt = ROW_BLOCK

    def body(x_ref, g_ref, o_ref):
        for s in range(0, w, group):
            xs = x_ref[:, s:s + group].astype(F32)
            r = lax.rsqrt(jnp.mean(xs * xs, axis=-1, keepdims=True) + EPS)
            o_ref[:, s:s + group] = (xs * r * g_ref[:, s:s + group]).astype(o_ref.dtype)

    grid = (t // bt,)
    h_specs, h_shapes, h_args, h_scratch = _hosted_parts(hosted)
    res = pl.pallas_call(
        _host(body, 2, 1, 0, hosted, grid), name=name, grid=grid,
        in_specs=[pl.BlockSpec((bt, w), lambda i: (i, 0)), pl.BlockSpec((1, w), lambda i: (0, 0))] + h_specs,
        out_specs=tuple([pl.BlockSpec((bt, w), lambda i: (i, 0))] + h_specs),
        out_shape=tuple([jax.ShapeDtypeStruct((t, w), BF16)] + h_shapes),
        scratch_shapes=h_scratch, compiler_params=_params(1),
    )(x, g, *h_args)
    return res if hosted else res[0]


def _norm_input_grad(terms, x, g, *, group, name, out_dtypes, resid=None, init=None, hosted=None, k_chunks=1):
    t, w = x.shape
    n_terms = len(terms)
    kc = [a.shape[1] // k_chunks for a, _, _ in terms]
    per_row = sum(c * a.dtype.itemsize for c, (a, _, _) in zip(kc, terms))
    per_row += w * sum(r.dtype.itemsize for r in (x, resid, init) if r is not None)
    per_row += w * sum(jnp.dtype(dt).itemsize for dt in out_dtypes)
    fixed = 2 * sum(w * c * 2 for c in kc)
    bt = next(b for b in (2 * ROW_BLOCK, ROW_BLOCK, ROW_BLOCK // 2, ROW_BLOCK // 4)
              if t % b == 0 and fixed + 2 * b * per_row + 5 * b * w * 4 <= MATMUL_VMEM_BUDGET)
    resid_at = 2 * n_terms + 2
    init_at = resid_at + (resid is not None)
    n_in = init_at + (init is not None)
    grid = (t // bt, k_chunks)

    def body(*refs):
        x_ref, g_ref = refs[2 * n_terms], refs[2 * n_terms + 1]
        dx_refs, dg_ref, dh_ref = refs[n_in:-2], refs[-2], refs[-1]
        chunk = pl.program_id(1)

        @pl.when((pl.program_id(0) == 0) & (chunk == 0))
        def _():
            dg_ref[...] = jnp.zeros_like(dg_ref)

        part = None
        for k in range(n_terms):
            if terms[k][2]:
                term = jnp.dot(refs[2 * k][...], refs[2 * k + 1][...], preferred_element_type=F32)
            else:
                term = _dot_nt(refs[2 * k][...], refs[2 * k + 1][...])
            part = term if part is None else part + term

        @pl.when(chunk == 0)
        def _():
            dh_ref[...] = part if init is None else refs[init_at][...] + part

        @pl.when(chunk > 0)
        def _():
            dh_ref[...] += part

        @pl.when(chunk == k_chunks - 1)
        def _():
            for s in range(0, w, group):
                xs = x_ref[:, s:s + group].astype(F32)
                dhs = dh_ref[:, s:s + group]
                r = lax.rsqrt(jnp.mean(xs * xs, axis=-1, keepdims=True) + EPS)
                xh = xs * r
                dg_ref[:, s:s + group] += jnp.sum(dhs * xh, axis=0, keepdims=True)
                dxh = dhs * g_ref[:, s:s + group]
                dx = r * (dxh - xh * jnp.mean(dxh * xh, axis=-1, keepdims=True))
                if resid is not None:
                    dx = refs[resid_at][:, s:s + group] + dx
                for dx_ref in dx_refs:
                    dx_ref[:, s:s + group] = dx.astype(dx_ref.dtype)

    row = pl.BlockSpec((bt, w), lambda i, k: (i, 0))
    vec = pl.BlockSpec((1, w), lambda i, k: (0, 0))
    in_specs, args = [], []
    for c, (a, wt, w_is_kn) in zip(kc, terms):
        assert wt.shape == ((a.shape[1], w) if w_is_kn else (w, a.shape[1]))
        w_spec = pl.BlockSpec((c, w), lambda i, k: (k, 0)) if w_is_kn else pl.BlockSpec((w, c), lambda i, k: (0, k))
        in_specs += [pl.BlockSpec((bt, c), lambda i, k: (i, k)), w_spec]
        args += [a, wt]
    extra = [r for r in (resid, init) if r is not None]
    in_specs += [row, vec] + [row] * len(extra)
    args += [x, g] + extra
    h_specs, h_shapes, h_args, h_scratch = _hosted_parts(hosted)
    return pl.pallas_call(
        _host(body, n_in, len(out_dtypes) + 1, 1, hosted, grid), name=name, grid=grid, in_specs=in_specs + h_specs,
        out_specs=tuple([row] * len(out_dtypes) + [vec] + h_specs),
        out_shape=tuple([jax.ShapeDtypeStruct((t, w), dt) for dt in out_dtypes] + [jax.ShapeDtypeStruct((1, w), F32)]
                        + h_shapes),
        scratch_shapes=[pltpu.VMEM((bt, w), F32)] + h_scratch, compiler_params=_params(2),
    )(*args, *h_args)


def _tile_plan(tile):
    is_q = tile < N_PAIRS
    is_dil = (tile % N_PAIRS) >= N_PAIRS // 2
    return is_q, is_dil, (0 if is_q else 2) + (1 if is_dil else 0)


def _proj_tile(kind, pair):
    return kind * N_PAIRS + pair


def _segment_ones():
    lane = np.arange(LANES)
    return jnp.asarray((lane[:, None] // HEAD_DIM) == (lane[None, :] // HEAD_DIM), BF16)


def _rope_tables(seq):
    inv_freq = jnp.power(jnp.float32(ROPE_THETA), -jnp.arange(ROPE_HALF, dtype=F32) * 2.0 / ROPE_DIM)
    ang = jnp.arange(seq).astype(F32)[:, None] * inv_freq[None, :]
    cos, sin = jnp.cos(ang), jnp.sin(ang)
    ones = jnp.ones((seq, HEAD_DIM - ROPE_DIM), F32)
    zeros = jnp.zeros((seq, HEAD_DIM - ROPE_DIM), F32)
    zh = jnp.zeros((seq, ROPE_HALF), F32)
    cos_t = jnp.concatenate([cos, cos, ones], axis=1)
    sin_a = jnp.concatenate([-sin, zh, zeros], axis=1)
    sin_b = jnp.concatenate([zh, sin, zeros], axis=1)
    return tuple(jnp.tile(tab, (1, 2)) for tab in (cos_t, sin_a, sin_b))


def _log_sigmoid(z):
    return jnp.minimum(z, 0.0) - jnp.log1p(jnp.exp(-jnp.abs(z)))


def _aug_placement():
    place = np.zeros((N_PAIRS, LANES, LANES), np.float32)
    for is_k in range(2):
        for pair in range(N_PAIRS // 2):
            for e in range(2):
                other = HEAD_DIM * (1 - e)
                ones_at = other + (AUG_C if is_k else AUG_ONE)
                c_at = other + (AUG_ONE if is_k else AUG_C)
                for n in range(3):
                    place[4 * is_k + pair, N_HEADS_FOX * n + 2 * pair + e, c_at + n] = -1.0 if is_k else 1.0
                    place[4 * is_k + pair, 3 * N_HEADS_FOX, ones_at + n] = 1.0
    return jnp.asarray(place, BF16)


def _qk_prep_fwd(proj, h1, w_fa_t, b_pad, gains, rope, seq):
    t = proj.shape[0]
    bt = ROW_BLOCK
    nsb = seq // bt
    seg = _segment_ones()
    rr = np.arange(bt)
    tri = jnp.asarray(rr[:, None] <= rr[None, :], BF16)

    def body(p_ref, h_ref, wfa_ref, b_ref, g_ref, cos_ref, sa_ref, sb_ref, seg_ref, tri_ref, place_ref,
             qk_ref, fa_ref, carry):
        @pl.when(pl.program_id(0) % nsb == 0)
        def _():
            carry[...] = jnp.zeros_like(carry)

        lane = lax.broadcasted_iota(jnp.int32, (bt, LANES), 1)
        fa = _dot_nt(h_ref[...], wfa_ref[...])
        fa_ref[...] = fa
        logf = jnp.where(lane < N_HEADS_FOX, _log_sigmoid(fa + b_ref[...]), 0.0)
        c_rows = _split_dot(logf.T[0:N_HEADS_FOX, :], tri_ref[...], 3) + carry[:, 0:1]
        carry[...] = jnp.broadcast_to(c_rows[:, bt - 1:bt], carry.shape)
        cblk = jnp.concatenate([c_rows, jnp.zeros((LANES - N_HEADS_FOX, bt), F32)], axis=0).T
        packed = jnp.where(lane == 3 * N_HEADS_FOX, 1.0, 0.0)
        rest = cblk * LOG2E
        for n in range(3):
            term = rest.astype(BF16).astype(F32)
            packed = packed + (pltpu.roll(term, N_HEADS_FOX * n, 1) if n else term)
            rest = rest - term
        packed = packed.astype(BF16)
        low = lane < HEAD_DIM

        for tile in range(2 * N_PAIRS):
            is_q, is_dil, grow = _tile_plan(tile)
            pair = tile % N_PAIRS
            src = _proj_tile(0 if is_q else 1, pair) * LANES
            xs = p_ref[:, src:src + LANES].astype(F32)
            r = lax.rsqrt(_split_dot(xs * xs, seg_ref[...], 2) * (1.0 / HEAD_DIM) + EPS)
            yv = xs * r * g_ref[grow:grow + 1, :]
            if is_dil:
                yv = (yv * cos_ref[...] + pltpu.roll(yv, LANES - ROPE_HALF, 1) * sa_ref[...]
                      + pltpu.roll(yv, ROPE_HALF, 1) * sb_ref[...])
                aug = jnp.zeros((bt, LANES), F32)
            else:
                aug = jnp.dot(packed, place_ref[(0 if is_q else N_PAIRS // 2) + pair], preferred_element_type=F32)
            if is_q:
                yv = yv * (HEAD_DIM ** -0.5 * LOG2E)
            dst = ((0 if is_q else N_HEADS) + 2 * pair) * LANES
            qk_ref[:, dst:dst + LANES] = jnp.where(low, yv, aug).astype(BF16)
            qk_ref[:, dst + LANES:dst + 2 * LANES] = jnp.where(low, aug, yv).astype(BF16)

    row128 = pl.BlockSpec((bt, LANES), lambda i: (i, 0))
    rope_spec = pl.BlockSpec((bt, LANES), lambda i: (i % nsb, 0))
    const = lambda shape: pl.BlockSpec(shape, lambda i: (0,) * len(shape))
    return pl.pallas_call(
        body, name="qk_prep_fwd", grid=(t // bt,),
        in_specs=[pl.BlockSpec((bt, 2 * D_MODEL), lambda i: (i, 0)), pl.BlockSpec((bt, D_MODEL), lambda i: (i, 0)),
                  const((LANES, D_MODEL)), const((1, LANES)), const((8, LANES)), rope_spec, rope_spec, rope_spec,
                  const((LANES, LANES)), const((bt, bt)), const((N_PAIRS, LANES, LANES))],
        out_specs=(pl.BlockSpec((bt, 2 * N_HEADS * LANES), lambda i: (i, 0)), row128),
        out_shape=(jax.ShapeDtypeStruct((t, 2 * N_HEADS * LANES), BF16), jax.ShapeDtypeStruct((t, LANES), F32)),
        scratch_shapes=[pltpu.VMEM((8, LANES), F32)], compiler_params=_params(1),
    )(proj, h1, w_fa_t, b_pad, gains, *rope, seg, tri, _aug_placement())


def _qk_prep_bwd(dq, dk, dqx, dkx, dv, proj, fa, b_pad, gains, rope, seq):
    t = proj.shape[0]
    bt = ROW_BLOCK
    nsb = seq // bt
    nblk = t // bt
    seg = _segment_ones()
    rr = np.arange(bt)
    triu = jnp.asarray(rr[:, None] >= rr[None, :], BF16)

    def body(dq_ref, dk_ref, dqx_ref, dkx_ref, dv_ref, p_ref, fa_ref, b_ref, g_ref, cos_ref, sa_ref, sb_ref, seg_ref,
             triu_ref, dp_ref, dg_ref, db_ref, carry):
        step = pl.program_id(0)

        @pl.when(step == 0)
        def _():
            dg_ref[...] = jnp.zeros_like(dg_ref)
            db_ref[...] = jnp.zeros_like(db_ref)

        @pl.when(step % nsb == 0)
        def _():
            carry[...] = jnp.zeros_like(carry)

        for tile in range(2 * N_PAIRS):
            is_q, is_dil, grow = _tile_plan(tile)
            first = _proj_tile(0 if is_q else 1, tile % N_PAIRS) * LANES
            cols = slice(first, first + LANES)
            src = dq_ref if is_q else dk_ref
            half = slice((tile % N_PAIRS) * LANES, (tile % N_PAIRS + 1) * LANES)
            dy = src[:, half]
            dy = dy * (HEAD_DIM ** -0.5 if is_q else LN2)
            if is_dil:
                dy = (dy * cos_ref[...] + pltpu.roll(dy * sa_ref[...], ROPE_HALF, 1)
                      + pltpu.roll(dy * sb_ref[...], LANES - ROPE_HALF, 1))
            xs = p_ref[:, cols].astype(F32)
            r = lax.rsqrt(_split_dot(xs * xs, seg_ref[...], 2) * (1.0 / HEAD_DIM) + EPS)
            xh = xs * r
            dg_ref[tile:tile + 1, :] += jnp.sum(dy * xh, axis=0, keepdims=True)
            dxh = dy * g_ref[grow:grow + 1, :]
            seg_mean = _split_dot(dxh * xh, seg_ref[...], 2) * (1.0 / HEAD_DIM)
            dp_ref[:, cols] = (r * (dxh - xh * seg_mean)).astype(BF16)

        lane = lax.broadcasted_iota(jnp.int32, (bt, LANES), 1)
        dc = jnp.zeros((bt, LANES), F32)
        for h in range(N_HEADS_FOX):
            other = (h // 2) * LANES + HEAD_DIM * (1 - h % 2)
            row_sum = dqx_ref[:, other + AUG_C:other + AUG_C + 1]
            col_sum = dkx_ref[:, other + AUG_ONE:other + AUG_ONE + 1]
            dc = jnp.where(lane == h, row_sum - col_sum, dc)
        d_rows = _split_dot(dc.T[0:N_HEADS_FOX, :], triu_ref[...], 3) + carry[:, 0:1]
        carry[...] = jnp.broadcast_to(d_rows[:, 0:1], carry.shape)
        dlogf = jnp.concatenate([d_rows, jnp.zeros((LANES - N_HEADS_FOX, bt), F32)], axis=0).T
        z = fa_ref[...] + b_ref[...]
        dfa = dlogf * (1.0 / (1.0 + jnp.exp(z)))
        db_ref[0:1, :] += jnp.sum(dfa, axis=0, keepdims=True)
        for group in range(2):
            first = _proj_tile(2, group * (N_PAIRS // 2)) * LANES
            dp_ref[:, first:first + W_GROUP] = dv_ref[:, group * W_GROUP:(group + 1) * W_GROUP]
        dp_ref[:, MAIN_COLS:PROJ_COLS] = dfa.astype(BF16)

    rev = lambda i: nblk - 1 - i
    row = lambda w: pl.BlockSpec((bt, w), lambda i: (rev(i), 0))
    rope_spec = pl.BlockSpec((bt, LANES), lambda i: (rev(i) % nsb, 0))
    const = lambda shape: pl.BlockSpec(shape, lambda i: (0, 0))
    return pl.pallas_call(
        body, name="qk_prep_bwd", grid=(nblk,),
        in_specs=[row(D_MODEL), row(D_MODEL), row(W_GROUP), row(W_GROUP), row(D_MODEL), row(2 * D_MODEL), row(LANES),
                  const((1, LANES)), const((8, LANES)), rope_spec, rope_spec, rope_spec, const((LANES, LANES)),
                  const((bt, bt))],
        out_specs=(row(PROJ_COLS), const((2 * N_PAIRS, LANES)), const((8, LANES))),
        out_shape=(jax.ShapeDtypeStruct((t, PROJ_COLS), BF16),
                   jax.ShapeDtypeStruct((2 * N_PAIRS, LANES), F32), jax.ShapeDtypeStruct((8, LANES), F32)),
        scratch_shapes=[pltpu.VMEM((8, LANES), F32)], compiler_params=_params(1),
    )(dq, dk, dqx, dkx, dv, proj, fa, b_pad, gains, *rope, seg, triu)


def _bias_tables(seq, keys_first):
    nb = seq // ATT_BLOCK
    idx = np.arange(ATT_BLOCK)
    q_idx, k_idx = (idx[None, None, :], idx[None, :, None]) if keys_first else (idx[None, :, None], idx[None, None, :])
    dist = np.arange(nb)[:, None, None] * ATT_BLOCK + q_idx - k_idx
    causal = dist >= 0
    count = np.zeros(dist.shape, np.int32)
    for window, dilation in DILATION_PAIRS:
        count = count + (causal & (dist % dilation == 0) & (dist <= window))
    fox = np.where(causal, 0.0, NEG)
    dil = np.where(count == 3, math.log2(3.0), np.where(count == 2, 1.0, np.where(count == 1, 0.0, NEG)))
    return jnp.asarray(np.stack([fox, dil], axis=0), F32)


def _attn_specs(seq):
    nb = seq // ATT_BLOCK
    col = pl.BlockSpec((seq, LANES), lambda b, j: (b, j))
    heads = lambda off: pl.BlockSpec((seq, 2 * LANES), lambda b, j: (b, off + j))
    v_spec = pl.BlockSpec((seq, LANES), lambda b, j: (b, _proj_tile(2, j)))
    table_spec = pl.BlockSpec((1, nb, ATT_BLOCK, ATT_BLOCK), lambda b, j: (j // (N_PAIRS // 2), 0, 0, 0))
    return col, heads, v_spec, table_spec


def _head_lanes(e, shape, axis):
    pos = lax.broadcasted_iota(jnp.int32, shape, axis)
    return pos < HEAD_DIM if e == 0 else pos >= HEAD_DIM


def _attn_fwd(qk, proj, tables, seq, hosted=None):
    t = qk.shape[0]
    nb = seq // ATT_BLOCK
    blk = ATT_BLOCK

    def body(q_ref, k_ref, v_ref, tab_ref, o_ref, lse_ref):
        mine = [_head_lanes(e, (seq, LANES), 1) for e in range(2)]
        lane = lax.broadcasted_iota(jnp.int32, (seq, LANES), 1)
        v_aug = [jnp.where(mine[e], v_ref[...], (lane == HEAD_DIM * (1 - e)).astype(BF16)) for e in range(2)]
        def scores(i, e):
            heads_e = slice(e * LANES, (e + 1) * LANES)
            s = _dot_nt(q_ref[i * blk:(i + 1) * blk, heads_e], k_ref[0:(i + 1) * blk, heads_e])
            s = jnp.concatenate([s[:, jj * blk:(jj + 1) * blk] + tab_ref[0, i - jj] for jj in range(i + 1)], axis=1)
            return s, jnp.max(s, axis=1, keepdims=True)

        chains = [(i, e) for i in reversed(range(nb)) for e in range(2)]
        ahead = 2
        pending = [scores(*chain) for chain in chains[:ahead]]
        done = {}
        for n, (i, e) in enumerate(chains):
            s, m = pending.pop(0)
            if n + ahead < len(chains):
                pending.append(scores(*chains[n + ahead]))
            acc = jnp.dot(jnp.exp2(s - m).astype(BF16), v_aug[e][0:(i + 1) * blk], preferred_element_type=F32)
            ones_at = HEAD_DIM * (1 - e)
            l = acc[:, ones_at:ones_at + 1]
            done[e] = (acc / l, m + jnp.log2(l))
            if e == 1:
                rows = slice(i * blk, (i + 1) * blk)
                o_ref[rows, :] = jnp.where(mine[0][rows], done[0][0], done[1][0]).astype(o_ref.dtype)
                lse_ref[rows, :] = jnp.where(mine[0][rows], done[0][1], done[1][1])

    col, heads, v_spec, table_spec = _attn_specs(seq)
    grid = (t // seq, N_PAIRS)
    h_specs, h_shapes, h_args, h_scratch = _hosted_parts(hosted)
    return pl.pallas_call(
        _host(body, 4, 2, 0, hosted, grid), name="attn_fwd", grid=grid,
        in_specs=[heads(0), heads(N_PAIRS), v_spec, table_spec] + h_specs,
        out_specs=tuple([col, col] + h_specs),
        out_shape=tuple([jax.ShapeDtypeStruct((t, D_MODEL), BF16), jax.ShapeDtypeStruct((t, D_MODEL), F32)] + h_shapes),
        scratch_shapes=h_scratch, compiler_params=_params(2),
    )(qk, qk, proj, tables, *h_args)


def _attn_bwd(qk, proj, tables, o, lse, do, seq, hosted=None):
    t = qk.shape[0]
    nb = seq // ATT_BLOCK
    blk = ATT_BLOCK
    group = math.gcd(nb, ATT_GROUP)

    def body(q_ref, k_ref, v_ref, tab_ref, o_ref, lse_ref, do_ref,
             dq_ref, dk_ref, dv_ref, dqx_ref, dkx_ref, dk_acc, dv_acc):
        mine = [_head_lanes(e, (blk, LANES), 1) for e in range(2)]
        top = _head_lanes(0, (LANES, blk), 0)
        head_rows = lax.broadcasted_iota(jnp.int32, (8, LANES), 0)
        head_of_lane = lax.broadcasted_iota(jnp.int32, (8, LANES), 1) // HEAD_DIM
        head_sel = (head_rows == head_of_lane).astype(BF16)
        dk_acc[...] = jnp.zeros_like(dk_acc)
        dv_acc[...] = jnp.zeros_like(dv_acc)

        def block_rows(i):
            return pl.ds(pl.multiple_of(i * blk, blk), blk)

        def q_group(g, _):
            base = g * group
            qs, doe, delta, lse_e = [], [], [], []
            for b in range(group):
                rows = block_rows(base + b)
                qs.append([q_ref[rows, e * LANES:(e + 1) * LANES] for e in range(2)])
                do_blk = do_ref[rows, :]
                doe.append([jnp.where(mine[e], do_blk, jnp.zeros_like(do_blk)) for e in range(2)])
                delta_t = _split_dot_nt(head_sel, do_blk.astype(F32) * o_ref[rows, :].astype(F32), 3)
                lse_t = _split_dot_nt(head_sel, lse_ref[rows, :], 3) * (1.0 / HEAD_DIM)
                delta.append([delta_t[e:e + 1, :] for e in range(2)])
                lse_e.append([lse_t[e:e + 1, :] for e in range(2)])

            def key_block(dq_t, jj, members):
                krows = block_rows(jj)
                v = v_ref[krows, :]
                dq_t = [list(d) for d in dq_t]
                lo, hi = slice(0, blk // 2), slice(blk // 2, blk)
                dv_part = [None, None]
                add = lambda acc, part: part if acc is None else acc + part

                def probs(k_sub, v_sub, keys, queries, b, e, dist):
                    q_sub, do_sub = qs[b][e][queries], doe[b][e][queries]
                    p_t = jnp.exp2(_dot_nt(k_sub, q_sub) + tab_ref[0, dist, keys, queries] - lse_e[b][e][:, queries])
                    ds_t = (p_t * (_dot_nt(v_sub, do_sub) - delta[b][e][:, queries])).astype(BF16)
                    return p_t.astype(BF16), ds_t, q_sub, do_sub, k_sub

                def outputs(tile):
                    p_t, ds_t, q_sub, do_sub, k_sub = tile
                    return (jnp.dot(p_t, do_sub, preferred_element_type=F32),
                            jnp.dot(ds_t, q_sub, preferred_element_type=F32), _dot_tn(k_sub, ds_t))

                for e in range(2):
                    k_e = k_ref[krows, e * LANES:(e + 1) * LANES]
                    dk_part = [None, None]
                    tiles = []
                    for b, dist in members:
                        if isinstance(dist, int) and dist == 0:
                            tiles.append((b, probs(k_e[lo], v[lo], lo, slice(0, blk), b, e, dist),
                                          probs(k_e[hi], v[hi], hi, hi, b, e, dist)))
                        else:
                            tiles.append((b, probs(k_e, v, slice(0, blk), slice(0, blk), b, e, dist), None))
                    for b, first, second in tiles:
                        if second is not None:
                            dv_a, dk_a, dq_a = outputs(first)
                            dv_b, dk_b, dq_b = outputs(second)
                            halves = ((dv_a, dk_a), (dv_b, dk_b))
                            dq = jnp.concatenate([dq_a[:, lo], dq_a[:, hi] + dq_b], axis=1)
                        else:
                            dv_f, dk_f, dq = outputs(first)
                            halves = ((dv_f[lo], dk_f[lo]), (dv_f[hi], dk_f[hi]))
                        for n, (dv_h, dk_h) in enumerate(halves):
                            dv_part[n] = add(dv_part[n], dv_h)
                            dk_part[n] = add(dk_part[n], dk_h)
                        dq_t[b][e] = dq_t[b][e] + dq
                    dk_acc[e, krows, :] += jnp.concatenate(dk_part, axis=0)
                dv_acc[krows, :] += jnp.concatenate(dv_part, axis=0)
                return tuple(tuple(d) for d in dq_t)

            zacc = jnp.zeros((LANES, blk), F32)
            dq_t = tuple((zacc, zacc) for _ in range(group))
            dq_t = lax.fori_loop(
                0, base, lambda jj, st: key_block(st, jj, [(b, base + b - jj) for b in range(group)]), dq_t)
            for a in range(group):
                dq_t = key_block(dq_t, base + a, [(b, b - a) for b in range(a, group)])
            for b in range(group):
                rows = block_rows(base + b)
                dq_ref[rows, :] = jnp.where(top, dq_t[b][0], dq_t[b][1]).T
                dqx_ref[rows, :] = jnp.where(top, dq_t[b][1], dq_t[b][0]).T
            return 0

        lax.fori_loop(0, nb // group, q_group, 0)
        lo = _head_lanes(0, (seq, LANES), 1)
        dk_ref[...] = jnp.where(lo, dk_acc[0], dk_acc[1])
        dkx_ref[...] = jnp.where(lo, dk_acc[1], dk_acc[0])
        dv_ref[...] = dv_acc[...].astype(dv_ref.dtype)

    col, heads, v_spec, table_spec = _attn_specs(seq)
    grid = (t // seq, N_PAIRS)
    h_specs, h_shapes, h_args, h_scratch = _hosted_parts(hosted)
    f32_out = jax.ShapeDtypeStruct((t, D_MODEL), F32)
    return pl.pallas_call(
        _host(body, 7, 5, 2, hosted, grid), name="attn_bwd", grid=grid,
        in_specs=[heads(0), heads(N_PAIRS), v_spec, table_spec, col, col, col] + h_specs,
        out_specs=tuple([col] * 5 + h_specs),
        out_shape=tuple([f32_out, f32_out, jax.ShapeDtypeStruct((t, D_MODEL), BF16), f32_out, f32_out] + h_shapes),
        scratch_shapes=[pltpu.VMEM((2, seq, LANES), F32), pltpu.VMEM((seq, LANES), F32)] + h_scratch,
        compiler_params=_params(2),
    )(qk, qk, proj, tables, o, lse, do, *h_args)


def _row_block(t):
    return 2 * ROW_BLOCK if t % (2 * ROW_BLOCK) == 0 else ROW_BLOCK


def _out_proj_ffn_norm(o, g_out, w_out, x, g_ffn):
    t = o.shape[0]
    bt = _row_block(t)

    def body(o_ref, go_ref, w_ref, x_ref, gf_ref, on_ref, x2_ref, h2_ref):
        for s in range(0, D_MODEL, W_GROUP):
            os_ = o_ref[:, s:s + W_GROUP].astype(F32)
            r = lax.rsqrt(jnp.mean(os_ * os_, axis=-1, keepdims=True) + EPS)
            on_ref[:, s:s + W_GROUP] = (os_ * r * go_ref[:, s:s + W_GROUP]).astype(BF16)
        x2 = x_ref[...] + jnp.dot(on_ref[...], w_ref[...], preferred_element_type=F32)
        x2_ref[...] = x2
        r2 = lax.rsqrt(jnp.mean(x2 * x2, axis=-1, keepdims=True) + EPS)
        h2_ref[...] = (x2 * r2 * gf_ref[...]).astype(BF16)

    row = pl.BlockSpec((bt, D_MODEL), lambda i: (i, 0))
    vec = pl.BlockSpec((1, D_MODEL), lambda i: (0, 0))
    return pl.pallas_call(
        body, name="out_proj", grid=(t // bt,),
        in_specs=[row, vec, pl.BlockSpec((D_MODEL, D_MODEL), lambda i: (0, 0)), row, vec],
        out_specs=(row, row, row),
        out_shape=(jax.ShapeDtypeStruct((t, D_MODEL), BF16), jax.ShapeDtypeStruct((t, D_MODEL), F32),
                   jax.ShapeDtypeStruct((t, D_MODEL), BF16)),
        compiler_params=_params(1),
    )(o, g_out, w_out, x, g_ffn)


def _ffn_gate_up(h2, w_gate_t, w_up_t):
    t = h2.shape[0]
    bt = _row_block(t)
    bn = _divisor_block(D_FF, WIDE_BLOCK)

    def body(h_ref, wg_ref, wu_ref, a_ref, u_ref, f_ref):
        a = _dot_nt(h_ref[...], wg_ref[...])
        u = _dot_nt(h_ref[...], wu_ref[...])
        a_ref[...] = a.astype(BF16)
        u_ref[...] = u.astype(BF16)
        f_ref[...] = (a * jax.nn.sigmoid(a) * u).astype(BF16)

    blk = pl.BlockSpec((bt, bn), lambda j, i: (i, j))
    w_blk = pl.BlockSpec((bn, D_MODEL), lambda j, i: (j, 0))
    shape = jax.ShapeDtypeStruct((t, D_FF), BF16)
    return pl.pallas_call(
        body, name="ffn_gate_up", grid=(D_FF // bn, t // bt),
        in_specs=[pl.BlockSpec((bt, D_MODEL), lambda j, i: (i, 0)), w_blk, w_blk],
        out_specs=(blk, blk, blk), out_shape=(shape, shape, shape), compiler_params=_params(2),
    )(h2, w_gate_t, w_up_t)


def _ffn_down_grad(dy16, w_down, a, u):
    t = a.shape[0]
    bt = _row_block(t)
    bn = _divisor_block(D_FF, WIDE_BLOCK)

    def body(dy_ref, w_ref, a_ref, u_ref, da_ref, du_ref):
        df = _dot_nt(dy_ref[...], w_ref[...])
        av = a_ref[...].astype(F32)
        sg = jax.nn.sigmoid(av)
        da_ref[...] = (df * u_ref[...].astype(F32) * sg * (1.0 + av * (1.0 - sg))).astype(BF16)
        du_ref[...] = (df * av * sg).astype(BF16)

    blk = pl.BlockSpec((bt, bn), lambda j, i: (i, j))
    shape = jax.ShapeDtypeStruct((t, D_FF), BF16)
    return pl.pallas_call(
        body, name="d_ffn_down", grid=(D_FF // bn, t // bt),
        in_specs=[pl.BlockSpec((bt, D_MODEL), lambda j, i: (i, 0)), pl.BlockSpec((bn, D_MODEL), lambda j, i: (j, 0)),
                  blk, blk],
        out_specs=(blk, blk), out_shape=(shape, shape), compiler_params=_params(2),
    )(dy16, w_down, a, u)


def _ffn_down_loss(f, w_down, x2, target):
    t, w = x2.shape
    bt = _row_block(t)

    def body(f_ref, w_ref, x_ref, t_ref, dy16_ref, loss_ref):
        @pl.when(pl.program_id(0) == 0)
        def _():
            loss_ref[...] = jnp.zeros_like(loss_ref)

        err = (x_ref[...] + jnp.dot(f_ref[...], w_ref[...], preferred_element_type=F32)) - t_ref[...]
        dy16_ref[...] = (err * (1.0 / w)).astype(BF16)
        loss_ref[...] += 0.5 * jnp.sum(jnp.mean(err * err, axis=-1, keepdims=True), axis=0, keepdims=True)

    row = pl.BlockSpec((bt, w), lambda i: (i, 0))
    return pl.pallas_call(
        body, name="ffn_down_loss", grid=(t // bt,),
        in_specs=[pl.BlockSpec((bt, D_FF), lambda i: (i, 0)), pl.BlockSpec((D_FF, w), lambda i: (0, 0)), row, row],
        out_specs=(row, pl.BlockSpec((8, LANES), lambda i: (0, 0))),
        out_shape=(jax.ShapeDtypeStruct((t, w), BF16), jax.ShapeDtypeStruct((8, LANES), F32)),
        compiler_params=_params(1),
    )(f, w_down, x2, target)


def _adamw(parts, w, m, v, *, name):
    _, rows, cols = w.shape
    br = rows if rows <= 512 else 256
    assert rows % br == 0

    def body(p_ref, w_ref, m_ref, v_ref, g_ref, d_ref, nm_ref, nv_ref):
        g = p_ref[0].astype(F32)
        for r in range(1, N_DEV):
            g = g + p_ref[r].astype(F32)
        m2 = ADAM_B1 * m_ref[0] + (1.0 - ADAM_B1) * g
        v2 = ADAM_B2 * v_ref[0] + (1.0 - ADAM_B2) * jnp.square(g)
        m_hat = m2 / (1.0 - ADAM_B1 ** ADAM_STEP)
        v_hat = v2 / (1.0 - ADAM_B2 ** ADAM_STEP)
        g_ref[0] = g
        d_ref[0] = -ADAM_LR * (m_hat / (jnp.sqrt(v_hat) + ADAM_EPS) + ADAM_WD * w_ref[0])
        nm_ref[0] = m2
        nv_ref[0] = v2

    blk = pl.BlockSpec((1, br, cols), lambda i: (0, i, 0))
    shape = jax.ShapeDtypeStruct((1, rows, cols), F32)
    return pl.pallas_call(
        body, name=name, grid=(rows // br,),
        in_specs=[pl.BlockSpec((N_DEV, br, cols), lambda i: (0, i, 0)), blk, blk, blk],
        out_specs=(blk, blk, blk, blk), out_shape=(shape, shape, shape, shape), compiler_params=_params(1),
    )(parts, w, m, v)


_QA, _KA, _VA, _FA, _QD, _KD, _VD = (0, 512), (512, 1024), (1024, 1536), (1536, 1544), (1544, 2056), (2056, 2568), (2568, 3080)
_MAIN_ORDER = (_QA, _QD, _KA, _KD, _VA, _VD)
MAIN_COLS = 3 * D_MODEL
PROJ_COLS = MAIN_COLS + LANES
COL_SHARDED = ("w_in", "w_gate", "w_up")


def _swap(w):
    return jnp.transpose(w, (0, 2, 1))


def _w_in_to_kernel(w_t):
    main = jnp.concatenate([w_t[a:b] for a, b in _MAIN_ORDER], axis=0)
    forget = jnp.pad(w_t[_FA[0]:_FA[1]], ((0, LANES - N_HEADS_FOX), (0, 0)))
    return main, forget


def _w_in_from_kernel(g_t):
    pos = {span: i * W_GROUP for i, span in enumerate(_MAIN_ORDER)}
    parts = []
    for span in (_QA, _KA, _VA, _FA, _QD, _KD, _VD):
        if span == _FA:
            parts.append(g_t[MAIN_COLS:MAIN_COLS + N_HEADS_FOX])
        else:
            parts.append(g_t[pos[span]:pos[span] + W_GROUP])
    return jnp.concatenate(parts, axis=0)


def _pack_small(vals):
    rows = []
    for name, _, n_rows in SMALL_LAYOUT:
        flat = vals[name].reshape(-1).astype(F32)
        rows.append(jnp.pad(flat, (0, n_rows * LANES - flat.shape[0])).reshape(n_rows, LANES))
    packed = jnp.concatenate(rows, axis=0)
    return jnp.pad(packed, ((0, SMALL_ROWS - packed.shape[0]), (0, 0)))


def _unpack_small(packed, like):
    out = {}
    for name, row, n_rows in SMALL_LAYOUT:
        n = like[name].size
        out[name] = packed[row:row + n_rows].reshape(-1)[:n].reshape(like[name].shape)
    return out


def _device_step(x, target, small, shards):
    bsz, seq, _ = x.shape
    t = bsz * seq
    xf = x.reshape(t, D_MODEL)
    tf = target.reshape(t, D_MODEL)
    row = lambda v: v.reshape(1, -1)
    g_out = jnp.concatenate([small["g_out_fox"], small["g_out_dil"]]).reshape(1, D_MODEL)
    gains = jnp.concatenate(
        [jnp.tile(small[n].reshape(1, HEAD_DIM), (1, 2)) for n in ("g_q_fox", "g_q_dil", "g_k_fox", "g_k_dil")]
        + [jnp.zeros((4, LANES), F32)], axis=0)
    b_pad = jnp.pad(small["b_forget"].reshape(1, N_HEADS_FOX), ((0, 0), (0, LANES - N_HEADS_FOX)))
    rope = _rope_tables(seq)
    tables_qk = _bias_tables(seq, keys_first=False)
    tables_kq = _bias_tables(seq, keys_first=True)

    h1, g_in = _rmsnorm_fwd(xf, row(small["g_mix"]), group=D_MODEL, name="norm_mix",
                            hosted=_ChipGather([(shards["w_in"], False)]))
    w_main_t, w_fa_t = _w_in_to_kernel(g_in.reshape(IN_COLS, D_MODEL))
    w_in_all_t = jnp.concatenate([w_main_t, w_fa_t], axis=0)
    proj = _matmul_nt(h1, w_main_t, name="in_proj", out_dtype=BF16)
    qk, fa = _qk_prep_fwd(proj, h1, w_fa_t, b_pad, gains, rope, seq)
    late = _Exchange([(shards[n], False) for n in ("w_out", "w_gate", "w_up", "w_down")])
    o, lse, g_out_w, g_gate, g_up, g_down = _attn_fwd(qk, proj, tables_qk, seq, hosted=late)
    w_out = g_out_w.reshape(D_MODEL, D_MODEL)
    w_gate_t = g_gate.reshape(D_FF, D_MODEL)
    w_up_t = g_up.reshape(D_FF, D_MODEL)
    w_down = g_down.reshape(D_FF, D_MODEL)
    on, x2, h2 = _out_proj_ffn_norm(o, g_out, w_out, xf, row(small["g_ffn"]))
    a, u, f = _ffn_gate_up(h2, w_gate_t, w_up_t)
    dy16, loss_tile = _ffn_down_loss(f, w_down, x2, tf)

    da, du = _ffn_down_grad(dy16, w_down, a, u)
    gw_down = _matmul_tn(f, dy16, name="gw_down")
    gw_gate_t = _matmul_tn(da, h2, name="gw_gate")
    gw_up_t = _matmul_tn(du, h2, name="gw_up")
    dh2_gate = _matmul_rows(da, w_gate_t, name="d_ffn_gate", out_dtype=BF16)
    dx2_16, dg_ffn = _norm_input_grad([(du, w_up_t, True)], x2, row(small["g_ffn"]), group=D_MODEL,
                                      name="d_ffn_up", out_dtypes=(BF16,), resid=dy16, init=dh2_gate)
    gw_out = _matmul_tn(on, dx2_16, name="gw_out")
    do, dg_out = _norm_input_grad([(dx2_16, w_out, False)], o, g_out, group=W_GROUP, name="d_out_proj",
                                  out_dtypes=(BF16,))

    shard_rows = lambda g: g.reshape(N_DEV, g.shape[0] // N_DEV, g.shape[1])
    ffn_grads = _Exchange([(shard_rows(g), True) for g in (gw_out, gw_gate_t, gw_up_t, gw_down)])
    dq, dk, dv, dqx, dkx, p_out, p_gate, p_up, p_down = _attn_bwd(qk, proj, tables_kq, o, lse, do, seq, hosted=ffn_grads)
    dproj, dgains, db = _qk_prep_bwd(dq, dk, dqx, dkx, dv, proj, fa, b_pad, gains, rope, seq)
    gw_in_t = _matmul_tn(dproj, h1, name="gw_in")
    in_grad = _Exchange([(shard_rows(_w_in_from_kernel(gw_in_t)), True)])
    dx, dg_mix, p_in = _norm_input_grad([(dproj, w_in_all_t, True)], xf, row(small["g_mix"]), group=D_MODEL,
                                        name="d_in_proj", out_dtypes=(F32,), resid=dx2_16, hosted=in_grad)

    fold = lambda rows: jnp.sum(rows[:, :HEAD_DIM] + rows[:, HEAD_DIM:], axis=0)
    half = N_PAIRS // 2
    gsmall = {
        "g_mix": dg_mix, "g_ffn": dg_ffn, "g_out_fox": dg_out[0, :W_GROUP], "g_out_dil": dg_out[0, W_GROUP:],
        "g_q_fox": fold(dgains[0:half]), "g_q_dil": fold(dgains[half:N_PAIRS]),
        "g_k_fox": fold(dgains[N_PAIRS:N_PAIRS + half]), "g_k_dil": fold(dgains[N_PAIRS + half:]),
        "b_forget": db[0, :N_HEADS_FOX],
    }
    packed = _pack_small(gsmall).at[LOSS_ROW].set(loss_tile[0])
    (p_small,) = _exchange("small_exchange", [(packed, False)])
    parts = {"w_in": p_in, "w_out": p_out, "w_gate": p_gate, "w_up": p_up, "w_down": p_down}
    return dx.reshape(x.shape), parts, p_small


def kernel(x, g_mix, w_in, b_forget, g_q_fox, g_k_fox, g_q_dil, g_k_dil, g_out_fox, g_out_dil, w_out, g_ffn, w_gate, w_up, w_down, loss_target, m_g_mix, m_w_in, m_b_forget, m_g_q_fox, m_g_k_fox, m_g_q_dil, m_g_k_dil, m_g_out_fox, m_g_out_dil, m_w_out, m_g_ffn, m_w_gate, m_w_up, m_w_down, v_g_mix, v_w_in, v_b_forget, v_g_q_fox, v_g_k_fox, v_g_q_dil, v_g_k_dil, v_g_out_fox, v_g_out_dil, v_w_out, v_g_ffn, v_w_gate, v_w_up, v_w_down):
    args = dict(locals())
    small_names = [name for name, _, _ in SMALL_LAYOUT]
    big_names = ["w_in", "w_out", "w_gate", "w_up", "w_down"]
    small = {n: args[n][0] for n in small_names}

    as_rows = lambda n, w: _swap(w) if n in COL_SHARDED else w
    shards = {n: as_rows(n, args[n])[0].astype(BF16) for n in big_names}
    grad_x, parts, p_small = _device_step(x, loss_target, small, shards)

    grads, deltas, new_m, new_v = {}, {}, {}, {}
    for n in big_names:
        res = _adamw(parts[n], as_rows(n, args[n]), as_rows(n, args["m_" + n]), as_rows(n, args["v_" + n]),
                     name="adamw_" + n)
        grads[n], deltas[n], new_m[n], new_v[n] = [as_rows(n, r) for r in res]
    res = _adamw(p_small, _pack_small(small)[None], _pack_small({n: args["m_" + n][0] for n in small_names})[None],
                 _pack_small({n: args["v_" + n][0] for n in small_names})[None], name="adamw_small")
    loss = res[0][0, LOSS_ROW, 0]
    for dst, packed_res in zip((grads, deltas, new_m, new_v), res):
        for n, val in _unpack_small(packed_res[0], small).items():
            dst[n] = val[None]

    order = ["g_mix", "w_in", "b_forget", "g_q_fox", "g_k_fox", "g_q_dil", "g_k_dil", "g_out_fox", "g_out_dil",
             "w_out", "g_ffn", "w_gate", "w_up", "w_down"]
    return (loss, grad_x, *[grads[n] for n in order], *[deltas[n] for n in order],
            *[new_m[n] for n in order], *[new_v[n] for n in order])
```

```python
import functools
import math

import jax
import jax.numpy as jnp
import numpy as np
from jax import lax
from jax.experimental import pallas as pl
from jax.experimental.pallas import tpu as pltpu

F32 = jnp.float32
BF16 = jnp.bfloat16

D_MODEL = 1024
HEAD_DIM = 64
LANES = 128
N_PAIRS = D_MODEL // LANES
N_HEADS = 2 * N_PAIRS
N_HEADS_FOX = 8
W_GROUP = 512
D_FF = 2816
IN_COLS = 3080
DILATION_PAIRS = ((128, 1), (512, 4), (2048, 16))
ROPE_THETA = 500000.0
ROPE_DIM = 16
ROPE_HALF = ROPE_DIM // 2
EPS = 1e-6
NEG = -1e30
LOG2E = 1.4426950408889634
LN2 = 0.6931471805599453
AUG_ONE = 0
AUG_C = 3
N_DEV = 8

ADAM_LR = 0.001
ADAM_B1 = 0.9
ADAM_B2 = 0.999
ADAM_EPS = 1e-08
ADAM_WD = 0.01
ADAM_STEP = 10

ROW_BLOCK = 512
TOKEN_STEP = 2048
WIDE_BLOCK = D_FF // 2
ATT_BLOCK = 512
ATT_GROUP = 4
VMEM_LIMIT = 56 * 1024 * 1024
MATMUL_VMEM_BUDGET = 44 * 1024 * 1024

SMALL_ROWS = 32
SMALL_LAYOUT = (("g_mix", 0, 8), ("g_ffn", 8, 8), ("g_out_fox", 16, 4), ("g_out_dil", 20, 4),
                ("g_q_fox", 24, 1), ("g_k_fox", 25, 1), ("g_q_dil", 26, 1), ("g_k_dil", 27, 1),
                ("b_forget", 28, 1))
LOSS_ROW = 29


def _params(n_grid):
    return pltpu.CompilerParams(dimension_semantics=("arbitrary",) * n_grid, vmem_limit_bytes=VMEM_LIMIT)


def _divisor_block(n, cap):
    best = None
    for b in range(LANES, min(n, cap) + 1, LANES):
        if n % b == 0:
            best = b
    assert best is not None, n
    return best


def _split_dot(a, b_exact, terms):
    acc = None
    rest = a
    for _ in range(terms):
        hi = rest.astype(BF16)
        part = jnp.dot(hi, b_exact, preferred_element_type=F32)
        acc = part if acc is None else acc + part
        rest = rest - hi.astype(F32)
    return acc


def _split_dot_nt(a_exact, b, terms):
    acc = None
    rest = b
    for _ in range(terms):
        hi = rest.astype(BF16)
        part = _dot_nt(a_exact, hi)
        acc = part if acc is None else acc + part
        rest = rest - hi.astype(F32)
    return acc


def _dot_nt(a, b):
    return lax.dot_general(a, b, (((1,), (1,)), ((), ())), preferred_element_type=F32)


def _dot_tn(a, b):
    return lax.dot_general(a, b, (((0,), (0,)), ((), ())), preferred_element_type=F32)


class _Exchange:
    def __init__(self, items):
        self.items = items
        self.n = len(items)
        self.arrays = [a for a, _ in items]
        self.out_shape = [jax.ShapeDtypeStruct((N_DEV,) + tuple(a.shape[1:] if sc else a.shape), a.dtype)
                          for a, sc in items]
        self.specs = [pl.BlockSpec(memory_space=pl.ANY)] * self.n
        self.scratch = [pltpu.SemaphoreType.DMA((self.n, N_DEV - 1)), pltpu.SemaphoreType.DMA((self.n, N_DEV - 1)),
                        pltpu.SemaphoreType.DMA((self.n,))]

    def run(self, ins, outs, sems, first, last, compute):
        send_sems, recv_sems, local_sems = sems
        x, y, c = lax.axis_index("x"), lax.axis_index("y"), lax.axis_index("c")
        me = 4 * x + 2 * y + c
        local, remote = [], []
        for k, (_, scatter) in enumerate(self.items):
            own = ins[k].at[me] if scatter else ins[k]
            local.append(pltpu.make_async_copy(own, outs[k].at[me], local_sems.at[k]))
        for r in range(1, N_DEV):
            px = 1 - x if r & 4 else x
            py = 1 - y if r & 2 else y
            pc = 1 - c if r & 1 else c
            peer = 4 * px + 2 * py + pc
            for k, (_, scatter) in enumerate(self.items):
                src = ins[k].at[peer] if scatter else ins[k]
                remote.append(pltpu.make_async_remote_copy(
                    src_ref=src, dst_ref=outs[k].at[me],
                    send_sem=send_sems.at[k, r - 1], recv_sem=recv_sems.at[k, r - 1],
                    device_id=(px, py, pc), device_id_type=pl.DeviceIdType.MESH))

        def start():
            for cp in local + remote:
                cp.start()

        def finish():
            for cp in remote:
                cp.wait_recv()
            for cp in remote:
                cp.wait_send()
            for cp in local:
                cp.wait()

        _run_phases(first, last, start, compute, finish)


def _run_phases(first, last, start, compute, finish):
    if first is None:
        start()
        compute()
        finish()
    else:
        pl.when(first)(start)
        compute()
        pl.when(last)(finish)


class _ChipGather(_Exchange):
    def run(self, ins, outs, sems, first, last, compute):
        send_sems, recv_sems, local_sems = sems
        x, y, c = lax.axis_index("x"), lax.axis_index("y"), lax.axis_index("c")
        sibling = (x, y, 1 - c)
        chips = [(1 - x, y), (x, 1 - y), (1 - x, 1 - y)]
        slot = lambda px, py, pc: 4 * px + 2 * py + pc

        def copy(k, n, src, dst_slot, to):
            return pltpu.make_async_remote_copy(
                src_ref=src, dst_ref=outs[k].at[dst_slot], send_sem=send_sems.at[k, n], recv_sem=recv_sems.at[k, n],
                device_id=to, device_id_type=pl.DeviceIdType.MESH)

        local, own, passed, arrivals = [], [], [], []
        for k in range(self.n):
            me = slot(x, y, c)
            local.append(pltpu.make_async_copy(ins[k], outs[k].at[me], local_sems.at[k]))
            own.append(copy(k, 0, ins[k], me, sibling))
            arrivals.append(copy(k, 0, ins[k], slot(*sibling), sibling))
            for j, chip in enumerate(chips):
                theirs = slot(*chip, c)
                own.append(copy(k, 1 + j, ins[k], me, (*chip, c)))
                passed.append((copy(k, 1 + j, ins[k], theirs, sibling),
                               copy(k, 4 + j, outs[k].at[theirs], theirs, sibling)))
                arrivals.append(copy(k, 4 + j, ins[k], slot(*chip, 1 - c), sibling))

        def start():
            for cp in local + own:
                cp.start()

        def finish():
            for landed, onward in passed:
                landed.wait_recv()
                onward.start()
            for cp in arrivals:
                cp.wait_recv()
            for cp in own + [onward for _, onward in passed]:
                cp.wait_send()
            for cp in local:
                cp.wait()

        _run_phases(first, last, start, compute, finish)


def _grid_ends(grid):
    ids = [pl.program_id(d) for d in range(len(grid))]
    first = functools.reduce(jnp.logical_and, [i == 0 for i in ids])
    last = functools.reduce(jnp.logical_and, [i == g - 1 for i, g in zip(ids, grid)])
    return first, last


def _host(core, n_in, n_out, n_scratch, hosted, grid):
    if hosted is None:
        return core
    nh = hosted.n

    def body(*refs):
        ins, rest = refs[:n_in], refs[n_in:]
        h_ins, rest = rest[:nh], rest[nh:]
        outs, rest = rest[:n_out], rest[n_out:]
        h_outs, rest = rest[:nh], rest[nh:]
        scratch, sems = rest[:n_scratch], rest[n_scratch:]
        first, last = _grid_ends(grid)
        hosted.run(h_ins, h_outs, sems, first, last, lambda: core(*ins, *outs, *scratch))

    return body


def _hosted_parts(hosted):
    if hosted is None:
        return [], [], [], []
    return list(hosted.specs), list(hosted.out_shape), list(hosted.arrays), list(hosted.scratch)


def _exchange(name, items):
    ex = _Exchange(items)
    n = ex.n

    def body(*refs):
        ex.run(refs[:n], refs[n:2 * n], refs[2 * n:], None, None, lambda: None)

    return pl.pallas_call(
        body, name=name, out_shape=tuple(ex.out_shape), in_specs=ex.specs, out_specs=tuple(ex.specs),
        scratch_shapes=ex.scratch,
    )(*ex.arrays)


def _matmul_blocks(t, k, n, a_bytes, o_bytes):
    for bt, cap in ((_row_block(t), WIDE_BLOCK), (_row_block(t), ROW_BLOCK), (ROW_BLOCK, ROW_BLOCK)):
        bn = _divisor_block(n, cap)
        if 2 * (bt * k * a_bytes + bn * k * 2 + bt * bn * o_bytes) <= MATMUL_VMEM_BUDGET:
            return bt, bn
    return ROW_BLOCK, _divisor_block(n, 2 * LANES)


def _matmul_nt(a, w, *, name, out_dtype):
    t, k = a.shape
    n = w.shape[0]
    assert w.shape[1] == k
    bt, bn = _matmul_blocks(t, k, n, a.dtype.itemsize, jnp.dtype(out_dtype).itemsize)

    def body(a_ref, w_ref, o_ref):
        o_ref[...] = _dot_nt(a_ref[...], w_ref[...]).astype(o_ref.dtype)

    return pl.pallas_call(
        body, name=name, grid=(t // bt, n // bn),
        in_specs=[pl.BlockSpec((bt, k), lambda i, j: (i, 0)), pl.BlockSpec((bn, k), lambda i, j: (j, 0))],
        out_specs=pl.BlockSpec((bt, bn), lambda i, j: (i, j)),
        out_shape=jax.ShapeDtypeStruct((t, n), out_dtype), compiler_params=_params(2),
    )(a, w)


def _matmul_rows(a, w, *, name, out_dtype=F32):
    t, k = a.shape
    n = w.shape[1]
    assert w.shape[0] == k
    bt = _row_block(t)

    def body(a_ref, w_ref, o_ref):
        o_ref[...] = jnp.dot(a_ref[...], w_ref[...], preferred_element_type=F32).astype(o_ref.dtype)

    return pl.pallas_call(
        body, name=name, grid=(t // bt,),
        in_specs=[pl.BlockSpec((bt, k), lambda i: (i, 0)), pl.BlockSpec((k, n), lambda i: (0, 0))],
        out_specs=pl.BlockSpec((bt, n), lambda i: (i, 0)),
        out_shape=jax.ShapeDtypeStruct((t, n), out_dtype), compiler_params=_params(1),
    )(a, w)


def _matmul_tn(a, b, *, name):
    t, m = a.shape
    n = b.shape[1]
    bt = TOKEN_STEP if t % TOKEN_STEP == 0 else ROW_BLOCK
    bm = _divisor_block(m, WIDE_BLOCK)
    bn = _divisor_block(n, WIDE_BLOCK)
    steps = t // bt

    def body(a_ref, b_ref, o_ref, acc):
        step = pl.program_id(2)

        @pl.when(step == 0)
        def _():
            acc[...] = jnp.zeros_like(acc)

        acc[...] += _dot_tn(a_ref[...], b_ref[...])

        @pl.when(step == steps - 1)
        def _():
            o_ref[...] = acc[...].astype(o_ref.dtype)

    return pl.pallas_call(
        body, name=name, grid=(m // bm, n // bn, steps),
        in_specs=[pl.BlockSpec((bt, bm), lambda i, j, s: (s, i)), pl.BlockSpec((bt, bn), lambda i, j, s: (s, j))],
        out_specs=pl.BlockSpec((bm, bn), lambda i, j, s: (i, j)),
        out_shape=jax.ShapeDtypeStruct((m, n), BF16), scratch_shapes=[pltpu.VMEM((bm, bn), F32)],
        compiler_params=_params(3),
    )(a, b)


def _rmsnorm_fwd(x, g, *, group, name, hosted=None):
    t, w = x.shape
    bt = ROW_BLOCK

    def body(x_ref, g_ref, o_ref, x16_ref):
        x16_ref[...] = x_ref[...].astype(BF16)
        for s in range(0, w, group):
            xs = x_ref[:, s:s + group].astype(F32)
            r = lax.rsqrt(jnp.mean(xs * xs, axis=-1, keepdims=True) + EPS)
            o_ref[:, s:s + group] = (xs * r * g_ref[:, s:s + group]).astype(o_ref.dtype)

    grid = (t // bt,)
    h_specs, h_shapes, h_args, h_scratch = _hosted_parts(hosted)
    rows = pl.BlockSpec((bt, w), lambda i: (i, 0))
    return pl.pallas_call(
        _host(body, 2, 2, 0, hosted, grid), name=name, grid=grid,
        in_specs=[rows, pl.BlockSpec((1, w), lambda i: (0, 0))] + h_specs,
        out_specs=tuple([rows, rows] + h_specs),
        out_shape=tuple([jax.ShapeDtypeStruct((t, w), BF16)] * 2 + h_shapes),
        scratch_shapes=h_scratch, compiler_params=_params(1),
    )(x, g, *h_args)


def _norm_input_grad(terms, x, g, *, group, name, out_dtypes, resid=None, init=None, hosted=None, k_chunks=1):
    t, w = x.shape
    n_terms = len(terms)
    kc = [a.shape[1] // k_chunks for a, _, _ in terms]
    per_row = sum(c * a.dtype.itemsize for c, (a, _, _) in zip(kc, terms))
    per_row += w * sum(r.dtype.itemsize for r in (x, resid, init) if r is not None)
    per_row += w * sum(jnp.dtype(dt).itemsize for dt in out_dtypes)
    fixed = 2 * sum(w * c * 2 for c in kc)
    bt = next(b for b in (2 * ROW_BLOCK, ROW_BLOCK, ROW_BLOCK // 2, ROW_BLOCK // 4)
              if t % b == 0 and fixed + 2 * b * per_row + 5 * b * w * 4 <= MATMUL_VMEM_BUDGET)
    resid_at = 2 * n_terms + 2
    init_at = resid_at + (resid is not None)
    n_in = init_at + (init is not None)
    grid = (t // bt, k_chunks)

    def body(*refs):
        x_ref, g_ref = refs[2 * n_terms], refs[2 * n_terms + 1]
        dx_refs, dg_ref, dh_ref = refs[n_in:-2], refs[-2], refs[-1]
        chunk = pl.program_id(1)

        @pl.when((pl.program_id(0) == 0) & (chunk == 0))
        def _():
            dg_ref[...] = jnp.zeros_like(dg_ref)

        part = None
        for k in range(n_terms):
            if terms[k][2]:
                term = jnp.dot(refs[2 * k][...], refs[2 * k + 1][...], preferred_element_type=F32)
            else:
                term = _dot_nt(refs[2 * k][...], refs[2 * k + 1][...])
            part = term if part is None else part + term

        @pl.when(chunk == 0)
        def _():
            dh_ref[...] = part if init is None else refs[init_at][...] + part

        @pl.when(chunk > 0)
        def _():
            dh_ref[...] += part

        @pl.when(chunk == k_chunks - 1)
        def _():
            for s in range(0, w, group):
                xs = x_ref[:, s:s + group].astype(F32)
                dhs = dh_ref[:, s:s + group]
                r = lax.rsqrt(jnp.mean(xs * xs, axis=-1, keepdims=True) + EPS)
                xh = xs * r
                dg_ref[:, s:s + group] += jnp.sum(dhs * xh, axis=0, keepdims=True)
                dxh = dhs * g_ref[:, s:s + group]
                dx = r * (dxh - xh * jnp.mean(dxh * xh, axis=-1, keepdims=True))
                if resid is not None:
                    dx = refs[resid_at][:, s:s + group] + dx
                for dx_ref in dx_refs:
                    dx_ref[:, s:s + group] = dx.astype(dx_ref.dtype)

    row = pl.BlockSpec((bt, w), lambda i, k: (i, 0))
    vec = pl.BlockSpec((1, w), lambda i, k: (0, 0))
    in_specs, args = [], []
    for c, (a, wt, w_is_kn) in zip(kc, terms):
        assert wt.shape == ((a.shape[1], w) if w_is_kn else (w, a.shape[1]))
        w_spec = pl.BlockSpec((c, w), lambda i, k: (k, 0)) if w_is_kn else pl.BlockSpec((w, c), lambda i, k: (0, k))
        in_specs += [pl.BlockSpec((bt, c), lambda i, k: (i, k)), w_spec]
        args += [a, wt]
    extra = [r for r in (resid, init) if r is not None]
    in_specs += [row, vec] + [row] * len(extra)
    args += [x, g] + extra
    h_specs, h_shapes, h_args, h_scratch = _hosted_parts(hosted)
    return pl.pallas_call(
        _host(body, n_in, len(out_dtypes) + 1, 1, hosted, grid), name=name, grid=grid, in_specs=in_specs + h_specs,
        out_specs=tuple([row] * len(out_dtypes) + [vec] + h_specs),
        out_shape=tuple([jax.ShapeDtypeStruct((t, w), dt) for dt in out_dtypes] + [jax.ShapeDtypeStruct((1, w), F32)]
                        + h_shapes),
        scratch_shapes=[pltpu.VMEM((bt, w), F32)] + h_scratch, compiler_params=_params(2),
    )(*args, *h_args)


def _tile_plan(tile):
    is_q = tile < N_PAIRS
    is_dil = (tile % N_PAIRS) >= N_PAIRS // 2
    return is_q, is_dil, (0 if is_q else 2) + (1 if is_dil else 0)


def _proj_tile(kind, pair):
    return kind * N_PAIRS + pair


def _segment_ones():
    lane = np.arange(LANES)
    return jnp.asarray((lane[:, None] // HEAD_DIM) == (lane[None, :] // HEAD_DIM), BF16)


def _rope_tables(seq):
    inv_freq = jnp.power(jnp.float32(ROPE_THETA), -jnp.arange(ROPE_HALF, dtype=F32) * 2.0 / ROPE_DIM)
    ang = jnp.arange(seq).astype(F32)[:, None] * inv_freq[None, :]
    cos, sin = jnp.cos(ang), jnp.sin(ang)
    ones = jnp.ones((seq, HEAD_DIM - ROPE_DIM), F32)
    zeros = jnp.zeros((seq, HEAD_DIM - ROPE_DIM), F32)
    zh = jnp.zeros((seq, ROPE_HALF), F32)
    cos_t = jnp.concatenate([cos, cos, ones], axis=1)
    sin_a = jnp.concatenate([-sin, zh, zeros], axis=1)
    sin_b = jnp.concatenate([zh, sin, zeros], axis=1)
    return tuple(jnp.tile(tab, (1, 2)) for tab in (cos_t, sin_a, sin_b))


def _log_sigmoid(z):
    return jnp.minimum(z, 0.0) - jnp.log1p(jnp.exp(-jnp.abs(z)))


def _aug_placement():
    place = np.zeros((N_PAIRS, LANES, LANES), np.float32)
    for is_k in range(2):
        for pair in range(N_PAIRS // 2):
            for e in range(2):
                other = HEAD_DIM * (1 - e)
                ones_at = other + (AUG_C if is_k else AUG_ONE)
                c_at = other + (AUG_ONE if is_k else AUG_C)
                for n in range(3):
                    place[4 * is_k + pair, N_HEADS_FOX * n + 2 * pair + e, c_at + n] = -1.0 if is_k else 1.0
                    place[4 * is_k + pair, 3 * N_HEADS_FOX, ones_at + n] = 1.0
    return jnp.asarray(place, BF16)


def _qk_prep_fwd(proj, h1, w_fa_t, b_pad, gains, rope, seq):
    t = proj.shape[0]
    bt = ROW_BLOCK
    nsb = seq // bt
    seg = _segment_ones()
    rr = np.arange(bt)
    tri = jnp.asarray(rr[:, None] <= rr[None, :], BF16)

    def body(p_ref, h_ref, wfa_ref, b_ref, g_ref, cos_ref, sa_ref, sb_ref, seg_ref, tri_ref, place_ref,
             qk_ref, fa_ref, carry):
        @pl.when(pl.program_id(0) % nsb == 0)
        def _():
            carry[...] = jnp.zeros_like(carry)

        lane = lax.broadcasted_iota(jnp.int32, (bt, LANES), 1)
        fa = _dot_nt(h_ref[...], wfa_ref[...])
        fa_ref[...] = fa
        logf = jnp.where(lane < N_HEADS_FOX, _log_sigmoid(fa + b_ref[...]), 0.0)
        c_rows = _split_dot(logf.T[0:N_HEADS_FOX, :], tri_ref[...], 3) + carry[:, 0:1]
        carry[...] = jnp.broadcast_to(c_rows[:, bt - 1:bt], carry.shape)
        cblk = jnp.concatenate([c_rows, jnp.zeros((LANES - N_HEADS_FOX, bt), F32)], axis=0).T
        packed = jnp.where(lane == 3 * N_HEADS_FOX, 1.0, 0.0)
        rest = cblk * LOG2E
        for n in range(3):
            term = rest.astype(BF16).astype(F32)
            packed = packed + (pltpu.roll(term, N_HEADS_FOX * n, 1) if n else term)
            rest = rest - term
        packed = packed.astype(BF16)
        low = lane < HEAD_DIM

        for tile in range(2 * N_PAIRS):
            is_q, is_dil, grow = _tile_plan(tile)
            pair = tile % N_PAIRS
            src = _proj_tile(0 if is_q else 1, pair) * LANES
            xs = p_ref[:, src:src + LANES].astype(F32)
            r = lax.rsqrt(_split_dot(xs * xs, seg_ref[...], 2) * (1.0 / HEAD_DIM) + EPS)
            yv = xs * r * g_ref[grow:grow + 1, :]
            if is_dil:
                yv = (yv * cos_ref[...] + pltpu.roll(yv, LANES - ROPE_HALF, 1) * sa_ref[...]
                      + pltpu.roll(yv, ROPE_HALF, 1) * sb_ref[...])
                aug = jnp.zeros((bt, LANES), F32)
            else:
                aug = jnp.dot(packed, place_ref[(0 if is_q else N_PAIRS // 2) + pair], preferred_element_type=F32)
            if is_q:
                yv = yv * (HEAD_DIM ** -0.5 * LOG2E)
            dst = ((0 if is_q else N_HEADS) + 2 * pair) * LANES
            qk_ref[:, dst:dst + LANES] = jnp.where(low, yv, aug).astype(BF16)
            qk_ref[:, dst + LANES:dst + 2 * LANES] = jnp.where(low, aug, yv).astype(BF16)

    row128 = pl.BlockSpec((bt, LANES), lambda i: (i, 0))
    rope_spec = pl.BlockSpec((bt, LANES), lambda i: (i % nsb, 0))
    const = lambda shape: pl.BlockSpec(shape, lambda i: (0,) * len(shape))
    return pl.pallas_call(
        body, name="qk_prep_fwd", grid=(t // bt,),
        in_specs=[pl.BlockSpec((bt, 2 * D_MODEL), lambda i: (i, 0)), pl.BlockSpec((bt, D_MODEL), lambda i: (i, 0)),
                  const((LANES, D_MODEL)), const((1, LANES)), const((8, LANES)), rope_spec, rope_spec, rope_spec,
                  const((LANES, LANES)), const((bt, bt)), const((N_PAIRS, LANES, LANES))],
        out_specs=(pl.BlockSpec((bt, 2 * N_HEADS * LANES), lambda i: (i, 0)), row128),
        out_shape=(jax.ShapeDtypeStruct((t, 2 * N_HEADS * LANES), BF16), jax.ShapeDtypeStruct((t, LANES), F32)),
        scratch_shapes=[pltpu.VMEM((8, LANES), F32)], compiler_params=_params(1),
    )(proj, h1, w_fa_t, b_pad, gains, *rope, seg, tri, _aug_placement())


def _qk_prep_bwd(dq, dk, dqx, dkx, dv, proj, fa, b_pad, gains, rope, seq):
    t = proj.shape[0]
    bt = ROW_BLOCK
    nsb = seq // bt
    nblk = t // bt
    seg = _segment_ones()
    rr = np.arange(bt)
    triu = jnp.asarray(rr[:, None] >= rr[None, :], BF16)

    def body(dq_ref, dk_ref, dqx_ref, dkx_ref, dv_ref, p_ref, fa_ref, b_ref, g_ref, cos_ref, sa_ref, sb_ref, seg_ref,
             triu_ref, dp_ref, dg_ref, db_ref, carry):
        step = pl.program_id(0)

        @pl.when(step == 0)
        def _():
            dg_ref[...] = jnp.zeros_like(dg_ref)
            db_ref[...] = jnp.zeros_like(db_ref)

        @pl.when(step % nsb == 0)
        def _():
            carry[...] = jnp.zeros_like(carry)

        for tile in range(2 * N_PAIRS):
            is_q, is_dil, grow = _tile_plan(tile)
            first = _proj_tile(0 if is_q else 1, tile % N_PAIRS) * LANES
            cols = slice(first, first + LANES)
            src = dq_ref if is_q else dk_ref
            half = slice((tile % N_PAIRS) * LANES, (tile % N_PAIRS + 1) * LANES)
            dy = src[:, half]
            dy = dy * (HEAD_DIM ** -0.5 if is_q else LN2)
            if is_dil:
                dy = (dy * cos_ref[...] + pltpu.roll(dy * sa_ref[...], ROPE_HALF, 1)
                      + pltpu.roll(dy * sb_ref[...], LANES - ROPE_HALF, 1))
            xs = p_ref[:, cols].astype(F32)
            r = lax.rsqrt(_split_dot(xs * xs, seg_ref[...], 2) * (1.0 / HEAD_DIM) + EPS)
            xh = xs * r
            dg_ref[tile:tile + 1, :] += jnp.sum(dy * xh, axis=0, keepdims=True)
            dxh = dy * g_ref[grow:grow + 1, :]
            seg_mean = _split_dot(dxh * xh, seg_ref[...], 2) * (1.0 / HEAD_DIM)
            dp_ref[:, cols] = (r * (dxh - xh * seg_mean)).astype(BF16)

        lane = lax.broadcasted_iota(jnp.int32, (bt, LANES), 1)
        dc = jnp.zeros((bt, LANES), F32)
        for h in range(N_HEADS_FOX):
            other = (h // 2) * LANES + HEAD_DIM * (1 - h % 2)
            row_sum = dqx_ref[:, other + AUG_C:other + AUG_C + 1]
            col_sum = dkx_ref[:, other + AUG_ONE:other + AUG_ONE + 1]
            dc = jnp.where(lane == h, row_sum - col_sum, dc)
        d_rows = _split_dot(dc.T[0:N_HEADS_FOX, :], triu_ref[...], 3) + carry[:, 0:1]
        carry[...] = jnp.broadcast_to(d_rows[:, 0:1], carry.shape)
        dlogf = jnp.concatenate([d_rows, jnp.zeros((LANES - N_HEADS_FOX, bt), F32)], axis=0).T
        z = fa_ref[...] + b_ref[...]
        dfa = dlogf * (1.0 / (1.0 + jnp.exp(z)))
        db_ref[0:1, :] += jnp.sum(dfa, axis=0, keepdims=True)
        for group in range(2):
            first = _proj_tile(2, group * (N_PAIRS // 2)) * LANES
            dp_ref[:, first:first + W_GROUP] = dv_ref[:, group * W_GROUP:(group + 1) * W_GROUP]
        dp_ref[:, MAIN_COLS:PROJ_COLS] = dfa.astype(BF16)

    rev = lambda i: nblk - 1 - i
    row = lambda w: pl.BlockSpec((bt, w), lambda i: (rev(i), 0))
    rope_spec = pl.BlockSpec((bt, LANES), lambda i: (rev(i) % nsb, 0))
    const = lambda shape: pl.BlockSpec(shape, lambda i: (0, 0))
    return pl.pallas_call(
        body, name="qk_prep_bwd", grid=(nblk,),
        in_specs=[row(D_MODEL), row(D_MODEL), row(W_GROUP), row(W_GROUP), row(D_MODEL), row(2 * D_MODEL), row(LANES),
                  const((1, LANES)), const((8, LANES)), rope_spec, rope_spec, rope_spec, const((LANES, LANES)),
                  const((bt, bt))],
        out_specs=(row(PROJ_COLS), const((2 * N_PAIRS, LANES)), const((8, LANES))),
        out_shape=(jax.ShapeDtypeStruct((t, PROJ_COLS), BF16),
                   jax.ShapeDtypeStruct((2 * N_PAIRS, LANES), F32), jax.ShapeDtypeStruct((8, LANES), F32)),
        scratch_shapes=[pltpu.VMEM((8, LANES), F32)], compiler_params=_params(1),
    )(dq, dk, dqx, dkx, dv, proj, fa, b_pad, gains, *rope, seg, triu)


def _bias_tables(seq, keys_first):
    nb = seq // ATT_BLOCK
    idx = np.arange(ATT_BLOCK)
    q_idx, k_idx = (idx[None, None, :], idx[None, :, None]) if keys_first else (idx[None, :, None], idx[None, None, :])
    dist = np.arange(nb)[:, None, None] * ATT_BLOCK + q_idx - k_idx
    causal = dist >= 0
    count = np.zeros(dist.shape, np.int32)
    for window, dilation in DILATION_PAIRS:
        count = count + (causal & (dist % dilation == 0) & (dist <= window))
    fox = np.where(causal, 0.0, NEG)
    dil = np.where(count == 3, math.log2(3.0), np.where(count == 2, 1.0, np.where(count == 1, 0.0, NEG)))
    return jnp.asarray(np.stack([fox, dil], axis=0), F32)


def _attn_specs(seq):
    nb = seq // ATT_BLOCK
    col = pl.BlockSpec((seq, LANES), lambda b, j: (b, j))
    heads = lambda off: pl.BlockSpec((seq, 2 * LANES), lambda b, j: (b, off + j))
    v_spec = pl.BlockSpec((seq, LANES), lambda b, j: (b, _proj_tile(2, j)))
    table_spec = pl.BlockSpec((1, nb, ATT_BLOCK, ATT_BLOCK), lambda b, j: (j // (N_PAIRS // 2), 0, 0, 0))
    return col, heads, v_spec, table_spec


def _head_lanes(e, shape, axis):
    pos = lax.broadcasted_iota(jnp.int32, shape, axis)
    return pos < HEAD_DIM if e == 0 else pos >= HEAD_DIM


def _attn_fwd(qk, proj, tables, seq, hosted=None):
    t = qk.shape[0]
    nb = seq // ATT_BLOCK
    blk = ATT_BLOCK

    def body(q_ref, k_ref, v_ref, tab_ref, o_ref, lse_ref):
        mine = [_head_lanes(e, (seq, LANES), 1) for e in range(2)]
        lane = lax.broadcasted_iota(jnp.int32, (seq, LANES), 1)
        v_aug = [jnp.where(mine[e], v_ref[...], (lane == HEAD_DIM * (1 - e)).astype(BF16)) for e in range(2)]
        def scores(i, e):
            heads_e = slice(e * LANES, (e + 1) * LANES)
            s = _dot_nt(q_ref[i * blk:(i + 1) * blk, heads_e], k_ref[0:(i + 1) * blk, heads_e])
            s = jnp.concatenate([s[:, jj * blk:(jj + 1) * blk] + tab_ref[0, i - jj] for jj in range(i + 1)], axis=1)
            return s, jnp.max(s, axis=1, keepdims=True)

        chains = [(i, e) for i in reversed(range(nb)) for e in range(2)]
        ahead = 2
        pending = [scores(*chain) for chain in chains[:ahead]]
        done = {}
        for n, (i, e) in enumerate(chains):
            s, m = pending.pop(0)
            if n + ahead < len(chains):
                pending.append(scores(*chains[n + ahead]))
            acc = jnp.dot(jnp.exp2(s - m).astype(BF16), v_aug[e][0:(i + 1) * blk], preferred_element_type=F32)
            ones_at = HEAD_DIM * (1 - e)
            l = acc[:, ones_at:ones_at + 1]
            done[e] = (acc / l, m + jnp.log2(l))
            if e == 1:
                rows = slice(i * blk, (i + 1) * blk)
                o_ref[rows, :] = jnp.where(mine[0][rows], done[0][0], done[1][0]).astype(o_ref.dtype)
                lse_ref[rows, :] = jnp.where(mine[0][rows], done[0][1], done[1][1])

    col, heads, v_spec, table_spec = _attn_specs(seq)
    grid = (t // seq, N_PAIRS)
    h_specs, h_shapes, h_args, h_scratch = _hosted_parts(hosted)
    return pl.pallas_call(
        _host(body, 4, 2, 0, hosted, grid), name="attn_fwd", grid=grid,
        in_specs=[heads(0), heads(N_PAIRS), v_spec, table_spec] + h_specs,
        out_specs=tuple([col, col] + h_specs),
        out_shape=tuple([jax.ShapeDtypeStruct((t, D_MODEL), BF16), jax.ShapeDtypeStruct((t, D_MODEL), F32)] + h_shapes),
        scratch_shapes=h_scratch, compiler_params=_params(2),
    )(qk, qk, proj, tables, *h_args)


def _attn_bwd(qk, proj, tables, o, lse, do, seq, hosted=None):
    t = qk.shape[0]
    nb = seq // ATT_BLOCK
    blk = ATT_BLOCK
    group = math.gcd(nb, ATT_GROUP)

    def body(q_ref, k_ref, v_ref, tab_ref, o_ref, lse_ref, do_ref,
             dq_ref, dk_ref, dv_ref, dqx_ref, dkx_ref, dk_acc, dv_acc):
        mine = [_head_lanes(e, (blk, LANES), 1) for e in range(2)]
        top = _head_lanes(0, (LANES, blk), 0)
        head_rows = lax.broadcasted_iota(jnp.int32, (8, LANES), 0)
        head_of_lane = lax.broadcasted_iota(jnp.int32, (8, LANES), 1) // HEAD_DIM
        head_sel = (head_rows == head_of_lane).astype(BF16)
        dk_acc[...] = jnp.zeros_like(dk_acc)
        dv_acc[...] = jnp.zeros_like(dv_acc)

        def block_rows(i):
            return pl.ds(pl.multiple_of(i * blk, blk), blk)

        def q_group(g, _):
            base = g * group
            qs, doe, delta, lse_e = [], [], [], []
            for b in range(group):
                rows = block_rows(base + b)
                qs.append([q_ref[rows, e * LANES:(e + 1) * LANES] for e in range(2)])
                do_blk = do_ref[rows, :]
                doe.append([jnp.where(mine[e], do_blk, jnp.zeros_like(do_blk)) for e in range(2)])
                delta_t = _split_dot_nt(head_sel, do_blk.astype(F32) * o_ref[rows, :].astype(F32), 3)
                lse_t = _split_dot_nt(head_sel, lse_ref[rows, :], 3) * (1.0 / HEAD_DIM)
                delta.append([delta_t[e:e + 1, :] for e in range(2)])
                lse_e.append([lse_t[e:e + 1, :] for e in range(2)])

            def key_block(dq_t, jj, members):
                krows = block_rows(jj)
                v = v_ref[krows, :]
                dq_t = [list(d) for d in dq_t]
                lo, hi = slice(0, blk // 2), slice(blk // 2, blk)
                dv_part = [None, None]
                add = lambda acc, part: part if acc is None else acc + part

                def probs(k_sub, v_sub, keys, queries, b, e, dist):
                    q_sub, do_sub = qs[b][e][queries], doe[b][e][queries]
                    p_t = jnp.exp2(_dot_nt(k_sub, q_sub) + tab_ref[0, dist, keys, queries] - lse_e[b][e][:, queries])
                    ds_t = (p_t * (_dot_nt(v_sub, do_sub) - delta[b][e][:, queries])).astype(BF16)
                    return p_t.astype(BF16), ds_t, q_sub, do_sub, k_sub

                def outputs(tile):
                    p_t, ds_t, q_sub, do_sub, k_sub = tile
                    return (jnp.dot(p_t, do_sub, preferred_element_type=F32),
                            jnp.dot(ds_t, q_sub, preferred_element_type=F32), _dot_tn(k_sub, ds_t))

                for e in range(2):
                    k_e = k_ref[krows, e * LANES:(e + 1) * LANES]
                    dk_part = [None, None]
                    tiles = []
                    for b, dist in members:
                        if isinstance(dist, int) and dist == 0:
                            tiles.append((b, probs(k_e[lo], v[lo], lo, slice(0, blk), b, e, dist),
                                          probs(k_e[hi], v[hi], hi, hi, b, e, dist)))
                        else:
                            tiles.append((b, probs(k_e, v, slice(0, blk), slice(0, blk), b, e, dist), None))
                    for b, first, second in tiles:
                        if second is not None:
                            dv_a, dk_a, dq_a = outputs(first)
                            dv_b, dk_b, dq_b = outputs(second)
                            halves = ((dv_a, dk_a), (dv_b, dk_b))
                            dq = jnp.concatenate([dq_a[:, lo], dq_a[:, hi] + dq_b], axis=1)
                        else:
                            dv_f, dk_f, dq = outputs(first)
                            halves = ((dv_f[lo], dk_f[lo]), (dv_f[hi], dk_f[hi]))
                        for n, (dv_h, dk_h) in enumerate(halves):
                            dv_part[n] = add(dv_part[n], dv_h)
                            dk_part[n] = add(dk_part[n], dk_h)
                        dq_t[b][e] = dq_t[b][e] + dq
                    dk_acc[e, krows, :] += jnp.concatenate(dk_part, axis=0)
                dv_acc[krows, :] += jnp.concatenate(dv_part, axis=0)
                return tuple(tuple(d) for d in dq_t)

            zacc = jnp.zeros((LANES, blk), F32)
            dq_t = tuple((zacc, zacc) for _ in range(group))
            dq_t = lax.fori_loop(
                0, base, lambda jj, st: key_block(st, jj, [(b, base + b - jj) for b in range(group)]), dq_t)
            for a in range(group):
                dq_t = key_block(dq_t, base + a, [(b, b - a) for b in range(a, group)])
            for b in range(group):
                rows = block_rows(base + b)
                dq_ref[rows, :] = jnp.where(top, dq_t[b][0], dq_t[b][1]).T
                dqx_ref[rows, :] = jnp.where(top, dq_t[b][1], dq_t[b][0]).T
            return 0

        lax.fori_loop(0, nb // group, q_group, 0)
        lo = _head_lanes(0, (seq, LANES), 1)
        dk_ref[...] = jnp.where(lo, dk_acc[0], dk_acc[1])
        dkx_ref[...] = jnp.where(lo, dk_acc[1], dk_acc[0])
        dv_ref[...] = dv_acc[...].astype(dv_ref.dtype)

    col, heads, v_spec, table_spec = _attn_specs(seq)
    grid = (t // seq, N_PAIRS)
    h_specs, h_shapes, h_args, h_scratch = _hosted_parts(hosted)
    f32_out = jax.ShapeDtypeStruct((t, D_MODEL), F32)
    return pl.pallas_call(
        _host(body, 7, 5, 2, hosted, grid), name="attn_bwd", grid=grid,
        in_specs=[heads(0), heads(N_PAIRS), v_spec, table_spec, col, col, col] + h_specs,
        out_specs=tuple([col] * 5 + h_specs),
        out_shape=tuple([f32_out, f32_out, jax.ShapeDtypeStruct((t, D_MODEL), BF16), f32_out, f32_out] + h_shapes),
        scratch_shapes=[pltpu.VMEM((2, seq, LANES), F32), pltpu.VMEM((seq, LANES), F32)] + h_scratch,
        compiler_params=_params(2),
    )(qk, qk, proj, tables, o, lse, do, *h_args)


def _row_block(t):
    return 2 * ROW_BLOCK if t % (2 * ROW_BLOCK) == 0 else ROW_BLOCK


def _out_proj_ffn_norm(o, g_out, w_out, x, g_ffn):
    t = o.shape[0]
    bt = _row_block(t)

    def body(o_ref, go_ref, w_ref, x_ref, gf_ref, on_ref, x2_ref, h2_ref):
        for s in range(0, D_MODEL, W_GROUP):
            os_ = o_ref[:, s:s + W_GROUP].astype(F32)
            r = lax.rsqrt(jnp.mean(os_ * os_, axis=-1, keepdims=True) + EPS)
            on_ref[:, s:s + W_GROUP] = (os_ * r * go_ref[:, s:s + W_GROUP]).astype(BF16)
        x2 = x_ref[...] + jnp.dot(on_ref[...], w_ref[...], preferred_element_type=F32)
        x2_ref[...] = x2.astype(BF16)
        r2 = lax.rsqrt(jnp.mean(x2 * x2, axis=-1, keepdims=True) + EPS)
        h2_ref[...] = (x2 * r2 * gf_ref[...]).astype(BF16)

    row = pl.BlockSpec((bt, D_MODEL), lambda i: (i, 0))
    vec = pl.BlockSpec((1, D_MODEL), lambda i: (0, 0))
    return pl.pallas_call(
        body, name="out_proj", grid=(t // bt,),
        in_specs=[row, vec, pl.BlockSpec((D_MODEL, D_MODEL), lambda i: (0, 0)), row, vec],
        out_specs=(row, row, row),
        out_shape=tuple([jax.ShapeDtypeStruct((t, D_MODEL), BF16)] * 3),
        compiler_params=_params(1),
    )(o, g_out, w_out, x, g_ffn)


def _ffn_gate_up(h2, w_gate_t, w_up_t):
    t = h2.shape[0]
    bt = _row_block(t)
    bn = _divisor_block(D_FF, WIDE_BLOCK)

    def body(h_ref, wg_ref, wu_ref, a_ref, u_ref, f_ref):
        a = _dot_nt(h_ref[...], wg_ref[...])
        u = _dot_nt(h_ref[...], wu_ref[...])
        a_ref[...] = a.astype(BF16)
        u_ref[...] = u.astype(BF16)
        f_ref[...] = (a * jax.nn.sigmoid(a) * u).astype(BF16)

    blk = pl.BlockSpec((bt, bn), lambda j, i: (i, j))
    w_blk = pl.BlockSpec((bn, D_MODEL), lambda j, i: (j, 0))
    shape = jax.ShapeDtypeStruct((t, D_FF), BF16)
    return pl.pallas_call(
        body, name="ffn_gate_up", grid=(D_FF // bn, t // bt),
        in_specs=[pl.BlockSpec((bt, D_MODEL), lambda j, i: (i, 0)), w_blk, w_blk],
        out_specs=(blk, blk, blk), out_shape=(shape, shape, shape), compiler_params=_params(2),
    )(h2, w_gate_t, w_up_t)


def _ffn_down_grad(dy16, w_down, a, u):
    t = a.shape[0]
    bt = _row_block(t)
    bn = _divisor_block(D_FF, WIDE_BLOCK)

    def body(dy_ref, w_ref, a_ref, u_ref, da_ref, du_ref):
        df = _dot_nt(dy_ref[...], w_ref[...])
        av = a_ref[...].astype(F32)
        sg = jax.nn.sigmoid(av)
        da_ref[...] = (df * u_ref[...].astype(F32) * sg * (1.0 + av * (1.0 - sg))).astype(BF16)
        du_ref[...] = (df * av * sg).astype(BF16)

    blk = pl.BlockSpec((bt, bn), lambda j, i: (i, j))
    shape = jax.ShapeDtypeStruct((t, D_FF), BF16)
    return pl.pallas_call(
        body, name="d_ffn_down", grid=(D_FF // bn, t // bt),
        in_specs=[pl.BlockSpec((bt, D_MODEL), lambda j, i: (i, 0)), pl.BlockSpec((bn, D_MODEL), lambda j, i: (j, 0)),
                  blk, blk],
        out_specs=(blk, blk), out_shape=(shape, shape), compiler_params=_params(2),
    )(dy16, w_down, a, u)


def _ffn_down_loss(f, w_down, x2, target):
    t, w = x2.shape
    bt = _row_block(t)

    def body(f_ref, w_ref, x_ref, t_ref, dy16_ref, loss_ref):
        @pl.when(pl.program_id(0) == 0)
        def _():
            loss_ref[...] = jnp.zeros_like(loss_ref)

        err = (x_ref[...] + jnp.dot(f_ref[...], w_ref[...], preferred_element_type=F32)) - t_ref[...]
        dy16_ref[...] = (err * (1.0 / w)).astype(BF16)
        loss_ref[...] += 0.5 * jnp.sum(jnp.mean(err * err, axis=-1, keepdims=True), axis=0, keepdims=True)

    row = pl.BlockSpec((bt, w), lambda i: (i, 0))
    return pl.pallas_call(
        body, name="ffn_down_loss", grid=(t // bt,),
        in_specs=[pl.BlockSpec((bt, D_FF), lambda i: (i, 0)), pl.BlockSpec((D_FF, w), lambda i: (0, 0)), row, row],
        out_specs=(row, pl.BlockSpec((8, LANES), lambda i: (0, 0))),
        out_shape=(jax.ShapeDtypeStruct((t, w), BF16), jax.ShapeDtypeStruct((8, LANES), F32)),
        compiler_params=_params(1),
    )(f, w_down, x2, target)


def _adamw(parts, w, m, v, *, name):
    _, rows, cols = w.shape
    br = rows if rows <= 512 else 256
    assert rows % br == 0

    def body(p_ref, w_ref, m_ref, v_ref, g_ref, d_ref, nm_ref, nv_ref):
        g = p_ref[0].astype(F32)
        for r in range(1, N_DEV):
            g = g + p_ref[r].astype(F32)
        m2 = ADAM_B1 * m_ref[0] + (1.0 - ADAM_B1) * g
        v2 = ADAM_B2 * v_ref[0] + (1.0 - ADAM_B2) * jnp.square(g)
        m_hat = m2 / (1.0 - ADAM_B1 ** ADAM_STEP)
        v_hat = v2 / (1.0 - ADAM_B2 ** ADAM_STEP)
        g_ref[0] = g
        d_ref[0] = -ADAM_LR * (m_hat / (jnp.sqrt(v_hat) + ADAM_EPS) + ADAM_WD * w_ref[0])
        nm_ref[0] = m2
        nv_ref[0] = v2

    blk = pl.BlockSpec((1, br, cols), lambda i: (0, i, 0))
    shape = jax.ShapeDtypeStruct((1, rows, cols), F32)
    return pl.pallas_call(
        body, name=name, grid=(rows // br,),
        in_specs=[pl.BlockSpec((N_DEV, br, cols), lambda i: (0, i, 0)), blk, blk, blk],
        out_specs=(blk, blk, blk, blk), out_shape=(shape, shape, shape, shape), compiler_params=_params(1),
    )(parts, w, m, v)


_QA, _KA, _VA, _FA, _QD, _KD, _VD = (0, 512), (512, 1024), (1024, 1536), (1536, 1544), (1544, 2056), (2056, 2568), (2568, 3080)
_MAIN_ORDER = (_QA, _QD, _KA, _KD, _VA, _VD)
MAIN_COLS = 3 * D_MODEL
PROJ_COLS = MAIN_COLS + LANES
COL_SHARDED = ("w_in", "w_gate", "w_up")


def _swap(w):
    return jnp.transpose(w, (0, 2, 1))


def _w_in_to_kernel(w_t):
    main = jnp.concatenate([w_t[a:b] for a, b in _MAIN_ORDER], axis=0)
    forget = jnp.pad(w_t[_FA[0]:_FA[1]], ((0, LANES - N_HEADS_FOX), (0, 0)))
    return main, forget


def _w_in_from_kernel(g_t):
    pos = {span: i * W_GROUP for i, span in enumerate(_MAIN_ORDER)}
    parts = []
    for span in (_QA, _KA, _VA, _FA, _QD, _KD, _VD):
        if span == _FA:
            parts.append(g_t[MAIN_COLS:MAIN_COLS + N_HEADS_FOX])
        else:
            parts.append(g_t[pos[span]:pos[span] + W_GROUP])
    return jnp.concatenate(parts, axis=0)


def _pack_small(vals):
    rows = []
    for name, _, n_rows in SMALL_LAYOUT:
        flat = vals[name].reshape(-1).astype(F32)
        rows.append(jnp.pad(flat, (0, n_rows * LANES - flat.shape[0])).reshape(n_rows, LANES))
    packed = jnp.concatenate(rows, axis=0)
    return jnp.pad(packed, ((0, SMALL_ROWS - packed.shape[0]), (0, 0)))


def _unpack_small(packed, like):
    out = {}
    for name, row, n_rows in SMALL_LAYOUT:
        n = like[name].size
        out[name] = packed[row:row + n_rows].reshape(-1)[:n].reshape(like[name].shape)
    return out


def _device_step(x, target, small, shards):
    bsz, seq, _ = x.shape
    t = bsz * seq
    xf = x.reshape(t, D_MODEL)
    tf = target.reshape(t, D_MODEL)
    row = lambda v: v.reshape(1, -1)
    g_out = jnp.concatenate([small["g_out_fox"], small["g_out_dil"]]).reshape(1, D_MODEL)
    gains = jnp.concatenate(
        [jnp.tile(small[n].reshape(1, HEAD_DIM), (1, 2)) for n in ("g_q_fox", "g_q_dil", "g_k_fox", "g_k_dil")]
        + [jnp.zeros((4, LANES), F32)], axis=0)
    b_pad = jnp.pad(small["b_forget"].reshape(1, N_HEADS_FOX), ((0, 0), (0, LANES - N_HEADS_FOX)))
    rope = _rope_tables(seq)
    tables_qk = _bias_tables(seq, keys_first=False)
    tables_kq = _bias_tables(seq, keys_first=True)

    h1, x16, g_in = _rmsnorm_fwd(xf, row(small["g_mix"]), group=D_MODEL, name="norm_mix",
                                 hosted=_ChipGather([(shards["w_in"], False)]))
    w_main_t, w_fa_t = _w_in_to_kernel(g_in.reshape(IN_COLS, D_MODEL))
    w_in_all_t = jnp.concatenate([w_main_t, w_fa_t], axis=0)
    proj = _matmul_nt(h1, w_main_t, name="in_proj", out_dtype=BF16)
    qk, fa = _qk_prep_fwd(proj, h1, w_fa_t, b_pad, gains, rope, seq)
    late = _Exchange([(shards[n], False) for n in ("w_out", "w_gate", "w_up", "w_down")])
    o, lse, g_out_w, g_gate, g_up, g_down = _attn_fwd(qk, proj, tables_qk, seq, hosted=late)
    w_out = g_out_w.reshape(D_MODEL, D_MODEL)
    w_gate_t = g_gate.reshape(D_FF, D_MODEL)
    w_up_t = g_up.reshape(D_FF, D_MODEL)
    w_down = g_down.reshape(D_FF, D_MODEL)
    on, x2, h2 = _out_proj_ffn_norm(o, g_out, w_out, x16, row(small["g_ffn"]))
    a, u, f = _ffn_gate_up(h2, w_gate_t, w_up_t)
    dy16, loss_tile = _ffn_down_loss(f, w_down, x2, tf)

    da, du = _ffn_down_grad(dy16, w_down, a, u)
    gw_down = _matmul_tn(f, dy16, name="gw_down")
    gw_gate_t = _matmul_tn(da, h2, name="gw_gate")
    gw_up_t = _matmul_tn(du, h2, name="gw_up")
    dh2_gate = _matmul_rows(da, w_gate_t, name="d_ffn_gate", out_dtype=BF16)
    dx2_16, dg_ffn = _norm_input_grad([(du, w_up_t, True)], x2, row(small["g_ffn"]), group=D_MODEL,
                                      name="d_ffn_up", out_dtypes=(BF16,), resid=dy16, init=dh2_gate)
    gw_out = _matmul_tn(on, dx2_16, name="gw_out")
    do, dg_out = _norm_input_grad([(dx2_16, w_out, False)], o, g_out, group=W_GROUP, name="d_out_proj",
                                  out_dtypes=(BF16,))

    shard_rows = lambda g: g.reshape(N_DEV, g.shape[0] // N_DEV, g.shape[1])
    ffn_grads = _Exchange([(shard_rows(g), True) for g in (gw_out, gw_gate_t, gw_up_t, gw_down)])
    dq, dk, dv, dqx, dkx, p_out, p_gate, p_up, p_down = _attn_bwd(qk, proj, tables_kq, o, lse, do, seq, hosted=ffn_grads)
    dproj, dgains, db = _qk_prep_bwd(dq, dk, dqx, dkx, dv, proj, fa, b_pad, gains, rope, seq)
    gw_in_t = _matmul_tn(dproj, h1, name="gw_in")
    in_grad = _Exchange([(shard_rows(_w_in_from_kernel(gw_in_t)), True)])
    dx, dg_mix, p_in = _norm_input_grad([(dproj, w_in_all_t, True)], x16, row(small["g_mix"]), group=D_MODEL,
                                        name="d_in_proj", out_dtypes=(F32,), resid=dx2_16, hosted=in_grad)

    fold = lambda rows: jnp.sum(rows[:, :HEAD_DIM] + rows[:, HEAD_DIM:], axis=0)
    half = N_PAIRS // 2
    gsmall = {
        "g_mix": dg_mix, "g_ffn": dg_ffn, "g_out_fox": dg_out[0, :W_GROUP], "g_out_dil": dg_out[0, W_GROUP:],
        "g_q_fox": fold(dgains[0:half]), "g_q_dil": fold(dgains[half:N_PAIRS]),
        "g_k_fox": fold(dgains[N_PAIRS:N_PAIRS + half]), "g_k_dil": fold(dgains[N_PAIRS + half:]),
        "b_forget": db[0, :N_HEADS_FOX],
    }
    packed = _pack_small(gsmall).at[LOSS_ROW].set(loss_tile[0])
    (p_small,) = _exchange("small_exchange", [(packed, False)])
    parts = {"w_in": p_in, "w_out": p_out, "w_gate": p_gate, "w_up": p_up, "w_down": p_down}
    return dx.reshape(x.shape), parts, p_small


def kernel(x, g_mix, w_in, b_forget, g_q_fox, g_k_fox, g_q_dil, g_k_dil, g_out_fox, g_out_dil, w_out, g_ffn, w_gate, w_up, w_down, loss_target, m_g_mix, m_w_in, m_b_forget, m_g_q_fox, m_g_k_fox, m_g_q_dil, m_g_k_dil, m_g_out_fox, m_g_out_dil, m_w_out, m_g_ffn, m_w_gate, m_w_up, m_w_down, v_g_mix, v_w_in, v_b_forget, v_g_q_fox, v_g_k_fox, v_g_q_dil, v_g_k_dil, v_g_out_fox, v_g_out_dil, v_w_out, v_g_ffn, v_w_gate, v_w_up, v_w_down):
    args = dict(locals())
    small_names = [name for name, _, _ in SMALL_LAYOUT]
    big_names = ["w_in", "w_out", "w_gate", "w_up", "w_down"]
    small = {n: args[n][0] for n in small_names}

    as_rows = lambda n, w: _swap(w) if n in COL_SHARDED else w
    shards = {n: as_rows(n, args[n])[0].astype(BF16) for n in big_names}
    grad_x, parts, p_small = _device_step(x, loss_target, small, shards)

    grads, deltas, new_m, new_v = {}, {}, {}, {}
    for n in big_names:
        res = _adamw(parts[n], as_rows(n, args[n]), as_rows(n, args["m_" + n]), as_rows(n, args["v_" + n]),
                     name="adamw_" + n)
        grads[n], deltas[n], new_m[n], new_v[n] = [as_rows(n, r) for r in res]
    res = _adamw(p_small, _pack_small(small)[None], _pack_small({n: args["m_" + n][0] for n in small_names})[None],
                 _pack_small({n: args["v_" + n][0] for n in small_names})[None], name="adamw_small")
    loss = res[0][0, LOSS_ROW, 0]
    for dst, packed_res in zip((grads, deltas, new_m, new_v), res):
        for n, val in _unpack_small(packed_res[0], small).items():
            dst[n] = val[None]

    order = ["g_mix", "w_in", "b_forget", "g_q_fox", "g_k_fox", "g_q_dil", "g_k_dil", "g_out_fox", "g_out_dil",
             "w_out", "g_ffn", "w_gate", "w_up", "w_down"]
    return (loss, grad_x, *[grads[n] for n in order], *[deltas[n] for n in order],
            *[new_m[n] for n in order], *[new_v[n] for n in order])
```

```python
import functools
import math

import jax
import jax.numpy as jnp
import numpy as np
from jax import lax
from jax.experimental import pallas as pl
from jax.experimental.pallas import tpu as pltpu

F32 = jnp.float32
BF16 = jnp.bfloat16

D_MODEL = 1024
HEAD_DIM = 64
LANES = 128
N_PAIRS = D_MODEL // LANES
N_HEADS = 2 * N_PAIRS
N_HEADS_FOX = 8
W_GROUP = 512
D_FF = 2816
IN_COLS = 3080
DILATION_PAIRS = ((128, 1), (512, 4), (2048, 16))
ROPE_THETA = 500000.0
ROPE_DIM = 16
ROPE_HALF = ROPE_DIM // 2
EPS = 1e-6
NEG = -1e30
LOG2E = 1.4426950408889634
LN2 = 0.6931471805599453
AUG_ONE = 0
AUG_C = 3
N_DEV = 8

ADAM_LR = 0.001
ADAM_B1 = 0.9
ADAM_B2 = 0.999
ADAM_EPS = 1e-08
ADAM_WD = 0.01
ADAM_STEP = 10

ROW_BLOCK = 512
TOKEN_STEP = 2048
WIDE_BLOCK = D_FF // 2
ATT_BLOCK = 512
ATT_GROUP = 4
VMEM_LIMIT = 56 * 1024 * 1024
MATMUL_VMEM_BUDGET = 44 * 1024 * 1024

SMALL_ROWS = 32
SMALL_LAYOUT = (("g_mix", 0, 8), ("g_ffn", 8, 8), ("g_out_fox", 16, 4), ("g_out_dil", 20, 4),
                ("g_q_fox", 24, 1), ("g_k_fox", 25, 1), ("g_q_dil", 26, 1), ("g_k_dil", 27, 1),
                ("b_forget", 28, 1))
LOSS_ROW = 29


def _params(n_grid):
    return pltpu.CompilerParams(dimension_semantics=("arbitrary",) * n_grid, vmem_limit_bytes=VMEM_LIMIT)


def _divisor_block(n, cap):
    best = None
    for b in range(LANES, min(n, cap) + 1, LANES):
        if n % b == 0:
            best = b
    assert best is not None, n
    return best


def _split_dot(a, b_exact, terms):
    acc = None
    rest = a
    for _ in range(terms):
        hi = rest.astype(BF16)
        part = jnp.dot(hi, b_exact, preferred_element_type=F32)
        acc = part if acc is None else acc + part
        rest = rest - hi.astype(F32)
    return acc


def _split_dot_nt(a_exact, b, terms):
    acc = None
    rest = b
    for _ in range(terms):
        hi = rest.astype(BF16)
        part = _dot_nt(a_exact, hi)
        acc = part if acc is None else acc + part
        rest = rest - hi.astype(F32)
    return acc


def _dot_nt(a, b):
    return lax.dot_general(a, b, (((1,), (1,)), ((), ())), preferred_element_type=F32)


def _dot_tn(a, b):
    return lax.dot_general(a, b, (((0,), (0,)), ((), ())), preferred_element_type=F32)


class _Exchange:
    def __init__(self, items):
        self.items = items
        self.n = len(items)
        self.arrays = [a for a, _ in items]
        self.out_shape = [jax.ShapeDtypeStruct((N_DEV,) + tuple(a.shape[1:] if sc else a.shape), a.dtype)
                          for a, sc in items]
        self.specs = [pl.BlockSpec(memory_space=pl.ANY)] * self.n
        self.scratch = [pltpu.SemaphoreType.DMA((self.n, N_DEV - 1)), pltpu.SemaphoreType.DMA((self.n, N_DEV - 1)),
                        pltpu.SemaphoreType.DMA((self.n,))]

    def run(self, ins, outs, sems, first, last, compute):
        send_sems, recv_sems, local_sems = sems
        x, y, c = lax.axis_index("x"), lax.axis_index("y"), lax.axis_index("c")
        me = 4 * x + 2 * y + c
        local, remote = [], []
        for k, (_, scatter) in enumerate(self.items):
            own = ins[k].at[me] if scatter else ins[k]
            local.append(pltpu.make_async_copy(own, outs[k].at[me], local_sems.at[k]))
        for r in range(1, N_DEV):
            px = 1 - x if r & 4 else x
            py = 1 - y if r & 2 else y
            pc = 1 - c if r & 1 else c
            peer = 4 * px + 2 * py + pc
            for k, (_, scatter) in enumerate(self.items):
                src = ins[k].at[peer] if scatter else ins[k]
                remote.append(pltpu.make_async_remote_copy(
                    src_ref=src, dst_ref=outs[k].at[me],
                    send_sem=send_sems.at[k, r - 1], recv_sem=recv_sems.at[k, r - 1],
                    device_id=(px, py, pc), device_id_type=pl.DeviceIdType.MESH))

        def start():
            for cp in local + remote:
                cp.start()

        def finish():
            for cp in remote:
                cp.wait_recv()
            for cp in remote:
                cp.wait_send()
            for cp in local:
                cp.wait()

        _run_phases(first, last, start, compute, finish)


def _run_phases(first, last, start, compute, finish):
    if first is None:
        start()
        compute()
        finish()
    else:
        pl.when(first)(start)
        compute()
        pl.when(last)(finish)


class _ChipGather(_Exchange):
    def run(self, ins, outs, sems, first, last, compute):
        send_sems, recv_sems, local_sems = sems
        x, y, c = lax.axis_index("x"), lax.axis_index("y"), lax.axis_index("c")
        sibling = (x, y, 1 - c)
        chips = [(1 - x, y), (x, 1 - y), (1 - x, 1 - y)]
        slot = lambda px, py, pc: 4 * px + 2 * py + pc

        def copy(k, n, src, dst_slot, to):
            return pltpu.make_async_remote_copy(
                src_ref=src, dst_ref=outs[k].at[dst_slot], send_sem=send_sems.at[k, n], recv_sem=recv_sems.at[k, n],
                device_id=to, device_id_type=pl.DeviceIdType.MESH)

        local, own, passed, arrivals = [], [], [], []
        for k in range(self.n):
            me = slot(x, y, c)
            local.append(pltpu.make_async_copy(ins[k], outs[k].at[me], local_sems.at[k]))
            own.append(copy(k, 0, ins[k], me, sibling))
            arrivals.append(copy(k, 0, ins[k], slot(*sibling), sibling))
            for j, chip in enumerate(chips):
                theirs = slot(*chip, c)
                own.append(copy(k, 1 + j, ins[k], me, (*chip, c)))
                passed.append((copy(k, 1 + j, ins[k], theirs, sibling),
                               copy(k, 4 + j, outs[k].at[theirs], theirs, sibling)))
                arrivals.append(copy(k, 4 + j, ins[k], slot(*chip, 1 - c), sibling))

        def start():
            for cp in local + own:
                cp.start()

        def finish():
            for landed, onward in passed:
                landed.wait_recv()
                onward.start()
            for cp in arrivals:
                cp.wait_recv()
            for cp in own + [onward for _, onward in passed]:
                cp.wait_send()
            for cp in local:
                cp.wait()

        _run_phases(first, last, start, compute, finish)


def _grid_ends(grid):
    ids = [pl.program_id(d) for d in range(len(grid))]
    first = functools.reduce(jnp.logical_and, [i == 0 for i in ids])
    last = functools.reduce(jnp.logical_and, [i == g - 1 for i, g in zip(ids, grid)])
    return first, last


def _host(core, n_in, n_out, n_scratch, hosted, grid):
    if hosted is None:
        return core
    nh = hosted.n

    def body(*refs):
        ins, rest = refs[:n_in], refs[n_in:]
        h_ins, rest = rest[:nh], rest[nh:]
        outs, rest = rest[:n_out], rest[n_out:]
        h_outs, rest = rest[:nh], rest[nh:]
        scratch, sems = rest[:n_scratch], rest[n_scratch:]
        first, last = _grid_ends(grid)
        hosted.run(h_ins, h_outs, sems, first, last, lambda: core(*ins, *outs, *scratch))

    return body


def _hosted_parts(hosted):
    if hosted is None:
        return [], [], [], []
    return list(hosted.specs), list(hosted.out_shape), list(hosted.arrays), list(hosted.scratch)


def _exchange(name, items):
    ex = _Exchange(items)
    n = ex.n

    def body(*refs):
        ex.run(refs[:n], refs[n:2 * n], refs[2 * n:], None, None, lambda: None)

    return pl.pallas_call(
        body, name=name, out_shape=tuple(ex.out_shape), in_specs=ex.specs, out_specs=tuple(ex.specs),
        scratch_shapes=ex.scratch,
    )(*ex.arrays)


def _matmul_blocks(t, k, n, a_bytes, o_bytes):
    for bt, cap in ((_row_block(t), WIDE_BLOCK), (_row_block(t), ROW_BLOCK), (ROW_BLOCK, ROW_BLOCK)):
        bn = _divisor_block(n, cap)
        if 2 * (bt * k * a_bytes + bn * k * 2 + bt * bn * o_bytes) <= MATMUL_VMEM_BUDGET:
            return bt, bn
    return ROW_BLOCK, _divisor_block(n, 2 * LANES)


def _matmul_nt(a, w, *, name, out_dtype):
    t, k = a.shape
    n = w.shape[0]
    assert w.shape[1] == k
    bt, bn = _matmul_blocks(t, k, n, a.dtype.itemsize, jnp.dtype(out_dtype).itemsize)

    def body(a_ref, w_ref, o_ref):
        o_ref[...] = _dot_nt(a_ref[...], w_ref[...]).astype(o_ref.dtype)

    return pl.pallas_call(
        body, name=name, grid=(t // bt, n // bn),
        in_specs=[pl.BlockSpec((bt, k), lambda i, j: (i, 0)), pl.BlockSpec((bn, k), lambda i, j: (j, 0))],
        out_specs=pl.BlockSpec((bt, bn), lambda i, j: (i, j)),
        out_shape=jax.ShapeDtypeStruct((t, n), out_dtype), compiler_params=_params(2),
    )(a, w)


def _matmul_rows(a, w, *, name, out_dtype=F32):
    t, k = a.shape
    n = w.shape[1]
    assert w.shape[0] == k
    bt = _row_block(t)

    def body(a_ref, w_ref, o_ref):
        o_ref[...] = jnp.dot(a_ref[...], w_ref[...], preferred_element_type=F32).astype(o_ref.dtype)

    return pl.pallas_call(
        body, name=name, grid=(t // bt,),
        in_specs=[pl.BlockSpec((bt, k), lambda i: (i, 0)), pl.BlockSpec((k, n), lambda i: (0, 0))],
        out_specs=pl.BlockSpec((bt, n), lambda i: (i, 0)),
        out_shape=jax.ShapeDtypeStruct((t, n), out_dtype), compiler_params=_params(1),
    )(a, w)


def _matmul_tn(a, b, *, name):
    t, m = a.shape
    n = b.shape[1]
    bt = TOKEN_STEP if t % TOKEN_STEP == 0 else ROW_BLOCK
    bm = _divisor_block(m, WIDE_BLOCK)
    bn = _divisor_block(n, WIDE_BLOCK)
    steps = t // bt

    def body(a_ref, b_ref, o_ref, acc):
        step = pl.program_id(2)

        @pl.when(step == 0)
        def _():
            acc[...] = jnp.zeros_like(acc)

        acc[...] += _dot_tn(a_ref[...], b_ref[...])

        @pl.when(step == steps - 1)
        def _():
            o_ref[...] = acc[...].astype(o_ref.dtype)

    return pl.pallas_call(
        body, name=name, grid=(m // bm, n // bn, steps),
        in_specs=[pl.BlockSpec((bt, bm), lambda i, j, s: (s, i)), pl.BlockSpec((bt, bn), lambda i, j, s: (s, j))],
        out_specs=pl.BlockSpec((bm, bn), lambda i, j, s: (i, j)),
        out_shape=jax.ShapeDtypeStruct((m, n), BF16), scratch_shapes=[pltpu.VMEM((bm, bn), F32)],
        compiler_params=_params(3),
    )(a, b)


def _rmsnorm_fwd(x, g, *, group, name, hosted=None):
    t, w = x.shape
    bt = ROW_BLOCK

    def body(x_ref, g_ref, o_ref, x16_ref):
        x16_ref[...] = x_ref[...].astype(BF16)
        for s in range(0, w, group):
            xs = x_ref[:, s:s + group].astype(F32)
            r = lax.rsqrt(jnp.mean(xs * xs, axis=-1, keepdims=True) + EPS)
            o_ref[:, s:s + group] = (xs * r * g_ref[:, s:s + group]).astype(o_ref.dtype)

    grid = (t // bt,)
    h_specs, h_shapes, h_args, h_scratch = _hosted_parts(hosted)
    rows = pl.BlockSpec((bt, w), lambda i: (i, 0))
    return pl.pallas_call(
        _host(body, 2, 2, 0, hosted, grid), name=name, grid=grid,
        in_specs=[rows, pl.BlockSpec((1, w), lambda i: (0, 0))] + h_specs,
        out_specs=tuple([rows, rows] + h_specs),
        out_shape=tuple([jax.ShapeDtypeStruct((t, w), BF16)] * 2 + h_shapes),
        scratch_shapes=h_scratch, compiler_params=_params(1),
    )(x, g, *h_args)


def _norm_input_grad(terms, x, g, *, group, name, out_dtypes, resid=None, init=None, hosted=None, k_chunks=1):
    t, w = x.shape
    n_terms = len(terms)
    kc = [a.shape[1] // k_chunks for a, _, _ in terms]
    per_row = sum(c * a.dtype.itemsize for c, (a, _, _) in zip(kc, terms))
    per_row += w * sum(r.dtype.itemsize for r in (x, resid, init) if r is not None)
    per_row += w * sum(jnp.dtype(dt).itemsize for dt in out_dtypes)
    fixed = 2 * sum(w * c * 2 for c in kc)
    bt = next(b for b in (2 * ROW_BLOCK, ROW_BLOCK, ROW_BLOCK // 2, ROW_BLOCK // 4)
              if t % b == 0 and fixed + 2 * b * per_row + 5 * b * w * 4 <= MATMUL_VMEM_BUDGET)
    resid_at = 2 * n_terms + 2
    init_at = resid_at + (resid is not None)
    n_in = init_at + (init is not None)
    grid = (t // bt, k_chunks)

    def body(*refs):
        x_ref, g_ref = refs[2 * n_terms], refs[2 * n_terms + 1]
        dx_refs, dg_ref, dh_ref = refs[n_in:-2], refs[-2], refs[-1]
        chunk = pl.program_id(1)

        @pl.when((pl.program_id(0) == 0) & (chunk == 0))
        def _():
            dg_ref[...] = jnp.zeros_like(dg_ref)

        part = None
        for k in range(n_terms):
            if terms[k][2]:
                term = jnp.dot(refs[2 * k][...], refs[2 * k + 1][...], preferred_element_type=F32)
            else:
                term = _dot_nt(refs[2 * k][...], refs[2 * k + 1][...])
            part = term if part is None else part + term

        @pl.when(chunk == 0)
        def _():
            dh_ref[...] = part if init is None else refs[init_at][...] + part

        @pl.when(chunk > 0)
        def _():
            dh_ref[...] += part

        @pl.when(chunk == k_chunks - 1)
        def _():
            for s in range(0, w, group):
                xs = x_ref[:, s:s + group].astype(F32)
                dhs = dh_ref[:, s:s + group]
                r = lax.rsqrt(jnp.mean(xs * xs, axis=-1, keepdims=True) + EPS)
                xh = xs * r
                dg_ref[:, s:s + group] += jnp.sum(dhs * xh, axis=0, keepdims=True)
                dxh = dhs * g_ref[:, s:s + group]
                dx = r * (dxh - xh * jnp.mean(dxh * xh, axis=-1, keepdims=True))
                if resid is not None:
                    dx = refs[resid_at][:, s:s + group] + dx
                for dx_ref in dx_refs:
                    dx_ref[:, s:s + group] = dx.astype(dx_ref.dtype)

    row = pl.BlockSpec((bt, w), lambda i, k: (i, 0))
    vec = pl.BlockSpec((1, w), lambda i, k: (0, 0))
    in_specs, args = [], []
    for c, (a, wt, w_is_kn) in zip(kc, terms):
        assert wt.shape == ((a.shape[1], w) if w_is_kn else (w, a.shape[1]))
        w_spec = pl.BlockSpec((c, w), lambda i, k: (k, 0)) if w_is_kn else pl.BlockSpec((w, c), lambda i, k: (0, k))
        in_specs += [pl.BlockSpec((bt, c), lambda i, k: (i, k)), w_spec]
        args += [a, wt]
    extra = [r for r in (resid, init) if r is not None]
    in_specs += [row, vec] + [row] * len(extra)
    args += [x, g] + extra
    h_specs, h_shapes, h_args, h_scratch = _hosted_parts(hosted)
    return pl.pallas_call(
        _host(body, n_in, len(out_dtypes) + 1, 1, hosted, grid), name=name, grid=grid, in_specs=in_specs + h_specs,
        out_specs=tuple([row] * len(out_dtypes) + [vec] + h_specs),
        out_shape=tuple([jax.ShapeDtypeStruct((t, w), dt) for dt in out_dtypes] + [jax.ShapeDtypeStruct((1, w), F32)]
                        + h_shapes),
        scratch_shapes=[pltpu.VMEM((bt, w), F32)] + h_scratch, compiler_params=_params(2),
    )(*args, *h_args)


def _tile_plan(tile):
    is_q = tile < N_PAIRS
    is_dil = (tile % N_PAIRS) >= N_PAIRS // 2
    return is_q, is_dil, (0 if is_q else 2) + (1 if is_dil else 0)


def _proj_tile(kind, pair):
    return kind * N_PAIRS + pair


def _segment_ones():
    lane = np.arange(LANES)
    return jnp.asarray((lane[:, None] // HEAD_DIM) == (lane[None, :] // HEAD_DIM), BF16)


def _rope_tables(seq):
    inv_freq = jnp.power(jnp.float32(ROPE_THETA), -jnp.arange(ROPE_HALF, dtype=F32) * 2.0 / ROPE_DIM)
    ang = jnp.arange(seq).astype(F32)[:, None] * inv_freq[None, :]
    cos, sin = jnp.cos(ang), jnp.sin(ang)
    ones = jnp.ones((seq, HEAD_DIM - ROPE_DIM), F32)
    zeros = jnp.zeros((seq, HEAD_DIM - ROPE_DIM), F32)
    zh = jnp.zeros((seq, ROPE_HALF), F32)
    cos_t = jnp.concatenate([cos, cos, ones], axis=1)
    sin_a = jnp.concatenate([-sin, zh, zeros], axis=1)
    sin_b = jnp.concatenate([zh, sin, zeros], axis=1)
    return tuple(jnp.tile(tab, (1, 2)) for tab in (cos_t, sin_a, sin_b))


def _log_sigmoid(z):
    return jnp.minimum(z, 0.0) - jnp.log1p(jnp.exp(-jnp.abs(z)))


def _aug_placement():
    place = np.zeros((N_PAIRS, LANES, LANES), np.float32)
    for is_k in range(2):
        for pair in range(N_PAIRS // 2):
            for e in range(2):
                other = HEAD_DIM * (1 - e)
                ones_at = other + (AUG_C if is_k else AUG_ONE)
                c_at = other + (AUG_ONE if is_k else AUG_C)
                for n in range(3):
                    place[4 * is_k + pair, N_HEADS_FOX * n + 2 * pair + e, c_at + n] = -1.0 if is_k else 1.0
                    place[4 * is_k + pair, 3 * N_HEADS_FOX, ones_at + n] = 1.0
    return jnp.asarray(place, BF16)


def _qk_prep_fwd(proj, h1, w_fa_t, b_pad, gains, rope, seq):
    t = proj.shape[0]
    bt = ROW_BLOCK
    nsb = seq // bt
    seg = _segment_ones()
    rr = np.arange(bt)
    tri = jnp.asarray(rr[:, None] <= rr[None, :], BF16)

    def body(p_ref, h_ref, wfa_ref, b_ref, g_ref, cos_ref, sa_ref, sb_ref, seg_ref, tri_ref, place_ref,
             qk_ref, fa_ref, carry):
        @pl.when(pl.program_id(0) % nsb == 0)
        def _():
            carry[...] = jnp.zeros_like(carry)

        lane = lax.broadcasted_iota(jnp.int32, (bt, LANES), 1)
        fa = _dot_nt(h_ref[...], wfa_ref[...])
        fa_ref[...] = fa
        logf = jnp.where(lane < N_HEADS_FOX, _log_sigmoid(fa + b_ref[...]), 0.0)
        c_rows = _split_dot(logf.T[0:N_HEADS_FOX, :], tri_ref[...], 3) + carry[:, 0:1]
        carry[...] = jnp.broadcast_to(c_rows[:, bt - 1:bt], carry.shape)
        cblk = jnp.concatenate([c_rows, jnp.zeros((LANES - N_HEADS_FOX, bt), F32)], axis=0).T
        packed = jnp.where(lane == 3 * N_HEADS_FOX, 1.0, 0.0)
        rest = cblk * LOG2E
        for n in range(3):
            term = rest.astype(BF16).astype(F32)
            packed = packed + (pltpu.roll(term, N_HEADS_FOX * n, 1) if n else term)
            rest = rest - term
        packed = packed.astype(BF16)
        low = lane < HEAD_DIM

        for tile in range(2 * N_PAIRS):
            is_q, is_dil, grow = _tile_plan(tile)
            pair = tile % N_PAIRS
            src = _proj_tile(0 if is_q else 1, pair) * LANES
            xs = p_ref[:, src:src + LANES].astype(F32)
            r = lax.rsqrt(_split_dot(xs * xs, seg_ref[...], 2) * (1.0 / HEAD_DIM) + EPS)
            yv = xs * r * g_ref[grow:grow + 1, :]
            if is_dil:
                yv = (yv * cos_ref[...] + pltpu.roll(yv, LANES - ROPE_HALF, 1) * sa_ref[...]
                      + pltpu.roll(yv, ROPE_HALF, 1) * sb_ref[...])
                aug = jnp.zeros((bt, LANES), F32)
            else:
                aug = jnp.dot(packed, place_ref[(0 if is_q else N_PAIRS // 2) + pair], preferred_element_type=F32)
            if is_q:
                yv = yv * (HEAD_DIM ** -0.5 * LOG2E)
            dst = ((0 if is_q else N_HEADS) + 2 * pair) * LANES
            qk_ref[:, dst:dst + LANES] = jnp.where(low, yv, aug).astype(BF16)
            qk_ref[:, dst + LANES:dst + 2 * LANES] = jnp.where(low, aug, yv).astype(BF16)

    row128 = pl.BlockSpec((bt, LANES), lambda i: (i, 0))
    rope_spec = pl.BlockSpec((bt, LANES), lambda i: (i % nsb, 0))
    const = lambda shape: pl.BlockSpec(shape, lambda i: (0,) * len(shape))
    return pl.pallas_call(
        body, name="qk_prep_fwd", grid=(t // bt,),
        in_specs=[pl.BlockSpec((bt, 2 * D_MODEL), lambda i: (i, 0)), pl.BlockSpec((bt, D_MODEL), lambda i: (i, 0)),
                  const((LANES, D_MODEL)), const((1, LANES)), const((8, LANES)), rope_spec, rope_spec, rope_spec,
                  const((LANES, LANES)), const((bt, bt)), const((N_PAIRS, LANES, LANES))],
        out_specs=(pl.BlockSpec((bt, 2 * N_HEADS * LANES), lambda i: (i, 0)), row128),
        out_shape=(jax.ShapeDtypeStruct((t, 2 * N_HEADS * LANES), BF16), jax.ShapeDtypeStruct((t, LANES), F32)),
        scratch_shapes=[pltpu.VMEM((8, LANES), F32)], compiler_params=_params(1),
    )(proj, h1, w_fa_t, b_pad, gains, *rope, seg, tri, _aug_placement())


def _qk_prep_bwd(dq, dk, dqx, dkx, dv, proj, fa, b_pad, gains, rope, seq):
    t = proj.shape[0]
    bt = ROW_BLOCK
    nsb = seq // bt
    nblk = t // bt
    seg = _segment_ones()
    rr = np.arange(bt)
    triu = jnp.asarray(rr[:, None] >= rr[None, :], BF16)

    def body(dq_ref, dk_ref, dqx_ref, dkx_ref, dv_ref, p_ref, fa_ref, b_ref, g_ref, cos_ref, sa_ref, sb_ref, seg_ref,
             triu_ref, dp_ref, dg_ref, db_ref, carry):
        step = pl.program_id(0)

        @pl.when(step == 0)
        def _():
            dg_ref[...] = jnp.zeros_like(dg_ref)
            db_ref[...] = jnp.zeros_like(db_ref)

        @pl.when(step % nsb == 0)
        def _():
            carry[...] = jnp.zeros_like(carry)

        for tile in range(2 * N_PAIRS):
            is_q, is_dil, grow = _tile_plan(tile)
            first = _proj_tile(0 if is_q else 1, tile % N_PAIRS) * LANES
            cols = slice(first, first + LANES)
            src = dq_ref if is_q else dk_ref
            half = slice((tile % N_PAIRS) * LANES, (tile % N_PAIRS + 1) * LANES)
            dy = src[:, half].astype(F32)
            dy = dy * (HEAD_DIM ** -0.5 if is_q else LN2)
            if is_dil:
                dy = (dy * cos_ref[...] + pltpu.roll(dy * sa_ref[...], ROPE_HALF, 1)
                      + pltpu.roll(dy * sb_ref[...], LANES - ROPE_HALF, 1))
            xs = p_ref[:, cols].astype(F32)
            r = lax.rsqrt(_split_dot(xs * xs, seg_ref[...], 2) * (1.0 / HEAD_DIM) + EPS)
            xh = xs * r
            dg_ref[tile:tile + 1, :] += jnp.sum(dy * xh, axis=0, keepdims=True)
            dxh = dy * g_ref[grow:grow + 1, :]
            seg_mean = _split_dot(dxh * xh, seg_ref[...], 2) * (1.0 / HEAD_DIM)
            dp_ref[:, cols] = (r * (dxh - xh * seg_mean)).astype(BF16)

        lane = lax.broadcasted_iota(jnp.int32, (bt, LANES), 1)
        dc = jnp.zeros((bt, LANES), F32)
        for h in range(N_HEADS_FOX):
            other = (h // 2) * LANES + HEAD_DIM * (1 - h % 2)
            row_sum = dqx_ref[:, other + AUG_C:other + AUG_C + 1]
            col_sum = dkx_ref[:, other + AUG_ONE:other + AUG_ONE + 1]
            dc = jnp.where(lane == h, row_sum - col_sum, dc)
        d_rows = _split_dot(dc.T[0:N_HEADS_FOX, :], triu_ref[...], 3) + carry[:, 0:1]
        carry[...] = jnp.broadcast_to(d_rows[:, 0:1], carry.shape)
        dlogf = jnp.concatenate([d_rows, jnp.zeros((LANES - N_HEADS_FOX, bt), F32)], axis=0).T
        z = fa_ref[...] + b_ref[...]
        dfa = dlogf * (1.0 / (1.0 + jnp.exp(z)))
        db_ref[0:1, :] += jnp.sum(dfa, axis=0, keepdims=True)
        for group in range(2):
            first = _proj_tile(2, group * (N_PAIRS // 2)) * LANES
            dp_ref[:, first:first + W_GROUP] = dv_ref[:, group * W_GROUP:(group + 1) * W_GROUP]
        dp_ref[:, MAIN_COLS:PROJ_COLS] = dfa.astype(BF16)

    rev = lambda i: nblk - 1 - i
    row = lambda w: pl.BlockSpec((bt, w), lambda i: (rev(i), 0))
    rope_spec = pl.BlockSpec((bt, LANES), lambda i: (rev(i) % nsb, 0))
    const = lambda shape: pl.BlockSpec(shape, lambda i: (0, 0))
    return pl.pallas_call(
        body, name="qk_prep_bwd", grid=(nblk,),
        in_specs=[row(D_MODEL), row(D_MODEL), row(W_GROUP), row(W_GROUP), row(D_MODEL), row(2 * D_MODEL), row(LANES),
                  const((1, LANES)), const((8, LANES)), rope_spec, rope_spec, rope_spec, const((LANES, LANES)),
                  const((bt, bt))],
        out_specs=(row(PROJ_COLS), const((2 * N_PAIRS, LANES)), const((8, LANES))),
        out_shape=(jax.ShapeDtypeStruct((t, PROJ_COLS), BF16),
                   jax.ShapeDtypeStruct((2 * N_PAIRS, LANES), F32), jax.ShapeDtypeStruct((8, LANES), F32)),
        scratch_shapes=[pltpu.VMEM((8, LANES), F32)], compiler_params=_params(1),
    )(dq, dk, dqx, dkx, dv, proj, fa, b_pad, gains, *rope, seg, triu)


def _bias_tables(seq, keys_first):
    nb = seq // ATT_BLOCK
    idx = np.arange(ATT_BLOCK)
    q_idx, k_idx = (idx[None, None, :], idx[None, :, None]) if keys_first else (idx[None, :, None], idx[None, None, :])
    dist = np.arange(nb)[:, None, None] * ATT_BLOCK + q_idx - k_idx
    causal = dist >= 0
    count = np.zeros(dist.shape, np.int32)
    for window, dilation in DILATION_PAIRS:
        count = count + (causal & (dist % dilation == 0) & (dist <= window))
    fox = np.where(causal, 0.0, NEG)
    dil = np.where(count == 3, math.log2(3.0), np.where(count == 2, 1.0, np.where(count == 1, 0.0, NEG)))
    return jnp.asarray(np.stack([fox, dil], axis=0), F32)


def _attn_specs(seq):
    nb = seq // ATT_BLOCK
    col = pl.BlockSpec((seq, LANES), lambda b, j: (b, j))
    heads = lambda off: pl.BlockSpec((seq, 2 * LANES), lambda b, j: (b, off + j))
    v_spec = pl.BlockSpec((seq, LANES), lambda b, j: (b, _proj_tile(2, j)))
    table_spec = pl.BlockSpec((1, nb, ATT_BLOCK, ATT_BLOCK), lambda b, j: (j // (N_PAIRS // 2), 0, 0, 0))
    return col, heads, v_spec, table_spec


def _head_lanes(e, shape, axis):
    pos = lax.broadcasted_iota(jnp.int32, shape, axis)
    return pos < HEAD_DIM if e == 0 else pos >= HEAD_DIM


def _attn_fwd(qk, proj, tables, seq, hosted=None):
    t = qk.shape[0]
    nb = seq // ATT_BLOCK
    blk = ATT_BLOCK

    def body(q_ref, k_ref, v_ref, tab_ref, o_ref, lse_ref):
        mine = [_head_lanes(e, (seq, LANES), 1) for e in range(2)]
        lane = lax.broadcasted_iota(jnp.int32, (seq, LANES), 1)
        v_aug = [jnp.where(mine[e], v_ref[...], (lane == HEAD_DIM * (1 - e)).astype(BF16)) for e in range(2)]
        def scores(i, e):
            heads_e = slice(e * LANES, (e + 1) * LANES)
            s = _dot_nt(q_ref[i * blk:(i + 1) * blk, heads_e], k_ref[0:(i + 1) * blk, heads_e])
            s = jnp.concatenate([s[:, jj * blk:(jj + 1) * blk] + tab_ref[0, i - jj] for jj in range(i + 1)], axis=1)
            return s, jnp.max(s, axis=1, keepdims=True)

        chains = [(i, e) for i in reversed(range(nb)) for e in range(2)]
        ahead = 2
        pending = [scores(*chain) for chain in chains[:ahead]]
        done = {}
        for n, (i, e) in enumerate(chains):
            s, m = pending.pop(0)
            if n + ahead < len(chains):
                pending.append(scores(*chains[n + ahead]))
            acc = jnp.dot(jnp.exp2(s - m).astype(BF16), v_aug[e][0:(i + 1) * blk], preferred_element_type=F32)
            ones_at = HEAD_DIM * (1 - e)
            l = acc[:, ones_at:ones_at + 1]
            done[e] = (acc / l, m + jnp.log2(l))
            if e == 1:
                rows = slice(i * blk, (i + 1) * blk)
                o_ref[rows, :] = jnp.where(mine[0][rows], done[0][0], done[1][0]).astype(o_ref.dtype)
                lse_ref[rows, :] = jnp.where(mine[0][rows], done[0][1], done[1][1])

    col, heads, v_spec, table_spec = _attn_specs(seq)
    grid = (t // seq, N_PAIRS)
    h_specs, h_shapes, h_args, h_scratch = _hosted_parts(hosted)
    return pl.pallas_call(
        _host(body, 4, 2, 0, hosted, grid), name="attn_fwd", grid=grid,
        in_specs=[heads(0), heads(N_PAIRS), v_spec, table_spec] + h_specs,
        out_specs=tuple([col, col] + h_specs),
        out_shape=tuple([jax.ShapeDtypeStruct((t, D_MODEL), BF16), jax.ShapeDtypeStruct((t, D_MODEL), F32)] + h_shapes),
        scratch_shapes=h_scratch, compiler_params=_params(2),
    )(qk, qk, proj, tables, *h_args)


def _attn_bwd(qk, proj, tables, o, lse, do, seq, hosted=None):
    t = qk.shape[0]
    nb = seq // ATT_BLOCK
    blk = ATT_BLOCK
    group = math.gcd(nb, ATT_GROUP)

    def body(q_ref, k_ref, v_ref, tab_ref, o_ref, lse_ref, do_ref,
             dq_ref, dk_ref, dv_ref, dqx_ref, dkx_ref, dk_acc, dv_acc):
        mine = [_head_lanes(e, (blk, LANES), 1) for e in range(2)]
        top = _head_lanes(0, (LANES, blk), 0)
        head_rows = lax.broadcasted_iota(jnp.int32, (8, LANES), 0)
        head_of_lane = lax.broadcasted_iota(jnp.int32, (8, LANES), 1) // HEAD_DIM
        head_sel = (head_rows == head_of_lane).astype(BF16)
        dk_acc[...] = jnp.zeros_like(dk_acc)
        dv_acc[...] = jnp.zeros_like(dv_acc)

        def block_rows(i):
            return pl.ds(pl.multiple_of(i * blk, blk), blk)

        def q_group(g, _):
            base = g * group
            qs, doe, delta, lse_e = [], [], [], []
            for b in range(group):
                rows = block_rows(base + b)
                qs.append([q_ref[rows, e * LANES:(e + 1) * LANES] for e in range(2)])
                do_blk = do_ref[rows, :]
                doe.append([jnp.where(mine[e], do_blk, jnp.zeros_like(do_blk)) for e in range(2)])
                delta_t = _split_dot_nt(head_sel, do_blk.astype(F32) * o_ref[rows, :].astype(F32), 3)
                lse_t = _split_dot_nt(head_sel, lse_ref[rows, :], 3) * (1.0 / HEAD_DIM)
                delta.append([delta_t[e:e + 1, :] for e in range(2)])
                lse_e.append([lse_t[e:e + 1, :] for e in range(2)])

            def key_block(dq_t, jj, members):
                krows = block_rows(jj)
                v = v_ref[krows, :]
                dq_t = [list(d) for d in dq_t]
                lo, hi = slice(0, blk // 2), slice(blk // 2, blk)
                dv_part = [None, None]
                add = lambda acc, part: part if acc is None else acc + part

                def probs(k_sub, v_sub, keys, queries, b, e, dist):
                    q_sub, do_sub = qs[b][e][queries], doe[b][e][queries]
                    p_t = jnp.exp2(_dot_nt(k_sub, q_sub) + tab_ref[0, dist, keys, queries] - lse_e[b][e][:, queries])
                    ds_t = (p_t * (_dot_nt(v_sub, do_sub) - delta[b][e][:, queries])).astype(BF16)
                    return p_t.astype(BF16), ds_t, q_sub, do_sub, k_sub

                def outputs(tile):
                    p_t, ds_t, q_sub, do_sub, k_sub = tile
                    return (jnp.dot(p_t, do_sub, preferred_element_type=F32),
                            jnp.dot(ds_t, q_sub, preferred_element_type=F32), _dot_tn(k_sub, ds_t))

                for e in range(2):
                    k_e = k_ref[krows, e * LANES:(e + 1) * LANES]
                    dk_part = [None, None]
                    tiles = []
                    for b, dist in members:
                        if isinstance(dist, int) and dist == 0:
                            tiles.append((b, probs(k_e[lo], v[lo], lo, slice(0, blk), b, e, dist),
                                          probs(k_e[hi], v[hi], hi, hi, b, e, dist)))
                        else:
                            tiles.append((b, probs(k_e, v, slice(0, blk), slice(0, blk), b, e, dist), None))
                    for b, first, second in tiles:
                        if second is not None:
                            dv_a, dk_a, dq_a = outputs(first)
                            dv_b, dk_b, dq_b = outputs(second)
                            halves = ((dv_a, dk_a), (dv_b, dk_b))
                            dq = jnp.concatenate([dq_a[:, lo], dq_a[:, hi] + dq_b], axis=1)
                        else:
                            dv_f, dk_f, dq = outputs(first)
                            halves = ((dv_f[lo], dk_f[lo]), (dv_f[hi], dk_f[hi]))
                        for n, (dv_h, dk_h) in enumerate(halves):
                            dv_part[n] = add(dv_part[n], dv_h)
                            dk_part[n] = add(dk_part[n], dk_h)
                        dq_t[b][e] = dq_t[b][e] + dq
                    dk_acc[e, krows, :] += jnp.concatenate(dk_part, axis=0)
                dv_acc[krows, :] += jnp.concatenate(dv_part, axis=0)
                return tuple(tuple(d) for d in dq_t)

            zacc = jnp.zeros((LANES, blk), F32)
            dq_t = tuple((zacc, zacc) for _ in range(group))
            dq_t = lax.fori_loop(
                0, base, lambda jj, st: key_block(st, jj, [(b, base + b - jj) for b in range(group)]), dq_t)
            for a in range(group):
                dq_t = key_block(dq_t, base + a, [(b, b - a) for b in range(a, group)])
            for b in range(group):
                rows = block_rows(base + b)
                dq_ref[rows, :] = jnp.where(top, dq_t[b][0], dq_t[b][1]).T.astype(BF16)
                dqx_ref[rows, :] = jnp.where(top, dq_t[b][1], dq_t[b][0]).T
            return 0

        lax.fori_loop(0, nb // group, q_group, 0)
        lo = _head_lanes(0, (seq, LANES), 1)
        dk_ref[...] = jnp.where(lo, dk_acc[0], dk_acc[1]).astype(BF16)
        dkx_ref[...] = jnp.where(lo, dk_acc[1], dk_acc[0])
        dv_ref[...] = dv_acc[...].astype(dv_ref.dtype)

    col, heads, v_spec, table_spec = _attn_specs(seq)
    grid = (t // seq, N_PAIRS)
    h_specs, h_shapes, h_args, h_scratch = _hosted_parts(hosted)
    f32_out = jax.ShapeDtypeStruct((t, D_MODEL), F32)
    return pl.pallas_call(
        _host(body, 7, 5, 2, hosted, grid), name="attn_bwd", grid=grid,
        in_specs=[heads(0), heads(N_PAIRS), v_spec, table_spec, col, col, col] + h_specs,
        out_specs=tuple([col] * 5 + h_specs),
        out_shape=tuple([jax.ShapeDtypeStruct((t, D_MODEL), BF16)] * 3 + [f32_out, f32_out] + h_shapes),
        scratch_shapes=[pltpu.VMEM((2, seq, LANES), F32), pltpu.VMEM((seq, LANES), F32)] + h_scratch,
        compiler_params=_params(2),
    )(qk, qk, proj, tables, o, lse, do, *h_args)


def _row_block(t):
    return 2 * ROW_BLOCK if t % (2 * ROW_BLOCK) == 0 else ROW_BLOCK


def _out_proj_ffn_norm(o, g_out, w_out, x, g_ffn):
    t = o.shape[0]
    bt = _row_block(t)

    def body(o_ref, go_ref, w_ref, x_ref, gf_ref, on_ref, x2_ref, h2_ref):
        for s in range(0, D_MODEL, W_GROUP):
            os_ = o_ref[:, s:s + W_GROUP].astype(F32)
            r = lax.rsqrt(jnp.mean(os_ * os_, axis=-1, keepdims=True) + EPS)
            on_ref[:, s:s + W_GROUP] = (os_ * r * go_ref[:, s:s + W_GROUP]).astype(BF16)
        x2 = x_ref[...] + jnp.dot(on_ref[...], w_ref[...], preferred_element_type=F32)
        x2_ref[...] = x2.astype(BF16)
        r2 = lax.rsqrt(jnp.mean(x2 * x2, axis=-1, keepdims=True) + EPS)
        h2_ref[...] = (x2 * r2 * gf_ref[...]).astype(BF16)

    row = pl.BlockSpec((bt, D_MODEL), lambda i: (i, 0))
    vec = pl.BlockSpec((1, D_MODEL), lambda i: (0, 0))
    return pl.pallas_call(
        body, name="out_proj", grid=(t // bt,),
        in_specs=[row, vec, pl.BlockSpec((D_MODEL, D_MODEL), lambda i: (0, 0)), row, vec],
        out_specs=(row, row, row),
        out_shape=tuple([jax.ShapeDtypeStruct((t, D_MODEL), BF16)] * 3),
        compiler_params=_params(1),
    )(o, g_out, w_out, x, g_ffn)


def _ffn_gate_up(h2, w_gate_t, w_up_t):
    t = h2.shape[0]
    bt = _row_block(t)
    bn = _divisor_block(D_FF, WIDE_BLOCK)

    def body(h_ref, wg_ref, wu_ref, a_ref, u_ref, f_ref):
        a = _dot_nt(h_ref[...], wg_ref[...])
        u = _dot_nt(h_ref[...], wu_ref[...])
        a_ref[...] = a.astype(BF16)
        u_ref[...] = u.astype(BF16)
        f_ref[...] = (a * jax.nn.sigmoid(a) * u).astype(BF16)

    blk = pl.BlockSpec((bt, bn), lambda j, i: (i, j))
    w_blk = pl.BlockSpec((bn, D_MODEL), lambda j, i: (j, 0))
    shape = jax.ShapeDtypeStruct((t, D_FF), BF16)
    return pl.pallas_call(
        body, name="ffn_gate_up", grid=(D_FF // bn, t // bt),
        in_specs=[pl.BlockSpec((bt, D_MODEL), lambda j, i: (i, 0)), w_blk, w_blk],
        out_specs=(blk, blk, blk), out_shape=(shape, shape, shape), compiler_params=_params(2),
    )(h2, w_gate_t, w_up_t)


def _ffn_down_grad(dy16, w_down, a, u):
    t = a.shape[0]
    bt = _row_block(t)
    bn = _divisor_block(D_FF, WIDE_BLOCK)

    def body(dy_ref, w_ref, a_ref, u_ref, da_ref, du_ref):
        df = _dot_nt(dy_ref[...], w_ref[...])
        av = a_ref[...].astype(F32)
        sg = jax.nn.sigmoid(av)
        da_ref[...] = (df * u_ref[...].astype(F32) * sg * (1.0 + av * (1.0 - sg))).astype(BF16)
        du_ref[...] = (df * av * sg).astype(BF16)

    blk = pl.BlockSpec((bt, bn), lambda j, i: (i, j))
    shape = jax.ShapeDtypeStruct((t, D_FF), BF16)
    return pl.pallas_call(
        body, name="d_ffn_down", grid=(D_FF // bn, t // bt),
        in_specs=[pl.BlockSpec((bt, D_MODEL), lambda j, i: (i, 0)), pl.BlockSpec((bn, D_MODEL), lambda j, i: (j, 0)),
                  blk, blk],
        out_specs=(blk, blk), out_shape=(shape, shape), compiler_params=_params(2),
    )(dy16, w_down, a, u)


def _ffn_down_loss(f, w_down, x2, target):
    t, w = x2.shape
    bt = _row_block(t)

    def body(f_ref, w_ref, x_ref, t_ref, dy16_ref, loss_ref):
        @pl.when(pl.program_id(0) == 0)
        def _():
            loss_ref[...] = jnp.zeros_like(loss_ref)

        err = (x_ref[...] + jnp.dot(f_ref[...], w_ref[...], preferred_element_type=F32)) - t_ref[...]
        dy16_ref[...] = (err * (1.0 / w)).astype(BF16)
        loss_ref[...] += 0.5 * jnp.sum(jnp.mean(err * err, axis=-1, keepdims=True), axis=0, keepdims=True)

    row = pl.BlockSpec((bt, w), lambda i: (i, 0))
    return pl.pallas_call(
        body, name="ffn_down_loss", grid=(t // bt,),
        in_specs=[pl.BlockSpec((bt, D_FF), lambda i: (i, 0)), pl.BlockSpec((D_FF, w), lambda i: (0, 0)), row, row],
        out_specs=(row, pl.BlockSpec((8, LANES), lambda i: (0, 0))),
        out_shape=(jax.ShapeDtypeStruct((t, w), BF16), jax.ShapeDtypeStruct((8, LANES), F32)),
        compiler_params=_params(1),
    )(f, w_down, x2, target)


def _adamw(parts, w, m, v, *, name):
    _, rows, cols = w.shape
    br = rows if rows <= 512 else 256
    assert rows % br == 0

    def body(p_ref, w_ref, m_ref, v_ref, g_ref, d_ref, nm_ref, nv_ref):
        g = p_ref[0].astype(F32)
        for r in range(1, N_DEV):
            g = g + p_ref[r].astype(F32)
        m2 = ADAM_B1 * m_ref[0] + (1.0 - ADAM_B1) * g
        v2 = ADAM_B2 * v_ref[0] + (1.0 - ADAM_B2) * jnp.square(g)
        m_hat = m2 / (1.0 - ADAM_B1 ** ADAM_STEP)
        v_hat = v2 / (1.0 - ADAM_B2 ** ADAM_STEP)
        g_ref[0] = g
        d_ref[0] = -ADAM_LR * (m_hat / (jnp.sqrt(v_hat) + ADAM_EPS) + ADAM_WD * w_ref[0])
        nm_ref[0] = m2
        nv_ref[0] = v2

    blk = pl.BlockSpec((1, br, cols), lambda i: (0, i, 0))
    shape = jax.ShapeDtypeStruct((1, rows, cols), F32)
    return pl.pallas_call(
        body, name=name, grid=(rows // br,),
        in_specs=[pl.BlockSpec((N_DEV, br, cols), lambda i: (0, i, 0)), blk, blk, blk],
        out_specs=(blk, blk, blk, blk), out_shape=(shape, shape, shape, shape), compiler_params=_params(1),
    )(parts, w, m, v)


_QA, _KA, _VA, _FA, _QD, _KD, _VD = (0, 512), (512, 1024), (1024, 1536), (1536, 1544), (1544, 2056), (2056, 2568), (2568, 3080)
_MAIN_ORDER = (_QA, _QD, _KA, _KD, _VA, _VD)
MAIN_COLS = 3 * D_MODEL
PROJ_COLS = MAIN_COLS + LANES
COL_SHARDED = ("w_in", "w_gate", "w_up")


def _swap(w):
    return jnp.transpose(w, (0, 2, 1))


def _w_in_to_kernel(w_t):
    main = jnp.concatenate([w_t[a:b] for a, b in _MAIN_ORDER], axis=0)
    forget = jnp.pad(w_t[_FA[0]:_FA[1]], ((0, LANES - N_HEADS_FOX), (0, 0)))
    return main, forget


def _w_in_from_kernel(g_t):
    pos = {span: i * W_GROUP for i, span in enumerate(_MAIN_ORDER)}
    parts = []
    for span in (_QA, _KA, _VA, _FA, _QD, _KD, _VD):
        if span == _FA:
            parts.append(g_t[MAIN_COLS:MAIN_COLS + N_HEADS_FOX])
        else:
            parts.append(g_t[pos[span]:pos[span] + W_GROUP])
    return jnp.concatenate(parts, axis=0)


def _pack_small(vals):
    rows = []
    for name, _, n_rows in SMALL_LAYOUT:
        flat = vals[name].reshape(-1).astype(F32)
        rows.append(jnp.pad(flat, (0, n_rows * LANES - flat.shape[0])).reshape(n_rows, LANES))
    packed = jnp.concatenate(rows, axis=0)
    return jnp.pad(packed, ((0, SMALL_ROWS - packed.shape[0]), (0, 0)))


def _unpack_small(packed, like):
    out = {}
    for name, row, n_rows in SMALL_LAYOUT:
        n = like[name].size
        out[name] = packed[row:row + n_rows].reshape(-1)[:n].reshape(like[name].shape)
    return out


def _device_step(x, target, small, shards):
    bsz, seq, _ = x.shape
    t = bsz * seq
    xf = x.reshape(t, D_MODEL)
    tf = target.reshape(t, D_MODEL)
    row = lambda v: v.reshape(1, -1)
    g_out = jnp.concatenate([small["g_out_fox"], small["g_out_dil"]]).reshape(1, D_MODEL)
    gains = jnp.concatenate(
        [jnp.tile(small[n].reshape(1, HEAD_DIM), (1, 2)) for n in ("g_q_fox", "g_q_dil", "g_k_fox", "g_k_dil")]
        + [jnp.zeros((4, LANES), F32)], axis=0)
    b_pad = jnp.pad(small["b_forget"].reshape(1, N_HEADS_FOX), ((0, 0), (0, LANES - N_HEADS_FOX)))
    rope = _rope_tables(seq)
    tables_qk = _bias_tables(seq, keys_first=False)
    tables_kq = _bias_tables(seq, keys_first=True)

    h1, x16, g_in = _rmsnorm_fwd(xf, row(small["g_mix"]), group=D_MODEL, name="norm_mix",
                                 hosted=_ChipGather([(shards["w_in"], False)]))
    w_main_t, w_fa_t = _w_in_to_kernel(g_in.reshape(IN_COLS, D_MODEL))
    w_in_all_t = jnp.concatenate([w_main_t, w_fa_t], axis=0)
    proj = _matmul_nt(h1, w_main_t, name="in_proj", out_dtype=BF16)
    qk, fa = _qk_prep_fwd(proj, h1, w_fa_t, b_pad, gains, rope, seq)
    late = _Exchange([(shards[n], False) for n in ("w_out", "w_gate", "w_up", "w_down")])
    o, lse, g_out_w, g_gate, g_up, g_down = _attn_fwd(qk, proj, tables_qk, seq, hosted=late)
    w_out = g_out_w.reshape(D_MODEL, D_MODEL)
    w_gate_t = g_gate.reshape(D_FF, D_MODEL)
    w_up_t = g_up.reshape(D_FF, D_MODEL)
    w_down = g_down.reshape(D_FF, D_MODEL)
    on, x2, h2 = _out_proj_ffn_norm(o, g_out, w_out, x16, row(small["g_ffn"]))
    a, u, f = _ffn_gate_up(h2, w_gate_t, w_up_t)
    dy16, loss_tile = _ffn_down_loss(f, w_down, x2, tf)

    da, du = _ffn_down_grad(dy16, w_down, a, u)
    gw_down = _matmul_tn(f, dy16, name="gw_down")
    gw_gate_t = _matmul_tn(da, h2, name="gw_gate")
    gw_up_t = _matmul_tn(du, h2, name="gw_up")
    dh2_gate = _matmul_rows(da, w_gate_t, name="d_ffn_gate", out_dtype=BF16)
    dx2_16, dg_ffn = _norm_input_grad([(du, w_up_t, True)], x2, row(small["g_ffn"]), group=D_MODEL,
                                      name="d_ffn_up", out_dtypes=(BF16,), resid=dy16, init=dh2_gate)
    gw_out = _matmul_tn(on, dx2_16, name="gw_out")
    do, dg_out = _norm_input_grad([(dx2_16, w_out, False)], o, g_out, group=W_GROUP, name="d_out_proj",
                                  out_dtypes=(BF16,))

    shard_rows = lambda g: g.reshape(N_DEV, g.shape[0] // N_DEV, g.shape[1])
    ffn_grads = _Exchange([(shard_rows(g), True) for g in (gw_out, gw_gate_t, gw_up_t, gw_down)])
    dq, dk, dv, dqx, dkx, p_out, p_gate, p_up, p_down = _attn_bwd(qk, proj, tables_kq, o, lse, do, seq, hosted=ffn_grads)
    dproj, dgains, db = _qk_prep_bwd(dq, dk, dqx, dkx, dv, proj, fa, b_pad, gains, rope, seq)
    gw_in_t = _matmul_tn(dproj, h1, name="gw_in")
    in_grad = _Exchange([(shard_rows(_w_in_from_kernel(gw_in_t)), True)])
    dx, dg_mix, p_in = _norm_input_grad([(dproj, w_in_all_t, True)], x16, row(small["g_mix"]), group=D_MODEL,
                                        name="d_in_proj", out_dtypes=(F32,), resid=dx2_16, hosted=in_grad)

    fold = lambda rows: jnp.sum(rows[:, :HEAD_DIM] + rows[:, HEAD_DIM:], axis=0)
    half = N_PAIRS // 2
    gsmall = {
        "g_mix": dg_mix, "g_ffn": dg_ffn, "g_out_fox": dg_out[0, :W_GROUP], "g_out_dil": dg_out[0, W_GROUP:],
        "g_q_fox": fold(dgains[0:half]), "g_q_dil": fold(dgains[half:N_PAIRS]),
        "g_k_fox": fold(dgains[N_PAIRS:N_PAIRS + half]), "g_k_dil": fold(dgains[N_PAIRS + half:]),
        "b_forget": db[0, :N_HEADS_FOX],
    }
    packed = _pack_small(gsmall).at[LOSS_ROW].set(loss_tile[0])
    (p_small,) = _exchange("small_exchange", [(packed, False)])
    parts = {"w_in": p_in, "w_out": p_out, "w_gate": p_gate, "w_up": p_up, "w_down": p_down}
    return dx.reshape(x.shape), parts, p_small


def kernel(x, g_mix, w_in, b_forget, g_q_fox, g_k_fox, g_q_dil, g_k_dil, g_out_fox, g_out_dil, w_out, g_ffn, w_gate, w_up, w_down, loss_target, m_g_mix, m_w_in, m_b_forget, m_g_q_fox, m_g_k_fox, m_g_q_dil, m_g_k_dil, m_g_out_fox, m_g_out_dil, m_w_out, m_g_ffn, m_w_gate, m_w_up, m_w_down, v_g_mix, v_w_in, v_b_forget, v_g_q_fox, v_g_k_fox, v_g_q_dil, v_g_k_dil, v_g_out_fox, v_g_out_dil, v_w_out, v_g_ffn, v_w_gate, v_w_up, v_w_down):
    args = dict(locals())
    small_names = [name for name, _, _ in SMALL_LAYOUT]
    big_names = ["w_in", "w_out", "w_gate", "w_up", "w_down"]
    small = {n: args[n][0] for n in small_names}

    as_rows = lambda n, w: _swap(w) if n in COL_SHARDED else w
    shards = {n: as_rows(n, args[n])[0].astype(BF16) for n in big_names}
    grad_x, parts, p_small = _device_step(x, loss_target, small, shards)

    grads, deltas, new_m, new_v = {}, {}, {}, {}
    for n in big_names:
        res = _adamw(parts[n], as_rows(n, args[n]), as_rows(n, args["m_" + n]), as_rows(n, args["v_" + n]),
                     name="adamw_" + n)
        grads[n], deltas[n], new_m[n], new_v[n] = [as_rows(n, r) for r in res]
    res = _adamw(p_small, _pack_small(small)[None], _pack_small({n: args["m_" + n][0] for n in small_names})[None],
                 _pack_small({n: args["v_" + n][0] for n in small_names})[None], name="adamw_small")
    loss = res[0][0, LOSS_ROW, 0]
    for dst, packed_res in zip((grads, deltas, new_m, new_v), res):
        for n, val in _unpack_small(packed_res[0], small).items():
            dst[n] = val[None]

    order = ["g_mix", "w_in", "b_forget", "g_q_fox", "g_k_fox", "g_q_dil", "g_k_dil", "g_out_fox", "g_out_dil",
             "w_out", "g_ffn", "w_gate", "w_up", "w_down"]
    return (loss, grad_x, *[grads[n] for n in order], *[deltas[n] for n in order],
            *[new_m[n] for n in order], *[new_v[n] for n in order])
```

```python
import functools
import math

import jax
import jax.numpy as jnp
import numpy as np
from jax import lax
from jax.experimental import pallas as pl
from jax.experimental.pallas import tpu as pltpu

F32 = jnp.float32
BF16 = jnp.bfloat16

D_MODEL = 1024
HEAD_DIM = 64
LANES = 128
N_PAIRS = D_MODEL // LANES
N_HEADS = 2 * N_PAIRS
N_HEADS_FOX = 8
W_GROUP = 512
D_FF = 2816
IN_COLS = 3080
DILATION_PAIRS = ((128, 1), (512, 4), (2048, 16))
ROPE_THETA = 500000.0
ROPE_DIM = 16
ROPE_HALF = ROPE_DIM // 2
EPS = 1e-6
NEG = -1e30
LOG2E = 1.4426950408889634
LN2 = 0.6931471805599453
AUG_ONE = 0
AUG_C = 3
N_DEV = 8

ADAM_LR = 0.001
ADAM_B1 = 0.9
ADAM_B2 = 0.999
ADAM_EPS = 1e-08
ADAM_WD = 0.01
ADAM_STEP = 10

ROW_BLOCK = 512
TOKEN_STEP = 2048
WIDE_BLOCK = D_FF // 2
ATT_BLOCK = 512
ATT_GROUP = 4
VMEM_LIMIT = 56 * 1024 * 1024
MATMUL_VMEM_BUDGET = 44 * 1024 * 1024

SMALL_ROWS = 32
SMALL_LAYOUT = (("g_mix", 0, 8), ("g_ffn", 8, 8), ("g_out_fox", 16, 4), ("g_out_dil", 20, 4),
                ("g_q_fox", 24, 1), ("g_k_fox", 25, 1), ("g_q_dil", 26, 1), ("g_k_dil", 27, 1),
                ("b_forget", 28, 1))
LOSS_ROW = 29


def _params(n_grid):
    return pltpu.CompilerParams(dimension_semantics=("arbitrary",) * n_grid, vmem_limit_bytes=VMEM_LIMIT)


def _divisor_block(n, cap):
    best = None
    for b in range(LANES, min(n, cap) + 1, LANES):
        if n % b == 0:
            best = b
    assert best is not None, n
    return best


def _split_dot(a, b_exact, terms):
    acc = None
    rest = a
    for _ in range(terms):
        hi = rest.astype(BF16)
        part = jnp.dot(hi, b_exact, preferred_element_type=F32)
        acc = part if acc is None else acc + part
        rest = rest - hi.astype(F32)
    return acc


def _split_dot_nt(a_exact, b, terms):
    acc = None
    rest = b
    for _ in range(terms):
        hi = rest.astype(BF16)
        part = _dot_nt(a_exact, hi)
        acc = part if acc is None else acc + part
        rest = rest - hi.astype(F32)
    return acc


def _dot_nt(a, b):
    return lax.dot_general(a, b, (((1,), (1,)), ((), ())), preferred_element_type=F32)


def _dot_tn(a, b):
    return lax.dot_general(a, b, (((0,), (0,)), ((), ())), preferred_element_type=F32)


class _Exchange:
    def __init__(self, items):
        self.items = items
        self.n = len(items)
        self.arrays = [a for a, _ in items]
        self.out_shape = [jax.ShapeDtypeStruct((N_DEV,) + tuple(a.shape[1:] if sc else a.shape), a.dtype)
                          for a, sc in items]
        self.specs = [pl.BlockSpec(memory_space=pl.ANY)] * self.n
        self.scratch = [pltpu.SemaphoreType.DMA((self.n, N_DEV - 1)), pltpu.SemaphoreType.DMA((self.n, N_DEV - 1)),
                        pltpu.SemaphoreType.DMA((self.n,))]

    def run(self, ins, outs, sems, first, last, compute):
        send_sems, recv_sems, local_sems = sems
        x, y, c = lax.axis_index("x"), lax.axis_index("y"), lax.axis_index("c")
        me = 4 * x + 2 * y + c
        local, remote = [], []
        for k, (_, scatter) in enumerate(self.items):
            own = ins[k].at[me] if scatter else ins[k]
            local.append(pltpu.make_async_copy(own, outs[k].at[me], local_sems.at[k]))
        for r in range(1, N_DEV):
            px = 1 - x if r & 4 else x
            py = 1 - y if r & 2 else y
            pc = 1 - c if r & 1 else c
            peer = 4 * px + 2 * py + pc
            for k, (_, scatter) in enumerate(self.items):
                src = ins[k].at[peer] if scatter else ins[k]
                remote.append(pltpu.make_async_remote_copy(
                    src_ref=src, dst_ref=outs[k].at[me],
                    send_sem=send_sems.at[k, r - 1], recv_sem=recv_sems.at[k, r - 1],
                    device_id=(px, py, pc), device_id_type=pl.DeviceIdType.MESH))

        def start():
            for cp in local + remote:
                cp.start()

        def finish():
            for cp in remote:
                cp.wait_recv()
            for cp in remote:
                cp.wait_send()
            for cp in local:
                cp.wait()

        _run_phases(first, last, start, compute, finish)


def _run_phases(first, last, start, compute, finish):
    if first is None:
        start()
        compute()
        finish()
    else:
        pl.when(first)(start)
        compute()
        pl.when(last)(finish)


class _ChipGather(_Exchange):
    def run(self, ins, outs, sems, first, last, compute):
        send_sems, recv_sems, local_sems = sems
        x, y, c = lax.axis_index("x"), lax.axis_index("y"), lax.axis_index("c")
        sibling = (x, y, 1 - c)
        chips = [(1 - x, y), (x, 1 - y), (1 - x, 1 - y)]
        slot = lambda px, py, pc: 4 * px + 2 * py + pc

        def copy(k, n, src, dst_slot, to):
            return pltpu.make_async_remote_copy(
                src_ref=src, dst_ref=outs[k].at[dst_slot], send_sem=send_sems.at[k, n], recv_sem=recv_sems.at[k, n],
                device_id=to, device_id_type=pl.DeviceIdType.MESH)

        local, own, passed, arrivals = [], [], [], []
        for k in range(self.n):
            me = slot(x, y, c)
            local.append(pltpu.make_async_copy(ins[k], outs[k].at[me], local_sems.at[k]))
            own.append(copy(k, 0, ins[k], me, sibling))
            arrivals.append(copy(k, 0, ins[k], slot(*sibling), sibling))
            for j, chip in enumerate(chips):
                theirs = slot(*chip, c)
                own.append(copy(k, 1 + j, ins[k], me, (*chip, c)))
                passed.append((copy(k, 1 + j, ins[k], theirs, sibling),
                               copy(k, 4 + j, outs[k].at[theirs], theirs, sibling)))
                arrivals.append(copy(k, 4 + j, ins[k], slot(*chip, 1 - c), sibling))

        def start():
            for cp in local + own:
                cp.start()

        def finish():
            for landed, onward in passed:
                landed.wait_recv()
                onward.start()
            for cp in arrivals:
                cp.wait_recv()
            for cp in own + [onward for _, onward in passed]:
                cp.wait_send()
            for cp in local:
                cp.wait()

        _run_phases(first, last, start, compute, finish)


def _grid_ends(grid):
    ids = [pl.program_id(d) for d in range(len(grid))]
    first = functools.reduce(jnp.logical_and, [i == 0 for i in ids])
    last = functools.reduce(jnp.logical_and, [i == g - 1 for i, g in zip(ids, grid)])
    return first, last


def _host(core, n_in, n_out, n_scratch, hosted, grid):
    if hosted is None:
        return core
    nh = hosted.n

    def body(*refs):
        ins, rest = refs[:n_in], refs[n_in:]
        h_ins, rest = rest[:nh], rest[nh:]
        outs, rest = rest[:n_out], rest[n_out:]
        h_outs, rest = rest[:nh], rest[nh:]
        scratch, sems = rest[:n_scratch], rest[n_scratch:]
        first, last = _grid_ends(grid)
        hosted.run(h_ins, h_outs, sems, first, last, lambda: core(*ins, *outs, *scratch))

    return body


def _hosted_parts(hosted):
    if hosted is None:
        return [], [], [], []
    return list(hosted.specs), list(hosted.out_shape), list(hosted.arrays), list(hosted.scratch)


def _exchange(name, items):
    ex = _Exchange(items)
    n = ex.n

    def body(*refs):
        ex.run(refs[:n], refs[n:2 * n], refs[2 * n:], None, None, lambda: None)

    return pl.pallas_call(
        body, name=name, out_shape=tuple(ex.out_shape), in_specs=ex.specs, out_specs=tuple(ex.specs),
        scratch_shapes=ex.scratch,
    )(*ex.arrays)


def _matmul_rows(a, w, *, name, out_dtype=F32):
    t, k = a.shape
    n = w.shape[1]
    assert w.shape[0] == k
    bt = _row_block(t)

    def body(a_ref, w_ref, o_ref):
        o_ref[...] = jnp.dot(a_ref[...], w_ref[...], preferred_element_type=F32).astype(o_ref.dtype)

    return pl.pallas_call(
        body, name=name, grid=(t // bt,),
        in_specs=[pl.BlockSpec((bt, k), lambda i: (i, 0)), pl.BlockSpec((k, n), lambda i: (0, 0))],
        out_specs=pl.BlockSpec((bt, n), lambda i: (i, 0)),
        out_shape=jax.ShapeDtypeStruct((t, n), out_dtype), compiler_params=_params(1),
    )(a, w)


def _matmul_tn(a, b, *, name):
    t, m = a.shape
    n = b.shape[1]
    bt = TOKEN_STEP if t % TOKEN_STEP == 0 else ROW_BLOCK
    bm = _divisor_block(m, WIDE_BLOCK)
    bn = _divisor_block(n, WIDE_BLOCK)
    steps = t // bt

    def body(a_ref, b_ref, o_ref, acc):
        step = pl.program_id(2)

        @pl.when(step == 0)
        def _():
            acc[...] = jnp.zeros_like(acc)

        acc[...] += _dot_tn(a_ref[...], b_ref[...])

        @pl.when(step == steps - 1)
        def _():
            o_ref[...] = acc[...].astype(o_ref.dtype)

    return pl.pallas_call(
        body, name=name, grid=(m // bm, n // bn, steps),
        in_specs=[pl.BlockSpec((bt, bm), lambda i, j, s: (s, i)), pl.BlockSpec((bt, bn), lambda i, j, s: (s, j))],
        out_specs=pl.BlockSpec((bm, bn), lambda i, j, s: (i, j)),
        out_shape=jax.ShapeDtypeStruct((m, n), BF16), scratch_shapes=[pltpu.VMEM((bm, bn), F32)],
        compiler_params=_params(3),
    )(a, b)


def _rmsnorm_fwd(x, g, *, group, name, hosted=None):
    t, w = x.shape
    bt = ROW_BLOCK

    def body(x_ref, g_ref, o_ref, x16_ref):
        x16_ref[...] = x_ref[...].astype(BF16)
        for s in range(0, w, group):
            xs = x_ref[:, s:s + group].astype(F32)
            r = lax.rsqrt(jnp.mean(xs * xs, axis=-1, keepdims=True) + EPS)
            o_ref[:, s:s + group] = (xs * r * g_ref[:, s:s + group]).astype(o_ref.dtype)

    grid = (t // bt,)
    h_specs, h_shapes, h_args, h_scratch = _hosted_parts(hosted)
    rows = pl.BlockSpec((bt, w), lambda i: (i, 0))
    return pl.pallas_call(
        _host(body, 2, 2, 0, hosted, grid), name=name, grid=grid,
        in_specs=[rows, pl.BlockSpec((1, w), lambda i: (0, 0))] + h_specs,
        out_specs=tuple([rows, rows] + h_specs),
        out_shape=tuple([jax.ShapeDtypeStruct((t, w), BF16)] * 2 + h_shapes),
        scratch_shapes=h_scratch, compiler_params=_params(1),
    )(x, g, *h_args)


def _norm_input_grad(terms, x, g, *, group, name, out_dtypes, resid=None, init=None, hosted=None, k_chunks=1):
    t, w = x.shape
    n_terms = len(terms)
    kc = [a.shape[1] // k_chunks for a, _, _ in terms]
    per_row = sum(c * a.dtype.itemsize for c, (a, _, _) in zip(kc, terms))
    per_row += w * sum(r.dtype.itemsize for r in (x, resid, init) if r is not None)
    per_row += w * sum(jnp.dtype(dt).itemsize for dt in out_dtypes)
    fixed = 2 * sum(w * c * 2 for c in kc)
    bt = next(b for b in (2 * ROW_BLOCK, ROW_BLOCK, ROW_BLOCK // 2, ROW_BLOCK // 4)
              if t % b == 0 and fixed + 2 * b * per_row + 5 * b * w * 4 <= MATMUL_VMEM_BUDGET)
    resid_at = 2 * n_terms + 2
    init_at = resid_at + (resid is not None)
    n_in = init_at + (init is not None)
    grid = (t // bt, k_chunks)

    def body(*refs):
        x_ref, g_ref = refs[2 * n_terms], refs[2 * n_terms + 1]
        dx_refs, dg_ref, dh_ref = refs[n_in:-2], refs[-2], refs[-1]
        chunk = pl.program_id(1)

        @pl.when((pl.program_id(0) == 0) & (chunk == 0))
        def _():
            dg_ref[...] = jnp.zeros_like(dg_ref)

        part = None
        for k in range(n_terms):
            if terms[k][2]:
                term = jnp.dot(refs[2 * k][...], refs[2 * k + 1][...], preferred_element_type=F32)
            else:
                term = _dot_nt(refs[2 * k][...], refs[2 * k + 1][...])
            part = term if part is None else part + term

        @pl.when(chunk == 0)
        def _():
            dh_ref[...] = part if init is None else refs[init_at][...] + part

        @pl.when(chunk > 0)
        def _():
            dh_ref[...] += part

        @pl.when(chunk == k_chunks - 1)
        def _():
            for s in range(0, w, group):
                xs = x_ref[:, s:s + group].astype(F32)
                dhs = dh_ref[:, s:s + group]
                r = lax.rsqrt(jnp.mean(xs * xs, axis=-1, keepdims=True) + EPS)
                xh = xs * r
                dg_ref[:, s:s + group] += jnp.sum(dhs * xh, axis=0, keepdims=True)
                dxh = dhs * g_ref[:, s:s + group]
                dx = r * (dxh - xh * jnp.mean(dxh * xh, axis=-1, keepdims=True))
                if resid is not None:
                    dx = refs[resid_at][:, s:s + group] + dx
                for dx_ref in dx_refs:
                    dx_ref[:, s:s + group] = dx.astype(dx_ref.dtype)

    row = pl.BlockSpec((bt, w), lambda i, k: (i, 0))
    vec = pl.BlockSpec((1, w), lambda i, k: (0, 0))
    in_specs, args = [], []
    for c, (a, wt, w_is_kn) in zip(kc, terms):
        assert wt.shape == ((a.shape[1], w) if w_is_kn else (w, a.shape[1]))
        w_spec = pl.BlockSpec((c, w), lambda i, k: (k, 0)) if w_is_kn else pl.BlockSpec((w, c), lambda i, k: (0, k))
        in_specs += [pl.BlockSpec((bt, c), lambda i, k: (i, k)), w_spec]
        args += [a, wt]
    extra = [r for r in (resid, init) if r is not None]
    in_specs += [row, vec] + [row] * len(extra)
    args += [x, g] + extra
    h_specs, h_shapes, h_args, h_scratch = _hosted_parts(hosted)
    return pl.pallas_call(
        _host(body, n_in, len(out_dtypes) + 1, 1, hosted, grid), name=name, grid=grid, in_specs=in_specs + h_specs,
        out_specs=tuple([row] * len(out_dtypes) + [vec] + h_specs),
        out_shape=tuple([jax.ShapeDtypeStruct((t, w), dt) for dt in out_dtypes] + [jax.ShapeDtypeStruct((1, w), F32)]
                        + h_shapes),
        scratch_shapes=[pltpu.VMEM((bt, w), F32)] + h_scratch, compiler_params=_params(2),
    )(*args, *h_args)


def _tile_plan(tile):
    is_q = tile < N_PAIRS
    is_dil = (tile % N_PAIRS) >= N_PAIRS // 2
    return is_q, is_dil, (0 if is_q else 2) + (1 if is_dil else 0)


def _proj_tile(kind, pair):
    return kind * N_PAIRS + pair


def _segment_ones():
    lane = np.arange(LANES)
    return jnp.asarray((lane[:, None] // HEAD_DIM) == (lane[None, :] // HEAD_DIM), BF16)


def _rope_tables(seq):
    inv_freq = jnp.power(jnp.float32(ROPE_THETA), -jnp.arange(ROPE_HALF, dtype=F32) * 2.0 / ROPE_DIM)
    ang = jnp.arange(seq).astype(F32)[:, None] * inv_freq[None, :]
    cos, sin = jnp.cos(ang), jnp.sin(ang)
    ones = jnp.ones((seq, HEAD_DIM - ROPE_DIM), F32)
    zeros = jnp.zeros((seq, HEAD_DIM - ROPE_DIM), F32)
    zh = jnp.zeros((seq, ROPE_HALF), F32)
    cos_t = jnp.concatenate([cos, cos, ones], axis=1)
    sin_a = jnp.concatenate([-sin, zh, zeros], axis=1)
    sin_b = jnp.concatenate([zh, sin, zeros], axis=1)
    return tuple(jnp.tile(tab, (1, 2)) for tab in (cos_t, sin_a, sin_b))


def _log_sigmoid(z):
    return jnp.minimum(z, 0.0) - jnp.log1p(jnp.exp(-jnp.abs(z)))


def _aug_placement():
    place = np.zeros((N_PAIRS, LANES, LANES), np.float32)
    for is_k in range(2):
        for pair in range(N_PAIRS // 2):
            for e in range(2):
                other = HEAD_DIM * (1 - e)
                ones_at = other + (AUG_C if is_k else AUG_ONE)
                c_at = other + (AUG_ONE if is_k else AUG_C)
                for n in range(3):
                    place[4 * is_k + pair, N_HEADS_FOX * n + 2 * pair + e, c_at + n] = -1.0 if is_k else 1.0
                    place[4 * is_k + pair, 3 * N_HEADS_FOX, ones_at + n] = 1.0
    return jnp.asarray(place, BF16)


def _qk_prep_fwd(h1, w_main_t, w_fa_t, b_pad, gains, rope, seq):
    t = h1.shape[0]
    bt = ROW_BLOCK
    nsb = seq // bt
    seg = _segment_ones()
    rr = np.arange(bt)
    tri = jnp.asarray(rr[:, None] <= rr[None, :], BF16)

    def body(h_ref, w_ref, wfa_ref, b_ref, g_ref, cos_ref, sa_ref, sb_ref, seg_ref, tri_ref, place_ref,
             p_ref, qk_ref, fa_ref, carry):
        @pl.when(pl.program_id(0) % nsb == 0)
        def _():
            carry[...] = jnp.zeros_like(carry)

        def project(first):
            cols = slice(first * LANES, (first + 2) * LANES)
            y = _dot_nt(h_ref[...], w_ref[cols, :]).astype(BF16)
            p_ref[:, cols] = y
            return y

        lane = lax.broadcasted_iota(jnp.int32, (bt, LANES), 1)
        fa = _dot_nt(h_ref[...], wfa_ref[...])
        fa_ref[...] = fa
        logf = jnp.where(lane < N_HEADS_FOX, _log_sigmoid(fa + b_ref[...]), 0.0)
        c_rows = _split_dot(logf.T[0:N_HEADS_FOX, :], tri_ref[...], 3) + carry[:, 0:1]
        carry[...] = jnp.broadcast_to(c_rows[:, bt - 1:bt], carry.shape)
        cblk = jnp.concatenate([c_rows, jnp.zeros((LANES - N_HEADS_FOX, bt), F32)], axis=0).T
        packed = jnp.where(lane == 3 * N_HEADS_FOX, 1.0, 0.0)
        rest = cblk * LOG2E
        for n in range(3):
            term = rest.astype(BF16).astype(F32)
            packed = packed + (pltpu.roll(term, N_HEADS_FOX * n, 1) if n else term)
            rest = rest - term
        packed = packed.astype(BF16)
        low = lane < HEAD_DIM

        ahead = project(0)
        for tile in range(2 * N_PAIRS):
            is_q, is_dil, grow = _tile_plan(tile)
            pair = tile % N_PAIRS
            assert tile == _proj_tile(0 if is_q else 1, pair)
            if tile % 2 == 0:
                both = ahead
                if tile + 2 < 2 * N_PAIRS:
                    ahead = project(tile + 2)
                if tile % 4 == 2:
                    project(2 * N_PAIRS + tile // 2 - 1)
            xs = both[:, (tile % 2) * LANES:(tile % 2 + 1) * LANES].astype(F32)
            r = lax.rsqrt(_split_dot(xs * xs, seg_ref[...], 2) * (1.0 / HEAD_DIM) + EPS)
            yv = xs * r * g_ref[grow:grow + 1, :]
            if is_dil:
                yv = (yv * cos_ref[...] + pltpu.roll(yv, LANES - ROPE_HALF, 1) * sa_ref[...]
                      + pltpu.roll(yv, ROPE_HALF, 1) * sb_ref[...])
                aug = jnp.zeros((bt, LANES), F32)
            else:
                aug = jnp.dot(packed, place_ref[(0 if is_q else N_PAIRS // 2) + pair], preferred_element_type=F32)
            if is_q:
                yv = yv * (HEAD_DIM ** -0.5 * LOG2E)
            dst = ((0 if is_q else N_HEADS) + 2 * pair) * LANES
            qk_ref[:, dst:dst + LANES] = jnp.where(low, yv, aug).astype(BF16)
            qk_ref[:, dst + LANES:dst + 2 * LANES] = jnp.where(low, aug, yv).astype(BF16)

    row128 = pl.BlockSpec((bt, LANES), lambda i: (i, 0))
    rope_spec = pl.BlockSpec((bt, LANES), lambda i: (i % nsb, 0))
    const = lambda shape: pl.BlockSpec(shape, lambda i: (0,) * len(shape))
    return pl.pallas_call(
        body, name="in_proj_qk_prep", grid=(t // bt,),
        in_specs=[pl.BlockSpec((bt, D_MODEL), lambda i: (i, 0)), const((MAIN_COLS, D_MODEL)),
                  const((LANES, D_MODEL)), const((1, LANES)), const((8, LANES)), rope_spec, rope_spec, rope_spec,
                  const((LANES, LANES)), const((bt, bt)), const((N_PAIRS, LANES, LANES))],
        out_specs=(pl.BlockSpec((bt, MAIN_COLS), lambda i: (i, 0)),
                   pl.BlockSpec((bt, 2 * N_HEADS * LANES), lambda i: (i, 0)), row128),
        out_shape=(jax.ShapeDtypeStruct((t, MAIN_COLS), BF16),
                   jax.ShapeDtypeStruct((t, 2 * N_HEADS * LANES), BF16), jax.ShapeDtypeStruct((t, LANES), F32)),
        scratch_shapes=[pltpu.VMEM((8, LANES), F32)], compiler_params=_params(1),
    )(h1, w_main_t, w_fa_t, b_pad, gains, *rope, seg, tri, _aug_placement())


def _qk_prep_bwd(dq, dk, dqx, dkx, dv, proj, fa, b_pad, gains, rope, seq):
    t = proj.shape[0]
    bt = ROW_BLOCK
    nsb = seq // bt
    nblk = t // bt
    seg = _segment_ones()
    rr = np.arange(bt)
    triu = jnp.asarray(rr[:, None] >= rr[None, :], BF16)

    def body(dq_ref, dk_ref, dqx_ref, dkx_ref, dv_ref, p_ref, fa_ref, b_ref, g_ref, cos_ref, sa_ref, sb_ref, seg_ref,
             triu_ref, dp_ref, dg_ref, db_ref, carry):
        step = pl.program_id(0)

        @pl.when(step == 0)
        def _():
            dg_ref[...] = jnp.zeros_like(dg_ref)
            db_ref[...] = jnp.zeros_like(db_ref)

        @pl.when(step % nsb == 0)
        def _():
            carry[...] = jnp.zeros_like(carry)

        for tile in range(2 * N_PAIRS):
            is_q, is_dil, grow = _tile_plan(tile)
            first = _proj_tile(0 if is_q else 1, tile % N_PAIRS) * LANES
            cols = slice(first, first + LANES)
            src = dq_ref if is_q else dk_ref
            half = slice((tile % N_PAIRS) * LANES, (tile % N_PAIRS + 1) * LANES)
            dy = src[:, half].astype(F32)
            dy = dy * (HEAD_DIM ** -0.5 if is_q else LN2)
            if is_dil:
                dy = (dy * cos_ref[...] + pltpu.roll(dy * sa_ref[...], ROPE_HALF, 1)
                      + pltpu.roll(dy * sb_ref[...], LANES - ROPE_HALF, 1))
            xs = p_ref[:, cols].astype(F32)
            r = lax.rsqrt(_split_dot(xs * xs, seg_ref[...], 2) * (1.0 / HEAD_DIM) + EPS)
            xh = xs * r
            dg_ref[tile:tile + 1, :] += jnp.sum(dy * xh, axis=0, keepdims=True)
            dxh = dy * g_ref[grow:grow + 1, :]
            seg_mean = _split_dot(dxh * xh, seg_ref[...], 2) * (1.0 / HEAD_DIM)
            dp_ref[:, cols] = (r * (dxh - xh * seg_mean)).astype(BF16)

        lane = lax.broadcasted_iota(jnp.int32, (bt, LANES), 1)
        dc = jnp.zeros((bt, LANES), F32)
        for h in range(N_HEADS_FOX):
            other = (h // 2) * LANES + HEAD_DIM * (1 - h % 2)
            row_sum = dqx_ref[:, other + AUG_C:other + AUG_C + 1]
            col_sum = dkx_ref[:, other + AUG_ONE:other + AUG_ONE + 1]
            dc = jnp.where(lane == h, row_sum - col_sum, dc)
        d_rows = _split_dot(dc.T[0:N_HEADS_FOX, :], triu_ref[...], 3) + carry[:, 0:1]
        carry[...] = jnp.broadcast_to(d_rows[:, 0:1], carry.shape)
        dlogf = jnp.concatenate([d_rows, jnp.zeros((LANES - N_HEADS_FOX, bt), F32)], axis=0).T
        z = fa_ref[...] + b_ref[...]
        dfa = dlogf * (1.0 / (1.0 + jnp.exp(z)))
        db_ref[0:1, :] += jnp.sum(dfa, axis=0, keepdims=True)
        for group in range(2):
            first = _proj_tile(2, group * (N_PAIRS // 2)) * LANES
            dp_ref[:, first:first + W_GROUP] = dv_ref[:, group * W_GROUP:(group + 1) * W_GROUP]
        dp_ref[:, MAIN_COLS:PROJ_COLS] = dfa.astype(BF16)

    rev = lambda i: nblk - 1 - i
    row = lambda w: pl.BlockSpec((bt, w), lambda i: (rev(i), 0))
    rope_spec = pl.BlockSpec((bt, LANES), lambda i: (rev(i) % nsb, 0))
    const = lambda shape: pl.BlockSpec(shape, lambda i: (0, 0))
    return pl.pallas_call(
        body, name="qk_prep_bwd", grid=(nblk,),
        in_specs=[row(D_MODEL), row(D_MODEL), row(W_GROUP), row(W_GROUP), row(D_MODEL), row(2 * D_MODEL), row(LANES),
                  const((1, LANES)), const((8, LANES)), rope_spec, rope_spec, rope_spec, const((LANES, LANES)),
                  const((bt, bt))],
        out_specs=(row(PROJ_COLS), const((2 * N_PAIRS, LANES)), const((8, LANES))),
        out_shape=(jax.ShapeDtypeStruct((t, PROJ_COLS), BF16),
                   jax.ShapeDtypeStruct((2 * N_PAIRS, LANES), F32), jax.ShapeDtypeStruct((8, LANES), F32)),
        scratch_shapes=[pltpu.VMEM((8, LANES), F32)], compiler_params=_params(1),
    )(dq, dk, dqx, dkx, dv, proj, fa, b_pad, gains, *rope, seg, triu)


def _bias_tables(seq, keys_first):
    nb = seq // ATT_BLOCK
    idx = np.arange(ATT_BLOCK)
    q_idx, k_idx = (idx[None, None, :], idx[None, :, None]) if keys_first else (idx[None, :, None], idx[None, None, :])
    dist = np.arange(nb)[:, None, None] * ATT_BLOCK + q_idx - k_idx
    causal = dist >= 0
    count = np.zeros(dist.shape, np.int32)
    for window, dilation in DILATION_PAIRS:
        count = count + (causal & (dist % dilation == 0) & (dist <= window))
    fox = np.where(causal, 0.0, NEG)
    dil = np.where(count == 3, math.log2(3.0), np.where(count == 2, 1.0, np.where(count == 1, 0.0, NEG)))
    return jnp.asarray(np.stack([fox, dil], axis=0), F32)


def _attn_specs(seq):
    nb = seq // ATT_BLOCK
    col = pl.BlockSpec((seq, LANES), lambda b, j: (b, j))
    heads = lambda off: pl.BlockSpec((seq, 2 * LANES), lambda b, j: (b, off + j))
    v_spec = pl.BlockSpec((seq, LANES), lambda b, j: (b, _proj_tile(2, j)))
    table_spec = pl.BlockSpec((1, nb, ATT_BLOCK, ATT_BLOCK), lambda b, j: (j // (N_PAIRS // 2), 0, 0, 0))
    return col, heads, v_spec, table_spec


def _head_lanes(e, shape, axis):
    pos = lax.broadcasted_iota(jnp.int32, shape, axis)
    return pos < HEAD_DIM if e == 0 else pos >= HEAD_DIM


def _attn_fwd(qk, proj, tables, seq, hosted=None):
    t = qk.shape[0]
    nb = seq // ATT_BLOCK
    blk = ATT_BLOCK

    def body(q_ref, k_ref, v_ref, tab_ref, o_ref, lse_ref):
        mine = [_head_lanes(e, (seq, LANES), 1) for e in range(2)]
        lane = lax.broadcasted_iota(jnp.int32, (seq, LANES), 1)
        v_aug = [jnp.where(mine[e], v_ref[...], (lane == HEAD_DIM * (1 - e)).astype(BF16)) for e in range(2)]
        def scores(i, e):
            heads_e = slice(e * LANES, (e + 1) * LANES)
            s = _dot_nt(q_ref[i * blk:(i + 1) * blk, heads_e], k_ref[0:(i + 1) * blk, heads_e])
            s = jnp.concatenate([s[:, jj * blk:(jj + 1) * blk] + tab_ref[0, i - jj] for jj in range(i + 1)], axis=1)
            return s, jnp.max(s, axis=1, keepdims=True)

        chains = [(i, e) for i in reversed(range(nb)) for e in range(2)]
        ahead = 2
        pending = [scores(*chain) for chain in chains[:ahead]]
        done = {}
        for n, (i, e) in enumerate(chains):
            s, m = pending.pop(0)
            if n + ahead < len(chains):
                pending.append(scores(*chains[n + ahead]))
            acc = jnp.dot(jnp.exp2(s - m).astype(BF16), v_aug[e][0:(i + 1) * blk], preferred_element_type=F32)
            ones_at = HEAD_DIM * (1 - e)
            l = acc[:, ones_at:ones_at + 1]
            done[e] = (acc / l, m + jnp.log2(l))
            if e == 1:
                rows = slice(i * blk, (i + 1) * blk)
                o_ref[rows, :] = jnp.where(mine[0][rows], done[0][0], done[1][0]).astype(o_ref.dtype)
                lse_ref[rows, :] = jnp.where(mine[0][rows], done[0][1], done[1][1])

    col, heads, v_spec, table_spec = _attn_specs(seq)
    grid = (t // seq, N_PAIRS)
    h_specs, h_shapes, h_args, h_scratch = _hosted_parts(hosted)
    return pl.pallas_call(
        _host(body, 4, 2, 0, hosted, grid), name="attn_fwd", grid=grid,
        in_specs=[heads(0), heads(N_PAIRS), v_spec, table_spec] + h_specs,
        out_specs=tuple([col, col] + h_specs),
        out_shape=tuple([jax.ShapeDtypeStruct((t, D_MODEL), BF16), jax.ShapeDtypeStruct((t, D_MODEL), F32)] + h_shapes),
        scratch_shapes=h_scratch, compiler_params=_params(2),
    )(qk, qk, proj, tables, *h_args)


def _attn_bwd(qk, proj, tables, o, lse, do, seq, hosted=None):
    t = qk.shape[0]
    nb = seq // ATT_BLOCK
    blk = ATT_BLOCK
    group = math.gcd(nb, ATT_GROUP)

    def body(q_ref, k_ref, v_ref, tab_ref, o_ref, lse_ref, do_ref,
             dq_ref, dk_ref, dv_ref, dqx_ref, dkx_ref, dk_acc, dv_acc):
        mine = [_head_lanes(e, (blk, LANES), 1) for e in range(2)]
        top = _head_lanes(0, (LANES, blk), 0)
        head_rows = lax.broadcasted_iota(jnp.int32, (8, LANES), 0)
        head_of_lane = lax.broadcasted_iota(jnp.int32, (8, LANES), 1) // HEAD_DIM
        head_sel = (head_rows == head_of_lane).astype(BF16)
        dk_acc[...] = jnp.zeros_like(dk_acc)
        dv_acc[...] = jnp.zeros_like(dv_acc)

        def block_rows(i):
            return pl.ds(pl.multiple_of(i * blk, blk), blk)

        def q_group(g, _):
            base = g * group
            qs, doe, delta, lse_e = [], [], [], []
            for b in range(group):
                rows = block_rows(base + b)
                qs.append([q_ref[rows, e * LANES:(e + 1) * LANES] for e in range(2)])
                do_blk = do_ref[rows, :]
                doe.append([jnp.where(mine[e], do_blk, jnp.zeros_like(do_blk)) for e in range(2)])
                delta_t = _split_dot_nt(head_sel, do_blk.astype(F32) * o_ref[rows, :].astype(F32), 3)
                lse_t = _split_dot_nt(head_sel, lse_ref[rows, :], 3) * (1.0 / HEAD_DIM)
                delta.append([delta_t[e:e + 1, :] for e in range(2)])
                lse_e.append([lse_t[e:e + 1, :] for e in range(2)])

            def key_block(dq_t, jj, members):
                krows = block_rows(jj)
                v = v_ref[krows, :]
                dq_t = [list(d) for d in dq_t]
                lo, hi = slice(0, blk // 2), slice(blk // 2, blk)
                dv_part = [None, None]
                add = lambda acc, part: part if acc is None else acc + part

                def probs(k_sub, v_sub, keys, queries, b, e, dist):
                    q_sub, do_sub = qs[b][e][queries], doe[b][e][queries]
                    p_t = jnp.exp2(_dot_nt(k_sub, q_sub) + tab_ref[0, dist, keys, queries] - lse_e[b][e][:, queries])
                    ds_t = (p_t * (_dot_nt(v_sub, do_sub) - delta[b][e][:, queries])).astype(BF16)
                    return p_t.astype(BF16), ds_t, q_sub, do_sub, k_sub

                def outputs(tile):
                    p_t, ds_t, q_sub, do_sub, k_sub = tile
                    return (jnp.dot(p_t, do_sub, preferred_element_type=F32),
                            jnp.dot(ds_t, q_sub, preferred_element_type=F32), _dot_tn(k_sub, ds_t))

                for e in range(2):
                    k_e = k_ref[krows, e * LANES:(e + 1) * LANES]
                    dk_part = [None, None]
                    tiles = []
                    for b, dist in members:
                        if isinstance(dist, int) and dist == 0:
                            tiles.append((b, probs(k_e[lo], v[lo], lo, slice(0, blk), b, e, dist),
                                          probs(k_e[hi], v[hi], hi, hi, b, e, dist)))
                        else:
                            tiles.append((b, probs(k_e, v, slice(0, blk), slice(0, blk), b, e, dist), None))
                    for b, first, second in tiles:
                        if second is not None:
                            dv_a, dk_a, dq_a = outputs(first)
                            dv_b, dk_b, dq_b = outputs(second)
                            halves = ((dv_a, dk_a), (dv_b, dk_b))
                            dq = jnp.concatenate([dq_a[:, lo], dq_a[:, hi] + dq_b], axis=1)
                        else:
                            dv_f, dk_f, dq = outputs(first)
                            halves = ((dv_f[lo], dk_f[lo]), (dv_f[hi], dk_f[hi]))
                        for n, (dv_h, dk_h) in enumerate(halves):
                            dv_part[n] = add(dv_part[n], dv_h)
                            dk_part[n] = add(dk_part[n], dk_h)
                        dq_t[b][e] = dq_t[b][e] + dq
                    dk_acc[e, krows, :] += jnp.concatenate(dk_part, axis=0)
                dv_acc[krows, :] += jnp.concatenate(dv_part, axis=0)
                return tuple(tuple(d) for d in dq_t)

            zacc = jnp.zeros((LANES, blk), F32)
            dq_t = tuple((zacc, zacc) for _ in range(group))
            dq_t = lax.fori_loop(
                0, base, lambda jj, st: key_block(st, jj, [(b, base + b - jj) for b in range(group)]), dq_t)
            for a in range(group):
                dq_t = key_block(dq_t, base + a, [(b, b - a) for b in range(a, group)])
            for b in range(group):
                rows = block_rows(base + b)
                dq_ref[rows, :] = jnp.where(top, dq_t[b][0], dq_t[b][1]).T.astype(BF16)
                dqx_ref[rows, :] = jnp.where(top, dq_t[b][1], dq_t[b][0]).T
            return 0

        lax.fori_loop(0, nb // group, q_group, 0)
        lo = _head_lanes(0, (seq, LANES), 1)
        dk_ref[...] = jnp.where(lo, dk_acc[0], dk_acc[1]).astype(BF16)
        dkx_ref[...] = jnp.where(lo, dk_acc[1], dk_acc[0])
        dv_ref[...] = dv_acc[...].astype(dv_ref.dtype)

    col, heads, v_spec, table_spec = _attn_specs(seq)
    grid = (t // seq, N_PAIRS)
    h_specs, h_shapes, h_args, h_scratch = _hosted_parts(hosted)
    f32_out = jax.ShapeDtypeStruct((t, D_MODEL), F32)
    return pl.pallas_call(
        _host(body, 7, 5, 2, hosted, grid), name="attn_bwd", grid=grid,
        in_specs=[heads(0), heads(N_PAIRS), v_spec, table_spec, col, col, col] + h_specs,
        out_specs=tuple([col] * 5 + h_specs),
        out_shape=tuple([jax.ShapeDtypeStruct((t, D_MODEL), BF16)] * 3 + [f32_out, f32_out] + h_shapes),
        scratch_shapes=[pltpu.VMEM((2, seq, LANES), F32), pltpu.VMEM((seq, LANES), F32)] + h_scratch,
        compiler_params=_params(2),
    )(qk, qk, proj, tables, o, lse, do, *h_args)


def _row_block(t):
    return 2 * ROW_BLOCK if t % (2 * ROW_BLOCK) == 0 else ROW_BLOCK


def _out_proj_ffn_norm(o, g_out, w_out, x, g_ffn):
    t = o.shape[0]
    bt = _row_block(t)

    def body(o_ref, go_ref, w_ref, x_ref, gf_ref, on_ref, x2_ref, h2_ref):
        for s in range(0, D_MODEL, W_GROUP):
            os_ = o_ref[:, s:s + W_GROUP].astype(F32)
            r = lax.rsqrt(jnp.mean(os_ * os_, axis=-1, keepdims=True) + EPS)
            on_ref[:, s:s + W_GROUP] = (os_ * r * go_ref[:, s:s + W_GROUP]).astype(BF16)
        x2 = x_ref[...] + jnp.dot(on_ref[...], w_ref[...], preferred_element_type=F32)
        x2_ref[...] = x2.astype(BF16)
        r2 = lax.rsqrt(jnp.mean(x2 * x2, axis=-1, keepdims=True) + EPS)
        h2_ref[...] = (x2 * r2 * gf_ref[...]).astype(BF16)

    row = pl.BlockSpec((bt, D_MODEL), lambda i: (i, 0))
    vec = pl.BlockSpec((1, D_MODEL), lambda i: (0, 0))
    return pl.pallas_call(
        body, name="out_proj", grid=(t // bt,),
        in_specs=[row, vec, pl.BlockSpec((D_MODEL, D_MODEL), lambda i: (0, 0)), row, vec],
        out_specs=(row, row, row),
        out_shape=tuple([jax.ShapeDtypeStruct((t, D_MODEL), BF16)] * 3),
        compiler_params=_params(1),
    )(o, g_out, w_out, x, g_ffn)


def _ffn_gate_up(h2, w_gate_t, w_up_t):
    t = h2.shape[0]
    bt = _row_block(t)
    bn = _divisor_block(D_FF, WIDE_BLOCK)

    def body(h_ref, wg_ref, wu_ref, a_ref, u_ref, f_ref):
        a = _dot_nt(h_ref[...], wg_ref[...])
        u = _dot_nt(h_ref[...], wu_ref[...])
        a_ref[...] = a.astype(BF16)
        u_ref[...] = u.astype(BF16)
        f_ref[...] = (a * jax.nn.sigmoid(a) * u).astype(BF16)

    blk = pl.BlockSpec((bt, bn), lambda j, i: (i, j))
    w_blk = pl.BlockSpec((bn, D_MODEL), lambda j, i: (j, 0))
    shape = jax.ShapeDtypeStruct((t, D_FF), BF16)
    return pl.pallas_call(
        body, name="ffn_gate_up", grid=(D_FF // bn, t // bt),
        in_specs=[pl.BlockSpec((bt, D_MODEL), lambda j, i: (i, 0)), w_blk, w_blk],
        out_specs=(blk, blk, blk), out_shape=(shape, shape, shape), compiler_params=_params(2),
    )(h2, w_gate_t, w_up_t)


def _ffn_down_grad(dy16, w_down, a, u):
    t = a.shape[0]
    bt = _row_block(t)
    bn = _divisor_block(D_FF, WIDE_BLOCK)

    def body(dy_ref, w_ref, a_ref, u_ref, da_ref, du_ref):
        df = _dot_nt(dy_ref[...], w_ref[...])
        av = a_ref[...].astype(F32)
        sg = jax.nn.sigmoid(av)
        da_ref[...] = (df * u_ref[...].astype(F32) * sg * (1.0 + av * (1.0 - sg))).astype(BF16)
        du_ref[...] = (df * av * sg).astype(BF16)

    blk = pl.BlockSpec((bt, bn), lambda j, i: (i, j))
    shape = jax.ShapeDtypeStruct((t, D_FF), BF16)
    return pl.pallas_call(
        body, name="d_ffn_down", grid=(D_FF // bn, t // bt),
        in_specs=[pl.BlockSpec((bt, D_MODEL), lambda j, i: (i, 0)), pl.BlockSpec((bn, D_MODEL), lambda j, i: (j, 0)),
                  blk, blk],
        out_specs=(blk, blk), out_shape=(shape, shape), compiler_params=_params(2),
    )(dy16, w_down, a, u)


def _ffn_down_loss(f, w_down, x2, target):
    t, w = x2.shape
    bt = _row_block(t)

    def body(f_ref, w_ref, x_ref, t_ref, dy16_ref, loss_ref):
        @pl.when(pl.program_id(0) == 0)
        def _():
            loss_ref[...] = jnp.zeros_like(loss_ref)

        err = (x_ref[...] + jnp.dot(f_ref[...], w_ref[...], preferred_element_type=F32)) - t_ref[...]
        dy16_ref[...] = (err * (1.0 / w)).astype(BF16)
        loss_ref[...] += 0.5 * jnp.sum(jnp.mean(err * err, axis=-1, keepdims=True), axis=0, keepdims=True)

    row = pl.BlockSpec((bt, w), lambda i: (i, 0))
    return pl.pallas_call(
        body, name="ffn_down_loss", grid=(t // bt,),
        in_specs=[pl.BlockSpec((bt, D_FF), lambda i: (i, 0)), pl.BlockSpec((D_FF, w), lambda i: (0, 0)), row, row],
        out_specs=(row, pl.BlockSpec((8, LANES), lambda i: (0, 0))),
        out_shape=(jax.ShapeDtypeStruct((t, w), BF16), jax.ShapeDtypeStruct((8, LANES), F32)),
        compiler_params=_params(1),
    )(f, w_down, x2, target)


def _adamw(parts, w, m, v, *, name):
    _, rows, cols = w.shape
    br = rows if rows <= 512 else 256
    assert rows % br == 0

    def body(p_ref, w_ref, m_ref, v_ref, g_ref, d_ref, nm_ref, nv_ref):
        g = p_ref[0].astype(F32)
        for r in range(1, N_DEV):
            g = g + p_ref[r].astype(F32)
        m2 = ADAM_B1 * m_ref[0] + (1.0 - ADAM_B1) * g
        v2 = ADAM_B2 * v_ref[0] + (1.0 - ADAM_B2) * jnp.square(g)
        m_hat = m2 / (1.0 - ADAM_B1 ** ADAM_STEP)
        v_hat = v2 / (1.0 - ADAM_B2 ** ADAM_STEP)
        g_ref[0] = g
        d_ref[0] = -ADAM_LR * (m_hat / (jnp.sqrt(v_hat) + ADAM_EPS) + ADAM_WD * w_ref[0])
        nm_ref[0] = m2
        nv_ref[0] = v2

    blk = pl.BlockSpec((1, br, cols), lambda i: (0, i, 0))
    shape = jax.ShapeDtypeStruct((1, rows, cols), F32)
    return pl.pallas_call(
        body, name=name, grid=(rows // br,),
        in_specs=[pl.BlockSpec((N_DEV, br, cols), lambda i: (0, i, 0)), blk, blk, blk],
        out_specs=(blk, blk, blk, blk), out_shape=(shape, shape, shape, shape), compiler_params=_params(1),
    )(parts, w, m, v)


_QA, _KA, _VA, _FA, _QD, _KD, _VD = (0, 512), (512, 1024), (1024, 1536), (1536, 1544), (1544, 2056), (2056, 2568), (2568, 3080)
_MAIN_ORDER = (_QA, _QD, _KA, _KD, _VA, _VD)
MAIN_COLS = 3 * D_MODEL
PROJ_COLS = MAIN_COLS + LANES
COL_SHARDED = ("w_in", "w_gate", "w_up")


def _swap(w):
    return jnp.transpose(w, (0, 2, 1))


def _w_in_to_kernel(w_t):
    main = jnp.concatenate([w_t[a:b] for a, b in _MAIN_ORDER], axis=0)
    forget = jnp.pad(w_t[_FA[0]:_FA[1]], ((0, LANES - N_HEADS_FOX), (0, 0)))
    return main, forget


def _w_in_from_kernel(g_t):
    pos = {span: i * W_GROUP for i, span in enumerate(_MAIN_ORDER)}
    parts = []
    for span in (_QA, _KA, _VA, _FA, _QD, _KD, _VD):
        if span == _FA:
            parts.append(g_t[MAIN_COLS:MAIN_COLS + N_HEADS_FOX])
        else:
            parts.append(g_t[pos[span]:pos[span] + W_GROUP])
    return jnp.concatenate(parts, axis=0)


def _pack_small(vals):
    rows = []
    for name, _, n_rows in SMALL_LAYOUT:
        flat = vals[name].reshape(-1).astype(F32)
        rows.append(jnp.pad(flat, (0, n_rows * LANES - flat.shape[0])).reshape(n_rows, LANES))
    packed = jnp.concatenate(rows, axis=0)
    return jnp.pad(packed, ((0, SMALL_ROWS - packed.shape[0]), (0, 0)))


def _unpack_small(packed, like):
    out = {}
    for name, row, n_rows in SMALL_LAYOUT:
        n = like[name].size
        out[name] = packed[row:row + n_rows].reshape(-1)[:n].reshape(like[name].shape)
    return out


def _device_step(x, target, small, shards):
    bsz, seq, _ = x.shape
    t = bsz * seq
    xf = x.reshape(t, D_MODEL)
    tf = target.reshape(t, D_MODEL)
    row = lambda v: v.reshape(1, -1)
    g_out = jnp.concatenate([small["g_out_fox"], small["g_out_dil"]]).reshape(1, D_MODEL)
    gains = jnp.concatenate(
        [jnp.tile(small[n].reshape(1, HEAD_DIM), (1, 2)) for n in ("g_q_fox", "g_q_dil", "g_k_fox", "g_k_dil")]
        + [jnp.zeros((4, LANES), F32)], axis=0)
    b_pad = jnp.pad(small["b_forget"].reshape(1, N_HEADS_FOX), ((0, 0), (0, LANES - N_HEADS_FOX)))
    rope = _rope_tables(seq)
    tables_qk = _bias_tables(seq, keys_first=False)
    tables_kq = _bias_tables(seq, keys_first=True)

    h1, x16, g_in = _rmsnorm_fwd(xf, row(small["g_mix"]), group=D_MODEL, name="norm_mix",
                                 hosted=_ChipGather([(shards["w_in"], False)]))
    w_main_t, w_fa_t = _w_in_to_kernel(g_in.reshape(IN_COLS, D_MODEL))
    w_in_all_t = jnp.concatenate([w_main_t, w_fa_t], axis=0)
    proj, qk, fa = _qk_prep_fwd(h1, w_main_t, w_fa_t, b_pad, gains, rope, seq)
    late = _Exchange([(shards[n], False) for n in ("w_out", "w_gate", "w_up", "w_down")])
    o, lse, g_out_w, g_gate, g_up, g_down = _attn_fwd(qk, proj, tables_qk, seq, hosted=late)
    w_out = g_out_w.reshape(D_MODEL, D_MODEL)
    w_gate_t = g_gate.reshape(D_FF, D_MODEL)
    w_up_t = g_up.reshape(D_FF, D_MODEL)
    w_down = g_down.reshape(D_FF, D_MODEL)
    on, x2, h2 = _out_proj_ffn_norm(o, g_out, w_out, x16, row(small["g_ffn"]))
    a, u, f = _ffn_gate_up(h2, w_gate_t, w_up_t)
    dy16, loss_tile = _ffn_down_loss(f, w_down, x2, tf)

    da, du = _ffn_down_grad(dy16, w_down, a, u)
    gw_down = _matmul_tn(f, dy16, name="gw_down")
    gw_gate_t = _matmul_tn(da, h2, name="gw_gate")
    gw_up_t = _matmul_tn(du, h2, name="gw_up")
    dh2_gate = _matmul_rows(da, w_gate_t, name="d_ffn_gate", out_dtype=BF16)
    dx2_16, dg_ffn = _norm_input_grad([(du, w_up_t, True)], x2, row(small["g_ffn"]), group=D_MODEL,
                                      name="d_ffn_up", out_dtypes=(BF16,), resid=dy16, init=dh2_gate)
    gw_out = _matmul_tn(on, dx2_16, name="gw_out")
    do, dg_out = _norm_input_grad([(dx2_16, w_out, False)], o, g_out, group=W_GROUP, name="d_out_proj",
                                  out_dtypes=(BF16,))

    shard_rows = lambda g: g.reshape(N_DEV, g.shape[0] // N_DEV, g.shape[1])
    ffn_grads = _Exchange([(shard_rows(g), True) for g in (gw_out, gw_gate_t, gw_up_t, gw_down)])
    dq, dk, dv, dqx, dkx, p_out, p_gate, p_up, p_down = _attn_bwd(qk, proj, tables_kq, o, lse, do, seq, hosted=ffn_grads)
    dproj, dgains, db = _qk_prep_bwd(dq, dk, dqx, dkx, dv, proj, fa, b_pad, gains, rope, seq)
    gw_in_t = _matmul_tn(dproj, h1, name="gw_in")
    in_grad = _Exchange([(shard_rows(_w_in_from_kernel(gw_in_t)), True)])
    dx, dg_mix, p_in = _norm_input_grad([(dproj, w_in_all_t, True)], x16, row(small["g_mix"]), group=D_MODEL,
                                        name="d_in_proj", out_dtypes=(F32,), resid=dx2_16, hosted=in_grad)

    fold = lambda rows: jnp.sum(rows[:, :HEAD_DIM] + rows[:, HEAD_DIM:], axis=0)
    half = N_PAIRS // 2
    gsmall = {
        "g_mix": dg_mix, "g_ffn": dg_ffn, "g_out_fox": dg_out[0, :W_GROUP], "g_out_dil": dg_out[0, W_GROUP:],
        "g_q_fox": fold(dgains[0:half]), "g_q_dil": fold(dgains[half:N_PAIRS]),
        "g_k_fox": fold(dgains[N_PAIRS:N_PAIRS + half]), "g_k_dil": fold(dgains[N_PAIRS + half:]),
        "b_forget": db[0, :N_HEADS_FOX],
    }
    packed = _pack_small(gsmall).at[LOSS_ROW].set(loss_tile[0])
    (p_small,) = _exchange("small_exchange", [(packed, False)])
    parts = {"w_in": p_in, "w_out": p_out, "w_gate": p_gate, "w_up": p_up, "w_down": p_down}
    return dx.reshape(x.shape), parts, p_small


def kernel(x, g_mix, w_in, b_forget, g_q_fox, g_k_fox, g_q_dil, g_k_dil, g_out_fox, g_out_dil, w_out, g_ffn, w_gate, w_up, w_down, loss_target, m_g_mix, m_w_in, m_b_forget, m_g_q_fox, m_g_k_fox, m_g_q_dil, m_g_k_dil, m_g_out_fox, m_g_out_dil, m_w_out, m_g_ffn, m_w_gate, m_w_up, m_w_down, v_g_mix, v_w_in, v_b_forget, v_g_q_fox, v_g_k_fox, v_g_q_dil, v_g_k_dil, v_g_out_fox, v_g_out_dil, v_w_out, v_g_ffn, v_w_gate, v_w_up, v_w_down):
    args = dict(locals())
    small_names = [name for name, _, _ in SMALL_LAYOUT]
    big_names = ["w_in", "w_out", "w_gate", "w_up", "w_down"]
    small = {n: args[n][0] for n in small_names}

    as_rows = lambda n, w: _swap(w) if n in COL_SHARDED else w
    shards = {n: as_rows(n, args[n])[0].astype(BF16) for n in big_names}
    grad_x, parts, p_small = _device_step(x, loss_target, small, shards)

    grads, deltas, new_m, new_v = {}, {}, {}, {}
    for n in big_names:
        res = _adamw(parts[n], as_rows(n, args[n]), as_rows(n, args["m_" + n]), as_rows(n, args["v_" + n]),
                     name="adamw_" + n)
        grads[n], deltas[n], new_m[n], new_v[n] = [as_rows(n, r) for r in res]
    res = _adamw(p_small, _pack_small(small)[None], _pack_small({n: args["m_" + n][0] for n in small_names})[None],
                 _pack_small({n: args["v_" + n][0] for n in small_names})[None], name="adamw_small")
    loss = res[0][0, LOSS_ROW, 0]
    for dst, packed_res in zip((grads, deltas, new_m, new_v), res):
        for n, val in _unpack_small(packed_res[0], small).items():
            dst[n] = val[None]

    order = ["g_mix", "w_in", "b_forget", "g_q_fox", "g_k_fox", "g_q_dil", "g_k_dil", "g_out_fox", "g_out_dil",
             "w_out", "g_ffn", "w_gate", "w_up", "w_down"]
    return (loss, grad_x, *[grads[n] for n in order], *[deltas[n] for n in order],
            *[new_m[n] for n in order], *[new_v[n] for n in order])
```

```python
import functools
import math

import jax
import jax.numpy as jnp
import numpy as np
from jax import lax
from jax.experimental import pallas as pl
from jax.experimental.pallas import tpu as pltpu

F32 = jnp.float32
BF16 = jnp.bfloat16

D_MODEL = 1024
HEAD_DIM = 64
LANES = 128
N_PAIRS = D_MODEL // LANES
N_HEADS = 2 * N_PAIRS
N_HEADS_FOX = 8
W_GROUP = 512
D_FF = 2816
IN_COLS = 3080
DILATION_PAIRS = ((128, 1), (512, 4), (2048, 16))
ROPE_THETA = 500000.0
ROPE_DIM = 16
ROPE_HALF = ROPE_DIM // 2
EPS = 1e-6
NEG = -1e30
LOG2E = 1.4426950408889634
LN2 = 0.6931471805599453
AUG_ONE = 0
AUG_C = 3
N_DEV = 8

ADAM_LR = 0.001
ADAM_B1 = 0.9
ADAM_B2 = 0.999
ADAM_EPS = 1e-08
ADAM_WD = 0.01
ADAM_STEP = 10

ROW_BLOCK = 512
TOKEN_STEP = 2048
WIDE_BLOCK = D_FF // 2
ATT_BLOCK = 512
ATT_GROUP = 4
VMEM_LIMIT = 56 * 1024 * 1024
MATMUL_VMEM_BUDGET = 44 * 1024 * 1024

SMALL_ROWS = 32
SMALL_LAYOUT = (("g_mix", 0, 8), ("g_ffn", 8, 8), ("g_out_fox", 16, 4), ("g_out_dil", 20, 4),
                ("g_q_fox", 24, 1), ("g_k_fox", 25, 1), ("g_q_dil", 26, 1), ("g_k_dil", 27, 1),
                ("b_forget", 28, 1))
LOSS_ROW = 29


def _params(n_grid):
    return pltpu.CompilerParams(dimension_semantics=("arbitrary",) * n_grid, vmem_limit_bytes=VMEM_LIMIT)


def _divisor_block(n, cap):
    best = None
    for b in range(LANES, min(n, cap) + 1, LANES):
        if n % b == 0:
            best = b
    assert best is not None, n
    return best


def _split_dot(a, b_exact, terms):
    acc = None
    rest = a
    for _ in range(terms):
        hi = rest.astype(BF16)
        part = jnp.dot(hi, b_exact, preferred_element_type=F32)
        acc = part if acc is None else acc + part
        rest = rest - hi.astype(F32)
    return acc


def _split_dot_nt(a_exact, b, terms):
    acc = None
    rest = b
    for _ in range(terms):
        hi = rest.astype(BF16)
        part = _dot_nt(a_exact, hi)
        acc = part if acc is None else acc + part
        rest = rest - hi.astype(F32)
    return acc


def _dot_nt(a, b):
    return lax.dot_general(a, b, (((1,), (1,)), ((), ())), preferred_element_type=F32)


def _dot_tn(a, b):
    return lax.dot_general(a, b, (((0,), (0,)), ((), ())), preferred_element_type=F32)


class _Exchange:
    def __init__(self, items):
        self.items = items
        self.n = len(items)
        self.arrays = [a for a, _ in items]
        self.out_shape = [jax.ShapeDtypeStruct((N_DEV,) + tuple(a.shape[1:] if sc else a.shape), a.dtype)
                          for a, sc in items]
        self.specs = [pl.BlockSpec(memory_space=pl.ANY)] * self.n
        self.scratch = [pltpu.SemaphoreType.DMA((self.n, N_DEV - 1)), pltpu.SemaphoreType.DMA((self.n, N_DEV - 1)),
                        pltpu.SemaphoreType.DMA((self.n,))]

    def run(self, ins, outs, sems, first, last, compute):
        send_sems, recv_sems, local_sems = sems
        x, y, c = lax.axis_index("x"), lax.axis_index("y"), lax.axis_index("c")
        me = 4 * x + 2 * y + c
        local, remote = [], []
        for k, (_, scatter) in enumerate(self.items):
            own = ins[k].at[me] if scatter else ins[k]
            local.append(pltpu.make_async_copy(own, outs[k].at[me], local_sems.at[k]))
        for r in range(1, N_DEV):
            px = 1 - x if r & 4 else x
            py = 1 - y if r & 2 else y
            pc = 1 - c if r & 1 else c
            peer = 4 * px + 2 * py + pc
            for k, (_, scatter) in enumerate(self.items):
                src = ins[k].at[peer] if scatter else ins[k]
                remote.append(pltpu.make_async_remote_copy(
                    src_ref=src, dst_ref=outs[k].at[me],
                    send_sem=send_sems.at[k, r - 1], recv_sem=recv_sems.at[k, r - 1],
                    device_id=(px, py, pc), device_id_type=pl.DeviceIdType.MESH))

        def start():
            for cp in local + remote:
                cp.start()

        def finish():
            for cp in remote:
                cp.wait_recv()
            for cp in remote:
                cp.wait_send()
            for cp in local:
                cp.wait()

        _run_phases(first, last, start, compute, finish)


def _run_phases(first, last, start, compute, finish):
    if first is None:
        start()
        compute()
        finish()
    else:
        pl.when(first)(start)
        compute()
        pl.when(last)(finish)


class _ChipGather(_Exchange):
    def run(self, ins, outs, sems, first, last, compute):
        send_sems, recv_sems, local_sems = sems
        x, y, c = lax.axis_index("x"), lax.axis_index("y"), lax.axis_index("c")
        sibling = (x, y, 1 - c)
        chips = [(1 - x, y), (x, 1 - y), (1 - x, 1 - y)]
        slot = lambda px, py, pc: 4 * px + 2 * py + pc

        def copy(k, n, src, dst_slot, to):
            return pltpu.make_async_remote_copy(
                src_ref=src, dst_ref=outs[k].at[dst_slot], send_sem=send_sems.at[k, n], recv_sem=recv_sems.at[k, n],
                device_id=to, device_id_type=pl.DeviceIdType.MESH)

        local, own, passed, arrivals = [], [], [], []
        for k in range(self.n):
            me = slot(x, y, c)
            local.append(pltpu.make_async_copy(ins[k], outs[k].at[me], local_sems.at[k]))
            own.append(copy(k, 0, ins[k], me, sibling))
            arrivals.append(copy(k, 0, ins[k], slot(*sibling), sibling))
            for j, chip in enumerate(chips):
                theirs = slot(*chip, c)
                own.append(copy(k, 1 + j, ins[k], me, (*chip, c)))
                passed.append((copy(k, 1 + j, ins[k], theirs, sibling),
                               copy(k, 4 + j, outs[k].at[theirs], theirs, sibling)))
                arrivals.append(copy(k, 4 + j, ins[k], slot(*chip, 1 - c), sibling))

        def start():
            for cp in local + own:
                cp.start()

        def finish():
            for landed, onward in passed:
                landed.wait_recv()
                onward.start()
            for cp in arrivals:
                cp.wait_recv()
            for cp in own + [onward for _, onward in passed]:
                cp.wait_send()
            for cp in local:
                cp.wait()

        _run_phases(first, last, start, compute, finish)


def _grid_ends(grid):
    ids = [pl.program_id(d) for d in range(len(grid))]
    first = functools.reduce(jnp.logical_and, [i == 0 for i in ids])
    last = functools.reduce(jnp.logical_and, [i == g - 1 for i, g in zip(ids, grid)])
    return first, last


def _host(core, n_in, n_out, n_scratch, hosted, grid):
    if hosted is None:
        return core
    nh = hosted.n

    def body(*refs):
        ins, rest = refs[:n_in], refs[n_in:]
        h_ins, rest = rest[:nh], rest[nh:]
        outs, rest = rest[:n_out], rest[n_out:]
        h_outs, rest = rest[:nh], rest[nh:]
        scratch, sems = rest[:n_scratch], rest[n_scratch:]
        first, last = _grid_ends(grid)
        hosted.run(h_ins, h_outs, sems, first, last, lambda: core(*ins, *outs, *scratch))

    return body


def _hosted_parts(hosted):
    if hosted is None:
        return [], [], [], []
    return list(hosted.specs), list(hosted.out_shape), list(hosted.arrays), list(hosted.scratch)


def _exchange(name, items):
    ex = _Exchange(items)
    n = ex.n

    def body(*refs):
        ex.run(refs[:n], refs[n:2 * n], refs[2 * n:], None, None, lambda: None)

    return pl.pallas_call(
        body, name=name, out_shape=tuple(ex.out_shape), in_specs=ex.specs, out_specs=tuple(ex.specs),
        scratch_shapes=ex.scratch,
    )(*ex.arrays)


def _matmul_rows(a, w, *, name, out_dtype=F32):
    t, k = a.shape
    n = w.shape[1]
    assert w.shape[0] == k
    bt = _row_block(t)

    def body(a_ref, w_ref, o_ref):
        o_ref[...] = jnp.dot(a_ref[...], w_ref[...], preferred_element_type=F32).astype(o_ref.dtype)

    return pl.pallas_call(
        body, name=name, grid=(t // bt,),
        in_specs=[pl.BlockSpec((bt, k), lambda i: (i, 0)), pl.BlockSpec((k, n), lambda i: (0, 0))],
        out_specs=pl.BlockSpec((bt, n), lambda i: (i, 0)),
        out_shape=jax.ShapeDtypeStruct((t, n), out_dtype), compiler_params=_params(1),
    )(a, w)


def _matmul_tn(a, b, *, name):
    t, m = a.shape
    n = b.shape[1]
    bt = TOKEN_STEP if t % TOKEN_STEP == 0 else ROW_BLOCK
    bm = _divisor_block(m, WIDE_BLOCK)
    bn = _divisor_block(n, WIDE_BLOCK)
    steps = t // bt

    def body(a_ref, b_ref, o_ref, acc):
        step = pl.program_id(2)

        @pl.when(step == 0)
        def _():
            acc[...] = jnp.zeros_like(acc)

        acc[...] += _dot_tn(a_ref[...], b_ref[...])

        @pl.when(step == steps - 1)
        def _():
            o_ref[...] = acc[...].astype(o_ref.dtype)

    return pl.pallas_call(
        body, name=name, grid=(m // bm, n // bn, steps),
        in_specs=[pl.BlockSpec((bt, bm), lambda i, j, s: (s, i)), pl.BlockSpec((bt, bn), lambda i, j, s: (s, j))],
        out_specs=pl.BlockSpec((bm, bn), lambda i, j, s: (i, j)),
        out_shape=jax.ShapeDtypeStruct((m, n), BF16), scratch_shapes=[pltpu.VMEM((bm, bn), F32)],
        compiler_params=_params(3),
    )(a, b)


def _rmsnorm_fwd(x, g, *, group, name, hosted=None):
    t, w = x.shape
    bt = ROW_BLOCK

    def body(x_ref, g_ref, o_ref, x16_ref):
        x16_ref[...] = x_ref[...].astype(BF16)
        for s in range(0, w, group):
            xs = x_ref[:, s:s + group].astype(F32)
            r = lax.rsqrt(jnp.mean(xs * xs, axis=-1, keepdims=True) + EPS)
            o_ref[:, s:s + group] = (xs * r * g_ref[:, s:s + group]).astype(o_ref.dtype)

    grid = (t // bt,)
    h_specs, h_shapes, h_args, h_scratch = _hosted_parts(hosted)
    rows = pl.BlockSpec((bt, w), lambda i: (i, 0))
    return pl.pallas_call(
        _host(body, 2, 2, 0, hosted, grid), name=name, grid=grid,
        in_specs=[rows, pl.BlockSpec((1, w), lambda i: (0, 0))] + h_specs,
        out_specs=tuple([rows, rows] + h_specs),
        out_shape=tuple([jax.ShapeDtypeStruct((t, w), BF16)] * 2 + h_shapes),
        scratch_shapes=h_scratch, compiler_params=_params(1),
    )(x, g, *h_args)


def _norm_input_grad(terms, x, g, *, group, name, out_dtypes, resid=None, init=None, hosted=None, k_chunks=1):
    t, w = x.shape
    n_terms = len(terms)
    kc = [a.shape[1] // k_chunks for a, _, _ in terms]
    per_row = sum(c * a.dtype.itemsize for c, (a, _, _) in zip(kc, terms))
    per_row += w * sum(r.dtype.itemsize for r in (x, resid, init) if r is not None)
    per_row += w * sum(jnp.dtype(dt).itemsize for dt in out_dtypes)
    fixed = 2 * sum(w * c * 2 for c in kc)
    bt = next(b for b in (2 * ROW_BLOCK, ROW_BLOCK, ROW_BLOCK // 2, ROW_BLOCK // 4)
              if t % b == 0 and fixed + 2 * b * per_row + 5 * b * w * 4 <= MATMUL_VMEM_BUDGET)
    resid_at = 2 * n_terms + 2
    init_at = resid_at + (resid is not None)
    n_in = init_at + (init is not None)
    grid = (t // bt, k_chunks)

    def body(*refs):
        x_ref, g_ref = refs[2 * n_terms], refs[2 * n_terms + 1]
        dx_refs, dg_ref, dh_ref = refs[n_in:-2], refs[-2], refs[-1]
        chunk = pl.program_id(1)

        @pl.when((pl.program_id(0) == 0) & (chunk == 0))
        def _():
            dg_ref[...] = jnp.zeros_like(dg_ref)

        part = None
        for k in range(n_terms):
            if terms[k][2]:
                term = jnp.dot(refs[2 * k][...], refs[2 * k + 1][...], preferred_element_type=F32)
            else:
                term = _dot_nt(refs[2 * k][...], refs[2 * k + 1][...])
            part = term if part is None else part + term

        @pl.when(chunk == 0)
        def _():
            dh_ref[...] = part if init is None else refs[init_at][...] + part

        @pl.when(chunk > 0)
        def _():
            dh_ref[...] += part

        @pl.when(chunk == k_chunks - 1)
        def _():
            for s in range(0, w, group):
                xs = x_ref[:, s:s + group].astype(F32)
                dhs = dh_ref[:, s:s + group]
                r = lax.rsqrt(jnp.mean(xs * xs, axis=-1, keepdims=True) + EPS)
                xh = xs * r
                dg_ref[:, s:s + group] += jnp.sum(dhs * xh, axis=0, keepdims=True)
                dxh = dhs * g_ref[:, s:s + group]
                dx = r * (dxh - xh * jnp.mean(dxh * xh, axis=-1, keepdims=True))
                if resid is not None:
                    dx = refs[resid_at][:, s:s + group] + dx
                for dx_ref in dx_refs:
                    dx_ref[:, s:s + group] = dx.astype(dx_ref.dtype)

    row = pl.BlockSpec((bt, w), lambda i, k: (i, 0))
    vec = pl.BlockSpec((1, w), lambda i, k: (0, 0))
    in_specs, args = [], []
    for c, (a, wt, w_is_kn) in zip(kc, terms):
        assert wt.shape == ((a.shape[1], w) if w_is_kn else (w, a.shape[1]))
        w_spec = pl.BlockSpec((c, w), lambda i, k: (k, 0)) if w_is_kn else pl.BlockSpec((w, c), lambda i, k: (0, k))
        in_specs += [pl.BlockSpec((bt, c), lambda i, k: (i, k)), w_spec]
        args += [a, wt]
    extra = [r for r in (resid, init) if r is not None]
    in_specs += [row, vec] + [row] * len(extra)
    args += [x, g] + extra
    h_specs, h_shapes, h_args, h_scratch = _hosted_parts(hosted)
    return pl.pallas_call(
        _host(body, n_in, len(out_dtypes) + 1, 1, hosted, grid), name=name, grid=grid, in_specs=in_specs + h_specs,
        out_specs=tuple([row] * len(out_dtypes) + [vec] + h_specs),
        out_shape=tuple([jax.ShapeDtypeStruct((t, w), dt) for dt in out_dtypes] + [jax.ShapeDtypeStruct((1, w), F32)]
                        + h_shapes),
        scratch_shapes=[pltpu.VMEM((bt, w), F32)] + h_scratch, compiler_params=_params(2),
    )(*args, *h_args)


def _tile_plan(tile):
    is_q = tile < N_PAIRS
    is_dil = (tile % N_PAIRS) >= N_PAIRS // 2
    return is_q, is_dil, (0 if is_q else 2) + (1 if is_dil else 0)


def _proj_tile(kind, pair):
    return kind * N_PAIRS + pair


def _segment_ones():
    lane = np.arange(LANES)
    return jnp.asarray((lane[:, None] // HEAD_DIM) == (lane[None, :] // HEAD_DIM), BF16)


def _rope_tables(seq):
    inv_freq = jnp.power(jnp.float32(ROPE_THETA), -jnp.arange(ROPE_HALF, dtype=F32) * 2.0 / ROPE_DIM)
    ang = jnp.arange(seq).astype(F32)[:, None] * inv_freq[None, :]
    cos, sin = jnp.cos(ang), jnp.sin(ang)
    ones = jnp.ones((seq, HEAD_DIM - ROPE_DIM), F32)
    zeros = jnp.zeros((seq, HEAD_DIM - ROPE_DIM), F32)
    zh = jnp.zeros((seq, ROPE_HALF), F32)
    cos_t = jnp.concatenate([cos, cos, ones], axis=1)
    sin_a = jnp.concatenate([-sin, zh, zeros], axis=1)
    sin_b = jnp.concatenate([zh, sin, zeros], axis=1)
    return tuple(jnp.tile(tab, (1, 2)) for tab in (cos_t, sin_a, sin_b))


def _log_sigmoid(z):
    return jnp.minimum(z, 0.0) - jnp.log1p(jnp.exp(-jnp.abs(z)))


def _aug_placement():
    place = np.zeros((N_PAIRS, LANES, LANES), np.float32)
    for is_k in range(2):
        for pair in range(N_PAIRS // 2):
            for e in range(2):
                other = HEAD_DIM * (1 - e)
                ones_at = other + (AUG_C if is_k else AUG_ONE)
                c_at = other + (AUG_ONE if is_k else AUG_C)
                for n in range(3):
                    place[4 * is_k + pair, N_HEADS_FOX * n + 2 * pair + e, c_at + n] = -1.0 if is_k else 1.0
                    place[4 * is_k + pair, 3 * N_HEADS_FOX, ones_at + n] = 1.0
    return jnp.asarray(place, BF16)


def _qk_prep_fwd(h1, w_main_t, w_fa_t, b_pad, gains, rope, seq):
    t = h1.shape[0]
    bt = math.gcd(seq, 2 * ROW_BLOCK)
    nsb = seq // bt
    seg = _segment_ones()
    rr = np.arange(bt)
    tri = jnp.asarray(rr[:, None] <= rr[None, :], BF16)

    def body(h_ref, w_ref, wfa_ref, b_ref, g_ref, cos_ref, sa_ref, sb_ref, seg_ref, tri_ref, place_ref,
             p_ref, qk_ref, fa_ref, carry):
        @pl.when(pl.program_id(0) % nsb == 0)
        def _():
            carry[...] = jnp.zeros_like(carry)

        def project(first):
            cols = slice(first * LANES, (first + 2) * LANES)
            y = _dot_nt(h_ref[...], w_ref[cols, :]).astype(BF16)
            p_ref[:, cols] = y
            return y

        lane = lax.broadcasted_iota(jnp.int32, (bt, LANES), 1)
        fa = _dot_nt(h_ref[...], wfa_ref[...])
        fa_ref[...] = fa
        logf = jnp.where(lane < N_HEADS_FOX, _log_sigmoid(fa + b_ref[...]), 0.0)
        c_rows = _split_dot(logf.T[0:N_HEADS_FOX, :], tri_ref[...], 3) + carry[:, 0:1]
        carry[...] = jnp.broadcast_to(c_rows[:, bt - 1:bt], carry.shape)
        cblk = jnp.concatenate([c_rows, jnp.zeros((LANES - N_HEADS_FOX, bt), F32)], axis=0).T
        packed = jnp.where(lane == 3 * N_HEADS_FOX, 1.0, 0.0)
        rest = cblk * LOG2E
        for n in range(3):
            term = rest.astype(BF16).astype(F32)
            packed = packed + (pltpu.roll(term, N_HEADS_FOX * n, 1) if n else term)
            rest = rest - term
        packed = packed.astype(BF16)
        low = lane < HEAD_DIM

        ahead = project(0)
        for tile in range(2 * N_PAIRS):
            is_q, is_dil, grow = _tile_plan(tile)
            pair = tile % N_PAIRS
            assert tile == _proj_tile(0 if is_q else 1, pair)
            if tile % 2 == 0:
                both = ahead
                if tile + 2 < 2 * N_PAIRS:
                    ahead = project(tile + 2)
                if tile % 4 == 2:
                    project(2 * N_PAIRS + tile // 2 - 1)
            xs = both[:, (tile % 2) * LANES:(tile % 2 + 1) * LANES].astype(F32)
            r = lax.rsqrt(_split_dot(xs * xs, seg_ref[...], 2) * (1.0 / HEAD_DIM) + EPS)
            yv = xs * r * g_ref[grow:grow + 1, :]
            if is_dil:
                yv = (yv * cos_ref[...] + pltpu.roll(yv, LANES - ROPE_HALF, 1) * sa_ref[...]
                      + pltpu.roll(yv, ROPE_HALF, 1) * sb_ref[...])
                aug = jnp.zeros((bt, LANES), F32)
            else:
                aug = jnp.dot(packed, place_ref[(0 if is_q else N_PAIRS // 2) + pair], preferred_element_type=F32)
            if is_q:
                yv = yv * (HEAD_DIM ** -0.5 * LOG2E)
            dst = ((0 if is_q else N_HEADS) + 2 * pair) * LANES
            qk_ref[:, dst:dst + LANES] = jnp.where(low, yv, aug).astype(BF16)
            qk_ref[:, dst + LANES:dst + 2 * LANES] = jnp.where(low, aug, yv).astype(BF16)

    row128 = pl.BlockSpec((bt, LANES), lambda i: (i, 0))
    rope_spec = pl.BlockSpec((bt, LANES), lambda i: (i % nsb, 0))
    const = lambda shape: pl.BlockSpec(shape, lambda i: (0,) * len(shape))
    return pl.pallas_call(
        body, name="in_proj_qk_prep", grid=(t // bt,),
        in_specs=[pl.BlockSpec((bt, D_MODEL), lambda i: (i, 0)), const((MAIN_COLS, D_MODEL)),
                  const((LANES, D_MODEL)), const((1, LANES)), const((8, LANES)), rope_spec, rope_spec, rope_spec,
                  const((LANES, LANES)), const((bt, bt)), const((N_PAIRS, LANES, LANES))],
        out_specs=(pl.BlockSpec((bt, MAIN_COLS), lambda i: (i, 0)),
                   pl.BlockSpec((bt, 2 * N_HEADS * LANES), lambda i: (i, 0)), row128),
        out_shape=(jax.ShapeDtypeStruct((t, MAIN_COLS), BF16),
                   jax.ShapeDtypeStruct((t, 2 * N_HEADS * LANES), BF16), jax.ShapeDtypeStruct((t, LANES), F32)),
        scratch_shapes=[pltpu.VMEM((8, LANES), F32)], compiler_params=_params(1),
    )(h1, w_main_t, w_fa_t, b_pad, gains, *rope, seg, tri, _aug_placement())


def _qk_prep_bwd(dq, dk, dqx, dkx, dv, proj, fa, b_pad, gains, rope, seq):
    t = proj.shape[0]
    bt = ROW_BLOCK
    nsb = seq // bt
    nblk = t // bt
    seg = _segment_ones()
    rr = np.arange(bt)
    triu = jnp.asarray(rr[:, None] >= rr[None, :], BF16)

    def body(dq_ref, dk_ref, dqx_ref, dkx_ref, dv_ref, p_ref, fa_ref, b_ref, g_ref, cos_ref, sa_ref, sb_ref, seg_ref,
             triu_ref, dp_ref, dg_ref, db_ref, carry):
        step = pl.program_id(0)

        @pl.when(step == 0)
        def _():
            dg_ref[...] = jnp.zeros_like(dg_ref)
            db_ref[...] = jnp.zeros_like(db_ref)

        @pl.when(step % nsb == 0)
        def _():
            carry[...] = jnp.zeros_like(carry)

        for tile in range(2 * N_PAIRS):
            is_q, is_dil, grow = _tile_plan(tile)
            first = _proj_tile(0 if is_q else 1, tile % N_PAIRS) * LANES
            cols = slice(first, first + LANES)
            src = dq_ref if is_q else dk_ref
            half = slice((tile % N_PAIRS) * LANES, (tile % N_PAIRS + 1) * LANES)
            dy = src[:, half].astype(F32)
            dy = dy * (HEAD_DIM ** -0.5 if is_q else LN2)
            if is_dil:
                dy = (dy * cos_ref[...] + pltpu.roll(dy * sa_ref[...], ROPE_HALF, 1)
                      + pltpu.roll(dy * sb_ref[...], LANES - ROPE_HALF, 1))
            xs = p_ref[:, cols].astype(F32)
            r = lax.rsqrt(_split_dot(xs * xs, seg_ref[...], 2) * (1.0 / HEAD_DIM) + EPS)
            xh = xs * r
            dg_ref[tile:tile + 1, :] += jnp.sum(dy * xh, axis=0, keepdims=True)
            dxh = dy * g_ref[grow:grow + 1, :]
            seg_mean = _split_dot(dxh * xh, seg_ref[...], 2) * (1.0 / HEAD_DIM)
            dp_ref[:, cols] = (r * (dxh - xh * seg_mean)).astype(BF16)

        lane = lax.broadcasted_iota(jnp.int32, (bt, LANES), 1)
        dc = jnp.zeros((bt, LANES), F32)
        for h in range(N_HEADS_FOX):
            other = (h // 2) * LANES + HEAD_DIM * (1 - h % 2)
            row_sum = dqx_ref[:, other + AUG_C:other + AUG_C + 1]
            col_sum = dkx_ref[:, other + AUG_ONE:other + AUG_ONE + 1]
            dc = jnp.where(lane == h, row_sum - col_sum, dc)
        d_rows = _split_dot(dc.T[0:N_HEADS_FOX, :], triu_ref[...], 3) + carry[:, 0:1]
        carry[...] = jnp.broadcast_to(d_rows[:, 0:1], carry.shape)
        dlogf = jnp.concatenate([d_rows, jnp.zeros((LANES - N_HEADS_FOX, bt), F32)], axis=0).T
        z = fa_ref[...] + b_ref[...]
        dfa = dlogf * (1.0 / (1.0 + jnp.exp(z)))
        db_ref[0:1, :] += jnp.sum(dfa, axis=0, keepdims=True)
        for group in range(2):
            first = _proj_tile(2, group * (N_PAIRS // 2)) * LANES
            dp_ref[:, first:first + W_GROUP] = dv_ref[:, group * W_GROUP:(group + 1) * W_GROUP]
        dp_ref[:, MAIN_COLS:PROJ_COLS] = dfa.astype(BF16)

    rev = lambda i: nblk - 1 - i
    row = lambda w: pl.BlockSpec((bt, w), lambda i: (rev(i), 0))
    rope_spec = pl.BlockSpec((bt, LANES), lambda i: (rev(i) % nsb, 0))
    const = lambda shape: pl.BlockSpec(shape, lambda i: (0, 0))
    return pl.pallas_call(
        body, name="qk_prep_bwd", grid=(nblk,),
        in_specs=[row(D_MODEL), row(D_MODEL), row(W_GROUP), row(W_GROUP), row(D_MODEL), row(2 * D_MODEL), row(LANES),
                  const((1, LANES)), const((8, LANES)), rope_spec, rope_spec, rope_spec, const((LANES, LANES)),
                  const((bt, bt))],
        out_specs=(row(PROJ_COLS), const((2 * N_PAIRS, LANES)), const((8, LANES))),
        out_shape=(jax.ShapeDtypeStruct((t, PROJ_COLS), BF16),
                   jax.ShapeDtypeStruct((2 * N_PAIRS, LANES), F32), jax.ShapeDtypeStruct((8, LANES), F32)),
        scratch_shapes=[pltpu.VMEM((8, LANES), F32)], compiler_params=_params(1),
    )(dq, dk, dqx, dkx, dv, proj, fa, b_pad, gains, *rope, seg, triu)


def _bias_tables(seq, keys_first):
    nb = seq // ATT_BLOCK
    idx = np.arange(ATT_BLOCK)
    q_idx, k_idx = (idx[None, None, :], idx[None, :, None]) if keys_first else (idx[None, :, None], idx[None, None, :])
    dist = np.arange(nb)[:, None, None] * ATT_BLOCK + q_idx - k_idx
    causal = dist >= 0
    count = np.zeros(dist.shape, np.int32)
    for window, dilation in DILATION_PAIRS:
        count = count + (causal & (dist % dilation == 0) & (dist <= window))
    fox = np.where(causal, 0.0, NEG)
    dil = np.where(count == 3, math.log2(3.0), np.where(count == 2, 1.0, np.where(count == 1, 0.0, NEG)))
    return jnp.asarray(np.stack([fox, dil], axis=0), F32)


def _attn_specs(seq):
    nb = seq // ATT_BLOCK
    col = pl.BlockSpec((seq, LANES), lambda b, j: (b, j))
    heads = lambda off: pl.BlockSpec((seq, 2 * LANES), lambda b, j: (b, off + j))
    v_spec = pl.BlockSpec((seq, LANES), lambda b, j: (b, _proj_tile(2, j)))
    table_spec = pl.BlockSpec((1, nb, ATT_BLOCK, ATT_BLOCK), lambda b, j: (j // (N_PAIRS // 2), 0, 0, 0))
    return col, heads, v_spec, table_spec


def _head_lanes(e, shape, axis):
    pos = lax.broadcasted_iota(jnp.int32, shape, axis)
    return pos < HEAD_DIM if e == 0 else pos >= HEAD_DIM


def _attn_fwd(qk, proj, tables, seq, hosted=None):
    t = qk.shape[0]
    nb = seq // ATT_BLOCK
    blk = ATT_BLOCK

    def body(q_ref, k_ref, v_ref, tab_ref, o_ref, lse_ref):
        mine = [_head_lanes(e, (seq, LANES), 1) for e in range(2)]
        lane = lax.broadcasted_iota(jnp.int32, (seq, LANES), 1)
        v_aug = [jnp.where(mine[e], v_ref[...], (lane == HEAD_DIM * (1 - e)).astype(BF16)) for e in range(2)]
        def scores(i, e):
            heads_e = slice(e * LANES, (e + 1) * LANES)
            s = _dot_nt(q_ref[i * blk:(i + 1) * blk, heads_e], k_ref[0:(i + 1) * blk, heads_e])
            s = jnp.concatenate([s[:, jj * blk:(jj + 1) * blk] + tab_ref[0, i - jj] for jj in range(i + 1)], axis=1)
            return s, jnp.max(s, axis=1, keepdims=True)

        chains = [(i, e) for i in reversed(range(nb)) for e in range(2)]
        ahead = 2
        pending = [scores(*chain) for chain in chains[:ahead]]
        done = {}
        for n, (i, e) in enumerate(chains):
            s, m = pending.pop(0)
            if n + ahead < len(chains):
                pending.append(scores(*chains[n + ahead]))
            acc = jnp.dot(jnp.exp2(s - m).astype(BF16), v_aug[e][0:(i + 1) * blk], preferred_element_type=F32)
            ones_at = HEAD_DIM * (1 - e)
            l = acc[:, ones_at:ones_at + 1]
            done[e] = (acc / l, m + jnp.log2(l))
            if e == 1:
                rows = slice(i * blk, (i + 1) * blk)
                o_ref[rows, :] = jnp.where(mine[0][rows], done[0][0], done[1][0]).astype(o_ref.dtype)
                lse_ref[rows, :] = jnp.where(mine[0][rows], done[0][1], done[1][1])

    col, heads, v_spec, table_spec = _attn_specs(seq)
    grid = (t // seq, N_PAIRS)
    h_specs, h_shapes, h_args, h_scratch = _hosted_parts(hosted)
    return pl.pallas_call(
        _host(body, 4, 2, 0, hosted, grid), name="attn_fwd", grid=grid,
        in_specs=[heads(0), heads(N_PAIRS), v_spec, table_spec] + h_specs,
        out_specs=tuple([col, col] + h_specs),
        out_shape=tuple([jax.ShapeDtypeStruct((t, D_MODEL), BF16), jax.ShapeDtypeStruct((t, D_MODEL), F32)] + h_shapes),
        scratch_shapes=h_scratch, compiler_params=_params(2),
    )(qk, qk, proj, tables, *h_args)


def _attn_bwd(qk, proj, tables, o, lse, do, seq, hosted=None):
    t = qk.shape[0]
    nb = seq // ATT_BLOCK
    blk = ATT_BLOCK
    group = math.gcd(nb, ATT_GROUP)

    def body(q_ref, k_ref, v_ref, tab_ref, o_ref, lse_ref, do_ref,
             dq_ref, dk_ref, dv_ref, dqx_ref, dkx_ref, dk_acc, dv_acc):
        mine = [_head_lanes(e, (blk, LANES), 1) for e in range(2)]
        top = _head_lanes(0, (LANES, blk), 0)
        head_rows = lax.broadcasted_iota(jnp.int32, (8, LANES), 0)
        head_of_lane = lax.broadcasted_iota(jnp.int32, (8, LANES), 1) // HEAD_DIM
        head_sel = (head_rows == head_of_lane).astype(BF16)
        dk_acc[...] = jnp.zeros_like(dk_acc)
        dv_acc[...] = jnp.zeros_like(dv_acc)

        def block_rows(i):
            return pl.ds(pl.multiple_of(i * blk, blk), blk)

        def q_group(g, _):
            base = g * group
            qs, doe, delta, lse_e = [], [], [], []
            for b in range(group):
                rows = block_rows(base + b)
                qs.append([q_ref[rows, e * LANES:(e + 1) * LANES] for e in range(2)])
                do_blk = do_ref[rows, :]
                doe.append([jnp.where(mine[e], do_blk, jnp.zeros_like(do_blk)) for e in range(2)])
                delta_t = _split_dot_nt(head_sel, do_blk.astype(F32) * o_ref[rows, :].astype(F32), 3)
                lse_t = _split_dot_nt(head_sel, lse_ref[rows, :], 3) * (1.0 / HEAD_DIM)
                delta.append([delta_t[e:e + 1, :] for e in range(2)])
                lse_e.append([lse_t[e:e + 1, :] for e in range(2)])

            def key_block(dq_t, jj, members):
                krows = block_rows(jj)
                v = v_ref[krows, :]
                dq_t = [list(d) for d in dq_t]
                lo, hi = slice(0, blk // 2), slice(blk // 2, blk)
                dv_part = [None, None]
                add = lambda acc, part: part if acc is None else acc + part

                def probs(k_sub, v_sub, keys, queries, b, e, dist):
                    q_sub, do_sub = qs[b][e][queries], doe[b][e][queries]
                    p_t = jnp.exp2(_dot_nt(k_sub, q_sub) + tab_ref[0, dist, keys, queries] - lse_e[b][e][:, queries])
                    ds_t = (p_t * (_dot_nt(v_sub, do_sub) - delta[b][e][:, queries])).astype(BF16)
                    return p_t.astype(BF16), ds_t, q_sub, do_sub, k_sub

                def outputs(tile):
                    p_t, ds_t, q_sub, do_sub, k_sub = tile
                    return (jnp.dot(p_t, do_sub, preferred_element_type=F32),
                            jnp.dot(ds_t, q_sub, preferred_element_type=F32), _dot_tn(k_sub, ds_t))

                for e in range(2):
                    k_e = k_ref[krows, e * LANES:(e + 1) * LANES]
                    dk_part = [None, None]
                    tiles = []
                    for b, dist in members:
                        if isinstance(dist, int) and dist == 0:
                            tiles.append((b, probs(k_e[lo], v[lo], lo, slice(0, blk), b, e, dist),
                                          probs(k_e[hi], v[hi], hi, hi, b, e, dist)))
                        else:
                            tiles.append((b, probs(k_e, v, slice(0, blk), slice(0, blk), b, e, dist), None))
                    for b, first, second in tiles:
                        if second is not None:
                            dv_a, dk_a, dq_a = outputs(first)
                            dv_b, dk_b, dq_b = outputs(second)
                            halves = ((dv_a, dk_a), (dv_b, dk_b))
                            dq = jnp.concatenate([dq_a[:, lo], dq_a[:, hi] + dq_b], axis=1)
                        else:
                            dv_f, dk_f, dq = outputs(first)
                            halves = ((dv_f[lo], dk_f[lo]), (dv_f[hi], dk_f[hi]))
                        for n, (dv_h, dk_h) in enumerate(halves):
                            dv_part[n] = add(dv_part[n], dv_h)
                            dk_part[n] = add(dk_part[n], dk_h)
                        dq_t[b][e] = dq_t[b][e] + dq
                    dk_acc[e, krows, :] += jnp.concatenate(dk_part, axis=0)
                dv_acc[krows, :] += jnp.concatenate(dv_part, axis=0)
                return tuple(tuple(d) for d in dq_t)

            zacc = jnp.zeros((LANES, blk), F32)
            dq_t = tuple((zacc, zacc) for _ in range(group))
            dq_t = lax.fori_loop(
                0, base, lambda jj, st: key_block(st, jj, [(b, base + b - jj) for b in range(group)]), dq_t)
            for a in range(group):
                dq_t = key_block(dq_t, base + a, [(b, b - a) for b in range(a, group)])
            for b in range(group):
                rows = block_rows(base + b)
                dq_ref[rows, :] = jnp.where(top, dq_t[b][0], dq_t[b][1]).T.astype(BF16)
                dqx_ref[rows, :] = jnp.where(top, dq_t[b][1], dq_t[b][0]).T
            return 0

        lax.fori_loop(0, nb // group, q_group, 0)
        lo = _head_lanes(0, (seq, LANES), 1)
        dk_ref[...] = jnp.where(lo, dk_acc[0], dk_acc[1]).astype(BF16)
        dkx_ref[...] = jnp.where(lo, dk_acc[1], dk_acc[0])
        dv_ref[...] = dv_acc[...].astype(dv_ref.dtype)

    col, heads, v_spec, table_spec = _attn_specs(seq)
    grid = (t // seq, N_PAIRS)
    h_specs, h_shapes, h_args, h_scratch = _hosted_parts(hosted)
    f32_out = jax.ShapeDtypeStruct((t, D_MODEL), F32)
    return pl.pallas_call(
        _host(body, 7, 5, 2, hosted, grid), name="attn_bwd", grid=grid,
        in_specs=[heads(0), heads(N_PAIRS), v_spec, table_spec, col, col, col] + h_specs,
        out_specs=tuple([col] * 5 + h_specs),
        out_shape=tuple([jax.ShapeDtypeStruct((t, D_MODEL), BF16)] * 3 + [f32_out, f32_out] + h_shapes),
        scratch_shapes=[pltpu.VMEM((2, seq, LANES), F32), pltpu.VMEM((seq, LANES), F32)] + h_scratch,
        compiler_params=_params(2),
    )(qk, qk, proj, tables, o, lse, do, *h_args)


def _row_block(t):
    return 2 * ROW_BLOCK if t % (2 * ROW_BLOCK) == 0 else ROW_BLOCK


def _out_proj_ffn_norm(o, g_out, w_out, x, g_ffn):
    t = o.shape[0]
    bt = _row_block(t)

    def body(o_ref, go_ref, w_ref, x_ref, gf_ref, on_ref, x2_ref, h2_ref):
        for s in range(0, D_MODEL, W_GROUP):
            os_ = o_ref[:, s:s + W_GROUP].astype(F32)
            r = lax.rsqrt(jnp.mean(os_ * os_, axis=-1, keepdims=True) + EPS)
            on_ref[:, s:s + W_GROUP] = (os_ * r * go_ref[:, s:s + W_GROUP]).astype(BF16)
        x2 = x_ref[...] + jnp.dot(on_ref[...], w_ref[...], preferred_element_type=F32)
        x2_ref[...] = x2.astype(BF16)
        r2 = lax.rsqrt(jnp.mean(x2 * x2, axis=-1, keepdims=True) + EPS)
        h2_ref[...] = (x2 * r2 * gf_ref[...]).astype(BF16)

    row = pl.BlockSpec((bt, D_MODEL), lambda i: (i, 0))
    vec = pl.BlockSpec((1, D_MODEL), lambda i: (0, 0))
    return pl.pallas_call(
        body, name="out_proj", grid=(t // bt,),
        in_specs=[row, vec, pl.BlockSpec((D_MODEL, D_MODEL), lambda i: (0, 0)), row, vec],
        out_specs=(row, row, row),
        out_shape=tuple([jax.ShapeDtypeStruct((t, D_MODEL), BF16)] * 3),
        compiler_params=_params(1),
    )(o, g_out, w_out, x, g_ffn)


def _ffn_gate_up(h2, w_gate_t, w_up_t):
    t = h2.shape[0]
    bt = _row_block(t)
    bn = _divisor_block(D_FF, WIDE_BLOCK)

    def body(h_ref, wg_ref, wu_ref, a_ref, u_ref, f_ref):
        a = _dot_nt(h_ref[...], wg_ref[...])
        u = _dot_nt(h_ref[...], wu_ref[...])
        a_ref[...] = a.astype(BF16)
        u_ref[...] = u.astype(BF16)
        f_ref[...] = (a * jax.nn.sigmoid(a) * u).astype(BF16)

    blk = pl.BlockSpec((bt, bn), lambda j, i: (i, j))
    w_blk = pl.BlockSpec((bn, D_MODEL), lambda j, i: (j, 0))
    shape = jax.ShapeDtypeStruct((t, D_FF), BF16)
    return pl.pallas_call(
        body, name="ffn_gate_up", grid=(D_FF // bn, t // bt),
        in_specs=[pl.BlockSpec((bt, D_MODEL), lambda j, i: (i, 0)), w_blk, w_blk],
        out_specs=(blk, blk, blk), out_shape=(shape, shape, shape), compiler_params=_params(2),
    )(h2, w_gate_t, w_up_t)


def _ffn_down_grad(dy16, w_down, a, u):
    t = a.shape[0]
    bt = _row_block(t)
    bn = _divisor_block(D_FF, WIDE_BLOCK)

    def body(dy_ref, w_ref, a_ref, u_ref, da_ref, du_ref):
        df = _dot_nt(dy_ref[...], w_ref[...])
        av = a_ref[...].astype(F32)
        sg = jax.nn.sigmoid(av)
        da_ref[...] = (df * u_ref[...].astype(F32) * sg * (1.0 + av * (1.0 - sg))).astype(BF16)
        du_ref[...] = (df * av * sg).astype(BF16)

    blk = pl.BlockSpec((bt, bn), lambda j, i: (i, j))
    shape = jax.ShapeDtypeStruct((t, D_FF), BF16)
    return pl.pallas_call(
        body, name="d_ffn_down", grid=(D_FF // bn, t // bt),
        in_specs=[pl.BlockSpec((bt, D_MODEL), lambda j, i: (i, 0)), pl.BlockSpec((bn, D_MODEL), lambda j, i: (j, 0)),
                  blk, blk],
        out_specs=(blk, blk), out_shape=(shape, shape), compiler_params=_params(2),
    )(dy16, w_down, a, u)


def _ffn_down_loss(f, w_down, x2, target):
    t, w = x2.shape
    bt = _row_block(t)

    def body(f_ref, w_ref, x_ref, t_ref, dy16_ref, loss_ref):
        @pl.when(pl.program_id(0) == 0)
        def _():
            loss_ref[...] = jnp.zeros_like(loss_ref)

        err = (x_ref[...] + jnp.dot(f_ref[...], w_ref[...], preferred_element_type=F32)) - t_ref[...]
        dy16_ref[...] = (err * (1.0 / w)).astype(BF16)
        loss_ref[...] += 0.5 * jnp.sum(jnp.mean(err * err, axis=-1, keepdims=True), axis=0, keepdims=True)

    row = pl.BlockSpec((bt, w), lambda i: (i, 0))
    return pl.pallas_call(
        body, name="ffn_down_loss", grid=(t // bt,),
        in_specs=[pl.BlockSpec((bt, D_FF), lambda i: (i, 0)), pl.BlockSpec((D_FF, w), lambda i: (0, 0)), row, row],
        out_specs=(row, pl.BlockSpec((8, LANES), lambda i: (0, 0))),
        out_shape=(jax.ShapeDtypeStruct((t, w), BF16), jax.ShapeDtypeStruct((8, LANES), F32)),
        compiler_params=_params(1),
    )(f, w_down, x2, target)


def _adamw(parts, w, m, v, *, name):
    _, rows, cols = w.shape
    br = rows if rows <= 512 else 256
    assert rows % br == 0

    def body(p_ref, w_ref, m_ref, v_ref, g_ref, d_ref, nm_ref, nv_ref):
        g = p_ref[0].astype(F32)
        for r in range(1, N_DEV):
            g = g + p_ref[r].astype(F32)
        m2 = ADAM_B1 * m_ref[0] + (1.0 - ADAM_B1) * g
        v2 = ADAM_B2 * v_ref[0] + (1.0 - ADAM_B2) * jnp.square(g)
        m_hat = m2 / (1.0 - ADAM_B1 ** ADAM_STEP)
        v_hat = v2 / (1.0 - ADAM_B2 ** ADAM_STEP)
        g_ref[0] = g
        d_ref[0] = -ADAM_LR * (m_hat / (jnp.sqrt(v_hat) + ADAM_EPS) + ADAM_WD * w_ref[0])
        nm_ref[0] = m2
        nv_ref[0] = v2

    blk = pl.BlockSpec((1, br, cols), lambda i: (0, i, 0))
    shape = jax.ShapeDtypeStruct((1, rows, cols), F32)
    return pl.pallas_call(
        body, name=name, grid=(rows // br,),
        in_specs=[pl.BlockSpec((N_DEV, br, cols), lambda i: (0, i, 0)), blk, blk, blk],
        out_specs=(blk, blk, blk, blk), out_shape=(shape, shape, shape, shape), compiler_params=_params(1),
    )(parts, w, m, v)


_QA, _KA, _VA, _FA, _QD, _KD, _VD = (0, 512), (512, 1024), (1024, 1536), (1536, 1544), (1544, 2056), (2056, 2568), (2568, 3080)
_MAIN_ORDER = (_QA, _QD, _KA, _KD, _VA, _VD)
MAIN_COLS = 3 * D_MODEL
PROJ_COLS = MAIN_COLS + LANES
COL_SHARDED = ("w_in", "w_gate", "w_up")


def _swap(w):
    return jnp.transpose(w, (0, 2, 1))


def _w_in_to_kernel(w_t):
    main = jnp.concatenate([w_t[a:b] for a, b in _MAIN_ORDER], axis=0)
    forget = jnp.pad(w_t[_FA[0]:_FA[1]], ((0, LANES - N_HEADS_FOX), (0, 0)))
    return main, forget


def _w_in_from_kernel(g_t):
    pos = {span: i * W_GROUP for i, span in enumerate(_MAIN_ORDER)}
    parts = []
    for span in (_QA, _KA, _VA, _FA, _QD, _KD, _VD):
        if span == _FA:
            parts.append(g_t[MAIN_COLS:MAIN_COLS + N_HEADS_FOX])
        else:
            parts.append(g_t[pos[span]:pos[span] + W_GROUP])
    return jnp.concatenate(parts, axis=0)


def _pack_small(vals):
    rows = []
    for name, _, n_rows in SMALL_LAYOUT:
        flat = vals[name].reshape(-1).astype(F32)
        rows.append(jnp.pad(flat, (0, n_rows * LANES - flat.shape[0])).reshape(n_rows, LANES))
    packed = jnp.concatenate(rows, axis=0)
    return jnp.pad(packed, ((0, SMALL_ROWS - packed.shape[0]), (0, 0)))


def _unpack_small(packed, like):
    out = {}
    for name, row, n_rows in SMALL_LAYOUT:
        n = like[name].size
        out[name] = packed[row:row + n_rows].reshape(-1)[:n].reshape(like[name].shape)
    return out


def _device_step(x, target, small, shards):
    bsz, seq, _ = x.shape
    t = bsz * seq
    xf = x.reshape(t, D_MODEL)
    tf = target.reshape(t, D_MODEL)
    row = lambda v: v.reshape(1, -1)
    g_out = jnp.concatenate([small["g_out_fox"], small["g_out_dil"]]).reshape(1, D_MODEL)
    gains = jnp.concatenate(
        [jnp.tile(small[n].reshape(1, HEAD_DIM), (1, 2)) for n in ("g_q_fox", "g_q_dil", "g_k_fox", "g_k_dil")]
        + [jnp.zeros((4, LANES), F32)], axis=0)
    b_pad = jnp.pad(small["b_forget"].reshape(1, N_HEADS_FOX), ((0, 0), (0, LANES - N_HEADS_FOX)))
    rope = _rope_tables(seq)
    tables_qk = _bias_tables(seq, keys_first=False)
    tables_kq = _bias_tables(seq, keys_first=True)

    h1, x16, g_in = _rmsnorm_fwd(xf, row(small["g_mix"]), group=D_MODEL, name="norm_mix",
                                 hosted=_ChipGather([(shards["w_in"], False)]))
    w_main_t, w_fa_t = _w_in_to_kernel(g_in.reshape(IN_COLS, D_MODEL))
    w_in_all_t = jnp.concatenate([w_main_t, w_fa_t], axis=0)
    proj, qk, fa = _qk_prep_fwd(h1, w_main_t, w_fa_t, b_pad, gains, rope, seq)
    late = _Exchange([(shards[n], False) for n in ("w_out", "w_gate", "w_up", "w_down")])
    o, lse, g_out_w, g_gate, g_up, g_down = _attn_fwd(qk, proj, tables_qk, seq, hosted=late)
    w_out = g_out_w.reshape(D_MODEL, D_MODEL)
    w_gate_t = g_gate.reshape(D_FF, D_MODEL)
    w_up_t = g_up.reshape(D_FF, D_MODEL)
    w_down = g_down.reshape(D_FF, D_MODEL)
    on, x2, h2 = _out_proj_ffn_norm(o, g_out, w_out, x16, row(small["g_ffn"]))
    a, u, f = _ffn_gate_up(h2, w_gate_t, w_up_t)
    dy16, loss_tile = _ffn_down_loss(f, w_down, x2, tf)

    da, du = _ffn_down_grad(dy16, w_down, a, u)
    gw_down = _matmul_tn(f, dy16, name="gw_down")
    gw_gate_t = _matmul_tn(da, h2, name="gw_gate")
    gw_up_t = _matmul_tn(du, h2, name="gw_up")
    dh2_gate = _matmul_rows(da, w_gate_t, name="d_ffn_gate", out_dtype=BF16)
    dx2_16, dg_ffn = _norm_input_grad([(du, w_up_t, True)], x2, row(small["g_ffn"]), group=D_MODEL,
                                      name="d_ffn_up", out_dtypes=(BF16,), resid=dy16, init=dh2_gate)
    gw_out = _matmul_tn(on, dx2_16, name="gw_out")
    do, dg_out = _norm_input_grad([(dx2_16, w_out, False)], o, g_out, group=W_GROUP, name="d_out_proj",
                                  out_dtypes=(BF16,))

    shard_rows = lambda g: g.reshape(N_DEV, g.shape[0] // N_DEV, g.shape[1])
    ffn_grads = _Exchange([(shard_rows(g), True) for g in (gw_out, gw_gate_t, gw_up_t, gw_down)])
    dq, dk, dv, dqx, dkx, p_out, p_gate, p_up, p_down = _attn_bwd(qk, proj, tables_kq, o, lse, do, seq, hosted=ffn_grads)
    dproj, dgains, db = _qk_prep_bwd(dq, dk, dqx, dkx, dv, proj, fa, b_pad, gains, rope, seq)
    gw_in_t = _matmul_tn(dproj, h1, name="gw_in")
    in_grad = _Exchange([(shard_rows(_w_in_from_kernel(gw_in_t)), True)])
    dx, dg_mix, p_in = _norm_input_grad([(dproj, w_in_all_t, True)], x16, row(small["g_mix"]), group=D_MODEL,
                                        name="d_in_proj", out_dtypes=(F32,), resid=dx2_16, hosted=in_grad)

    fold = lambda rows: jnp.sum(rows[:, :HEAD_DIM] + rows[:, HEAD_DIM:], axis=0)
    half = N_PAIRS // 2
    gsmall = {
        "g_mix": dg_mix, "g_ffn": dg_ffn, "g_out_fox": dg_out[0, :W_GROUP], "g_out_dil": dg_out[0, W_GROUP:],
        "g_q_fox": fold(dgains[0:half]), "g_q_dil": fold(dgains[half:N_PAIRS]),
        "g_k_fox": fold(dgains[N_PAIRS:N_PAIRS + half]), "g_k_dil": fold(dgains[N_PAIRS + half:]),
        "b_forget": db[0, :N_HEADS_FOX],
    }
    packed = _pack_small(gsmall).at[LOSS_ROW].set(loss_tile[0])
    (p_small,) = _exchange("small_exchange", [(packed, False)])
    parts = {"w_in": p_in, "w_out": p_out, "w_gate": p_gate, "w_up": p_up, "w_down": p_down}
    return dx.reshape(x.shape), parts, p_small


def kernel(x, g_mix, w_in, b_forget, g_q_fox, g_k_fox, g_q_dil, g_k_dil, g_out_fox, g_out_dil, w_out, g_ffn, w_gate, w_up, w_down, loss_target, m_g_mix, m_w_in, m_b_forget, m_g_q_fox, m_g_k_fox, m_g_q_dil, m_g_k_dil, m_g_out_fox, m_g_out_dil, m_w_out, m_g_ffn, m_w_gate, m_w_up, m_w_down, v_g_mix, v_w_in, v_b_forget, v_g_q_fox, v_g_k_fox, v_g_q_dil, v_g_k_dil, v_g_out_fox, v_g_out_dil, v_w_out, v_g_ffn, v_w_gate, v_w_up, v_w_down):
    args = dict(locals())
    small_names = [name for name, _, _ in SMALL_LAYOUT]
    big_names = ["w_in", "w_out", "w_gate", "w_up", "w_down"]
    small = {n: args[n][0] for n in small_names}

    as_rows = lambda n, w: _swap(w) if n in COL_SHARDED else w
    shards = {n: as_rows(n, args[n])[0].astype(BF16) for n in big_names}
    grad_x, parts, p_small = _device_step(x, loss_target, small, shards)

    grads, deltas, new_m, new_v = {}, {}, {}, {}
    for n in big_names:
        res = _adamw(parts[n], as_rows(n, args[n]), as_rows(n, args["m_" + n]), as_rows(n, args["v_" + n]),
                     name="adamw_" + n)
        grads[n], deltas[n], new_m[n], new_v[n] = [as_rows(n, r) for r in res]
    res = _adamw(p_small, _pack_small(small)[None], _pack_small({n: args["m_" + n][0] for n in small_names})[None],
                 _pack_small({n: args["v_" + n][0] for n in small_names})[None], name="adamw_small")
    loss = res[0][0, LOSS_ROW, 0]
    for dst, packed_res in zip((grads, deltas, new_m, new_v), res):
        for n, val in _unpack_small(packed_res[0], small).items():
            dst[n] = val[None]

    order = ["g_mix", "w_in", "b_forget", "g_q_fox", "g_k_fox", "g_q_dil", "g_k_dil", "g_out_fox", "g_out_dil",
             "w_out", "g_ffn", "w_gate", "w_up", "w_down"]
    return (loss, grad_x, *[grads[n] for n in order], *[deltas[n] for n in order],
            *[new_m[n] for n in order], *[new_v[n] for n in order])
```

```python
import functools
import math

import jax
import jax.numpy as jnp
import numpy as np
from jax import lax
from jax.experimental import pallas as pl
from jax.experimental.pallas import tpu as pltpu

F32 = jnp.float32
BF16 = jnp.bfloat16

D_MODEL = 1024
HEAD_DIM = 64
LANES = 128
N_PAIRS = D_MODEL // LANES
N_HEADS = 2 * N_PAIRS
N_HEADS_FOX = 8
W_GROUP = 512
D_FF = 2816
IN_COLS = 3080
DILATION_PAIRS = ((128, 1), (512, 4), (2048, 16))
ROPE_THETA = 500000.0
ROPE_DIM = 16
ROPE_HALF = ROPE_DIM // 2
EPS = 1e-6
NEG = -1e30
LOG2E = 1.4426950408889634
LN2 = 0.6931471805599453
AUG_ONE = 0
AUG_C = 3
N_DEV = 8

ADAM_LR = 0.001
ADAM_B1 = 0.9
ADAM_B2 = 0.999
ADAM_EPS = 1e-08
ADAM_WD = 0.01
ADAM_STEP = 10

ROW_BLOCK = 512
TOKEN_STEP = 2048
WIDE_BLOCK = D_FF // 2
ATT_BLOCK = 512
ATT_GROUP = 4
VMEM_LIMIT = 56 * 1024 * 1024
MATMUL_VMEM_BUDGET = 44 * 1024 * 1024

SMALL_ROWS = 32
SMALL_LAYOUT = (("g_mix", 0, 8), ("g_ffn", 8, 8), ("g_out_fox", 16, 4), ("g_out_dil", 20, 4),
                ("g_q_fox", 24, 1), ("g_k_fox", 25, 1), ("g_q_dil", 26, 1), ("g_k_dil", 27, 1),
                ("b_forget", 28, 1))
LOSS_ROW = 29


def _params(n_grid):
    return pltpu.CompilerParams(dimension_semantics=("arbitrary",) * n_grid, vmem_limit_bytes=VMEM_LIMIT)


def _divisor_block(n, cap):
    best = None
    for b in range(LANES, min(n, cap) + 1, LANES):
        if n % b == 0:
            best = b
    assert best is not None, n
    return best


def _split_dot(a, b_exact, terms):
    acc = None
    rest = a
    for _ in range(terms):
        hi = rest.astype(BF16)
        part = jnp.dot(hi, b_exact, preferred_element_type=F32)
        acc = part if acc is None else acc + part
        rest = rest - hi.astype(F32)
    return acc


def _split_dot_nt(a_exact, b, terms):
    acc = None
    rest = b
    for _ in range(terms):
        hi = rest.astype(BF16)
        part = _dot_nt(a_exact, hi)
        acc = part if acc is None else acc + part
        rest = rest - hi.astype(F32)
    return acc


def _dot_nt(a, b):
    return lax.dot_general(a, b, (((1,), (1,)), ((), ())), preferred_element_type=F32)


def _dot_tn(a, b):
    return lax.dot_general(a, b, (((0,), (0,)), ((), ())), preferred_element_type=F32)


class _Exchange:
    def __init__(self, items):
        self.items = items
        self.n = len(items)
        self.arrays = [a for a, _ in items]
        self.out_shape = [jax.ShapeDtypeStruct((N_DEV,) + tuple(a.shape[1:] if sc else a.shape), a.dtype)
                          for a, sc in items]
        self.specs = [pl.BlockSpec(memory_space=pl.ANY)] * self.n
        self.scratch = [pltpu.SemaphoreType.DMA((self.n, N_DEV - 1)), pltpu.SemaphoreType.DMA((self.n, N_DEV - 1)),
                        pltpu.SemaphoreType.DMA((self.n,))]

    def run(self, ins, outs, sems, first, last, compute):
        send_sems, recv_sems, local_sems = sems
        x, y, c = lax.axis_index("x"), lax.axis_index("y"), lax.axis_index("c")
        me = 4 * x + 2 * y + c
        local, remote = [], []
        for k, (_, scatter) in enumerate(self.items):
            own = ins[k].at[me] if scatter else ins[k]
            local.append(pltpu.make_async_copy(own, outs[k].at[me], local_sems.at[k]))
        for r in range(1, N_DEV):
            px = 1 - x if r & 4 else x
            py = 1 - y if r & 2 else y
            pc = 1 - c if r & 1 else c
            peer = 4 * px + 2 * py + pc
            for k, (_, scatter) in enumerate(self.items):
                src = ins[k].at[peer] if scatter else ins[k]
                remote.append(pltpu.make_async_remote_copy(
                    src_ref=src, dst_ref=outs[k].at[me],
                    send_sem=send_sems.at[k, r - 1], recv_sem=recv_sems.at[k, r - 1],
                    device_id=(px, py, pc), device_id_type=pl.DeviceIdType.MESH))

        def start():
            for cp in local + remote:
                cp.start()

        def finish():
            for cp in remote:
                cp.wait_recv()
            for cp in remote:
                cp.wait_send()
            for cp in local:
                cp.wait()

        _run_phases(first, last, start, compute, finish)


def _run_phases(first, last, start, compute, finish):
    if first is None:
        start()
        compute()
        finish()
    else:
        pl.when(first)(start)
        compute()
        pl.when(last)(finish)


class _ChipGather(_Exchange):
    def run(self, ins, outs, sems, first, last, compute):
        send_sems, recv_sems, local_sems = sems
        x, y, c = lax.axis_index("x"), lax.axis_index("y"), lax.axis_index("c")
        sibling = (x, y, 1 - c)
        chips = [(1 - x, y), (x, 1 - y), (1 - x, 1 - y)]
        slot = lambda px, py, pc: 4 * px + 2 * py + pc

        def copy(k, n, src, dst_slot, to):
            return pltpu.make_async_remote_copy(
                src_ref=src, dst_ref=outs[k].at[dst_slot], send_sem=send_sems.at[k, n], recv_sem=recv_sems.at[k, n],
                device_id=to, device_id_type=pl.DeviceIdType.MESH)

        local, own, passed, arrivals = [], [], [], []
        for k in range(self.n):
            me = slot(x, y, c)
            local.append(pltpu.make_async_copy(ins[k], outs[k].at[me], local_sems.at[k]))
            own.append(copy(k, 0, ins[k], me, sibling))
            arrivals.append(copy(k, 0, ins[k], slot(*sibling), sibling))
            for j, chip in enumerate(chips):
                theirs = slot(*chip, c)
                own.append(copy(k, 1 + j, ins[k], me, (*chip, c)))
                passed.append((copy(k, 1 + j, ins[k], theirs, sibling),
                               copy(k, 4 + j, outs[k].at[theirs], theirs, sibling)))
                arrivals.append(copy(k, 4 + j, ins[k], slot(*chip, 1 - c), sibling))

        def start():
            for cp in local + own:
                cp.start()

        def finish():
            for landed, onward in passed:
                landed.wait_recv()
                onward.start()
            for cp in arrivals:
                cp.wait_recv()
            for cp in own + [onward for _, onward in passed]:
                cp.wait_send()
            for cp in local:
                cp.wait()

        _run_phases(first, last, start, compute, finish)


def _grid_ends(grid):
    ids = [pl.program_id(d) for d in range(len(grid))]
    first = functools.reduce(jnp.logical_and, [i == 0 for i in ids])
    last = functools.reduce(jnp.logical_and, [i == g - 1 for i, g in zip(ids, grid)])
    return first, last


def _host(core, n_in, n_out, n_scratch, hosted, grid):
    if hosted is None:
        return core
    nh = hosted.n

    def body(*refs):
        ins, rest = refs[:n_in], refs[n_in:]
        h_ins, rest = rest[:nh], rest[nh:]
        outs, rest = rest[:n_out], rest[n_out:]
        h_outs, rest = rest[:nh], rest[nh:]
        scratch, sems = rest[:n_scratch], rest[n_scratch:]
        first, last = _grid_ends(grid)
        hosted.run(h_ins, h_outs, sems, first, last, lambda: core(*ins, *outs, *scratch))

    return body


def _hosted_parts(hosted):
    if hosted is None:
        return [], [], [], []
    return list(hosted.specs), list(hosted.out_shape), list(hosted.arrays), list(hosted.scratch)


def _exchange(name, items):
    ex = _Exchange(items)
    n = ex.n

    def body(*refs):
        ex.run(refs[:n], refs[n:2 * n], refs[2 * n:], None, None, lambda: None)

    return pl.pallas_call(
        body, name=name, out_shape=tuple(ex.out_shape), in_specs=ex.specs, out_specs=tuple(ex.specs),
        scratch_shapes=ex.scratch,
    )(*ex.arrays)


def _matmul_rows(a, w, *, name, out_dtype=F32):
    t, k = a.shape
    n = w.shape[1]
    assert w.shape[0] == k
    bt = _row_block(t)

    def body(a_ref, w_ref, o_ref):
        o_ref[...] = jnp.dot(a_ref[...], w_ref[...], preferred_element_type=F32).astype(o_ref.dtype)

    return pl.pallas_call(
        body, name=name, grid=(t // bt,),
        in_specs=[pl.BlockSpec((bt, k), lambda i: (i, 0)), pl.BlockSpec((k, n), lambda i: (0, 0))],
        out_specs=pl.BlockSpec((bt, n), lambda i: (i, 0)),
        out_shape=jax.ShapeDtypeStruct((t, n), out_dtype), compiler_params=_params(1),
    )(a, w)


def _matmul_tn(a, b, *, name):
    t, m = a.shape
    n = b.shape[1]
    bt = TOKEN_STEP if t % TOKEN_STEP == 0 else ROW_BLOCK
    bm = _divisor_block(m, WIDE_BLOCK)
    bn = _divisor_block(n, WIDE_BLOCK)
    steps = t // bt

    def body(a_ref, b_ref, o_ref, acc):
        step = pl.program_id(2)

        @pl.when(step == 0)
        def _():
            acc[...] = jnp.zeros_like(acc)

        acc[...] += _dot_tn(a_ref[...], b_ref[...])

        @pl.when(step == steps - 1)
        def _():
            o_ref[...] = acc[...].astype(o_ref.dtype)

    return pl.pallas_call(
        body, name=name, grid=(m // bm, n // bn, steps),
        in_specs=[pl.BlockSpec((bt, bm), lambda i, j, s: (s, i)), pl.BlockSpec((bt, bn), lambda i, j, s: (s, j))],
        out_specs=pl.BlockSpec((bm, bn), lambda i, j, s: (i, j)),
        out_shape=jax.ShapeDtypeStruct((m, n), BF16), scratch_shapes=[pltpu.VMEM((bm, bn), F32)],
        compiler_params=_params(3),
    )(a, b)


def _rmsnorm_fwd(x, g, *, group, name, hosted=None):
    t, w = x.shape
    bt = ROW_BLOCK

    def body(x_ref, g_ref, o_ref, x16_ref):
        x16_ref[...] = x_ref[...].astype(BF16)
        for s in range(0, w, group):
            xs = x_ref[:, s:s + group].astype(F32)
            r = lax.rsqrt(jnp.mean(xs * xs, axis=-1, keepdims=True) + EPS)
            o_ref[:, s:s + group] = (xs * r * g_ref[:, s:s + group]).astype(o_ref.dtype)

    grid = (t // bt,)
    h_specs, h_shapes, h_args, h_scratch = _hosted_parts(hosted)
    rows = pl.BlockSpec((bt, w), lambda i: (i, 0))
    return pl.pallas_call(
        _host(body, 2, 2, 0, hosted, grid), name=name, grid=grid,
        in_specs=[rows, pl.BlockSpec((1, w), lambda i: (0, 0))] + h_specs,
        out_specs=tuple([rows, rows] + h_specs),
        out_shape=tuple([jax.ShapeDtypeStruct((t, w), BF16)] * 2 + h_shapes),
        scratch_shapes=h_scratch, compiler_params=_params(1),
    )(x, g, *h_args)


def _norm_input_grad(terms, x, g, *, group, name, out_dtypes, resid=None, init=None, hosted=None, k_chunks=1):
    t, w = x.shape
    n_terms = len(terms)
    kc = [a.shape[1] // k_chunks for a, _, _ in terms]
    per_row = sum(c * a.dtype.itemsize for c, (a, _, _) in zip(kc, terms))
    per_row += w * sum(r.dtype.itemsize for r in (x, resid, init) if r is not None)
    per_row += w * sum(jnp.dtype(dt).itemsize for dt in out_dtypes)
    fixed = 2 * sum(w * c * 2 for c in kc)
    bt = next(b for b in (2 * ROW_BLOCK, ROW_BLOCK, ROW_BLOCK // 2, ROW_BLOCK // 4)
              if t % b == 0 and fixed + 2 * b * per_row + 5 * b * w * 4 <= MATMUL_VMEM_BUDGET)
    resid_at = 2 * n_terms + 2
    init_at = resid_at + (resid is not None)
    n_in = init_at + (init is not None)
    grid = (t // bt, k_chunks)

    def body(*refs):
        x_ref, g_ref = refs[2 * n_terms], refs[2 * n_terms + 1]
        dx_refs, dg_ref = refs[n_in:n_in + len(out_dtypes)], refs[n_in + len(out_dtypes)]
        chunk = pl.program_id(1)

        @pl.when((pl.program_id(0) == 0) & (chunk == 0))
        def _():
            dg_ref[...] = jnp.zeros_like(dg_ref)

        def matmuls(rows):
            part = None
            for k in range(n_terms):
                if terms[k][2]:
                    term = jnp.dot(refs[2 * k][rows, :], refs[2 * k + 1][...], preferred_element_type=F32)
                else:
                    term = _dot_nt(refs[2 * k][rows, :], refs[2 * k + 1][...])
                part = term if part is None else part + term
            return part

        def norm_backward(rows, dh):
            for s in range(0, w, group):
                xs = x_ref[rows, s:s + group].astype(F32)
                dhs = dh[:, s:s + group]
                r = lax.rsqrt(jnp.mean(xs * xs, axis=-1, keepdims=True) + EPS)
                xh = xs * r
                dg_ref[:, s:s + group] += jnp.sum(dhs * xh, axis=0, keepdims=True)
                dxh = dhs * g_ref[:, s:s + group]
                dx = r * (dxh - xh * jnp.mean(dxh * xh, axis=-1, keepdims=True))
                if resid is not None:
                    dx = refs[resid_at][rows, s:s + group] + dx
                for dx_ref in dx_refs:
                    dx_ref[rows, s:s + group] = dx.astype(dx_ref.dtype)

        if k_chunks == 1:
            halves = [slice(0, bt // 2), slice(bt // 2, bt)]
            dh = [matmuls(rows) if init is None else refs[init_at][rows, :] + matmuls(rows) for rows in halves]
            for rows, dh_half in zip(halves, dh):
                norm_backward(rows, dh_half)
            return
        dh_ref = refs[-1]
        part = matmuls(slice(None))

        @pl.when(chunk == 0)
        def _():
            dh_ref[...] = part if init is None else refs[init_at][...] + part

        @pl.when(chunk > 0)
        def _():
            dh_ref[...] += part

        @pl.when(chunk == k_chunks - 1)
        def _():
            norm_backward(slice(None), dh_ref[...])

    row = pl.BlockSpec((bt, w), lambda i, k: (i, 0))
    vec = pl.BlockSpec((1, w), lambda i, k: (0, 0))
    in_specs, args = [], []
    for c, (a, wt, w_is_kn) in zip(kc, terms):
        assert wt.shape == ((a.shape[1], w) if w_is_kn else (w, a.shape[1]))
        w_spec = pl.BlockSpec((c, w), lambda i, k: (k, 0)) if w_is_kn else pl.BlockSpec((w, c), lambda i, k: (0, k))
        in_specs += [pl.BlockSpec((bt, c), lambda i, k: (i, k)), w_spec]
        args += [a, wt]
    extra = [r for r in (resid, init) if r is not None]
    in_specs += [row, vec] + [row] * len(extra)
    args += [x, g] + extra
    h_specs, h_shapes, h_args, h_scratch = _hosted_parts(hosted)
    own_scratch = [pltpu.VMEM((bt, w), F32)] if k_chunks > 1 else []
    return pl.pallas_call(
        _host(body, n_in, len(out_dtypes) + 1, len(own_scratch), hosted, grid), name=name, grid=grid,
        in_specs=in_specs + h_specs,
        out_specs=tuple([row] * len(out_dtypes) + [vec] + h_specs),
        out_shape=tuple([jax.ShapeDtypeStruct((t, w), dt) for dt in out_dtypes] + [jax.ShapeDtypeStruct((1, w), F32)]
                        + h_shapes),
        scratch_shapes=own_scratch + h_scratch, compiler_params=_params(2),
    )(*args, *h_args)


def _tile_plan(tile):
    is_q = tile < N_PAIRS
    is_dil = (tile % N_PAIRS) >= N_PAIRS // 2
    return is_q, is_dil, (0 if is_q else 2) + (1 if is_dil else 0)


def _proj_tile(kind, pair):
    return kind * N_PAIRS + pair


def _segment_ones():
    lane = np.arange(LANES)
    return jnp.asarray((lane[:, None] // HEAD_DIM) == (lane[None, :] // HEAD_DIM), BF16)


def _rope_tables(seq):
    inv_freq = jnp.power(jnp.float32(ROPE_THETA), -jnp.arange(ROPE_HALF, dtype=F32) * 2.0 / ROPE_DIM)
    ang = jnp.arange(seq).astype(F32)[:, None] * inv_freq[None, :]
    cos, sin = jnp.cos(ang), jnp.sin(ang)
    ones = jnp.ones((seq, HEAD_DIM - ROPE_DIM), F32)
    zeros = jnp.zeros((seq, HEAD_DIM - ROPE_DIM), F32)
    zh = jnp.zeros((seq, ROPE_HALF), F32)
    cos_t = jnp.concatenate([cos, cos, ones], axis=1)
    sin_a = jnp.concatenate([-sin, zh, zeros], axis=1)
    sin_b = jnp.concatenate([zh, sin, zeros], axis=1)
    return tuple(jnp.tile(tab, (1, 2)) for tab in (cos_t, sin_a, sin_b))


def _log_sigmoid(z):
    return jnp.minimum(z, 0.0) - jnp.log1p(jnp.exp(-jnp.abs(z)))


def _aug_placement():
    place = np.zeros((N_PAIRS, LANES, LANES), np.float32)
    for is_k in range(2):
        for pair in range(N_PAIRS // 2):
            for e in range(2):
                other = HEAD_DIM * (1 - e)
                ones_at = other + (AUG_C if is_k else AUG_ONE)
                c_at = other + (AUG_ONE if is_k else AUG_C)
                for n in range(3):
                    place[4 * is_k + pair, N_HEADS_FOX * n + 2 * pair + e, c_at + n] = -1.0 if is_k else 1.0
                    place[4 * is_k + pair, 3 * N_HEADS_FOX, ones_at + n] = 1.0
    return jnp.asarray(place, BF16)


def _qk_prep_fwd(h1, w_main_t, w_fa_t, b_pad, gains, rope, seq):
    t = h1.shape[0]
    bt = ROW_BLOCK
    nsb = seq // bt
    seg = _segment_ones()
    rr = np.arange(bt)
    tri = jnp.asarray(rr[:, None] <= rr[None, :], BF16)

    def body(h_ref, w_ref, wfa_ref, b_ref, g_ref, cos_ref, sa_ref, sb_ref, seg_ref, tri_ref, place_ref,
             p_ref, qk_ref, fa_ref, carry):
        @pl.when(pl.program_id(0) % nsb == 0)
        def _():
            carry[...] = jnp.zeros_like(carry)

        def project(first):
            cols = slice(first * LANES, (first + 2) * LANES)
            y = _dot_nt(h_ref[...], w_ref[cols, :]).astype(BF16)
            p_ref[:, cols] = y
            return y

        lane = lax.broadcasted_iota(jnp.int32, (bt, LANES), 1)
        fa = _dot_nt(h_ref[...], wfa_ref[...])
        fa_ref[...] = fa
        logf = jnp.where(lane < N_HEADS_FOX, _log_sigmoid(fa + b_ref[...]), 0.0)
        c_rows = _split_dot(logf.T[0:N_HEADS_FOX, :], tri_ref[...], 3) + carry[:, 0:1]
        carry[...] = jnp.broadcast_to(c_rows[:, bt - 1:bt], carry.shape)
        cblk = jnp.concatenate([c_rows, jnp.zeros((LANES - N_HEADS_FOX, bt), F32)], axis=0).T
        packed = jnp.where(lane == 3 * N_HEADS_FOX, 1.0, 0.0)
        rest = cblk * LOG2E
        for n in range(3):
            term = rest.astype(BF16).astype(F32)
            packed = packed + (pltpu.roll(term, N_HEADS_FOX * n, 1) if n else term)
            rest = rest - term
        packed = packed.astype(BF16)
        low = lane < HEAD_DIM

        ahead = project(0)
        for tile in range(2 * N_PAIRS):
            is_q, is_dil, grow = _tile_plan(tile)
            pair = tile % N_PAIRS
            assert tile == _proj_tile(0 if is_q else 1, pair)
            if tile % 2 == 0:
                both = ahead
                if tile + 2 < 2 * N_PAIRS:
                    ahead = project(tile + 2)
                if tile % 4 == 2:
                    project(2 * N_PAIRS + tile // 2 - 1)
            xs = both[:, (tile % 2) * LANES:(tile % 2 + 1) * LANES].astype(F32)
            r = lax.rsqrt(_split_dot(xs * xs, seg_ref[...], 2) * (1.0 / HEAD_DIM) + EPS)
            yv = xs * r * g_ref[grow:grow + 1, :]
            if is_dil:
                yv = (yv * cos_ref[...] + pltpu.roll(yv, LANES - ROPE_HALF, 1) * sa_ref[...]
                      + pltpu.roll(yv, ROPE_HALF, 1) * sb_ref[...])
                aug = jnp.zeros((bt, LANES), F32)
            else:
                aug = jnp.dot(packed, place_ref[(0 if is_q else N_PAIRS // 2) + pair], preferred_element_type=F32)
            if is_q:
                yv = yv * (HEAD_DIM ** -0.5 * LOG2E)
            dst = ((0 if is_q else N_HEADS) + 2 * pair) * LANES
            qk_ref[:, dst:dst + LANES] = jnp.where(low, yv, aug).astype(BF16)
            qk_ref[:, dst + LANES:dst + 2 * LANES] = jnp.where(low, aug, yv).astype(BF16)

    row128 = pl.BlockSpec((bt, LANES), lambda i: (i, 0))
    rope_spec = pl.BlockSpec((bt, LANES), lambda i: (i % nsb, 0))
    const = lambda shape: pl.BlockSpec(shape, lambda i: (0,) * len(shape))
    return pl.pallas_call(
        body, name="in_proj_qk_prep", grid=(t // bt,),
        in_specs=[pl.BlockSpec((bt, D_MODEL), lambda i: (i, 0)), const((MAIN_COLS, D_MODEL)),
                  const((LANES, D_MODEL)), const((1, LANES)), const((8, LANES)), rope_spec, rope_spec, rope_spec,
                  const((LANES, LANES)), const((bt, bt)), const((N_PAIRS, LANES, LANES))],
        out_specs=(pl.BlockSpec((bt, MAIN_COLS), lambda i: (i, 0)),
                   pl.BlockSpec((bt, 2 * N_HEADS * LANES), lambda i: (i, 0)), row128),
        out_shape=(jax.ShapeDtypeStruct((t, MAIN_COLS), BF16),
                   jax.ShapeDtypeStruct((t, 2 * N_HEADS * LANES), BF16), jax.ShapeDtypeStruct((t, LANES), F32)),
        scratch_shapes=[pltpu.VMEM((8, LANES), F32)], compiler_params=_params(1),
    )(h1, w_main_t, w_fa_t, b_pad, gains, *rope, seg, tri, _aug_placement())


def _qk_prep_bwd(dq, dk, dqx, dkx, dv, proj, fa, b_pad, gains, rope, seq):
    t = proj.shape[0]
    bt = ROW_BLOCK
    nsb = seq // bt
    nblk = t // bt
    seg = _segment_ones()
    rr = np.arange(bt)
    triu = jnp.asarray(rr[:, None] >= rr[None, :], BF16)

    def body(dq_ref, dk_ref, dqx_ref, dkx_ref, dv_ref, p_ref, fa_ref, b_ref, g_ref, cos_ref, sa_ref, sb_ref, seg_ref,
             triu_ref, dp_ref, dg_ref, db_ref, carry):
        step = pl.program_id(0)

        @pl.when(step == 0)
        def _():
            dg_ref[...] = jnp.zeros_like(dg_ref)
            db_ref[...] = jnp.zeros_like(db_ref)

        @pl.when(step % nsb == 0)
        def _():
            carry[...] = jnp.zeros_like(carry)

        for tile in range(2 * N_PAIRS):
            is_q, is_dil, grow = _tile_plan(tile)
            first = _proj_tile(0 if is_q else 1, tile % N_PAIRS) * LANES
            cols = slice(first, first + LANES)
            src = dq_ref if is_q else dk_ref
            half = slice((tile % N_PAIRS) * LANES, (tile % N_PAIRS + 1) * LANES)
            dy = src[:, half].astype(F32)
            dy = dy * (HEAD_DIM ** -0.5 if is_q else LN2)
            if is_dil:
                dy = (dy * cos_ref[...] + pltpu.roll(dy * sa_ref[...], ROPE_HALF, 1)
                      + pltpu.roll(dy * sb_ref[...], LANES - ROPE_HALF, 1))
            xs = p_ref[:, cols].astype(F32)
            r = lax.rsqrt(_split_dot(xs * xs, seg_ref[...], 2) * (1.0 / HEAD_DIM) + EPS)
            xh = xs * r
            dg_ref[tile:tile + 1, :] += jnp.sum(dy * xh, axis=0, keepdims=True)
            dxh = dy * g_ref[grow:grow + 1, :]
            seg_mean = _split_dot(dxh * xh, seg_ref[...], 2) * (1.0 / HEAD_DIM)
            dp_ref[:, cols] = (r * (dxh - xh * seg_mean)).astype(BF16)

        lane = lax.broadcasted_iota(jnp.int32, (bt, LANES), 1)
        dc = jnp.zeros((bt, LANES), F32)
        for h in range(N_HEADS_FOX):
            other = (h // 2) * LANES + HEAD_DIM * (1 - h % 2)
            row_sum = dqx_ref[:, other + AUG_C:other + AUG_C + 1]
            col_sum = dkx_ref[:, other + AUG_ONE:other + AUG_ONE + 1]
            dc = jnp.where(lane == h, row_sum - col_sum, dc)
        d_rows = _split_dot(dc.T[0:N_HEADS_FOX, :], triu_ref[...], 3) + carry[:, 0:1]
        carry[...] = jnp.broadcast_to(d_rows[:, 0:1], carry.shape)
        dlogf = jnp.concatenate([d_rows, jnp.zeros((LANES - N_HEADS_FOX, bt), F32)], axis=0).T
        z = fa_ref[...] + b_ref[...]
        dfa = dlogf * (1.0 / (1.0 + jnp.exp(z)))
        db_ref[0:1, :] += jnp.sum(dfa, axis=0, keepdims=True)
        for group in range(2):
            first = _proj_tile(2, group * (N_PAIRS // 2)) * LANES
            dp_ref[:, first:first + W_GROUP] = dv_ref[:, group * W_GROUP:(group + 1) * W_GROUP]
        dp_ref[:, MAIN_COLS:PROJ_COLS] = dfa.astype(BF16)

    rev = lambda i: nblk - 1 - i
    row = lambda w: pl.BlockSpec((bt, w), lambda i: (rev(i), 0))
    rope_spec = pl.BlockSpec((bt, LANES), lambda i: (rev(i) % nsb, 0))
    const = lambda shape: pl.BlockSpec(shape, lambda i: (0, 0))
    return pl.pallas_call(
        body, name="qk_prep_bwd", grid=(nblk,),
        in_specs=[row(D_MODEL), row(D_MODEL), row(W_GROUP), row(W_GROUP), row(D_MODEL), row(2 * D_MODEL), row(LANES),
                  const((1, LANES)), const((8, LANES)), rope_spec, rope_spec, rope_spec, const((LANES, LANES)),
                  const((bt, bt))],
        out_specs=(row(PROJ_COLS), const((2 * N_PAIRS, LANES)), const((8, LANES))),
        out_shape=(jax.ShapeDtypeStruct((t, PROJ_COLS), BF16),
                   jax.ShapeDtypeStruct((2 * N_PAIRS, LANES), F32), jax.ShapeDtypeStruct((8, LANES), F32)),
        scratch_shapes=[pltpu.VMEM((8, LANES), F32)], compiler_params=_params(1),
    )(dq, dk, dqx, dkx, dv, proj, fa, b_pad, gains, *rope, seg, triu)


def _bias_tables(seq, keys_first):
    nb = seq // ATT_BLOCK
    idx = np.arange(ATT_BLOCK)
    q_idx, k_idx = (idx[None, None, :], idx[None, :, None]) if keys_first else (idx[None, :, None], idx[None, None, :])
    dist = np.arange(nb)[:, None, None] * ATT_BLOCK + q_idx - k_idx
    causal = dist >= 0
    count = np.zeros(dist.shape, np.int32)
    for window, dilation in DILATION_PAIRS:
        count = count + (causal & (dist % dilation == 0) & (dist <= window))
    fox = np.where(causal, 0.0, NEG)
    dil = np.where(count == 3, math.log2(3.0), np.where(count == 2, 1.0, np.where(count == 1, 0.0, NEG)))
    return jnp.asarray(np.stack([fox, dil], axis=0), F32)


def _attn_specs(seq):
    nb = seq // ATT_BLOCK
    col = pl.BlockSpec((seq, LANES), lambda b, j: (b, j))
    heads = lambda off: pl.BlockSpec((seq, 2 * LANES), lambda b, j: (b, off + j))
    v_spec = pl.BlockSpec((seq, LANES), lambda b, j: (b, _proj_tile(2, j)))
    table_spec = pl.BlockSpec((1, nb, ATT_BLOCK, ATT_BLOCK), lambda b, j: (j // (N_PAIRS // 2), 0, 0, 0))
    return col, heads, v_spec, table_spec


def _head_lanes(e, shape, axis):
    pos = lax.broadcasted_iota(jnp.int32, shape, axis)
    return pos < HEAD_DIM if e == 0 else pos >= HEAD_DIM


def _attn_fwd(qk, proj, tables, seq, hosted=None):
    t = qk.shape[0]
    nb = seq // ATT_BLOCK
    blk = ATT_BLOCK

    def body(q_ref, k_ref, v_ref, tab_ref, o_ref, lse_ref):
        mine = [_head_lanes(e, (seq, LANES), 1) for e in range(2)]
        lane = lax.broadcasted_iota(jnp.int32, (seq, LANES), 1)
        v_aug = [jnp.where(mine[e], v_ref[...], (lane == HEAD_DIM * (1 - e)).astype(BF16)) for e in range(2)]
        def scores(i, e):
            heads_e = slice(e * LANES, (e + 1) * LANES)
            s = _dot_nt(q_ref[i * blk:(i + 1) * blk, heads_e], k_ref[0:(i + 1) * blk, heads_e])
            s = jnp.concatenate([s[:, jj * blk:(jj + 1) * blk] + tab_ref[0, i - jj] for jj in range(i + 1)], axis=1)
            return s, jnp.max(s, axis=1, keepdims=True)

        chains = [(i, e) for i in reversed(range(nb)) for e in range(2)]
        ahead = 2
        pending = [scores(*chain) for chain in chains[:ahead]]
        done = {}
        for n, (i, e) in enumerate(chains):
            s, m = pending.pop(0)
            if n + ahead < len(chains):
                pending.append(scores(*chains[n + ahead]))
            acc = jnp.dot(jnp.exp2(s - m).astype(BF16), v_aug[e][0:(i + 1) * blk], preferred_element_type=F32)
            ones_at = HEAD_DIM * (1 - e)
            l = acc[:, ones_at:ones_at + 1]
            done[e] = (acc / l, m + jnp.log2(l))
            if e == 1:
                rows = slice(i * blk, (i + 1) * blk)
                o_ref[rows, :] = jnp.where(mine[0][rows], done[0][0], done[1][0]).astype(o_ref.dtype)
                lse_ref[rows, :] = jnp.where(mine[0][rows], done[0][1], done[1][1])

    col, heads, v_spec, table_spec = _attn_specs(seq)
    grid = (t // seq, N_PAIRS)
    h_specs, h_shapes, h_args, h_scratch = _hosted_parts(hosted)
    return pl.pallas_call(
        _host(body, 4, 2, 0, hosted, grid), name="attn_fwd", grid=grid,
        in_specs=[heads(0), heads(N_PAIRS), v_spec, table_spec] + h_specs,
        out_specs=tuple([col, col] + h_specs),
        out_shape=tuple([jax.ShapeDtypeStruct((t, D_MODEL), BF16), jax.ShapeDtypeStruct((t, D_MODEL), F32)] + h_shapes),
        scratch_shapes=h_scratch, compiler_params=_params(2),
    )(qk, qk, proj, tables, *h_args)


def _attn_bwd(qk, proj, tables, o, lse, do, seq, hosted=None):
    t = qk.shape[0]
    nb = seq // ATT_BLOCK
    blk = ATT_BLOCK
    group = math.gcd(nb, ATT_GROUP)

    def body(q_ref, k_ref, v_ref, tab_ref, o_ref, lse_ref, do_ref,
             dq_ref, dk_ref, dv_ref, dqx_ref, dkx_ref, dk_acc, dv_acc):
        mine = [_head_lanes(e, (blk, LANES), 1) for e in range(2)]
        top = _head_lanes(0, (LANES, blk), 0)
        head_rows = lax.broadcasted_iota(jnp.int32, (8, LANES), 0)
        head_of_lane = lax.broadcasted_iota(jnp.int32, (8, LANES), 1) // HEAD_DIM
        head_sel = (head_rows == head_of_lane).astype(BF16)
        dk_acc[...] = jnp.zeros_like(dk_acc)
        dv_acc[...] = jnp.zeros_like(dv_acc)

        def block_rows(i):
            return pl.ds(pl.multiple_of(i * blk, blk), blk)

        def q_group(g, _):
            base = g * group
            qs, doe, delta, lse_e = [], [], [], []
            for b in range(group):
                rows = block_rows(base + b)
                qs.append([q_ref[rows, e * LANES:(e + 1) * LANES] for e in range(2)])
                do_blk = do_ref[rows, :]
                doe.append([jnp.where(mine[e], do_blk, jnp.zeros_like(do_blk)) for e in range(2)])
                delta_t = _split_dot_nt(head_sel, do_blk.astype(F32) * o_ref[rows, :].astype(F32), 3)
                lse_t = _split_dot_nt(head_sel, lse_ref[rows, :], 3) * (1.0 / HEAD_DIM)
                delta.append([delta_t[e:e + 1, :] for e in range(2)])
                lse_e.append([lse_t[e:e + 1, :] for e in range(2)])

            def key_block(dq_t, jj, members):
                krows = block_rows(jj)
                v = v_ref[krows, :]
                dq_t = [list(d) for d in dq_t]
                lo, hi = slice(0, blk // 2), slice(blk // 2, blk)
                dv_part = [None, None]
                add = lambda acc, part: part if acc is None else acc + part

                def probs(k_sub, v_sub, keys, queries, b, e, dist):
                    q_sub, do_sub = qs[b][e][queries], doe[b][e][queries]
                    p_t = jnp.exp2(_dot_nt(k_sub, q_sub) + tab_ref[0, dist, keys, queries] - lse_e[b][e][:, queries])
                    ds_t = (p_t * (_dot_nt(v_sub, do_sub) - delta[b][e][:, queries])).astype(BF16)
                    return p_t.astype(BF16), ds_t, q_sub, do_sub, k_sub

                def outputs(tile):
                    p_t, ds_t, q_sub, do_sub, k_sub = tile
                    return (jnp.dot(p_t, do_sub, preferred_element_type=F32),
                            jnp.dot(ds_t, q_sub, preferred_element_type=F32), _dot_tn(k_sub, ds_t))

                for e in range(2):
                    k_e = k_ref[krows, e * LANES:(e + 1) * LANES]
                    dk_part = [None, None]
                    tiles = []
                    for b, dist in members:
                        if isinstance(dist, int) and dist == 0:
                            tiles.append((b, probs(k_e[lo], v[lo], lo, slice(0, blk), b, e, dist),
                                          probs(k_e[hi], v[hi], hi, hi, b, e, dist)))
                        else:
                            tiles.append((b, probs(k_e, v, slice(0, blk), slice(0, blk), b, e, dist), None))
                    for b, first, second in tiles:
                        if second is not None:
                            dv_a, dk_a, dq_a = outputs(first)
                            dv_b, dk_b, dq_b = outputs(second)
                            halves = ((dv_a, dk_a), (dv_b, dk_b))
                            dq = jnp.concatenate([dq_a[:, lo], dq_a[:, hi] + dq_b], axis=1)
                        else:
                            dv_f, dk_f, dq = outputs(first)
                            halves = ((dv_f[lo], dk_f[lo]), (dv_f[hi], dk_f[hi]))
                        for n, (dv_h, dk_h) in enumerate(halves):
                            dv_part[n] = add(dv_part[n], dv_h)
                            dk_part[n] = add(dk_part[n], dk_h)
                        dq_t[b][e] = dq_t[b][e] + dq
                    dk_acc[e, krows, :] += jnp.concatenate(dk_part, axis=0)
                dv_acc[krows, :] += jnp.concatenate(dv_part, axis=0)
                return tuple(tuple(d) for d in dq_t)

            zacc = jnp.zeros((LANES, blk), F32)
            dq_t = tuple((zacc, zacc) for _ in range(group))
            dq_t = lax.fori_loop(
                0, base, lambda jj, st: key_block(st, jj, [(b, base + b - jj) for b in range(group)]), dq_t)
            for a in range(group):
                dq_t = key_block(dq_t, base + a, [(b, b - a) for b in range(a, group)])
            for b in range(group):
                rows = block_rows(base + b)
                dq_ref[rows, :] = jnp.where(top, dq_t[b][0], dq_t[b][1]).T.astype(BF16)
                dqx_ref[rows, :] = jnp.where(top, dq_t[b][1], dq_t[b][0]).T
            return 0

        lax.fori_loop(0, nb // group, q_group, 0)
        lo = _head_lanes(0, (seq, LANES), 1)
        dk_ref[...] = jnp.where(lo, dk_acc[0], dk_acc[1]).astype(BF16)
        dkx_ref[...] = jnp.where(lo, dk_acc[1], dk_acc[0])
        dv_ref[...] = dv_acc[...].astype(dv_ref.dtype)

    col, heads, v_spec, table_spec = _attn_specs(seq)
    grid = (t // seq, N_PAIRS)
    h_specs, h_shapes, h_args, h_scratch = _hosted_parts(hosted)
    f32_out = jax.ShapeDtypeStruct((t, D_MODEL), F32)
    return pl.pallas_call(
        _host(body, 7, 5, 2, hosted, grid), name="attn_bwd", grid=grid,
        in_specs=[heads(0), heads(N_PAIRS), v_spec, table_spec, col, col, col] + h_specs,
        out_specs=tuple([col] * 5 + h_specs),
        out_shape=tuple([jax.ShapeDtypeStruct((t, D_MODEL), BF16)] * 3 + [f32_out, f32_out] + h_shapes),
        scratch_shapes=[pltpu.VMEM((2, seq, LANES), F32), pltpu.VMEM((seq, LANES), F32)] + h_scratch,
        compiler_params=_params(2),
    )(qk, qk, proj, tables, o, lse, do, *h_args)


def _row_block(t):
    return 2 * ROW_BLOCK if t % (2 * ROW_BLOCK) == 0 else ROW_BLOCK


def _out_proj_ffn_norm(o, g_out, w_out, x, g_ffn):
    t = o.shape[0]
    bt = _row_block(t)

    def body(o_ref, go_ref, w_ref, x_ref, gf_ref, on_ref, x2_ref, h2_ref):
        for s in range(0, D_MODEL, W_GROUP):
            os_ = o_ref[:, s:s + W_GROUP].astype(F32)
            r = lax.rsqrt(jnp.mean(os_ * os_, axis=-1, keepdims=True) + EPS)
            on_ref[:, s:s + W_GROUP] = (os_ * r * go_ref[:, s:s + W_GROUP]).astype(BF16)
        x2 = x_ref[...] + jnp.dot(on_ref[...], w_ref[...], preferred_element_type=F32)
        x2_ref[...] = x2.astype(BF16)
        r2 = lax.rsqrt(jnp.mean(x2 * x2, axis=-1, keepdims=True) + EPS)
        h2_ref[...] = (x2 * r2 * gf_ref[...]).astype(BF16)

    row = pl.BlockSpec((bt, D_MODEL), lambda i: (i, 0))
    vec = pl.BlockSpec((1, D_MODEL), lambda i: (0, 0))
    return pl.pallas_call(
        body, name="out_proj", grid=(t // bt,),
        in_specs=[row, vec, pl.BlockSpec((D_MODEL, D_MODEL), lambda i: (0, 0)), row, vec],
        out_specs=(row, row, row),
        out_shape=tuple([jax.ShapeDtypeStruct((t, D_MODEL), BF16)] * 3),
        compiler_params=_params(1),
    )(o, g_out, w_out, x, g_ffn)


def _ffn_gate_up(h2, w_gate_t, w_up_t):
    t = h2.shape[0]
    bt = _row_block(t)
    bn = _divisor_block(D_FF, WIDE_BLOCK)

    def body(h_ref, wg_ref, wu_ref, a_ref, u_ref, f_ref):
        a = _dot_nt(h_ref[...], wg_ref[...])
        u = _dot_nt(h_ref[...], wu_ref[...])
        a_ref[...] = a.astype(BF16)
        u_ref[...] = u.astype(BF16)
        f_ref[...] = (a * jax.nn.sigmoid(a) * u).astype(BF16)

    blk = pl.BlockSpec((bt, bn), lambda j, i: (i, j))
    w_blk = pl.BlockSpec((bn, D_MODEL), lambda j, i: (j, 0))
    shape = jax.ShapeDtypeStruct((t, D_FF), BF16)
    return pl.pallas_call(
        body, name="ffn_gate_up", grid=(D_FF // bn, t // bt),
        in_specs=[pl.BlockSpec((bt, D_MODEL), lambda j, i: (i, 0)), w_blk, w_blk],
        out_specs=(blk, blk, blk), out_shape=(shape, shape, shape), compiler_params=_params(2),
    )(h2, w_gate_t, w_up_t)


def _ffn_down_grad(dy16, w_down, a, u):
    t = a.shape[0]
    bt = _row_block(t)
    bn = _divisor_block(D_FF, WIDE_BLOCK)

    def body(dy_ref, w_ref, a_ref, u_ref, da_ref, du_ref):
        df = _dot_nt(dy_ref[...], w_ref[...])
        av = a_ref[...].astype(F32)
        sg = jax.nn.sigmoid(av)
        da_ref[...] = (df * u_ref[...].astype(F32) * sg * (1.0 + av * (1.0 - sg))).astype(BF16)
        du_ref[...] = (df * av * sg).astype(BF16)

    blk = pl.BlockSpec((bt, bn), lambda j, i: (i, j))
    shape = jax.ShapeDtypeStruct((t, D_FF), BF16)
    return pl.pallas_call(
        body, name="d_ffn_down", grid=(D_FF // bn, t // bt),
        in_specs=[pl.BlockSpec((bt, D_MODEL), lambda j, i: (i, 0)), pl.BlockSpec((bn, D_MODEL), lambda j, i: (j, 0)),
                  blk, blk],
        out_specs=(blk, blk), out_shape=(shape, shape), compiler_params=_params(2),
    )(dy16, w_down, a, u)


def _ffn_down_loss(f, w_down, x2, target):
    t, w = x2.shape
    bt = _row_block(t)

    def body(f_ref, w_ref, x_ref, t_ref, dy16_ref, loss_ref):
        @pl.when(pl.program_id(0) == 0)
        def _():
            loss_ref[...] = jnp.zeros_like(loss_ref)

        err = (x_ref[...] + jnp.dot(f_ref[...], w_ref[...], preferred_element_type=F32)) - t_ref[...]
        dy16_ref[...] = (err * (1.0 / w)).astype(BF16)
        loss_ref[...] += 0.5 * jnp.sum(jnp.mean(err * err, axis=-1, keepdims=True), axis=0, keepdims=True)

    row = pl.BlockSpec((bt, w), lambda i: (i, 0))
    return pl.pallas_call(
        body, name="ffn_down_loss", grid=(t // bt,),
        in_specs=[pl.BlockSpec((bt, D_FF), lambda i: (i, 0)), pl.BlockSpec((D_FF, w), lambda i: (0, 0)), row, row],
        out_specs=(row, pl.BlockSpec((8, LANES), lambda i: (0, 0))),
        out_shape=(jax.ShapeDtypeStruct((t, w), BF16), jax.ShapeDtypeStruct((8, LANES), F32)),
        compiler_params=_params(1),
    )(f, w_down, x2, target)


def _adamw(parts, w, m, v, *, name):
    _, rows, cols = w.shape
    br = rows if rows <= 512 else 256
    assert rows % br == 0

    def body(p_ref, w_ref, m_ref, v_ref, g_ref, d_ref, nm_ref, nv_ref):
        g = p_ref[0].astype(F32)
        for r in range(1, N_DEV):
            g = g + p_ref[r].astype(F32)
        m2 = ADAM_B1 * m_ref[0] + (1.0 - ADAM_B1) * g
        v2 = ADAM_B2 * v_ref[0] + (1.0 - ADAM_B2) * jnp.square(g)
        m_hat = m2 / (1.0 - ADAM_B1 ** ADAM_STEP)
        v_hat = v2 / (1.0 - ADAM_B2 ** ADAM_STEP)
        g_ref[0] = g
        d_ref[0] = -ADAM_LR * (m_hat / (jnp.sqrt(v_hat) + ADAM_EPS) + ADAM_WD * w_ref[0])
        nm_ref[0] = m2
        nv_ref[0] = v2

    blk = pl.BlockSpec((1, br, cols), lambda i: (0, i, 0))
    shape = jax.ShapeDtypeStruct((1, rows, cols), F32)
    return pl.pallas_call(
        body, name=name, grid=(rows // br,),
        in_specs=[pl.BlockSpec((N_DEV, br, cols), lambda i: (0, i, 0)), blk, blk, blk],
        out_specs=(blk, blk, blk, blk), out_shape=(shape, shape, shape, shape), compiler_params=_params(1),
    )(parts, w, m, v)


_QA, _KA, _VA, _FA, _QD, _KD, _VD = (0, 512), (512, 1024), (1024, 1536), (1536, 1544), (1544, 2056), (2056, 2568), (2568, 3080)
_MAIN_ORDER = (_QA, _QD, _KA, _KD, _VA, _VD)
MAIN_COLS = 3 * D_MODEL
PROJ_COLS = MAIN_COLS + LANES
COL_SHARDED = ("w_in", "w_gate", "w_up")


def _swap(w):
    return jnp.transpose(w, (0, 2, 1))


def _w_in_to_kernel(w_t):
    main = jnp.concatenate([w_t[a:b] for a, b in _MAIN_ORDER], axis=0)
    forget = jnp.pad(w_t[_FA[0]:_FA[1]], ((0, LANES - N_HEADS_FOX), (0, 0)))
    return main, forget


def _w_in_from_kernel(g_t):
    pos = {span: i * W_GROUP for i, span in enumerate(_MAIN_ORDER)}
    parts = []
    for span in (_QA, _KA, _VA, _FA, _QD, _KD, _VD):
        if span == _FA:
            parts.append(g_t[MAIN_COLS:MAIN_COLS + N_HEADS_FOX])
        else:
            parts.append(g_t[pos[span]:pos[span] + W_GROUP])
    return jnp.concatenate(parts, axis=0)


def _pack_small(vals):
    rows = []
    for name, _, n_rows in SMALL_LAYOUT:
        flat = vals[name].reshape(-1).astype(F32)
        rows.append(jnp.pad(flat, (0, n_rows * LANES - flat.shape[0])).reshape(n_rows, LANES))
    packed = jnp.concatenate(rows, axis=0)
    return jnp.pad(packed, ((0, SMALL_ROWS - packed.shape[0]), (0, 0)))


def _unpack_small(packed, like):
    out = {}
    for name, row, n_rows in SMALL_LAYOUT:
        n = like[name].size
        out[name] = packed[row:row + n_rows].reshape(-1)[:n].reshape(like[name].shape)
    return out


def _device_step(x, target, small, shards):
    bsz, seq, _ = x.shape
    t = bsz * seq
    xf = x.reshape(t, D_MODEL)
    tf = target.reshape(t, D_MODEL)
    row = lambda v: v.reshape(1, -1)
    g_out = jnp.concatenate([small["g_out_fox"], small["g_out_dil"]]).reshape(1, D_MODEL)
    gains = jnp.concatenate(
        [jnp.tile(small[n].reshape(1, HEAD_DIM), (1, 2)) for n in ("g_q_fox", "g_q_dil", "g_k_fox", "g_k_dil")]
        + [jnp.zeros((4, LANES), F32)], axis=0)
    b_pad = jnp.pad(small["b_forget"].reshape(1, N_HEADS_FOX), ((0, 0), (0, LANES - N_HEADS_FOX)))
    rope = _rope_tables(seq)
    tables_qk = _bias_tables(seq, keys_first=False)
    tables_kq = _bias_tables(seq, keys_first=True)

    h1, x16, g_in = _rmsnorm_fwd(xf, row(small["g_mix"]), group=D_MODEL, name="norm_mix",
                                 hosted=_ChipGather([(shards["w_in"], False)]))
    w_main_t, w_fa_t = _w_in_to_kernel(g_in.reshape(IN_COLS, D_MODEL))
    w_in_all_t = jnp.concatenate([w_main_t, w_fa_t], axis=0)
    proj, qk, fa = _qk_prep_fwd(h1, w_main_t, w_fa_t, b_pad, gains, rope, seq)
    late = _Exchange([(shards[n], False) for n in ("w_out", "w_gate", "w_up", "w_down")])
    o, lse, g_out_w, g_gate, g_up, g_down = _attn_fwd(qk, proj, tables_qk, seq, hosted=late)
    w_out = g_out_w.reshape(D_MODEL, D_MODEL)
    w_gate_t = g_gate.reshape(D_FF, D_MODEL)
    w_up_t = g_up.reshape(D_FF, D_MODEL)
    w_down = g_down.reshape(D_FF, D_MODEL)
    on, x2, h2 = _out_proj_ffn_norm(o, g_out, w_out, x16, row(small["g_ffn"]))
    a, u, f = _ffn_gate_up(h2, w_gate_t, w_up_t)
    dy16, loss_tile = _ffn_down_loss(f, w_down, x2, tf)

    da, du = _ffn_down_grad(dy16, w_down, a, u)
    gw_down = _matmul_tn(f, dy16, name="gw_down")
    gw_gate_t = _matmul_tn(da, h2, name="gw_gate")
    gw_up_t = _matmul_tn(du, h2, name="gw_up")
    dh2_gate = _matmul_rows(da, w_gate_t, name="d_ffn_gate", out_dtype=BF16)
    dx2_16, dg_ffn = _norm_input_grad([(du, w_up_t, True)], x2, row(small["g_ffn"]), group=D_MODEL,
                                      name="d_ffn_up", out_dtypes=(BF16,), resid=dy16, init=dh2_gate)
    gw_out = _matmul_tn(on, dx2_16, name="gw_out")
    do, dg_out = _norm_input_grad([(dx2_16, w_out, False)], o, g_out, group=W_GROUP, name="d_out_proj",
                                  out_dtypes=(BF16,))

    shard_rows = lambda g: g.reshape(N_DEV, g.shape[0] // N_DEV, g.shape[1])
    ffn_grads = _Exchange([(shard_rows(g), True) for g in (gw_out, gw_gate_t, gw_up_t, gw_down)])
    dq, dk, dv, dqx, dkx, p_out, p_gate, p_up, p_down = _attn_bwd(qk, proj, tables_kq, o, lse, do, seq, hosted=ffn_grads)
    dproj, dgains, db = _qk_prep_bwd(dq, dk, dqx, dkx, dv, proj, fa, b_pad, gains, rope, seq)
    gw_in_t = _matmul_tn(dproj, h1, name="gw_in")
    in_grad = _Exchange([(shard_rows(_w_in_from_kernel(gw_in_t)), True)])
    dx, dg_mix, p_in = _norm_input_grad([(dproj, w_in_all_t, True)], x16, row(small["g_mix"]), group=D_MODEL,
                                        name="d_in_proj", out_dtypes=(F32,), resid=dx2_16, hosted=in_grad)

    fold = lambda rows: jnp.sum(rows[:, :HEAD_DIM] + rows[:, HEAD_DIM:], axis=0)
    half = N_PAIRS // 2
    gsmall = {
        "g_mix": dg_mix, "g_ffn": dg_ffn, "g_out_fox": dg_out[0, :W_GROUP], "g_out_dil": dg_out[0, W_GROUP:],
        "g_q_fox": fold(dgains[0:half]), "g_q_dil": fold(dgains[half:N_PAIRS]),
        "g_k_fox": fold(dgains[N_PAIRS:N_PAIRS + half]), "g_k_dil": fold(dgains[N_PAIRS + half:]),
        "b_forget": db[0, :N_HEADS_FOX],
    }
    packed = _pack_small(gsmall).at[LOSS_ROW].set(loss_tile[0])
    (p_small,) = _exchange("small_exchange", [(packed, False)])
    parts = {"w_in": p_in, "w_out": p_out, "w_gate": p_gate, "w_up": p_up, "w_down": p_down}
    return dx.reshape(x.shape), parts, p_small


def kernel(x, g_mix, w_in, b_forget, g_q_fox, g_k_fox, g_q_dil, g_k_dil, g_out_fox, g_out_dil, w_out, g_ffn, w_gate, w_up, w_down, loss_target, m_g_mix, m_w_in, m_b_forget, m_g_q_fox, m_g_k_fox, m_g_q_dil, m_g_k_dil, m_g_out_fox, m_g_out_dil, m_w_out, m_g_ffn, m_w_gate, m_w_up, m_w_down, v_g_mix, v_w_in, v_b_forget, v_g_q_fox, v_g_k_fox, v_g_q_dil, v_g_k_dil, v_g_out_fox, v_g_out_dil, v_w_out, v_g_ffn, v_w_gate, v_w_up, v_w_down):
    args = dict(locals())
    small_names = [name for name, _, _ in SMALL_LAYOUT]
    big_names = ["w_in", "w_out", "w_gate", "w_up", "w_down"]
    small = {n: args[n][0] for n in small_names}

    as_rows = lambda n, w: _swap(w) if n in COL_SHARDED else w
    shards = {n: as_rows(n, args[n])[0].astype(BF16) for n in big_names}
    grad_x, parts, p_small = _device_step(x, loss_target, small, shards)

    grads, deltas, new_m, new_v = {}, {}, {}, {}
    for n in big_names:
        res = _adamw(parts[n], as_rows(n, args[n]), as_rows(n, args["m_" + n]), as_rows(n, args["v_" + n]),
                     name="adamw_" + n)
        grads[n], deltas[n], new_m[n], new_v[n] = [as_rows(n, r) for r in res]
    res = _adamw(p_small, _pack_small(small)[None], _pack_small({n: args["m_" + n][0] for n in small_names})[None],
                 _pack_small({n: args["v_" + n][0] for n in small_names})[None], name="adamw_small")
    loss = res[0][0, LOSS_ROW, 0]
    for dst, packed_res in zip((grads, deltas, new_m, new_v), res):
        for n, val in _unpack_small(packed_res[0], small).items():
            dst[n] = val[None]

    order = ["g_mix", "w_in", "b_forget", "g_q_fox", "g_k_fox", "g_q_dil", "g_k_dil", "g_out_fox", "g_out_dil",
             "w_out", "g_ffn", "w_gate", "w_up", "w_down"]
    return (loss, grad_x, *[grads[n] for n in order], *[deltas[n] for n in order],
            *[new_m[n] for n in order], *[new_v[n] for n in order])
```

```python
import functools
import math

import jax
import jax.numpy as jnp
import numpy as np
from jax import lax
from jax.experimental import pallas as pl
from jax.experimental.pallas import tpu as pltpu

F32 = jnp.float32
BF16 = jnp.bfloat16

D_MODEL = 1024
HEAD_DIM = 64
LANES = 128
N_PAIRS = D_MODEL // LANES
N_HEADS = 2 * N_PAIRS
N_HEADS_FOX = 8
W_GROUP = 512
D_FF = 2816
IN_COLS = 3080
DILATION_PAIRS = ((128, 1), (512, 4), (2048, 16))
ROPE_THETA = 500000.0
ROPE_DIM = 16
ROPE_HALF = ROPE_DIM // 2
EPS = 1e-6
NEG = -1e30
LOG2E = 1.4426950408889634
LN2 = 0.6931471805599453
AUG_ONE = 0
AUG_C = 3
N_DEV = 8

ADAM_LR = 0.001
ADAM_B1 = 0.9
ADAM_B2 = 0.999
ADAM_EPS = 1e-08
ADAM_WD = 0.01
ADAM_STEP = 10

ROW_BLOCK = 512
TOKEN_STEP = 2048
WIDE_BLOCK = D_FF // 2
ATT_BLOCK = 512
ATT_GROUP = 4
VMEM_LIMIT = 56 * 1024 * 1024
MATMUL_VMEM_BUDGET = 44 * 1024 * 1024

SMALL_ROWS = 32
SMALL_LAYOUT = (("g_mix", 0, 8), ("g_ffn", 8, 8), ("g_out_fox", 16, 4), ("g_out_dil", 20, 4),
                ("g_q_fox", 24, 1), ("g_k_fox", 25, 1), ("g_q_dil", 26, 1), ("g_k_dil", 27, 1),
                ("b_forget", 28, 1))
LOSS_ROW = 29


def _params(n_grid):
    return pltpu.CompilerParams(dimension_semantics=("arbitrary",) * n_grid, vmem_limit_bytes=VMEM_LIMIT)


def _divisor_block(n, cap):
    best = None
    for b in range(LANES, min(n, cap) + 1, LANES):
        if n % b == 0:
            best = b
    assert best is not None, n
    return best


def _split_dot(a, b_exact, terms):
    acc = None
    rest = a
    for _ in range(terms):
        hi = rest.astype(BF16)
        part = jnp.dot(hi, b_exact, preferred_element_type=F32)
        acc = part if acc is None else acc + part
        rest = rest - hi.astype(F32)
    return acc


def _split_dot_nt(a_exact, b, terms):
    acc = None
    rest = b
    for _ in range(terms):
        hi = rest.astype(BF16)
        part = _dot_nt(a_exact, hi)
        acc = part if acc is None else acc + part
        rest = rest - hi.astype(F32)
    return acc


def _dot_nt(a, b):
    return lax.dot_general(a, b, (((1,), (1,)), ((), ())), preferred_element_type=F32)


def _dot_tn(a, b):
    return lax.dot_general(a, b, (((0,), (0,)), ((), ())), preferred_element_type=F32)


class _Exchange:
    def __init__(self, items):
        self.items = items
        self.n = len(items)
        self.arrays = [a for a, _ in items]
        self.out_shape = [jax.ShapeDtypeStruct((N_DEV,) + tuple(a.shape[1:] if sc else a.shape), a.dtype)
                          for a, sc in items]
        self.specs = [pl.BlockSpec(memory_space=pl.ANY)] * self.n
        self.scratch = [pltpu.SemaphoreType.DMA((self.n, N_DEV - 1)), pltpu.SemaphoreType.DMA((self.n, N_DEV - 1)),
                        pltpu.SemaphoreType.DMA((self.n,))]

    def run(self, ins, outs, sems, first, last, compute):
        send_sems, recv_sems, local_sems = sems
        x, y, c = lax.axis_index("x"), lax.axis_index("y"), lax.axis_index("c")
        me = 4 * x + 2 * y + c
        local, remote = [], []
        for k, (_, scatter) in enumerate(self.items):
            own = ins[k].at[me] if scatter else ins[k]
            local.append(pltpu.make_async_copy(own, outs[k].at[me], local_sems.at[k]))
        for r in range(1, N_DEV):
            px = 1 - x if r & 4 else x
            py = 1 - y if r & 2 else y
            pc = 1 - c if r & 1 else c
            peer = 4 * px + 2 * py + pc
            for k, (_, scatter) in enumerate(self.items):
                src = ins[k].at[peer] if scatter else ins[k]
                remote.append(pltpu.make_async_remote_copy(
                    src_ref=src, dst_ref=outs[k].at[me],
                    send_sem=send_sems.at[k, r - 1], recv_sem=recv_sems.at[k, r - 1],
                    device_id=(px, py, pc), device_id_type=pl.DeviceIdType.MESH))

        def start():
            for cp in local + remote:
                cp.start()

        def finish():
            for cp in remote:
                cp.wait_recv()
            for cp in remote:
                cp.wait_send()
            for cp in local:
                cp.wait()

        _run_phases(first, last, start, compute, finish)


def _run_phases(first, last, start, compute, finish):
    if first is None:
        start()
        compute()
        finish()
    else:
        pl.when(first)(start)
        compute()
        pl.when(last)(finish)


class _ChipGather(_Exchange):
    def run(self, ins, outs, sems, first, last, compute):
        send_sems, recv_sems, local_sems = sems
        x, y, c = lax.axis_index("x"), lax.axis_index("y"), lax.axis_index("c")
        sibling = (x, y, 1 - c)
        chips = [(1 - x, y), (x, 1 - y), (1 - x, 1 - y)]
        slot = lambda px, py, pc: 4 * px + 2 * py + pc

        def copy(k, n, src, dst_slot, to):
            return pltpu.make_async_remote_copy(
                src_ref=src, dst_ref=outs[k].at[dst_slot], send_sem=send_sems.at[k, n], recv_sem=recv_sems.at[k, n],
                device_id=to, device_id_type=pl.DeviceIdType.MESH)

        local, own, passed, arrivals = [], [], [], []
        for k in range(self.n):
            me = slot(x, y, c)
            local.append(pltpu.make_async_copy(ins[k], outs[k].at[me], local_sems.at[k]))
            own.append(copy(k, 0, ins[k], me, sibling))
            arrivals.append(copy(k, 0, ins[k], slot(*sibling), sibling))
            for j, chip in enumerate(chips):
                theirs = slot(*chip, c)
                own.append(copy(k, 1 + j, ins[k], me, (*chip, c)))
                passed.append((copy(k, 1 + j, ins[k], theirs, sibling),
                               copy(k, 4 + j, outs[k].at[theirs], theirs, sibling)))
                arrivals.append(copy(k, 4 + j, ins[k], slot(*chip, 1 - c), sibling))

        def start():
            for cp in local + own:
                cp.start()

        def finish():
            for landed, onward in passed:
                landed.wait_recv()
                onward.start()
            for cp in arrivals:
                cp.wait_recv()
            for cp in own + [onward for _, onward in passed]:
                cp.wait_send()
            for cp in local:
                cp.wait()

        _run_phases(first, last, start, compute, finish)


def _grid_ends(grid):
    ids = [pl.program_id(d) for d in range(len(grid))]
    first = functools.reduce(jnp.logical_and, [i == 0 for i in ids])
    last = functools.reduce(jnp.logical_and, [i == g - 1 for i, g in zip(ids, grid)])
    return first, last


def _host(core, n_in, n_out, n_scratch, hosted, grid):
    if hosted is None:
        return core
    nh = hosted.n

    def body(*refs):
        ins, rest = refs[:n_in], refs[n_in:]
        h_ins, rest = rest[:nh], rest[nh:]
        outs, rest = rest[:n_out], rest[n_out:]
        h_outs, rest = rest[:nh], rest[nh:]
        scratch, sems = rest[:n_scratch], rest[n_scratch:]
        first, last = _grid_ends(grid)
        hosted.run(h_ins, h_outs, sems, first, last, lambda: core(*ins, *outs, *scratch))

    return body


def _hosted_parts(hosted):
    if hosted is None:
        return [], [], [], []
    return list(hosted.specs), list(hosted.out_shape), list(hosted.arrays), list(hosted.scratch)


def _exchange(name, items):
    ex = _Exchange(items)
    n = ex.n

    def body(*refs):
        ex.run(refs[:n], refs[n:2 * n], refs[2 * n:], None, None, lambda: None)

    return pl.pallas_call(
        body, name=name, out_shape=tuple(ex.out_shape), in_specs=ex.specs, out_specs=tuple(ex.specs),
        scratch_shapes=ex.scratch,
    )(*ex.arrays)


def _matmul_rows(a, w, *, name, out_dtype=F32):
    t, k = a.shape
    n = w.shape[1]
    assert w.shape[0] == k
    bt = _row_block(t)

    def body(a_ref, w_ref, o_ref):
        o_ref[...] = jnp.dot(a_ref[...], w_ref[...], preferred_element_type=F32).astype(o_ref.dtype)

    return pl.pallas_call(
        body, name=name, grid=(t // bt,),
        in_specs=[pl.BlockSpec((bt, k), lambda i: (i, 0)), pl.BlockSpec((k, n), lambda i: (0, 0))],
        out_specs=pl.BlockSpec((bt, n), lambda i: (i, 0)),
        out_shape=jax.ShapeDtypeStruct((t, n), out_dtype), compiler_params=_params(1),
    )(a, w)


def _matmul_tn(a, b, *, name):
    t, m = a.shape
    n = b.shape[1]
    bt = TOKEN_STEP if t % TOKEN_STEP == 0 else ROW_BLOCK
    bm = _divisor_block(m, WIDE_BLOCK)
    bn = _divisor_block(n, WIDE_BLOCK)
    steps = t // bt

    def body(a_ref, b_ref, o_ref, acc):
        step = pl.program_id(2)

        @pl.when(step == 0)
        def _():
            acc[...] = jnp.zeros_like(acc)

        acc[...] += _dot_tn(a_ref[...], b_ref[...])

        @pl.when(step == steps - 1)
        def _():
            o_ref[...] = acc[...].astype(o_ref.dtype)

    return pl.pallas_call(
        body, name=name, grid=(m // bm, n // bn, steps),
        in_specs=[pl.BlockSpec((bt, bm), lambda i, j, s: (s, i)), pl.BlockSpec((bt, bn), lambda i, j, s: (s, j))],
        out_specs=pl.BlockSpec((bm, bn), lambda i, j, s: (i, j)),
        out_shape=jax.ShapeDtypeStruct((m, n), BF16), scratch_shapes=[pltpu.VMEM((bm, bn), F32)],
        compiler_params=_params(3),
    )(a, b)


def _rmsnorm_fwd(x, g, *, group, name, hosted=None):
    t, w = x.shape
    bt = ROW_BLOCK

    def body(x_ref, g_ref, o_ref, x16_ref):
        x16_ref[...] = x_ref[...].astype(BF16)
        for s in range(0, w, group):
            xs = x_ref[:, s:s + group].astype(F32)
            r = lax.rsqrt(jnp.mean(xs * xs, axis=-1, keepdims=True) + EPS)
            o_ref[:, s:s + group] = (xs * r * g_ref[:, s:s + group]).astype(o_ref.dtype)

    grid = (t // bt,)
    h_specs, h_shapes, h_args, h_scratch = _hosted_parts(hosted)
    rows = pl.BlockSpec((bt, w), lambda i: (i, 0))
    return pl.pallas_call(
        _host(body, 2, 2, 0, hosted, grid), name=name, grid=grid,
        in_specs=[rows, pl.BlockSpec((1, w), lambda i: (0, 0))] + h_specs,
        out_specs=tuple([rows, rows] + h_specs),
        out_shape=tuple([jax.ShapeDtypeStruct((t, w), BF16)] * 2 + h_shapes),
        scratch_shapes=h_scratch, compiler_params=_params(1),
    )(x, g, *h_args)


def _norm_input_grad(terms, x, g, *, group, name, out_dtypes, resid=None, init=None, hosted=None, k_chunks=1):
    t, w = x.shape
    n_terms = len(terms)
    kc = [a.shape[1] // k_chunks for a, _, _ in terms]
    per_row = sum(c * a.dtype.itemsize for c, (a, _, _) in zip(kc, terms))
    per_row += w * sum(r.dtype.itemsize for r in (x, resid, init) if r is not None)
    per_row += w * sum(jnp.dtype(dt).itemsize for dt in out_dtypes)
    fixed = 2 * sum(w * c * 2 for c in kc)
    bt = next(b for b in (2 * ROW_BLOCK, ROW_BLOCK, ROW_BLOCK // 2, ROW_BLOCK // 4)
              if t % b == 0 and fixed + 2 * b * per_row + 5 * b * w * 4 <= MATMUL_VMEM_BUDGET)
    resid_at = 2 * n_terms + 2
    init_at = resid_at + (resid is not None)
    n_in = init_at + (init is not None)
    grid = (t // bt, k_chunks)

    def body(*refs):
        x_ref, g_ref = refs[2 * n_terms], refs[2 * n_terms + 1]
        dx_refs, dg_ref = refs[n_in:n_in + len(out_dtypes)], refs[n_in + len(out_dtypes)]
        chunk = pl.program_id(1)

        @pl.when((pl.program_id(0) == 0) & (chunk == 0))
        def _():
            dg_ref[...] = jnp.zeros_like(dg_ref)

        def matmuls(rows):
            part = None
            for k in range(n_terms):
                if terms[k][2]:
                    term = jnp.dot(refs[2 * k][rows, :], refs[2 * k + 1][...], preferred_element_type=F32)
                else:
                    term = _dot_nt(refs[2 * k][rows, :], refs[2 * k + 1][...])
                part = term if part is None else part + term
            return part

        def norm_backward(rows, dh):
            for s in range(0, w, group):
                xs = x_ref[rows, s:s + group].astype(F32)
                dhs = dh[:, s:s + group]
                r = lax.rsqrt(jnp.mean(xs * xs, axis=-1, keepdims=True) + EPS)
                xh = xs * r
                dg_ref[:, s:s + group] += jnp.sum(dhs * xh, axis=0, keepdims=True)
                dxh = dhs * g_ref[:, s:s + group]
                dx = r * (dxh - xh * jnp.mean(dxh * xh, axis=-1, keepdims=True))
                if resid is not None:
                    dx = refs[resid_at][rows, s:s + group] + dx
                for dx_ref in dx_refs:
                    dx_ref[rows, s:s + group] = dx.astype(dx_ref.dtype)

        if k_chunks == 1:
            halves = [slice(0, bt // 2), slice(bt // 2, bt)]
            dh = [matmuls(rows) if init is None else refs[init_at][rows, :] + matmuls(rows) for rows in halves]
            for rows, dh_half in zip(halves, dh):
                norm_backward(rows, dh_half)
            return
        dh_ref = refs[-1]
        part = matmuls(slice(None))

        @pl.when(chunk == 0)
        def _():
            dh_ref[...] = part if init is None else refs[init_at][...] + part

        @pl.when(chunk > 0)
        def _():
            dh_ref[...] += part

        @pl.when(chunk == k_chunks - 1)
        def _():
            norm_backward(slice(None), dh_ref[...])

    row = pl.BlockSpec((bt, w), lambda i, k: (i, 0))
    vec = pl.BlockSpec((1, w), lambda i, k: (0, 0))
    in_specs, args = [], []
    for c, (a, wt, w_is_kn) in zip(kc, terms):
        assert wt.shape == ((a.shape[1], w) if w_is_kn else (w, a.shape[1]))
        w_spec = pl.BlockSpec((c, w), lambda i, k: (k, 0)) if w_is_kn else pl.BlockSpec((w, c), lambda i, k: (0, k))
        in_specs += [pl.BlockSpec((bt, c), lambda i, k: (i, k)), w_spec]
        args += [a, wt]
    extra = [r for r in (resid, init) if r is not None]
    in_specs += [row, vec] + [row] * len(extra)
    args += [x, g] + extra
    h_specs, h_shapes, h_args, h_scratch = _hosted_parts(hosted)
    own_scratch = [pltpu.VMEM((bt, w), F32)] if k_chunks > 1 else []
    return pl.pallas_call(
        _host(body, n_in, len(out_dtypes) + 1, len(own_scratch), hosted, grid), name=name, grid=grid,
        in_specs=in_specs + h_specs,
        out_specs=tuple([row] * len(out_dtypes) + [vec] + h_specs),
        out_shape=tuple([jax.ShapeDtypeStruct((t, w), dt) for dt in out_dtypes] + [jax.ShapeDtypeStruct((1, w), F32)]
                        + h_shapes),
        scratch_shapes=own_scratch + h_scratch, compiler_params=_params(2),
    )(*args, *h_args)


def _tile_plan(tile):
    is_q = tile < N_PAIRS
    is_dil = (tile % N_PAIRS) >= N_PAIRS // 2
    return is_q, is_dil, (0 if is_q else 2) + (1 if is_dil else 0)


def _proj_tile(kind, pair):
    return kind * N_PAIRS + pair


def _segment_ones():
    lane = np.arange(LANES)
    return jnp.asarray((lane[:, None] // HEAD_DIM) == (lane[None, :] // HEAD_DIM), BF16)


def _rope_tables(seq):
    inv_freq = jnp.power(jnp.float32(ROPE_THETA), -jnp.arange(ROPE_HALF, dtype=F32) * 2.0 / ROPE_DIM)
    ang = jnp.arange(seq).astype(F32)[:, None] * inv_freq[None, :]
    cos, sin = jnp.cos(ang), jnp.sin(ang)
    ones = jnp.ones((seq, HEAD_DIM - ROPE_DIM), F32)
    zeros = jnp.zeros((seq, HEAD_DIM - ROPE_DIM), F32)
    zh = jnp.zeros((seq, ROPE_HALF), F32)
    cos_t = jnp.concatenate([cos, cos, ones], axis=1)
    sin_a = jnp.concatenate([-sin, zh, zeros], axis=1)
    sin_b = jnp.concatenate([zh, sin, zeros], axis=1)
    return tuple(jnp.tile(tab, (1, 2)) for tab in (cos_t, sin_a, sin_b))


def _log_sigmoid(z):
    return jnp.minimum(z, 0.0) - jnp.log1p(jnp.exp(-jnp.abs(z)))


def _aug_placement():
    place = np.zeros((N_PAIRS, LANES, LANES), np.float32)
    for is_k in range(2):
        for pair in range(N_PAIRS // 2):
            for e in range(2):
                other = HEAD_DIM * (1 - e)
                ones_at = other + (AUG_C if is_k else AUG_ONE)
                c_at = other + (AUG_ONE if is_k else AUG_C)
                for n in range(3):
                    place[4 * is_k + pair, N_HEADS_FOX * n + 2 * pair + e, c_at + n] = -1.0 if is_k else 1.0
                    place[4 * is_k + pair, 3 * N_HEADS_FOX, ones_at + n] = 1.0
    return jnp.asarray(place, BF16)


def _qk_prep_fwd(h1, w_main_t, w_fa_t, b_pad, gains, rope, seq):
    t = h1.shape[0]
    bt = ROW_BLOCK
    nsb = seq // bt
    seg = _segment_ones()
    rr = np.arange(bt)
    tri = jnp.asarray(rr[:, None] <= rr[None, :], BF16)

    def body(h_ref, w_ref, wfa_ref, b_ref, g_ref, cos_ref, sa_ref, sb_ref, seg_ref, tri_ref, place_ref,
             p_ref, qk_ref, fa_ref, carry):
        @pl.when(pl.program_id(0) % nsb == 0)
        def _():
            carry[...] = jnp.zeros_like(carry)

        def project(first):
            cols = slice(first * LANES, (first + 2) * LANES)
            y = _dot_nt(h_ref[...], w_ref[cols, :]).astype(BF16)
            p_ref[:, cols] = y
            return y

        lane = lax.broadcasted_iota(jnp.int32, (bt, LANES), 1)
        fa = _dot_nt(h_ref[...], wfa_ref[...])
        fa_ref[...] = fa
        logf = jnp.where(lane < N_HEADS_FOX, _log_sigmoid(fa + b_ref[...]), 0.0)
        c_rows = _split_dot(logf.T[0:N_HEADS_FOX, :], tri_ref[...], 3) + carry[:, 0:1]
        carry[...] = jnp.broadcast_to(c_rows[:, bt - 1:bt], carry.shape)
        cblk = jnp.concatenate([c_rows, jnp.zeros((LANES - N_HEADS_FOX, bt), F32)], axis=0).T
        packed = jnp.where(lane == 3 * N_HEADS_FOX, 1.0, 0.0)
        rest = cblk * LOG2E
        for n in range(3):
            term = rest.astype(BF16).astype(F32)
            packed = packed + (pltpu.roll(term, N_HEADS_FOX * n, 1) if n else term)
            rest = rest - term
        packed = packed.astype(BF16)
        low = lane < HEAD_DIM

        ahead = project(0)
        for tile in range(2 * N_PAIRS):
            is_q, is_dil, grow = _tile_plan(tile)
            pair = tile % N_PAIRS
            assert tile == _proj_tile(0 if is_q else 1, pair)
            if tile % 2 == 0:
                both = ahead
                if tile + 2 < 2 * N_PAIRS:
                    ahead = project(tile + 2)
                if tile % 4 == 2:
                    project(2 * N_PAIRS + tile // 2 - 1)
            xs = both[:, (tile % 2) * LANES:(tile % 2 + 1) * LANES].astype(F32)
            r = lax.rsqrt(_split_dot(xs * xs, seg_ref[...], 2) * (1.0 / HEAD_DIM) + EPS)
            yv = xs * r * g_ref[grow:grow + 1, :]
            if is_dil:
                yv = (yv * cos_ref[...] + pltpu.roll(yv, LANES - ROPE_HALF, 1) * sa_ref[...]
                      + pltpu.roll(yv, ROPE_HALF, 1) * sb_ref[...])
                aug = jnp.zeros((bt, LANES), F32)
            else:
                aug = jnp.dot(packed, place_ref[(0 if is_q else N_PAIRS // 2) + pair], preferred_element_type=F32)
            if is_q:
                yv = yv * (HEAD_DIM ** -0.5 * LOG2E)
            dst = ((0 if is_q else N_HEADS) + 2 * pair) * LANES
            qk_ref[:, dst:dst + LANES] = jnp.where(low, yv, aug).astype(BF16)
            qk_ref[:, dst + LANES:dst + 2 * LANES] = jnp.where(low, aug, yv).astype(BF16)

    row128 = pl.BlockSpec((bt, LANES), lambda i: (i, 0))
    rope_spec = pl.BlockSpec((bt, LANES), lambda i: (i % nsb, 0))
    const = lambda shape: pl.BlockSpec(shape, lambda i: (0,) * len(shape))
    return pl.pallas_call(
        body, name="in_proj_qk_prep", grid=(t // bt,),
        in_specs=[pl.BlockSpec((bt, D_MODEL), lambda i: (i, 0)), const((MAIN_COLS, D_MODEL)),
                  const((LANES, D_MODEL)), const((1, LANES)), const((8, LANES)), rope_spec, rope_spec, rope_spec,
                  const((LANES, LANES)), const((bt, bt)), const((N_PAIRS, LANES, LANES))],
        out_specs=(pl.BlockSpec((bt, MAIN_COLS), lambda i: (i, 0)),
                   pl.BlockSpec((bt, 2 * N_HEADS * LANES), lambda i: (i, 0)), row128),
        out_shape=(jax.ShapeDtypeStruct((t, MAIN_COLS), BF16),
                   jax.ShapeDtypeStruct((t, 2 * N_HEADS * LANES), BF16), jax.ShapeDtypeStruct((t, LANES), F32)),
        scratch_shapes=[pltpu.VMEM((8, LANES), F32)], compiler_params=_params(1),
    )(h1, w_main_t, w_fa_t, b_pad, gains, *rope, seg, tri, _aug_placement())


def _qk_prep_bwd(dq, dk, dqx, dkx, dv, proj, fa, b_pad, gains, rope, seq):
    t = proj.shape[0]
    bt = ROW_BLOCK
    nsb = seq // bt
    nblk = t // bt
    seg = _segment_ones()
    rr = np.arange(bt)
    triu = jnp.asarray(rr[:, None] >= rr[None, :], BF16)

    def body(dq_ref, dk_ref, dqx_ref, dkx_ref, dv_ref, p_ref, fa_ref, b_ref, g_ref, cos_ref, sa_ref, sb_ref, seg_ref,
             triu_ref, dp_ref, dg_ref, db_ref, carry):
        step = pl.program_id(0)

        @pl.when(step == 0)
        def _():
            dg_ref[...] = jnp.zeros_like(dg_ref)
            db_ref[...] = jnp.zeros_like(db_ref)

        @pl.when(step % nsb == 0)
        def _():
            carry[...] = jnp.zeros_like(carry)

        for tile in range(2 * N_PAIRS):
            is_q, is_dil, grow = _tile_plan(tile)
            first = _proj_tile(0 if is_q else 1, tile % N_PAIRS) * LANES
            cols = slice(first, first + LANES)
            src = dq_ref if is_q else dk_ref
            half = slice((tile % N_PAIRS) * LANES, (tile % N_PAIRS + 1) * LANES)
            dy = src[:, half].astype(F32)
            dy = dy * (HEAD_DIM ** -0.5 if is_q else LN2)
            if is_dil:
                dy = (dy * cos_ref[...] + pltpu.roll(dy * sa_ref[...], ROPE_HALF, 1)
                      + pltpu.roll(dy * sb_ref[...], LANES - ROPE_HALF, 1))
            xs = p_ref[:, cols].astype(F32)
            r = lax.rsqrt(_split_dot(xs * xs, seg_ref[...], 2) * (1.0 / HEAD_DIM) + EPS)
            xh = xs * r
            dg_ref[tile:tile + 1, :] += jnp.sum(dy * xh, axis=0, keepdims=True)
            dxh = dy * g_ref[grow:grow + 1, :]
            seg_mean = _split_dot(dxh * xh, seg_ref[...], 2) * (1.0 / HEAD_DIM)
            dp_ref[:, cols] = (r * (dxh - xh * seg_mean)).astype(BF16)

        lane = lax.broadcasted_iota(jnp.int32, (bt, LANES), 1)
        dc = jnp.zeros((bt, LANES), F32)
        for h in range(N_HEADS_FOX):
            other = (h // 2) * LANES + HEAD_DIM * (1 - h % 2)
            row_sum = dqx_ref[:, other + AUG_C:other + AUG_C + 1]
            col_sum = dkx_ref[:, other + AUG_ONE:other + AUG_ONE + 1]
            dc = jnp.where(lane == h, row_sum - col_sum, dc)
        d_rows = _split_dot(dc.T[0:N_HEADS_FOX, :], triu_ref[...], 3) + carry[:, 0:1]
        carry[...] = jnp.broadcast_to(d_rows[:, 0:1], carry.shape)
        dlogf = jnp.concatenate([d_rows, jnp.zeros((LANES - N_HEADS_FOX, bt), F32)], axis=0).T
        z = fa_ref[...] + b_ref[...]
        dfa = dlogf * (1.0 / (1.0 + jnp.exp(z)))
        db_ref[0:1, :] += jnp.sum(dfa, axis=0, keepdims=True)
        for group in range(2):
            first = _proj_tile(2, group * (N_PAIRS // 2)) * LANES
            dp_ref[:, first:first + W_GROUP] = dv_ref[:, group * W_GROUP:(group + 1) * W_GROUP]
        dp_ref[:, MAIN_COLS:PROJ_COLS] = dfa.astype(BF16)

    rev = lambda i: nblk - 1 - i
    row = lambda w: pl.BlockSpec((bt, w), lambda i: (rev(i), 0))
    rope_spec = pl.BlockSpec((bt, LANES), lambda i: (rev(i) % nsb, 0))
    const = lambda shape: pl.BlockSpec(shape, lambda i: (0, 0))
    return pl.pallas_call(
        body, name="qk_prep_bwd", grid=(nblk,),
        in_specs=[row(D_MODEL), row(D_MODEL), row(W_GROUP), row(W_GROUP), row(D_MODEL), row(2 * D_MODEL), row(LANES),
                  const((1, LANES)), const((8, LANES)), rope_spec, rope_spec, rope_spec, const((LANES, LANES)),
                  const((bt, bt))],
        out_specs=(row(PROJ_COLS), const((2 * N_PAIRS, LANES)), const((8, LANES))),
        out_shape=(jax.ShapeDtypeStruct((t, PROJ_COLS), BF16),
                   jax.ShapeDtypeStruct((2 * N_PAIRS, LANES), F32), jax.ShapeDtypeStruct((8, LANES), F32)),
        scratch_shapes=[pltpu.VMEM((8, LANES), F32)], compiler_params=_params(1),
    )(dq, dk, dqx, dkx, dv, proj, fa, b_pad, gains, *rope, seg, triu)


def _bias_tables(seq, keys_first):
    nb = seq // ATT_BLOCK
    idx = np.arange(ATT_BLOCK)
    q_idx, k_idx = (idx[None, None, :], idx[None, :, None]) if keys_first else (idx[None, :, None], idx[None, None, :])
    dist = np.arange(nb)[:, None, None] * ATT_BLOCK + q_idx - k_idx
    causal = dist >= 0
    count = np.zeros(dist.shape, np.int32)
    for window, dilation in DILATION_PAIRS:
        count = count + (causal & (dist % dilation == 0) & (dist <= window))
    fox = np.where(causal, 0.0, NEG)
    dil = np.where(count == 3, math.log2(3.0), np.where(count == 2, 1.0, np.where(count == 1, 0.0, NEG)))
    return jnp.asarray(np.stack([fox, dil], axis=0), F32)


def _attn_specs(seq):
    nb = seq // ATT_BLOCK
    col = pl.BlockSpec((seq, LANES), lambda b, j: (b, j))
    heads = lambda off: pl.BlockSpec((seq, 2 * LANES), lambda b, j: (b, off + j))
    v_spec = pl.BlockSpec((seq, LANES), lambda b, j: (b, _proj_tile(2, j)))
    table_spec = pl.BlockSpec((1, nb, ATT_BLOCK, ATT_BLOCK), lambda b, j: (j // (N_PAIRS // 2), 0, 0, 0))
    return col, heads, v_spec, table_spec


def _head_lanes(e, shape, axis):
    pos = lax.broadcasted_iota(jnp.int32, shape, axis)
    return pos < HEAD_DIM if e == 0 else pos >= HEAD_DIM


def _attn_fwd(qk, proj, tables, seq, hosted=None):
    t = qk.shape[0]
    nb = seq // ATT_BLOCK
    blk = ATT_BLOCK

    def body(q_ref, k_ref, v_ref, tab_ref, o_ref, lse_ref):
        mine = [_head_lanes(e, (seq, LANES), 1) for e in range(2)]
        lane = lax.broadcasted_iota(jnp.int32, (seq, LANES), 1)
        v_aug = [jnp.where(mine[e], v_ref[...], (lane == HEAD_DIM * (1 - e)).astype(BF16)) for e in range(2)]
        def scores(i, e):
            heads_e = slice(e * LANES, (e + 1) * LANES)
            s = _dot_nt(q_ref[i * blk:(i + 1) * blk, heads_e], k_ref[0:(i + 1) * blk, heads_e])
            s = jnp.concatenate([s[:, jj * blk:(jj + 1) * blk] + tab_ref[0, i - jj] for jj in range(i + 1)], axis=1)
            return s, jnp.max(s, axis=1, keepdims=True)

        chains = [(i, e) for i in reversed(range(nb)) for e in range(2)]
        ahead = 2
        pending = [scores(*chain) for chain in chains[:ahead]]
        done = {}
        for n, (i, e) in enumerate(chains):
            s, m = pending.pop(0)
            if n + ahead < len(chains):
                pending.append(scores(*chains[n + ahead]))
            acc = jnp.dot(jnp.exp2(s - m).astype(BF16), v_aug[e][0:(i + 1) * blk], preferred_element_type=F32)
            ones_at = HEAD_DIM * (1 - e)
            l = acc[:, ones_at:ones_at + 1]
            done[e] = (acc / l, m + jnp.log2(l))
            if e == 1:
                rows = slice(i * blk, (i + 1) * blk)
                o_ref[rows, :] = jnp.where(mine[0][rows], done[0][0], done[1][0]).astype(o_ref.dtype)
                lse_ref[rows, :] = jnp.where(mine[0][rows], done[0][1], done[1][1])

    col, heads, v_spec, table_spec = _attn_specs(seq)
    grid = (t // seq, N_PAIRS)
    h_specs, h_shapes, h_args, h_scratch = _hosted_parts(hosted)
    return pl.pallas_call(
        _host(body, 4, 2, 0, hosted, grid), name="attn_fwd", grid=grid,
        in_specs=[heads(0), heads(N_PAIRS), v_spec, table_spec] + h_specs,
        out_specs=tuple([col, col] + h_specs),
        out_shape=tuple([jax.ShapeDtypeStruct((t, D_MODEL), BF16), jax.ShapeDtypeStruct((t, D_MODEL), F32)] + h_shapes),
        scratch_shapes=h_scratch, compiler_params=_params(2),
    )(qk, qk, proj, tables, *h_args)


def _attn_bwd(qk, proj, tables, o, lse, do, seq, hosted=None):
    t = qk.shape[0]
    nb = seq // ATT_BLOCK
    blk = ATT_BLOCK
    group = math.gcd(nb, ATT_GROUP)

    def body(q_ref, k_ref, v_ref, tab_ref, o_ref, lse_ref, do_ref,
             dq_ref, dk_ref, dv_ref, dqx_ref, dkx_ref, dk_acc, dv_acc):
        mine = [_head_lanes(e, (blk, LANES), 1) for e in range(2)]
        top = _head_lanes(0, (LANES, blk), 0)
        head_rows = lax.broadcasted_iota(jnp.int32, (8, LANES), 0)
        head_of_lane = lax.broadcasted_iota(jnp.int32, (8, LANES), 1) // HEAD_DIM
        head_sel = (head_rows == head_of_lane).astype(BF16)
        dk_acc[...] = jnp.zeros_like(dk_acc)
        dv_acc[...] = jnp.zeros_like(dv_acc)

        def block_rows(i):
            return pl.ds(pl.multiple_of(i * blk, blk), blk)

        def q_group(g, _):
            base = g * group
            qs, doe, delta, lse_e = [], [], [], []
            for b in range(group):
                rows = block_rows(base + b)
                qs.append([q_ref[rows, e * LANES:(e + 1) * LANES] for e in range(2)])
                do_blk = do_ref[rows, :]
                doe.append([jnp.where(mine[e], do_blk, jnp.zeros_like(do_blk)) for e in range(2)])
                delta_t = _split_dot_nt(head_sel, do_blk.astype(F32) * o_ref[rows, :].astype(F32), 3)
                lse_t = _split_dot_nt(head_sel, lse_ref[rows, :], 3) * (1.0 / HEAD_DIM)
                delta.append([delta_t[e:e + 1, :] for e in range(2)])
                lse_e.append([lse_t[e:e + 1, :] for e in range(2)])

            def key_block(dq_t, jj, members):
                krows = block_rows(jj)
                v = v_ref[krows, :]
                dq_t = [list(d) for d in dq_t]
                lo, hi = slice(0, blk // 2), slice(blk // 2, blk)
                dv_part = [None, None]
                add = lambda acc, part: part if acc is None else acc + part

                def probs(k_sub, v_sub, keys, queries, b, e, dist):
                    q_sub, do_sub = qs[b][e][queries], doe[b][e][queries]
                    p_t = jnp.exp2(_dot_nt(k_sub, q_sub) + tab_ref[0, dist, keys, queries] - lse_e[b][e][:, queries])
                    ds_t = (p_t * (_dot_nt(v_sub, do_sub) - delta[b][e][:, queries])).astype(BF16)
                    return p_t.astype(BF16), ds_t, q_sub, do_sub, k_sub

                def outputs(tile):
                    p_t, ds_t, q_sub, do_sub, k_sub = tile
                    return (jnp.dot(p_t, do_sub, preferred_element_type=F32),
                            jnp.dot(ds_t, q_sub, preferred_element_type=F32), _dot_tn(k_sub, ds_t))

                for e in range(2):
                    k_e = k_ref[krows, e * LANES:(e + 1) * LANES]
                    dk_part = [None, None]
                    tiles = []
                    for b, dist in members:
                        if isinstance(dist, int) and dist == 0:
                            tiles.append((b, probs(k_e[lo], v[lo], lo, slice(0, blk), b, e, dist),
                                          probs(k_e[hi], v[hi], hi, hi, b, e, dist)))
                        else:
                            tiles.append((b, probs(k_e, v, slice(0, blk), slice(0, blk), b, e, dist), None))
                    for b, first, second in tiles:
                        if second is not None:
                            dv_a, dk_a, dq_a = outputs(first)
                            dv_b, dk_b, dq_b = outputs(second)
                            halves = ((dv_a, dk_a), (dv_b, dk_b))
                            dq = jnp.concatenate([dq_a[:, lo], dq_a[:, hi] + dq_b], axis=1)
                        else:
                            dv_f, dk_f, dq = outputs(first)
                            halves = ((dv_f[lo], dk_f[lo]), (dv_f[hi], dk_f[hi]))
                        for n, (dv_h, dk_h) in enumerate(halves):
                            dv_part[n] = add(dv_part[n], dv_h)
                            dk_part[n] = add(dk_part[n], dk_h)
                        dq_t[b][e] = dq_t[b][e] + dq
                    dk_acc[e, krows, :] += jnp.concatenate(dk_part, axis=0)
                dv_acc[krows, :] += jnp.concatenate(dv_part, axis=0)
                return tuple(tuple(d) for d in dq_t)

            zacc = jnp.zeros((LANES, blk), F32)
            dq_t = tuple((zacc, zacc) for _ in range(group))
            dq_t = lax.fori_loop(
                0, base, lambda jj, st: key_block(st, jj, [(b, base + b - jj) for b in range(group)]), dq_t)
            for a in range(group):
                dq_t = key_block(dq_t, base + a, [(b, b - a) for b in range(a, group)])
            for b in range(group):
                rows = block_rows(base + b)
                dq_ref[rows, :] = jnp.where(top, dq_t[b][0], dq_t[b][1]).T.astype(BF16)
                dqx_ref[rows, :] = jnp.where(top, dq_t[b][1], dq_t[b][0]).T
            return 0

        lax.fori_loop(0, nb // group, q_group, 0)
        lo = _head_lanes(0, (seq, LANES), 1)
        dk_ref[...] = jnp.where(lo, dk_acc[0], dk_acc[1]).astype(BF16)
        dkx_ref[...] = jnp.where(lo, dk_acc[1], dk_acc[0])
        dv_ref[...] = dv_acc[...].astype(dv_ref.dtype)

    col, heads, v_spec, table_spec = _attn_specs(seq)
    grid = (t // seq, N_PAIRS)
    h_specs, h_shapes, h_args, h_scratch = _hosted_parts(hosted)
    f32_out = jax.ShapeDtypeStruct((t, D_MODEL), F32)
    return pl.pallas_call(
        _host(body, 7, 5, 2, hosted, grid), name="attn_bwd", grid=grid,
        in_specs=[heads(0), heads(N_PAIRS), v_spec, table_spec, col, col, col] + h_specs,
        out_specs=tuple([col] * 5 + h_specs),
        out_shape=tuple([jax.ShapeDtypeStruct((t, D_MODEL), BF16)] * 3 + [f32_out, f32_out] + h_shapes),
        scratch_shapes=[pltpu.VMEM((2, seq, LANES), F32), pltpu.VMEM((seq, LANES), F32)] + h_scratch,
        compiler_params=_params(2),
    )(qk, qk, proj, tables, o, lse, do, *h_args)


def _row_block(t):
    return 2 * ROW_BLOCK if t % (2 * ROW_BLOCK) == 0 else ROW_BLOCK


def _out_proj_ffn_norm(o, g_out, w_out, x, g_ffn):
    t = o.shape[0]
    bt = _row_block(t)

    def body(o_ref, go_ref, w_ref, x_ref, gf_ref, on_ref, x2_ref, h2_ref):
        halves = [slice(0, bt // 2), slice(bt // 2, bt)]
        x2s = []
        for rows in halves:
            for s in range(0, D_MODEL, W_GROUP):
                os_ = o_ref[rows, s:s + W_GROUP].astype(F32)
                r = lax.rsqrt(jnp.mean(os_ * os_, axis=-1, keepdims=True) + EPS)
                on_ref[rows, s:s + W_GROUP] = (os_ * r * go_ref[:, s:s + W_GROUP]).astype(BF16)
            x2s.append(x_ref[rows, :] + jnp.dot(on_ref[rows, :], w_ref[...], preferred_element_type=F32))
        for rows, x2 in zip(halves, x2s):
            x2_ref[rows, :] = x2.astype(BF16)
            r2 = lax.rsqrt(jnp.mean(x2 * x2, axis=-1, keepdims=True) + EPS)
            h2_ref[rows, :] = (x2 * r2 * gf_ref[...]).astype(BF16)

    row = pl.BlockSpec((bt, D_MODEL), lambda i: (i, 0))
    vec = pl.BlockSpec((1, D_MODEL), lambda i: (0, 0))
    return pl.pallas_call(
        body, name="out_proj", grid=(t // bt,),
        in_specs=[row, vec, pl.BlockSpec((D_MODEL, D_MODEL), lambda i: (0, 0)), row, vec],
        out_specs=(row, row, row),
        out_shape=tuple([jax.ShapeDtypeStruct((t, D_MODEL), BF16)] * 3),
        compiler_params=_params(1),
    )(o, g_out, w_out, x, g_ffn)


def _ffn_gate_up(h2, w_gate_t, w_up_t):
    t = h2.shape[0]
    bt = _row_block(t)
    bn = _divisor_block(D_FF, WIDE_BLOCK)

    def body(h_ref, wg_ref, wu_ref, a_ref, u_ref, f_ref):
        a = _dot_nt(h_ref[...], wg_ref[...])
        u = _dot_nt(h_ref[...], wu_ref[...])
        a_ref[...] = a.astype(BF16)
        u_ref[...] = u.astype(BF16)
        f_ref[...] = (a * jax.nn.sigmoid(a) * u).astype(BF16)

    blk = pl.BlockSpec((bt, bn), lambda j, i: (i, j))
    w_blk = pl.BlockSpec((bn, D_MODEL), lambda j, i: (j, 0))
    shape = jax.ShapeDtypeStruct((t, D_FF), BF16)
    return pl.pallas_call(
        body, name="ffn_gate_up", grid=(D_FF // bn, t // bt),
        in_specs=[pl.BlockSpec((bt, D_MODEL), lambda j, i: (i, 0)), w_blk, w_blk],
        out_specs=(blk, blk, blk), out_shape=(shape, shape, shape), compiler_params=_params(2),
    )(h2, w_gate_t, w_up_t)


def _ffn_down_grad(dy16, w_down, a, u):
    t = a.shape[0]
    bt = _row_block(t)
    bn = _divisor_block(D_FF, WIDE_BLOCK)

    def body(dy_ref, w_ref, a_ref, u_ref, da_ref, du_ref):
        df = _dot_nt(dy_ref[...], w_ref[...])
        av = a_ref[...].astype(F32)
        sg = jax.nn.sigmoid(av)
        da_ref[...] = (df * u_ref[...].astype(F32) * sg * (1.0 + av * (1.0 - sg))).astype(BF16)
        du_ref[...] = (df * av * sg).astype(BF16)

    blk = pl.BlockSpec((bt, bn), lambda j, i: (i, j))
    shape = jax.ShapeDtypeStruct((t, D_FF), BF16)
    return pl.pallas_call(
        body, name="d_ffn_down", grid=(D_FF // bn, t // bt),
        in_specs=[pl.BlockSpec((bt, D_MODEL), lambda j, i: (i, 0)), pl.BlockSpec((bn, D_MODEL), lambda j, i: (j, 0)),
                  blk, blk],
        out_specs=(blk, blk), out_shape=(shape, shape), compiler_params=_params(2),
    )(dy16, w_down, a, u)


def _ffn_down_loss(f, w_down, x2, target):
    t, w = x2.shape
    bt = _row_block(t)

    def body(f_ref, w_ref, x_ref, t_ref, dy16_ref, loss_ref):
        @pl.when(pl.program_id(0) == 0)
        def _():
            loss_ref[...] = jnp.zeros_like(loss_ref)

        err = (x_ref[...] + jnp.dot(f_ref[...], w_ref[...], preferred_element_type=F32)) - t_ref[...]
        dy16_ref[...] = (err * (1.0 / w)).astype(BF16)
        loss_ref[...] += 0.5 * jnp.sum(jnp.mean(err * err, axis=-1, keepdims=True), axis=0, keepdims=True)

    row = pl.BlockSpec((bt, w), lambda i: (i, 0))
    return pl.pallas_call(
        body, name="ffn_down_loss", grid=(t // bt,),
        in_specs=[pl.BlockSpec((bt, D_FF), lambda i: (i, 0)), pl.BlockSpec((D_FF, w), lambda i: (0, 0)), row, row],
        out_specs=(row, pl.BlockSpec((8, LANES), lambda i: (0, 0))),
        out_shape=(jax.ShapeDtypeStruct((t, w), BF16), jax.ShapeDtypeStruct((8, LANES), F32)),
        compiler_params=_params(1),
    )(f, w_down, x2, target)


def _adamw(parts, w, m, v, *, name):
    _, rows, cols = w.shape
    br = rows if rows <= 512 else 256
    assert rows % br == 0

    def body(p_ref, w_ref, m_ref, v_ref, g_ref, d_ref, nm_ref, nv_ref):
        g = p_ref[0].astype(F32)
        for r in range(1, N_DEV):
            g = g + p_ref[r].astype(F32)
        m2 = ADAM_B1 * m_ref[0] + (1.0 - ADAM_B1) * g
        v2 = ADAM_B2 * v_ref[0] + (1.0 - ADAM_B2) * jnp.square(g)
        m_hat = m2 / (1.0 - ADAM_B1 ** ADAM_STEP)
        v_hat = v2 / (1.0 - ADAM_B2 ** ADAM_STEP)
        g_ref[0] = g
        d_ref[0] = -ADAM_LR * (m_hat / (jnp.sqrt(v_hat) + ADAM_EPS) + ADAM_WD * w_ref[0])
        nm_ref[0] = m2
        nv_ref[0] = v2

    blk = pl.BlockSpec((1, br, cols), lambda i: (0, i, 0))
    shape = jax.ShapeDtypeStruct((1, rows, cols), F32)
    return pl.pallas_call(
        body, name=name, grid=(rows // br,),
        in_specs=[pl.BlockSpec((N_DEV, br, cols), lambda i: (0, i, 0)), blk, blk, blk],
        out_specs=(blk, blk, blk, blk), out_shape=(shape, shape, shape, shape), compiler_params=_params(1),
    )(parts, w, m, v)


_QA, _KA, _VA, _FA, _QD, _KD, _VD = (0, 512), (512, 1024), (1024, 1536), (1536, 1544), (1544, 2056), (2056, 2568), (2568, 3080)
_MAIN_ORDER = (_QA, _QD, _KA, _KD, _VA, _VD)
MAIN_COLS = 3 * D_MODEL
PROJ_COLS = MAIN_COLS + LANES
COL_SHARDED = ("w_in", "w_gate", "w_up")


def _swap(w):
    return jnp.transpose(w, (0, 2, 1))


def _w_in_to_kernel(w_t):
    main = jnp.concatenate([w_t[a:b] for a, b in _MAIN_ORDER], axis=0)
    forget = jnp.pad(w_t[_FA[0]:_FA[1]], ((0, LANES - N_HEADS_FOX), (0, 0)))
    return main, forget


def _w_in_from_kernel(g_t):
    pos = {span: i * W_GROUP for i, span in enumerate(_MAIN_ORDER)}
    parts = []
    for span in (_QA, _KA, _VA, _FA, _QD, _KD, _VD):
        if span == _FA:
            parts.append(g_t[MAIN_COLS:MAIN_COLS + N_HEADS_FOX])
        else:
            parts.append(g_t[pos[span]:pos[span] + W_GROUP])
    return jnp.concatenate(parts, axis=0)


def _pack_small(vals):
    rows = []
    for name, _, n_rows in SMALL_LAYOUT:
        flat = vals[name].reshape(-1).astype(F32)
        rows.append(jnp.pad(flat, (0, n_rows * LANES - flat.shape[0])).reshape(n_rows, LANES))
    packed = jnp.concatenate(rows, axis=0)
    return jnp.pad(packed, ((0, SMALL_ROWS - packed.shape[0]), (0, 0)))


def _unpack_small(packed, like):
    out = {}
    for name, row, n_rows in SMALL_LAYOUT:
        n = like[name].size
        out[name] = packed[row:row + n_rows].reshape(-1)[:n].reshape(like[name].shape)
    return out


def _device_step(x, target, small, shards):
    bsz, seq, _ = x.shape
    t = bsz * seq
    xf = x.reshape(t, D_MODEL)
    tf = target.reshape(t, D_MODEL)
    row = lambda v: v.reshape(1, -1)
    g_out = jnp.concatenate([small["g_out_fox"], small["g_out_dil"]]).reshape(1, D_MODEL)
    gains = jnp.concatenate(
        [jnp.tile(small[n].reshape(1, HEAD_DIM), (1, 2)) for n in ("g_q_fox", "g_q_dil", "g_k_fox", "g_k_dil")]
        + [jnp.zeros((4, LANES), F32)], axis=0)
    b_pad = jnp.pad(small["b_forget"].reshape(1, N_HEADS_FOX), ((0, 0), (0, LANES - N_HEADS_FOX)))
    rope = _rope_tables(seq)
    tables_qk = _bias_tables(seq, keys_first=False)
    tables_kq = _bias_tables(seq, keys_first=True)

    h1, x16, g_in = _rmsnorm_fwd(xf, row(small["g_mix"]), group=D_MODEL, name="norm_mix",
                                 hosted=_ChipGather([(shards["w_in"], False)]))
    w_main_t, w_fa_t = _w_in_to_kernel(g_in.reshape(IN_COLS, D_MODEL))
    w_in_all_t = jnp.concatenate([w_main_t, w_fa_t], axis=0)
    proj, qk, fa = _qk_prep_fwd(h1, w_main_t, w_fa_t, b_pad, gains, rope, seq)
    late = _Exchange([(shards[n], False) for n in ("w_out", "w_gate", "w_up", "w_down")])
    o, lse, g_out_w, g_gate, g_up, g_down = _attn_fwd(qk, proj, tables_qk, seq, hosted=late)
    w_out = g_out_w.reshape(D_MODEL, D_MODEL)
    w_gate_t = g_gate.reshape(D_FF, D_MODEL)
    w_up_t = g_up.reshape(D_FF, D_MODEL)
    w_down = g_down.reshape(D_FF, D_MODEL)
    on, x2, h2 = _out_proj_ffn_norm(o, g_out, w_out, x16, row(small["g_ffn"]))
    a, u, f = _ffn_gate_up(h2, w_gate_t, w_up_t)
    dy16, loss_tile = _ffn_down_loss(f, w_down, x2, tf)

    da, du = _ffn_down_grad(dy16, w_down, a, u)
    gw_down = _matmul_tn(f, dy16, name="gw_down")
    gw_gate_t = _matmul_tn(da, h2, name="gw_gate")
    gw_up_t = _matmul_tn(du, h2, name="gw_up")
    dh2_gate = _matmul_rows(da, w_gate_t, name="d_ffn_gate", out_dtype=BF16)
    dx2_16, dg_ffn = _norm_input_grad([(du, w_up_t, True)], x2, row(small["g_ffn"]), group=D_MODEL,
                                      name="d_ffn_up", out_dtypes=(BF16,), resid=dy16, init=dh2_gate)
    gw_out = _matmul_tn(on, dx2_16, name="gw_out")
    do, dg_out = _norm_input_grad([(dx2_16, w_out, False)], o, g_out, group=W_GROUP, name="d_out_proj",
                                  out_dtypes=(BF16,))

    shard_rows = lambda g: g.reshape(N_DEV, g.shape[0] // N_DEV, g.shape[1])
    ffn_grads = _Exchange([(shard_rows(g), True) for g in (gw_out, gw_gate_t, gw_up_t, gw_down)])
    dq, dk, dv, dqx, dkx, p_out, p_gate, p_up, p_down = _attn_bwd(qk, proj, tables_kq, o, lse, do, seq, hosted=ffn_grads)
    dproj, dgains, db = _qk_prep_bwd(dq, dk, dqx, dkx, dv, proj, fa, b_pad, gains, rope, seq)
    gw_in_t = _matmul_tn(dproj, h1, name="gw_in")
    in_grad = _Exchange([(shard_rows(_w_in_from_kernel(gw_in_t)), True)])
    dx, dg_mix, p_in = _norm_input_grad([(dproj, w_in_all_t, True)], x16, row(small["g_mix"]), group=D_MODEL,
                                        name="d_in_proj", out_dtypes=(F32,), resid=dx2_16, hosted=in_grad)

    fold = lambda rows: jnp.sum(rows[:, :HEAD_DIM] + rows[:, HEAD_DIM:], axis=0)
    half = N_PAIRS // 2
    gsmall = {
        "g_mix": dg_mix, "g_ffn": dg_ffn, "g_out_fox": dg_out[0, :W_GROUP], "g_out_dil": dg_out[0, W_GROUP:],
        "g_q_fox": fold(dgains[0:half]), "g_q_dil": fold(dgains[half:N_PAIRS]),
        "g_k_fox": fold(dgains[N_PAIRS:N_PAIRS + half]), "g_k_dil": fold(dgains[N_PAIRS + half:]),
        "b_forget": db[0, :N_HEADS_FOX],
    }
    packed = _pack_small(gsmall).at[LOSS_ROW].set(loss_tile[0])
    (p_small,) = _exchange("small_exchange", [(packed, False)])
    parts = {"w_in": p_in, "w_out": p_out, "w_gate": p_gate, "w_up": p_up, "w_down": p_down}
    return dx.reshape(x.shape), parts, p_small


def kernel(x, g_mix, w_in, b_forget, g_q_fox, g_k_fox, g_q_dil, g_k_dil, g_out_fox, g_out_dil, w_out, g_ffn, w_gate, w_up, w_down, loss_target, m_g_mix, m_w_in, m_b_forget, m_g_q_fox, m_g_k_fox, m_g_q_dil, m_g_k_dil, m_g_out_fox, m_g_out_dil, m_w_out, m_g_ffn, m_w_gate, m_w_up, m_w_down, v_g_mix, v_w_in, v_b_forget, v_g_q_fox, v_g_k_fox, v_g_q_dil, v_g_k_dil, v_g_out_fox, v_g_out_dil, v_w_out, v_g_ffn, v_w_gate, v_w_up, v_w_down):
    args = dict(locals())
    small_names = [name for name, _, _ in SMALL_LAYOUT]
    big_names = ["w_in", "w_out", "w_gate", "w_up", "w_down"]
    small = {n: args[n][0] for n in small_names}

    as_rows = lambda n, w: _swap(w) if n in COL_SHARDED else w
    shards = {n: as_rows(n, args[n])[0].astype(BF16) for n in big_names}
    grad_x, parts, p_small = _device_step(x, loss_target, small, shards)

    grads, deltas, new_m, new_v = {}, {}, {}, {}
    for n in big_names:
        res = _adamw(parts[n], as_rows(n, args[n]), as_rows(n, args["m_" + n]), as_rows(n, args["v_" + n]),
                     name="adamw_" + n)
        grads[n], deltas[n], new_m[n], new_v[n] = [as_rows(n, r) for r in res]
    res = _adamw(p_small, _pack_small(small)[None], _pack_small({n: args["m_" + n][0] for n in small_names})[None],
                 _pack_small({n: args["v_" + n][0] for n in small_names})[None], name="adamw_small")
    loss = res[0][0, LOSS_ROW, 0]
    for dst, packed_res in zip((grads, deltas, new_m, new_v), res):
        for n, val in _unpack_small(packed_res[0], small).items():
            dst[n] = val[None]

    order = ["g_mix", "w_in", "b_forget", "g_q_fox", "g_k_fox", "g_q_dil", "g_k_dil", "g_out_fox", "g_out_dil",
             "w_out", "g_ffn", "w_gate", "w_up", "w_down"]
    return (loss, grad_x, *[grads[n] for n in order], *[deltas[n] for n in order],
            *[new_m[n] for n in order], *[new_v[n] for n in order])
```

```python
import functools
import math

import jax
import jax.numpy as jnp
import numpy as np
from jax import lax
from jax.experimental import pallas as pl
from jax.experimental.pallas import tpu as pltpu

F32 = jnp.float32
BF16 = jnp.bfloat16

D_MODEL = 1024
HEAD_DIM = 64
LANES = 128
N_PAIRS = D_MODEL // LANES
N_HEADS = 2 * N_PAIRS
N_HEADS_FOX = 8
W_GROUP = 512
D_FF = 2816
IN_COLS = 3080
DILATION_PAIRS = ((128, 1), (512, 4), (2048, 16))
ROPE_THETA = 500000.0
ROPE_DIM = 16
ROPE_HALF = ROPE_DIM // 2
EPS = 1e-6
NEG = -1e30
LOG2E = 1.4426950408889634
LN2 = 0.6931471805599453
AUG_ONE = 0
AUG_C = 3
N_DEV = 8

ADAM_LR = 0.001
ADAM_B1 = 0.9
ADAM_B2 = 0.999
ADAM_EPS = 1e-08
ADAM_WD = 0.01
ADAM_STEP = 10

ROW_BLOCK = 512
TOKEN_STEP = 2048
WIDE_BLOCK = D_FF // 2
ATT_BLOCK = 512
ATT_GROUP = 4
VMEM_LIMIT = 56 * 1024 * 1024
MATMUL_VMEM_BUDGET = 44 * 1024 * 1024

SMALL_ROWS = 32
SMALL_LAYOUT = (("g_mix", 0, 8), ("g_ffn", 8, 8), ("g_out_fox", 16, 4), ("g_out_dil", 20, 4),
                ("g_q_fox", 24, 1), ("g_k_fox", 25, 1), ("g_q_dil", 26, 1), ("g_k_dil", 27, 1),
                ("b_forget", 28, 1))
LOSS_ROW = 29


def _params(n_grid):
    return pltpu.CompilerParams(dimension_semantics=("arbitrary",) * n_grid, vmem_limit_bytes=VMEM_LIMIT)


def _divisor_block(n, cap):
    best = None
    for b in range(LANES, min(n, cap) + 1, LANES):
        if n % b == 0:
            best = b
    assert best is not None, n
    return best


def _split_dot(a, b_exact, terms):
    acc = None
    rest = a
    for _ in range(terms):
        hi = rest.astype(BF16)
        part = jnp.dot(hi, b_exact, preferred_element_type=F32)
        acc = part if acc is None else acc + part
        rest = rest - hi.astype(F32)
    return acc


def _split_dot_nt(a_exact, b, terms):
    acc = None
    rest = b
    for _ in range(terms):
        hi = rest.astype(BF16)
        part = _dot_nt(a_exact, hi)
        acc = part if acc is None else acc + part
        rest = rest - hi.astype(F32)
    return acc


def _dot_nt(a, b):
    return lax.dot_general(a, b, (((1,), (1,)), ((), ())), preferred_element_type=F32)


def _dot_tn(a, b):
    return lax.dot_general(a, b, (((0,), (0,)), ((), ())), preferred_element_type=F32)


class _Exchange:
    def __init__(self, items):
        self.items = items
        self.n = len(items)
        self.arrays = [a for a, _ in items]
        self.out_shape = [jax.ShapeDtypeStruct((N_DEV,) + tuple(a.shape[1:] if sc else a.shape), a.dtype)
                          for a, sc in items]
        self.specs = [pl.BlockSpec(memory_space=pl.ANY)] * self.n
        self.scratch = [pltpu.SemaphoreType.DMA((self.n, N_DEV - 1)), pltpu.SemaphoreType.DMA((self.n, N_DEV - 1)),
                        pltpu.SemaphoreType.DMA((self.n,))]

    def run(self, ins, outs, sems, first, last, compute):
        send_sems, recv_sems, local_sems = sems
        x, y, c = lax.axis_index("x"), lax.axis_index("y"), lax.axis_index("c")
        me = 4 * x + 2 * y + c
        local, remote = [], []
        for k, (_, scatter) in enumerate(self.items):
            own = ins[k].at[me] if scatter else ins[k]
            local.append(pltpu.make_async_copy(own, outs[k].at[me], local_sems.at[k]))
        for r in range(1, N_DEV):
            px = 1 - x if r & 4 else x
            py = 1 - y if r & 2 else y
            pc = 1 - c if r & 1 else c
            peer = 4 * px + 2 * py + pc
            for k, (_, scatter) in enumerate(self.items):
                src = ins[k].at[peer] if scatter else ins[k]
                remote.append(pltpu.make_async_remote_copy(
                    src_ref=src, dst_ref=outs[k].at[me],
                    send_sem=send_sems.at[k, r - 1], recv_sem=recv_sems.at[k, r - 1],
                    device_id=(px, py, pc), device_id_type=pl.DeviceIdType.MESH))

        def start():
            for cp in local + remote:
                cp.start()

        def finish():
            for cp in remote:
                cp.wait_recv()
            for cp in remote:
                cp.wait_send()
            for cp in local:
                cp.wait()

        _run_phases(first, last, start, compute, finish)


def _run_phases(first, last, start, compute, finish):
    if first is None:
        start()
        compute()
        finish()
    else:
        pl.when(first)(start)
        compute()
        pl.when(last)(finish)


class _ChipGather(_Exchange):
    def run(self, ins, outs, sems, first, last, compute):
        send_sems, recv_sems, local_sems = sems
        x, y, c = lax.axis_index("x"), lax.axis_index("y"), lax.axis_index("c")
        sibling = (x, y, 1 - c)
        chips = [(1 - x, y), (x, 1 - y), (1 - x, 1 - y)]
        slot = lambda px, py, pc: 4 * px + 2 * py + pc

        def copy(k, n, src, dst_slot, to):
            return pltpu.make_async_remote_copy(
                src_ref=src, dst_ref=outs[k].at[dst_slot], send_sem=send_sems.at[k, n], recv_sem=recv_sems.at[k, n],
                device_id=to, device_id_type=pl.DeviceIdType.MESH)

        local, own, passed, arrivals = [], [], [], []
        for k in range(self.n):
            me = slot(x, y, c)
            local.append(pltpu.make_async_copy(ins[k], outs[k].at[me], local_sems.at[k]))
            own.append(copy(k, 0, ins[k], me, sibling))
            arrivals.append(copy(k, 0, ins[k], slot(*sibling), sibling))
            for j, chip in enumerate(chips):
                theirs = slot(*chip, c)
                own.append(copy(k, 1 + j, ins[k], me, (*chip, c)))
                passed.append((copy(k, 1 + j, ins[k], theirs, sibling),
                               copy(k, 4 + j, outs[k].at[theirs], theirs, sibling)))
                arrivals.append(copy(k, 4 + j, ins[k], slot(*chip, 1 - c), sibling))

        def start():
            for cp in local + own:
                cp.start()

        def finish():
            for landed, onward in passed:
                landed.wait_recv()
                onward.start()
            for cp in arrivals:
                cp.wait_recv()
            for cp in own + [onward for _, onward in passed]:
                cp.wait_send()
            for cp in local:
                cp.wait()

        _run_phases(first, last, start, compute, finish)


def _grid_ends(grid):
    ids = [pl.program_id(d) for d in range(len(grid))]
    first = functools.reduce(jnp.logical_and, [i == 0 for i in ids])
    last = functools.reduce(jnp.logical_and, [i == g - 1 for i, g in zip(ids, grid)])
    return first, last


def _host(core, n_in, n_out, n_scratch, hosted, grid):
    if hosted is None:
        return core
    nh = hosted.n

    def body(*refs):
        ins, rest = refs[:n_in], refs[n_in:]
        h_ins, rest = rest[:nh], rest[nh:]
        outs, rest = rest[:n_out], rest[n_out:]
        h_outs, rest = rest[:nh], rest[nh:]
        scratch, sems = rest[:n_scratch], rest[n_scratch:]
        first, last = _grid_ends(grid)
        hosted.run(h_ins, h_outs, sems, first, last, lambda: core(*ins, *outs, *scratch))

    return body


def _hosted_parts(hosted):
    if hosted is None:
        return [], [], [], []
    return list(hosted.specs), list(hosted.out_shape), list(hosted.arrays), list(hosted.scratch)


def _exchange(name, items):
    ex = _Exchange(items)
    n = ex.n

    def body(*refs):
        ex.run(refs[:n], refs[n:2 * n], refs[2 * n:], None, None, lambda: None)

    return pl.pallas_call(
        body, name=name, out_shape=tuple(ex.out_shape), in_specs=ex.specs, out_specs=tuple(ex.specs),
        scratch_shapes=ex.scratch,
    )(*ex.arrays)


def _matmul_rows(a, w, *, name, out_dtype=F32):
    t, k = a.shape
    n = w.shape[1]
    assert w.shape[0] == k
    bt = _row_block(t)

    def body(a_ref, w_ref, o_ref):
        o_ref[...] = jnp.dot(a_ref[...], w_ref[...], preferred_element_type=F32).astype(o_ref.dtype)

    return pl.pallas_call(
        body, name=name, grid=(t // bt,),
        in_specs=[pl.BlockSpec((bt, k), lambda i: (i, 0)), pl.BlockSpec((k, n), lambda i: (0, 0))],
        out_specs=pl.BlockSpec((bt, n), lambda i: (i, 0)),
        out_shape=jax.ShapeDtypeStruct((t, n), out_dtype), compiler_params=_params(1),
    )(a, w)


def _matmul_tn(a, b, *, name):
    t, m = a.shape
    n = b.shape[1]
    bt = TOKEN_STEP if t % TOKEN_STEP == 0 else ROW_BLOCK
    bm = _divisor_block(m, WIDE_BLOCK)
    bn = _divisor_block(n, WIDE_BLOCK)
    steps = t // bt

    def body(a_ref, b_ref, o_ref, acc):
        step = pl.program_id(2)

        @pl.when(step == 0)
        def _():
            acc[...] = jnp.zeros_like(acc)

        acc[...] += _dot_tn(a_ref[...], b_ref[...])

        @pl.when(step == steps - 1)
        def _():
            o_ref[...] = acc[...].astype(o_ref.dtype)

    return pl.pallas_call(
        body, name=name, grid=(m // bm, n // bn, steps),
        in_specs=[pl.BlockSpec((bt, bm), lambda i, j, s: (s, i)), pl.BlockSpec((bt, bn), lambda i, j, s: (s, j))],
        out_specs=pl.BlockSpec((bm, bn), lambda i, j, s: (i, j)),
        out_shape=jax.ShapeDtypeStruct((m, n), BF16), scratch_shapes=[pltpu.VMEM((bm, bn), F32)],
        compiler_params=_params(3),
    )(a, b)


def _rmsnorm_fwd(x, g, *, group, name, hosted=None):
    t, w = x.shape
    bt = ROW_BLOCK

    def body(x_ref, g_ref, o_ref, x16_ref):
        x16_ref[...] = x_ref[...].astype(BF16)
        for s in range(0, w, group):
            xs = x_ref[:, s:s + group].astype(F32)
            r = lax.rsqrt(jnp.mean(xs * xs, axis=-1, keepdims=True) + EPS)
            o_ref[:, s:s + group] = (xs * r * g_ref[:, s:s + group]).astype(o_ref.dtype)

    grid = (t // bt,)
    h_specs, h_shapes, h_args, h_scratch = _hosted_parts(hosted)
    rows = pl.BlockSpec((bt, w), lambda i: (i, 0))
    return pl.pallas_call(
        _host(body, 2, 2, 0, hosted, grid), name=name, grid=grid,
        in_specs=[rows, pl.BlockSpec((1, w), lambda i: (0, 0))] + h_specs,
        out_specs=tuple([rows, rows] + h_specs),
        out_shape=tuple([jax.ShapeDtypeStruct((t, w), BF16)] * 2 + h_shapes),
        scratch_shapes=h_scratch, compiler_params=_params(1),
    )(x, g, *h_args)


def _norm_input_grad(terms, x, g, *, group, name, out_dtypes, resid=None, init=None, hosted=None, k_chunks=1):
    t, w = x.shape
    n_terms = len(terms)
    kc = [a.shape[1] // k_chunks for a, _, _ in terms]
    per_row = sum(c * a.dtype.itemsize for c, (a, _, _) in zip(kc, terms))
    per_row += w * sum(r.dtype.itemsize for r in (x, resid, init) if r is not None)
    per_row += w * sum(jnp.dtype(dt).itemsize for dt in out_dtypes)
    fixed = 2 * sum(w * c * 2 for c in kc)
    bt = next(b for b in (2 * ROW_BLOCK, ROW_BLOCK, ROW_BLOCK // 2, ROW_BLOCK // 4)
              if t % b == 0 and fixed + 2 * b * per_row + 5 * b * w * 4 <= MATMUL_VMEM_BUDGET)
    resid_at = 2 * n_terms + 2
    init_at = resid_at + (resid is not None)
    n_in = init_at + (init is not None)
    grid = (t // bt, k_chunks)

    def body(*refs):
        x_ref, g_ref = refs[2 * n_terms], refs[2 * n_terms + 1]
        dx_refs, dg_ref = refs[n_in:n_in + len(out_dtypes)], refs[n_in + len(out_dtypes)]
        chunk = pl.program_id(1)

        @pl.when((pl.program_id(0) == 0) & (chunk == 0))
        def _():
            dg_ref[...] = jnp.zeros_like(dg_ref)

        def matmuls(rows):
            part = None
            for k in range(n_terms):
                if terms[k][2]:
                    term = jnp.dot(refs[2 * k][rows, :], refs[2 * k + 1][...], preferred_element_type=F32)
                else:
                    term = _dot_nt(refs[2 * k][rows, :], refs[2 * k + 1][...])
                part = term if part is None else part + term
            return part

        def norm_backward(rows, dh):
            for s in range(0, w, group):
                xs = x_ref[rows, s:s + group].astype(F32)
                dhs = dh[:, s:s + group]
                r = lax.rsqrt(jnp.mean(xs * xs, axis=-1, keepdims=True) + EPS)
                xh = xs * r
                dg_ref[:, s:s + group] += jnp.sum(dhs * xh, axis=0, keepdims=True)
                dxh = dhs * g_ref[:, s:s + group]
                dx = r * (dxh - xh * jnp.mean(dxh * xh, axis=-1, keepdims=True))
                if resid is not None:
                    dx = refs[resid_at][rows, s:s + group] + dx
                for dx_ref in dx_refs:
                    dx_ref[rows, s:s + group] = dx.astype(dx_ref.dtype)

        if k_chunks == 1:
            halves = [slice(0, bt // 2), slice(bt // 2, bt)]
            dh = [matmuls(rows) if init is None else refs[init_at][rows, :] + matmuls(rows) for rows in halves]
            for rows, dh_half in zip(halves, dh):
                norm_backward(rows, dh_half)
            return
        dh_ref = refs[-1]
        part = matmuls(slice(None))

        @pl.when(chunk == 0)
        def _():
            dh_ref[...] = part if init is None else refs[init_at][...] + part

        @pl.when(chunk > 0)
        def _():
            dh_ref[...] += part

        @pl.when(chunk == k_chunks - 1)
        def _():
            norm_backward(slice(None), dh_ref[...])

    row = pl.BlockSpec((bt, w), lambda i, k: (i, 0))
    vec = pl.BlockSpec((1, w), lambda i, k: (0, 0))
    in_specs, args = [], []
    for c, (a, wt, w_is_kn) in zip(kc, terms):
        assert wt.shape == ((a.shape[1], w) if w_is_kn else (w, a.shape[1]))
        w_spec = pl.BlockSpec((c, w), lambda i, k: (k, 0)) if w_is_kn else pl.BlockSpec((w, c), lambda i, k: (0, k))
        in_specs += [pl.BlockSpec((bt, c), lambda i, k: (i, k)), w_spec]
        args += [a, wt]
    extra = [r for r in (resid, init) if r is not None]
    in_specs += [row, vec] + [row] * len(extra)
    args += [x, g] + extra
    h_specs, h_shapes, h_args, h_scratch = _hosted_parts(hosted)
    own_scratch = [pltpu.VMEM((bt, w), F32)] if k_chunks > 1 else []
    return pl.pallas_call(
        _host(body, n_in, len(out_dtypes) + 1, len(own_scratch), hosted, grid), name=name, grid=grid,
        in_specs=in_specs + h_specs,
        out_specs=tuple([row] * len(out_dtypes) + [vec] + h_specs),
        out_shape=tuple([jax.ShapeDtypeStruct((t, w), dt) for dt in out_dtypes] + [jax.ShapeDtypeStruct((1, w), F32)]
                        + h_shapes),
        scratch_shapes=own_scratch + h_scratch, compiler_params=_params(2),
    )(*args, *h_args)


def _tile_plan(tile):
    is_q = tile < N_PAIRS
    is_dil = (tile % N_PAIRS) >= N_PAIRS // 2
    return is_q, is_dil, (0 if is_q else 2) + (1 if is_dil else 0)


def _proj_tile(kind, pair):
    return kind * N_PAIRS + pair


def _segment_ones():
    lane = np.arange(LANES)
    return jnp.asarray((lane[:, None] // HEAD_DIM) == (lane[None, :] // HEAD_DIM), BF16)


def _rope_tables(seq):
    inv_freq = jnp.power(jnp.float32(ROPE_THETA), -jnp.arange(ROPE_HALF, dtype=F32) * 2.0 / ROPE_DIM)
    ang = jnp.arange(seq).astype(F32)[:, None] * inv_freq[None, :]
    cos, sin = jnp.cos(ang), jnp.sin(ang)
    ones = jnp.ones((seq, HEAD_DIM - ROPE_DIM), F32)
    zeros = jnp.zeros((seq, HEAD_DIM - ROPE_DIM), F32)
    zh = jnp.zeros((seq, ROPE_HALF), F32)
    cos_t = jnp.concatenate([cos, cos, ones], axis=1)
    sin_a = jnp.concatenate([-sin, zh, zeros], axis=1)
    sin_b = jnp.concatenate([zh, sin, zeros], axis=1)
    return tuple(jnp.tile(tab, (1, 2)) for tab in (cos_t, sin_a, sin_b))


def _log_sigmoid(z):
    return jnp.minimum(z, 0.0) - jnp.log1p(jnp.exp(-jnp.abs(z)))


def _aug_placement():
    place = np.zeros((N_PAIRS, LANES, LANES), np.float32)
    for is_k in range(2):
        for pair in range(N_PAIRS // 2):
            for e in range(2):
                other = HEAD_DIM * (1 - e)
                ones_at = other + (AUG_C if is_k else AUG_ONE)
                c_at = other + (AUG_ONE if is_k else AUG_C)
                for n in range(3):
                    place[4 * is_k + pair, N_HEADS_FOX * n + 2 * pair + e, c_at + n] = -1.0 if is_k else 1.0
                    place[4 * is_k + pair, 3 * N_HEADS_FOX, ones_at + n] = 1.0
    return jnp.asarray(place, BF16)


def _qk_prep_fwd(h1, w_main_t, w_fa_t, b_pad, gains, rope, seq):
    t = h1.shape[0]
    bt = ROW_BLOCK
    nsb = seq // bt
    seg = _segment_ones()
    rr = np.arange(bt)
    tri = jnp.asarray(rr[:, None] <= rr[None, :], BF16)

    def body(h_ref, w_ref, wfa_ref, b_ref, g_ref, cos_ref, sa_ref, sb_ref, seg_ref, tri_ref, place_ref,
             p_ref, qk_ref, fa_ref, carry):
        @pl.when(pl.program_id(0) % nsb == 0)
        def _():
            carry[...] = jnp.zeros_like(carry)

        def project(first):
            cols = slice(first * LANES, (first + 2) * LANES)
            y = _dot_nt(h_ref[...], w_ref[cols, :]).astype(BF16)
            p_ref[:, cols] = y
            return y

        lane = lax.broadcasted_iota(jnp.int32, (bt, LANES), 1)
        fa = _dot_nt(h_ref[...], wfa_ref[...])
        fa_ref[...] = fa
        logf = jnp.where(lane < N_HEADS_FOX, _log_sigmoid(fa + b_ref[...]), 0.0)
        c_rows = _split_dot(logf.T[0:N_HEADS_FOX, :], tri_ref[...], 3) + carry[:, 0:1]
        carry[...] = jnp.broadcast_to(c_rows[:, bt - 1:bt], carry.shape)
        cblk = jnp.concatenate([c_rows, jnp.zeros((LANES - N_HEADS_FOX, bt), F32)], axis=0).T
        packed = jnp.where(lane == 3 * N_HEADS_FOX, 1.0, 0.0)
        rest = cblk * LOG2E
        for n in range(3):
            term = rest.astype(BF16).astype(F32)
            packed = packed + (pltpu.roll(term, N_HEADS_FOX * n, 1) if n else term)
            rest = rest - term
        packed = packed.astype(BF16)
        low = lane < HEAD_DIM

        ahead = project(0)
        for tile in range(2 * N_PAIRS):
            is_q, is_dil, grow = _tile_plan(tile)
            pair = tile % N_PAIRS
            assert tile == _proj_tile(0 if is_q else 1, pair)
            if tile % 2 == 0:
                both = ahead
                if tile + 2 < 2 * N_PAIRS:
                    ahead = project(tile + 2)
                if tile % 4 == 2:
                    project(2 * N_PAIRS + tile // 2 - 1)
            xs = both[:, (tile % 2) * LANES:(tile % 2 + 1) * LANES].astype(F32)
            r = lax.rsqrt(_split_dot(xs * xs, seg_ref[...], 2) * (1.0 / HEAD_DIM) + EPS)
            yv = xs * r * g_ref[grow:grow + 1, :]
            if is_dil:
                yv = (yv * cos_ref[...] + pltpu.roll(yv, LANES - ROPE_HALF, 1) * sa_ref[...]
                      + pltpu.roll(yv, ROPE_HALF, 1) * sb_ref[...])
                aug = jnp.zeros((bt, LANES), F32)
            else:
                aug = jnp.dot(packed, place_ref[(0 if is_q else N_PAIRS // 2) + pair], preferred_element_type=F32)
            if is_q:
                yv = yv * (HEAD_DIM ** -0.5 * LOG2E)
            dst = ((0 if is_q else N_HEADS) + 2 * pair) * LANES
            qk_ref[:, dst:dst + LANES] = jnp.where(low, yv, aug).astype(BF16)
            qk_ref[:, dst + LANES:dst + 2 * LANES] = jnp.where(low, aug, yv).astype(BF16)

    row128 = pl.BlockSpec((bt, LANES), lambda i: (i, 0))
    rope_spec = pl.BlockSpec((bt, LANES), lambda i: (i % nsb, 0))
    const = lambda shape: pl.BlockSpec(shape, lambda i: (0,) * len(shape))
    return pl.pallas_call(
        body, name="in_proj_qk_prep", grid=(t // bt,),
        in_specs=[pl.BlockSpec((bt, D_MODEL), lambda i: (i, 0)), const((MAIN_COLS, D_MODEL)),
                  const((LANES, D_MODEL)), const((1, LANES)), const((8, LANES)), rope_spec, rope_spec, rope_spec,
                  const((LANES, LANES)), const((bt, bt)), const((N_PAIRS, LANES, LANES))],
        out_specs=(pl.BlockSpec((bt, MAIN_COLS), lambda i: (i, 0)),
                   pl.BlockSpec((bt, 2 * N_HEADS * LANES), lambda i: (i, 0)), row128),
        out_shape=(jax.ShapeDtypeStruct((t, MAIN_COLS), BF16),
                   jax.ShapeDtypeStruct((t, 2 * N_HEADS * LANES), BF16), jax.ShapeDtypeStruct((t, LANES), F32)),
        scratch_shapes=[pltpu.VMEM((8, LANES), F32)], compiler_params=_params(1),
    )(h1, w_main_t, w_fa_t, b_pad, gains, *rope, seg, tri, _aug_placement())


def _qk_prep_bwd(dq, dk, dqx, dkx, dv, proj, fa, b_pad, gains, rope, seq):
    t = proj.shape[0]
    bt = ROW_BLOCK
    nsb = seq // bt
    nblk = t // bt
    seg = _segment_ones()
    rr = np.arange(bt)
    triu = jnp.asarray(rr[:, None] >= rr[None, :], BF16)

    def body(dq_ref, dk_ref, dqx_ref, dkx_ref, dv_ref, p_ref, fa_ref, b_ref, g_ref, cos_ref, sa_ref, sb_ref, seg_ref,
             triu_ref, dp_ref, dg_ref, db_ref, carry):
        step = pl.program_id(0)

        @pl.when(step == 0)
        def _():
            dg_ref[...] = jnp.zeros_like(dg_ref)
            db_ref[...] = jnp.zeros_like(db_ref)

        @pl.when(step % nsb == 0)
        def _():
            carry[...] = jnp.zeros_like(carry)

        for tile in range(2 * N_PAIRS):
            is_q, is_dil, grow = _tile_plan(tile)
            first = _proj_tile(0 if is_q else 1, tile % N_PAIRS) * LANES
            cols = slice(first, first + LANES)
            src = dq_ref if is_q else dk_ref
            half = slice((tile % N_PAIRS) * LANES, (tile % N_PAIRS + 1) * LANES)
            dy = src[:, half].astype(F32)
            dy = dy * (HEAD_DIM ** -0.5 if is_q else LN2)
            if is_dil:
                dy = (dy * cos_ref[...] + pltpu.roll(dy * sa_ref[...], ROPE_HALF, 1)
                      + pltpu.roll(dy * sb_ref[...], LANES - ROPE_HALF, 1))
            xs = p_ref[:, cols].astype(F32)
            r = lax.rsqrt(_split_dot(xs * xs, seg_ref[...], 2) * (1.0 / HEAD_DIM) + EPS)
            xh = xs * r
            dg_ref[tile:tile + 1, :] += jnp.sum(dy * xh, axis=0, keepdims=True)
            dxh = dy * g_ref[grow:grow + 1, :]
            seg_mean = _split_dot(dxh * xh, seg_ref[...], 2) * (1.0 / HEAD_DIM)
            dp_ref[:, cols] = (r * (dxh - xh * seg_mean)).astype(BF16)

        lane = lax.broadcasted_iota(jnp.int32, (bt, LANES), 1)
        dc = jnp.zeros((bt, LANES), F32)
        for h in range(N_HEADS_FOX):
            other = (h // 2) * LANES + HEAD_DIM * (1 - h % 2)
            row_sum = dqx_ref[:, other + AUG_C:other + AUG_C + 1]
            col_sum = dkx_ref[:, other + AUG_ONE:other + AUG_ONE + 1]
            dc = jnp.where(lane == h, row_sum - col_sum, dc)
        d_rows = _split_dot(dc.T[0:N_HEADS_FOX, :], triu_ref[...], 3) + carry[:, 0:1]
        carry[...] = jnp.broadcast_to(d_rows[:, 0:1], carry.shape)
        dlogf = jnp.concatenate([d_rows, jnp.zeros((LANES - N_HEADS_FOX, bt), F32)], axis=0).T
        z = fa_ref[...] + b_ref[...]
        dfa = dlogf * (1.0 / (1.0 + jnp.exp(z)))
        db_ref[0:1, :] += jnp.sum(dfa, axis=0, keepdims=True)
        for group in range(2):
            first = _proj_tile(2, group * (N_PAIRS // 2)) * LANES
            dp_ref[:, first:first + W_GROUP] = dv_ref[:, group * W_GROUP:(group + 1) * W_GROUP]
        dp_ref[:, MAIN_COLS:PROJ_COLS] = dfa.astype(BF16)

    rev = lambda i: nblk - 1 - i
    row = lambda w: pl.BlockSpec((bt, w), lambda i: (rev(i), 0))
    rope_spec = pl.BlockSpec((bt, LANES), lambda i: (rev(i) % nsb, 0))
    const = lambda shape: pl.BlockSpec(shape, lambda i: (0, 0))
    return pl.pallas_call(
        body, name="qk_prep_bwd", grid=(nblk,),
        in_specs=[row(D_MODEL), row(D_MODEL), row(W_GROUP), row(W_GROUP), row(D_MODEL), row(2 * D_MODEL), row(LANES),
                  const((1, LANES)), const((8, LANES)), rope_spec, rope_spec, rope_spec, const((LANES, LANES)),
                  const((bt, bt))],
        out_specs=(row(PROJ_COLS), const((2 * N_PAIRS, LANES)), const((8, LANES))),
        out_shape=(jax.ShapeDtypeStruct((t, PROJ_COLS), BF16),
                   jax.ShapeDtypeStruct((2 * N_PAIRS, LANES), F32), jax.ShapeDtypeStruct((8, LANES), F32)),
        scratch_shapes=[pltpu.VMEM((8, LANES), F32)], compiler_params=_params(1),
    )(dq, dk, dqx, dkx, dv, proj, fa, b_pad, gains, *rope, seg, triu)


def _bias_tables(seq, keys_first):
    nb = seq // ATT_BLOCK
    idx = np.arange(ATT_BLOCK)
    q_idx, k_idx = (idx[None, None, :], idx[None, :, None]) if keys_first else (idx[None, :, None], idx[None, None, :])
    dist = np.arange(nb)[:, None, None] * ATT_BLOCK + q_idx - k_idx
    causal = dist >= 0
    count = np.zeros(dist.shape, np.int32)
    for window, dilation in DILATION_PAIRS:
        count = count + (causal & (dist % dilation == 0) & (dist <= window))
    fox = np.where(causal, 0.0, NEG)
    dil = np.where(count == 3, math.log2(3.0), np.where(count == 2, 1.0, np.where(count == 1, 0.0, NEG)))
    return jnp.asarray(np.stack([fox, dil], axis=0), F32)


def _attn_specs(seq):
    nb = seq // ATT_BLOCK
    col = pl.BlockSpec((seq, LANES), lambda b, j: (b, j))
    heads = lambda off: pl.BlockSpec((seq, 2 * LANES), lambda b, j: (b, off + j))
    v_spec = pl.BlockSpec((seq, LANES), lambda b, j: (b, _proj_tile(2, j)))
    table_spec = pl.BlockSpec((1, nb, ATT_BLOCK, ATT_BLOCK), lambda b, j: (j // (N_PAIRS // 2), 0, 0, 0))
    return col, heads, v_spec, table_spec


def _head_lanes(e, shape, axis):
    pos = lax.broadcasted_iota(jnp.int32, shape, axis)
    return pos < HEAD_DIM if e == 0 else pos >= HEAD_DIM


def _attn_fwd(qk, proj, tables, seq, hosted=None):
    t = qk.shape[0]
    nb = seq // ATT_BLOCK
    blk = ATT_BLOCK

    def body(q_ref, k_ref, v_ref, tab_ref, o_ref, lse_ref):
        mine = [_head_lanes(e, (seq, LANES), 1) for e in range(2)]
        lane = lax.broadcasted_iota(jnp.int32, (seq, LANES), 1)
        v_aug = [jnp.where(mine[e], v_ref[...], (lane == HEAD_DIM * (1 - e)).astype(BF16)) for e in range(2)]
        def scores(i, e):
            heads_e = slice(e * LANES, (e + 1) * LANES)
            s = _dot_nt(q_ref[i * blk:(i + 1) * blk, heads_e], k_ref[0:(i + 1) * blk, heads_e])
            s = jnp.concatenate([s[:, jj * blk:(jj + 1) * blk] + tab_ref[0, i - jj] for jj in range(i + 1)], axis=1)
            return s, jnp.max(s, axis=1, keepdims=True)

        chains = [(i, e) for i in reversed(range(nb)) for e in range(2)]
        ahead = 2
        pending = [scores(*chain) for chain in chains[:ahead]]
        done = {}
        for n, (i, e) in enumerate(chains):
            s, m = pending.pop(0)
            if n + ahead < len(chains):
                pending.append(scores(*chains[n + ahead]))
            acc = jnp.dot(jnp.exp2(s - m).astype(BF16), v_aug[e][0:(i + 1) * blk], preferred_element_type=F32)
            ones_at = HEAD_DIM * (1 - e)
            l = acc[:, ones_at:ones_at + 1]
            done[e] = (acc / l, m + jnp.log2(l))
            if e == 1:
                rows = slice(i * blk, (i + 1) * blk)
                o_ref[rows, :] = jnp.where(mine[0][rows], done[0][0], done[1][0]).astype(o_ref.dtype)
                lse_ref[rows, :] = jnp.where(mine[0][rows], done[0][1], done[1][1])

    col, heads, v_spec, table_spec = _attn_specs(seq)
    grid = (t // seq, N_PAIRS)
    h_specs, h_shapes, h_args, h_scratch = _hosted_parts(hosted)
    return pl.pallas_call(
        _host(body, 4, 2, 0, hosted, grid), name="attn_fwd", grid=grid,
        in_specs=[heads(0), heads(N_PAIRS), v_spec, table_spec] + h_specs,
        out_specs=tuple([col, col] + h_specs),
        out_shape=tuple([jax.ShapeDtypeStruct((t, D_MODEL), BF16), jax.ShapeDtypeStruct((t, D_MODEL), F32)] + h_shapes),
        scratch_shapes=h_scratch, compiler_params=_params(2),
    )(qk, qk, proj, tables, *h_args)


def _attn_bwd(qk, proj, tables, o, lse, do, seq, hosted=None):
    t = qk.shape[0]
    nb = seq // ATT_BLOCK
    blk = ATT_BLOCK
    group = math.gcd(nb, ATT_GROUP)

    def body(q_ref, k_ref, v_ref, tab_ref, o_ref, lse_ref, do_ref,
             dq_ref, dk_ref, dv_ref, dqx_ref, dkx_ref, dk_acc, dv_acc):
        mine = [_head_lanes(e, (blk, LANES), 1) for e in range(2)]
        top = _head_lanes(0, (LANES, blk), 0)
        head_rows = lax.broadcasted_iota(jnp.int32, (8, LANES), 0)
        head_of_lane = lax.broadcasted_iota(jnp.int32, (8, LANES), 1) // HEAD_DIM
        head_sel = (head_rows == head_of_lane).astype(BF16)
        dk_acc[...] = jnp.zeros_like(dk_acc)
        dv_acc[...] = jnp.zeros_like(dv_acc)

        def block_rows(i):
            return pl.ds(pl.multiple_of(i * blk, blk), blk)

        def q_group(g, _):
            base = g * group
            qs, doe, delta, lse_e = [], [], [], []
            for b in range(group):
                rows = block_rows(base + b)
                qs.append([q_ref[rows, e * LANES:(e + 1) * LANES] for e in range(2)])
                do_blk = do_ref[rows, :]
                doe.append([jnp.where(mine[e], do_blk, jnp.zeros_like(do_blk)) for e in range(2)])
                delta_t = _split_dot_nt(head_sel, do_blk.astype(F32) * o_ref[rows, :].astype(F32), 3)
                lse_t = _split_dot_nt(head_sel, lse_ref[rows, :], 3) * (1.0 / HEAD_DIM)
                delta.append([delta_t[e:e + 1, :] for e in range(2)])
                lse_e.append([lse_t[e:e + 1, :] for e in range(2)])

            def key_block(dq_t, jj, members):
                krows = block_rows(jj)
                v = v_ref[krows, :]
                dq_t = [list(d) for d in dq_t]
                lo, hi = slice(0, blk // 2), slice(blk // 2, blk)
                dv_part = [None, None]
                add = lambda acc, part: part if acc is None else acc + part

                def probs(k_sub, v_sub, keys, queries, b, e, dist):
                    q_sub, do_sub = qs[b][e][queries], doe[b][e][queries]
                    p_t = jnp.exp2(_dot_nt(k_sub, q_sub) + tab_ref[0, dist, keys, queries] - lse_e[b][e][:, queries])
                    ds_t = (p_t * (_dot_nt(v_sub, do_sub) - delta[b][e][:, queries])).astype(BF16)
                    return p_t.astype(BF16), ds_t, q_sub, do_sub, k_sub

                def outputs(tile):
                    p_t, ds_t, q_sub, do_sub, k_sub = tile
                    return (jnp.dot(p_t, do_sub, preferred_element_type=F32),
                            jnp.dot(ds_t, q_sub, preferred_element_type=F32), _dot_tn(k_sub, ds_t))

                for e in range(2):
                    k_e = k_ref[krows, e * LANES:(e + 1) * LANES]
                    dk_part = [None, None]
                    tiles = []
                    for b, dist in members:
                        if isinstance(dist, int) and dist == 0:
                            tiles.append((b, probs(k_e[lo], v[lo], lo, slice(0, blk), b, e, dist),
                                          probs(k_e[hi], v[hi], hi, hi, b, e, dist)))
                        else:
                            tiles.append((b, probs(k_e, v, slice(0, blk), slice(0, blk), b, e, dist), None))
                    for b, first, second in tiles:
                        if second is not None:
                            dv_a, dk_a, dq_a = outputs(first)
                            dv_b, dk_b, dq_b = outputs(second)
                            halves = ((dv_a, dk_a), (dv_b, dk_b))
                            dq = jnp.concatenate([dq_a[:, lo], dq_a[:, hi] + dq_b], axis=1)
                        else:
                            dv_f, dk_f, dq = outputs(first)
                            halves = ((dv_f[lo], dk_f[lo]), (dv_f[hi], dk_f[hi]))
                        for n, (dv_h, dk_h) in enumerate(halves):
                            dv_part[n] = add(dv_part[n], dv_h)
                            dk_part[n] = add(dk_part[n], dk_h)
                        dq_t[b][e] = dq_t[b][e] + dq
                    dk_acc[e, krows, :] += jnp.concatenate(dk_part, axis=0)
                dv_acc[krows, :] += jnp.concatenate(dv_part, axis=0)
                return tuple(tuple(d) for d in dq_t)

            zacc = jnp.zeros((LANES, blk), F32)
            dq_t = tuple((zacc, zacc) for _ in range(group))
            dq_t = lax.fori_loop(
                0, base, lambda jj, st: key_block(st, jj, [(b, base + b - jj) for b in range(group)]), dq_t)
            for a in range(group):
                dq_t = key_block(dq_t, base + a, [(b, b - a) for b in range(a, group)])
            for b in range(group):
                rows = block_rows(base + b)
                dq_ref[rows, :] = jnp.where(top, dq_t[b][0], dq_t[b][1]).T.astype(BF16)
                dqx_ref[rows, :] = jnp.where(top, dq_t[b][1], dq_t[b][0]).T
            return 0

        lax.fori_loop(0, nb // group, q_group, 0)
        lo = _head_lanes(0, (seq, LANES), 1)
        dk_ref[...] = jnp.where(lo, dk_acc[0], dk_acc[1]).astype(BF16)
        dkx_ref[...] = jnp.where(lo, dk_acc[1], dk_acc[0])
        dv_ref[...] = dv_acc[...].astype(dv_ref.dtype)

    col, heads, v_spec, table_spec = _attn_specs(seq)
    grid = (t // seq, N_PAIRS)
    h_specs, h_shapes, h_args, h_scratch = _hosted_parts(hosted)
    f32_out = jax.ShapeDtypeStruct((t, D_MODEL), F32)
    return pl.pallas_call(
        _host(body, 7, 5, 2, hosted, grid), name="attn_bwd", grid=grid,
        in_specs=[heads(0), heads(N_PAIRS), v_spec, table_spec, col, col, col] + h_specs,
        out_specs=tuple([col] * 5 + h_specs),
        out_shape=tuple([jax.ShapeDtypeStruct((t, D_MODEL), BF16)] * 3 + [f32_out, f32_out] + h_shapes),
        scratch_shapes=[pltpu.VMEM((2, seq, LANES), F32), pltpu.VMEM((seq, LANES), F32)] + h_scratch,
        compiler_params=_params(2),
    )(qk, qk, proj, tables, o, lse, do, *h_args)


def _row_block(t):
    return 2 * ROW_BLOCK if t % (2 * ROW_BLOCK) == 0 else ROW_BLOCK


def _out_proj_ffn_norm(o, g_out, w_out, x, g_ffn):
    t = o.shape[0]
    bt = _row_block(t)

    def body(o_ref, go_ref, w_ref, x_ref, gf_ref, on_ref, x2_ref, h2_ref):
        halves = [slice(0, bt // 2), slice(bt // 2, bt)]
        x2s = []
        for rows in halves:
            for s in range(0, D_MODEL, W_GROUP):
                os_ = o_ref[rows, s:s + W_GROUP].astype(F32)
                r = lax.rsqrt(jnp.mean(os_ * os_, axis=-1, keepdims=True) + EPS)
                on_ref[rows, s:s + W_GROUP] = (os_ * r * go_ref[:, s:s + W_GROUP]).astype(BF16)
            x2s.append(x_ref[rows, :] + jnp.dot(on_ref[rows, :], w_ref[...], preferred_element_type=F32))
        for rows, x2 in zip(halves, x2s):
            x2_ref[rows, :] = x2.astype(BF16)
            r2 = lax.rsqrt(jnp.mean(x2 * x2, axis=-1, keepdims=True) + EPS)
            h2_ref[rows, :] = (x2 * r2 * gf_ref[...]).astype(BF16)

    row = pl.BlockSpec((bt, D_MODEL), lambda i: (i, 0))
    vec = pl.BlockSpec((1, D_MODEL), lambda i: (0, 0))
    return pl.pallas_call(
        body, name="out_proj", grid=(t // bt,),
        in_specs=[row, vec, pl.BlockSpec((D_MODEL, D_MODEL), lambda i: (0, 0)), row, vec],
        out_specs=(row, row, row),
        out_shape=tuple([jax.ShapeDtypeStruct((t, D_MODEL), BF16)] * 3),
        compiler_params=_params(1),
    )(o, g_out, w_out, x, g_ffn)


def _ffn_gate_up(h2, w_gate_t, w_up_t):
    t = h2.shape[0]
    bt = _row_block(t)
    bn = _divisor_block(D_FF, WIDE_BLOCK)

    def body(h_ref, wg_ref, wu_ref, a_ref, u_ref, f_ref):
        a = _dot_nt(h_ref[...], wg_ref[...])
        u = _dot_nt(h_ref[...], wu_ref[...])
        a_ref[...] = a.astype(BF16)
        u_ref[...] = u.astype(BF16)
        f_ref[...] = (a * jax.nn.sigmoid(a) * u).astype(BF16)

    blk = pl.BlockSpec((bt, bn), lambda j, i: (i, j))
    w_blk = pl.BlockSpec((bn, D_MODEL), lambda j, i: (j, 0))
    shape = jax.ShapeDtypeStruct((t, D_FF), BF16)
    return pl.pallas_call(
        body, name="ffn_gate_up", grid=(D_FF // bn, t // bt),
        in_specs=[pl.BlockSpec((bt, D_MODEL), lambda j, i: (i, 0)), w_blk, w_blk],
        out_specs=(blk, blk, blk), out_shape=(shape, shape, shape), compiler_params=_params(2),
    )(h2, w_gate_t, w_up_t)


def _ffn_down_grad(dy16, w_down, a, u):
    t = a.shape[0]
    bt = _row_block(t)
    bn = _divisor_block(D_FF, WIDE_BLOCK)

    def body(dy_ref, w_ref, a_ref, u_ref, da_ref, du_ref):
        df = _dot_nt(dy_ref[...], w_ref[...])
        av = a_ref[...].astype(F32)
        sg = jax.nn.sigmoid(av)
        da_ref[...] = (df * u_ref[...].astype(F32) * sg * (1.0 + av * (1.0 - sg))).astype(BF16)
        du_ref[...] = (df * av * sg).astype(BF16)

    blk = pl.BlockSpec((bt, bn), lambda j, i: (i, j))
    shape = jax.ShapeDtypeStruct((t, D_FF), BF16)
    return pl.pallas_call(
        body, name="d_ffn_down", grid=(D_FF // bn, t // bt),
        in_specs=[pl.BlockSpec((bt, D_MODEL), lambda j, i: (i, 0)), pl.BlockSpec((bn, D_MODEL), lambda j, i: (j, 0)),
                  blk, blk],
        out_specs=(blk, blk), out_shape=(shape, shape), compiler_params=_params(2),
    )(dy16, w_down, a, u)


def _ffn_down_loss(f, w_down, x2, target):
    t, w = x2.shape
    bt = _row_block(t)

    def body(f_ref, w_ref, x_ref, t_ref, dy16_ref, loss_ref):
        @pl.when(pl.program_id(0) == 0)
        def _():
            loss_ref[...] = jnp.zeros_like(loss_ref)

        halves = [slice(0, bt // 2), slice(bt // 2, bt)]
        ys = [jnp.dot(f_ref[rows, :], w_ref[...], preferred_element_type=F32) for rows in halves]
        for rows, y in zip(halves, ys):
            err = (x_ref[rows, :] + y) - t_ref[rows, :]
            dy16_ref[rows, :] = (err * (1.0 / w)).astype(BF16)
            loss_ref[...] += 0.5 * jnp.sum(jnp.mean(err * err, axis=-1, keepdims=True), axis=0, keepdims=True)

    row = pl.BlockSpec((bt, w), lambda i: (i, 0))
    return pl.pallas_call(
        body, name="ffn_down_loss", grid=(t // bt,),
        in_specs=[pl.BlockSpec((bt, D_FF), lambda i: (i, 0)), pl.BlockSpec((D_FF, w), lambda i: (0, 0)), row, row],
        out_specs=(row, pl.BlockSpec((8, LANES), lambda i: (0, 0))),
        out_shape=(jax.ShapeDtypeStruct((t, w), BF16), jax.ShapeDtypeStruct((8, LANES), F32)),
        compiler_params=_params(1),
    )(f, w_down, x2, target)


def _adamw(parts, w, m, v, *, name):
    _, rows, cols = w.shape
    br = rows if rows <= 512 else 256
    assert rows % br == 0

    def body(p_ref, w_ref, m_ref, v_ref, g_ref, d_ref, nm_ref, nv_ref):
        g = p_ref[0].astype(F32)
        for r in range(1, N_DEV):
            g = g + p_ref[r].astype(F32)
        m2 = ADAM_B1 * m_ref[0] + (1.0 - ADAM_B1) * g
        v2 = ADAM_B2 * v_ref[0] + (1.0 - ADAM_B2) * jnp.square(g)
        m_hat = m2 / (1.0 - ADAM_B1 ** ADAM_STEP)
        v_hat = v2 / (1.0 - ADAM_B2 ** ADAM_STEP)
        g_ref[0] = g
        d_ref[0] = -ADAM_LR * (m_hat / (jnp.sqrt(v_hat) + ADAM_EPS) + ADAM_WD * w_ref[0])
        nm_ref[0] = m2
        nv_ref[0] = v2

    blk = pl.BlockSpec((1, br, cols), lambda i: (0, i, 0))
    shape = jax.ShapeDtypeStruct((1, rows, cols), F32)
    return pl.pallas_call(
        body, name=name, grid=(rows // br,),
        in_specs=[pl.BlockSpec((N_DEV, br, cols), lambda i: (0, i, 0)), blk, blk, blk],
        out_specs=(blk, blk, blk, blk), out_shape=(shape, shape, shape, shape), compiler_params=_params(1),
    )(parts, w, m, v)


_QA, _KA, _VA, _FA, _QD, _KD, _VD = (0, 512), (512, 1024), (1024, 1536), (1536, 1544), (1544, 2056), (2056, 2568), (2568, 3080)
_MAIN_ORDER = (_QA, _QD, _KA, _KD, _VA, _VD)
MAIN_COLS = 3 * D_MODEL
PROJ_COLS = MAIN_COLS + LANES
COL_SHARDED = ("w_in", "w_gate", "w_up")


def _swap(w):
    return jnp.transpose(w, (0, 2, 1))


def _w_in_to_kernel(w_t):
    main = jnp.concatenate([w_t[a:b] for a, b in _MAIN_ORDER], axis=0)
    forget = jnp.pad(w_t[_FA[0]:_FA[1]], ((0, LANES - N_HEADS_FOX), (0, 0)))
    return main, forget


def _w_in_from_kernel(g_t):
    pos = {span: i * W_GROUP for i, span in enumerate(_MAIN_ORDER)}
    parts = []
    for span in (_QA, _KA, _VA, _FA, _QD, _KD, _VD):
        if span == _FA:
            parts.append(g_t[MAIN_COLS:MAIN_COLS + N_HEADS_FOX])
        else:
            parts.append(g_t[pos[span]:pos[span] + W_GROUP])
    return jnp.concatenate(parts, axis=0)


def _pack_small(vals):
    rows = []
    for name, _, n_rows in SMALL_LAYOUT:
        flat = vals[name].reshape(-1).astype(F32)
        rows.append(jnp.pad(flat, (0, n_rows * LANES - flat.shape[0])).reshape(n_rows, LANES))
    packed = jnp.concatenate(rows, axis=0)
    return jnp.pad(packed, ((0, SMALL_ROWS - packed.shape[0]), (0, 0)))


def _unpack_small(packed, like):
    out = {}
    for name, row, n_rows in SMALL_LAYOUT:
        n = like[name].size
        out[name] = packed[row:row + n_rows].reshape(-1)[:n].reshape(like[name].shape)
    return out


def _device_step(x, target, small, shards):
    bsz, seq, _ = x.shape
    t = bsz * seq
    xf = x.reshape(t, D_MODEL)
    tf = target.reshape(t, D_MODEL)
    row = lambda v: v.reshape(1, -1)
    g_out = jnp.concatenate([small["g_out_fox"], small["g_out_dil"]]).reshape(1, D_MODEL)
    gains = jnp.concatenate(
        [jnp.tile(small[n].reshape(1, HEAD_DIM), (1, 2)) for n in ("g_q_fox", "g_q_dil", "g_k_fox", "g_k_dil")]
        + [jnp.zeros((4, LANES), F32)], axis=0)
    b_pad = jnp.pad(small["b_forget"].reshape(1, N_HEADS_FOX), ((0, 0), (0, LANES - N_HEADS_FOX)))
    rope = _rope_tables(seq)
    tables_qk = _bias_tables(seq, keys_first=False)
    tables_kq = _bias_tables(seq, keys_first=True)

    h1, x16, g_in = _rmsnorm_fwd(xf, row(small["g_mix"]), group=D_MODEL, name="norm_mix",
                                 hosted=_ChipGather([(shards["w_in"], False)]))
    w_main_t, w_fa_t = _w_in_to_kernel(g_in.reshape(IN_COLS, D_MODEL))
    w_in_all_t = jnp.concatenate([w_main_t, w_fa_t], axis=0)
    proj, qk, fa = _qk_prep_fwd(h1, w_main_t, w_fa_t, b_pad, gains, rope, seq)
    late = _Exchange([(shards[n], False) for n in ("w_out", "w_gate", "w_up", "w_down")])
    o, lse, g_out_w, g_gate, g_up, g_down = _attn_fwd(qk, proj, tables_qk, seq, hosted=late)
    w_out = g_out_w.reshape(D_MODEL, D_MODEL)
    w_gate_t = g_gate.reshape(D_FF, D_MODEL)
    w_up_t = g_up.reshape(D_FF, D_MODEL)
    w_down = g_down.reshape(D_FF, D_MODEL)
    on, x2, h2 = _out_proj_ffn_norm(o, g_out, w_out, x16, row(small["g_ffn"]))
    a, u, f = _ffn_gate_up(h2, w_gate_t, w_up_t)
    dy16, loss_tile = _ffn_down_loss(f, w_down, x2, tf)

    da, du = _ffn_down_grad(dy16, w_down, a, u)
    gw_down = _matmul_tn(f, dy16, name="gw_down")
    gw_gate_t = _matmul_tn(da, h2, name="gw_gate")
    gw_up_t = _matmul_tn(du, h2, name="gw_up")
    dh2_gate = _matmul_rows(da, w_gate_t, name="d_ffn_gate", out_dtype=BF16)
    dx2_16, dg_ffn = _norm_input_grad([(du, w_up_t, True)], x2, row(small["g_ffn"]), group=D_MODEL,
                                      name="d_ffn_up", out_dtypes=(BF16,), resid=dy16, init=dh2_gate)
    gw_out = _matmul_tn(on, dx2_16, name="gw_out")
    do, dg_out = _norm_input_grad([(dx2_16, w_out, False)], o, g_out, group=W_GROUP, name="d_out_proj",
                                  out_dtypes=(BF16,))

    shard_rows = lambda g: g.reshape(N_DEV, g.shape[0] // N_DEV, g.shape[1])
    ffn_grads = _Exchange([(shard_rows(g), True) for g in (gw_out, gw_gate_t, gw_up_t, gw_down)])
    dq, dk, dv, dqx, dkx, p_out, p_gate, p_up, p_down = _attn_bwd(qk, proj, tables_kq, o, lse, do, seq, hosted=ffn_grads)
    dproj, dgains, db = _qk_prep_bwd(dq, dk, dqx, dkx, dv, proj, fa, b_pad, gains, rope, seq)
    gw_in_t = _matmul_tn(dproj, h1, name="gw_in")
    in_grad = _Exchange([(shard_rows(_w_in_from_kernel(gw_in_t)), True)])
    dx, dg_mix, p_in = _norm_input_grad([(dproj, w_in_all_t, True)], x16, row(small["g_mix"]), group=D_MODEL,
                                        name="d_in_proj", out_dtypes=(F32,), resid=dx2_16, hosted=in_grad)

    fold = lambda rows: jnp.sum(rows[:, :HEAD_DIM] + rows[:, HEAD_DIM:], axis=0)
    half = N_PAIRS // 2
    gsmall = {
        "g_mix": dg_mix, "g_ffn": dg_ffn, "g_out_fox": dg_out[0, :W_GROUP], "g_out_dil": dg_out[0, W_GROUP:],
        "g_q_fox": fold(dgains[0:half]), "g_q_dil": fold(dgains[half:N_PAIRS]),
        "g_k_fox": fold(dgains[N_PAIRS:N_PAIRS + half]), "g_k_dil": fold(dgains[N_PAIRS + half:]),
        "b_forget": db[0, :N_HEADS_FOX],
    }
    packed = _pack_small(gsmall).at[LOSS_ROW].set(loss_tile[0])
    (p_small,) = _exchange("small_exchange", [(packed, False)])
    parts = {"w_in": p_in, "w_out": p_out, "w_gate": p_gate, "w_up": p_up, "w_down": p_down}
    return dx.reshape(x.shape), parts, p_small


def kernel(x, g_mix, w_in, b_forget, g_q_fox, g_k_fox, g_q_dil, g_k_dil, g_out_fox, g_out_dil, w_out, g_ffn, w_gate, w_up, w_down, loss_target, m_g_mix, m_w_in, m_b_forget, m_g_q_fox, m_g_k_fox, m_g_q_dil, m_g_k_dil, m_g_out_fox, m_g_out_dil, m_w_out, m_g_ffn, m_w_gate, m_w_up, m_w_down, v_g_mix, v_w_in, v_b_forget, v_g_q_fox, v_g_k_fox, v_g_q_dil, v_g_k_dil, v_g_out_fox, v_g_out_dil, v_w_out, v_g_ffn, v_w_gate, v_w_up, v_w_down):
    args = dict(locals())
    small_names = [name for name, _, _ in SMALL_LAYOUT]
    big_names = ["w_in", "w_out", "w_gate", "w_up", "w_down"]
    small = {n: args[n][0] for n in small_names}

    as_rows = lambda n, w: _swap(w) if n in COL_SHARDED else w
    shards = {n: as_rows(n, args[n])[0].astype(BF16) for n in big_names}
    grad_x, parts, p_small = _device_step(x, loss_target, small, shards)

    grads, deltas, new_m, new_v = {}, {}, {}, {}
    for n in big_names:
        res = _adamw(parts[n], as_rows(n, args[n]), as_rows(n, args["m_" + n]), as_rows(n, args["v_" + n]),
                     name="adamw_" + n)
        grads[n], deltas[n], new_m[n], new_v[n] = [as_rows(n, r) for r in res]
    res = _adamw(p_small, _pack_small(small)[None], _pack_small({n: args["m_" + n][0] for n in small_names})[None],
                 _pack_small({n: args["v_" + n][0] for n in small_names})[None], name="adamw_small")
    loss = res[0][0, LOSS_ROW, 0]
    for dst, packed_res in zip((grads, deltas, new_m, new_v), res):
        for n, val in _unpack_small(packed_res[0], small).items():
            dst[n] = val[None]

    order = ["g_mix", "w_in", "b_forget", "g_q_fox", "g_k_fox", "g_q_dil", "g_k_dil", "g_out_fox", "g_out_dil",
             "w_out", "g_ffn", "w_gate", "w_up", "w_down"]
    return (loss, grad_x, *[grads[n] for n in order], *[deltas[n] for n in order],
            *[new_m[n] for n in order], *[new_v[n] for n in order])
```
